```python
import math
import jax
import jax.numpy as jnp
from jax import lax
import numpy as np

D_MODEL = 2048
BATCH = 4
SEQ = 4096
DEPTH = 1
DEC_BATCH = 32
DEC_SEQ = 32
PAST_LEN = 2048

CHUNK = 64
Q_BLOCK = 128
SSM_WIDTH = D_MODEL // 2
SSM_GROUP = 16
SSM_GROUPS = SSM_WIDTH // SSM_GROUP
SSM_STATE = 64
ATTN_WIDTH = D_MODEL // 2
HEAD_DIM = 128
N_HEADS = ATTN_WIDTH // HEAD_DIM
N_KV_HEADS = 2
KV_GROUP = N_HEADS // N_KV_HEADS
IDX_HEADS = 8
IDX_DIM = 64
IDX_TOPK = 256
ROPE_THETA = 500000.0
N_EXPERT_GROUPS = 4
EXPERTS_PER_GROUP = 8
N_EXPERTS = N_EXPERT_GROUPS * EXPERTS_PER_GROUP
TOP_K = 2
EXPERT_FF = D_MODEL // 2
MOE_BLOCK = 128
EPS = 1e-6
IN_SIZES = (SSM_WIDTH, N_HEADS * HEAD_DIM, N_KV_HEADS * HEAD_DIM, N_KV_HEADS * HEAD_DIM,
            IDX_HEADS * IDX_DIM, IDX_DIM, IDX_HEADS, D_MODEL, D_MODEL)
IN_COLS = sum(IN_SIZES)
IN_SPLITS = tuple(int(s) for s in np.cumsum(IN_SIZES)[:-1])

kernel_name = 'hybrid_s5_dsa_hmoe_stream_step'


def rms_norm(x, g):
    xf = x.astype(jnp.float32)
    y = xf * lax.rsqrt(jnp.mean(xf * xf, axis=-1, keepdims=True) + EPS)
    return (y * g.astype(jnp.float32)).astype(x.dtype)


def partial_rope(x, pos):
    d = x.shape[-1]
    r = d // 4
    half = r // 2
    inv = ROPE_THETA ** (-jnp.arange(half, dtype=jnp.float32) * 2.0 / r)
    ang = pos.astype(jnp.float32)[:, None] * inv[None, :]
    cos = jnp.cos(ang)[:, None, :]
    sin = jnp.sin(ang)[:, None, :]
    xf = x.astype(jnp.float32)
    x1, x2, rest = xf[..., :half], xf[..., half:r], xf[..., r:]
    return jnp.concatenate([x1 * cos - x2 * sin, x2 * cos + x1 * sin, rest], axis=-1).astype(x.dtype)


def _cplx_affine_combine(e1, e2):
    a1r, a1i, b1r, b1i = e1
    a2r, a2i, b2r, b2i = e2
    return (a2r * a1r - a2i * a1i, a2r * a1i + a2i * a1r,
            a2r * b1r - a2i * b1i + b2r, a2r * b1i + a2i * b1r + b2i)


def s5_ssm(u, A_re, A_im, log_dt, B_re, B_im, C_re, C_im, D_skip, h0_re, h0_im):
    f32 = jnp.float32
    bsz, T, _ = u.shape
    uf = u.astype(f32).reshape(bsz, T, SSM_GROUPS, SSM_GROUP)
    lam_re, lam_im = A_re.astype(f32), A_im.astype(f32)
    dt = jnp.exp(log_dt.astype(f32))[:, None]
    mag = jnp.exp(lam_re * dt)
    lb_re, lb_im = mag * jnp.cos(lam_im * dt), mag * jnp.sin(lam_im * dt)
    den = lam_re * lam_re + lam_im * lam_im
    num_re = lb_re - 1.0
    z_re = (num_re * lam_re + lb_im * lam_im) / den
    z_im = (lb_im * lam_re - num_re * lam_im) / den
    bu_re = jnp.einsum('btgh,gph->btgp', uf, B_re.astype(f32))
    bu_im = jnp.einsum('btgh,gph->btgp', uf, B_im.astype(f32))
    e_re = z_re * bu_re - z_im * bu_im
    e_im = z_re * bu_im + z_im * bu_re
    if h0_re is not None:
        h_re, h_im = h0_re.astype(f32), h0_im.astype(f32)
        e_re = e_re.at[:, 0].add(lb_re * h_re - lb_im * h_im)
        e_im = e_im.at[:, 0].add(lb_re * h_im + lb_im * h_re)
    shp = (1, T, SSM_GROUPS, SSM_STATE)
    a_re = jnp.broadcast_to(lb_re, shp)
    a_im = jnp.broadcast_to(lb_im, shp)
    _, _, s_re, s_im = lax.associative_scan(_cplx_affine_combine, (a_re, a_im, e_re, e_im), axis=1)
    y = (jnp.einsum('btgp,ghp->btgh', s_re, C_re.astype(f32))
         - jnp.einsum('btgp,ghp->btgh', s_im, C_im.astype(f32)))
    y = y.reshape(bsz, T, SSM_WIDTH) + D_skip.astype(f32) * u.astype(f32)
    return y.astype(u.dtype), s_re[:, -1], s_im[:, -1]


def dsa_attend(q, qi, wi, k, v, ki, q_pos, k_pos, n_sel):
    f32 = jnp.float32
    bsz, tq = q.shape[:2]
    allowed = (k_pos[None, :] // CHUNK) <= (q_pos[:, None] // CHUNK)
    rel = jax.nn.relu(jnp.einsum('bthd,bsd->bths', qi, ki).astype(f32) * IDX_DIM ** -0.5)
    iscore = jnp.einsum('bth,bths->bts', wi.astype(f32), rel)
    iscore = jnp.where(allowed[None], iscore, -jnp.inf)
    top_val, top_idx = lax.top_k(iscore, n_sel)
    valid = top_val > -jnp.inf
    gather = jax.vmap(lambda rows, idx: rows[idx])
    k_sel = gather(k, top_idx)
    v_sel = gather(v, top_idx)
    qg = q.reshape(bsz, tq, N_KV_HEADS, KV_GROUP, HEAD_DIM)
    logits = jnp.einsum('btkgd,btnkd->btkgn', qg, k_sel).astype(f32) * HEAD_DIM ** -0.5
    logits = jnp.where(valid[:, :, None, None, :], logits, -jnp.inf)
    probs = jax.nn.softmax(logits, axis=-1).astype(v.dtype)
    out = jnp.einsum('btkgn,btnkd->btkgd', probs, v_sel)
    return out.reshape(bsz, tq, ATTN_WIDTH)


def dsa_prompt(q, qi, wi, k, v, ki, pos, n_sel):
    bsz, T = q.shape[:2]
    nb = T // Q_BLOCK

    def to_blocks(a):
        return jnp.swapaxes(a.reshape(bsz, nb, Q_BLOCK, *a.shape[2:]), 0, 1)

    def one(args):
        qb, qib, wib, pb = args
        return dsa_attend(qb, qib, wib, k, v, ki, pb, pos, n_sel)

    out = lax.map(one, (to_blocks(q), to_blocks(qi), to_blocks(wi), pos.reshape(nb, Q_BLOCK)))
    return jnp.swapaxes(out, 0, 1).reshape(bsz, T, ATTN_WIDTH)


def hier_moe(x, w_rg, b_rg, w_re, b_re, w_gate, w_up, w_down):
    f32 = jnp.float32
    bsz, T, D = x.shape
    n_tok = bsz * T
    xt = x.reshape(n_tok, D)
    g_logits = (xt @ w_rg).astype(f32) + b_rg.astype(f32)
    g_sel = jnp.argmax(g_logits, axis=-1).astype(jnp.int32)
    g_w = jnp.take_along_axis(jax.nn.softmax(g_logits, axis=-1), g_sel[:, None], axis=-1)
    e_logits = ((xt @ w_re).astype(f32) + b_re.astype(f32)).reshape(n_tok, N_EXPERT_GROUPS, EXPERTS_PER_GROUP)
    e_in = jnp.take_along_axis(e_logits, g_sel[:, None, None], axis=1)[:, 0]
    top_v, top_i = lax.top_k(e_in, TOP_K)
    w_tok = jax.nn.softmax(top_v, axis=-1) * g_w
    eid = g_sel[:, None] * EXPERTS_PER_GROUP + top_i.astype(jnp.int32)
    n_assign = n_tok * TOP_K
    e_flat = eid.reshape(-1)
    w_flat = w_tok.reshape(-1)
    tok = jnp.repeat(jnp.arange(n_tok, dtype=jnp.int32), TOP_K)
    counts = jnp.bincount(e_flat, length=N_EXPERTS)
    padded = (counts + MOE_BLOCK - 1) // MOE_BLOCK * MOE_BLOCK
    pad_end = jnp.cumsum(padded)
    pad_start = pad_end - padded
    start = jnp.cumsum(counts) - counts
    order = jnp.argsort(e_flat)
    se = e_flat[order]
    dest = pad_start[se] + jnp.arange(n_assign, dtype=jnp.int32) - start[se]
    rows = -(-(n_assign + N_EXPERTS * (MOE_BLOCK - 1)) // MOE_BLOCK) * MOE_BLOCK
    n_blocks = rows // MOE_BLOCK
    row_tok = jnp.full((rows,), n_tok, jnp.int32).at[dest].set(tok[order])
    row_w = jnp.zeros((rows,), f32).at[dest].set(w_flat[order])
    blk_e = jnp.minimum(jnp.searchsorted(pad_end, jnp.arange(n_blocks, dtype=jnp.int32) * MOE_BLOCK, side='right'),
                        N_EXPERTS - 1)
    x_pad = jnp.concatenate([xt, jnp.zeros((1, D), xt.dtype)], axis=0)
    xs = x_pad[row_tok].reshape(n_blocks, MOE_BLOCK, D)

    def expert_block(args):
        xb, e = args
        h = jax.nn.silu(xb @ w_gate[e]) * (xb @ w_up[e])
        return h @ w_down[e]

    ys = lax.map(expert_block, (xs, blk_e)).reshape(rows, D)
    out = jax.ops.segment_sum(ys * row_w[:, None].astype(ys.dtype), row_tok, num_segments=n_tok + 1)[:n_tok]
    return out.reshape(bsz, T, D)


def hybrid_layer(x, pos, past, p, n_sel):
    bsz, T, _ = x.shape
    xn = rms_norm(x, p['norm_mix_g'])
    u, q, k, v, qi, ki, wi, gate_a, gate_b = jnp.split(xn @ p['w_in'], IN_SPLITS, axis=-1)
    q = partial_rope(rms_norm(q.reshape(bsz, T, N_HEADS, HEAD_DIM), p['q_norm_g']), pos)
    k = partial_rope(rms_norm(k.reshape(bsz, T, N_KV_HEADS, HEAD_DIM), p['k_norm_g']), pos)
    v = v.reshape(bsz, T, N_KV_HEADS, HEAD_DIM)
    qi = partial_rope(qi.reshape(bsz, T, IDX_HEADS, IDX_DIM), pos)
    ki = partial_rope(rms_norm(ki.reshape(bsz, T, 1, IDX_DIM), p['idx_k_norm_g']), pos)[:, :, 0]
    wi = wi * IDX_HEADS ** -0.5
    if past is None:
        h0_re = None
        h0_im = None
        attn = dsa_prompt(q, qi, wi, k, v, ki, pos, n_sel)
    else:
        cache_k, cache_v, cache_ki, h0_re, h0_im = past
        k_all = jnp.concatenate([cache_k, k], axis=1)
        v_all = jnp.concatenate([cache_v, v], axis=1)
        ki_all = jnp.concatenate([cache_ki, ki], axis=1)
        k_pos = jnp.arange(k_all.shape[1], dtype=jnp.int32)
        attn = dsa_attend(q, qi, wi, k_all, v_all, ki_all, pos, k_pos, n_sel)
    y_ssm, s_re, s_im = s5_ssm(u, p['ssm_A_re'], p['ssm_A_im'], p['ssm_log_dt'], p['ssm_B_re'], p['ssm_B_im'],
                               p['ssm_C_re'], p['ssm_C_im'], p['ssm_D'], h0_re, h0_im)
    g = jax.nn.gelu(y_ssm)
    branch_a = (g @ p['w_glu_val']) * jax.nn.sigmoid(g @ p['w_glu_gate'])
    branch_b = attn @ p['w_attn_branch']
    merged = jax.nn.sigmoid(gate_a) * branch_a + jax.nn.sigmoid(gate_b) * branch_b
    h = x + merged @ p['w_out']
    y = h + hier_moe(rms_norm(h, p['norm_ffn_g']), p['w_router_group'], p['b_router_group'],
                     p['w_router_expert'], p['b_router_expert'], p['w_exp_gate'], p['w_exp_up'], p['w_exp_down'])
    return (y, k, v, ki, s_re, s_im)


def setup_inputs(seed: int = 0) -> dict:
    key = jax.random.key(seed)
    ks = iter(jax.random.split(key, 40))
    f32 = jnp.float32
    L = DEPTH

    def nrm(shape, scale):
        return scale * jax.random.normal(next(ks), shape, f32)

    def gain(n):
        return 1.0 + nrm((L, n), 0.01)

    n_idx = jnp.arange(SSM_STATE, dtype=f32)
    return {
        'x_prompt': nrm((BATCH, SEQ, D_MODEL), 1.0),
        'x_sample': nrm((DEC_BATCH, DEC_SEQ, D_MODEL), 1.0),
        'cache_k': nrm((L, DEC_BATCH, PAST_LEN, N_KV_HEADS, HEAD_DIM), 1.0),
        'cache_v': nrm((L, DEC_BATCH, PAST_LEN, N_KV_HEADS, HEAD_DIM), 1.0),
        'cache_idx_k': nrm((L, DEC_BATCH, PAST_LEN, IDX_DIM), 1.0),
        'state_ssm_re': nrm((L, DEC_BATCH, SSM_GROUPS, SSM_STATE), 0.1),
        'state_ssm_im': nrm((L, DEC_BATCH, SSM_GROUPS, SSM_STATE), 0.1),
        'norm_mix_g': gain(D_MODEL),
        'w_in': nrm((L, D_MODEL, IN_COLS), D_MODEL ** -0.5),
        'q_norm_g': gain(HEAD_DIM),
        'k_norm_g': gain(HEAD_DIM),
        'idx_k_norm_g': gain(IDX_DIM),
        'ssm_A_re': -0.5 + nrm((L, SSM_GROUPS, SSM_STATE), 0.01),
        'ssm_A_im': jnp.pi * n_idx + nrm((L, SSM_GROUPS, SSM_STATE), 0.01),
        'ssm_log_dt': jax.random.uniform(next(ks), (L, SSM_GROUPS), f32, minval=math.log(1e-3), maxval=math.log(1e-1)),
        'ssm_B_re': nrm((L, SSM_GROUPS, SSM_STATE, SSM_GROUP), (2 * SSM_GROUP) ** -0.5),
        'ssm_B_im': nrm((L, SSM_GROUPS, SSM_STATE, SSM_GROUP), (2 * SSM_GROUP) ** -0.5),
        'ssm_C_re': nrm((L, SSM_GROUPS, SSM_GROUP, SSM_STATE), SSM_STATE ** -0.5),
        'ssm_C_im': nrm((L, SSM_GROUPS, SSM_GROUP, SSM_STATE), SSM_STATE ** -0.5),
        'ssm_D': nrm((L, SSM_WIDTH), 1.0),
        'w_glu_val': nrm((L, SSM_WIDTH, D_MODEL), SSM_WIDTH ** -0.5),
        'w_glu_gate': nrm((L, SSM_WIDTH, D_MODEL), SSM_WIDTH ** -0.5),
        'w_attn_branch': nrm((L, ATTN_WIDTH, D_MODEL), ATTN_WIDTH ** -0.5),
        'w_out': nrm((L, D_MODEL, D_MODEL), D_MODEL ** -0.5),
        'norm_ffn_g': gain(D_MODEL),
        'w_router_group': nrm((L, D_MODEL, N_EXPERT_GROUPS), D_MODEL ** -0.5),
        'b_router_group': nrm((L, N_EXPERT_GROUPS), 0.01),
        'w_router_expert': nrm((L, D_MODEL, N_EXPERTS), D_MODEL ** -0.5),
        'b_router_expert': nrm((L, N_EXPERTS), 0.01),
        'w_exp_gate': nrm((L, N_EXPERTS, D_MODEL, EXPERT_FF), D_MODEL ** -0.5),
        'w_exp_up': nrm((L, N_EXPERTS, D_MODEL, EXPERT_FF), D_MODEL ** -0.5),
        'w_exp_down': nrm((L, N_EXPERTS, EXPERT_FF, D_MODEL), EXPERT_FF ** -0.5),
    }


def reference(x_prompt, x_sample, cache_k, cache_v, cache_idx_k, state_ssm_re, state_ssm_im,
              norm_mix_g, w_in, q_norm_g, k_norm_g, idx_k_norm_g, ssm_A_re, ssm_A_im, ssm_log_dt,
              ssm_B_re, ssm_B_im, ssm_C_re, ssm_C_im, ssm_D, w_glu_val, w_glu_gate, w_attn_branch, w_out,
              norm_ffn_g, w_router_group, b_router_group, w_router_expert, b_router_expert,
              w_exp_gate, w_exp_up, w_exp_down):
    seq = x_prompt.shape[1]
    past_len = cache_k.shape[2]
    t_new = x_sample.shape[1]
    pos_p = jnp.arange(seq, dtype=jnp.int32)
    pos_s = past_len + jnp.arange(t_new, dtype=jnp.int32)
    n_sel_p = min(IDX_TOPK, seq // 4)
    n_sel_s = min(IDX_TOPK, (past_len + t_new) // 4)
    y_p, y_s = x_prompt, x_sample
    st_p, st_s = [], []
    for l in range(DEPTH):
        p = {
            'norm_mix_g': norm_mix_g[l], 'w_in': w_in[l], 'q_norm_g': q_norm_g[l], 'k_norm_g': k_norm_g[l],
            'idx_k_norm_g': idx_k_norm_g[l], 'ssm_A_re': ssm_A_re[l], 'ssm_A_im': ssm_A_im[l],
            'ssm_log_dt': ssm_log_dt[l], 'ssm_B_re': ssm_B_re[l], 'ssm_B_im': ssm_B_im[l],
            'ssm_C_re': ssm_C_re[l], 'ssm_C_im': ssm_C_im[l], 'ssm_D': ssm_D[l],
            'w_glu_val': w_glu_val[l], 'w_glu_gate': w_glu_gate[l], 'w_attn_branch': w_attn_branch[l],
            'w_out': w_out[l], 'norm_ffn_g': norm_ffn_g[l], 'w_router_group': w_router_group[l],
            'b_router_group': b_router_group[l], 'w_router_expert': w_router_expert[l],
            'b_router_expert': b_router_expert[l], 'w_exp_gate': w_exp_gate[l], 'w_exp_up': w_exp_up[l],
            'w_exp_down': w_exp_down[l],
        }
        y_p, *new_p = hybrid_layer(y_p, pos_p, None, p, n_sel_p)
        past = (cache_k[l], cache_v[l], cache_idx_k[l], state_ssm_re[l], state_ssm_im[l])
        y_s, *new_s = hybrid_layer(y_s, pos_s, past, p, n_sel_s)
        st_p.append(new_p)
        st_s.append(new_s)
    k_p, v_p, ki_p, sre_p, sim_p = [jnp.stack(z) for z in zip(*st_p)]
    k_s, v_s, ki_s, sre_s, sim_s = [jnp.stack(z) for z in zip(*st_s)]
    return (y_p, y_s, k_p, v_p, ki_p, sre_p, sim_p, k_s, v_s, ki_s, sre_s, sim_s)
```

```python
import functools

import numpy as np
import jax
import jax.numpy as jnp
from jax import lax
from jax.experimental import pallas as pl
from jax.experimental.pallas import tpu as pltpu

F32 = jnp.float32
BF16 = jnp.bfloat16
I32 = jnp.int32

D_MODEL = 2048
CHUNK = 64
SSM_WIDTH = 1024
SSM_GROUP = 16
SSM_GROUPS = 64
SSM_STATE = 64
ATTN_WIDTH = 1024
HEAD_DIM = 128
N_HEADS = 8
N_KV_HEADS = 2
KV_GROUP = 4
IDX_HEADS = 8
IDX_DIM = 64
IDX_TOPK = 256
ROPE_THETA = 500000.0
N_EXPERT_GROUPS = 4
EXPERTS_PER_GROUP = 8
N_EXPERTS = 32
TOP_K = 2
EXPERT_FF = 1024
EPS = 1e-6

LANES = 128
SUBLANES = 8
VMEM_LIMIT = 56 * 1024 * 1024

COL_U, COL_Q, COL_GA, COL_GB, COL_K, COL_V, COL_QI, COL_KIWI = 0, 1024, 2048, 4096, 6144, 6400, 6656, 7168
PROJ_COLS = 7296
PROJ_TN = 2432

SSM_LB = SSM_WIDTH // LANES
SSM_SB = 8 * SSM_STATE

INT_MIN = np.int32(-2 ** 31)
KEY_NEG_INF = np.int32(np.array([0xFF800000], np.uint32).view(np.int32)[0] ^ 0x7FFFFFFF)


def _params(sem, vmem=VMEM_LIMIT):
    return pltpu.CompilerParams(dimension_semantics=sem, vmem_limit_bytes=vmem)


def _dot(a, b):
    return jnp.dot(a, b, preferred_element_type=F32)


def _dot_nt(a, b):
    return lax.dot_general(a, b, (((1,), (1,)), ((), ())), preferred_element_type=F32)


def _split_bf16(x):
    hi = x.astype(BF16)
    lo = (x - hi.astype(F32)).astype(BF16)
    return hi, lo


def _in_proj_kernel(x_ref, g_ref, w_ref, o_ref, xn_ref):
    @pl.when(pl.program_id(1) == 0)
    def _():
        x = x_ref[...]
        ms = jnp.mean(x * x, axis=-1, keepdims=True)
        xn_ref[...] = (x * lax.rsqrt(ms + EPS) * g_ref[...]).astype(BF16)

    o_ref[...] = _dot(xn_ref[...], w_ref[...])


def in_proj(x, gain, w_bf16, *, tm=512):
    n_tok = x.shape[0]
    return pl.pallas_call(
        _in_proj_kernel,
        grid=(n_tok // tm, PROJ_COLS // PROJ_TN),
        in_specs=[pl.BlockSpec((tm, D_MODEL), lambda i, j: (i, 0)),
                  pl.BlockSpec((1, D_MODEL), lambda i, j: (0, 0)),
                  pl.BlockSpec((D_MODEL, PROJ_TN), lambda i, j: (0, j))],
        out_specs=pl.BlockSpec((tm, PROJ_TN), lambda i, j: (i, j)),
        out_shape=jax.ShapeDtypeStruct((n_tok, PROJ_COLS), F32),
        scratch_shapes=[pltpu.VMEM((tm, D_MODEL), BF16)],
        compiler_params=_params(("arbitrary", "arbitrary")),
        name="in_proj",
    )(x, gain, w_bf16)


def _rope(x, c, s_lo, s_hi, half):
    n = x.shape[-1]
    return x * c + pltpu.roll(x, n - half, 1) * s_lo + pltpu.roll(x, half, 1) * s_hi


def _head_norm(x, g):
    ms = jnp.mean(x * x, axis=-1, keepdims=True)
    return x * lax.rsqrt(ms + EPS) * g


def _qk_post_kernel(q_ref, k_ref, v_ref, qi_ref, kw_ref, c128_ref, sl128_ref, sh128_ref,
                    c64_ref, sl64_ref, sh64_ref, qg_ref, kg_ref, ig_ref,
                    qo_ref, kf_ref, kb_ref, vf_ref, vb_ref, qio_ref, kif_ref, kib_ref, wo_ref):
    c128, sl128, sh128 = c128_ref[...], sl128_ref[...], sh128_ref[...]
    c64, sl64, sh64 = c64_ref[...], sl64_ref[...], sh64_ref[...]
    half128 = HEAD_DIM // 8
    half64 = IDX_DIM // 8
    for h in range(N_HEADS):
        sl = slice(h * LANES, (h + 1) * LANES)
        qo_ref[:, sl] = _rope(_head_norm(q_ref[:, sl], qg_ref[...]), c128, sl128, sh128, half128).astype(BF16)
    for h in range(N_KV_HEADS):
        sl = slice(h * LANES, (h + 1) * LANES)
        kk = _rope(_head_norm(k_ref[:, sl], kg_ref[...]), c128, sl128, sh128, half128)
        kf_ref[:, sl] = kk
        kb_ref[:, sl] = kk.astype(BF16)
    v = v_ref[...]
    vf_ref[...] = v
    vb_ref[...] = v.astype(BF16)
    lane = lax.broadcasted_iota(I32, c64.shape, 1)
    low = lane < IDX_DIM
    for p in range(IDX_HEADS // 2):
        x = _rope(qi_ref[:, p * LANES:(p + 1) * LANES], c64, sl64, sh64, half64)
        qio_ref[:, (2 * p) * LANES:(2 * p + 1) * LANES] = jnp.where(low, x, 0.0).astype(BF16)
        qio_ref[:, (2 * p + 1) * LANES:(2 * p + 2) * LANES] = jnp.where(low, pltpu.roll(x, IDX_DIM, 1), 0.0).astype(BF16)
    kw = kw_ref[...]
    ms = jnp.sum(jnp.where(low, kw * kw, 0.0), axis=-1, keepdims=True) * (1.0 / IDX_DIM)
    ki = _rope(kw * lax.rsqrt(ms + EPS) * ig_ref[...], c64, sl64, sh64, half64)
    kif_ref[...] = ki[:, :IDX_DIM]
    kib_ref[...] = jnp.where(low, ki, 0.0).astype(BF16)
    wo_ref[...] = (pltpu.roll(kw, IDX_DIM, 1) * IDX_HEADS ** -0.5) * IDX_DIM ** -0.5


def _rope_tables(pos, head_dim):
    r = head_dim // 4
    half = r // 2
    inv = ROPE_THETA ** (-jnp.arange(half, dtype=F32) * 2.0 / r)
    ang = pos.astype(F32)[:, None] * inv[None, :]
    cos, sin = jnp.cos(ang), jnp.sin(ang)
    n = pos.shape[0]
    zh = jnp.zeros((n, half), F32)
    rest = head_dim - r
    c = jnp.concatenate([cos, cos, jnp.ones((n, rest), F32)], axis=-1)
    s_lo = jnp.concatenate([-sin, zh, jnp.zeros((n, rest), F32)], axis=-1)
    s_hi = jnp.concatenate([zh, sin, jnp.zeros((n, rest), F32)], axis=-1)
    rep = LANES // head_dim
    return tuple(jnp.tile(t, (1, rep)) for t in (c, s_lo, s_hi))


def qk_post(proj, pos, q_gain, k_gain, ik_gain, *, tm=512):
    n_tok = proj.shape[0]
    t128 = _rope_tables(pos, HEAD_DIM)
    t64 = _rope_tables(pos, IDX_DIM)
    ik_gain128 = jnp.concatenate([ik_gain, jnp.zeros((LANES - IDX_DIM,), F32)])[None, :]

    def col(width, start):
        return pl.BlockSpec((tm, width), lambda i: (i, start // width))

    def row(width):
        return pl.BlockSpec((tm, width), lambda i: (i, 0))

    gain = pl.BlockSpec((1, LANES), lambda i: (0, 0))
    kvw = N_KV_HEADS * HEAD_DIM
    return pl.pallas_call(
        _qk_post_kernel,
        grid=(n_tok // tm,),
        in_specs=[col(ATTN_WIDTH, COL_Q), col(kvw, COL_K), col(kvw, COL_V), col(IDX_HEADS * IDX_DIM, COL_QI),
                  col(LANES, COL_KIWI)] + [row(LANES)] * 6 + [gain] * 3,
        out_specs=[row(ATTN_WIDTH), row(kvw), row(kvw), row(kvw), row(kvw), row(IDX_HEADS * LANES),
                   row(IDX_DIM), row(LANES), row(LANES)],
        out_shape=[jax.ShapeDtypeStruct((n_tok, ATTN_WIDTH), BF16),
                   jax.ShapeDtypeStruct((n_tok, kvw), F32), jax.ShapeDtypeStruct((n_tok, kvw), BF16),
                   jax.ShapeDtypeStruct((n_tok, kvw), F32), jax.ShapeDtypeStruct((n_tok, kvw), BF16),
                   jax.ShapeDtypeStruct((n_tok, IDX_HEADS * LANES), BF16),
                   jax.ShapeDtypeStruct((n_tok, IDX_DIM), F32), jax.ShapeDtypeStruct((n_tok, LANES), BF16),
                   jax.ShapeDtypeStruct((n_tok, LANES), F32)],
        compiler_params=_params(("arbitrary",)),
        name="qk_post",
    )(proj, proj, proj, proj, proj, *t128, *t64, q_gain[None, :], k_gain[None, :], ik_gain128)


def _gelu_tanh(x):
    return 0.5 * x * (1.0 + jnp.tanh(np.float32(np.sqrt(2.0 / np.pi)) * (x + 0.044715 * (x * x * x))))


def _ssm_kernel(u_ref, wbh_ref, wbl_ref, wc_ref, cst_ref, d_ref, h0_ref, g_ref, sre_ref, sim_ref,
                er_ref, ei_ref, car_ref, *, tc):
    c = pl.program_id(2)

    @pl.when(c == 0)
    def _():
        car_ref[...] = h0_ref[...]

    u = u_ref[...]
    uh, ul = _split_bf16(u)
    wh = wbh_ref[...]
    e = _dot(uh, wh) + _dot(ul, wh) + _dot(uh, wbl_ref[...])
    er_ref[...] = e[:, :SSM_SB]
    ei_ref[...] = e[:, SSM_SB:]

    def body(r, carry):
        cr, ci = carry
        i0 = pl.multiple_of(r * SUBLANES, SUBLANES)
        xr = er_ref[pl.ds(i0, SUBLANES), :]
        xi = ei_ref[pl.ds(i0, SUBLANES), :]
        for n, k in enumerate((1, 2, 4)):
            ar, ai = cst_ref[2 * n], cst_ref[2 * n + 1]
            sr, si = pltpu.roll(xr, k, 0), pltpu.roll(xi, k, 0)
            xr, xi = xr + ar * sr - ai * si, xi + ar * si + ai * sr
        pr, pi_ = cst_ref[6], cst_ref[7]
        xr, xi = xr + pr * cr - pi_ * ci, xi + pr * ci + pi_ * cr
        er_ref[pl.ds(i0, SUBLANES), :] = xr
        ei_ref[pl.ds(i0, SUBLANES), :] = xi
        return xr[SUBLANES - 1:SUBLANES, :], xi[SUBLANES - 1:SUBLANES, :]

    cr, ci = lax.fori_loop(0, tc // SUBLANES, body, (car_ref[0:1, :], car_ref[1:2, :]))
    car_ref[0:1, :] = cr
    car_ref[1:2, :] = ci

    y = _dot(er_ref[...].astype(BF16), wc_ref[0]) - _dot(ei_ref[...].astype(BF16), wc_ref[1])
    y = y + d_ref[...] * u
    g_ref[...] = _gelu_tanh(y).astype(BF16)

    @pl.when(c == pl.num_programs(2) - 1)
    def _():
        sre_ref[...] = cr
        sim_ref[...] = ci


def _ssm_weights(a_re, a_im, log_dt, b_re, b_im, c_re, c_im):
    lam_re, lam_im = a_re, a_im
    dt = jnp.exp(log_dt)[:, None]
    mag = jnp.exp(lam_re * dt)
    lb_re, lb_im = mag * jnp.cos(lam_im * dt), mag * jnp.sin(lam_im * dt)
    den = lam_re * lam_re + lam_im * lam_im
    num_re = lb_re - 1.0
    z_re = (num_re * lam_re + lb_im * lam_im) / den
    z_im = (lb_im * lam_re - num_re * lam_im) / den
    zb_re = z_re[:, :, None] * b_re - z_im[:, :, None] * b_im
    zb_im = z_re[:, :, None] * b_im + z_im[:, :, None] * b_re
    eye = jnp.eye(8, dtype=F32)

    def blockdiag_in(w):
        return jnp.einsum('jgph,gk->jghkp', w.reshape(SSM_LB, 8, SSM_STATE, SSM_GROUP), eye).reshape(SSM_LB, LANES, SSM_SB)

    def blockdiag_out(w):
        return jnp.einsum('jghp,gk->jkpgh', w.reshape(SSM_LB, 8, SSM_GROUP, SSM_STATE), eye).reshape(SSM_LB, SSM_SB, LANES)

    wb = jnp.concatenate([blockdiag_in(zb_re), blockdiag_in(zb_im)], axis=-1)
    wb_hi = wb.astype(BF16)
    wb_lo = (wb - wb_hi.astype(F32)).astype(BF16)
    wc = jnp.stack([blockdiag_out(c_re), blockdiag_out(c_im)], axis=1).astype(BF16)

    pw = [(lb_re, lb_im)]
    for _ in range(7):
        pr, pi_ = pw[-1]
        pw.append((pr * lb_re - pi_ * lb_im, pr * lb_im + pi_ * lb_re))
    rows = jnp.arange(SUBLANES)[:, None]

    def lane(x):
        return x.reshape(SSM_LB, 1, SSM_SB)

    cst = []
    for k in (1, 2, 4):
        for part in pw[k - 1]:
            cst.append(jnp.where(rows >= k, lane(part), 0.0))
    cst.append(jnp.concatenate([lane(pw[r][0]) for r in range(SUBLANES)], axis=1))
    cst.append(jnp.concatenate([lane(pw[r][1]) for r in range(SUBLANES)], axis=1))
    cst = jnp.stack(cst, axis=1)
    return wb_hi, wb_lo, wc, cst


def ssm(proj, ssm_w, d_skip, h0, *, n_batch, seq, tc, row0):
    wb_hi, wb_lo, wc, cst = ssm_w
    n_chunks = seq // tc
    blk0 = row0 // tc
    n_tok = n_batch * seq
    state_shape = jax.ShapeDtypeStruct((n_batch, SSM_LB, 1, SSM_SB), F32)
    state_spec = pl.BlockSpec((None, None, 1, SSM_SB), lambda b, j, c: (b, j, 0, 0))
    g, s_re, s_im = pl.pallas_call(
        functools.partial(_ssm_kernel, tc=tc),
        grid=(n_batch, SSM_LB, n_chunks),
        in_specs=[pl.BlockSpec((tc, LANES), lambda b, j, c: (blk0 + b * n_chunks + c, j)),
                  pl.BlockSpec((None, LANES, 2 * SSM_SB), lambda b, j, c: (j, 0, 0)),
                  pl.BlockSpec((None, LANES, 2 * SSM_SB), lambda b, j, c: (j, 0, 0)),
                  pl.BlockSpec((None, 2, SSM_SB, LANES), lambda b, j, c: (j, 0, 0, 0)),
                  pl.BlockSpec((None, 8, SUBLANES, SSM_SB), lambda b, j, c: (j, 0, 0, 0)),
                  pl.BlockSpec((1, LANES), lambda b, j, c: (0, j)),
                  pl.BlockSpec((None, None, 2, SSM_SB), lambda b, j, c: (b, j, 0, 0))],
        out_specs=[pl.BlockSpec((tc, LANES), lambda b, j, c: (b * n_chunks + c, j)), state_spec, state_spec],
        out_shape=[jax.ShapeDtypeStruct((n_tok, SSM_WIDTH), BF16), state_shape, state_shape],
        scratch_shapes=[pltpu.VMEM((tc, SSM_SB), F32), pltpu.VMEM((tc, SSM_SB), F32), pltpu.VMEM((2, SSM_SB), F32)],
        compiler_params=_params(("arbitrary", "arbitrary", "arbitrary")),
        name="ssm",
    )(proj, wb_hi, wb_lo, wc, cst, d_skip[None, :], h0)
    return g, s_re.reshape(n_batch, SSM_GROUPS, SSM_STATE), s_im.reshape(n_batch, SSM_GROUPS, SSM_STATE)


def _row_sum(x):
    return jnp.sum(x, axis=1, keepdims=True)


def _row_count(mask):
    return _row_sum(jnp.where(mask, 1, 0))


def _dsa_kernel(q_ref, qi_ref, wi_ref, k_ref, v_ref, ki_ref, o_ref, key_ref, bias_ref,
                *, bq, n_keys, q_pos0, s_valid, n_sel):
    qb = pl.program_id(1)
    col = lax.broadcasted_iota(I32, (bq, n_keys), 1)
    qpos = q_pos0 + qb * bq + lax.broadcasted_iota(I32, (bq, 1), 0)
    allowed = col < jnp.minimum((qpos // CHUNK + 1) * CHUNK, s_valid)

    ki = ki_ref[...]
    score = None
    for h in range(IDX_HEADS):
        d = _dot_nt(qi_ref[:, h * LANES:(h + 1) * LANES], ki)
        t = jnp.maximum(d, 0.0) * wi_ref[:, h:h + 1]
        score = t if score is None else score + t
    score = jnp.where(score == 0.0, 0.0, score)
    bits = pltpu.bitcast(score, I32)
    key = jnp.where(bits < 0, bits ^ np.int32(0x7FFFFFFF), bits)
    key_ref[...] = jnp.where(allowed, key, KEY_NEG_INF)

    def bisect(i, base):
        cand = base + lax.shift_left(np.int32(1), np.int32(31) - i)
        cnt = _row_count(key_ref[...] >= cand)
        return jnp.where(cnt >= n_sel, cand, base)

    thr = lax.fori_loop(0, 32, bisect, jnp.full((bq, 1), INT_MIN, I32))

    key = key_ref[...]
    need = n_sel - _row_count(key > thr)
    n_eq = _row_count(key == thr)

    def tie_cut():
        def step(i, j0):
            cand = j0 + lax.shift_left(np.int32(1), np.int32(n_bits - 1) - i)
            cnt = _row_sum(jnp.where(key_ref[...] == thr, jnp.where(col < cand, 1, 0), 0))
            return jnp.where(cnt < need, cand, j0)
        return lax.fori_loop(0, n_bits, step, jnp.zeros((bq, 1), I32))

    n_bits = int(n_keys - 1).bit_length()
    split = jnp.max(jnp.where(n_eq > need, 1, 0)) > 0
    j_last = lax.cond(split, tie_cut, lambda: jnp.full((bq, 1), n_keys, I32))
    tie_bias = jnp.where(thr == KEY_NEG_INF, -jnp.inf, 0.0)
    bias_ref[...] = jnp.where(key > thr, 0.0,
                              jnp.where(key == thr, jnp.where(col <= j_last, tie_bias, -jnp.inf), -jnp.inf))

    scale = HEAD_DIM ** -0.5
    for kv in range(N_KV_HEADS):
        kk = k_ref[:, kv * HEAD_DIM:(kv + 1) * HEAD_DIM]
        vv = v_ref[:, kv * HEAD_DIM:(kv + 1) * HEAD_DIM]
        for g in range(KV_GROUP):
            sl = slice((kv * KV_GROUP + g) * HEAD_DIM, (kv * KV_GROUP + g + 1) * HEAD_DIM)
            logits = _dot_nt(q_ref[:, sl], kk) * scale + bias_ref[...]
            m = jnp.max(logits, axis=1, keepdims=True)
            p = jnp.exp(logits - m)
            den = jnp.sum(p, axis=1, keepdims=True)
            o_ref[:, sl] = (_dot(p.astype(BF16), vv) / den).astype(BF16)


def dsa(q, qi, wi, k, v, ki, *, bq, q_blk0, n_qblk, n_keys, q_pos0, s_valid, n_sel):
    n_batch = q.shape[0]

    def qspec(width):
        return pl.BlockSpec((None, bq, width), lambda b, i: (b, q_blk0 + i, 0))

    def kspec(width):
        return pl.BlockSpec((None, n_keys, width), lambda b, i: (b, 0, 0))

    kvw = N_KV_HEADS * HEAD_DIM
    return pl.pallas_call(
        functools.partial(_dsa_kernel, bq=bq, n_keys=n_keys, q_pos0=q_pos0 + q_blk0 * bq, s_valid=s_valid, n_sel=n_sel),
        grid=(n_batch, n_qblk),
        in_specs=[qspec(ATTN_WIDTH), qspec(IDX_HEADS * LANES), qspec(LANES), kspec(kvw), kspec(kvw), kspec(LANES)],
        out_specs=pl.BlockSpec((None, bq, ATTN_WIDTH), lambda b, i: (b, i, 0)),
        out_shape=jax.ShapeDtypeStruct((n_batch, n_qblk * bq, ATTN_WIDTH), BF16),
        scratch_shapes=[pltpu.VMEM((bq, n_keys), I32), pltpu.VMEM((bq, n_keys), F32)],
        compiler_params=_params(("arbitrary", "arbitrary")),
        name="dsa",
    )(q, qi, wi, k, v, ki)


def _merge_kernel(g_ref, a_ref, ga_ref, gb_ref, wv_ref, wg_ref, wb_ref, o_ref):
    g = g_ref[...]
    branch_a = _dot(g, wv_ref[...]) * jax.nn.sigmoid(_dot(g, wg_ref[...]))
    branch_b = _dot(a_ref[...], wb_ref[...])
    merged = jax.nn.sigmoid(ga_ref[...]) * branch_a + jax.nn.sigmoid(gb_ref[...]) * branch_b
    o_ref[...] = merged.astype(BF16)


def merge(g, attn, proj, w_val, w_gate, w_branch, *, tm=1024, tn=512):
    n_tok = g.shape[0]
    nj = D_MODEL // tn

    def wspec():
        return pl.BlockSpec((SSM_WIDTH, tn), lambda i, j: (0, j))

    return pl.pallas_call(
        _merge_kernel,
        grid=(n_tok // tm, nj),
        in_specs=[pl.BlockSpec((tm, SSM_WIDTH), lambda i, j: (i, 0)),
                  pl.BlockSpec((tm, ATTN_WIDTH), lambda i, j: (i, 0)),
                  pl.BlockSpec((tm, tn), lambda i, j: (i, COL_GA // tn + j)),
                  pl.BlockSpec((tm, tn), lambda i, j: (i, COL_GB // tn + j)),
                  wspec(), wspec(), wspec()],
        out_specs=pl.BlockSpec((tm, tn), lambda i, j: (i, j)),
        out_shape=jax.ShapeDtypeStruct((n_tok, D_MODEL), BF16),
        compiler_params=_params(("arbitrary", "arbitrary")),
        name="merge",
    )(g, attn, proj, proj, w_val, w_gate, w_branch)


ROUTER_COLS = N_EXPERT_GROUPS + N_EXPERTS


def _out_proj_kernel(x_ref, m_ref, wo_ref, gn_ref, wrh_ref, wrl_ref, br_ref, h_ref, hn_ref, lg_ref):
    h = x_ref[...] + _dot(m_ref[...], wo_ref[...])
    h_ref[...] = h
    ms = jnp.mean(h * h, axis=-1, keepdims=True)
    hn = h * lax.rsqrt(ms + EPS) * gn_ref[...]
    hn_ref[...] = hn
    hh, hl = _split_bf16(hn)
    wrh = wrh_ref[...]
    lg_ref[...] = _dot(hh, wrh) + _dot(hl, wrh) + _dot(hh, wrl_ref[...]) + br_ref[...]


def out_proj(x, merged, w_out, ffn_gain, w_router, b_router, *, tm=256):
    n_tok = x.shape[0]
    wr = jnp.concatenate([w_router, jnp.zeros((D_MODEL, LANES - ROUTER_COLS), F32)], axis=1)
    wr_hi = wr.astype(BF16)
    wr_lo = (wr - wr_hi.astype(F32)).astype(BF16)
    br = jnp.concatenate([b_router, jnp.zeros((LANES - ROUTER_COLS,), F32)])[None, :]

    def row(width):
        return pl.BlockSpec((tm, width), lambda i: (i, 0))

    def const(shape):
        return pl.BlockSpec(shape, lambda i: (0, 0), pipeline_mode=pl.Buffered(1))

    return pl.pallas_call(
        _out_proj_kernel,
        grid=(n_tok // tm,),
        in_specs=[row(D_MODEL), row(D_MODEL), const((D_MODEL, D_MODEL)), const((1, D_MODEL)),
                  const((D_MODEL, LANES)), const((D_MODEL, LANES)), const((1, LANES))],
        out_specs=[row(D_MODEL), row(D_MODEL), row(LANES)],
        out_shape=[jax.ShapeDtypeStruct((n_tok, D_MODEL), F32), jax.ShapeDtypeStruct((n_tok, D_MODEL), F32),
                   jax.ShapeDtypeStruct((n_tok, LANES), F32)],
        compiler_params=_params(("arbitrary",)),
        name="out_proj",
    )(x, merged, w_out, ffn_gain[None, :], wr_hi, wr_lo, br)


MOE_TM = 256


def _gather_rows(idx_ref, idx0, src_hbm, dst, sem, n_rows):
    def body(r, carry):
        t = idx_ref[idx0 + r]
        pltpu.make_async_copy(src_hbm.at[pl.ds(t, 1)], dst.at[pl.ds(r, 1)], sem).start()
        return carry
    lax.fori_loop(0, n_rows, body, 0)


def _wait_rows(src_hbm, dst, sem, n_rows):
    pltpu.make_async_copy(src_hbm.at[pl.ds(0, n_rows)], dst, sem).wait()


def _moe_kernel(blk_e_ref, n_used_ref, tok_ref, hn_hbm, roww_ref, wg_ref, wu_ref, wd_ref, ys_ref, xbuf, sem):
    i = pl.program_id(0)
    n_used = n_used_ref[0]

    @pl.when(i == 0)
    def _():
        _gather_rows(tok_ref, 0, hn_hbm, xbuf.at[0], sem.at[0], MOE_TM)

    @pl.when(i + 1 < n_used)
    def _():
        nxt = (i + 1) % 2
        _gather_rows(tok_ref, (i + 1) * MOE_TM, hn_hbm, xbuf.at[nxt], sem.at[nxt], MOE_TM)

    @pl.when(i < n_used)
    def _():
        slot = i % 2
        _wait_rows(hn_hbm, xbuf.at[slot], sem.at[slot], MOE_TM)
        x = xbuf[slot].astype(BF16)
        hg = _dot(x, wg_ref[...])
        hu = _dot(x, wu_ref[...])
        hmid = (jax.nn.silu(hg) * hu).astype(BF16)
        ys_ref[...] = _dot(hmid, wd_ref[...]) * roww_ref[...]

    @pl.when(i >= n_used)
    def _():
        ys_ref[...] = jnp.zeros_like(ys_ref)


def moe(hn, blk_e, n_used, row_tok, row_w, w_gate, w_up, w_down):
    rows = row_tok.shape[0]
    n_blocks = rows // MOE_TM
    grid_spec = pltpu.PrefetchScalarGridSpec(
        num_scalar_prefetch=3,
        grid=(n_blocks,),
        in_specs=[pl.BlockSpec(memory_space=pl.ANY),
                  pl.BlockSpec((MOE_TM, 1), lambda i, be, nu, tk: (i, 0)),
                  pl.BlockSpec((None, D_MODEL, EXPERT_FF), lambda i, be, nu, tk: (be[i], 0, 0)),
                  pl.BlockSpec((None, D_MODEL, EXPERT_FF), lambda i, be, nu, tk: (be[i], 0, 0)),
                  pl.BlockSpec((None, EXPERT_FF, D_MODEL), lambda i, be, nu, tk: (be[i], 0, 0))],
        out_specs=pl.BlockSpec((MOE_TM, D_MODEL), lambda i, be, nu, tk: (i, 0)),
        scratch_shapes=[pltpu.VMEM((2, MOE_TM, D_MODEL), F32), pltpu.SemaphoreType.DMA((2,))],
    )
    return pl.pallas_call(
        _moe_kernel,
        grid_spec=grid_spec,
        out_shape=jax.ShapeDtypeStruct((rows, D_MODEL), F32),
        compiler_params=_params(("arbitrary",)),
        name="moe",
    )(blk_e, n_used, row_tok, hn, row_w[:, None], w_gate, w_up, w_down)


def _route(logits, b_shape_tokens):
    n_tok = b_shape_tokens
    g_logits = logits[:, :N_EXPERT_GROUPS]
    g_sel = jnp.argmax(g_logits, axis=-1).astype(I32)
    g_w = jnp.take_along_axis(jax.nn.softmax(g_logits, axis=-1), g_sel[:, None], axis=-1)
    e_logits = logits[:, N_EXPERT_GROUPS:ROUTER_COLS].reshape(n_tok, N_EXPERT_GROUPS, EXPERTS_PER_GROUP)
    e_in = jnp.take_along_axis(e_logits, g_sel[:, None, None], axis=1)[:, 0]
    top_v, top_i = lax.top_k(e_in, TOP_K)
    w_tok = jax.nn.softmax(top_v, axis=-1) * g_w
    eid = g_sel[:, None] * EXPERTS_PER_GROUP + top_i.astype(I32)

    n_assign = n_tok * TOP_K
    e_flat = eid.reshape(-1)
    w_flat = w_tok.reshape(-1)
    tok = jnp.repeat(jnp.arange(n_tok, dtype=I32), TOP_K)
    counts = jnp.bincount(e_flat, length=N_EXPERTS).astype(I32)
    padded = (counts + MOE_TM - 1) // MOE_TM * MOE_TM
    pad_end = jnp.cumsum(padded)
    pad_start = pad_end - padded
    start = jnp.cumsum(counts) - counts
    order = jnp.argsort(e_flat)
    se = e_flat[order]
    dest = pad_start[se] + jnp.arange(n_assign, dtype=I32) - start[se]
    rows = -(-(n_assign + N_EXPERTS * (MOE_TM - 1)) // MOE_TM) * MOE_TM
    n_blocks = rows // MOE_TM
    row_tok = jnp.zeros((rows,), I32).at[dest].set(tok[order])
    row_w = jnp.zeros((rows,), F32).at[dest].set(w_flat[order])
    n_used = (pad_end[-1] // MOE_TM).astype(I32)
    blk = jnp.minimum(jnp.arange(n_blocks, dtype=I32), n_used - 1)
    blk_e = jnp.minimum(jnp.searchsorted(pad_end, blk * MOE_TM, side='right'), N_EXPERTS - 1).astype(I32)
    row_of_assign = jnp.zeros((n_assign,), I32).at[order].set(dest).reshape(n_tok, TOP_K)
    return blk_e, n_used.reshape(1), row_tok, row_w, row_of_assign


def _combine_kernel(r0_ref, r1_ref, ys_hbm, h_ref, o_ref, buf, sem, *, tm, tok0):
    i = pl.program_id(0)

    def issue(block, slot):
        _gather_rows(r0_ref, tok0 + block * tm, ys_hbm, buf.at[slot, 0], sem.at[slot], tm)
        _gather_rows(r1_ref, tok0 + block * tm, ys_hbm, buf.at[slot, 1], sem.at[slot], tm)

    @pl.when(i == 0)
    def _():
        issue(0, 0)

    @pl.when(i + 1 < pl.num_programs(0))
    def _():
        issue(i + 1, (i + 1) % 2)

    slot = i % 2
    _wait_rows(ys_hbm, buf.at[slot, 0], sem.at[slot], tm)
    _wait_rows(ys_hbm, buf.at[slot, 1], sem.at[slot], tm)
    o_ref[...] = h_ref[...] + (buf[slot, 0] + buf[slot, 1])


def combine(ys, h, rows0, rows1, *, tok0, n_tok, tm=256):
    blk0 = tok0 // tm
    grid_spec = pltpu.PrefetchScalarGridSpec(
        num_scalar_prefetch=2,
        grid=(n_tok // tm,),
        in_specs=[pl.BlockSpec(memory_space=pl.ANY),
                  pl.BlockSpec((tm, D_MODEL), lambda i, a, b: (blk0 + i, 0))],
        out_specs=pl.BlockSpec((tm, D_MODEL), lambda i, a, b: (i, 0)),
        scratch_shapes=[pltpu.VMEM((2, 2, tm, D_MODEL), F32), pltpu.SemaphoreType.DMA((2,))],
    )
    return pl.pallas_call(
        functools.partial(_combine_kernel, tm=tm, tok0=tok0),
        grid_spec=grid_spec,
        out_shape=jax.ShapeDtypeStruct((n_tok, D_MODEL), F32),
        compiler_params=_params(("arbitrary",)),
        name="combine",
    )(rows0, rows1, ys, h)


def _regroup_w_in(w_in):
    sizes = (SSM_WIDTH, ATTN_WIDTH, 256, 256, IDX_HEADS * IDX_DIM, IDX_DIM, IDX_HEADS, D_MODEL, D_MODEL)
    u, q, k, v, qi, ki, wi, ga, gb = jnp.split(w_in, np.cumsum(sizes)[:-1].tolist(), axis=1)
    pad = jnp.zeros((D_MODEL, PROJ_COLS - COL_KIWI - IDX_DIM - IDX_HEADS), F32)
    return jnp.concatenate([u, q, ga, gb, k, v, qi, ki, wi, pad], axis=1).astype(BF16)


def _layer(x_p, x_s, cache_k, cache_v, cache_ki, h0_re, h0_im, p):
    bp, tp, _ = x_p.shape
    bs, ts, _ = x_s.shape
    past = cache_k.shape[1]
    n_p, n_s = bp * tp, bs * ts
    n_tok = n_p + n_s
    kvw = N_KV_HEADS * HEAD_DIM

    x = jnp.concatenate([x_p.reshape(n_p, D_MODEL), x_s.reshape(n_s, D_MODEL)], axis=0)
    pos = jnp.concatenate([jnp.tile(jnp.arange(tp, dtype=I32), bp), jnp.tile(past + jnp.arange(ts, dtype=I32), bs)])

    proj = in_proj(x, p['norm_mix_g'][None, :], _regroup_w_in(p['w_in']))
    q_b, k_f, k_b, v_f, v_b, qi_b, ki_f, ki_b, wi = qk_post(proj, pos, p['q_norm_g'], p['k_norm_g'], p['idx_k_norm_g'])

    ssm_w = _ssm_weights(p['ssm_A_re'], p['ssm_A_im'], p['ssm_log_dt'], p['ssm_B_re'], p['ssm_B_im'],
                         p['ssm_C_re'], p['ssm_C_im'])
    zeros_h0 = jnp.zeros((bp, SSM_LB, 2, SSM_SB), F32)
    g_p, sre_p, sim_p = ssm(proj, ssm_w, p['ssm_D'], zeros_h0, n_batch=bp, seq=tp, tc=512, row0=0)
    h0 = jnp.stack([h0_re.reshape(bs, SSM_LB, SSM_SB), h0_im.reshape(bs, SSM_LB, SSM_SB)], axis=2)
    g_s, sre_s, sim_s = ssm(proj, ssm_w, p['ssm_D'], h0, n_batch=bs, seq=ts, tc=ts, row0=n_p)
    g = jnp.concatenate([g_p, g_s], axis=0)

    def seqs(a, n0, n1, b, t):
        return a[n0:n1].reshape(b, t, a.shape[-1])

    bq = 128
    n_sel_p = min(IDX_TOPK, tp // 4)
    qp, qip, wip = seqs(q_b, 0, n_p, bp, tp), seqs(qi_b, 0, n_p, bp, tp), seqs(wi, 0, n_p, bp, tp)
    kp, vp, kip = seqs(k_b, 0, n_p, bp, tp), seqs(v_b, 0, n_p, bp, tp), seqs(ki_b, 0, n_p, bp, tp)
    n_buckets = 4
    per = tp // bq // n_buckets
    attn_p = jnp.concatenate(
        [dsa(qp, qip, wip, kp, vp, kip, bq=bq, q_blk0=n * per, n_qblk=per, n_keys=(n + 1) * per * bq,
             q_pos0=0, s_valid=tp, n_sel=n_sel_p) for n in range(n_buckets)], axis=1)

    s_all = past + ts
    s_pad = -(-s_all // LANES) * LANES
    n_sel_s = min(IDX_TOPK, s_all // 4)

    def with_cache(cache, new, width):
        c = cache.reshape(bs, past, -1).astype(BF16)
        if c.shape[-1] < width:
            c = jnp.concatenate([c, jnp.zeros((bs, past, width - c.shape[-1]), BF16)], axis=-1)
        return jnp.concatenate([c, new, jnp.zeros((bs, s_pad - s_all, width), BF16)], axis=1)

    ks = with_cache(cache_k, seqs(k_b, n_p, n_tok, bs, ts), kvw)
    vs = with_cache(cache_v, seqs(v_b, n_p, n_tok, bs, ts), kvw)
    kis = with_cache(cache_ki, seqs(ki_b, n_p, n_tok, bs, ts), LANES)
    attn_s = dsa(seqs(q_b, n_p, n_tok, bs, ts), seqs(qi_b, n_p, n_tok, bs, ts), seqs(wi, n_p, n_tok, bs, ts),
                 ks, vs, kis, bq=ts, q_blk0=0, n_qblk=1, n_keys=s_pad, q_pos0=past, s_valid=s_all, n_sel=n_sel_s)
    attn = jnp.concatenate([attn_p.reshape(n_p, ATTN_WIDTH), attn_s.reshape(n_s, ATTN_WIDTH)], axis=0)

    merged = merge(g, attn, proj, p['w_glu_val'].astype(BF16), p['w_glu_gate'].astype(BF16),
                   p['w_attn_branch'].astype(BF16))
    w_router = jnp.concatenate([p['w_router_group'], p['w_router_expert']], axis=1)
    b_router = jnp.concatenate([p['b_router_group'], p['b_router_expert']])
    h, hn, logits = out_proj(x, merged, p['w_out'].astype(BF16), p['norm_ffn_g'], w_router, b_router)

    blk_e, n_used, row_tok, row_w, row_of_assign = _route(logits, n_tok)
    ys = moe(hn, blk_e, n_used, row_tok, row_w, p['w_exp_gate'].astype(BF16), p['w_exp_up'].astype(BF16),
             p['w_exp_down'].astype(BF16))
    rows0, rows1 = row_of_assign[:, 0], row_of_assign[:, 1]
    y_p = combine(ys, h, rows0, rows1, tok0=0, n_tok=n_p).reshape(bp, tp, D_MODEL)
    y_s = combine(ys, h, rows0, rows1, tok0=n_p, n_tok=n_s).reshape(bs, ts, D_MODEL)

    def heads(a, n0, n1, b, t):
        return a[n0:n1].reshape(b, t, N_KV_HEADS, HEAD_DIM)

    new_p = (heads(k_f, 0, n_p, bp, tp), heads(v_f, 0, n_p, bp, tp), ki_f[:n_p].reshape(bp, tp, IDX_DIM), sre_p, sim_p)
    new_s = (heads(k_f, n_p, n_tok, bs, ts), heads(v_f, n_p, n_tok, bs, ts), ki_f[n_p:].reshape(bs, ts, IDX_DIM),
             sre_s, sim_s)
    return y_p, y_s, new_p, new_s


def kernel(x_prompt, x_sample, cache_k, cache_v, cache_idx_k, state_ssm_re, state_ssm_im, norm_mix_g, w_in, q_norm_g, k_norm_g, idx_k_norm_g, ssm_A_re, ssm_A_im, ssm_log_dt, ssm_B_re, ssm_B_im, ssm_C_re, ssm_C_im, ssm_D, w_glu_val, w_glu_gate, w_attn_branch, w_out, norm_ffn_g, w_router_group, b_router_group, w_router_expert, b_router_expert, w_exp_gate, w_exp_up, w_exp_down):
    depth = w_in.shape[0]
    assert depth == 1, "prompt and sample tokens are batched through one layer"
    names = ('norm_mix_g', 'w_in', 'q_norm_g', 'k_norm_g', 'idx_k_norm_g', 'ssm_A_re', 'ssm_A_im', 'ssm_log_dt',
             'ssm_B_re', 'ssm_B_im', 'ssm_C_re', 'ssm_C_im', 'ssm_D', 'w_glu_val', 'w_glu_gate', 'w_attn_branch',
             'w_out', 'norm_ffn_g', 'w_router_group', 'b_router_group', 'w_router_expert', 'b_router_expert',
             'w_exp_gate', 'w_exp_up', 'w_exp_down')
    vals = (norm_mix_g, w_in, q_norm_g, k_norm_g, idx_k_norm_g, ssm_A_re, ssm_A_im, ssm_log_dt, ssm_B_re, ssm_B_im,
            ssm_C_re, ssm_C_im, ssm_D, w_glu_val, w_glu_gate, w_attn_branch, w_out, norm_ffn_g, w_router_group,
            b_router_group, w_router_expert, b_router_expert, w_exp_gate, w_exp_up, w_exp_down)
    p = {n: v[0] for n, v in zip(names, vals)}
    y_p, y_s, new_p, new_s = _layer(x_prompt, x_sample, cache_k[0], cache_v[0], cache_idx_k[0],
                                    state_ssm_re[0], state_ssm_im[0], p)
    st_p = tuple(a[None] for a in new_p)
    st_s = tuple(a[None] for a in new_s)
    return (y_p, y_s) + st_p + st_s
```

```python
import functools

import numpy as np
import jax
import jax.numpy as jnp
from jax import lax
from jax.experimental import pallas as pl
from jax.experimental.pallas import tpu as pltpu

F32 = jnp.float32
BF16 = jnp.bfloat16
I32 = jnp.int32

D_MODEL = 2048
CHUNK = 64
SSM_WIDTH = 1024
SSM_GROUP = 16
SSM_GROUPS = 64
SSM_STATE = 64
ATTN_WIDTH = 1024
HEAD_DIM = 128
N_HEADS = 8
N_KV_HEADS = 2
KV_GROUP = 4
IDX_HEADS = 8
IDX_DIM = 64
IDX_TOPK = 256
ROPE_THETA = 500000.0
N_EXPERT_GROUPS = 4
EXPERTS_PER_GROUP = 8
N_EXPERTS = 32
TOP_K = 2
EXPERT_FF = 1024
EPS = 1e-6

LANES = 128
SUBLANES = 8
VMEM_LIMIT = 56 * 1024 * 1024

COL_U, COL_Q, COL_GA, COL_GB, COL_K, COL_V, COL_QI, COL_KIWI = 0, 1024, 2048, 4096, 6144, 6400, 6656, 7168
PROJ_COLS = 7296
PROJ_TN = 2432
KV_WIDTH = N_KV_HEADS * HEAD_DIM

SSM_LB = SSM_WIDTH // LANES
SSM_SB = 8 * SSM_STATE

INT_MIN = np.int32(-2 ** 31)
KEY_NEG_INF = np.int32(np.array([0xFF800000], np.uint32).view(np.int32)[0] ^ 0x7FFFFFFF)


def _params(sem, vmem=VMEM_LIMIT):
    return pltpu.CompilerParams(dimension_semantics=sem, vmem_limit_bytes=vmem)


def _dot(a, b):
    return jnp.dot(a, b, preferred_element_type=F32)


def _dot_nt(a, b):
    return lax.dot_general(a, b, (((1,), (1,)), ((), ())), preferred_element_type=F32)


def _split_bf16(x):
    hi = x.astype(BF16)
    lo = (x - hi.astype(F32)).astype(BF16)
    return hi, lo


def _in_proj_kernel(x_ref, g_ref, w_ref, o_ref, xn_ref):
    @pl.when(pl.program_id(1) == 0)
    def _():
        x = x_ref[...]
        ms = jnp.mean(x * x, axis=-1, keepdims=True)
        xn_ref[...] = (x * lax.rsqrt(ms + EPS) * g_ref[...]).astype(BF16)

    o_ref[...] = _dot(xn_ref[...], w_ref[...])


def in_proj(x, gain, w_bf16, *, tm=512):
    n_tok = x.shape[0]
    return pl.pallas_call(
        _in_proj_kernel,
        grid=(n_tok // tm, PROJ_COLS // PROJ_TN),
        in_specs=[pl.BlockSpec((tm, D_MODEL), lambda i, j: (i, 0)),
                  pl.BlockSpec((1, D_MODEL), lambda i, j: (0, 0)),
                  pl.BlockSpec((D_MODEL, PROJ_TN), lambda i, j: (0, j))],
        out_specs=pl.BlockSpec((tm, PROJ_TN), lambda i, j: (i, j)),
        out_shape=jax.ShapeDtypeStruct((n_tok, PROJ_COLS), F32),
        scratch_shapes=[pltpu.VMEM((tm, D_MODEL), BF16)],
        compiler_params=_params(("arbitrary", "arbitrary")),
        name="in_proj",
    )(x, gain, w_bf16)


def _rope(x, c, s_lo, s_hi, half):
    n = x.shape[-1]
    return x * c + pltpu.roll(x, n - half, 1) * s_lo + pltpu.roll(x, half, 1) * s_hi


def _head_norm(x, g):
    ms = jnp.mean(x * x, axis=-1, keepdims=True)
    return x * lax.rsqrt(ms + EPS) * g


def _qk_post_kernel(q_ref, k_ref, v_ref, qi_ref, kw_ref, c128_ref, sl128_ref, sh128_ref,
                    c64_ref, sl64_ref, sh64_ref, qg_ref, kg_ref, ig_ref,
                    qo_ref, kf_ref, kb_ref, vf_ref, vb_ref, qio_ref, kif_ref, kib_ref, wo_ref):
    c128, sl128, sh128 = c128_ref[...], sl128_ref[...], sh128_ref[...]
    c64, sl64, sh64 = c64_ref[...], sl64_ref[...], sh64_ref[...]
    half128 = HEAD_DIM // 8
    half64 = IDX_DIM // 8
    for h in range(N_HEADS):
        sl = slice(h * LANES, (h + 1) * LANES)
        qo_ref[:, sl] = _rope(_head_norm(q_ref[:, sl], qg_ref[...]), c128, sl128, sh128, half128).astype(BF16)
    for h in range(N_KV_HEADS):
        sl = slice(h * LANES, (h + 1) * LANES)
        kk = _rope(_head_norm(k_ref[:, sl], kg_ref[...]), c128, sl128, sh128, half128)
        kf_ref[:, sl] = kk
        kb_ref[:, sl] = kk.astype(BF16)
    v = v_ref[...]
    vf_ref[...] = v
    vb_ref[...] = v.astype(BF16)
    lane = lax.broadcasted_iota(I32, c64.shape, 1)
    low = lane < IDX_DIM
    for p in range(IDX_HEADS // 2):
        x = _rope(qi_ref[:, p * LANES:(p + 1) * LANES], c64, sl64, sh64, half64)
        qio_ref[:, (2 * p) * LANES:(2 * p + 1) * LANES] = jnp.where(low, x, 0.0).astype(BF16)
        qio_ref[:, (2 * p + 1) * LANES:(2 * p + 2) * LANES] = jnp.where(low, pltpu.roll(x, IDX_DIM, 1), 0.0).astype(BF16)
    kw = kw_ref[...]
    ms = jnp.sum(jnp.where(low, kw * kw, 0.0), axis=-1, keepdims=True) * (1.0 / IDX_DIM)
    ki = _rope(kw * lax.rsqrt(ms + EPS) * ig_ref[...], c64, sl64, sh64, half64)
    kif_ref[...] = ki[:, :IDX_DIM]
    kib_ref[...] = jnp.where(low, ki, 0.0).astype(BF16)
    wo_ref[...] = (pltpu.roll(kw, IDX_DIM, 1) * IDX_HEADS ** -0.5) * IDX_DIM ** -0.5


def _rope_tables(pos, head_dim):
    r = head_dim // 4
    half = r // 2
    inv = ROPE_THETA ** (-jnp.arange(half, dtype=F32) * 2.0 / r)
    ang = pos.astype(F32)[:, None] * inv[None, :]
    cos, sin = jnp.cos(ang), jnp.sin(ang)
    n = pos.shape[0]
    zh = jnp.zeros((n, half), F32)
    rest = head_dim - r
    c = jnp.concatenate([cos, cos, jnp.ones((n, rest), F32)], axis=-1)
    s_lo = jnp.concatenate([-sin, zh, jnp.zeros((n, rest), F32)], axis=-1)
    s_hi = jnp.concatenate([zh, sin, jnp.zeros((n, rest), F32)], axis=-1)
    rep = LANES // head_dim
    return tuple(jnp.tile(t, (1, rep)) for t in (c, s_lo, s_hi))


def qk_post(proj, pos, q_gain, k_gain, ik_gain, *, tm=512):
    n_tok = proj.shape[0]
    t128 = _rope_tables(pos, HEAD_DIM)
    t64 = _rope_tables(pos, IDX_DIM)
    ik_gain128 = jnp.concatenate([ik_gain, jnp.zeros((LANES - IDX_DIM,), F32)])[None, :]

    def col(width, start):
        return pl.BlockSpec((tm, width), lambda i: (i, start // width))

    def row(width):
        return pl.BlockSpec((tm, width), lambda i: (i, 0))

    gain = pl.BlockSpec((1, LANES), lambda i: (0, 0))
    return pl.pallas_call(
        _qk_post_kernel,
        grid=(n_tok // tm,),
        in_specs=[col(ATTN_WIDTH, COL_Q), col(KV_WIDTH, COL_K), col(KV_WIDTH, COL_V), col(IDX_HEADS * IDX_DIM, COL_QI),
                  col(LANES, COL_KIWI)] + [row(LANES)] * 6 + [gain] * 3,
        out_specs=[row(ATTN_WIDTH), row(KV_WIDTH), row(KV_WIDTH), row(KV_WIDTH), row(KV_WIDTH), row(IDX_HEADS * LANES),
                   row(IDX_DIM), row(LANES), row(LANES)],
        out_shape=[jax.ShapeDtypeStruct((n_tok, ATTN_WIDTH), BF16),
                   jax.ShapeDtypeStruct((n_tok, KV_WIDTH), F32), jax.ShapeDtypeStruct((n_tok, KV_WIDTH), BF16),
                   jax.ShapeDtypeStruct((n_tok, KV_WIDTH), F32), jax.ShapeDtypeStruct((n_tok, KV_WIDTH), BF16),
                   jax.ShapeDtypeStruct((n_tok, IDX_HEADS * LANES), BF16),
                   jax.ShapeDtypeStruct((n_tok, IDX_DIM), F32), jax.ShapeDtypeStruct((n_tok, LANES), BF16),
                   jax.ShapeDtypeStruct((n_tok, LANES), F32)],
        compiler_params=_params(("arbitrary",)),
        name="qk_post",
    )(proj, proj, proj, proj, proj, *t128, *t64, q_gain[None, :], k_gain[None, :], ik_gain128)


def _gelu_tanh(x):
    return 0.5 * x * (1.0 + jnp.tanh(np.float32(np.sqrt(2.0 / np.pi)) * (x + 0.044715 * (x * x * x))))


def _ssm_kernel(u_ref, wb_ref, wc_ref, cst_ref, d_ref, h0_ref, g_ref, sre_ref, sim_ref,
                er_ref, ei_ref, car_ref, *, tc):
    c = pl.program_id(2)

    @pl.when(c == 0)
    def _():
        car_ref[...] = h0_ref[...]

    u = u_ref[...]
    e = _dot(u.astype(BF16), wb_ref[...])
    er_ref[...] = e[:, :SSM_SB]
    ei_ref[...] = e[:, SSM_SB:]

    def body(r, carry):
        cr, ci = carry
        i0 = pl.multiple_of(r * SUBLANES, SUBLANES)
        xr = er_ref[pl.ds(i0, SUBLANES), :]
        xi = ei_ref[pl.ds(i0, SUBLANES), :]
        for n, k in enumerate((1, 2, 4)):
            ar, ai = cst_ref[2 * n], cst_ref[2 * n + 1]
            sr, si = pltpu.roll(xr, k, 0), pltpu.roll(xi, k, 0)
            xr, xi = xr + ar * sr - ai * si, xi + ar * si + ai * sr
        pr, pi_ = cst_ref[6], cst_ref[7]
        xr, xi = xr + pr * cr - pi_ * ci, xi + pr * ci + pi_ * cr
        er_ref[pl.ds(i0, SUBLANES), :] = xr
        ei_ref[pl.ds(i0, SUBLANES), :] = xi
        return xr[SUBLANES - 1:SUBLANES, :], xi[SUBLANES - 1:SUBLANES, :]

    cr, ci = lax.fori_loop(0, tc // SUBLANES, body, (car_ref[0:1, :], car_ref[1:2, :]))
    car_ref[0:1, :] = cr
    car_ref[1:2, :] = ci

    y = _dot(er_ref[...].astype(BF16), wc_ref[0]) - _dot(ei_ref[...].astype(BF16), wc_ref[1])
    y = y + d_ref[...] * u
    g_ref[...] = _gelu_tanh(y).astype(BF16)

    @pl.when(c == pl.num_programs(2) - 1)
    def _():
        sre_ref[...] = cr
        sim_ref[...] = ci


def _ssm_weights(a_re, a_im, log_dt, b_re, b_im, c_re, c_im):
    lam_re, lam_im = a_re, a_im
    dt = jnp.exp(log_dt)[:, None]
    mag = jnp.exp(lam_re * dt)
    lb_re, lb_im = mag * jnp.cos(lam_im * dt), mag * jnp.sin(lam_im * dt)
    den = lam_re * lam_re + lam_im * lam_im
    num_re = lb_re - 1.0
    z_re = (num_re * lam_re + lb_im * lam_im) / den
    z_im = (lb_im * lam_re - num_re * lam_im) / den
    zb_re = z_re[:, :, None] * b_re - z_im[:, :, None] * b_im
    zb_im = z_re[:, :, None] * b_im + z_im[:, :, None] * b_re
    eye = jnp.eye(8, dtype=F32)

    def blockdiag_in(w):
        return jnp.einsum('jgph,gk->jghkp', w.reshape(SSM_LB, 8, SSM_STATE, SSM_GROUP), eye).reshape(SSM_LB, LANES, SSM_SB)

    def blockdiag_out(w):
        return jnp.einsum('jghp,gk->jkpgh', w.reshape(SSM_LB, 8, SSM_GROUP, SSM_STATE), eye).reshape(SSM_LB, SSM_SB, LANES)

    wb = jnp.concatenate([blockdiag_in(zb_re), blockdiag_in(zb_im)], axis=-1).astype(BF16)
    wc = jnp.stack([blockdiag_out(c_re), blockdiag_out(c_im)], axis=1).astype(BF16)

    pw = [(lb_re, lb_im)]
    for _ in range(7):
        pr, pi_ = pw[-1]
        pw.append((pr * lb_re - pi_ * lb_im, pr * lb_im + pi_ * lb_re))
    rows = jnp.arange(SUBLANES)[:, None]

    def lane(x):
        return x.reshape(SSM_LB, 1, SSM_SB)

    cst = []
    for k in (1, 2, 4):
        for part in pw[k - 1]:
            cst.append(jnp.where(rows >= k, lane(part), 0.0))
    cst.append(jnp.concatenate([lane(pw[r][0]) for r in range(SUBLANES)], axis=1))
    cst.append(jnp.concatenate([lane(pw[r][1]) for r in range(SUBLANES)], axis=1))
    cst = jnp.stack(cst, axis=1)
    return wb, wc, cst


def ssm(proj, ssm_w, d_skip, h0, *, n_batch, seq, tc, row0):
    wb, wc, cst = ssm_w
    n_chunks = seq // tc
    blk0 = row0 // tc
    n_tok = n_batch * seq
    state_shape = jax.ShapeDtypeStruct((n_batch, SSM_LB, 1, SSM_SB), F32)
    state_spec = pl.BlockSpec((None, None, 1, SSM_SB), lambda b, j, c: (b, j, 0, 0))
    g, s_re, s_im = pl.pallas_call(
        functools.partial(_ssm_kernel, tc=tc),
        grid=(n_batch, SSM_LB, n_chunks),
        in_specs=[pl.BlockSpec((tc, LANES), lambda b, j, c: (blk0 + b * n_chunks + c, j)),
                  pl.BlockSpec((None, LANES, 2 * SSM_SB), lambda b, j, c: (j, 0, 0)),
                  pl.BlockSpec((None, 2, SSM_SB, LANES), lambda b, j, c: (j, 0, 0, 0)),
                  pl.BlockSpec((None, 8, SUBLANES, SSM_SB), lambda b, j, c: (j, 0, 0, 0)),
                  pl.BlockSpec((1, LANES), lambda b, j, c: (0, j)),
                  pl.BlockSpec((None, None, 2, SSM_SB), lambda b, j, c: (b, j, 0, 0))],
        out_specs=[pl.BlockSpec((tc, LANES), lambda b, j, c: (b * n_chunks + c, j)), state_spec, state_spec],
        out_shape=[jax.ShapeDtypeStruct((n_tok, SSM_WIDTH), BF16), state_shape, state_shape],
        scratch_shapes=[pltpu.VMEM((tc, SSM_SB), F32), pltpu.VMEM((tc, SSM_SB), F32), pltpu.VMEM((2, SSM_SB), F32)],
        compiler_params=_params(("arbitrary", "arbitrary", "arbitrary")),
        name="ssm",
    )(proj, wb, wc, cst, d_skip[None, :], h0)
    return g, s_re.reshape(n_batch, SSM_GROUPS, SSM_STATE), s_im.reshape(n_batch, SSM_GROUPS, SSM_STATE)


def _row_sum(x):
    return jnp.sum(x, axis=1, keepdims=True)


def _row_count(mask):
    return _row_sum(jnp.where(mask, 1, 0))


def _dsa_body(q_ref, qi_ref, wi_ref, k_ref, v_ref, ki_ref, o_ref, key_ref, bias_ref, *, q_pos_first, s_valid, n_sel):
    bq, n_keys = key_ref.shape
    col = lax.broadcasted_iota(I32, (bq, n_keys), 1)
    qpos = q_pos_first + lax.broadcasted_iota(I32, (bq, 1), 0)
    allowed = col < jnp.minimum((qpos // CHUNK + 1) * CHUNK, s_valid)

    ki = ki_ref[...]
    score = None
    for h in range(IDX_HEADS):
        d = _dot_nt(qi_ref[:, h * LANES:(h + 1) * LANES], ki)
        t = jnp.maximum(d, 0.0) * wi_ref[:, h:h + 1]
        score = t if score is None else score + t
    score = jnp.where(score == 0.0, 0.0, score)
    bits = pltpu.bitcast(score, I32)
    key = jnp.where(bits < 0, bits ^ np.int32(0x7FFFFFFF), bits)
    key_ref[...] = jnp.where(allowed, key, KEY_NEG_INF)

    def bisect(i, base):
        cand = base + lax.shift_left(np.int32(1), np.int32(31) - i)
        cnt = _row_count(key_ref[...] >= cand)
        return jnp.where(cnt >= n_sel, cand, base)

    thr = lax.fori_loop(0, 32, bisect, jnp.full((bq, 1), INT_MIN, I32))

    key = key_ref[...]
    need = n_sel - _row_count(key > thr)
    n_eq = _row_count(key == thr)
    n_bits = int(n_keys - 1).bit_length()

    def tie_cut():
        def step(i, j0):
            cand = j0 + lax.shift_left(np.int32(1), np.int32(n_bits - 1) - i)
            cnt = _row_sum(jnp.where(key_ref[...] == thr, jnp.where(col < cand, 1, 0), 0))
            return jnp.where(cnt < need, cand, j0)
        return lax.fori_loop(0, n_bits, step, jnp.zeros((bq, 1), I32))

    split = jnp.max(jnp.where(n_eq > need, 1, 0)) > 0
    j_last = lax.cond(split, tie_cut, lambda: jnp.full((bq, 1), n_keys, I32))
    tie_bias = jnp.where(thr == KEY_NEG_INF, -jnp.inf, 0.0)
    bias_ref[...] = jnp.where(key > thr, 0.0,
                              jnp.where(key == thr, jnp.where(col <= j_last, tie_bias, -jnp.inf), -jnp.inf))

    scale = HEAD_DIM ** -0.5
    for kv in range(N_KV_HEADS):
        kk = k_ref[:, kv * HEAD_DIM:(kv + 1) * HEAD_DIM]
        vv = v_ref[:, kv * HEAD_DIM:(kv + 1) * HEAD_DIM]
        for g in range(KV_GROUP):
            sl = slice((kv * KV_GROUP + g) * HEAD_DIM, (kv * KV_GROUP + g + 1) * HEAD_DIM)
            logits = _dot_nt(q_ref[:, sl], kk) * scale + bias_ref[...]
            m = jnp.max(logits, axis=1, keepdims=True)
            p = jnp.exp(logits - m)
            den = jnp.sum(p, axis=1, keepdims=True)
            o_ref[:, sl] = (_dot(p.astype(BF16), vv) / den).astype(BF16)


def _dsa_kernel(q_ref, qi_ref, wi_ref, k_ref, v_ref, ki_ref, o_ref, key_ref, bias_ref, *, q_pos0, s_valid, n_sel):
    bq = key_ref.shape[0]
    _dsa_body(q_ref, qi_ref, wi_ref, k_ref, v_ref, ki_ref, o_ref, key_ref, bias_ref,
              q_pos_first=q_pos0 + pl.program_id(1) * bq, s_valid=s_valid, n_sel=n_sel)


def dsa(q, qi, wi, k, v, ki, *, bq, q_blk0, n_qblk, n_keys, n_sel):
    n_batch, seq = q.shape[:2]

    def qspec(width):
        return pl.BlockSpec((None, bq, width), lambda b, i: (b, q_blk0 + i, 0))

    def kspec(width):
        return pl.BlockSpec((None, n_keys, width), lambda b, i: (b, 0, 0))

    return pl.pallas_call(
        functools.partial(_dsa_kernel, q_pos0=q_blk0 * bq, s_valid=seq, n_sel=n_sel),
        grid=(n_batch, n_qblk),
        in_specs=[qspec(ATTN_WIDTH), qspec(IDX_HEADS * LANES), qspec(LANES), kspec(KV_WIDTH), kspec(KV_WIDTH), kspec(LANES)],
        out_specs=pl.BlockSpec((None, bq, ATTN_WIDTH), lambda b, i: (b, i, 0)),
        out_shape=jax.ShapeDtypeStruct((n_batch, n_qblk * bq, ATTN_WIDTH), BF16),
        scratch_shapes=[pltpu.VMEM((bq, n_keys), I32), pltpu.VMEM((bq, n_keys), F32)],
        compiler_params=_params(("arbitrary", "arbitrary")),
        name="dsa",
    )(q, qi, wi, k, v, ki)


def _dsa_step_kernel(q_ref, qi_ref, wi_ref, ck_ref, cv_ref, cki_ref, nk_ref, nv_ref, nki_ref, o_ref,
                     k_buf, v_buf, ki_buf, key_ref, bias_ref, *, past, n_sel):
    ts = nk_ref.shape[0]
    n_keys = k_buf.shape[0]
    for buf, cache, new in ((k_buf, ck_ref, nk_ref), (v_buf, cv_ref, nv_ref)):
        buf[0:past, :] = cache[...].astype(BF16)
        buf[past:past + ts, :] = new[...]
        buf[past + ts:n_keys, :] = jnp.zeros((n_keys - past - ts, KV_WIDTH), BF16)
    ki_buf[0:past, 0:IDX_DIM] = cki_ref[...].astype(BF16)
    ki_buf[0:past, IDX_DIM:LANES] = jnp.zeros((past, LANES - IDX_DIM), BF16)
    ki_buf[past:past + ts, :] = nki_ref[...]
    ki_buf[past + ts:n_keys, :] = jnp.zeros((n_keys - past - ts, LANES), BF16)
    _dsa_body(q_ref, qi_ref, wi_ref, k_buf, v_buf, ki_buf, o_ref, key_ref, bias_ref,
              q_pos_first=past, s_valid=past + ts, n_sel=n_sel)


def dsa_step(q, qi, wi, cache_k, cache_v, cache_ki, k_new, v_new, ki_new, *, n_sel):
    n_batch, ts = q.shape[:2]
    past = cache_k.shape[1]
    n_keys = -(-(past + ts) // LANES) * LANES

    def spec(rows, width):
        return pl.BlockSpec((None, rows, width), lambda b: (b, 0, 0))

    return pl.pallas_call(
        functools.partial(_dsa_step_kernel, past=past, n_sel=n_sel),
        grid=(n_batch,),
        in_specs=[spec(ts, ATTN_WIDTH), spec(ts, IDX_HEADS * LANES), spec(ts, LANES),
                  spec(past, KV_WIDTH), spec(past, KV_WIDTH), spec(past, IDX_DIM),
                  spec(ts, KV_WIDTH), spec(ts, KV_WIDTH), spec(ts, LANES)],
        out_specs=spec(ts, ATTN_WIDTH),
        out_shape=jax.ShapeDtypeStruct((n_batch, ts, ATTN_WIDTH), BF16),
        scratch_shapes=[pltpu.VMEM((n_keys, KV_WIDTH), BF16), pltpu.VMEM((n_keys, KV_WIDTH), BF16),
                        pltpu.VMEM((n_keys, LANES), BF16),
                        pltpu.VMEM((ts, n_keys), I32), pltpu.VMEM((ts, n_keys), F32)],
        compiler_params=_params(("arbitrary",)),
        name="dsa_step",
    )(q, qi, wi, cache_k, cache_v, cache_ki, k_new, v_new, ki_new)


def _merge_kernel(g_ref, a_ref, ga_ref, gb_ref, wv_ref, wg_ref, wb_ref, o_ref):
    g = g_ref[...]
    branch_a = _dot(g, wv_ref[...]) * jax.nn.sigmoid(_dot(g, wg_ref[...]))
    branch_b = _dot(a_ref[...], wb_ref[...])
    merged = jax.nn.sigmoid(ga_ref[...]) * branch_a + jax.nn.sigmoid(gb_ref[...]) * branch_b
    o_ref[...] = merged.astype(BF16)


def merge(g, attn, proj, w_val, w_gate, w_branch, *, tm=1024, tn=512):
    n_tok = g.shape[0]
    nj = D_MODEL // tn

    def wspec():
        return pl.BlockSpec((SSM_WIDTH, tn), lambda i, j: (0, j))

    return pl.pallas_call(
        _merge_kernel,
        grid=(n_tok // tm, nj),
        in_specs=[pl.BlockSpec((tm, SSM_WIDTH), lambda i, j: (i, 0)),
                  pl.BlockSpec((tm, ATTN_WIDTH), lambda i, j: (i, 0)),
                  pl.BlockSpec((tm, tn), lambda i, j: (i, COL_GA // tn + j)),
                  pl.BlockSpec((tm, tn), lambda i, j: (i, COL_GB // tn + j)),
                  wspec(), wspec(), wspec()],
        out_specs=pl.BlockSpec((tm, tn), lambda i, j: (i, j)),
        out_shape=jax.ShapeDtypeStruct((n_tok, D_MODEL), BF16),
        compiler_params=_params(("arbitrary", "arbitrary")),
        name="merge",
    )(g, attn, proj, proj, w_val, w_gate, w_branch)


ROUTER_COLS = N_EXPERT_GROUPS + N_EXPERTS
MOE_TM = 256


def _first_lane_of_max(x, lane_f):
    m = jnp.max(x, axis=1, keepdims=True)
    return m, jnp.min(jnp.where(x == m, lane_f, float(LANES)), axis=1, keepdims=True)


def _out_proj_kernel(x_ref, m_ref, wo_ref, gn_ref, wrh_ref, wrl_ref, br_ref,
                     h_ref, hn_ref, ri_ref, rw_ref, cnt_ref, carry_ref):
    @pl.when(pl.program_id(0) == 0)
    def _():
        carry_ref[...] = jnp.zeros_like(carry_ref)

    h = x_ref[...] + _dot(m_ref[...], wo_ref[...])
    h_ref[...] = h
    ms = jnp.mean(h * h, axis=-1, keepdims=True)
    hn = h * lax.rsqrt(ms + EPS) * gn_ref[...]
    hn_ref[...] = hn
    hh, hl = _split_bf16(hn)
    wrh = wrh_ref[...]
    lg = _dot(hh, wrh) + _dot(hl, wrh) + _dot(hh, wrl_ref[...]) + br_ref[...]

    tm = lg.shape[0]
    lane = lax.broadcasted_iota(I32, lg.shape, 1)
    lane_f = lane.astype(F32)
    ninf = -jnp.inf
    gl = jnp.where(lane < N_EXPERT_GROUPS, lg, ninf)
    gmax, gsel = _first_lane_of_max(gl, lane_f)
    g_w = 1.0 / jnp.sum(jnp.exp(gl - gmax), axis=1, keepdims=True)
    lo = N_EXPERT_GROUPS + EXPERTS_PER_GROUP * gsel
    el = jnp.where(lane_f >= lo, jnp.where(lane_f < lo + EXPERTS_PER_GROUP, lg, ninf), ninf)
    v1, i1 = _first_lane_of_max(el, lane_f)
    el2 = jnp.where(lane_f == i1, ninf, el)
    v2, i2 = _first_lane_of_max(el2, lane_f)
    t = jnp.exp(v2 - v1)
    s1 = 1.0 / (1.0 + t)
    w1 = s1 * g_w
    w2 = (t * s1) * g_w

    m1 = jnp.where(lane_f == i1, 1.0, 0.0)
    m2 = jnp.where(lane_f == i2, 1.0, 0.0)
    both = m1 + m2
    tri = jnp.where(lax.broadcasted_iota(I32, (tm, tm), 0) > lax.broadcasted_iota(I32, (tm, tm), 1), 1.0, 0.0)
    before = _dot(tri.astype(BF16), both.astype(BF16)) + carry_ref[...]
    r1 = jnp.sum(before * m1, axis=1, keepdims=True)
    r2 = jnp.sum(before * m2, axis=1, keepdims=True)
    carry_ref[...] = carry_ref[...] + jnp.sum(both, axis=0, keepdims=True)
    cnt_ref[...] = carry_ref[...]
    e1 = i1 - float(N_EXPERT_GROUPS)
    e2 = i2 - float(N_EXPERT_GROUPS)
    ri_ref[...] = jnp.where(lane == 0, e1, jnp.where(lane == 1, e2, jnp.where(lane == 2, r1, jnp.where(lane == 3, r2, 0.0)))).astype(I32)
    rw_ref[...] = jnp.where(lane == 0, w1, jnp.where(lane == 1, w2, 0.0))


def out_proj(x, merged, w_out, ffn_gain, w_router, b_router, *, tm=256):
    n_tok = x.shape[0]
    wr = jnp.concatenate([w_router, jnp.zeros((D_MODEL, LANES - ROUTER_COLS), F32)], axis=1)
    wr_hi = wr.astype(BF16)
    wr_lo = (wr - wr_hi.astype(F32)).astype(BF16)
    br = jnp.concatenate([b_router, jnp.zeros((LANES - ROUTER_COLS,), F32)])[None, :]

    def row(width):
        return pl.BlockSpec((tm, width), lambda i: (i, 0))

    def const(shape):
        return pl.BlockSpec(shape, lambda i: (0, 0), pipeline_mode=pl.Buffered(1))

    return pl.pallas_call(
        _out_proj_kernel,
        grid=(n_tok // tm,),
        in_specs=[row(D_MODEL), row(D_MODEL), const((D_MODEL, D_MODEL)), const((1, D_MODEL)),
                  const((D_MODEL, LANES)), const((D_MODEL, LANES)), const((1, LANES))],
        out_specs=[row(D_MODEL), row(D_MODEL), row(LANES), row(LANES), pl.BlockSpec((1, LANES), lambda i: (0, 0))],
        out_shape=[jax.ShapeDtypeStruct((n_tok, D_MODEL), F32), jax.ShapeDtypeStruct((n_tok, D_MODEL), F32),
                   jax.ShapeDtypeStruct((n_tok, LANES), I32), jax.ShapeDtypeStruct((n_tok, LANES), F32),
                   jax.ShapeDtypeStruct((1, LANES), F32)],
        scratch_shapes=[pltpu.VMEM((1, LANES), F32)],
        compiler_params=_params(("arbitrary",)),
        name="out_proj",
    )(x, merged, w_out, ffn_gain[None, :], wr_hi, wr_lo, br)


def _block_layout(counts):
    padded = (counts + MOE_TM - 1) // MOE_TM * MOE_TM
    pad_end = jnp.cumsum(padded).astype(I32)
    pad_start = pad_end - padded
    n_used = pad_end[-1] // MOE_TM
    return pad_start, pad_end, n_used


def _moe_rows(n_tok):
    return -(-(n_tok * TOP_K + N_EXPERTS * (MOE_TM - 1)) // MOE_TM) * MOE_TM


DISPATCH_TM = 512


def _wait_rows(src_hbm, dst, sem, n_rows):
    pltpu.make_async_copy(src_hbm.at[pl.ds(0, n_rows)], dst, sem).wait()


def _dispatch_kernel(d0_ref, d1_ref, pe_ref, cnt_ref, nu_ref, hn_hbm, xs_hbm, zbuf, sem, semz, *, n_blocks):
    i = pl.program_id(0)

    def zero_block(row0):
        return pltpu.make_async_copy(zbuf, xs_hbm.at[pl.ds(pl.multiple_of(row0, MOE_TM), MOE_TM)], semz)

    @pl.when(i == 0)
    def _():
        zbuf[...] = jnp.zeros_like(zbuf)
        for start in (True, False):
            for e in range(N_EXPERTS):
                @pl.when(cnt_ref[e] > 0)
                def _():
                    cp = zero_block(pe_ref[e] - MOE_TM)
                    cp.start() if start else cp.wait()

            def tail(b, c):
                cp = zero_block(b * MOE_TM)
                cp.start() if start else cp.wait()
                return c
            lax.fori_loop(nu_ref[0], n_blocks, tail, 0)

    base = i * DISPATCH_TM

    def body(r, c):
        t = base + r
        src = hn_hbm.at[pl.ds(t, 1)]
        pltpu.make_async_copy(src, xs_hbm.at[pl.ds(d0_ref[t], 1)], sem).start()
        pltpu.make_async_copy(src, xs_hbm.at[pl.ds(d1_ref[t], 1)], sem).start()
        return c
    lax.fori_loop(0, DISPATCH_TM, body, 0, unroll=8)
    for _ in range(TOP_K):
        _wait_rows(hn_hbm, xs_hbm.at[pl.ds(0, DISPATCH_TM)], sem, DISPATCH_TM)


def dispatch(hn, dest0, dest1, pad_end, counts, n_used):
    n_tok = hn.shape[0]
    rows = _moe_rows(n_tok)
    grid_spec = pltpu.PrefetchScalarGridSpec(
        num_scalar_prefetch=5,
        grid=(n_tok // DISPATCH_TM,),
        in_specs=[pl.BlockSpec(memory_space=pl.ANY)],
        out_specs=pl.BlockSpec(memory_space=pl.ANY),
        scratch_shapes=[pltpu.VMEM((MOE_TM, D_MODEL), F32), pltpu.SemaphoreType.DMA(()), pltpu.SemaphoreType.DMA(())],
    )
    return pl.pallas_call(
        functools.partial(_dispatch_kernel, n_blocks=rows // MOE_TM),
        grid_spec=grid_spec,
        out_shape=jax.ShapeDtypeStruct((rows, D_MODEL), F32),
        compiler_params=_params(("arbitrary",)),
        name="dispatch",
    )(dest0, dest1, pad_end, counts, n_used, hn)


def _moe_kernel(blk_e_ref, nu_ref, xs_ref, wg_ref, wu_ref, wd_ref, ys_ref):
    i = pl.program_id(0)

    @pl.when(i < nu_ref[0])
    def _():
        x = xs_ref[...].astype(BF16)
        hg = _dot(x, wg_ref[...])
        hu = _dot(x, wu_ref[...])
        hmid = (jax.nn.silu(hg) * hu).astype(BF16)
        ys_ref[...] = _dot(hmid, wd_ref[...])

    @pl.when(i >= nu_ref[0])
    def _():
        ys_ref[...] = jnp.zeros_like(ys_ref)


def moe(xs, blk_e, n_used, w_gate, w_up, w_down):
    rows = xs.shape[0]
    grid_spec = pltpu.PrefetchScalarGridSpec(
        num_scalar_prefetch=2,
        grid=(rows // MOE_TM,),
        in_specs=[pl.BlockSpec((MOE_TM, D_MODEL), lambda i, be, nu: (jnp.minimum(i, nu[0] - 1), 0)),
                  pl.BlockSpec((None, D_MODEL, EXPERT_FF), lambda i, be, nu: (be[i], 0, 0)),
                  pl.BlockSpec((None, D_MODEL, EXPERT_FF), lambda i, be, nu: (be[i], 0, 0)),
                  pl.BlockSpec((None, EXPERT_FF, D_MODEL), lambda i, be, nu: (be[i], 0, 0))],
        out_specs=pl.BlockSpec((MOE_TM, D_MODEL), lambda i, be, nu: (i, 0)),
    )
    return pl.pallas_call(
        _moe_kernel,
        grid_spec=grid_spec,
        out_shape=jax.ShapeDtypeStruct((rows, D_MODEL), F32),
        compiler_params=_params(("arbitrary",)),
        name="moe",
    )(blk_e, n_used, xs, w_gate, w_up, w_down)


def _gather_rows(idx_ref, idx0, src_hbm, dst, sem, n_rows):
    def body(r, carry):
        t = idx_ref[idx0 + r]
        pltpu.make_async_copy(src_hbm.at[pl.ds(t, 1)], dst.at[pl.ds(r, 1)], sem).start()
        return carry
    lax.fori_loop(0, n_rows, body, 0, unroll=8)


def _combine_kernel(r0_ref, r1_ref, ys_hbm, h_ref, w_ref, o_ref, buf, sem, *, tm, tok0):
    i = pl.program_id(0)

    def issue(block, slot):
        _gather_rows(r0_ref, tok0 + block * tm, ys_hbm, buf.at[slot, 0], sem.at[slot], tm)
        _gather_rows(r1_ref, tok0 + block * tm, ys_hbm, buf.at[slot, 1], sem.at[slot], tm)

    @pl.when(i == 0)
    def _():
        issue(0, 0)

    @pl.when(i + 1 < pl.num_programs(0))
    def _():
        issue(i + 1, (i + 1) % 2)

    slot = i % 2
    _wait_rows(ys_hbm, buf.at[slot, 0], sem.at[slot], tm)
    _wait_rows(ys_hbm, buf.at[slot, 1], sem.at[slot], tm)
    w = w_ref[...]
    o_ref[...] = h_ref[...] + (buf[slot, 0] * w[:, 0:1] + buf[slot, 1] * w[:, 1:2])


def combine(ys, h, route_w, rows0, rows1, *, tok0, n_tok, tm=256):
    blk0 = tok0 // tm
    grid_spec = pltpu.PrefetchScalarGridSpec(
        num_scalar_prefetch=2,
        grid=(n_tok // tm,),
        in_specs=[pl.BlockSpec(memory_space=pl.ANY),
                  pl.BlockSpec((tm, D_MODEL), lambda i, a, b: (blk0 + i, 0)),
                  pl.BlockSpec((tm, LANES), lambda i, a, b: (blk0 + i, 0))],
        out_specs=pl.BlockSpec((tm, D_MODEL), lambda i, a, b: (i, 0)),
        scratch_shapes=[pltpu.VMEM((2, 2, tm, D_MODEL), F32), pltpu.SemaphoreType.DMA((2,))],
    )
    return pl.pallas_call(
        functools.partial(_combine_kernel, tm=tm, tok0=tok0),
        grid_spec=grid_spec,
        out_shape=jax.ShapeDtypeStruct((n_tok, D_MODEL), F32),
        compiler_params=_params(("arbitrary",)),
        name="combine",
    )(rows0, rows1, ys, h, route_w)


def _regroup_w_in(w_in):
    sizes = (SSM_WIDTH, ATTN_WIDTH, KV_WIDTH, KV_WIDTH, IDX_HEADS * IDX_DIM, IDX_DIM, IDX_HEADS, D_MODEL, D_MODEL)
    u, q, k, v, qi, ki, wi, ga, gb = jnp.split(w_in, np.cumsum(sizes)[:-1].tolist(), axis=1)
    pad = jnp.zeros((D_MODEL, PROJ_COLS - COL_KIWI - IDX_DIM - IDX_HEADS), F32)
    return jnp.concatenate([u, q, ga, gb, k, v, qi, ki, wi, pad], axis=1).astype(BF16)


def _layer(x_p, x_s, cache_k, cache_v, cache_ki, h0_re, h0_im, p):
    bp, tp, _ = x_p.shape
    bs, ts, _ = x_s.shape
    past = cache_k.shape[1]
    n_p, n_s = bp * tp, bs * ts
    n_tok = n_p + n_s

    x = jnp.concatenate([x_p.reshape(n_p, D_MODEL), x_s.reshape(n_s, D_MODEL)], axis=0)
    pos = jnp.concatenate([jnp.tile(jnp.arange(tp, dtype=I32), bp), jnp.tile(past + jnp.arange(ts, dtype=I32), bs)])

    proj = in_proj(x, p['norm_mix_g'][None, :], _regroup_w_in(p['w_in']))
    q_b, k_f, k_b, v_f, v_b, qi_b, ki_f, ki_b, wi = qk_post(proj, pos, p['q_norm_g'], p['k_norm_g'], p['idx_k_norm_g'])

    ssm_w = _ssm_weights(p['ssm_A_re'], p['ssm_A_im'], p['ssm_log_dt'], p['ssm_B_re'], p['ssm_B_im'],
                         p['ssm_C_re'], p['ssm_C_im'])
    zeros_h0 = jnp.zeros((bp, SSM_LB, 2, SSM_SB), F32)
    g_p, sre_p, sim_p = ssm(proj, ssm_w, p['ssm_D'], zeros_h0, n_batch=bp, seq=tp, tc=512, row0=0)
    h0 = jnp.stack([h0_re.reshape(bs, SSM_LB, SSM_SB), h0_im.reshape(bs, SSM_LB, SSM_SB)], axis=2)
    g_s, sre_s, sim_s = ssm(proj, ssm_w, p['ssm_D'], h0, n_batch=bs, seq=ts, tc=ts, row0=n_p)
    g = jnp.concatenate([g_p, g_s], axis=0)

    def seqs(a, n0, n1, b, t):
        return a[n0:n1].reshape(b, t, a.shape[-1])

    bq = 128
    qp, qip, wip = seqs(q_b, 0, n_p, bp, tp), seqs(qi_b, 0, n_p, bp, tp), seqs(wi, 0, n_p, bp, tp)
    kp, vp, kip = seqs(k_b, 0, n_p, bp, tp), seqs(v_b, 0, n_p, bp, tp), seqs(ki_b, 0, n_p, bp, tp)
    n_buckets = 4
    per = tp // bq // n_buckets
    attn_p = jnp.concatenate(
        [dsa(qp, qip, wip, kp, vp, kip, bq=bq, q_blk0=n * per, n_qblk=per, n_keys=(n + 1) * per * bq,
             n_sel=min(IDX_TOPK, tp // 4)) for n in range(n_buckets)], axis=1)
    attn_s = dsa_step(seqs(q_b, n_p, n_tok, bs, ts), seqs(qi_b, n_p, n_tok, bs, ts), seqs(wi, n_p, n_tok, bs, ts),
                      cache_k.reshape(bs, past, KV_WIDTH), cache_v.reshape(bs, past, KV_WIDTH), cache_ki,
                      seqs(k_b, n_p, n_tok, bs, ts), seqs(v_b, n_p, n_tok, bs, ts), seqs(ki_b, n_p, n_tok, bs, ts),
                      n_sel=min(IDX_TOPK, (past + ts) // 4))
    attn = jnp.concatenate([attn_p.reshape(n_p, ATTN_WIDTH), attn_s.reshape(n_s, ATTN_WIDTH)], axis=0)

    merged = merge(g, attn, proj, p['w_glu_val'].astype(BF16), p['w_glu_gate'].astype(BF16),
                   p['w_attn_branch'].astype(BF16))
    w_router = jnp.concatenate([p['w_router_group'], p['w_router_expert']], axis=1)
    b_router = jnp.concatenate([p['b_router_group'], p['b_router_expert']])
    h, hn, route_i, route_w, cnt = out_proj(x, merged, p['w_out'].astype(BF16), p['norm_ffn_g'], w_router, b_router)

    counts = cnt[0, N_EXPERT_GROUPS:ROUTER_COLS].astype(I32)
    pad_start, pad_end, n_used = _block_layout(counts)
    dest0 = pad_start[route_i[:, 0]] + route_i[:, 2]
    dest1 = pad_start[route_i[:, 1]] + route_i[:, 3]
    n_blocks = _moe_rows(n_tok) // MOE_TM
    blk = jnp.minimum(jnp.arange(n_blocks, dtype=I32), n_used - 1)
    blk_e = jnp.minimum(jnp.searchsorted(pad_end, blk * MOE_TM, side='right'), N_EXPERTS - 1).astype(I32)
    n_used = n_used.reshape(1)

    xs = dispatch(hn, dest0, dest1, pad_end, counts, n_used)
    ys = moe(xs, blk_e, n_used, p['w_exp_gate'].astype(BF16), p['w_exp_up'].astype(BF16), p['w_exp_down'].astype(BF16))
    y_p = combine(ys, h, route_w, dest0, dest1, tok0=0, n_tok=n_p).reshape(bp, tp, D_MODEL)
    y_s = combine(ys, h, route_w, dest0, dest1, tok0=n_p, n_tok=n_s).reshape(bs, ts, D_MODEL)

    def heads(a, n0, n1, b, t):
        return a[n0:n1].reshape(b, t, N_KV_HEADS, HEAD_DIM)

    new_p = (heads(k_f, 0, n_p, bp, tp), heads(v_f, 0, n_p, bp, tp), ki_f[:n_p].reshape(bp, tp, IDX_DIM), sre_p, sim_p)
    new_s = (heads(k_f, n_p, n_tok, bs, ts), heads(v_f, n_p, n_tok, bs, ts), ki_f[n_p:].reshape(bs, ts, IDX_DIM),
             sre_s, sim_s)
    return y_p, y_s, new_p, new_s


def kernel(x_prompt, x_sample, cache_k, cache_v, cache_idx_k, state_ssm_re, state_ssm_im, norm_mix_g, w_in, q_norm_g, k_norm_g, idx_k_norm_g, ssm_A_re, ssm_A_im, ssm_log_dt, ssm_B_re, ssm_B_im, ssm_C_re, ssm_C_im, ssm_D, w_glu_val, w_glu_gate, w_attn_branch, w_out, norm_ffn_g, w_router_group, b_router_group, w_router_expert, b_router_expert, w_exp_gate, w_exp_up, w_exp_down):
    depth = w_in.shape[0]
    assert depth == 1, "prompt and sample tokens are batched through one layer"
    names = ('norm_mix_g', 'w_in', 'q_norm_g', 'k_norm_g', 'idx_k_norm_g', 'ssm_A_re', 'ssm_A_im', 'ssm_log_dt',
             'ssm_B_re', 'ssm_B_im', 'ssm_C_re', 'ssm_C_im', 'ssm_D', 'w_glu_val', 'w_glu_gate', 'w_attn_branch',
             'w_out', 'norm_ffn_g', 'w_router_group', 'b_router_group', 'w_router_expert', 'b_router_expert',
             'w_exp_gate', 'w_exp_up', 'w_exp_down')
    vals = (norm_mix_g, w_in, q_norm_g, k_norm_g, idx_k_norm_g, ssm_A_re, ssm_A_im, ssm_log_dt, ssm_B_re, ssm_B_im,
            ssm_C_re, ssm_C_im, ssm_D, w_glu_val, w_glu_gate, w_attn_branch, w_out, norm_ffn_g, w_router_group,
            b_router_group, w_router_expert, b_router_expert, w_exp_gate, w_exp_up, w_exp_down)
    p = {n: v[0] for n, v in zip(names, vals)}
    y_p, y_s, new_p, new_s = _layer(x_prompt, x_sample, cache_k[0], cache_v[0], cache_idx_k[0],
                                    state_ssm_re[0], state_ssm_im[0], p)
    st_p = tuple(a[None] for a in new_p)
    st_s = tuple(a[None] for a in new_s)
    return (y_p, y_s) + st_p + st_s
```

```python
import functools

import numpy as np
import jax
import jax.numpy as jnp
from jax import lax
from jax.experimental import pallas as pl
from jax.experimental.pallas import tpu as pltpu

F32 = jnp.float32
BF16 = jnp.bfloat16
I32 = jnp.int32

D_MODEL = 2048
CHUNK = 64
SSM_WIDTH = 1024
SSM_GROUP = 16
SSM_GROUPS = 64
SSM_STATE = 64
ATTN_WIDTH = 1024
HEAD_DIM = 128
N_HEADS = 8
N_KV_HEADS = 2
KV_GROUP = 4
IDX_HEADS = 8
IDX_DIM = 64
IDX_TOPK = 256
ROPE_THETA = 500000.0
N_EXPERT_GROUPS = 4
EXPERTS_PER_GROUP = 8
N_EXPERTS = 32
TOP_K = 2
EXPERT_FF = 1024
EPS = 1e-6

LANES = 128
SUBLANES = 8
VMEM_LIMIT = 56 * 1024 * 1024

COL_U, COL_Q, COL_GA, COL_GB, COL_K, COL_V, COL_QI, COL_KIWI = 0, 1024, 2048, 4096, 6144, 6400, 6656, 7168
PROJ_COLS = 7296
PROJ_TN = 2432
KV_WIDTH = N_KV_HEADS * HEAD_DIM

SSM_LB = SSM_WIDTH // LANES
SSM_SB = 8 * SSM_STATE

INT_MIN = np.int32(-2 ** 31)
KEY_NEG_INF = np.int32(np.array([0xFF800000], np.uint32).view(np.int32)[0] ^ 0x7FFFFFFF)


def _params(sem, vmem=VMEM_LIMIT):
    return pltpu.CompilerParams(dimension_semantics=sem, vmem_limit_bytes=vmem)


def _dot(a, b):
    return jnp.dot(a, b, preferred_element_type=F32)


def _dot_nt(a, b):
    return lax.dot_general(a, b, (((1,), (1,)), ((), ())), preferred_element_type=F32)


def _split_bf16(x):
    hi = x.astype(BF16)
    lo = (x - hi.astype(F32)).astype(BF16)
    return hi, lo


def _in_proj_kernel(x_ref, g_ref, w_ref, o_ref, xn_ref):
    @pl.when(pl.program_id(1) == 0)
    def _():
        x = x_ref[...]
        ms = jnp.mean(x * x, axis=-1, keepdims=True)
        xn_ref[...] = (x * lax.rsqrt(ms + EPS) * g_ref[...]).astype(BF16)

    o_ref[...] = _dot(xn_ref[...], w_ref[...])


def in_proj(x, gain, w_bf16, *, tm=512):
    n_tok = x.shape[0]
    return pl.pallas_call(
        _in_proj_kernel,
        grid=(n_tok // tm, PROJ_COLS // PROJ_TN),
        in_specs=[pl.BlockSpec((tm, D_MODEL), lambda i, j: (i, 0)),
                  pl.BlockSpec((1, D_MODEL), lambda i, j: (0, 0)),
                  pl.BlockSpec((D_MODEL, PROJ_TN), lambda i, j: (0, j))],
        out_specs=pl.BlockSpec((tm, PROJ_TN), lambda i, j: (i, j)),
        out_shape=jax.ShapeDtypeStruct((n_tok, PROJ_COLS), F32),
        scratch_shapes=[pltpu.VMEM((tm, D_MODEL), BF16)],
        compiler_params=_params(("arbitrary", "arbitrary")),
        name="in_proj",
    )(x, gain, w_bf16)


def _rope(x, c, s_lo, s_hi, half):
    n = x.shape[-1]
    return x * c + pltpu.roll(x, n - half, 1) * s_lo + pltpu.roll(x, half, 1) * s_hi


def _head_norm(x, g):
    ms = jnp.mean(x * x, axis=-1, keepdims=True)
    return x * lax.rsqrt(ms + EPS) * g


V_AUG = 2 * HEAD_DIM


def _store_v_aug(dst_ref, row0, v):
    n = v.shape[0]
    one_col = jnp.where(lax.broadcasted_iota(I32, (n, HEAD_DIM), 1) == 0, 1.0, 0.0).astype(BF16)
    for h in range(N_KV_HEADS):
        dst_ref[row0:row0 + n, h * V_AUG:h * V_AUG + HEAD_DIM] = v[:, h * HEAD_DIM:(h + 1) * HEAD_DIM].astype(BF16)
        dst_ref[row0:row0 + n, h * V_AUG + HEAD_DIM:(h + 1) * V_AUG] = one_col


def _qk_post_kernel(q_ref, k_ref, v_ref, qi_ref, kw_ref, c128_ref, sl128_ref, sh128_ref,
                    c64_ref, sl64_ref, sh64_ref, qg_ref, kg_ref, ig_ref,
                    qo_ref, kf_ref, kb_ref, vf_ref, vb_ref, qio_ref, kif_ref, kib_ref, wo_ref):
    c128, sl128, sh128 = c128_ref[...], sl128_ref[...], sh128_ref[...]
    c64, sl64, sh64 = c64_ref[...], sl64_ref[...], sh64_ref[...]
    half128 = HEAD_DIM // 8
    half64 = IDX_DIM // 8
    for h in range(N_HEADS):
        sl = slice(h * LANES, (h + 1) * LANES)
        qo_ref[:, sl] = _rope(_head_norm(q_ref[:, sl], qg_ref[...]), c128, sl128, sh128, half128).astype(BF16)
    for h in range(N_KV_HEADS):
        sl = slice(h * LANES, (h + 1) * LANES)
        kk = _rope(_head_norm(k_ref[:, sl], kg_ref[...]), c128, sl128, sh128, half128)
        kf_ref[:, sl] = kk
        kb_ref[:, sl] = kk.astype(BF16)
    v = v_ref[...]
    vf_ref[...] = v
    _store_v_aug(vb_ref, 0, v)
    lane = lax.broadcasted_iota(I32, c64.shape, 1)
    low = lane < IDX_DIM
    for p in range(IDX_HEADS // 2):
        x = _rope(qi_ref[:, p * LANES:(p + 1) * LANES], c64, sl64, sh64, half64)
        qio_ref[:, (2 * p) * LANES:(2 * p + 1) * LANES] = jnp.where(low, x, 0.0).astype(BF16)
        qio_ref[:, (2 * p + 1) * LANES:(2 * p + 2) * LANES] = jnp.where(low, pltpu.roll(x, IDX_DIM, 1), 0.0).astype(BF16)
    kw = kw_ref[...]
    ms = jnp.sum(jnp.where(low, kw * kw, 0.0), axis=-1, keepdims=True) * (1.0 / IDX_DIM)
    ki = _rope(kw * lax.rsqrt(ms + EPS) * ig_ref[...], c64, sl64, sh64, half64)
    kif_ref[...] = ki[:, :IDX_DIM]
    kib_ref[...] = jnp.where(low, ki, 0.0).astype(BF16)
    wo_ref[...] = (pltpu.roll(kw, IDX_DIM, 1) * IDX_HEADS ** -0.5) * IDX_DIM ** -0.5


def _rope_tables(pos, head_dim):
    r = head_dim // 4
    half = r // 2
    inv = ROPE_THETA ** (-jnp.arange(half, dtype=F32) * 2.0 / r)
    ang = pos.astype(F32)[:, None] * inv[None, :]
    cos, sin = jnp.cos(ang), jnp.sin(ang)
    n = pos.shape[0]
    zh = jnp.zeros((n, half), F32)
    rest = head_dim - r
    c = jnp.concatenate([cos, cos, jnp.ones((n, rest), F32)], axis=-1)
    s_lo = jnp.concatenate([-sin, zh, jnp.zeros((n, rest), F32)], axis=-1)
    s_hi = jnp.concatenate([zh, sin, jnp.zeros((n, rest), F32)], axis=-1)
    rep = LANES // head_dim
    return tuple(jnp.tile(t, (1, rep)) for t in (c, s_lo, s_hi))


QK_TM = 512


def qk_post(proj, table_pos, table_block, q_gain, k_gain, ik_gain):
    tm = QK_TM
    n_tok = proj.shape[0]
    t128 = _rope_tables(table_pos, HEAD_DIM)
    t64 = _rope_tables(table_pos, IDX_DIM)
    ik_gain128 = jnp.concatenate([ik_gain, jnp.zeros((LANES - IDX_DIM,), F32)])[None, :]

    def col(width, start):
        return pl.BlockSpec((tm, width), lambda i: (i, start // width))

    def row(width):
        return pl.BlockSpec((tm, width), lambda i: (i, 0))

    table = pl.BlockSpec((tm, LANES), lambda i: (table_block(i), 0))
    gain = pl.BlockSpec((1, LANES), lambda i: (0, 0))
    return pl.pallas_call(
        _qk_post_kernel,
        grid=(n_tok // tm,),
        in_specs=[col(ATTN_WIDTH, COL_Q), col(KV_WIDTH, COL_K), col(KV_WIDTH, COL_V), col(IDX_HEADS * IDX_DIM, COL_QI),
                  col(LANES, COL_KIWI)] + [table] * 6 + [gain] * 3,
        out_specs=[row(ATTN_WIDTH), row(KV_WIDTH), row(KV_WIDTH), row(KV_WIDTH), row(N_KV_HEADS * V_AUG), row(IDX_HEADS * LANES),
                   row(IDX_DIM), row(LANES), row(LANES)],
        out_shape=[jax.ShapeDtypeStruct((n_tok, ATTN_WIDTH), BF16),
                   jax.ShapeDtypeStruct((n_tok, KV_WIDTH), F32), jax.ShapeDtypeStruct((n_tok, KV_WIDTH), BF16),
                   jax.ShapeDtypeStruct((n_tok, KV_WIDTH), F32), jax.ShapeDtypeStruct((n_tok, N_KV_HEADS * V_AUG), BF16),
                   jax.ShapeDtypeStruct((n_tok, IDX_HEADS * LANES), BF16),
                   jax.ShapeDtypeStruct((n_tok, IDX_DIM), F32), jax.ShapeDtypeStruct((n_tok, LANES), BF16),
                   jax.ShapeDtypeStruct((n_tok, LANES), F32)],
        compiler_params=_params(("arbitrary",)),
        name="qk_post",
    )(proj, proj, proj, proj, proj, *t128, *t64, q_gain[None, :], k_gain[None, :], ik_gain128)


def _gelu_tanh(x):
    return 0.5 * x * (1.0 + jnp.tanh(np.float32(np.sqrt(2.0 / np.pi)) * (x + 0.044715 * (x * x * x))))


def _ssm_kernel(u_ref, wb_ref, wc_ref, cst_ref, d_ref, h0_ref, g_ref, sre_ref, sim_ref,
                er_ref, ei_ref, car_ref, *, tc):
    c = pl.program_id(2)

    @pl.when(c == 0)
    def _():
        car_ref[...] = h0_ref[...]

    u = u_ref[...]
    e = _dot(u.astype(BF16), wb_ref[...])
    er_ref[...] = e[:, :SSM_SB]
    ei_ref[...] = e[:, SSM_SB:]

    def body(r, carry):
        cr, ci = carry
        i0 = pl.multiple_of(r * SUBLANES, SUBLANES)
        xr = er_ref[pl.ds(i0, SUBLANES), :]
        xi = ei_ref[pl.ds(i0, SUBLANES), :]
        for n, k in enumerate((1, 2, 4)):
            ar, ai = cst_ref[2 * n], cst_ref[2 * n + 1]
            sr, si = pltpu.roll(xr, k, 0), pltpu.roll(xi, k, 0)
            xr, xi = xr + ar * sr - ai * si, xi + ar * si + ai * sr
        pr, pi_ = cst_ref[6], cst_ref[7]
        xr, xi = xr + pr * cr - pi_ * ci, xi + pr * ci + pi_ * cr
        er_ref[pl.ds(i0, SUBLANES), :] = xr
        ei_ref[pl.ds(i0, SUBLANES), :] = xi
        return xr[SUBLANES - 1:SUBLANES, :], xi[SUBLANES - 1:SUBLANES, :]

    cr, ci = lax.fori_loop(0, tc // SUBLANES, body, (car_ref[0:1, :], car_ref[1:2, :]))
    car_ref[0:1, :] = cr
    car_ref[1:2, :] = ci

    y = _dot(er_ref[...].astype(BF16), wc_ref[0]) - _dot(ei_ref[...].astype(BF16), wc_ref[1])
    y = y + d_ref[...] * u
    g_ref[...] = _gelu_tanh(y).astype(BF16)

    @pl.when(c == pl.num_programs(2) - 1)
    def _():
        sre_ref[...] = cr
        sim_ref[...] = ci


def _ssm_weights(a_re, a_im, log_dt, b_re, b_im, c_re, c_im):
    lam_re, lam_im = a_re, a_im
    dt = jnp.exp(log_dt)[:, None]
    mag = jnp.exp(lam_re * dt)
    lb_re, lb_im = mag * jnp.cos(lam_im * dt), mag * jnp.sin(lam_im * dt)
    den = lam_re * lam_re + lam_im * lam_im
    num_re = lb_re - 1.0
    z_re = (num_re * lam_re + lb_im * lam_im) / den
    z_im = (lb_im * lam_re - num_re * lam_im) / den
    zb_re = z_re[:, :, None] * b_re - z_im[:, :, None] * b_im
    zb_im = z_re[:, :, None] * b_im + z_im[:, :, None] * b_re
    eye = jnp.eye(8, dtype=F32)

    def blockdiag_in(w):
        return jnp.einsum('jgph,gk->jghkp', w.reshape(SSM_LB, 8, SSM_STATE, SSM_GROUP), eye).reshape(SSM_LB, LANES, SSM_SB)

    def blockdiag_out(w):
        return jnp.einsum('jghp,gk->jkpgh', w.reshape(SSM_LB, 8, SSM_GROUP, SSM_STATE), eye).reshape(SSM_LB, SSM_SB, LANES)

    wb = jnp.concatenate([blockdiag_in(zb_re), blockdiag_in(zb_im)], axis=-1).astype(BF16)
    wc = jnp.stack([blockdiag_out(c_re), blockdiag_out(c_im)], axis=1).astype(BF16)

    pw = [(lb_re, lb_im)]
    for _ in range(7):
        pr, pi_ = pw[-1]
        pw.append((pr * lb_re - pi_ * lb_im, pr * lb_im + pi_ * lb_re))
    rows = jnp.arange(SUBLANES)[:, None]

    def lane(x):
        return x.reshape(SSM_LB, 1, SSM_SB)

    cst = []
    for k in (1, 2, 4):
        for part in pw[k - 1]:
            cst.append(jnp.where(rows >= k, lane(part), 0.0))
    cst.append(jnp.concatenate([lane(pw[r][0]) for r in range(SUBLANES)], axis=1))
    cst.append(jnp.concatenate([lane(pw[r][1]) for r in range(SUBLANES)], axis=1))
    cst = jnp.stack(cst, axis=1)
    return wb, wc, cst


def ssm(proj, ssm_w, d_skip, h0, *, n_batch, seq, tc, row0):
    wb, wc, cst = ssm_w
    n_chunks = seq // tc
    blk0 = row0 // tc
    n_tok = n_batch * seq
    state_shape = jax.ShapeDtypeStruct((n_batch, SSM_LB, 1, SSM_SB), F32)
    state_spec = pl.BlockSpec((None, None, 1, SSM_SB), lambda b, j, c: (b, j, 0, 0))
    g, s_re, s_im = pl.pallas_call(
        functools.partial(_ssm_kernel, tc=tc),
        grid=(n_batch, SSM_LB, n_chunks),
        in_specs=[pl.BlockSpec((tc, LANES), lambda b, j, c: (blk0 + b * n_chunks + c, j)),
                  pl.BlockSpec((None, LANES, 2 * SSM_SB), lambda b, j, c: (j, 0, 0)),
                  pl.BlockSpec((None, 2, SSM_SB, LANES), lambda b, j, c: (j, 0, 0, 0)),
                  pl.BlockSpec((None, 8, SUBLANES, SSM_SB), lambda b, j, c: (j, 0, 0, 0)),
                  pl.BlockSpec((1, LANES), lambda b, j, c: (0, j)),
                  pl.BlockSpec((None, None, 2, SSM_SB), lambda b, j, c: (b, j, 0, 0))],
        out_specs=[pl.BlockSpec((tc, LANES), lambda b, j, c: (b * n_chunks + c, j)), state_spec, state_spec],
        out_shape=[jax.ShapeDtypeStruct((n_tok, SSM_WIDTH), BF16), state_shape, state_shape],
        scratch_shapes=[pltpu.VMEM((tc, SSM_SB), F32), pltpu.VMEM((tc, SSM_SB), F32), pltpu.VMEM((2, SSM_SB), F32)],
        compiler_params=_params(("arbitrary", "arbitrary", "arbitrary")),
        name="ssm",
    )(proj, wb, wc, cst, d_skip[None, :], h0)
    return g, s_re.reshape(n_batch, SSM_GROUPS, SSM_STATE), s_im.reshape(n_batch, SSM_GROUPS, SSM_STATE)


def _ssm_step_kernel(u_ref, wb_ref, wc_ref, cst_ref, d_ref, h0_ref, g_ref, sre_ref, sim_ref, er_ref, ei_ref, *, seq):
    n_seq = h0_ref.shape[1]
    u = u_ref[...]
    e = _dot(u.astype(BF16), wb_ref[...])
    n_lt = SSM_SB // LANES
    y = d_ref[...] * u
    for lt in range(n_lt):
        sl = slice(lt * LANES, (lt + 1) * LANES)
        er_ref[...] = e[:, lt * LANES:(lt + 1) * LANES]
        ei_ref[...] = e[:, SSM_SB + lt * LANES:SSM_SB + (lt + 1) * LANES]
        lr, li = cst_ref[6, 0:1, sl], cst_ref[7, 0:1, sl]
        sr, si = h0_ref[0, :, sl], h0_ref[1, :, sl]
        for t in range(seq):
            rows = pl.ds(t, n_seq, stride=seq)
            sr, si = lr * sr - li * si + er_ref[rows, :], lr * si + li * sr + ei_ref[rows, :]
            er_ref[rows, :] = sr
            ei_ref[rows, :] = si
        y = y + (_dot(er_ref[...].astype(BF16), wc_ref[0, sl, :]) - _dot(ei_ref[...].astype(BF16), wc_ref[1, sl, :]))
        sre_ref[:, sl] = sr
        sim_ref[:, sl] = si
    g_ref[...] = _gelu_tanh(y).astype(BF16)


def ssm_step(proj, ssm_w, d_skip, h0, *, n_batch, seq, row0):
    wb, wc, cst = ssm_w
    n_tok = n_batch * seq
    assert row0 % n_tok == 0
    state_shape = jax.ShapeDtypeStruct((SSM_LB, n_batch, SSM_SB), F32)
    state_spec = pl.BlockSpec((None, n_batch, SSM_SB), lambda j: (j, 0, 0))
    g, s_re, s_im = pl.pallas_call(
        functools.partial(_ssm_step_kernel, seq=seq),
        grid=(SSM_LB,),
        in_specs=[pl.BlockSpec((n_tok, LANES), lambda j: (row0 // n_tok, j)),
                  pl.BlockSpec((None, LANES, 2 * SSM_SB), lambda j: (j, 0, 0)),
                  pl.BlockSpec((None, 2, SSM_SB, LANES), lambda j: (j, 0, 0, 0)),
                  pl.BlockSpec((None, 8, SUBLANES, SSM_SB), lambda j: (j, 0, 0, 0)),
                  pl.BlockSpec((1, LANES), lambda j: (0, j)),
                  pl.BlockSpec((None, 2, n_batch, SSM_SB), lambda j: (j, 0, 0, 0))],
        out_specs=[pl.BlockSpec((n_tok, LANES), lambda j: (0, j)), state_spec, state_spec],
        out_shape=[jax.ShapeDtypeStruct((n_tok, SSM_WIDTH), BF16), state_shape, state_shape],
        scratch_shapes=[pltpu.VMEM((n_tok, LANES), F32), pltpu.VMEM((n_tok, LANES), F32)],
        compiler_params=_params(("arbitrary",)),
        name="ssm_step",
    )(proj, wb, wc, cst, d_skip[None, :], h0)

    def per_seq(s):
        return s.transpose(1, 0, 2).reshape(n_batch, SSM_GROUPS, SSM_STATE)

    return g, per_seq(s_re), per_seq(s_im)


def _row_sum(x):
    return jnp.sum(x, axis=1, keepdims=True)


def _row_count(mask):
    return _row_sum(jnp.where(mask, 1, 0))


def _dsa_body(q_ref, qi_ref, wi_ref, k_ref, v_ref, ki_ref, o_ref, key_ref, bias_ref, *, q_pos_first, s_valid, n_sel):
    bq, n_keys = key_ref.shape
    col = lax.broadcasted_iota(I32, (bq, n_keys), 1)
    qpos = q_pos_first + lax.broadcasted_iota(I32, (bq, 1), 0)
    allowed = col < jnp.minimum((qpos // CHUNK + 1) * CHUNK, s_valid)

    ki = ki_ref[...]
    score = None
    for h in range(IDX_HEADS):
        d = _dot_nt(qi_ref[:, h * LANES:(h + 1) * LANES], ki)
        t = jnp.maximum(d, 0.0) * wi_ref[:, h:h + 1]
        score = t if score is None else score + t
    score = jnp.where(score == 0.0, 0.0, score)
    bits = pltpu.bitcast(score, I32)
    key = jnp.where(bits < 0, bits ^ np.int32(0x7FFFFFFF), bits)
    key_ref[...] = jnp.where(allowed, key, KEY_NEG_INF)

    def bisect(i, base):
        cand = base + lax.shift_left(np.int32(1), np.int32(31) - i)
        cnt = _row_count(key_ref[...] >= cand)
        return jnp.where(cnt >= n_sel, cand, base)

    thr = lax.fori_loop(0, 32, bisect, jnp.full((bq, 1), INT_MIN, I32))
    thr = jnp.maximum(thr, KEY_NEG_INF)

    key = key_ref[...]
    need = n_sel - _row_count(key > thr)
    n_eq = _row_count(key == thr)
    n_bits = int(n_keys - 1).bit_length()

    def tie_cut():
        def step(i, j0):
            cand = j0 + lax.shift_left(np.int32(1), np.int32(n_bits - 1) - i)
            cnt = _row_sum(jnp.where(key_ref[...] == thr, jnp.where(col < cand, 1, 0), 0))
            return jnp.where(cnt < need, cand, j0)
        return lax.fori_loop(0, n_bits, step, jnp.zeros((bq, 1), I32))

    split = jnp.max(jnp.where(n_eq > need, 1, 0)) > 0
    j_last = lax.cond(split, tie_cut, lambda: jnp.full((bq, 1), n_keys, I32))
    tie_bias = jnp.where(thr == KEY_NEG_INF, -jnp.inf, 0.0)
    bias_ref[...] = jnp.where(key > thr, 0.0,
                              jnp.where(key == thr, jnp.where(col <= j_last, tie_bias, -jnp.inf), -jnp.inf))

    c = np.float32(HEAD_DIM ** -0.5 * np.log2(np.e))
    for kv in range(N_KV_HEADS):
        kk = k_ref[:, kv * HEAD_DIM:(kv + 1) * HEAD_DIM]
        vv = v_ref[:, kv * V_AUG:(kv + 1) * V_AUG]
        for g in range(KV_GROUP):
            sl = slice((kv * KV_GROUP + g) * HEAD_DIM, (kv * KV_GROUP + g + 1) * HEAD_DIM)
            s = _dot_nt(q_ref[:, sl], kk) + bias_ref[...]
            m = jnp.max(s, axis=1, keepdims=True)
            p = jnp.exp2((s - m) * c)
            pv = _dot(p.astype(BF16), vv)
            o_ref[:, sl] = (pv[:, :HEAD_DIM] / pv[:, HEAD_DIM:HEAD_DIM + 1]).astype(BF16)


def _dsa_kernel(q_ref, qi_ref, wi_ref, k_ref, v_ref, ki_ref, o_ref, key_ref, bias_ref, *, q_pos0, s_valid, n_sel):
    bq = key_ref.shape[0]
    _dsa_body(q_ref, qi_ref, wi_ref, k_ref, v_ref, ki_ref, o_ref, key_ref, bias_ref,
              q_pos_first=q_pos0 + pl.program_id(1) * bq, s_valid=s_valid, n_sel=n_sel)


def dsa(q, qi, wi, k, v, ki, *, bq, q_blk0, n_qblk, n_keys, n_sel):
    n_batch, seq = q.shape[:2]

    def qspec(width):
        return pl.BlockSpec((None, bq, width), lambda b, i: (b, q_blk0 + i, 0))

    def kspec(width):
        return pl.BlockSpec((None, n_keys, width), lambda b, i: (b, 0, 0))

    return pl.pallas_call(
        functools.partial(_dsa_kernel, q_pos0=q_blk0 * bq, s_valid=seq, n_sel=n_sel),
        grid=(n_batch, n_qblk),
        in_specs=[qspec(ATTN_WIDTH), qspec(IDX_HEADS * LANES), qspec(LANES), kspec(KV_WIDTH), kspec(N_KV_HEADS * V_AUG),
                  kspec(LANES)],
        out_specs=pl.BlockSpec((None, bq, ATTN_WIDTH), lambda b, i: (b, i, 0)),
        out_shape=jax.ShapeDtypeStruct((n_batch, n_qblk * bq, ATTN_WIDTH), BF16),
        scratch_shapes=[pltpu.VMEM((bq, n_keys), I32), pltpu.VMEM((bq, n_keys), F32)],
        compiler_params=_params(("arbitrary", "arbitrary")),
        name="dsa",
    )(q, qi, wi, k, v, ki)


def _dsa_step_kernel(q_ref, qi_ref, wi_ref, ck_ref, cv_ref, cki_ref, nk_ref, nv_ref, nki_ref, o_ref,
                     k_buf, v_buf, ki_buf, key_ref, bias_ref, *, past, n_sel):
    ts = nk_ref.shape[0]
    n_keys = k_buf.shape[0]
    k_buf[0:past, :] = ck_ref[...].astype(BF16)
    _store_v_aug(v_buf, 0, cv_ref[...])
    for buf, new in ((k_buf, nk_ref), (v_buf, nv_ref)):
        buf[past:past + ts, :] = new[...]
        buf[past + ts:n_keys, :] = jnp.zeros((n_keys - past - ts, buf.shape[1]), BF16)
    ki_buf[0:past, 0:IDX_DIM] = cki_ref[...].astype(BF16)
    ki_buf[0:past, IDX_DIM:LANES] = jnp.zeros((past, LANES - IDX_DIM), BF16)
    ki_buf[past:past + ts, :] = nki_ref[...]
    ki_buf[past + ts:n_keys, :] = jnp.zeros((n_keys - past - ts, LANES), BF16)
    _dsa_body(q_ref, qi_ref, wi_ref, k_buf, v_buf, ki_buf, o_ref, key_ref, bias_ref,
              q_pos_first=past, s_valid=past + ts, n_sel=n_sel)


def dsa_step(q, qi, wi, cache_k, cache_v, cache_ki, k_new, v_new, ki_new, *, n_sel):
    n_batch, ts = q.shape[:2]
    past = cache_k.shape[1]
    n_keys = -(-(past + ts) // LANES) * LANES

    def spec(rows, width):
        return pl.BlockSpec((None, rows, width), lambda b: (b, 0, 0))

    return pl.pallas_call(
        functools.partial(_dsa_step_kernel, past=past, n_sel=n_sel),
        grid=(n_batch,),
        in_specs=[spec(ts, ATTN_WIDTH), spec(ts, IDX_HEADS * LANES), spec(ts, LANES),
                  spec(past, KV_WIDTH), spec(past, KV_WIDTH), spec(past, IDX_DIM),
                  spec(ts, KV_WIDTH), spec(ts, N_KV_HEADS * V_AUG), spec(ts, LANES)],
        out_specs=spec(ts, ATTN_WIDTH),
        out_shape=jax.ShapeDtypeStruct((n_batch, ts, ATTN_WIDTH), BF16),
        scratch_shapes=[pltpu.VMEM((n_keys, KV_WIDTH), BF16), pltpu.VMEM((n_keys, N_KV_HEADS * V_AUG), BF16),
                        pltpu.VMEM((n_keys, LANES), BF16),
                        pltpu.VMEM((ts, n_keys), I32), pltpu.VMEM((ts, n_keys), F32)],
        compiler_params=_params(("arbitrary",)),
        name="dsa_step",
    )(q, qi, wi, cache_k, cache_v, cache_ki, k_new, v_new, ki_new)


def _merge_kernel(g_ref, a_ref, ga_ref, gb_ref, wv_ref, wg_ref, wb_ref, o_ref):
    g = g_ref[...]
    branch_a = _dot(g, wv_ref[...]) * jax.nn.sigmoid(_dot(g, wg_ref[...]))
    branch_b = _dot(a_ref[...], wb_ref[...])
    merged = jax.nn.sigmoid(ga_ref[...]) * branch_a + jax.nn.sigmoid(gb_ref[...]) * branch_b
    o_ref[...] = merged.astype(BF16)


def merge(g, attn, proj, w_val, w_gate, w_branch, *, tm=1024, tn=512):
    n_tok = g.shape[0]
    nj = D_MODEL // tn

    def wspec():
        return pl.BlockSpec((SSM_WIDTH, tn), lambda i, j: (0, j))

    return pl.pallas_call(
        _merge_kernel,
        grid=(n_tok // tm, nj),
        in_specs=[pl.BlockSpec((tm, SSM_WIDTH), lambda i, j: (i, 0)),
                  pl.BlockSpec((tm, ATTN_WIDTH), lambda i, j: (i, 0)),
                  pl.BlockSpec((tm, tn), lambda i, j: (i, COL_GA // tn + j)),
                  pl.BlockSpec((tm, tn), lambda i, j: (i, COL_GB // tn + j)),
                  wspec(), wspec(), wspec()],
        out_specs=pl.BlockSpec((tm, tn), lambda i, j: (i, j)),
        out_shape=jax.ShapeDtypeStruct((n_tok, D_MODEL), BF16),
        compiler_params=_params(("arbitrary", "arbitrary")),
        name="merge",
    )(g, attn, proj, proj, w_val, w_gate, w_branch)


ROUTER_COLS = N_EXPERT_GROUPS + N_EXPERTS
MOE_TM = 256


def _first_lane_of_max(x, lane_f):
    m = jnp.max(x, axis=1, keepdims=True)
    return m, jnp.min(jnp.where(x == m, lane_f, float(LANES)), axis=1, keepdims=True)


def _out_proj_kernel(x_ref, m_ref, wo_ref, gn_ref, wrh_ref, wrl_ref, br_ref,
                     h_ref, hn_ref, ri_ref, rw_ref, cnt_ref, carry_ref):
    @pl.when(pl.program_id(0) == 0)
    def _():
        carry_ref[...] = jnp.zeros_like(carry_ref)

    h = x_ref[...] + _dot(m_ref[...], wo_ref[...])
    h_ref[...] = h
    ms = jnp.mean(h * h, axis=-1, keepdims=True)
    hn = h * lax.rsqrt(ms + EPS) * gn_ref[...]
    hn_ref[...] = hn
    hh, hl = _split_bf16(hn)
    wrh = wrh_ref[...]
    lg = _dot(hh, wrh) + _dot(hl, wrh) + _dot(hh, wrl_ref[...]) + br_ref[...]

    tm = lg.shape[0]
    lane = lax.broadcasted_iota(I32, lg.shape, 1)
    lane_f = lane.astype(F32)
    ninf = -jnp.inf
    gl = jnp.where(lane < N_EXPERT_GROUPS, lg, ninf)
    gmax, gsel = _first_lane_of_max(gl, lane_f)
    g_w = 1.0 / jnp.sum(jnp.exp(gl - gmax), axis=1, keepdims=True)
    lo = N_EXPERT_GROUPS + EXPERTS_PER_GROUP * gsel
    el = jnp.where(lane_f >= lo, jnp.where(lane_f < lo + EXPERTS_PER_GROUP, lg, ninf), ninf)
    v1, i1 = _first_lane_of_max(el, lane_f)
    el2 = jnp.where(lane_f == i1, ninf, el)
    v2, i2 = _first_lane_of_max(el2, lane_f)
    t = jnp.exp(v2 - v1)
    s1 = 1.0 / (1.0 + t)
    w1 = s1 * g_w
    w2 = (t * s1) * g_w

    m1 = jnp.where(lane_f == i1, 1.0, 0.0)
    m2 = jnp.where(lane_f == i2, 1.0, 0.0)
    both = m1 + m2
    tri = jnp.where(lax.broadcasted_iota(I32, (tm, tm), 0) > lax.broadcasted_iota(I32, (tm, tm), 1), 1.0, 0.0)
    before = _dot(tri.astype(BF16), both.astype(BF16)) + carry_ref[...]
    r1 = jnp.sum(before * m1, axis=1, keepdims=True)
    r2 = jnp.sum(before * m2, axis=1, keepdims=True)
    carry_ref[...] = carry_ref[...] + jnp.sum(both, axis=0, keepdims=True)
    cnt_ref[...] = carry_ref[...]
    e1 = i1 - float(N_EXPERT_GROUPS)
    e2 = i2 - float(N_EXPERT_GROUPS)
    fields = jnp.where(lane == 0, e1, jnp.where(lane == 1, e2, jnp.where(lane == 2, r1, jnp.where(lane == 3, r2, 0.0))))
    ri_ref[...] = fields.T[0:SUBLANES, :].astype(I32)
    rw_ref[...] = jnp.where(lane == 0, w1, jnp.where(lane == 1, w2, 0.0))


def out_proj(x, merged, w_out, ffn_gain, w_router, b_router, *, tm=256):
    n_tok = x.shape[0]
    wr = jnp.concatenate([w_router, jnp.zeros((D_MODEL, LANES - ROUTER_COLS), F32)], axis=1)
    wr_hi = wr.astype(BF16)
    wr_lo = (wr - wr_hi.astype(F32)).astype(BF16)
    br = jnp.concatenate([b_router, jnp.zeros((LANES - ROUTER_COLS,), F32)])[None, :]

    def row(width):
        return pl.BlockSpec((tm, width), lambda i: (i, 0))

    def const(shape):
        return pl.BlockSpec(shape, lambda i: (0, 0), pipeline_mode=pl.Buffered(1))

    return pl.pallas_call(
        _out_proj_kernel,
        grid=(n_tok // tm,),
        in_specs=[row(D_MODEL), row(D_MODEL), const((D_MODEL, D_MODEL)), const((1, D_MODEL)),
                  const((D_MODEL, LANES)), const((D_MODEL, LANES)), const((1, LANES))],
        out_specs=[row(D_MODEL), row(D_MODEL), pl.BlockSpec((SUBLANES, tm), lambda i: (0, i)), row(LANES),
                   pl.BlockSpec((1, LANES), lambda i: (0, 0))],
        out_shape=[jax.ShapeDtypeStruct((n_tok, D_MODEL), F32), jax.ShapeDtypeStruct((n_tok, D_MODEL), F32),
                   jax.ShapeDtypeStruct((SUBLANES, n_tok), I32), jax.ShapeDtypeStruct((n_tok, LANES), F32),
                   jax.ShapeDtypeStruct((1, LANES), F32)],
        scratch_shapes=[pltpu.VMEM((1, LANES), F32)],
        compiler_params=_params(("arbitrary",)),
        name="out_proj",
    )(x, merged, w_out, ffn_gain[None, :], wr_hi, wr_lo, br)


def _block_layout(counts):
    padded = (counts + MOE_TM - 1) // MOE_TM * MOE_TM
    pad_end = jnp.cumsum(padded).astype(I32)
    pad_start = pad_end - padded
    n_used = pad_end[-1] // MOE_TM
    return pad_start, pad_end, n_used


def _moe_rows(n_tok):
    return -(-(n_tok * TOP_K + N_EXPERTS * (MOE_TM - 1)) // MOE_TM) * MOE_TM


DISPATCH_TM = 512


def _wait_rows(src_hbm, dst, sem, n_rows):
    pltpu.make_async_copy(src_hbm.at[pl.ds(0, n_rows)], dst, sem).wait()


def _dispatch_kernel(d0_ref, d1_ref, pe_ref, cnt_ref, nu_ref, hn_ref, xs_hbm, zbuf, sem, semz, *, n_blocks):
    i = pl.program_id(0)

    def zero_block(row0):
        return pltpu.make_async_copy(zbuf, xs_hbm.at[pl.ds(pl.multiple_of(row0, MOE_TM), MOE_TM)], semz)

    @pl.when(i == 0)
    def _():
        zbuf[...] = jnp.zeros_like(zbuf)
        for start in (True, False):
            for e in range(N_EXPERTS):
                @pl.when(cnt_ref[e] > 0)
                def _():
                    cp = zero_block(pe_ref[e] - MOE_TM)
                    cp.start() if start else cp.wait()

            def tail(b, c):
                cp = zero_block(b * MOE_TM)
                cp.start() if start else cp.wait()
                return c
            lax.fori_loop(nu_ref[0], n_blocks, tail, 0)

    base = i * DISPATCH_TM

    def body(r, c):
        src = hn_ref.at[pl.ds(r, 1)]
        pltpu.make_async_copy(src, xs_hbm.at[pl.ds(d0_ref[base + r], 1)], sem).start()
        pltpu.make_async_copy(src, xs_hbm.at[pl.ds(d1_ref[base + r], 1)], sem).start()
        return c
    lax.fori_loop(0, DISPATCH_TM, body, 0, unroll=8)
    for _ in range(TOP_K):
        pltpu.make_async_copy(hn_ref, xs_hbm.at[pl.ds(0, DISPATCH_TM)], sem).wait()


def dispatch(hn, dest0, dest1, pad_end, counts, n_used):
    n_tok = hn.shape[0]
    rows = _moe_rows(n_tok)
    grid_spec = pltpu.PrefetchScalarGridSpec(
        num_scalar_prefetch=5,
        grid=(n_tok // DISPATCH_TM,),
        in_specs=[pl.BlockSpec((DISPATCH_TM, D_MODEL), lambda i, *_: (i, 0))],
        out_specs=pl.BlockSpec(memory_space=pl.ANY),
        scratch_shapes=[pltpu.VMEM((MOE_TM, D_MODEL), F32), pltpu.SemaphoreType.DMA(()), pltpu.SemaphoreType.DMA(())],
    )
    return pl.pallas_call(
        functools.partial(_dispatch_kernel, n_blocks=rows // MOE_TM),
        grid_spec=grid_spec,
        out_shape=jax.ShapeDtypeStruct((rows, D_MODEL), F32),
        compiler_params=_params(("arbitrary",)),
        name="dispatch",
    )(dest0, dest1, pad_end, counts, n_used, hn)


def _moe_kernel(blk_e_ref, nu_ref, xs_ref, wg_ref, wu_ref, wd_ref, ys_ref):
    i = pl.program_id(0)

    @pl.when(i < nu_ref[0])
    def _():
        x = xs_ref[...].astype(BF16)
        hg = _dot(x, wg_ref[...])
        hu = _dot(x, wu_ref[...])
        hmid = (jax.nn.silu(hg) * hu).astype(BF16)
        ys_ref[...] = _dot(hmid, wd_ref[...])

    @pl.when(i >= nu_ref[0])
    def _():
        ys_ref[...] = jnp.zeros_like(ys_ref)


def moe(xs, blk_e, n_used, w_gate, w_up, w_down):
    rows = xs.shape[0]
    grid_spec = pltpu.PrefetchScalarGridSpec(
        num_scalar_prefetch=2,
        grid=(rows // MOE_TM,),
        in_specs=[pl.BlockSpec((MOE_TM, D_MODEL), lambda i, be, nu: (jnp.minimum(i, nu[0] - 1), 0)),
                  pl.BlockSpec((None, D_MODEL, EXPERT_FF), lambda i, be, nu: (be[i], 0, 0)),
                  pl.BlockSpec((None, D_MODEL, EXPERT_FF), lambda i, be, nu: (be[i], 0, 0)),
                  pl.BlockSpec((None, EXPERT_FF, D_MODEL), lambda i, be, nu: (be[i], 0, 0))],
        out_specs=pl.BlockSpec((MOE_TM, D_MODEL), lambda i, be, nu: (i, 0)),
    )
    return pl.pallas_call(
        _moe_kernel,
        grid_spec=grid_spec,
        out_shape=jax.ShapeDtypeStruct((rows, D_MODEL), F32),
        compiler_params=_params(("arbitrary",)),
        name="moe",
    )(blk_e, n_used, xs, w_gate, w_up, w_down)


def _gather_rows(idx_ref, idx0, src_hbm, dst, sem, n_rows):
    def body(r, carry):
        t = idx_ref[idx0 + r]
        pltpu.make_async_copy(src_hbm.at[pl.ds(t, 1)], dst.at[pl.ds(r, 1)], sem).start()
        return carry
    lax.fori_loop(0, n_rows, body, 0, unroll=8)


def _combine_kernel(r0_ref, r1_ref, ys_hbm, h_ref, w_ref, o_ref, buf, sem, *, tm, tok0):
    i = pl.program_id(0)

    def issue(block, slot):
        _gather_rows(r0_ref, tok0 + block * tm, ys_hbm, buf.at[slot, 0], sem.at[slot], tm)
        _gather_rows(r1_ref, tok0 + block * tm, ys_hbm, buf.at[slot, 1], sem.at[slot], tm)

    @pl.when(i == 0)
    def _():
        issue(0, 0)

    @pl.when(i + 1 < pl.num_programs(0))
    def _():
        issue(i + 1, (i + 1) % 2)

    slot = i % 2
    _wait_rows(ys_hbm, buf.at[slot, 0], sem.at[slot], tm)
    _wait_rows(ys_hbm, buf.at[slot, 1], sem.at[slot], tm)
    w = w_ref[...]
    o_ref[...] = h_ref[...] + (buf[slot, 0] * w[:, 0:1] + buf[slot, 1] * w[:, 1:2])


def combine(ys, h, route_w, rows0, rows1, *, tok0, n_tok, tm=256):
    blk0 = tok0 // tm
    grid_spec = pltpu.PrefetchScalarGridSpec(
        num_scalar_prefetch=2,
        grid=(n_tok // tm,),
        in_specs=[pl.BlockSpec(memory_space=pl.ANY),
                  pl.BlockSpec((tm, D_MODEL), lambda i, a, b: (blk0 + i, 0)),
                  pl.BlockSpec((tm, LANES), lambda i, a, b: (blk0 + i, 0))],
        out_specs=pl.BlockSpec((tm, D_MODEL), lambda i, a, b: (i, 0)),
        scratch_shapes=[pltpu.VMEM((2, 2, tm, D_MODEL), F32), pltpu.SemaphoreType.DMA((2,))],
    )
    return pl.pallas_call(
        functools.partial(_combine_kernel, tm=tm, tok0=tok0),
        grid_spec=grid_spec,
        out_shape=jax.ShapeDtypeStruct((n_tok, D_MODEL), F32),
        compiler_params=_params(("arbitrary",)),
        name="combine",
    )(rows0, rows1, ys, h, route_w)


def _regroup_w_in(w_in):
    sizes = (SSM_WIDTH, ATTN_WIDTH, KV_WIDTH, KV_WIDTH, IDX_HEADS * IDX_DIM, IDX_DIM, IDX_HEADS, D_MODEL, D_MODEL)
    u, q, k, v, qi, ki, wi, ga, gb = jnp.split(w_in, np.cumsum(sizes)[:-1].tolist(), axis=1)
    pad = jnp.zeros((D_MODEL, PROJ_COLS - COL_KIWI - IDX_DIM - IDX_HEADS), F32)
    return jnp.concatenate([u, q, ga, gb, k, v, qi, ki, wi, pad], axis=1).astype(BF16)


def _layer(x_p, x_s, cache_k, cache_v, cache_ki, h0_re, h0_im, p):
    bp, tp, _ = x_p.shape
    bs, ts, _ = x_s.shape
    past = cache_k.shape[1]
    n_p, n_s = bp * tp, bs * ts
    n_tok = n_p + n_s

    x = jnp.concatenate([x_p.reshape(n_p, D_MODEL), x_s.reshape(n_s, D_MODEL)], axis=0)
    table_pos = jnp.concatenate([jnp.arange(tp, dtype=I32), jnp.tile(past + jnp.arange(ts, dtype=I32), QK_TM // ts)])
    seq_tiles, prompt_tiles = tp // QK_TM, n_p // QK_TM

    def table_block(i):
        return jnp.where(i < prompt_tiles, i % seq_tiles, seq_tiles)

    proj = in_proj(x, p['norm_mix_g'][None, :], _regroup_w_in(p['w_in']))
    q_b, k_f, k_b, v_f, v_b, qi_b, ki_f, ki_b, wi = qk_post(proj, table_pos, table_block, p['q_norm_g'], p['k_norm_g'],
                                                               p['idx_k_norm_g'])

    ssm_w = _ssm_weights(p['ssm_A_re'], p['ssm_A_im'], p['ssm_log_dt'], p['ssm_B_re'], p['ssm_B_im'],
                         p['ssm_C_re'], p['ssm_C_im'])
    zeros_h0 = jnp.zeros((bp, SSM_LB, 2, SSM_SB), F32)
    g_p, sre_p, sim_p = ssm(proj, ssm_w, p['ssm_D'], zeros_h0, n_batch=bp, seq=tp, tc=512, row0=0)
    h0 = jnp.stack([h0_re.reshape(bs, SSM_LB, SSM_SB), h0_im.reshape(bs, SSM_LB, SSM_SB)]).transpose(2, 0, 1, 3)
    g_s, sre_s, sim_s = ssm_step(proj, ssm_w, p['ssm_D'], h0, n_batch=bs, seq=ts, row0=n_p)
    g = jnp.concatenate([g_p, g_s], axis=0)

    def seqs(a, n0, n1, b, t):
        return a[n0:n1].reshape(b, t, a.shape[-1])

    bq = 128
    qp, qip, wip = seqs(q_b, 0, n_p, bp, tp), seqs(qi_b, 0, n_p, bp, tp), seqs(wi, 0, n_p, bp, tp)
    kp, vp, kip = seqs(k_b, 0, n_p, bp, tp), seqs(v_b, 0, n_p, bp, tp), seqs(ki_b, 0, n_p, bp, tp)
    n_buckets = min(8, tp // bq)
    per = tp // bq // n_buckets
    attn_p = jnp.concatenate(
        [dsa(qp, qip, wip, kp, vp, kip, bq=bq, q_blk0=n * per, n_qblk=per, n_keys=(n + 1) * per * bq,
             n_sel=min(IDX_TOPK, tp // 4)) for n in range(n_buckets)], axis=1)
    attn_s = dsa_step(seqs(q_b, n_p, n_tok, bs, ts), seqs(qi_b, n_p, n_tok, bs, ts), seqs(wi, n_p, n_tok, bs, ts),
                      cache_k.reshape(bs, past, KV_WIDTH), cache_v.reshape(bs, past, KV_WIDTH), cache_ki,
                      seqs(k_b, n_p, n_tok, bs, ts), seqs(v_b, n_p, n_tok, bs, ts), seqs(ki_b, n_p, n_tok, bs, ts),
                      n_sel=min(IDX_TOPK, (past + ts) // 4))
    attn = jnp.concatenate([attn_p.reshape(n_p, ATTN_WIDTH), attn_s.reshape(n_s, ATTN_WIDTH)], axis=0)

    merged = merge(g, attn, proj, p['w_glu_val'].astype(BF16), p['w_glu_gate'].astype(BF16),
                   p['w_attn_branch'].astype(BF16))
    w_router = jnp.concatenate([p['w_router_group'], p['w_router_expert']], axis=1)
    b_router = jnp.concatenate([p['b_router_group'], p['b_router_expert']])
    h, hn, route_i, route_w, cnt = out_proj(x, merged, p['w_out'].astype(BF16), p['norm_ffn_g'], w_router, b_router)

    counts = cnt[0, N_EXPERT_GROUPS:ROUTER_COLS].astype(I32)
    pad_start, pad_end, n_used = _block_layout(counts)
    dest0 = pad_start[route_i[0]] + route_i[2]
    dest1 = pad_start[route_i[1]] + route_i[3]
    n_blocks = _moe_rows(n_tok) // MOE_TM
    blk = jnp.minimum(jnp.arange(n_blocks, dtype=I32), n_used - 1)
    blk_e = jnp.minimum(jnp.searchsorted(pad_end, blk * MOE_TM, side='right'), N_EXPERTS - 1).astype(I32)
    n_used = n_used.reshape(1)

    xs = dispatch(hn, dest0, dest1, pad_end, counts, n_used)
    ys = moe(xs, blk_e, n_used, p['w_exp_gate'].astype(BF16), p['w_exp_up'].astype(BF16), p['w_exp_down'].astype(BF16))
    y_p = combine(ys, h, route_w, dest0, dest1, tok0=0, n_tok=n_p).reshape(bp, tp, D_MODEL)
    y_s = combine(ys, h, route_w, dest0, dest1, tok0=n_p, n_tok=n_s).reshape(bs, ts, D_MODEL)

    def heads(a, n0, n1, b, t):
        return a[n0:n1].reshape(b, t, N_KV_HEADS, HEAD_DIM)

    new_p = (heads(k_f, 0, n_p, bp, tp), heads(v_f, 0, n_p, bp, tp), ki_f[:n_p].reshape(bp, tp, IDX_DIM), sre_p, sim_p)
    new_s = (heads(k_f, n_p, n_tok, bs, ts), heads(v_f, n_p, n_tok, bs, ts), ki_f[n_p:].reshape(bs, ts, IDX_DIM),
             sre_s, sim_s)
    return y_p, y_s, new_p, new_s


def kernel(x_prompt, x_sample, cache_k, cache_v, cache_idx_k, state_ssm_re, state_ssm_im, norm_mix_g, w_in, q_norm_g, k_norm_g, idx_k_norm_g, ssm_A_re, ssm_A_im, ssm_log_dt, ssm_B_re, ssm_B_im, ssm_C_re, ssm_C_im, ssm_D, w_glu_val, w_glu_gate, w_attn_branch, w_out, norm_ffn_g, w_router_group, b_router_group, w_router_expert, b_router_expert, w_exp_gate, w_exp_up, w_exp_down):
    depth = w_in.shape[0]
    assert depth == 1, "prompt and sample tokens are batched through one layer"
    names = ('norm_mix_g', 'w_in', 'q_norm_g', 'k_norm_g', 'idx_k_norm_g', 'ssm_A_re', 'ssm_A_im', 'ssm_log_dt',
             'ssm_B_re', 'ssm_B_im', 'ssm_C_re', 'ssm_C_im', 'ssm_D', 'w_glu_val', 'w_glu_gate', 'w_attn_branch',
             'w_out', 'norm_ffn_g', 'w_router_group', 'b_router_group', 'w_router_expert', 'b_router_expert',
             'w_exp_gate', 'w_exp_up', 'w_exp_down')
    vals = (norm_mix_g, w_in, q_norm_g, k_norm_g, idx_k_norm_g, ssm_A_re, ssm_A_im, ssm_log_dt, ssm_B_re, ssm_B_im,
            ssm_C_re, ssm_C_im, ssm_D, w_glu_val, w_glu_gate, w_attn_branch, w_out, norm_ffn_g, w_router_group,
            b_router_group, w_router_expert, b_router_expert, w_exp_gate, w_exp_up, w_exp_down)
    p = {n: v[0] for n, v in zip(names, vals)}
    y_p, y_s, new_p, new_s = _layer(x_prompt, x_sample, cache_k[0], cache_v[0], cache_idx_k[0],
                                    state_ssm_re[0], state_ssm_im[0], p)
    st_p = tuple(a[None] for a in new_p)
    st_s = tuple(a[None] for a in new_s)
    return (y_p, y_s) + st_p + st_s
```

```python
import functools

import numpy as np
import jax
import jax.numpy as jnp
from jax import lax
from jax.experimental import pallas as pl
from jax.experimental.pallas import tpu as pltpu

F32 = jnp.float32
BF16 = jnp.bfloat16
I32 = jnp.int32

D_MODEL = 2048
CHUNK = 64
SSM_WIDTH = 1024
SSM_GROUP = 16
SSM_GROUPS = 64
SSM_STATE = 64
ATTN_WIDTH = 1024
HEAD_DIM = 128
N_HEADS = 8
N_KV_HEADS = 2
KV_GROUP = 4
IDX_HEADS = 8
IDX_DIM = 64
IDX_TOPK = 256
ROPE_THETA = 500000.0
N_EXPERT_GROUPS = 4
EXPERTS_PER_GROUP = 8
N_EXPERTS = 32
TOP_K = 2
EXPERT_FF = 1024
EPS = 1e-6

LANES = 128
SUBLANES = 8
VMEM_LIMIT = 56 * 1024 * 1024

COL_U, COL_Q, COL_GA, COL_GB, COL_K, COL_V, COL_QI, COL_KIWI = 0, 1024, 2048, 4096, 6144, 6400, 6656, 7168
PROJ_COLS = 7296
PROJ_TN = 2432
KV_WIDTH = N_KV_HEADS * HEAD_DIM

SSM_LB = SSM_WIDTH // LANES
SSM_SB = 8 * SSM_STATE

INT_MIN = np.int32(-2 ** 31)
KEY_NEG_INF = np.int32(np.array([0xFF800000], np.uint32).view(np.int32)[0] ^ 0x7FFFFFFF)


def _params(sem, vmem=VMEM_LIMIT):
    return pltpu.CompilerParams(dimension_semantics=sem, vmem_limit_bytes=vmem)


def _dot(a, b):
    return jnp.dot(a, b, preferred_element_type=F32)


def _dot_nt(a, b):
    return lax.dot_general(a, b, (((1,), (1,)), ((), ())), preferred_element_type=F32)


def _split_bf16(x):
    hi = x.astype(BF16)
    lo = (x - hi.astype(F32)).astype(BF16)
    return hi, lo


def _in_proj_kernel(x_ref, g_ref, w_ref, o_ref, xn_ref):
    @pl.when(pl.program_id(1) == 0)
    def _():
        x = x_ref[...]
        ms = jnp.mean(x * x, axis=-1, keepdims=True)
        xn_ref[...] = (x * lax.rsqrt(ms + EPS) * g_ref[...]).astype(BF16)

    o_ref[...] = _dot(xn_ref[...], w_ref[...])


def in_proj(x, gain, w_bf16, *, tm=512):
    n_tok = x.shape[0]
    return pl.pallas_call(
        _in_proj_kernel,
        grid=(n_tok // tm, PROJ_COLS // PROJ_TN),
        in_specs=[pl.BlockSpec((tm, D_MODEL), lambda i, j: (i, 0)),
                  pl.BlockSpec((1, D_MODEL), lambda i, j: (0, 0)),
                  pl.BlockSpec((D_MODEL, PROJ_TN), lambda i, j: (0, j))],
        out_specs=pl.BlockSpec((tm, PROJ_TN), lambda i, j: (i, j)),
        out_shape=jax.ShapeDtypeStruct((n_tok, PROJ_COLS), F32),
        scratch_shapes=[pltpu.VMEM((tm, D_MODEL), BF16)],
        compiler_params=_params(("arbitrary", "arbitrary")),
        name="in_proj",
    )(x, gain, w_bf16)


def _rope(x, c, s_lo, s_hi, half):
    n = x.shape[-1]
    return x * c + pltpu.roll(x, n - half, 1) * s_lo + pltpu.roll(x, half, 1) * s_hi


def _head_norm(x, g):
    ms = jnp.mean(x * x, axis=-1, keepdims=True)
    return x * lax.rsqrt(ms + EPS) * g


V_AUG = 2 * HEAD_DIM


def _store_v_aug(dst_ref, row0, v):
    n = v.shape[0]
    one_col = jnp.where(lax.broadcasted_iota(I32, (n, HEAD_DIM), 1) == 0, 1.0, 0.0).astype(BF16)
    for h in range(N_KV_HEADS):
        dst_ref[row0:row0 + n, h * V_AUG:h * V_AUG + HEAD_DIM] = v[:, h * HEAD_DIM:(h + 1) * HEAD_DIM].astype(BF16)
        dst_ref[row0:row0 + n, h * V_AUG + HEAD_DIM:(h + 1) * V_AUG] = one_col


def _qk_post_kernel(q_ref, k_ref, v_ref, qi_ref, kw_ref, c128_ref, sl128_ref, sh128_ref,
                    c64_ref, sl64_ref, sh64_ref, qg_ref, kg_ref, ig_ref,
                    qo_ref, kf_ref, kb_ref, vf_ref, vb_ref, qio_ref, kif_ref, kib_ref, wo_ref):
    c128, sl128, sh128 = c128_ref[...], sl128_ref[...], sh128_ref[...]
    c64, sl64, sh64 = c64_ref[...], sl64_ref[...], sh64_ref[...]
    half128 = HEAD_DIM // 8
    half64 = IDX_DIM // 8
    for h in range(N_HEADS):
        sl = slice(h * LANES, (h + 1) * LANES)
        qo_ref[:, sl] = _rope(_head_norm(q_ref[:, sl], qg_ref[...]), c128, sl128, sh128, half128).astype(BF16)
    for h in range(N_KV_HEADS):
        sl = slice(h * LANES, (h + 1) * LANES)
        kk = _rope(_head_norm(k_ref[:, sl], kg_ref[...]), c128, sl128, sh128, half128)
        kf_ref[:, sl] = kk
        kb_ref[:, sl] = kk.astype(BF16)
    v = v_ref[...]
    vf_ref[...] = v
    _store_v_aug(vb_ref, 0, v)
    lane = lax.broadcasted_iota(I32, c64.shape, 1)
    low = lane < IDX_DIM
    for p in range(IDX_HEADS // 2):
        x = _rope(qi_ref[:, p * LANES:(p + 1) * LANES], c64, sl64, sh64, half64)
        qio_ref[:, (2 * p) * LANES:(2 * p + 1) * LANES] = jnp.where(low, x, 0.0).astype(BF16)
        qio_ref[:, (2 * p + 1) * LANES:(2 * p + 2) * LANES] = jnp.where(low, pltpu.roll(x, IDX_DIM, 1), 0.0).astype(BF16)
    kw = kw_ref[...]
    ms = jnp.sum(jnp.where(low, kw * kw, 0.0), axis=-1, keepdims=True) * (1.0 / IDX_DIM)
    ki = _rope(kw * lax.rsqrt(ms + EPS) * ig_ref[...], c64, sl64, sh64, half64)
    kif_ref[...] = ki[:, :IDX_DIM]
    kib_ref[...] = jnp.where(low, ki, 0.0).astype(BF16)
    wo_ref[...] = (pltpu.roll(kw, IDX_DIM, 1) * IDX_HEADS ** -0.5) * IDX_DIM ** -0.5


def _rope_tables(pos, head_dim):
    r = head_dim // 4
    half = r // 2
    inv = ROPE_THETA ** (-jnp.arange(half, dtype=F32) * 2.0 / r)
    ang = pos.astype(F32)[:, None] * inv[None, :]
    cos, sin = jnp.cos(ang), jnp.sin(ang)
    n = pos.shape[0]
    zh = jnp.zeros((n, half), F32)
    rest = head_dim - r
    c = jnp.concatenate([cos, cos, jnp.ones((n, rest), F32)], axis=-1)
    s_lo = jnp.concatenate([-sin, zh, jnp.zeros((n, rest), F32)], axis=-1)
    s_hi = jnp.concatenate([zh, sin, jnp.zeros((n, rest), F32)], axis=-1)
    rep = LANES // head_dim
    return tuple(jnp.tile(t, (1, rep)) for t in (c, s_lo, s_hi))


QK_TM = 512


def qk_post(proj, table_pos, table_block, q_gain, k_gain, ik_gain):
    tm = QK_TM
    n_tok = proj.shape[0]
    t128 = _rope_tables(table_pos, HEAD_DIM)
    t64 = _rope_tables(table_pos, IDX_DIM)
    ik_gain128 = jnp.concatenate([ik_gain, jnp.zeros((LANES - IDX_DIM,), F32)])[None, :]

    def col(width, start):
        return pl.BlockSpec((tm, width), lambda i: (i, start // width))

    def row(width):
        return pl.BlockSpec((tm, width), lambda i: (i, 0))

    table = pl.BlockSpec((tm, LANES), lambda i: (table_block(i), 0))
    gain = pl.BlockSpec((1, LANES), lambda i: (0, 0))
    return pl.pallas_call(
        _qk_post_kernel,
        grid=(n_tok // tm,),
        in_specs=[col(ATTN_WIDTH, COL_Q), col(KV_WIDTH, COL_K), col(KV_WIDTH, COL_V), col(IDX_HEADS * IDX_DIM, COL_QI),
                  col(LANES, COL_KIWI)] + [table] * 6 + [gain] * 3,
        out_specs=[row(ATTN_WIDTH), row(KV_WIDTH), row(KV_WIDTH), row(KV_WIDTH), row(N_KV_HEADS * V_AUG), row(IDX_HEADS * LANES),
                   row(IDX_DIM), row(LANES), row(LANES)],
        out_shape=[jax.ShapeDtypeStruct((n_tok, ATTN_WIDTH), BF16),
                   jax.ShapeDtypeStruct((n_tok, KV_WIDTH), F32), jax.ShapeDtypeStruct((n_tok, KV_WIDTH), BF16),
                   jax.ShapeDtypeStruct((n_tok, KV_WIDTH), F32), jax.ShapeDtypeStruct((n_tok, N_KV_HEADS * V_AUG), BF16),
                   jax.ShapeDtypeStruct((n_tok, IDX_HEADS * LANES), BF16),
                   jax.ShapeDtypeStruct((n_tok, IDX_DIM), F32), jax.ShapeDtypeStruct((n_tok, LANES), BF16),
                   jax.ShapeDtypeStruct((n_tok, LANES), F32)],
        compiler_params=_params(("arbitrary",)),
        name="qk_post",
    )(proj, proj, proj, proj, proj, *t128, *t64, q_gain[None, :], k_gain[None, :], ik_gain128)


def _gelu_tanh(x):
    return 0.5 * x * (1.0 + jnp.tanh(np.float32(np.sqrt(2.0 / np.pi)) * (x + 0.044715 * (x * x * x))))


def _ssm_kernel(u_ref, wb_ref, wc_ref, cst_ref, d_ref, h0_ref, g_ref, sre_ref, sim_ref,
                er_ref, ei_ref, car_ref, *, tc):
    c = pl.program_id(2)

    @pl.when(c == 0)
    def _():
        car_ref[...] = h0_ref[...]

    u = u_ref[...]
    e = _dot(u.astype(BF16), wb_ref[...])
    er_ref[...] = e[:, :SSM_SB]
    ei_ref[...] = e[:, SSM_SB:]

    def body(r, carry):
        cr, ci = carry
        i0 = pl.multiple_of(r * SUBLANES, SUBLANES)
        xr = er_ref[pl.ds(i0, SUBLANES), :]
        xi = ei_ref[pl.ds(i0, SUBLANES), :]
        for n, k in enumerate((1, 2, 4)):
            ar, ai = cst_ref[2 * n], cst_ref[2 * n + 1]
            sr, si = pltpu.roll(xr, k, 0), pltpu.roll(xi, k, 0)
            xr, xi = xr + ar * sr - ai * si, xi + ar * si + ai * sr
        pr, pi_ = cst_ref[6], cst_ref[7]
        xr, xi = xr + pr * cr - pi_ * ci, xi + pr * ci + pi_ * cr
        er_ref[pl.ds(i0, SUBLANES), :] = xr
        ei_ref[pl.ds(i0, SUBLANES), :] = xi
        return xr[SUBLANES - 1:SUBLANES, :], xi[SUBLANES - 1:SUBLANES, :]

    cr, ci = lax.fori_loop(0, tc // SUBLANES, body, (car_ref[0:1, :], car_ref[1:2, :]))
    car_ref[0:1, :] = cr
    car_ref[1:2, :] = ci

    y = _dot(er_ref[...].astype(BF16), wc_ref[0]) - _dot(ei_ref[...].astype(BF16), wc_ref[1])
    y = y + d_ref[...] * u
    g_ref[...] = _gelu_tanh(y).astype(BF16)

    @pl.when(c == pl.num_programs(2) - 1)
    def _():
        sre_ref[...] = cr
        sim_ref[...] = ci


def _ssm_weights(a_re, a_im, log_dt, b_re, b_im, c_re, c_im):
    lam_re, lam_im = a_re, a_im
    dt = jnp.exp(log_dt)[:, None]
    mag = jnp.exp(lam_re * dt)
    lb_re, lb_im = mag * jnp.cos(lam_im * dt), mag * jnp.sin(lam_im * dt)
    den = lam_re * lam_re + lam_im * lam_im
    num_re = lb_re - 1.0
    z_re = (num_re * lam_re + lb_im * lam_im) / den
    z_im = (lb_im * lam_re - num_re * lam_im) / den
    zb_re = z_re[:, :, None] * b_re - z_im[:, :, None] * b_im
    zb_im = z_re[:, :, None] * b_im + z_im[:, :, None] * b_re
    eye = jnp.eye(8, dtype=F32)

    def blockdiag_in(w):
        return jnp.einsum('jgph,gk->jghkp', w.reshape(SSM_LB, 8, SSM_STATE, SSM_GROUP), eye).reshape(SSM_LB, LANES, SSM_SB)

    def blockdiag_out(w):
        return jnp.einsum('jghp,gk->jkpgh', w.reshape(SSM_LB, 8, SSM_GROUP, SSM_STATE), eye).reshape(SSM_LB, SSM_SB, LANES)

    wb = jnp.concatenate([blockdiag_in(zb_re), blockdiag_in(zb_im)], axis=-1).astype(BF16)
    wc = jnp.stack([blockdiag_out(c_re), blockdiag_out(c_im)], axis=1).astype(BF16)

    pw = [(lb_re, lb_im)]
    for _ in range(7):
        pr, pi_ = pw[-1]
        pw.append((pr * lb_re - pi_ * lb_im, pr * lb_im + pi_ * lb_re))
    rows = jnp.arange(SUBLANES)[:, None]

    def lane(x):
        return x.reshape(SSM_LB, 1, SSM_SB)

    cst = []
    for k in (1, 2, 4):
        for part in pw[k - 1]:
            cst.append(jnp.where(rows >= k, lane(part), 0.0))
    cst.append(jnp.concatenate([lane(pw[r][0]) for r in range(SUBLANES)], axis=1))
    cst.append(jnp.concatenate([lane(pw[r][1]) for r in range(SUBLANES)], axis=1))
    cst = jnp.stack(cst, axis=1)
    return wb, wc, cst


def ssm(proj, ssm_w, d_skip, h0, *, n_batch, seq, tc, row0):
    wb, wc, cst = ssm_w
    n_chunks = seq // tc
    blk0 = row0 // tc
    n_tok = n_batch * seq
    state_shape = jax.ShapeDtypeStruct((n_batch, SSM_LB, 1, SSM_SB), F32)
    state_spec = pl.BlockSpec((None, None, 1, SSM_SB), lambda b, j, c: (b, j, 0, 0))
    g, s_re, s_im = pl.pallas_call(
        functools.partial(_ssm_kernel, tc=tc),
        grid=(n_batch, SSM_LB, n_chunks),
        in_specs=[pl.BlockSpec((tc, LANES), lambda b, j, c: (blk0 + b * n_chunks + c, j)),
                  pl.BlockSpec((None, LANES, 2 * SSM_SB), lambda b, j, c: (j, 0, 0)),
                  pl.BlockSpec((None, 2, SSM_SB, LANES), lambda b, j, c: (j, 0, 0, 0)),
                  pl.BlockSpec((None, 8, SUBLANES, SSM_SB), lambda b, j, c: (j, 0, 0, 0)),
                  pl.BlockSpec((1, LANES), lambda b, j, c: (0, j)),
                  pl.BlockSpec((None, None, 2, SSM_SB), lambda b, j, c: (b, j, 0, 0))],
        out_specs=[pl.BlockSpec((tc, LANES), lambda b, j, c: (b * n_chunks + c, j)), state_spec, state_spec],
        out_shape=[jax.ShapeDtypeStruct((n_tok, SSM_WIDTH), BF16), state_shape, state_shape],
        scratch_shapes=[pltpu.VMEM((tc, SSM_SB), F32), pltpu.VMEM((tc, SSM_SB), F32), pltpu.VMEM((2, SSM_SB), F32)],
        compiler_params=_params(("arbitrary", "arbitrary", "arbitrary")),
        name="ssm",
    )(proj, wb, wc, cst, d_skip[None, :], h0)
    return g, s_re.reshape(n_batch, SSM_GROUPS, SSM_STATE), s_im.reshape(n_batch, SSM_GROUPS, SSM_STATE)


def _ssm_step_kernel(u_ref, wb_ref, wc_ref, cst_ref, d_ref, h0_ref, g_ref, sre_ref, sim_ref, er_ref, ei_ref, *, seq):
    n_seq = h0_ref.shape[1]
    u = u_ref[...]
    e = _dot(u.astype(BF16), wb_ref[...])
    n_lt = SSM_SB // LANES
    y = d_ref[...] * u
    for lt in range(n_lt):
        sl = slice(lt * LANES, (lt + 1) * LANES)
        er_ref[...] = e[:, lt * LANES:(lt + 1) * LANES]
        ei_ref[...] = e[:, SSM_SB + lt * LANES:SSM_SB + (lt + 1) * LANES]
        lr, li = cst_ref[6, 0:1, sl], cst_ref[7, 0:1, sl]
        sr, si = h0_ref[0, :, sl], h0_ref[1, :, sl]
        for t in range(seq):
            rows = pl.ds(t, n_seq, stride=seq)
            sr, si = lr * sr - li * si + er_ref[rows, :], lr * si + li * sr + ei_ref[rows, :]
            er_ref[rows, :] = sr
            ei_ref[rows, :] = si
        y = y + (_dot(er_ref[...].astype(BF16), wc_ref[0, sl, :]) - _dot(ei_ref[...].astype(BF16), wc_ref[1, sl, :]))
        sre_ref[:, sl] = sr
        sim_ref[:, sl] = si
    g_ref[...] = _gelu_tanh(y).astype(BF16)


def ssm_step(proj, ssm_w, d_skip, h0, *, n_batch, seq, row0):
    wb, wc, cst = ssm_w
    n_tok = n_batch * seq
    assert row0 % n_tok == 0
    state_shape = jax.ShapeDtypeStruct((SSM_LB, n_batch, SSM_SB), F32)
    state_spec = pl.BlockSpec((None, n_batch, SSM_SB), lambda j: (j, 0, 0))
    g, s_re, s_im = pl.pallas_call(
        functools.partial(_ssm_step_kernel, seq=seq),
        grid=(SSM_LB,),
        in_specs=[pl.BlockSpec((n_tok, LANES), lambda j: (row0 // n_tok, j)),
                  pl.BlockSpec((None, LANES, 2 * SSM_SB), lambda j: (j, 0, 0)),
                  pl.BlockSpec((None, 2, SSM_SB, LANES), lambda j: (j, 0, 0, 0)),
                  pl.BlockSpec((None, 8, SUBLANES, SSM_SB), lambda j: (j, 0, 0, 0)),
                  pl.BlockSpec((1, LANES), lambda j: (0, j)),
                  pl.BlockSpec((None, 2, n_batch, SSM_SB), lambda j: (j, 0, 0, 0))],
        out_specs=[pl.BlockSpec((n_tok, LANES), lambda j: (0, j)), state_spec, state_spec],
        out_shape=[jax.ShapeDtypeStruct((n_tok, SSM_WIDTH), BF16), state_shape, state_shape],
        scratch_shapes=[pltpu.VMEM((n_tok, LANES), F32), pltpu.VMEM((n_tok, LANES), F32)],
        compiler_params=_params(("arbitrary",)),
        name="ssm_step",
    )(proj, wb, wc, cst, d_skip[None, :], h0)

    def per_seq(s):
        return s.transpose(1, 0, 2).reshape(n_batch, SSM_GROUPS, SSM_STATE)

    return g, per_seq(s_re), per_seq(s_im)


def _row_sum(x):
    return jnp.sum(x, axis=1, keepdims=True)


def _row_count(mask):
    return _row_sum(jnp.where(mask, 1, 0))


def _dsa_body(q_ref, qi_ref, wi_ref, k_ref, v_ref, ki_ref, o_ref, key_ref, bias_ref, *, q_pos_first, s_valid, n_sel):
    bq, n_keys = key_ref.shape
    col = lax.broadcasted_iota(I32, (bq, n_keys), 1)
    qpos = q_pos_first + lax.broadcasted_iota(I32, (bq, 1), 0)
    allowed = col < jnp.minimum((qpos // CHUNK + 1) * CHUNK, s_valid)

    ki = ki_ref[...]
    score = None
    for h in range(IDX_HEADS):
        d = _dot_nt(qi_ref[:, h * LANES:(h + 1) * LANES], ki)
        t = jnp.maximum(d, 0.0) * wi_ref[:, h:h + 1]
        score = t if score is None else score + t
    score = jnp.where(score == 0.0, 0.0, score)
    bits = pltpu.bitcast(score, I32)
    key = jnp.where(bits < 0, bits ^ np.int32(0x7FFFFFFF), bits)
    key_ref[...] = jnp.where(allowed, key, KEY_NEG_INF)

    def bisect(i, base):
        cand = base + lax.shift_left(np.int32(1), np.int32(31) - i)
        cnt = _row_count(key_ref[...] >= cand)
        return jnp.where(cnt >= n_sel, cand, base)

    thr = lax.fori_loop(0, 32, bisect, jnp.full((bq, 1), INT_MIN, I32))
    thr = jnp.maximum(thr, KEY_NEG_INF)

    key = key_ref[...]
    need = n_sel - _row_count(key > thr)
    n_eq = _row_count(key == thr)
    n_bits = int(n_keys - 1).bit_length()

    def tie_cut():
        def step(i, j0):
            cand = j0 + lax.shift_left(np.int32(1), np.int32(n_bits - 1) - i)
            cnt = _row_sum(jnp.where(key_ref[...] == thr, jnp.where(col < cand, 1, 0), 0))
            return jnp.where(cnt < need, cand, j0)
        return lax.fori_loop(0, n_bits, step, jnp.zeros((bq, 1), I32))

    split = jnp.max(jnp.where(n_eq > need, 1, 0)) > 0
    j_last = lax.cond(split, tie_cut, lambda: jnp.full((bq, 1), n_keys, I32))
    tie_bias = jnp.where(thr == KEY_NEG_INF, -jnp.inf, 0.0)
    bias_ref[...] = jnp.where(key > thr, 0.0,
                              jnp.where(key == thr, jnp.where(col <= j_last, tie_bias, -jnp.inf), -jnp.inf))

    c = np.float32(HEAD_DIM ** -0.5 * np.log2(np.e))
    for kv in range(N_KV_HEADS):
        kk = k_ref[:, kv * HEAD_DIM:(kv + 1) * HEAD_DIM]
        vv = v_ref[:, kv * V_AUG:(kv + 1) * V_AUG]
        for g in range(KV_GROUP):
            sl = slice((kv * KV_GROUP + g) * HEAD_DIM, (kv * KV_GROUP + g + 1) * HEAD_DIM)
            s = _dot_nt(q_ref[:, sl], kk) + bias_ref[...]
            m = jnp.max(s, axis=1, keepdims=True)
            p = jnp.exp2((s - m) * c)
            pv = _dot(p.astype(BF16), vv)
            o_ref[:, sl] = (pv[:, :HEAD_DIM] / pv[:, HEAD_DIM:HEAD_DIM + 1]).astype(BF16)


def _dsa_kernel(q_ref, qi_ref, wi_ref, k_ref, v_ref, ki_ref, o_ref, key_ref, bias_ref, *, q_pos0, s_valid, n_sel):
    bq = key_ref.shape[0]
    _dsa_body(q_ref, qi_ref, wi_ref, k_ref, v_ref, ki_ref, o_ref, key_ref, bias_ref,
              q_pos_first=q_pos0 + pl.program_id(1) * bq, s_valid=s_valid, n_sel=n_sel)


def dsa(q, qi, wi, k, v, ki, *, bq, q_blk0, n_qblk, n_keys, n_sel):
    n_batch, seq = q.shape[:2]

    def qspec(width):
        return pl.BlockSpec((None, bq, width), lambda b, i: (b, q_blk0 + i, 0))

    def kspec(width):
        return pl.BlockSpec((None, n_keys, width), lambda b, i: (b, 0, 0))

    return pl.pallas_call(
        functools.partial(_dsa_kernel, q_pos0=q_blk0 * bq, s_valid=seq, n_sel=n_sel),
        grid=(n_batch, n_qblk),
        in_specs=[qspec(ATTN_WIDTH), qspec(IDX_HEADS * LANES), qspec(LANES), kspec(KV_WIDTH), kspec(N_KV_HEADS * V_AUG),
                  kspec(LANES)],
        out_specs=pl.BlockSpec((None, bq, ATTN_WIDTH), lambda b, i: (b, i, 0)),
        out_shape=jax.ShapeDtypeStruct((n_batch, n_qblk * bq, ATTN_WIDTH), BF16),
        scratch_shapes=[pltpu.VMEM((bq, n_keys), I32), pltpu.VMEM((bq, n_keys), F32)],
        compiler_params=_params(("arbitrary", "arbitrary")),
        name="dsa",
    )(q, qi, wi, k, v, ki)


def _dsa_step_kernel(q_ref, qi_ref, wi_ref, ck_ref, cv_ref, cki_ref, nk_ref, nv_ref, nki_ref, o_ref,
                     k_buf, v_buf, ki_buf, key_ref, bias_ref, *, past, n_sel):
    ts = nk_ref.shape[0]
    n_keys = k_buf.shape[0]
    k_buf[0:past, :] = ck_ref[...].astype(BF16)
    _store_v_aug(v_buf, 0, cv_ref[...])
    for buf, new in ((k_buf, nk_ref), (v_buf, nv_ref)):
        buf[past:past + ts, :] = new[...]
        buf[past + ts:n_keys, :] = jnp.zeros((n_keys - past - ts, buf.shape[1]), BF16)
    ki_buf[0:past, 0:IDX_DIM] = cki_ref[...].astype(BF16)
    ki_buf[0:past, IDX_DIM:LANES] = jnp.zeros((past, LANES - IDX_DIM), BF16)
    ki_buf[past:past + ts, :] = nki_ref[...]
    ki_buf[past + ts:n_keys, :] = jnp.zeros((n_keys - past - ts, LANES), BF16)
    _dsa_body(q_ref, qi_ref, wi_ref, k_buf, v_buf, ki_buf, o_ref, key_ref, bias_ref,
              q_pos_first=past, s_valid=past + ts, n_sel=n_sel)


def dsa_step(q, qi, wi, cache_k, cache_v, cache_ki, k_new, v_new, ki_new, *, n_sel):
    n_batch, ts = q.shape[:2]
    past = cache_k.shape[1]
    n_keys = -(-(past + ts) // LANES) * LANES

    def spec(rows, width):
        return pl.BlockSpec((None, rows, width), lambda b: (b, 0, 0))

    return pl.pallas_call(
        functools.partial(_dsa_step_kernel, past=past, n_sel=n_sel),
        grid=(n_batch,),
        in_specs=[spec(ts, ATTN_WIDTH), spec(ts, IDX_HEADS * LANES), spec(ts, LANES),
                  spec(past, KV_WIDTH), spec(past, KV_WIDTH), spec(past, IDX_DIM),
                  spec(ts, KV_WIDTH), spec(ts, N_KV_HEADS * V_AUG), spec(ts, LANES)],
        out_specs=spec(ts, ATTN_WIDTH),
        out_shape=jax.ShapeDtypeStruct((n_batch, ts, ATTN_WIDTH), BF16),
        scratch_shapes=[pltpu.VMEM((n_keys, KV_WIDTH), BF16), pltpu.VMEM((n_keys, N_KV_HEADS * V_AUG), BF16),
                        pltpu.VMEM((n_keys, LANES), BF16),
                        pltpu.VMEM((ts, n_keys), I32), pltpu.VMEM((ts, n_keys), F32)],
        compiler_params=_params(("arbitrary",)),
        name="dsa_step",
    )(q, qi, wi, cache_k, cache_v, cache_ki, k_new, v_new, ki_new)


def _merge_kernel(g_ref, a_ref, ga_ref, gb_ref, wv_ref, wg_ref, wb_ref, o_ref):
    g = g_ref[...]
    branch_a = _dot(g, wv_ref[...]) * jax.nn.sigmoid(_dot(g, wg_ref[...]))
    branch_b = _dot(a_ref[...], wb_ref[...])
    merged = jax.nn.sigmoid(ga_ref[...]) * branch_a + jax.nn.sigmoid(gb_ref[...]) * branch_b
    o_ref[...] = merged.astype(BF16)


def merge(g, attn, proj, w_val, w_gate, w_branch, *, tm=1024, tn=512):
    n_tok = g.shape[0]
    nj = D_MODEL // tn

    def wspec():
        return pl.BlockSpec((SSM_WIDTH, tn), lambda i, j: (0, j))

    return pl.pallas_call(
        _merge_kernel,
        grid=(n_tok // tm, nj),
        in_specs=[pl.BlockSpec((tm, SSM_WIDTH), lambda i, j: (i, 0)),
                  pl.BlockSpec((tm, ATTN_WIDTH), lambda i, j: (i, 0)),
                  pl.BlockSpec((tm, tn), lambda i, j: (i, COL_GA // tn + j)),
                  pl.BlockSpec((tm, tn), lambda i, j: (i, COL_GB // tn + j)),
                  wspec(), wspec(), wspec()],
        out_specs=pl.BlockSpec((tm, tn), lambda i, j: (i, j)),
        out_shape=jax.ShapeDtypeStruct((n_tok, D_MODEL), BF16),
        compiler_params=_params(("arbitrary", "arbitrary")),
        name="merge",
    )(g, attn, proj, proj, w_val, w_gate, w_branch)


ROUTER_COLS = N_EXPERT_GROUPS + N_EXPERTS
MOE_TM = 256


def _first_lane_of_max(x, lane_f):
    m = jnp.max(x, axis=1, keepdims=True)
    return m, jnp.min(jnp.where(x == m, lane_f, float(LANES)), axis=1, keepdims=True)


def _out_proj_kernel(x_ref, m_ref, wo_ref, gn_ref, wrh_ref, wrl_ref, br_ref, cin_ref,
                     h_ref, hn_ref, ri_ref, rw_ref, cnt_ref, carry_ref):
    @pl.when(pl.program_id(0) == 0)
    def _():
        carry_ref[...] = cin_ref[...]

    h = x_ref[...] + _dot(m_ref[...], wo_ref[...])
    h_ref[...] = h
    ms = jnp.mean(h * h, axis=-1, keepdims=True)
    hn = h * lax.rsqrt(ms + EPS) * gn_ref[...]
    hn_ref[...] = hn
    hh, hl = _split_bf16(hn)
    wrh = wrh_ref[...]
    lg = _dot(hh, wrh) + _dot(hl, wrh) + _dot(hh, wrl_ref[...]) + br_ref[...]

    tm = lg.shape[0]
    lane = lax.broadcasted_iota(I32, lg.shape, 1)
    lane_f = lane.astype(F32)
    ninf = -jnp.inf
    gl = jnp.where(lane < N_EXPERT_GROUPS, lg, ninf)
    gmax, gsel = _first_lane_of_max(gl, lane_f)
    g_w = 1.0 / jnp.sum(jnp.exp(gl - gmax), axis=1, keepdims=True)
    lo = N_EXPERT_GROUPS + EXPERTS_PER_GROUP * gsel
    el = jnp.where(lane_f >= lo, jnp.where(lane_f < lo + EXPERTS_PER_GROUP, lg, ninf), ninf)
    v1, i1 = _first_lane_of_max(el, lane_f)
    el2 = jnp.where(lane_f == i1, ninf, el)
    v2, i2 = _first_lane_of_max(el2, lane_f)
    t = jnp.exp(v2 - v1)
    s1 = 1.0 / (1.0 + t)
    w1 = s1 * g_w
    w2 = (t * s1) * g_w

    m1 = jnp.where(lane_f == i1, 1.0, 0.0)
    m2 = jnp.where(lane_f == i2, 1.0, 0.0)
    both = m1 + m2
    tri = jnp.where(lax.broadcasted_iota(I32, (tm, tm), 0) > lax.broadcasted_iota(I32, (tm, tm), 1), 1.0, 0.0)
    before = _dot(tri.astype(BF16), both.astype(BF16)) + carry_ref[...]
    r1 = jnp.sum(before * m1, axis=1, keepdims=True)
    r2 = jnp.sum(before * m2, axis=1, keepdims=True)
    carry_ref[...] = carry_ref[...] + jnp.sum(both, axis=0, keepdims=True)
    cnt_ref[...] = carry_ref[...]
    e1 = i1 - float(N_EXPERT_GROUPS)
    e2 = i2 - float(N_EXPERT_GROUPS)
    fields = jnp.where(lane == 0, e1, jnp.where(lane == 1, e2, jnp.where(lane == 2, r1, jnp.where(lane == 3, r2, 0.0))))
    ri_ref[...] = fields.T[0:SUBLANES, :].astype(I32)
    rw_ref[...] = jnp.where(lane == 0, w1, jnp.where(lane == 1, w2, 0.0))


def _router_weights(w_router_group, b_router_group, w_router_expert, b_router_expert):
    wr = jnp.concatenate([w_router_group, w_router_expert, jnp.zeros((D_MODEL, LANES - ROUTER_COLS), F32)], axis=1)
    wr_hi = wr.astype(BF16)
    wr_lo = (wr - wr_hi.astype(F32)).astype(BF16)
    br = jnp.concatenate([b_router_group, b_router_expert, jnp.zeros((LANES - ROUTER_COLS,), F32)])[None, :]
    return wr_hi, wr_lo, br


def out_proj(x, merged, w_out, ffn_gain, router_w, counts_in, *, tm=256):
    n_tok = x.shape[0]
    wr_hi, wr_lo, br = router_w

    def row(width):
        return pl.BlockSpec((tm, width), lambda i: (i, 0))

    def const(shape):
        return pl.BlockSpec(shape, lambda i: (0, 0), pipeline_mode=pl.Buffered(1))

    return pl.pallas_call(
        _out_proj_kernel,
        grid=(n_tok // tm,),
        in_specs=[row(D_MODEL), row(D_MODEL), const((D_MODEL, D_MODEL)), const((1, D_MODEL)),
                  const((D_MODEL, LANES)), const((D_MODEL, LANES)), const((1, LANES)), const((1, LANES))],
        out_specs=[row(D_MODEL), row(D_MODEL), pl.BlockSpec((SUBLANES, tm), lambda i: (0, i)), row(LANES),
                   pl.BlockSpec((1, LANES), lambda i: (0, 0))],
        out_shape=[jax.ShapeDtypeStruct((n_tok, D_MODEL), F32), jax.ShapeDtypeStruct((n_tok, D_MODEL), F32),
                   jax.ShapeDtypeStruct((SUBLANES, n_tok), I32), jax.ShapeDtypeStruct((n_tok, LANES), F32),
                   jax.ShapeDtypeStruct((1, LANES), F32)],
        scratch_shapes=[pltpu.VMEM((1, LANES), F32)],
        compiler_params=_params(("arbitrary",)),
        name="out_proj",
    )(x, merged, w_out, ffn_gain[None, :], wr_hi, wr_lo, br, counts_in)


def _block_layout(counts):
    padded = (counts + MOE_TM - 1) // MOE_TM * MOE_TM
    pad_end = jnp.cumsum(padded).astype(I32)
    pad_start = pad_end - padded
    n_used = pad_end[-1] // MOE_TM
    return pad_start, pad_end, n_used


def _moe_rows(n_tok):
    return -(-(n_tok * TOP_K + N_EXPERTS * (MOE_TM - 1)) // MOE_TM) * MOE_TM


DISPATCH_TM = 512


def _wait_rows(src_hbm, dst, sem, n_rows):
    pltpu.make_async_copy(src_hbm.at[pl.ds(0, n_rows)], dst, sem).wait()


def _dispatch_kernel(d0_ref, d1_ref, pe_ref, cnt_ref, nu_ref, hna_ref, hnb_ref, xs_hbm, zbuf, sem, semz,
                     *, n_blocks, a_tiles):
    i = pl.program_id(0)

    def zero_block(row0):
        return pltpu.make_async_copy(zbuf, xs_hbm.at[pl.ds(pl.multiple_of(row0, MOE_TM), MOE_TM)], semz)

    @pl.when(i == 0)
    def _():
        zbuf[...] = jnp.zeros_like(zbuf)
        for start in (True, False):
            for e in range(N_EXPERTS):
                @pl.when(cnt_ref[e] > 0)
                def _():
                    cp = zero_block(pe_ref[e] - MOE_TM)
                    cp.start() if start else cp.wait()

            def tail(b, c):
                cp = zero_block(b * MOE_TM)
                cp.start() if start else cp.wait()
                return c
            lax.fori_loop(nu_ref[0], n_blocks, tail, 0)

    base = i * DISPATCH_TM

    def scatter(hn_ref):
        def body(r, c):
            src = hn_ref.at[pl.ds(r, 1)]
            pltpu.make_async_copy(src, xs_hbm.at[pl.ds(d0_ref[base + r], 1)], sem).start()
            pltpu.make_async_copy(src, xs_hbm.at[pl.ds(d1_ref[base + r], 1)], sem).start()
            return c
        lax.fori_loop(0, DISPATCH_TM, body, 0, unroll=8)
        for _ in range(TOP_K):
            pltpu.make_async_copy(hn_ref, xs_hbm.at[pl.ds(0, DISPATCH_TM)], sem).wait()

    @pl.when(i < a_tiles)
    def _():
        scatter(hna_ref)

    @pl.when(i >= a_tiles)
    def _():
        scatter(hnb_ref)


def dispatch(hn_a, hn_b, dest0, dest1, pad_end, counts, n_used):
    a_tiles, b_tiles = hn_a.shape[0] // DISPATCH_TM, hn_b.shape[0] // DISPATCH_TM
    rows = _moe_rows(hn_a.shape[0] + hn_b.shape[0])
    grid_spec = pltpu.PrefetchScalarGridSpec(
        num_scalar_prefetch=5,
        grid=(a_tiles + b_tiles,),
        in_specs=[pl.BlockSpec((DISPATCH_TM, D_MODEL), lambda i, *_: (jnp.minimum(i, a_tiles - 1), 0)),
                  pl.BlockSpec((DISPATCH_TM, D_MODEL), lambda i, *_: (jnp.maximum(i - a_tiles, 0), 0))],
        out_specs=pl.BlockSpec(memory_space=pl.ANY),
        scratch_shapes=[pltpu.VMEM((MOE_TM, D_MODEL), F32), pltpu.SemaphoreType.DMA(()), pltpu.SemaphoreType.DMA(())],
    )
    return pl.pallas_call(
        functools.partial(_dispatch_kernel, n_blocks=rows // MOE_TM, a_tiles=a_tiles),
        grid_spec=grid_spec,
        out_shape=jax.ShapeDtypeStruct((rows, D_MODEL), F32),
        compiler_params=_params(("arbitrary",)),
        name="dispatch",
    )(dest0, dest1, pad_end, counts, n_used, hn_a, hn_b)


def _moe_kernel(blk_e_ref, nu_ref, xs_ref, wg_ref, wu_ref, wd_ref, ys_ref):
    i = pl.program_id(0)

    @pl.when(i < nu_ref[0])
    def _():
        x = xs_ref[...].astype(BF16)
        hg = _dot(x, wg_ref[...])
        hu = _dot(x, wu_ref[...])
        hmid = (jax.nn.silu(hg) * hu).astype(BF16)
        ys_ref[...] = _dot(hmid, wd_ref[...])

    @pl.when(i >= nu_ref[0])
    def _():
        ys_ref[...] = jnp.zeros_like(ys_ref)


def moe(xs, blk_e, n_used, w_gate, w_up, w_down):
    rows = xs.shape[0]
    grid_spec = pltpu.PrefetchScalarGridSpec(
        num_scalar_prefetch=2,
        grid=(rows // MOE_TM,),
        in_specs=[pl.BlockSpec((MOE_TM, D_MODEL), lambda i, be, nu: (jnp.minimum(i, nu[0] - 1), 0)),
                  pl.BlockSpec((None, D_MODEL, EXPERT_FF), lambda i, be, nu: (be[i], 0, 0)),
                  pl.BlockSpec((None, D_MODEL, EXPERT_FF), lambda i, be, nu: (be[i], 0, 0)),
                  pl.BlockSpec((None, EXPERT_FF, D_MODEL), lambda i, be, nu: (be[i], 0, 0))],
        out_specs=pl.BlockSpec((MOE_TM, D_MODEL), lambda i, be, nu: (i, 0)),
    )
    return pl.pallas_call(
        _moe_kernel,
        grid_spec=grid_spec,
        out_shape=jax.ShapeDtypeStruct((rows, D_MODEL), F32),
        compiler_params=_params(("arbitrary",)),
        name="moe",
    )(blk_e, n_used, xs, w_gate, w_up, w_down)


def _gather_rows(idx_ref, idx0, src_hbm, dst, sem, n_rows):
    def body(r, carry):
        t = idx_ref[idx0 + r]
        pltpu.make_async_copy(src_hbm.at[pl.ds(t, 1)], dst.at[pl.ds(r, 1)], sem).start()
        return carry
    lax.fori_loop(0, n_rows, body, 0, unroll=8)


def _combine_kernel(r0_ref, r1_ref, ys_hbm, h_ref, w_ref, o_ref, buf, sem, *, tm, tok0):
    i = pl.program_id(0)

    def issue(block, slot):
        _gather_rows(r0_ref, tok0 + block * tm, ys_hbm, buf.at[slot, 0], sem.at[slot], tm)
        _gather_rows(r1_ref, tok0 + block * tm, ys_hbm, buf.at[slot, 1], sem.at[slot], tm)

    @pl.when(i == 0)
    def _():
        issue(0, 0)

    @pl.when(i + 1 < pl.num_programs(0))
    def _():
        issue(i + 1, (i + 1) % 2)

    slot = i % 2
    _wait_rows(ys_hbm, buf.at[slot, 0], sem.at[slot], tm)
    _wait_rows(ys_hbm, buf.at[slot, 1], sem.at[slot], tm)
    w = w_ref[...]
    o_ref[...] = h_ref[...] + (buf[slot, 0] * w[:, 0:1] + buf[slot, 1] * w[:, 1:2])


def combine(ys, h, route_w, rows0, rows1, *, tok0, tm=256):
    n_tok = h.shape[0]
    grid_spec = pltpu.PrefetchScalarGridSpec(
        num_scalar_prefetch=2,
        grid=(n_tok // tm,),
        in_specs=[pl.BlockSpec(memory_space=pl.ANY),
                  pl.BlockSpec((tm, D_MODEL), lambda i, a, b: (i, 0)),
                  pl.BlockSpec((tm, LANES), lambda i, a, b: (i, 0))],
        out_specs=pl.BlockSpec((tm, D_MODEL), lambda i, a, b: (i, 0)),
        scratch_shapes=[pltpu.VMEM((2, 2, tm, D_MODEL), F32), pltpu.SemaphoreType.DMA((2,))],
    )
    return pl.pallas_call(
        functools.partial(_combine_kernel, tm=tm, tok0=tok0),
        grid_spec=grid_spec,
        out_shape=jax.ShapeDtypeStruct((n_tok, D_MODEL), F32),
        compiler_params=_params(("arbitrary",)),
        name="combine",
    )(rows0, rows1, ys, h, route_w)


def _regroup_w_in(w_in):
    sizes = (SSM_WIDTH, ATTN_WIDTH, KV_WIDTH, KV_WIDTH, IDX_HEADS * IDX_DIM, IDX_DIM, IDX_HEADS, D_MODEL, D_MODEL)
    u, q, k, v, qi, ki, wi, ga, gb = jnp.split(w_in, np.cumsum(sizes)[:-1].tolist(), axis=1)
    pad = jnp.zeros((D_MODEL, PROJ_COLS - COL_KIWI - IDX_DIM - IDX_HEADS), F32)
    return jnp.concatenate([u, q, ga, gb, k, v, qi, ki, wi, pad], axis=1).astype(BF16)


def _layer(x_p, x_s, cache_k, cache_v, cache_ki, h0_re, h0_im, p):
    bp, tp, _ = x_p.shape
    bs, ts, _ = x_s.shape
    past = cache_k.shape[1]
    n_p, n_s = bp * tp, bs * ts
    n_tok = n_p + n_s

    w_in = _regroup_w_in(p['w_in'])
    ssm_w = _ssm_weights(p['ssm_A_re'], p['ssm_A_im'], p['ssm_log_dt'], p['ssm_B_re'], p['ssm_B_im'],
                         p['ssm_C_re'], p['ssm_C_im'])
    glu_w = (p['w_glu_val'].astype(BF16), p['w_glu_gate'].astype(BF16), p['w_attn_branch'].astype(BF16))
    w_out = p['w_out'].astype(BF16)
    router_w = _router_weights(p['w_router_group'], p['b_router_group'], p['w_router_expert'], p['b_router_expert'])
    seq_tiles = tp // QK_TM

    def front(x, table_pos, table_block):
        proj = in_proj(x, p['norm_mix_g'][None, :], w_in)
        return proj, qk_post(proj, table_pos, table_block, p['q_norm_g'], p['k_norm_g'], p['idx_k_norm_g'])

    def seqs(a, b, t):
        return a.reshape(b, t, a.shape[-1])

    xp = x_p.reshape(n_p, D_MODEL)
    proj_p, (q_b, kf_p, k_b, vf_p, v_b, qi_b, kif_p, ki_b, wi) = front(
        xp, jnp.arange(tp, dtype=I32), lambda i: i % seq_tiles)
    g_p, sre_p, sim_p = ssm(proj_p, ssm_w, p['ssm_D'], jnp.zeros((bp, SSM_LB, 2, SSM_SB), F32),
                            n_batch=bp, seq=tp, tc=512, row0=0)
    bq = 128
    n_buckets = min(8, tp // bq)
    per = tp // bq // n_buckets
    qp, qip, wip = seqs(q_b, bp, tp), seqs(qi_b, bp, tp), seqs(wi, bp, tp)
    kp, vp, kip = seqs(k_b, bp, tp), seqs(v_b, bp, tp), seqs(ki_b, bp, tp)
    attn_p = jnp.concatenate(
        [dsa(qp, qip, wip, kp, vp, kip, bq=bq, q_blk0=n * per, n_qblk=per, n_keys=(n + 1) * per * bq,
             n_sel=min(IDX_TOPK, tp // 4)) for n in range(n_buckets)], axis=1).reshape(n_p, ATTN_WIDTH)
    merged_p = merge(g_p, attn_p, proj_p, *glu_w)
    h_p, hn_p, ri_p, rw_p, cnt_p = out_proj(xp, merged_p, w_out, p['norm_ffn_g'], router_w, jnp.zeros((1, LANES), F32))

    xs_ = x_s.reshape(n_s, D_MODEL)
    proj_s, (q_b, kf_s, k_b, vf_s, v_b, qi_b, kif_s, ki_b, wi) = front(
        xs_, jnp.tile(past + jnp.arange(ts, dtype=I32), QK_TM // ts), lambda i: 0)
    h0 = jnp.stack([h0_re.reshape(bs, SSM_LB, SSM_SB), h0_im.reshape(bs, SSM_LB, SSM_SB)]).transpose(2, 0, 1, 3)
    g_s, sre_s, sim_s = ssm_step(proj_s, ssm_w, p['ssm_D'], h0, n_batch=bs, seq=ts, row0=0)
    attn_s = dsa_step(seqs(q_b, bs, ts), seqs(qi_b, bs, ts), seqs(wi, bs, ts),
                      cache_k.reshape(bs, past, KV_WIDTH), cache_v.reshape(bs, past, KV_WIDTH), cache_ki,
                      seqs(k_b, bs, ts), seqs(v_b, bs, ts), seqs(ki_b, bs, ts),
                      n_sel=min(IDX_TOPK, (past + ts) // 4)).reshape(n_s, ATTN_WIDTH)
    merged_s = merge(g_s, attn_s, proj_s, *glu_w)
    h_s, hn_s, ri_s, rw_s, cnt = out_proj(xs_, merged_s, w_out, p['norm_ffn_g'], router_w, cnt_p)

    counts = cnt[0, N_EXPERT_GROUPS:ROUTER_COLS].astype(I32)
    pad_start, pad_end, n_used = _block_layout(counts)
    route_i = jnp.concatenate([ri_p, ri_s], axis=1)
    dest0 = pad_start[route_i[0]] + route_i[2]
    dest1 = pad_start[route_i[1]] + route_i[3]
    n_blocks = _moe_rows(n_tok) // MOE_TM
    blk = jnp.minimum(jnp.arange(n_blocks, dtype=I32), n_used - 1)
    blk_e = jnp.minimum(jnp.sum((pad_end[None, :] <= (blk * MOE_TM)[:, None]).astype(I32), axis=1), N_EXPERTS - 1)
    n_used = n_used.reshape(1)

    xs = dispatch(hn_p, hn_s, dest0, dest1, pad_end, counts, n_used)
    ys = moe(xs, blk_e, n_used, p['w_exp_gate'].astype(BF16), p['w_exp_up'].astype(BF16), p['w_exp_down'].astype(BF16))
    y_p = combine(ys, h_p, rw_p, dest0, dest1, tok0=0).reshape(bp, tp, D_MODEL)
    y_s = combine(ys, h_s, rw_s, dest0, dest1, tok0=n_p).reshape(bs, ts, D_MODEL)

    def heads(a, b, t):
        return a.reshape(b, t, N_KV_HEADS, HEAD_DIM)

    new_p = (heads(kf_p, bp, tp), heads(vf_p, bp, tp), kif_p.reshape(bp, tp, IDX_DIM), sre_p, sim_p)
    new_s = (heads(kf_s, bs, ts), heads(vf_s, bs, ts), kif_s.reshape(bs, ts, IDX_DIM), sre_s, sim_s)
    return y_p, y_s, new_p, new_s


def kernel(x_prompt, x_sample, cache_k, cache_v, cache_idx_k, state_ssm_re, state_ssm_im, norm_mix_g, w_in, q_norm_g, k_norm_g, idx_k_norm_g, ssm_A_re, ssm_A_im, ssm_log_dt, ssm_B_re, ssm_B_im, ssm_C_re, ssm_C_im, ssm_D, w_glu_val, w_glu_gate, w_attn_branch, w_out, norm_ffn_g, w_router_group, b_router_group, w_router_expert, b_router_expert, w_exp_gate, w_exp_up, w_exp_down):
    depth = w_in.shape[0]
    assert depth == 1, "prompt and sample tokens are batched through one layer"
    names = ('norm_mix_g', 'w_in', 'q_norm_g', 'k_norm_g', 'idx_k_norm_g', 'ssm_A_re', 'ssm_A_im', 'ssm_log_dt',
             'ssm_B_re', 'ssm_B_im', 'ssm_C_re', 'ssm_C_im', 'ssm_D', 'w_glu_val', 'w_glu_gate', 'w_attn_branch',
             'w_out', 'norm_ffn_g', 'w_router_group', 'b_router_group', 'w_router_expert', 'b_router_expert',
             'w_exp_gate', 'w_exp_up', 'w_exp_down')
    vals = (norm_mix_g, w_in, q_norm_g, k_norm_g, idx_k_norm_g, ssm_A_re, ssm_A_im, ssm_log_dt, ssm_B_re, ssm_B_im,
            ssm_C_re, ssm_C_im, ssm_D, w_glu_val, w_glu_gate, w_attn_branch, w_out, norm_ffn_g, w_router_group,
            b_router_group, w_router_expert, b_router_expert, w_exp_gate, w_exp_up, w_exp_down)
    p = {n: v[0] for n, v in zip(names, vals)}
    y_p, y_s, new_p, new_s = _layer(x_prompt, x_sample, cache_k[0], cache_v[0], cache_idx_k[0],
                                    state_ssm_re[0], state_ssm_im[0], p)
    st_p = tuple(a[None] for a in new_p)
    st_s = tuple(a[None] for a in new_s)
    return (y_p, y_s) + st_p + st_s
```

```python
import functools

import numpy as np
import jax
import jax.numpy as jnp
from jax import lax
from jax.experimental import pallas as pl
from jax.experimental.pallas import tpu as pltpu

F32 = jnp.float32
BF16 = jnp.bfloat16
I32 = jnp.int32

D_MODEL = 2048
CHUNK = 64
SSM_WIDTH = 1024
SSM_GROUP = 16
SSM_GROUPS = 64
SSM_STATE = 64
ATTN_WIDTH = 1024
HEAD_DIM = 128
N_HEADS = 8
N_KV_HEADS = 2
KV_GROUP = 4
IDX_HEADS = 8
IDX_DIM = 64
IDX_TOPK = 256
ROPE_THETA = 500000.0
N_EXPERT_GROUPS = 4
EXPERTS_PER_GROUP = 8
N_EXPERTS = 32
TOP_K = 2
EXPERT_FF = 1024
EPS = 1e-6

LANES = 128
SUBLANES = 8
VMEM_LIMIT = 56 * 1024 * 1024

COL_U, COL_Q, COL_GA, COL_GB, COL_K, COL_V, COL_QI, COL_KIWI = 0, 1024, 2048, 4096, 6144, 6400, 6656, 7168
PROJ_COLS = 7296
PROJ_TN = 2432
KV_WIDTH = N_KV_HEADS * HEAD_DIM

SSM_LB = SSM_WIDTH // LANES
SSM_SB = 8 * SSM_STATE

INT_MIN = np.int32(-2 ** 31)
KEY_NEG_INF = np.int32(np.array([0xFF800000], np.uint32).view(np.int32)[0] ^ 0x7FFFFFFF)


def _params(sem, vmem=VMEM_LIMIT):
    return pltpu.CompilerParams(dimension_semantics=sem, vmem_limit_bytes=vmem)


def _dot(a, b):
    return jnp.dot(a, b, preferred_element_type=F32)


def _dot_nt(a, b):
    return lax.dot_general(a, b, (((1,), (1,)), ((), ())), preferred_element_type=F32)


def _split_bf16(x):
    hi = x.astype(BF16)
    lo = (x - hi.astype(F32)).astype(BF16)
    return hi, lo


def _in_proj_kernel(x_ref, g_ref, w_ref, o_ref, xn_ref):
    @pl.when(pl.program_id(1) == 0)
    def _():
        x = x_ref[...]
        ms = jnp.mean(x * x, axis=-1, keepdims=True)
        xn_ref[...] = (x * lax.rsqrt(ms + EPS) * g_ref[...]).astype(BF16)

    o_ref[...] = _dot(xn_ref[...], w_ref[...])


def in_proj(x, gain, w_bf16, *, tm=512):
    n_tok = x.shape[0]
    return pl.pallas_call(
        _in_proj_kernel,
        grid=(n_tok // tm, PROJ_COLS // PROJ_TN),
        in_specs=[pl.BlockSpec((tm, D_MODEL), lambda i, j: (i, 0)),
                  pl.BlockSpec((1, D_MODEL), lambda i, j: (0, 0)),
                  pl.BlockSpec((D_MODEL, PROJ_TN), lambda i, j: (0, j))],
        out_specs=pl.BlockSpec((tm, PROJ_TN), lambda i, j: (i, j)),
        out_shape=jax.ShapeDtypeStruct((n_tok, PROJ_COLS), F32),
        scratch_shapes=[pltpu.VMEM((tm, D_MODEL), BF16)],
        compiler_params=_params(("arbitrary", "arbitrary")),
        name="in_proj",
    )(x, gain, w_bf16)


def _rope(x, c, s_lo, s_hi, half):
    n = x.shape[-1]
    return x * c + pltpu.roll(x, n - half, 1) * s_lo + pltpu.roll(x, half, 1) * s_hi


def _head_norm(x, g):
    ms = jnp.mean(x * x, axis=-1, keepdims=True)
    return x * lax.rsqrt(ms + EPS) * g


V_AUG = 2 * HEAD_DIM


def _store_v_aug(dst_ref, row0, v):
    n = v.shape[0]
    one_col = jnp.where(lax.broadcasted_iota(I32, (n, HEAD_DIM), 1) == 0, 1.0, 0.0).astype(BF16)
    for h in range(N_KV_HEADS):
        dst_ref[row0:row0 + n, h * V_AUG:h * V_AUG + HEAD_DIM] = v[:, h * HEAD_DIM:(h + 1) * HEAD_DIM].astype(BF16)
        dst_ref[row0:row0 + n, h * V_AUG + HEAD_DIM:(h + 1) * V_AUG] = one_col


def _qk_post_kernel(q_ref, k_ref, v_ref, qi_ref, kw_ref, c128_ref, sl128_ref, sh128_ref,
                    c64_ref, sl64_ref, sh64_ref, qg_ref, kg_ref, ig_ref,
                    qo_ref, kf_ref, kb_ref, vf_ref, vb_ref, qio_ref, kif_ref, kib_ref, wo_ref):
    c128, sl128, sh128 = c128_ref[...], sl128_ref[...], sh128_ref[...]
    c64, sl64, sh64 = c64_ref[...], sl64_ref[...], sh64_ref[...]
    half128 = HEAD_DIM // 8
    half64 = IDX_DIM // 8
    for h in range(N_HEADS):
        sl = slice(h * LANES, (h + 1) * LANES)
        qo_ref[:, sl] = _rope(_head_norm(q_ref[:, sl], qg_ref[...]), c128, sl128, sh128, half128).astype(BF16)
    for h in range(N_KV_HEADS):
        sl = slice(h * LANES, (h + 1) * LANES)
        kk = _rope(_head_norm(k_ref[:, sl], kg_ref[...]), c128, sl128, sh128, half128)
        kf_ref[:, sl] = kk
        kb_ref[:, sl] = kk.astype(BF16)
    v = v_ref[...]
    vf_ref[...] = v
    _store_v_aug(vb_ref, 0, v)
    lane = lax.broadcasted_iota(I32, c64.shape, 1)
    low = lane < IDX_DIM
    for p in range(IDX_HEADS // 2):
        x = _rope(qi_ref[:, p * LANES:(p + 1) * LANES], c64, sl64, sh64, half64)
        qio_ref[:, (2 * p) * LANES:(2 * p + 1) * LANES] = jnp.where(low, x, 0.0).astype(BF16)
        qio_ref[:, (2 * p + 1) * LANES:(2 * p + 2) * LANES] = jnp.where(low, pltpu.roll(x, IDX_DIM, 1), 0.0).astype(BF16)
    kw = kw_ref[...]
    ms = jnp.sum(jnp.where(low, kw * kw, 0.0), axis=-1, keepdims=True) * (1.0 / IDX_DIM)
    ki = _rope(kw * lax.rsqrt(ms + EPS) * ig_ref[...], c64, sl64, sh64, half64)
    kif_ref[...] = ki[:, :IDX_DIM]
    kib_ref[...] = jnp.where(low, ki, 0.0).astype(BF16)
    wo_ref[...] = (pltpu.roll(kw, IDX_DIM, 1) * IDX_HEADS ** -0.5) * IDX_DIM ** -0.5


def _rope_tables(pos, head_dim):
    r = head_dim // 4
    half = r // 2
    inv = ROPE_THETA ** (-jnp.arange(half, dtype=F32) * 2.0 / r)
    ang = pos.astype(F32)[:, None] * inv[None, :]
    cos, sin = jnp.cos(ang), jnp.sin(ang)
    n = pos.shape[0]
    zh = jnp.zeros((n, half), F32)
    rest = head_dim - r
    c = jnp.concatenate([cos, cos, jnp.ones((n, rest), F32)], axis=-1)
    s_lo = jnp.concatenate([-sin, zh, jnp.zeros((n, rest), F32)], axis=-1)
    s_hi = jnp.concatenate([zh, sin, jnp.zeros((n, rest), F32)], axis=-1)
    rep = LANES // head_dim
    return tuple(jnp.tile(t, (1, rep)) for t in (c, s_lo, s_hi))


QK_TM = 512


def qk_post(proj, table_pos, table_block, q_gain, k_gain, ik_gain):
    tm = QK_TM
    n_tok = proj.shape[0]
    t128 = _rope_tables(table_pos, HEAD_DIM)
    t64 = _rope_tables(table_pos, IDX_DIM)
    ik_gain128 = jnp.concatenate([ik_gain, jnp.zeros((LANES - IDX_DIM,), F32)])[None, :]

    def col(width, start):
        return pl.BlockSpec((tm, width), lambda i: (i, start // width))

    def row(width):
        return pl.BlockSpec((tm, width), lambda i: (i, 0))

    table = pl.BlockSpec((tm, LANES), lambda i: (table_block(i), 0))
    gain = pl.BlockSpec((1, LANES), lambda i: (0, 0))
    return pl.pallas_call(
        _qk_post_kernel,
        grid=(n_tok // tm,),
        in_specs=[col(ATTN_WIDTH, COL_Q), col(KV_WIDTH, COL_K), col(KV_WIDTH, COL_V), col(IDX_HEADS * IDX_DIM, COL_QI),
                  col(LANES, COL_KIWI)] + [table] * 6 + [gain] * 3,
        out_specs=[row(ATTN_WIDTH), row(KV_WIDTH), row(KV_WIDTH), row(KV_WIDTH), row(N_KV_HEADS * V_AUG), row(IDX_HEADS * LANES),
                   row(IDX_DIM), row(LANES), row(LANES)],
        out_shape=[jax.ShapeDtypeStruct((n_tok, ATTN_WIDTH), BF16),
                   jax.ShapeDtypeStruct((n_tok, KV_WIDTH), F32), jax.ShapeDtypeStruct((n_tok, KV_WIDTH), BF16),
                   jax.ShapeDtypeStruct((n_tok, KV_WIDTH), F32), jax.ShapeDtypeStruct((n_tok, N_KV_HEADS * V_AUG), BF16),
                   jax.ShapeDtypeStruct((n_tok, IDX_HEADS * LANES), BF16),
                   jax.ShapeDtypeStruct((n_tok, IDX_DIM), F32), jax.ShapeDtypeStruct((n_tok, LANES), BF16),
                   jax.ShapeDtypeStruct((n_tok, LANES), F32)],
        compiler_params=_params(("arbitrary",)),
        name="qk_post",
    )(proj, proj, proj, proj, proj, *t128, *t64, q_gain[None, :], k_gain[None, :], ik_gain128)


def _gelu_tanh(x):
    return 0.5 * x * (1.0 + jnp.tanh(np.float32(np.sqrt(2.0 / np.pi)) * (x + 0.044715 * (x * x * x))))


def _ssm_kernel(u_ref, wb_ref, wc_ref, cst_ref, d_ref, h0_ref, g_ref, sre_ref, sim_ref,
                er_ref, ei_ref, car_ref, *, tc):
    c = pl.program_id(2)

    @pl.when(c == 0)
    def _():
        car_ref[...] = h0_ref[...]

    u = u_ref[...]
    e = _dot(u.astype(BF16), wb_ref[...])
    er_ref[...] = e[:, :SSM_SB]
    ei_ref[...] = e[:, SSM_SB:]

    def body(r, carry):
        cr, ci = carry
        i0 = pl.multiple_of(r * SUBLANES, SUBLANES)
        xr = er_ref[pl.ds(i0, SUBLANES), :]
        xi = ei_ref[pl.ds(i0, SUBLANES), :]
        for n, k in enumerate((1, 2, 4)):
            ar, ai = cst_ref[2 * n], cst_ref[2 * n + 1]
            sr, si = pltpu.roll(xr, k, 0), pltpu.roll(xi, k, 0)
            xr, xi = xr + ar * sr - ai * si, xi + ar * si + ai * sr
        pr, pi_ = cst_ref[6], cst_ref[7]
        xr, xi = xr + pr * cr - pi_ * ci, xi + pr * ci + pi_ * cr
        er_ref[pl.ds(i0, SUBLANES), :] = xr
        ei_ref[pl.ds(i0, SUBLANES), :] = xi
        return xr[SUBLANES - 1:SUBLANES, :], xi[SUBLANES - 1:SUBLANES, :]

    cr, ci = lax.fori_loop(0, tc // SUBLANES, body, (car_ref[0:1, :], car_ref[1:2, :]))
    car_ref[0:1, :] = cr
    car_ref[1:2, :] = ci

    y = _dot(er_ref[...].astype(BF16), wc_ref[0]) - _dot(ei_ref[...].astype(BF16), wc_ref[1])
    y = y + d_ref[...] * u
    g_ref[...] = _gelu_tanh(y).astype(BF16)

    @pl.when(c == pl.num_programs(2) - 1)
    def _():
        sre_ref[...] = cr
        sim_ref[...] = ci


def _ssm_weights(a_re, a_im, log_dt, b_re, b_im, c_re, c_im):
    lam_re, lam_im = a_re, a_im
    dt = jnp.exp(log_dt)[:, None]
    mag = jnp.exp(lam_re * dt)
    lb_re, lb_im = mag * jnp.cos(lam_im * dt), mag * jnp.sin(lam_im * dt)
    den = lam_re * lam_re + lam_im * lam_im
    num_re = lb_re - 1.0
    z_re = (num_re * lam_re + lb_im * lam_im) / den
    z_im = (lb_im * lam_re - num_re * lam_im) / den
    zb_re = z_re[:, :, None] * b_re - z_im[:, :, None] * b_im
    zb_im = z_re[:, :, None] * b_im + z_im[:, :, None] * b_re
    eye = jnp.eye(8, dtype=F32)

    def blockdiag_in(w):
        return jnp.einsum('jgph,gk->jghkp', w.reshape(SSM_LB, 8, SSM_STATE, SSM_GROUP), eye).reshape(SSM_LB, LANES, SSM_SB)

    def blockdiag_out(w):
        return jnp.einsum('jghp,gk->jkpgh', w.reshape(SSM_LB, 8, SSM_GROUP, SSM_STATE), eye).reshape(SSM_LB, SSM_SB, LANES)

    wb = jnp.concatenate([blockdiag_in(zb_re), blockdiag_in(zb_im)], axis=-1).astype(BF16)
    wc = jnp.stack([blockdiag_out(c_re), blockdiag_out(c_im)], axis=1).astype(BF16)

    pw = [(lb_re, lb_im)]
    for _ in range(7):
        pr, pi_ = pw[-1]
        pw.append((pr * lb_re - pi_ * lb_im, pr * lb_im + pi_ * lb_re))
    rows = jnp.arange(SUBLANES)[:, None]

    def lane(x):
        return x.reshape(SSM_LB, 1, SSM_SB)

    cst = []
    for k in (1, 2, 4):
        for part in pw[k - 1]:
            cst.append(jnp.where(rows >= k, lane(part), 0.0))
    cst.append(jnp.concatenate([lane(pw[r][0]) for r in range(SUBLANES)], axis=1))
    cst.append(jnp.concatenate([lane(pw[r][1]) for r in range(SUBLANES)], axis=1))
    cst = jnp.stack(cst, axis=1)
    return wb, wc, cst


def ssm(proj, ssm_w, d_skip, h0, *, n_batch, seq, tc, row0):
    wb, wc, cst = ssm_w
    n_chunks = seq // tc
    blk0 = row0 // tc
    n_tok = n_batch * seq
    state_shape = jax.ShapeDtypeStruct((n_batch, SSM_LB, 1, SSM_SB), F32)
    state_spec = pl.BlockSpec((None, None, 1, SSM_SB), lambda b, j, c: (b, j, 0, 0))
    g, s_re, s_im = pl.pallas_call(
        functools.partial(_ssm_kernel, tc=tc),
        grid=(n_batch, SSM_LB, n_chunks),
        in_specs=[pl.BlockSpec((tc, LANES), lambda b, j, c: (blk0 + b * n_chunks + c, j)),
                  pl.BlockSpec((None, LANES, 2 * SSM_SB), lambda b, j, c: (j, 0, 0)),
                  pl.BlockSpec((None, 2, SSM_SB, LANES), lambda b, j, c: (j, 0, 0, 0)),
                  pl.BlockSpec((None, 8, SUBLANES, SSM_SB), lambda b, j, c: (j, 0, 0, 0)),
                  pl.BlockSpec((1, LANES), lambda b, j, c: (0, j)),
                  pl.BlockSpec((None, None, 2, SSM_SB), lambda b, j, c: (b, j, 0, 0))],
        out_specs=[pl.BlockSpec((tc, LANES), lambda b, j, c: (b * n_chunks + c, j)), state_spec, state_spec],
        out_shape=[jax.ShapeDtypeStruct((n_tok, SSM_WIDTH), BF16), state_shape, state_shape],
        scratch_shapes=[pltpu.VMEM((tc, SSM_SB), F32), pltpu.VMEM((tc, SSM_SB), F32), pltpu.VMEM((2, SSM_SB), F32)],
        compiler_params=_params(("arbitrary", "arbitrary", "arbitrary")),
        name="ssm",
    )(proj, wb, wc, cst, d_skip[None, :], h0)
    return g, s_re.reshape(n_batch, SSM_GROUPS, SSM_STATE), s_im.reshape(n_batch, SSM_GROUPS, SSM_STATE)


def _ssm_step_kernel(u_ref, wb_ref, wc_ref, cst_ref, d_ref, h0_ref, g_ref, sre_ref, sim_ref, er_ref, ei_ref, *, seq):
    n_seq = h0_ref.shape[1]
    u = u_ref[...]
    e = _dot(u.astype(BF16), wb_ref[...])
    n_lt = SSM_SB // LANES
    y = d_ref[...] * u
    for lt in range(n_lt):
        sl = slice(lt * LANES, (lt + 1) * LANES)
        er_ref[...] = e[:, lt * LANES:(lt + 1) * LANES]
        ei_ref[...] = e[:, SSM_SB + lt * LANES:SSM_SB + (lt + 1) * LANES]
        lr, li = cst_ref[6, 0:1, sl], cst_ref[7, 0:1, sl]
        sr, si = h0_ref[0, :, sl], h0_ref[1, :, sl]
        for t in range(seq):
            rows = pl.ds(t, n_seq, stride=seq)
            sr, si = lr * sr - li * si + er_ref[rows, :], lr * si + li * sr + ei_ref[rows, :]
            er_ref[rows, :] = sr
            ei_ref[rows, :] = si
        y = y + (_dot(er_ref[...].astype(BF16), wc_ref[0, sl, :]) - _dot(ei_ref[...].astype(BF16), wc_ref[1, sl, :]))
        sre_ref[:, sl] = sr
        sim_ref[:, sl] = si
    g_ref[...] = _gelu_tanh(y).astype(BF16)


def ssm_step(proj, ssm_w, d_skip, h0, *, n_batch, seq, row0):
    wb, wc, cst = ssm_w
    n_tok = n_batch * seq
    assert row0 % n_tok == 0
    state_shape = jax.ShapeDtypeStruct((SSM_LB, n_batch, SSM_SB), F32)
    state_spec = pl.BlockSpec((None, n_batch, SSM_SB), lambda j: (j, 0, 0))
    g, s_re, s_im = pl.pallas_call(
        functools.partial(_ssm_step_kernel, seq=seq),
        grid=(SSM_LB,),
        in_specs=[pl.BlockSpec((n_tok, LANES), lambda j: (row0 // n_tok, j)),
                  pl.BlockSpec((None, LANES, 2 * SSM_SB), lambda j: (j, 0, 0)),
                  pl.BlockSpec((None, 2, SSM_SB, LANES), lambda j: (j, 0, 0, 0)),
                  pl.BlockSpec((None, 8, SUBLANES, SSM_SB), lambda j: (j, 0, 0, 0)),
                  pl.BlockSpec((1, LANES), lambda j: (0, j)),
                  pl.BlockSpec((None, 2, n_batch, SSM_SB), lambda j: (j, 0, 0, 0))],
        out_specs=[pl.BlockSpec((n_tok, LANES), lambda j: (0, j)), state_spec, state_spec],
        out_shape=[jax.ShapeDtypeStruct((n_tok, SSM_WIDTH), BF16), state_shape, state_shape],
        scratch_shapes=[pltpu.VMEM((n_tok, LANES), F32), pltpu.VMEM((n_tok, LANES), F32)],
        compiler_params=_params(("arbitrary",)),
        name="ssm_step",
    )(proj, wb, wc, cst, d_skip[None, :], h0)

    def per_seq(s):
        return s.transpose(1, 0, 2).reshape(n_batch, SSM_GROUPS, SSM_STATE)

    return g, per_seq(s_re), per_seq(s_im)


def _row_sum(x):
    return jnp.sum(x, axis=1, keepdims=True)


def _row_count(mask):
    return _row_sum(jnp.where(mask, 1, 0))


I16 = jnp.int16
I16_MIN = -2 ** 15


def _count16(ref, cand, compare):
    accs = [None] * 4
    for t in range(ref.shape[1] // LANES):
        x = jnp.where(compare(ref[:, t * LANES:(t + 1) * LANES], cand), I16(1), I16(0))
        accs[t % 4] = x if accs[t % 4] is None else accs[t % 4] + x
    accs = [a for a in accs if a is not None]
    total = accs[0]
    for a in accs[1:]:
        total = total + a
    return _row_sum(total.astype(I32))


def _bisect16(ref, target):
    def step(i, base):
        cand = base + lax.shift_left(np.int32(1), np.int32(15) - i)
        cnt = _count16(ref, cand.astype(I16), lambda a, b: a >= b)
        return jnp.where(cnt >= target, cand, base)
    return lax.fori_loop(0, 16, step, jnp.full((ref.shape[0], 1), I16_MIN, I32))


def _stack_heads(ref, heads):
    return jnp.concatenate([ref[:, h * LANES:(h + 1) * LANES] for h in heads], axis=0)


def _dsa_body(q_ref, qi_ref, wi_ref, k_ref, v_ref, ki_ref, o_ref, key_ref, bias_ref, hi_ref, lo_ref,
              *, q_pos_first, s_valid, n_sel):
    bq, n_keys = key_ref.shape
    col = lax.broadcasted_iota(I32, (bq, n_keys), 1)
    qpos = q_pos_first + lax.broadcasted_iota(I32, (bq, 1), 0)
    allowed = col < jnp.minimum((qpos // CHUNK + 1) * CHUNK, s_valid)

    ki = ki_ref[...]
    score = None
    for h0 in range(0, IDX_HEADS, 4):
        d = _dot_nt(_stack_heads(qi_ref, range(h0, h0 + 4)), ki)
        for j in range(4):
            t = jnp.maximum(d[j * bq:(j + 1) * bq], 0.0) * wi_ref[:, h0 + j:h0 + j + 1]
            score = t if score is None else score + t
    score = jnp.where(score == 0.0, 0.0, score)
    bits = pltpu.bitcast(score, I32)
    key = jnp.where(bits < 0, bits ^ np.int32(0x7FFFFFFF), bits)
    key = jnp.where(allowed, key, KEY_NEG_INF)
    key_ref[...] = key

    hi_ref[...] = (key >> 16).astype(I16)
    lo_ref[...] = ((key & 0xFFFF) + I16_MIN).astype(I16)
    thr_hi = _bisect16(hi_ref, n_sel)
    thr_hi16 = thr_hi.astype(I16)
    need_lo = n_sel - _count16(hi_ref, thr_hi16, lambda a, b: a > b)
    lo_ref[...] = jnp.where(hi_ref[...] == thr_hi16, lo_ref[...], I16(I16_MIN))
    thr_lo = _bisect16(lo_ref, need_lo)
    thr = lax.shift_left(thr_hi, np.int32(16)) + (thr_lo - I16_MIN)
    thr = jnp.maximum(thr, KEY_NEG_INF)

    key = key_ref[...]
    need = n_sel - _row_count(key > thr)
    n_eq = _row_count(key == thr)
    n_bits = int(n_keys - 1).bit_length()

    def tie_cut():
        def step(i, j0):
            cand = j0 + lax.shift_left(np.int32(1), np.int32(n_bits - 1) - i)
            cnt = _row_sum(jnp.where(key_ref[...] == thr, jnp.where(col < cand, 1, 0), 0))
            return jnp.where(cnt < need, cand, j0)
        return lax.fori_loop(0, n_bits, step, jnp.zeros((bq, 1), I32))

    split = jnp.max(jnp.where(n_eq > need, 1, 0)) > 0
    j_last = lax.cond(split, tie_cut, lambda: jnp.full((bq, 1), n_keys, I32))
    tie_bias = jnp.where(thr == KEY_NEG_INF, -jnp.inf, 0.0)
    bias_ref[...] = jnp.where(key > thr, 0.0,
                              jnp.where(key == thr, jnp.where(col <= j_last, tie_bias, -jnp.inf), -jnp.inf))

    c = np.float32(HEAD_DIM ** -0.5 * np.log2(np.e))
    for kv in range(N_KV_HEADS):
        heads = range(kv * KV_GROUP, (kv + 1) * KV_GROUP)
        s_all = _dot_nt(_stack_heads(q_ref, heads), k_ref[:, kv * HEAD_DIM:(kv + 1) * HEAD_DIM])
        ps = []
        for g in range(KV_GROUP):
            s = s_all[g * bq:(g + 1) * bq] + bias_ref[...]
            m = jnp.max(s, axis=1, keepdims=True)
            ps.append(jnp.exp2((s - m) * c).astype(BF16))
        pv = _dot(jnp.concatenate(ps, axis=0), v_ref[:, kv * V_AUG:(kv + 1) * V_AUG])
        for g, h in enumerate(heads):
            o = pv[g * bq:(g + 1) * bq]
            o_ref[:, h * HEAD_DIM:(h + 1) * HEAD_DIM] = (o[:, :HEAD_DIM] / o[:, HEAD_DIM:HEAD_DIM + 1]).astype(BF16)


def _dsa_scratch(bq, n_keys):
    return [pltpu.VMEM((bq, n_keys), I32), pltpu.VMEM((bq, n_keys), F32),
            pltpu.VMEM((bq, n_keys), I16), pltpu.VMEM((bq, n_keys), I16)]


def _dsa_kernel(q_ref, qi_ref, wi_ref, k_ref, v_ref, ki_ref, o_ref, *scratch, q_pos0, s_valid, n_sel):
    bq = scratch[0].shape[0]
    _dsa_body(q_ref, qi_ref, wi_ref, k_ref, v_ref, ki_ref, o_ref, *scratch,
              q_pos_first=q_pos0 + pl.program_id(1) * bq, s_valid=s_valid, n_sel=n_sel)


def dsa(q, qi, wi, k, v, ki, *, bq, q_blk0, n_qblk, n_keys, n_sel):
    n_batch, seq = q.shape[:2]

    def qspec(width):
        return pl.BlockSpec((None, bq, width), lambda b, i: (b, q_blk0 + i, 0))

    def kspec(width):
        return pl.BlockSpec((None, n_keys, width), lambda b, i: (b, 0, 0))

    return pl.pallas_call(
        functools.partial(_dsa_kernel, q_pos0=q_blk0 * bq, s_valid=seq, n_sel=n_sel),
        grid=(n_batch, n_qblk),
        in_specs=[qspec(ATTN_WIDTH), qspec(IDX_HEADS * LANES), qspec(LANES), kspec(KV_WIDTH), kspec(N_KV_HEADS * V_AUG),
                  kspec(LANES)],
        out_specs=pl.BlockSpec((None, bq, ATTN_WIDTH), lambda b, i: (b, i, 0)),
        out_shape=jax.ShapeDtypeStruct((n_batch, n_qblk * bq, ATTN_WIDTH), BF16),
        scratch_shapes=_dsa_scratch(bq, n_keys),
        compiler_params=_params(("arbitrary", "arbitrary")),
        name="dsa",
    )(q, qi, wi, k, v, ki)


def _dsa_step_kernel(q_ref, qi_ref, wi_ref, ck_ref, cv_ref, cki_ref, nk_ref, nv_ref, nki_ref, o_ref,
                     k_buf, v_buf, ki_buf, *scratch, past, n_sel):
    ts = nk_ref.shape[0]
    n_keys = k_buf.shape[0]
    k_buf[0:past, :] = ck_ref[...].astype(BF16)
    _store_v_aug(v_buf, 0, cv_ref[...])
    for buf, new in ((k_buf, nk_ref), (v_buf, nv_ref)):
        buf[past:past + ts, :] = new[...]
        buf[past + ts:n_keys, :] = jnp.zeros((n_keys - past - ts, buf.shape[1]), BF16)
    ki_buf[0:past, 0:IDX_DIM] = cki_ref[...].astype(BF16)
    ki_buf[0:past, IDX_DIM:LANES] = jnp.zeros((past, LANES - IDX_DIM), BF16)
    ki_buf[past:past + ts, :] = nki_ref[...]
    ki_buf[past + ts:n_keys, :] = jnp.zeros((n_keys - past - ts, LANES), BF16)
    _dsa_body(q_ref, qi_ref, wi_ref, k_buf, v_buf, ki_buf, o_ref, *scratch,
              q_pos_first=past, s_valid=past + ts, n_sel=n_sel)


def dsa_step(q, qi, wi, cache_k, cache_v, cache_ki, k_new, v_new, ki_new, *, n_sel):
    n_batch, ts = q.shape[:2]
    past = cache_k.shape[1]
    n_keys = -(-(past + ts) // LANES) * LANES

    def spec(rows, width):
        return pl.BlockSpec((None, rows, width), lambda b: (b, 0, 0))

    return pl.pallas_call(
        functools.partial(_dsa_step_kernel, past=past, n_sel=n_sel),
        grid=(n_batch,),
        in_specs=[spec(ts, ATTN_WIDTH), spec(ts, IDX_HEADS * LANES), spec(ts, LANES),
                  spec(past, KV_WIDTH), spec(past, KV_WIDTH), spec(past, IDX_DIM),
                  spec(ts, KV_WIDTH), spec(ts, N_KV_HEADS * V_AUG), spec(ts, LANES)],
        out_specs=spec(ts, ATTN_WIDTH),
        out_shape=jax.ShapeDtypeStruct((n_batch, ts, ATTN_WIDTH), BF16),
        scratch_shapes=[pltpu.VMEM((n_keys, KV_WIDTH), BF16), pltpu.VMEM((n_keys, N_KV_HEADS * V_AUG), BF16),
                        pltpu.VMEM((n_keys, LANES), BF16),
                        *_dsa_scratch(ts, n_keys)],
        compiler_params=_params(("arbitrary",)),
        name="dsa_step",
    )(q, qi, wi, cache_k, cache_v, cache_ki, k_new, v_new, ki_new)


def _merge_kernel(g_ref, a_ref, ga_ref, gb_ref, wv_ref, wg_ref, wb_ref, o_ref):
    g = g_ref[...]
    branch_a = _dot(g, wv_ref[...]) * jax.nn.sigmoid(_dot(g, wg_ref[...]))
    branch_b = _dot(a_ref[...], wb_ref[...])
    merged = jax.nn.sigmoid(ga_ref[...]) * branch_a + jax.nn.sigmoid(gb_ref[...]) * branch_b
    o_ref[...] = merged.astype(BF16)


def merge(g, attn, proj, w_val, w_gate, w_branch, *, tm=1024, tn=512):
    n_tok = g.shape[0]
    nj = D_MODEL // tn

    def wspec():
        return pl.BlockSpec((SSM_WIDTH, tn), lambda i, j: (0, j))

    return pl.pallas_call(
        _merge_kernel,
        grid=(n_tok // tm, nj),
        in_specs=[pl.BlockSpec((tm, SSM_WIDTH), lambda i, j: (i, 0)),
                  pl.BlockSpec((tm, ATTN_WIDTH), lambda i, j: (i, 0)),
                  pl.BlockSpec((tm, tn), lambda i, j: (i, COL_GA // tn + j)),
                  pl.BlockSpec((tm, tn), lambda i, j: (i, COL_GB // tn + j)),
                  wspec(), wspec(), wspec()],
        out_specs=pl.BlockSpec((tm, tn), lambda i, j: (i, j)),
        out_shape=jax.ShapeDtypeStruct((n_tok, D_MODEL), BF16),
        compiler_params=_params(("arbitrary", "arbitrary")),
        name="merge",
    )(g, attn, proj, proj, w_val, w_gate, w_branch)


ROUTER_COLS = N_EXPERT_GROUPS + N_EXPERTS
MOE_TM = 256


def _first_lane_of_max(x, lane_f):
    m = jnp.max(x, axis=1, keepdims=True)
    return m, jnp.min(jnp.where(x == m, lane_f, float(LANES)), axis=1, keepdims=True)


def _out_proj_kernel(x_ref, m_ref, wo_ref, gn_ref, wrh_ref, wrl_ref, br_ref, cin_ref,
                     h_ref, hn_ref, ri_ref, rw_ref, cnt_ref, carry_ref):
    @pl.when(pl.program_id(0) == 0)
    def _():
        carry_ref[...] = cin_ref[...]

    h = x_ref[...] + _dot(m_ref[...], wo_ref[...])
    h_ref[...] = h
    ms = jnp.mean(h * h, axis=-1, keepdims=True)
    hn = h * lax.rsqrt(ms + EPS) * gn_ref[...]
    hn_ref[...] = hn
    hh, hl = _split_bf16(hn)
    wrh = wrh_ref[...]
    lg = _dot(hh, wrh) + _dot(hl, wrh) + _dot(hh, wrl_ref[...]) + br_ref[...]

    tm = lg.shape[0]
    lane = lax.broadcasted_iota(I32, lg.shape, 1)
    lane_f = lane.astype(F32)
    ninf = -jnp.inf
    gl = jnp.where(lane < N_EXPERT_GROUPS, lg, ninf)
    gmax, gsel = _first_lane_of_max(gl, lane_f)
    g_w = 1.0 / jnp.sum(jnp.exp(gl - gmax), axis=1, keepdims=True)
    lo = N_EXPERT_GROUPS + EXPERTS_PER_GROUP * gsel
    el = jnp.where(lane_f >= lo, jnp.where(lane_f < lo + EXPERTS_PER_GROUP, lg, ninf), ninf)
    v1, i1 = _first_lane_of_max(el, lane_f)
    el2 = jnp.where(lane_f == i1, ninf, el)
    v2, i2 = _first_lane_of_max(el2, lane_f)
    t = jnp.exp(v2 - v1)
    s1 = 1.0 / (1.0 + t)
    w1 = s1 * g_w
    w2 = (t * s1) * g_w

    m1 = jnp.where(lane_f == i1, 1.0, 0.0)
    m2 = jnp.where(lane_f == i2, 1.0, 0.0)
    both = m1 + m2
    tri = jnp.where(lax.broadcasted_iota(I32, (tm, tm), 0) > lax.broadcasted_iota(I32, (tm, tm), 1), 1.0, 0.0)
    before = _dot(tri.astype(BF16), both.astype(BF16)) + carry_ref[...]
    r1 = jnp.sum(before * m1, axis=1, keepdims=True)
    r2 = jnp.sum(before * m2, axis=1, keepdims=True)
    carry_ref[...] = carry_ref[...] + jnp.sum(both, axis=0, keepdims=True)
    cnt_ref[...] = carry_ref[...]
    e1 = i1 - float(N_EXPERT_GROUPS)
    e2 = i2 - float(N_EXPERT_GROUPS)
    fields = jnp.where(lane == 0, e1, jnp.where(lane == 1, e2, jnp.where(lane == 2, r1, jnp.where(lane == 3, r2, 0.0))))
    ri_ref[...] = fields.T[0:SUBLANES, :].astype(I32)
    rw_ref[...] = jnp.where(lane == 0, w1, jnp.where(lane == 1, w2, 0.0))


def _router_weights(w_router_group, b_router_group, w_router_expert, b_router_expert):
    wr = jnp.concatenate([w_router_group, w_router_expert, jnp.zeros((D_MODEL, LANES - ROUTER_COLS), F32)], axis=1)
    wr_hi = wr.astype(BF16)
    wr_lo = (wr - wr_hi.astype(F32)).astype(BF16)
    br = jnp.concatenate([b_router_group, b_router_expert, jnp.zeros((LANES - ROUTER_COLS,), F32)])[None, :]
    return wr_hi, wr_lo, br


def out_proj(x, merged, w_out, ffn_gain, router_w, counts_in, *, tm=256):
    n_tok = x.shape[0]
    wr_hi, wr_lo, br = router_w

    def row(width):
        return pl.BlockSpec((tm, width), lambda i: (i, 0))

    def const(shape):
        return pl.BlockSpec(shape, lambda i: (0, 0), pipeline_mode=pl.Buffered(1))

    return pl.pallas_call(
        _out_proj_kernel,
        grid=(n_tok // tm,),
        in_specs=[row(D_MODEL), row(D_MODEL), const((D_MODEL, D_MODEL)), const((1, D_MODEL)),
                  const((D_MODEL, LANES)), const((D_MODEL, LANES)), const((1, LANES)), const((1, LANES))],
        out_specs=[row(D_MODEL), row(D_MODEL), pl.BlockSpec((SUBLANES, tm), lambda i: (0, i)), row(LANES),
                   pl.BlockSpec((1, LANES), lambda i: (0, 0))],
        out_shape=[jax.ShapeDtypeStruct((n_tok, D_MODEL), F32), jax.ShapeDtypeStruct((n_tok, D_MODEL), F32),
                   jax.ShapeDtypeStruct((SUBLANES, n_tok), I32), jax.ShapeDtypeStruct((n_tok, LANES), F32),
                   jax.ShapeDtypeStruct((1, LANES), F32)],
        scratch_shapes=[pltpu.VMEM((1, LANES), F32)],
        compiler_params=_params(("arbitrary",)),
        name="out_proj",
    )(x, merged, w_out, ffn_gain[None, :], wr_hi, wr_lo, br, counts_in)


def _block_layout(counts):
    padded = (counts + MOE_TM - 1) // MOE_TM * MOE_TM
    pad_end = jnp.cumsum(padded).astype(I32)
    pad_start = pad_end - padded
    n_used = pad_end[-1] // MOE_TM
    return pad_start, pad_end, n_used


def _moe_rows(n_tok):
    return -(-(n_tok * TOP_K + N_EXPERTS * (MOE_TM - 1)) // MOE_TM) * MOE_TM


DISPATCH_TM = 512


def _wait_rows(src_hbm, dst, sem, n_rows):
    pltpu.make_async_copy(src_hbm.at[pl.ds(0, n_rows)], dst, sem).wait()


def _dispatch_kernel(d0_ref, d1_ref, pe_ref, cnt_ref, nu_ref, hna_ref, hnb_ref, xs_hbm, zbuf, sem, semz,
                     *, n_blocks, a_tiles):
    i = pl.program_id(0)

    def zero_block(row0):
        return pltpu.make_async_copy(zbuf, xs_hbm.at[pl.ds(pl.multiple_of(row0, MOE_TM), MOE_TM)], semz)

    @pl.when(i == 0)
    def _():
        zbuf[...] = jnp.zeros_like(zbuf)
        for start in (True, False):
            for e in range(N_EXPERTS):
                @pl.when(cnt_ref[e] > 0)
                def _():
                    cp = zero_block(pe_ref[e] - MOE_TM)
                    cp.start() if start else cp.wait()

            def tail(b, c):
                cp = zero_block(b * MOE_TM)
                cp.start() if start else cp.wait()
                return c
            lax.fori_loop(nu_ref[0], n_blocks, tail, 0)

    base = i * DISPATCH_TM

    def scatter(hn_ref):
        def body(r, c):
            src = hn_ref.at[pl.ds(r, 1)]
            pltpu.make_async_copy(src, xs_hbm.at[pl.ds(d0_ref[base + r], 1)], sem).start()
            pltpu.make_async_copy(src, xs_hbm.at[pl.ds(d1_ref[base + r], 1)], sem).start()
            return c
        lax.fori_loop(0, DISPATCH_TM, body, 0, unroll=8)
        for _ in range(TOP_K):
            pltpu.make_async_copy(hn_ref, xs_hbm.at[pl.ds(0, DISPATCH_TM)], sem).wait()

    @pl.when(i < a_tiles)
    def _():
        scatter(hna_ref)

    @pl.when(i >= a_tiles)
    def _():
        scatter(hnb_ref)


def dispatch(hn_a, hn_b, dest0, dest1, pad_end, counts, n_used):
    a_tiles, b_tiles = hn_a.shape[0] // DISPATCH_TM, hn_b.shape[0] // DISPATCH_TM
    rows = _moe_rows(hn_a.shape[0] + hn_b.shape[0])
    grid_spec = pltpu.PrefetchScalarGridSpec(
        num_scalar_prefetch=5,
        grid=(a_tiles + b_tiles,),
        in_specs=[pl.BlockSpec((DISPATCH_TM, D_MODEL), lambda i, *_: (jnp.minimum(i, a_tiles - 1), 0)),
                  pl.BlockSpec((DISPATCH_TM, D_MODEL), lambda i, *_: (jnp.maximum(i - a_tiles, 0), 0))],
        out_specs=pl.BlockSpec(memory_space=pl.ANY),
        scratch_shapes=[pltpu.VMEM((MOE_TM, D_MODEL), F32), pltpu.SemaphoreType.DMA(()), pltpu.SemaphoreType.DMA(())],
    )
    return pl.pallas_call(
        functools.partial(_dispatch_kernel, n_blocks=rows // MOE_TM, a_tiles=a_tiles),
        grid_spec=grid_spec,
        out_shape=jax.ShapeDtypeStruct((rows, D_MODEL), F32),
        compiler_params=_params(("arbitrary",)),
        name="dispatch",
    )(dest0, dest1, pad_end, counts, n_used, hn_a, hn_b)


def _moe_kernel(blk_e_ref, nu_ref, xs_ref, wg_ref, wu_ref, wd_ref, ys_ref):
    i = pl.program_id(0)

    @pl.when(i < nu_ref[0])
    def _():
        x = xs_ref[...].astype(BF16)
        hg = _dot(x, wg_ref[...])
        hu = _dot(x, wu_ref[...])
        hmid = (jax.nn.silu(hg) * hu).astype(BF16)
        ys_ref[...] = _dot(hmid, wd_ref[...])

    @pl.when(i >= nu_ref[0])
    def _():
        ys_ref[...] = jnp.zeros_like(ys_ref)


def moe(xs, blk_e, n_used, w_gate, w_up, w_down):
    rows = xs.shape[0]
    grid_spec = pltpu.PrefetchScalarGridSpec(
        num_scalar_prefetch=2,
        grid=(rows // MOE_TM,),
        in_specs=[pl.BlockSpec((MOE_TM, D_MODEL), lambda i, be, nu: (jnp.minimum(i, nu[0] - 1), 0)),
                  pl.BlockSpec((None, D_MODEL, EXPERT_FF), lambda i, be, nu: (be[i], 0, 0)),
                  pl.BlockSpec((None, D_MODEL, EXPERT_FF), lambda i, be, nu: (be[i], 0, 0)),
                  pl.BlockSpec((None, EXPERT_FF, D_MODEL), lambda i, be, nu: (be[i], 0, 0))],
        out_specs=pl.BlockSpec((MOE_TM, D_MODEL), lambda i, be, nu: (i, 0)),
    )
    return pl.pallas_call(
        _moe_kernel,
        grid_spec=grid_spec,
        out_shape=jax.ShapeDtypeStruct((rows, D_MODEL), F32),
        compiler_params=_params(("arbitrary",)),
        name="moe",
    )(blk_e, n_used, xs, w_gate, w_up, w_down)


def _gather_rows(idx_ref, idx0, src_hbm, dst, sem, n_rows):
    def body(r, carry):
        t = idx_ref[idx0 + r]
        pltpu.make_async_copy(src_hbm.at[pl.ds(t, 1)], dst.at[pl.ds(r, 1)], sem).start()
        return carry
    lax.fori_loop(0, n_rows, body, 0, unroll=8)


def _combine_kernel(r0_ref, r1_ref, ys_hbm, h_ref, w_ref, o_ref, buf, sem, *, tm, tok0):
    i = pl.program_id(0)

    def issue(block, slot):
        _gather_rows(r0_ref, tok0 + block * tm, ys_hbm, buf.at[slot, 0], sem.at[slot], tm)
        _gather_rows(r1_ref, tok0 + block * tm, ys_hbm, buf.at[slot, 1], sem.at[slot], tm)

    @pl.when(i == 0)
    def _():
        issue(0, 0)

    @pl.when(i + 1 < pl.num_programs(0))
    def _():
        issue(i + 1, (i + 1) % 2)

    slot = i % 2
    _wait_rows(ys_hbm, buf.at[slot, 0], sem.at[slot], tm)
    _wait_rows(ys_hbm, buf.at[slot, 1], sem.at[slot], tm)
    w = w_ref[...]
    o_ref[...] = h_ref[...] + (buf[slot, 0] * w[:, 0:1] + buf[slot, 1] * w[:, 1:2])


def combine(ys, h, route_w, rows0, rows1, *, tok0, tm=256):
    n_tok = h.shape[0]
    grid_spec = pltpu.PrefetchScalarGridSpec(
        num_scalar_prefetch=2,
        grid=(n_tok // tm,),
        in_specs=[pl.BlockSpec(memory_space=pl.ANY),
                  pl.BlockSpec((tm, D_MODEL), lambda i, a, b: (i, 0)),
                  pl.BlockSpec((tm, LANES), lambda i, a, b: (i, 0))],
        out_specs=pl.BlockSpec((tm, D_MODEL), lambda i, a, b: (i, 0)),
        scratch_shapes=[pltpu.VMEM((2, 2, tm, D_MODEL), F32), pltpu.SemaphoreType.DMA((2,))],
    )
    return pl.pallas_call(
        functools.partial(_combine_kernel, tm=tm, tok0=tok0),
        grid_spec=grid_spec,
        out_shape=jax.ShapeDtypeStruct((n_tok, D_MODEL), F32),
        compiler_params=_params(("arbitrary",)),
        name="combine",
    )(rows0, rows1, ys, h, route_w)


def _regroup_w_in(w_in):
    sizes = (SSM_WIDTH, ATTN_WIDTH, KV_WIDTH, KV_WIDTH, IDX_HEADS * IDX_DIM, IDX_DIM, IDX_HEADS, D_MODEL, D_MODEL)
    u, q, k, v, qi, ki, wi, ga, gb = jnp.split(w_in, np.cumsum(sizes)[:-1].tolist(), axis=1)
    pad = jnp.zeros((D_MODEL, PROJ_COLS - COL_KIWI - IDX_DIM - IDX_HEADS), F32)
    return jnp.concatenate([u, q, ga, gb, k, v, qi, ki, wi, pad], axis=1).astype(BF16)


def _layer(x_p, x_s, cache_k, cache_v, cache_ki, h0_re, h0_im, p):
    bp, tp, _ = x_p.shape
    bs, ts, _ = x_s.shape
    past = cache_k.shape[1]
    n_p, n_s = bp * tp, bs * ts
    n_tok = n_p + n_s

    w_in = _regroup_w_in(p['w_in'])
    ssm_w = _ssm_weights(p['ssm_A_re'], p['ssm_A_im'], p['ssm_log_dt'], p['ssm_B_re'], p['ssm_B_im'],
                         p['ssm_C_re'], p['ssm_C_im'])
    glu_w = (p['w_glu_val'].astype(BF16), p['w_glu_gate'].astype(BF16), p['w_attn_branch'].astype(BF16))
    w_out = p['w_out'].astype(BF16)
    router_w = _router_weights(p['w_router_group'], p['b_router_group'], p['w_router_expert'], p['b_router_expert'])
    seq_tiles = tp // QK_TM

    def front(x, table_pos, table_block):
        proj = in_proj(x, p['norm_mix_g'][None, :], w_in)
        return proj, qk_post(proj, table_pos, table_block, p['q_norm_g'], p['k_norm_g'], p['idx_k_norm_g'])

    def seqs(a, b, t):
        return a.reshape(b, t, a.shape[-1])

    xp = x_p.reshape(n_p, D_MODEL)
    proj_p, (q_b, kf_p, k_b, vf_p, v_b, qi_b, kif_p, ki_b, wi) = front(
        xp, jnp.arange(tp, dtype=I32), lambda i: i % seq_tiles)
    g_p, sre_p, sim_p = ssm(proj_p, ssm_w, p['ssm_D'], jnp.zeros((bp, SSM_LB, 2, SSM_SB), F32),
                            n_batch=bp, seq=tp, tc=512, row0=0)
    bq = 128
    n_buckets = min(8, tp // bq)
    per = tp // bq // n_buckets
    qp, qip, wip = seqs(q_b, bp, tp), seqs(qi_b, bp, tp), seqs(wi, bp, tp)
    kp, vp, kip = seqs(k_b, bp, tp), seqs(v_b, bp, tp), seqs(ki_b, bp, tp)
    attn_p = jnp.concatenate(
        [dsa(qp, qip, wip, kp, vp, kip, bq=bq, q_blk0=n * per, n_qblk=per, n_keys=(n + 1) * per * bq,
             n_sel=min(IDX_TOPK, tp // 4)) for n in range(n_buckets)], axis=1).reshape(n_p, ATTN_WIDTH)
    merged_p = merge(g_p, attn_p, proj_p, *glu_w)
    h_p, hn_p, ri_p, rw_p, cnt_p = out_proj(xp, merged_p, w_out, p['norm_ffn_g'], router_w, jnp.zeros((1, LANES), F32))

    xs_ = x_s.reshape(n_s, D_MODEL)
    proj_s, (q_b, kf_s, k_b, vf_s, v_b, qi_b, kif_s, ki_b, wi) = front(
        xs_, jnp.tile(past + jnp.arange(ts, dtype=I32), QK_TM // ts), lambda i: 0)
    h0 = jnp.stack([h0_re.reshape(bs, SSM_LB, SSM_SB), h0_im.reshape(bs, SSM_LB, SSM_SB)]).transpose(2, 0, 1, 3)
    g_s, sre_s, sim_s = ssm_step(proj_s, ssm_w, p['ssm_D'], h0, n_batch=bs, seq=ts, row0=0)
    attn_s = dsa_step(seqs(q_b, bs, ts), seqs(qi_b, bs, ts), seqs(wi, bs, ts),
                      cache_k.reshape(bs, past, KV_WIDTH), cache_v.reshape(bs, past, KV_WIDTH), cache_ki,
                      seqs(k_b, bs, ts), seqs(v_b, bs, ts), seqs(ki_b, bs, ts),
                      n_sel=min(IDX_TOPK, (past + ts) // 4)).reshape(n_s, ATTN_WIDTH)
    merged_s = merge(g_s, attn_s, proj_s, *glu_w)
    h_s, hn_s, ri_s, rw_s, cnt = out_proj(xs_, merged_s, w_out, p['norm_ffn_g'], router_w, cnt_p)

    counts = cnt[0, N_EXPERT_GROUPS:ROUTER_COLS].astype(I32)
    pad_start, pad_end, n_used = _block_layout(counts)
    route_i = jnp.concatenate([ri_p, ri_s], axis=1)
    dest0 = pad_start[route_i[0]] + route_i[2]
    dest1 = pad_start[route_i[1]] + route_i[3]
    n_blocks = _moe_rows(n_tok) // MOE_TM
    blk = jnp.minimum(jnp.arange(n_blocks, dtype=I32), n_used - 1)
    blk_e = jnp.minimum(jnp.sum((pad_end[None, :] <= (blk * MOE_TM)[:, None]).astype(I32), axis=1), N_EXPERTS - 1)
    n_used = n_used.reshape(1)

    xs = dispatch(hn_p, hn_s, dest0, dest1, pad_end, counts, n_used)
    ys = moe(xs, blk_e, n_used, p['w_exp_gate'].astype(BF16), p['w_exp_up'].astype(BF16), p['w_exp_down'].astype(BF16))
    y_p = combine(ys, h_p, rw_p, dest0, dest1, tok0=0).reshape(bp, tp, D_MODEL)
    y_s = combine(ys, h_s, rw_s, dest0, dest1, tok0=n_p).reshape(bs, ts, D_MODEL)

    def heads(a, b, t):
        return a.reshape(b, t, N_KV_HEADS, HEAD_DIM)

    new_p = (heads(kf_p, bp, tp), heads(vf_p, bp, tp), kif_p.reshape(bp, tp, IDX_DIM), sre_p, sim_p)
    new_s = (heads(kf_s, bs, ts), heads(vf_s, bs, ts), kif_s.reshape(bs, ts, IDX_DIM), sre_s, sim_s)
    return y_p, y_s, new_p, new_s


def kernel(x_prompt, x_sample, cache_k, cache_v, cache_idx_k, state_ssm_re, state_ssm_im, norm_mix_g, w_in, q_norm_g, k_norm_g, idx_k_norm_g, ssm_A_re, ssm_A_im, ssm_log_dt, ssm_B_re, ssm_B_im, ssm_C_re, ssm_C_im, ssm_D, w_glu_val, w_glu_gate, w_attn_branch, w_out, norm_ffn_g, w_router_group, b_router_group, w_router_expert, b_router_expert, w_exp_gate, w_exp_up, w_exp_down):
    depth = w_in.shape[0]
    assert depth == 1, "prompt and sample tokens are batched through one layer"
    names = ('norm_mix_g', 'w_in', 'q_norm_g', 'k_norm_g', 'idx_k_norm_g', 'ssm_A_re', 'ssm_A_im', 'ssm_log_dt',
             'ssm_B_re', 'ssm_B_im', 'ssm_C_re', 'ssm_C_im', 'ssm_D', 'w_glu_val', 'w_glu_gate', 'w_attn_branch',
             'w_out', 'norm_ffn_g', 'w_router_group', 'b_router_group', 'w_router_expert', 'b_router_expert',
             'w_exp_gate', 'w_exp_up', 'w_exp_down')
    vals = (norm_mix_g, w_in, q_norm_g, k_norm_g, idx_k_norm_g, ssm_A_re, ssm_A_im, ssm_log_dt, ssm_B_re, ssm_B_im,
            ssm_C_re, ssm_C_im, ssm_D, w_glu_val, w_glu_gate, w_attn_branch, w_out, norm_ffn_g, w_router_group,
            b_router_group, w_router_expert, b_router_expert, w_exp_gate, w_exp_up, w_exp_down)
    p = {n: v[0] for n, v in zip(names, vals)}
    y_p, y_s, new_p, new_s = _layer(x_prompt, x_sample, cache_k[0], cache_v[0], cache_idx_k[0],
                                    state_ssm_re[0], state_ssm_im[0], p)
    st_p = tuple(a[None] for a in new_p)
    st_s = tuple(a[None] for a in new_s)
    return (y_p, y_s) + st_p + st_s
```

```python
import functools

import numpy as np
import jax
import jax.numpy as jnp
from jax import lax
from jax.experimental import pallas as pl
from jax.experimental.pallas import tpu as pltpu

F32 = jnp.float32
BF16 = jnp.bfloat16
I32 = jnp.int32

D_MODEL = 2048
CHUNK = 64
SSM_WIDTH = 1024
SSM_GROUP = 16
SSM_GROUPS = 64
SSM_STATE = 64
ATTN_WIDTH = 1024
HEAD_DIM = 128
N_HEADS = 8
N_KV_HEADS = 2
KV_GROUP = 4
IDX_HEADS = 8
IDX_DIM = 64
IDX_TOPK = 256
ROPE_THETA = 500000.0
N_EXPERT_GROUPS = 4
EXPERTS_PER_GROUP = 8
N_EXPERTS = 32
TOP_K = 2
EXPERT_FF = 1024
EPS = 1e-6

LANES = 128
SUBLANES = 8
VMEM_LIMIT = 56 * 1024 * 1024

COL_U, COL_Q, COL_GA, COL_GB, COL_K, COL_V, COL_QI, COL_KIWI = 0, 1024, 2048, 4096, 6144, 6400, 6656, 7168
PROJ_COLS = 7296
PROJ_TN = 2432
KV_WIDTH = N_KV_HEADS * HEAD_DIM

SSM_LB = SSM_WIDTH // LANES
SSM_SB = 8 * SSM_STATE

INT_MIN = np.int32(-2 ** 31)
KEY_NEG_INF = np.int32(np.array([0xFF800000], np.uint32).view(np.int32)[0] ^ 0x7FFFFFFF)


def _params(sem, vmem=VMEM_LIMIT):
    return pltpu.CompilerParams(dimension_semantics=sem, vmem_limit_bytes=vmem)


def _dot(a, b):
    return jnp.dot(a, b, preferred_element_type=F32)


def _dot_nt(a, b):
    return lax.dot_general(a, b, (((1,), (1,)), ((), ())), preferred_element_type=F32)


def _split_bf16(x):
    hi = x.astype(BF16)
    lo = (x - hi.astype(F32)).astype(BF16)
    return hi, lo


def _in_proj_kernel(x_ref, g_ref, w_ref, o_ref, xn_ref):
    @pl.when(pl.program_id(1) == 0)
    def _():
        x = x_ref[...]
        ms = jnp.mean(x * x, axis=-1, keepdims=True)
        xn_ref[...] = (x * lax.rsqrt(ms + EPS) * g_ref[...]).astype(BF16)

    o_ref[...] = _dot(xn_ref[...], w_ref[...])


def in_proj(x, gain, w_bf16, *, tm=512):
    n_tok = x.shape[0]
    return pl.pallas_call(
        _in_proj_kernel,
        grid=(n_tok // tm, PROJ_COLS // PROJ_TN),
        in_specs=[pl.BlockSpec((tm, D_MODEL), lambda i, j: (i, 0)),
                  pl.BlockSpec((1, D_MODEL), lambda i, j: (0, 0)),
                  pl.BlockSpec((D_MODEL, PROJ_TN), lambda i, j: (0, j))],
        out_specs=pl.BlockSpec((tm, PROJ_TN), lambda i, j: (i, j)),
        out_shape=jax.ShapeDtypeStruct((n_tok, PROJ_COLS), F32),
        scratch_shapes=[pltpu.VMEM((tm, D_MODEL), BF16)],
        compiler_params=_params(("arbitrary", "arbitrary")),
        name="in_proj",
    )(x, gain, w_bf16)


def _rope(x, c, s_lo, s_hi, half):
    n = x.shape[-1]
    return x * c + pltpu.roll(x, n - half, 1) * s_lo + pltpu.roll(x, half, 1) * s_hi


def _head_norm(x, g):
    ms = jnp.mean(x * x, axis=-1, keepdims=True)
    return x * lax.rsqrt(ms + EPS) * g


V_AUG = 2 * HEAD_DIM


def _store_v_aug(dst_ref, row0, v):
    n = v.shape[0]
    one_col = jnp.where(lax.broadcasted_iota(I32, (n, HEAD_DIM), 1) == 0, 1.0, 0.0).astype(BF16)
    for h in range(N_KV_HEADS):
        dst_ref[row0:row0 + n, h * V_AUG:h * V_AUG + HEAD_DIM] = v[:, h * HEAD_DIM:(h + 1) * HEAD_DIM].astype(BF16)
        dst_ref[row0:row0 + n, h * V_AUG + HEAD_DIM:(h + 1) * V_AUG] = one_col


def _qk_post_kernel(q_ref, k_ref, v_ref, qi_ref, kw_ref, c128_ref, sl128_ref, sh128_ref,
                    c64_ref, sl64_ref, sh64_ref, qg_ref, kg_ref, ig_ref,
                    qo_ref, kf_ref, kb_ref, vf_ref, vb_ref, qio_ref, kif_ref, kib_ref, wo_ref):
    c128, sl128, sh128 = c128_ref[...], sl128_ref[...], sh128_ref[...]
    c64, sl64, sh64 = c64_ref[...], sl64_ref[...], sh64_ref[...]
    half128 = HEAD_DIM // 8
    half64 = IDX_DIM // 8
    for h in range(N_HEADS):
        sl = slice(h * LANES, (h + 1) * LANES)
        qo_ref[:, sl] = _rope(_head_norm(q_ref[:, sl], qg_ref[...]), c128, sl128, sh128, half128).astype(BF16)
    for h in range(N_KV_HEADS):
        sl = slice(h * LANES, (h + 1) * LANES)
        kk = _rope(_head_norm(k_ref[:, sl], kg_ref[...]), c128, sl128, sh128, half128)
        kf_ref[:, sl] = kk
        kb_ref[:, sl] = kk.astype(BF16)
    v = v_ref[...]
    vf_ref[...] = v
    _store_v_aug(vb_ref, 0, v)
    lane = lax.broadcasted_iota(I32, c64.shape, 1)
    low = lane < IDX_DIM
    for p in range(IDX_HEADS // 2):
        x = _rope(qi_ref[:, p * LANES:(p + 1) * LANES], c64, sl64, sh64, half64)
        qio_ref[:, (2 * p) * LANES:(2 * p + 1) * LANES] = jnp.where(low, x, 0.0).astype(BF16)
        qio_ref[:, (2 * p + 1) * LANES:(2 * p + 2) * LANES] = jnp.where(low, pltpu.roll(x, IDX_DIM, 1), 0.0).astype(BF16)
    kw = kw_ref[...]
    ms = jnp.sum(jnp.where(low, kw * kw, 0.0), axis=-1, keepdims=True) * (1.0 / IDX_DIM)
    ki = _rope(kw * lax.rsqrt(ms + EPS) * ig_ref[...], c64, sl64, sh64, half64)
    kif_ref[...] = ki[:, :IDX_DIM]
    kib_ref[...] = jnp.where(low, ki, 0.0).astype(BF16)
    wo_ref[...] = (pltpu.roll(kw, IDX_DIM, 1) * IDX_HEADS ** -0.5) * IDX_DIM ** -0.5


def _rope_tables(pos, head_dim):
    r = head_dim // 4
    half = r // 2
    inv = ROPE_THETA ** (-jnp.arange(half, dtype=F32) * 2.0 / r)
    ang = pos.astype(F32)[:, None] * inv[None, :]
    cos, sin = jnp.cos(ang), jnp.sin(ang)
    n = pos.shape[0]
    zh = jnp.zeros((n, half), F32)
    rest = head_dim - r
    c = jnp.concatenate([cos, cos, jnp.ones((n, rest), F32)], axis=-1)
    s_lo = jnp.concatenate([-sin, zh, jnp.zeros((n, rest), F32)], axis=-1)
    s_hi = jnp.concatenate([zh, sin, jnp.zeros((n, rest), F32)], axis=-1)
    rep = LANES // head_dim
    return tuple(jnp.tile(t, (1, rep)) for t in (c, s_lo, s_hi))


QK_TM = 512


def qk_post(proj, table_pos, table_block, q_gain, k_gain, ik_gain):
    tm = QK_TM
    n_tok = proj.shape[0]
    t128 = _rope_tables(table_pos, HEAD_DIM)
    t64 = _rope_tables(table_pos, IDX_DIM)
    ik_gain128 = jnp.concatenate([ik_gain, jnp.zeros((LANES - IDX_DIM,), F32)])[None, :]

    def col(width, start):
        return pl.BlockSpec((tm, width), lambda i: (i, start // width))

    def row(width):
        return pl.BlockSpec((tm, width), lambda i: (i, 0))

    table = pl.BlockSpec((tm, LANES), lambda i: (table_block(i), 0))
    gain = pl.BlockSpec((1, LANES), lambda i: (0, 0))
    return pl.pallas_call(
        _qk_post_kernel,
        grid=(n_tok // tm,),
        in_specs=[col(ATTN_WIDTH, COL_Q), col(KV_WIDTH, COL_K), col(KV_WIDTH, COL_V), col(IDX_HEADS * IDX_DIM, COL_QI),
                  col(LANES, COL_KIWI)] + [table] * 6 + [gain] * 3,
        out_specs=[row(ATTN_WIDTH), row(KV_WIDTH), row(KV_WIDTH), row(KV_WIDTH), row(N_KV_HEADS * V_AUG), row(IDX_HEADS * LANES),
                   row(IDX_DIM), row(LANES), row(LANES)],
        out_shape=[jax.ShapeDtypeStruct((n_tok, ATTN_WIDTH), BF16),
                   jax.ShapeDtypeStruct((n_tok, KV_WIDTH), F32), jax.ShapeDtypeStruct((n_tok, KV_WIDTH), BF16),
                   jax.ShapeDtypeStruct((n_tok, KV_WIDTH), F32), jax.ShapeDtypeStruct((n_tok, N_KV_HEADS * V_AUG), BF16),
                   jax.ShapeDtypeStruct((n_tok, IDX_HEADS * LANES), BF16),
                   jax.ShapeDtypeStruct((n_tok, IDX_DIM), F32), jax.ShapeDtypeStruct((n_tok, LANES), BF16),
                   jax.ShapeDtypeStruct((n_tok, LANES), F32)],
        compiler_params=_params(("arbitrary",)),
        name="qk_post",
    )(proj, proj, proj, proj, proj, *t128, *t64, q_gain[None, :], k_gain[None, :], ik_gain128)


def _gelu_tanh(x):
    return 0.5 * x * (1.0 + jnp.tanh(np.float32(np.sqrt(2.0 / np.pi)) * (x + 0.044715 * (x * x * x))))


def _ssm_kernel(u_ref, wb_ref, wc_ref, cst_ref, d_ref, h0_ref, g_ref, sre_ref, sim_ref,
                er_ref, ei_ref, car_ref, *, tc):
    c = pl.program_id(2)

    @pl.when(c == 0)
    def _():
        car_ref[...] = h0_ref[...]

    u = u_ref[...]
    e = _dot(u.astype(BF16), wb_ref[...])
    er_ref[...] = e[:, :SSM_SB]
    ei_ref[...] = e[:, SSM_SB:]

    def body(r, carry):
        cr, ci = carry
        i0 = pl.multiple_of(r * SUBLANES, SUBLANES)
        xr = er_ref[pl.ds(i0, SUBLANES), :]
        xi = ei_ref[pl.ds(i0, SUBLANES), :]
        for n, k in enumerate((1, 2, 4)):
            ar, ai = cst_ref[2 * n], cst_ref[2 * n + 1]
            sr, si = pltpu.roll(xr, k, 0), pltpu.roll(xi, k, 0)
            xr, xi = xr + ar * sr - ai * si, xi + ar * si + ai * sr
        pr, pi_ = cst_ref[6], cst_ref[7]
        xr, xi = xr + pr * cr - pi_ * ci, xi + pr * ci + pi_ * cr
        er_ref[pl.ds(i0, SUBLANES), :] = xr
        ei_ref[pl.ds(i0, SUBLANES), :] = xi
        return xr[SUBLANES - 1:SUBLANES, :], xi[SUBLANES - 1:SUBLANES, :]

    cr, ci = lax.fori_loop(0, tc // SUBLANES, body, (car_ref[0:1, :], car_ref[1:2, :]))
    car_ref[0:1, :] = cr
    car_ref[1:2, :] = ci

    y = _dot(er_ref[...].astype(BF16), wc_ref[0]) - _dot(ei_ref[...].astype(BF16), wc_ref[1])
    y = y + d_ref[...] * u
    g_ref[...] = _gelu_tanh(y).astype(BF16)

    @pl.when(c == pl.num_programs(2) - 1)
    def _():
        sre_ref[...] = cr
        sim_ref[...] = ci


def _ssm_weights(a_re, a_im, log_dt, b_re, b_im, c_re, c_im):
    lam_re, lam_im = a_re, a_im
    dt = jnp.exp(log_dt)[:, None]
    mag = jnp.exp(lam_re * dt)
    lb_re, lb_im = mag * jnp.cos(lam_im * dt), mag * jnp.sin(lam_im * dt)
    den = lam_re * lam_re + lam_im * lam_im
    num_re = lb_re - 1.0
    z_re = (num_re * lam_re + lb_im * lam_im) / den
    z_im = (lb_im * lam_re - num_re * lam_im) / den
    zb_re = z_re[:, :, None] * b_re - z_im[:, :, None] * b_im
    zb_im = z_re[:, :, None] * b_im + z_im[:, :, None] * b_re
    eye = jnp.eye(8, dtype=F32)

    def blockdiag_in(w):
        return jnp.einsum('jgph,gk->jghkp', w.reshape(SSM_LB, 8, SSM_STATE, SSM_GROUP), eye).reshape(SSM_LB, LANES, SSM_SB)

    def blockdiag_out(w):
        return jnp.einsum('jghp,gk->jkpgh', w.reshape(SSM_LB, 8, SSM_GROUP, SSM_STATE), eye).reshape(SSM_LB, SSM_SB, LANES)

    wb = jnp.concatenate([blockdiag_in(zb_re), blockdiag_in(zb_im)], axis=-1).astype(BF16)
    wc = jnp.stack([blockdiag_out(c_re), blockdiag_out(c_im)], axis=1).astype(BF16)

    pw = [(lb_re, lb_im)]
    for _ in range(7):
        pr, pi_ = pw[-1]
        pw.append((pr * lb_re - pi_ * lb_im, pr * lb_im + pi_ * lb_re))
    rows = jnp.arange(SUBLANES)[:, None]

    def lane(x):
        return x.reshape(SSM_LB, 1, SSM_SB)

    cst = []
    for k in (1, 2, 4):
        for part in pw[k - 1]:
            cst.append(jnp.where(rows >= k, lane(part), 0.0))
    cst.append(jnp.concatenate([lane(pw[r][0]) for r in range(SUBLANES)], axis=1))
    cst.append(jnp.concatenate([lane(pw[r][1]) for r in range(SUBLANES)], axis=1))
    cst = jnp.stack(cst, axis=1)
    return wb, wc, cst


def ssm(proj, ssm_w, d_skip, h0, *, n_batch, seq, tc, row0):
    wb, wc, cst = ssm_w
    n_chunks = seq // tc
    blk0 = row0 // tc
    n_tok = n_batch * seq
    state_shape = jax.ShapeDtypeStruct((n_batch, SSM_LB, 1, SSM_SB), F32)
    state_spec = pl.BlockSpec((None, None, 1, SSM_SB), lambda b, j, c: (b, j, 0, 0))
    g, s_re, s_im = pl.pallas_call(
        functools.partial(_ssm_kernel, tc=tc),
        grid=(n_batch, SSM_LB, n_chunks),
        in_specs=[pl.BlockSpec((tc, LANES), lambda b, j, c: (blk0 + b * n_chunks + c, j)),
                  pl.BlockSpec((None, LANES, 2 * SSM_SB), lambda b, j, c: (j, 0, 0)),
                  pl.BlockSpec((None, 2, SSM_SB, LANES), lambda b, j, c: (j, 0, 0, 0)),
                  pl.BlockSpec((None, 8, SUBLANES, SSM_SB), lambda b, j, c: (j, 0, 0, 0)),
                  pl.BlockSpec((1, LANES), lambda b, j, c: (0, j)),
                  pl.BlockSpec((None, None, 2, SSM_SB), lambda b, j, c: (b, j, 0, 0))],
        out_specs=[pl.BlockSpec((tc, LANES), lambda b, j, c: (b * n_chunks + c, j)), state_spec, state_spec],
        out_shape=[jax.ShapeDtypeStruct((n_tok, SSM_WIDTH), BF16), state_shape, state_shape],
        scratch_shapes=[pltpu.VMEM((tc, SSM_SB), F32), pltpu.VMEM((tc, SSM_SB), F32), pltpu.VMEM((2, SSM_SB), F32)],
        compiler_params=_params(("arbitrary", "arbitrary", "arbitrary")),
        name="ssm",
    )(proj, wb, wc, cst, d_skip[None, :], h0)
    return g, s_re.reshape(n_batch, SSM_GROUPS, SSM_STATE), s_im.reshape(n_batch, SSM_GROUPS, SSM_STATE)


def _ssm_step_kernel(u_ref, wb_ref, wc_ref, cst_ref, d_ref, h0_ref, g_ref, sre_ref, sim_ref, er_ref, ei_ref, *, seq):
    n_seq = h0_ref.shape[1]
    u = u_ref[...]
    e = _dot(u.astype(BF16), wb_ref[...])
    n_lt = SSM_SB // LANES
    y = d_ref[...] * u
    for lt in range(n_lt):
        sl = slice(lt * LANES, (lt + 1) * LANES)
        er_ref[...] = e[:, lt * LANES:(lt + 1) * LANES]
        ei_ref[...] = e[:, SSM_SB + lt * LANES:SSM_SB + (lt + 1) * LANES]
        lr, li = cst_ref[6, 0:1, sl], cst_ref[7, 0:1, sl]
        sr, si = h0_ref[0, :, sl], h0_ref[1, :, sl]
        for t in range(seq):
            rows = pl.ds(t, n_seq, stride=seq)
            sr, si = lr * sr - li * si + er_ref[rows, :], lr * si + li * sr + ei_ref[rows, :]
            er_ref[rows, :] = sr
            ei_ref[rows, :] = si
        y = y + (_dot(er_ref[...].astype(BF16), wc_ref[0, sl, :]) - _dot(ei_ref[...].astype(BF16), wc_ref[1, sl, :]))
        sre_ref[:, sl] = sr
        sim_ref[:, sl] = si
    g_ref[...] = _gelu_tanh(y).astype(BF16)


def ssm_step(proj, ssm_w, d_skip, h0, *, n_batch, seq, row0):
    wb, wc, cst = ssm_w
    n_tok = n_batch * seq
    assert row0 % n_tok == 0
    state_shape = jax.ShapeDtypeStruct((SSM_LB, n_batch, SSM_SB), F32)
    state_spec = pl.BlockSpec((None, n_batch, SSM_SB), lambda j: (j, 0, 0))
    g, s_re, s_im = pl.pallas_call(
        functools.partial(_ssm_step_kernel, seq=seq),
        grid=(SSM_LB,),
        in_specs=[pl.BlockSpec((n_tok, LANES), lambda j: (row0 // n_tok, j)),
                  pl.BlockSpec((None, LANES, 2 * SSM_SB), lambda j: (j, 0, 0)),
                  pl.BlockSpec((None, 2, SSM_SB, LANES), lambda j: (j, 0, 0, 0)),
                  pl.BlockSpec((None, 8, SUBLANES, SSM_SB), lambda j: (j, 0, 0, 0)),
                  pl.BlockSpec((1, LANES), lambda j: (0, j)),
                  pl.BlockSpec((None, 2, n_batch, SSM_SB), lambda j: (j, 0, 0, 0))],
        out_specs=[pl.BlockSpec((n_tok, LANES), lambda j: (0, j)), state_spec, state_spec],
        out_shape=[jax.ShapeDtypeStruct((n_tok, SSM_WIDTH), BF16), state_shape, state_shape],
        scratch_shapes=[pltpu.VMEM((n_tok, LANES), F32), pltpu.VMEM((n_tok, LANES), F32)],
        compiler_params=_params(("arbitrary",)),
        name="ssm_step",
    )(proj, wb, wc, cst, d_skip[None, :], h0)

    def per_seq(s):
        return s.transpose(1, 0, 2).reshape(n_batch, SSM_GROUPS, SSM_STATE)

    return g, per_seq(s_re), per_seq(s_im)


def _row_sum(x):
    return jnp.sum(x, axis=1, keepdims=True)


def _row_count(mask):
    return _row_sum(jnp.where(mask, 1, 0))


I16 = jnp.int16
I16_MIN = -2 ** 15


def _count16(ref, cand, compare):
    accs = [None] * 4
    for t in range(ref.shape[1] // LANES):
        x = jnp.where(compare(ref[:, t * LANES:(t + 1) * LANES], cand), I16(1), I16(0))
        accs[t % 4] = x if accs[t % 4] is None else accs[t % 4] + x
    accs = [a for a in accs if a is not None]
    total = accs[0]
    for a in accs[1:]:
        total = total + a
    return _row_sum(total.astype(I32))


def _bisect16(ref, target):
    def step(i, base):
        cand = base + lax.shift_left(np.int32(1), np.int32(15) - i)
        cnt = _count16(ref, cand.astype(I16), lambda a, b: a >= b)
        return jnp.where(cnt >= target, cand, base)
    return lax.fori_loop(0, 16, step, jnp.full((ref.shape[0], 1), I16_MIN, I32))


def _stack_heads(ref, heads):
    return jnp.concatenate([ref[:, h * LANES:(h + 1) * LANES] for h in heads], axis=0)


def _dsa_body(q_ref, qi_ref, wi_ref, k_ref, v_ref, ki_ref, o_ref, key_ref, bias_ref, hi_ref, lo_ref,
              *, q_pos_first, s_valid, n_sel, packed_bisect, stack):
    bq, n_keys = key_ref.shape
    col = lax.broadcasted_iota(I32, (bq, n_keys), 1)
    qpos = q_pos_first + lax.broadcasted_iota(I32, (bq, 1), 0)
    allowed = col < jnp.minimum((qpos // CHUNK + 1) * CHUNK, s_valid)

    ki = ki_ref[...]
    score = None
    for h0 in range(0, IDX_HEADS, stack):
        d = _dot_nt(_stack_heads(qi_ref, range(h0, h0 + stack)), ki)
        for j in range(stack):
            t = jnp.maximum(d[j * bq:(j + 1) * bq], 0.0) * wi_ref[:, h0 + j:h0 + j + 1]
            score = t if score is None else score + t
    score = jnp.where(score == 0.0, 0.0, score)
    bits = pltpu.bitcast(score, I32)
    key = jnp.where(bits < 0, bits ^ np.int32(0x7FFFFFFF), bits)
    key = jnp.where(allowed, key, KEY_NEG_INF)
    key_ref[...] = key

    if packed_bisect:
        hi_ref[...] = (key >> 16).astype(I16)
        lo_ref[...] = ((key & 0xFFFF) + I16_MIN).astype(I16)
        thr_hi = _bisect16(hi_ref, n_sel)
        thr_hi16 = thr_hi.astype(I16)
        need_lo = n_sel - _count16(hi_ref, thr_hi16, lambda a, b: a > b)
        lo_ref[...] = jnp.where(hi_ref[...] == thr_hi16, lo_ref[...], I16(I16_MIN))
        thr_lo = _bisect16(lo_ref, need_lo)
        thr = lax.shift_left(thr_hi, np.int32(16)) + (thr_lo - I16_MIN)
    else:
        def bisect(i, base):
            cand = base + lax.shift_left(np.int32(1), np.int32(31) - i)
            cnt = _row_count(key_ref[...] >= cand)
            return jnp.where(cnt >= n_sel, cand, base)
        thr = lax.fori_loop(0, 32, bisect, jnp.full((bq, 1), INT_MIN, I32))
    thr = jnp.maximum(thr, KEY_NEG_INF)

    key = key_ref[...]
    need = n_sel - _row_count(key > thr)
    n_eq = _row_count(key == thr)
    n_bits = int(n_keys - 1).bit_length()

    def tie_cut():
        def step(i, j0):
            cand = j0 + lax.shift_left(np.int32(1), np.int32(n_bits - 1) - i)
            cnt = _row_sum(jnp.where(key_ref[...] == thr, jnp.where(col < cand, 1, 0), 0))
            return jnp.where(cnt < need, cand, j0)
        return lax.fori_loop(0, n_bits, step, jnp.zeros((bq, 1), I32))

    split = jnp.max(jnp.where(n_eq > need, 1, 0)) > 0
    j_last = lax.cond(split, tie_cut, lambda: jnp.full((bq, 1), n_keys, I32))
    tie_bias = jnp.where(thr == KEY_NEG_INF, -jnp.inf, 0.0)
    bias_ref[...] = jnp.where(key > thr, 0.0,
                              jnp.where(key == thr, jnp.where(col <= j_last, tie_bias, -jnp.inf), -jnp.inf))

    c = np.float32(HEAD_DIM ** -0.5 * np.log2(np.e))
    for h0 in range(0, N_HEADS, stack):
        kv = h0 // KV_GROUP
        heads = range(h0, h0 + stack)
        s_all = _dot_nt(_stack_heads(q_ref, heads), k_ref[:, kv * HEAD_DIM:(kv + 1) * HEAD_DIM])
        ps = []
        for g in range(stack):
            s = s_all[g * bq:(g + 1) * bq] + bias_ref[...]
            m = jnp.max(s, axis=1, keepdims=True)
            ps.append(jnp.exp2((s - m) * c).astype(BF16))
        pv = _dot(jnp.concatenate(ps, axis=0), v_ref[:, kv * V_AUG:(kv + 1) * V_AUG])
        for g, h in enumerate(heads):
            o = pv[g * bq:(g + 1) * bq]
            o_ref[:, h * HEAD_DIM:(h + 1) * HEAD_DIM] = (o[:, :HEAD_DIM] / o[:, HEAD_DIM:HEAD_DIM + 1]).astype(BF16)


def _dsa_scratch(bq, n_keys):
    return [pltpu.VMEM((bq, n_keys), I32), pltpu.VMEM((bq, n_keys), F32),
            pltpu.VMEM((bq, n_keys), I16), pltpu.VMEM((bq, n_keys), I16)]


def _dsa_kernel(q_ref, qi_ref, wi_ref, k_ref, v_ref, ki_ref, o_ref, *scratch, q_pos0, **static):
    bq = scratch[0].shape[0]
    _dsa_body(q_ref, qi_ref, wi_ref, k_ref, v_ref, ki_ref, o_ref, *scratch,
              q_pos_first=q_pos0 + pl.program_id(1) * bq, **static)


def dsa(q, qi, wi, k, v, ki, *, bq, q_blk0, n_qblk, n_keys, n_sel, packed_bisect, stack):
    n_batch, seq = q.shape[:2]

    def qspec(width):
        return pl.BlockSpec((None, bq, width), lambda b, i: (b, q_blk0 + i, 0))

    def kspec(width):
        return pl.BlockSpec((None, n_keys, width), lambda b, i: (b, 0, 0))

    return pl.pallas_call(
        functools.partial(_dsa_kernel, q_pos0=q_blk0 * bq, s_valid=seq, n_sel=n_sel, packed_bisect=packed_bisect,
                          stack=stack),
        grid=(n_batch, n_qblk),
        in_specs=[qspec(ATTN_WIDTH), qspec(IDX_HEADS * LANES), qspec(LANES), kspec(KV_WIDTH), kspec(N_KV_HEADS * V_AUG),
                  kspec(LANES)],
        out_specs=pl.BlockSpec((None, bq, ATTN_WIDTH), lambda b, i: (b, i, 0)),
        out_shape=jax.ShapeDtypeStruct((n_batch, n_qblk * bq, ATTN_WIDTH), BF16),
        scratch_shapes=_dsa_scratch(bq, n_keys),
        compiler_params=_params(("arbitrary", "arbitrary")),
        name="dsa",
    )(q, qi, wi, k, v, ki)


def _dsa_step_kernel(q_ref, qi_ref, wi_ref, ck_ref, cv_ref, cki_ref, nk_ref, nv_ref, nki_ref, o_ref,
                     k_buf, v_buf, ki_buf, *scratch, past, n_sel):
    ts = nk_ref.shape[0]
    n_keys = k_buf.shape[0]
    k_buf[0:past, :] = ck_ref[...].astype(BF16)
    _store_v_aug(v_buf, 0, cv_ref[...])
    for buf, new in ((k_buf, nk_ref), (v_buf, nv_ref)):
        buf[past:past + ts, :] = new[...]
        buf[past + ts:n_keys, :] = jnp.zeros((n_keys - past - ts, buf.shape[1]), BF16)
    ki_buf[0:past, 0:IDX_DIM] = cki_ref[...].astype(BF16)
    ki_buf[0:past, IDX_DIM:LANES] = jnp.zeros((past, LANES - IDX_DIM), BF16)
    ki_buf[past:past + ts, :] = nki_ref[...]
    ki_buf[past + ts:n_keys, :] = jnp.zeros((n_keys - past - ts, LANES), BF16)
    _dsa_body(q_ref, qi_ref, wi_ref, k_buf, v_buf, ki_buf, o_ref, *scratch,
              q_pos_first=past, s_valid=past + ts, n_sel=n_sel, packed_bisect=True, stack=KV_GROUP)


def dsa_step(q, qi, wi, cache_k, cache_v, cache_ki, k_new, v_new, ki_new, *, n_sel):
    n_batch, ts = q.shape[:2]
    past = cache_k.shape[1]
    n_keys = -(-(past + ts) // LANES) * LANES

    def spec(rows, width):
        return pl.BlockSpec((None, rows, width), lambda b: (b, 0, 0))

    return pl.pallas_call(
        functools.partial(_dsa_step_kernel, past=past, n_sel=n_sel),
        grid=(n_batch,),
        in_specs=[spec(ts, ATTN_WIDTH), spec(ts, IDX_HEADS * LANES), spec(ts, LANES),
                  spec(past, KV_WIDTH), spec(past, KV_WIDTH), spec(past, IDX_DIM),
                  spec(ts, KV_WIDTH), spec(ts, N_KV_HEADS * V_AUG), spec(ts, LANES)],
        out_specs=spec(ts, ATTN_WIDTH),
        out_shape=jax.ShapeDtypeStruct((n_batch, ts, ATTN_WIDTH), BF16),
        scratch_shapes=[pltpu.VMEM((n_keys, KV_WIDTH), BF16), pltpu.VMEM((n_keys, N_KV_HEADS * V_AUG), BF16),
                        pltpu.VMEM((n_keys, LANES), BF16),
                        *_dsa_scratch(ts, n_keys)],
        compiler_params=_params(("arbitrary",)),
        name="dsa_step",
    )(q, qi, wi, cache_k, cache_v, cache_ki, k_new, v_new, ki_new)


def _merge_kernel(g_ref, a_ref, ga_ref, gb_ref, wv_ref, wg_ref, wb_ref, o_ref):
    g = g_ref[...]
    branch_a = _dot(g, wv_ref[...]) * jax.nn.sigmoid(_dot(g, wg_ref[...]))
    branch_b = _dot(a_ref[...], wb_ref[...])
    merged = jax.nn.sigmoid(ga_ref[...]) * branch_a + jax.nn.sigmoid(gb_ref[...]) * branch_b
    o_ref[...] = merged.astype(BF16)


def merge(g, attn, proj, w_val, w_gate, w_branch, *, tm=1024, tn=512):
    n_tok = g.shape[0]
    nj = D_MODEL // tn

    def wspec():
        return pl.BlockSpec((SSM_WIDTH, tn), lambda i, j: (0, j))

    return pl.pallas_call(
        _merge_kernel,
        grid=(n_tok // tm, nj),
        in_specs=[pl.BlockSpec((tm, SSM_WIDTH), lambda i, j: (i, 0)),
                  pl.BlockSpec((tm, ATTN_WIDTH), lambda i, j: (i, 0)),
                  pl.BlockSpec((tm, tn), lambda i, j: (i, COL_GA // tn + j)),
                  pl.BlockSpec((tm, tn), lambda i, j: (i, COL_GB // tn + j)),
                  wspec(), wspec(), wspec()],
        out_specs=pl.BlockSpec((tm, tn), lambda i, j: (i, j)),
        out_shape=jax.ShapeDtypeStruct((n_tok, D_MODEL), BF16),
        compiler_params=_params(("arbitrary", "arbitrary")),
        name="merge",
    )(g, attn, proj, proj, w_val, w_gate, w_branch)


ROUTER_COLS = N_EXPERT_GROUPS + N_EXPERTS
MOE_TM = 256


def _first_lane_of_max(x, lane_f):
    m = jnp.max(x, axis=1, keepdims=True)
    return m, jnp.min(jnp.where(x == m, lane_f, float(LANES)), axis=1, keepdims=True)


def _out_proj_kernel(x_ref, m_ref, wo_ref, gn_ref, wrh_ref, wrl_ref, br_ref, cin_ref,
                     h_ref, hn_ref, ri_ref, rw_ref, cnt_ref, carry_ref):
    @pl.when(pl.program_id(0) == 0)
    def _():
        carry_ref[...] = cin_ref[...]

    h = x_ref[...] + _dot(m_ref[...], wo_ref[...])
    h_ref[...] = h
    ms = jnp.mean(h * h, axis=-1, keepdims=True)
    hn = h * lax.rsqrt(ms + EPS) * gn_ref[...]
    hn_ref[...] = hn
    hh, hl = _split_bf16(hn)
    wrh = wrh_ref[...]
    lg = _dot(hh, wrh) + _dot(hl, wrh) + _dot(hh, wrl_ref[...]) + br_ref[...]

    tm = lg.shape[0]
    lane = lax.broadcasted_iota(I32, lg.shape, 1)
    lane_f = lane.astype(F32)
    ninf = -jnp.inf
    gl = jnp.where(lane < N_EXPERT_GROUPS, lg, ninf)
    gmax, gsel = _first_lane_of_max(gl, lane_f)
    g_w = 1.0 / jnp.sum(jnp.exp(gl - gmax), axis=1, keepdims=True)
    lo = N_EXPERT_GROUPS + EXPERTS_PER_GROUP * gsel
    el = jnp.where(lane_f >= lo, jnp.where(lane_f < lo + EXPERTS_PER_GROUP, lg, ninf), ninf)
    v1, i1 = _first_lane_of_max(el, lane_f)
    el2 = jnp.where(lane_f == i1, ninf, el)
    v2, i2 = _first_lane_of_max(el2, lane_f)
    t = jnp.exp(v2 - v1)
    s1 = 1.0 / (1.0 + t)
    w1 = s1 * g_w
    w2 = (t * s1) * g_w

    m1 = jnp.where(lane_f == i1, 1.0, 0.0)
    m2 = jnp.where(lane_f == i2, 1.0, 0.0)
    both = m1 + m2
    tri = jnp.where(lax.broadcasted_iota(I32, (tm, tm), 0) > lax.broadcasted_iota(I32, (tm, tm), 1), 1.0, 0.0)
    before = _dot(tri.astype(BF16), both.astype(BF16)) + carry_ref[...]
    r1 = jnp.sum(before * m1, axis=1, keepdims=True)
    r2 = jnp.sum(before * m2, axis=1, keepdims=True)
    carry_ref[...] = carry_ref[...] + jnp.sum(both, axis=0, keepdims=True)
    cnt_ref[...] = carry_ref[...]
    e1 = i1 - float(N_EXPERT_GROUPS)
    e2 = i2 - float(N_EXPERT_GROUPS)
    fields = jnp.where(lane == 0, e1, jnp.where(lane == 1, e2, jnp.where(lane == 2, r1, jnp.where(lane == 3, r2, 0.0))))
    ri_ref[...] = fields.T[0:SUBLANES, :].astype(I32)
    rw_ref[...] = jnp.where(lane == 0, w1, jnp.where(lane == 1, w2, 0.0))


def _router_weights(w_router_group, b_router_group, w_router_expert, b_router_expert):
    wr = jnp.concatenate([w_router_group, w_router_expert, jnp.zeros((D_MODEL, LANES - ROUTER_COLS), F32)], axis=1)
    wr_hi = wr.astype(BF16)
    wr_lo = (wr - wr_hi.astype(F32)).astype(BF16)
    br = jnp.concatenate([b_router_group, b_router_expert, jnp.zeros((LANES - ROUTER_COLS,), F32)])[None, :]
    return wr_hi, wr_lo, br


def out_proj(x, merged, w_out, ffn_gain, router_w, counts_in, *, tm=256):
    n_tok = x.shape[0]
    wr_hi, wr_lo, br = router_w

    def row(width):
        return pl.BlockSpec((tm, width), lambda i: (i, 0))

    def const(shape):
        return pl.BlockSpec(shape, lambda i: (0, 0), pipeline_mode=pl.Buffered(1))

    return pl.pallas_call(
        _out_proj_kernel,
        grid=(n_tok // tm,),
        in_specs=[row(D_MODEL), row(D_MODEL), const((D_MODEL, D_MODEL)), const((1, D_MODEL)),
                  const((D_MODEL, LANES)), const((D_MODEL, LANES)), const((1, LANES)), const((1, LANES))],
        out_specs=[row(D_MODEL), row(D_MODEL), pl.BlockSpec((SUBLANES, tm), lambda i: (0, i)), row(LANES),
                   pl.BlockSpec((1, LANES), lambda i: (0, 0))],
        out_shape=[jax.ShapeDtypeStruct((n_tok, D_MODEL), F32), jax.ShapeDtypeStruct((n_tok, D_MODEL), F32),
                   jax.ShapeDtypeStruct((SUBLANES, n_tok), I32), jax.ShapeDtypeStruct((n_tok, LANES), F32),
                   jax.ShapeDtypeStruct((1, LANES), F32)],
        scratch_shapes=[pltpu.VMEM((1, LANES), F32)],
        compiler_params=_params(("arbitrary",)),
        name="out_proj",
    )(x, merged, w_out, ffn_gain[None, :], wr_hi, wr_lo, br, counts_in)


def _block_layout(counts):
    padded = (counts + MOE_TM - 1) // MOE_TM * MOE_TM
    pad_end = jnp.cumsum(padded).astype(I32)
    pad_start = pad_end - padded
    n_used = pad_end[-1] // MOE_TM
    return pad_start, pad_end, n_used


def _moe_rows(n_tok):
    return -(-(n_tok * TOP_K + N_EXPERTS * (MOE_TM - 1)) // MOE_TM) * MOE_TM


DISPATCH_TM = 512


def _wait_rows(src_hbm, dst, sem, n_rows):
    pltpu.make_async_copy(src_hbm.at[pl.ds(0, n_rows)], dst, sem).wait()


def _dispatch_kernel(d0_ref, d1_ref, pe_ref, cnt_ref, nu_ref, hna_ref, hnb_ref, xs_hbm, zbuf, sem, semz,
                     *, n_blocks, a_tiles):
    i = pl.program_id(0)

    def zero_block(row0):
        return pltpu.make_async_copy(zbuf, xs_hbm.at[pl.ds(pl.multiple_of(row0, MOE_TM), MOE_TM)], semz)

    @pl.when(i == 0)
    def _():
        zbuf[...] = jnp.zeros_like(zbuf)
        for start in (True, False):
            for e in range(N_EXPERTS):
                @pl.when(cnt_ref[e] > 0)
                def _():
                    cp = zero_block(pe_ref[e] - MOE_TM)
                    cp.start() if start else cp.wait()

            def tail(b, c):
                cp = zero_block(b * MOE_TM)
                cp.start() if start else cp.wait()
                return c
            lax.fori_loop(nu_ref[0], n_blocks, tail, 0)

    base = i * DISPATCH_TM

    def scatter(hn_ref):
        def body(r, c):
            src = hn_ref.at[pl.ds(r, 1)]
            pltpu.make_async_copy(src, xs_hbm.at[pl.ds(d0_ref[base + r], 1)], sem).start()
            pltpu.make_async_copy(src, xs_hbm.at[pl.ds(d1_ref[base + r], 1)], sem).start()
            return c
        lax.fori_loop(0, DISPATCH_TM, body, 0, unroll=8)
        for _ in range(TOP_K):
            pltpu.make_async_copy(hn_ref, xs_hbm.at[pl.ds(0, DISPATCH_TM)], sem).wait()

    @pl.when(i < a_tiles)
    def _():
        scatter(hna_ref)

    @pl.when(i >= a_tiles)
    def _():
        scatter(hnb_ref)


def dispatch(hn_a, hn_b, dest0, dest1, pad_end, counts, n_used):
    a_tiles, b_tiles = hn_a.shape[0] // DISPATCH_TM, hn_b.shape[0] // DISPATCH_TM
    rows = _moe_rows(hn_a.shape[0] + hn_b.shape[0])
    grid_spec = pltpu.PrefetchScalarGridSpec(
        num_scalar_prefetch=5,
        grid=(a_tiles + b_tiles,),
        in_specs=[pl.BlockSpec((DISPATCH_TM, D_MODEL), lambda i, *_: (jnp.minimum(i, a_tiles - 1), 0)),
                  pl.BlockSpec((DISPATCH_TM, D_MODEL), lambda i, *_: (jnp.maximum(i - a_tiles, 0), 0))],
        out_specs=pl.BlockSpec(memory_space=pl.ANY),
        scratch_shapes=[pltpu.VMEM((MOE_TM, D_MODEL), F32), pltpu.SemaphoreType.DMA(()), pltpu.SemaphoreType.DMA(())],
    )
    return pl.pallas_call(
        functools.partial(_dispatch_kernel, n_blocks=rows // MOE_TM, a_tiles=a_tiles),
        grid_spec=grid_spec,
        out_shape=jax.ShapeDtypeStruct((rows, D_MODEL), F32),
        compiler_params=_params(("arbitrary",)),
        name="dispatch",
    )(dest0, dest1, pad_end, counts, n_used, hn_a, hn_b)


def _moe_kernel(blk_e_ref, nu_ref, xs_ref, wg_ref, wu_ref, wd_ref, ys_ref):
    i = pl.program_id(0)

    @pl.when(i < nu_ref[0])
    def _():
        x = xs_ref[...].astype(BF16)
        hg = _dot(x, wg_ref[...])
        hu = _dot(x, wu_ref[...])
        hmid = (jax.nn.silu(hg) * hu).astype(BF16)
        ys_ref[...] = _dot(hmid, wd_ref[...])

    @pl.when(i >= nu_ref[0])
    def _():
        ys_ref[...] = jnp.zeros_like(ys_ref)


def moe(xs, blk_e, n_used, w_gate, w_up, w_down):
    rows = xs.shape[0]
    grid_spec = pltpu.PrefetchScalarGridSpec(
        num_scalar_prefetch=2,
        grid=(rows // MOE_TM,),
        in_specs=[pl.BlockSpec((MOE_TM, D_MODEL), lambda i, be, nu: (jnp.minimum(i, nu[0] - 1), 0)),
                  pl.BlockSpec((None, D_MODEL, EXPERT_FF), lambda i, be, nu: (be[i], 0, 0)),
                  pl.BlockSpec((None, D_MODEL, EXPERT_FF), lambda i, be, nu: (be[i], 0, 0)),
                  pl.BlockSpec((None, EXPERT_FF, D_MODEL), lambda i, be, nu: (be[i], 0, 0))],
        out_specs=pl.BlockSpec((MOE_TM, D_MODEL), lambda i, be, nu: (i, 0)),
    )
    return pl.pallas_call(
        _moe_kernel,
        grid_spec=grid_spec,
        out_shape=jax.ShapeDtypeStruct((rows, D_MODEL), F32),
        compiler_params=_params(("arbitrary",)),
        name="moe",
    )(blk_e, n_used, xs, w_gate, w_up, w_down)


def _gather_rows(idx_ref, idx0, src_hbm, dst, sem, n_rows):
    def body(r, carry):
        t = idx_ref[idx0 + r]
        pltpu.make_async_copy(src_hbm.at[pl.ds(t, 1)], dst.at[pl.ds(r, 1)], sem).start()
        return carry
    lax.fori_loop(0, n_rows, body, 0, unroll=8)


def _combine_kernel(r0_ref, r1_ref, ys_hbm, h_ref, w_ref, o_ref, buf, sem, *, tm, tok0):
    i = pl.program_id(0)

    def issue(block, slot):
        _gather_rows(r0_ref, tok0 + block * tm, ys_hbm, buf.at[slot, 0], sem.at[slot], tm)
        _gather_rows(r1_ref, tok0 + block * tm, ys_hbm, buf.at[slot, 1], sem.at[slot], tm)

    @pl.when(i == 0)
    def _():
        issue(0, 0)

    @pl.when(i + 1 < pl.num_programs(0))
    def _():
        issue(i + 1, (i + 1) % 2)

    slot = i % 2
    _wait_rows(ys_hbm, buf.at[slot, 0], sem.at[slot], tm)
    _wait_rows(ys_hbm, buf.at[slot, 1], sem.at[slot], tm)
    w = w_ref[...]
    o_ref[...] = h_ref[...] + (buf[slot, 0] * w[:, 0:1] + buf[slot, 1] * w[:, 1:2])


def combine(ys, h, route_w, rows0, rows1, *, tok0, tm=256):
    n_tok = h.shape[0]
    grid_spec = pltpu.PrefetchScalarGridSpec(
        num_scalar_prefetch=2,
        grid=(n_tok // tm,),
        in_specs=[pl.BlockSpec(memory_space=pl.ANY),
                  pl.BlockSpec((tm, D_MODEL), lambda i, a, b: (i, 0)),
                  pl.BlockSpec((tm, LANES), lambda i, a, b: (i, 0))],
        out_specs=pl.BlockSpec((tm, D_MODEL), lambda i, a, b: (i, 0)),
        scratch_shapes=[pltpu.VMEM((2, 2, tm, D_MODEL), F32), pltpu.SemaphoreType.DMA((2,))],
    )
    return pl.pallas_call(
        functools.partial(_combine_kernel, tm=tm, tok0=tok0),
        grid_spec=grid_spec,
        out_shape=jax.ShapeDtypeStruct((n_tok, D_MODEL), F32),
        compiler_params=_params(("arbitrary",)),
        name="combine",
    )(rows0, rows1, ys, h, route_w)


def _regroup_w_in(w_in):
    sizes = (SSM_WIDTH, ATTN_WIDTH, KV_WIDTH, KV_WIDTH, IDX_HEADS * IDX_DIM, IDX_DIM, IDX_HEADS, D_MODEL, D_MODEL)
    u, q, k, v, qi, ki, wi, ga, gb = jnp.split(w_in, np.cumsum(sizes)[:-1].tolist(), axis=1)
    pad = jnp.zeros((D_MODEL, PROJ_COLS - COL_KIWI - IDX_DIM - IDX_HEADS), F32)
    return jnp.concatenate([u, q, ga, gb, k, v, qi, ki, wi, pad], axis=1).astype(BF16)


def _layer(x_p, x_s, cache_k, cache_v, cache_ki, h0_re, h0_im, p):
    bp, tp, _ = x_p.shape
    bs, ts, _ = x_s.shape
    past = cache_k.shape[1]
    n_p, n_s = bp * tp, bs * ts
    n_tok = n_p + n_s

    w_in = _regroup_w_in(p['w_in'])
    ssm_w = _ssm_weights(p['ssm_A_re'], p['ssm_A_im'], p['ssm_log_dt'], p['ssm_B_re'], p['ssm_B_im'],
                         p['ssm_C_re'], p['ssm_C_im'])
    glu_w = (p['w_glu_val'].astype(BF16), p['w_glu_gate'].astype(BF16), p['w_attn_branch'].astype(BF16))
    w_out = p['w_out'].astype(BF16)
    router_w = _router_weights(p['w_router_group'], p['b_router_group'], p['w_router_expert'], p['b_router_expert'])
    seq_tiles = tp // QK_TM

    def front(x, table_pos, table_block):
        proj = in_proj(x, p['norm_mix_g'][None, :], w_in)
        return proj, qk_post(proj, table_pos, table_block, p['q_norm_g'], p['k_norm_g'], p['idx_k_norm_g'])

    def seqs(a, b, t):
        return a.reshape(b, t, a.shape[-1])

    xp = x_p.reshape(n_p, D_MODEL)
    proj_p, (q_b, kf_p, k_b, vf_p, v_b, qi_b, kif_p, ki_b, wi) = front(
        xp, jnp.arange(tp, dtype=I32), lambda i: i % seq_tiles)
    g_p, sre_p, sim_p = ssm(proj_p, ssm_w, p['ssm_D'], jnp.zeros((bp, SSM_LB, 2, SSM_SB), F32),
                            n_batch=bp, seq=tp, tc=512, row0=0)
    bq = 128
    n_buckets = min(8, tp // bq)
    per = tp // bq // n_buckets
    qp, qip, wip = seqs(q_b, bp, tp), seqs(qi_b, bp, tp), seqs(wi, bp, tp)
    kp, vp, kip = seqs(k_b, bp, tp), seqs(v_b, bp, tp), seqs(ki_b, bp, tp)
    attn_p = jnp.concatenate(
        [dsa(qp, qip, wip, kp, vp, kip, bq=bq, q_blk0=n * per, n_qblk=per, n_keys=(n + 1) * per * bq,
             n_sel=min(IDX_TOPK, tp // 4), packed_bisect=(n % 2 == 0), stack=(1 if n % 2 == 0 else KV_GROUP))
         for n in range(n_buckets)], axis=1).reshape(n_p, ATTN_WIDTH)
    merged_p = merge(g_p, attn_p, proj_p, *glu_w)
    h_p, hn_p, ri_p, rw_p, cnt_p = out_proj(xp, merged_p, w_out, p['norm_ffn_g'], router_w, jnp.zeros((1, LANES), F32))

    xs_ = x_s.reshape(n_s, D_MODEL)
    proj_s, (q_b, kf_s, k_b, vf_s, v_b, qi_b, kif_s, ki_b, wi) = front(
        xs_, jnp.tile(past + jnp.arange(ts, dtype=I32), QK_TM // ts), lambda i: 0)
    h0 = jnp.stack([h0_re.reshape(bs, SSM_LB, SSM_SB), h0_im.reshape(bs, SSM_LB, SSM_SB)]).transpose(2, 0, 1, 3)
    g_s, sre_s, sim_s = ssm_step(proj_s, ssm_w, p['ssm_D'], h0, n_batch=bs, seq=ts, row0=0)
    attn_s = dsa_step(seqs(q_b, bs, ts), seqs(qi_b, bs, ts), seqs(wi, bs, ts),
                      cache_k.reshape(bs, past, KV_WIDTH), cache_v.reshape(bs, past, KV_WIDTH), cache_ki,
                      seqs(k_b, bs, ts), seqs(v_b, bs, ts), seqs(ki_b, bs, ts),
                      n_sel=min(IDX_TOPK, (past + ts) // 4)).reshape(n_s, ATTN_WIDTH)
    merged_s = merge(g_s, attn_s, proj_s, *glu_w)
    h_s, hn_s, ri_s, rw_s, cnt = out_proj(xs_, merged_s, w_out, p['norm_ffn_g'], router_w, cnt_p)

    counts = cnt[0, N_EXPERT_GROUPS:ROUTER_COLS].astype(I32)
    pad_start, pad_end, n_used = _block_layout(counts)
    route_i = jnp.concatenate([ri_p, ri_s], axis=1)
    dest0 = pad_start[route_i[0]] + route_i[2]
    dest1 = pad_start[route_i[1]] + route_i[3]
    n_blocks = _moe_rows(n_tok) // MOE_TM
    blk = jnp.minimum(jnp.arange(n_blocks, dtype=I32), n_used - 1)
    blk_e = jnp.minimum(jnp.sum((pad_end[None, :] <= (blk * MOE_TM)[:, None]).astype(I32), axis=1), N_EXPERTS - 1)
    n_used = n_used.reshape(1)

    xs = dispatch(hn_p, hn_s, dest0, dest1, pad_end, counts, n_used)
    ys = moe(xs, blk_e, n_used, p['w_exp_gate'].astype(BF16), p['w_exp_up'].astype(BF16), p['w_exp_down'].astype(BF16))
    y_p = combine(ys, h_p, rw_p, dest0, dest1, tok0=0).reshape(bp, tp, D_MODEL)
    y_s = combine(ys, h_s, rw_s, dest0, dest1, tok0=n_p).reshape(bs, ts, D_MODEL)

    def heads(a, b, t):
        return a.reshape(b, t, N_KV_HEADS, HEAD_DIM)

    new_p = (heads(kf_p, bp, tp), heads(vf_p, bp, tp), kif_p.reshape(bp, tp, IDX_DIM), sre_p, sim_p)
    new_s = (heads(kf_s, bs, ts), heads(vf_s, bs, ts), kif_s.reshape(bs, ts, IDX_DIM), sre_s, sim_s)
    return y_p, y_s, new_p, new_s


def kernel(x_prompt, x_sample, cache_k, cache_v, cache_idx_k, state_ssm_re, state_ssm_im, norm_mix_g, w_in, q_norm_g, k_norm_g, idx_k_norm_g, ssm_A_re, ssm_A_im, ssm_log_dt, ssm_B_re, ssm_B_im, ssm_C_re, ssm_C_im, ssm_D, w_glu_val, w_glu_gate, w_attn_branch, w_out, norm_ffn_g, w_router_group, b_router_group, w_router_expert, b_router_expert, w_exp_gate, w_exp_up, w_exp_down):
    depth = w_in.shape[0]
    assert depth == 1, "prompt and sample tokens are batched through one layer"
    names = ('norm_mix_g', 'w_in', 'q_norm_g', 'k_norm_g', 'idx_k_norm_g', 'ssm_A_re', 'ssm_A_im', 'ssm_log_dt',
             'ssm_B_re', 'ssm_B_im', 'ssm_C_re', 'ssm_C_im', 'ssm_D', 'w_glu_val', 'w_glu_gate', 'w_attn_branch',
             'w_out', 'norm_ffn_g', 'w_router_group', 'b_router_group', 'w_router_expert', 'b_router_expert',
             'w_exp_gate', 'w_exp_up', 'w_exp_down')
    vals = (norm_mix_g, w_in, q_norm_g, k_norm_g, idx_k_norm_g, ssm_A_re, ssm_A_im, ssm_log_dt, ssm_B_re, ssm_B_im,
            ssm_C_re, ssm_C_im, ssm_D, w_glu_val, w_glu_gate, w_attn_branch, w_out, norm_ffn_g, w_router_group,
            b_router_group, w_router_expert, b_router_expert, w_exp_gate, w_exp_up, w_exp_down)
    p = {n: v[0] for n, v in zip(names, vals)}
    y_p, y_s, new_p, new_s = _layer(x_prompt, x_sample, cache_k[0], cache_v[0], cache_idx_k[0],
                                    state_ssm_re[0], state_ssm_im[0], p)
    st_p = tuple(a[None] for a in new_p)
    st_s = tuple(a[None] for a in new_s)
    return (y_p, y_s) + st_p + st_s
```

```python
import functools

import numpy as np
import jax
import jax.numpy as jnp
from jax import lax
from jax.experimental import pallas as pl
from jax.experimental.pallas import tpu as pltpu

F32 = jnp.float32
BF16 = jnp.bfloat16
I32 = jnp.int32

D_MODEL = 2048
CHUNK = 64
SSM_WIDTH = 1024
SSM_GROUP = 16
SSM_GROUPS = 64
SSM_STATE = 64
ATTN_WIDTH = 1024
HEAD_DIM = 128
N_HEADS = 8
N_KV_HEADS = 2
KV_GROUP = 4
IDX_HEADS = 8
IDX_DIM = 64
IDX_TOPK = 256
ROPE_THETA = 500000.0
N_EXPERT_GROUPS = 4
EXPERTS_PER_GROUP = 8
N_EXPERTS = 32
TOP_K = 2
EXPERT_FF = 1024
EPS = 1e-6

LANES = 128
SUBLANES = 8
VMEM_LIMIT = 56 * 1024 * 1024

COL_U, COL_Q, COL_GA, COL_GB, COL_K, COL_V, COL_QI, COL_KIWI = 0, 1024, 2048, 4096, 6144, 6400, 6656, 7168
PROJ_COLS = 7296
PROJ_TN = 2432
KV_WIDTH = N_KV_HEADS * HEAD_DIM

SSM_LB = SSM_WIDTH // LANES
SSM_SB = 8 * SSM_STATE

INT_MIN = np.int32(-2 ** 31)
KEY_NEG_INF = np.int32(np.array([0xFF800000], np.uint32).view(np.int32)[0] ^ 0x7FFFFFFF)


def _params(sem, vmem=VMEM_LIMIT):
    return pltpu.CompilerParams(dimension_semantics=sem, vmem_limit_bytes=vmem)


def _dot(a, b):
    return jnp.dot(a, b, preferred_element_type=F32)


def _dot_nt(a, b):
    return lax.dot_general(a, b, (((1,), (1,)), ((), ())), preferred_element_type=F32)


def _split_bf16(x):
    hi = x.astype(BF16)
    lo = (x - hi.astype(F32)).astype(BF16)
    return hi, lo


def _in_proj_kernel(x_ref, g_ref, w_ref, o_ref, xn_ref):
    @pl.when(pl.program_id(1) == 0)
    def _():
        x = x_ref[...]
        ms = jnp.mean(x * x, axis=-1, keepdims=True)
        xn_ref[...] = (x * lax.rsqrt(ms + EPS) * g_ref[...]).astype(BF16)

    o_ref[...] = _dot(xn_ref[...], w_ref[...])


def in_proj(x, gain, w_bf16, *, tm=512):
    n_tok = x.shape[0]
    return pl.pallas_call(
        _in_proj_kernel,
        grid=(n_tok // tm, PROJ_COLS // PROJ_TN),
        in_specs=[pl.BlockSpec((tm, D_MODEL), lambda i, j: (i, 0)),
                  pl.BlockSpec((1, D_MODEL), lambda i, j: (0, 0)),
                  pl.BlockSpec((D_MODEL, PROJ_TN), lambda i, j: (0, j))],
        out_specs=pl.BlockSpec((tm, PROJ_TN), lambda i, j: (i, j)),
        out_shape=jax.ShapeDtypeStruct((n_tok, PROJ_COLS), F32),
        scratch_shapes=[pltpu.VMEM((tm, D_MODEL), BF16)],
        compiler_params=_params(("arbitrary", "arbitrary")),
        name="in_proj",
    )(x, gain, w_bf16)


def _rope(x, c, s_lo, s_hi, half):
    n = x.shape[-1]
    return x * c + pltpu.roll(x, n - half, 1) * s_lo + pltpu.roll(x, half, 1) * s_hi


def _head_norm(x, g):
    ms = jnp.mean(x * x, axis=-1, keepdims=True)
    return x * lax.rsqrt(ms + EPS) * g


V_AUG = 2 * HEAD_DIM


def _store_v_aug(dst_ref, row0, v):
    n = v.shape[0]
    one_col = jnp.where(lax.broadcasted_iota(I32, (n, HEAD_DIM), 1) == 0, 1.0, 0.0).astype(BF16)
    for h in range(N_KV_HEADS):
        dst_ref[row0:row0 + n, h * V_AUG:h * V_AUG + HEAD_DIM] = v[:, h * HEAD_DIM:(h + 1) * HEAD_DIM].astype(BF16)
        dst_ref[row0:row0 + n, h * V_AUG + HEAD_DIM:(h + 1) * V_AUG] = one_col


def _qk_post_kernel(q_ref, k_ref, v_ref, qi_ref, kw_ref, c128_ref, sl128_ref, sh128_ref,
                    c64_ref, sl64_ref, sh64_ref, qg_ref, kg_ref, ig_ref,
                    qo_ref, kf_ref, kb_ref, vf_ref, vb_ref, qio_ref, kif_ref, kib_ref, wo_ref):
    c128, sl128, sh128 = c128_ref[...], sl128_ref[...], sh128_ref[...]
    c64, sl64, sh64 = c64_ref[...], sl64_ref[...], sh64_ref[...]
    half128 = HEAD_DIM // 8
    half64 = IDX_DIM // 8
    for h in range(N_HEADS):
        sl = slice(h * LANES, (h + 1) * LANES)
        qo_ref[:, sl] = _rope(_head_norm(q_ref[:, sl], qg_ref[...]), c128, sl128, sh128, half128).astype(BF16)
    for h in range(N_KV_HEADS):
        sl = slice(h * LANES, (h + 1) * LANES)
        kk = _rope(_head_norm(k_ref[:, sl], kg_ref[...]), c128, sl128, sh128, half128)
        kf_ref[:, sl] = kk
        kb_ref[:, sl] = kk.astype(BF16)
    v = v_ref[...]
    vf_ref[...] = v
    _store_v_aug(vb_ref, 0, v)
    lane = lax.broadcasted_iota(I32, c64.shape, 1)
    low = lane < IDX_DIM
    for p in range(IDX_HEADS // 2):
        x = _rope(qi_ref[:, p * LANES:(p + 1) * LANES], c64, sl64, sh64, half64)
        qio_ref[:, (2 * p) * LANES:(2 * p + 1) * LANES] = jnp.where(low, x, 0.0).astype(BF16)
        qio_ref[:, (2 * p + 1) * LANES:(2 * p + 2) * LANES] = jnp.where(low, pltpu.roll(x, IDX_DIM, 1), 0.0).astype(BF16)
    kw = kw_ref[...]
    ms = jnp.sum(jnp.where(low, kw * kw, 0.0), axis=-1, keepdims=True) * (1.0 / IDX_DIM)
    ki = _rope(kw * lax.rsqrt(ms + EPS) * ig_ref[...], c64, sl64, sh64, half64)
    kif_ref[...] = ki[:, :IDX_DIM]
    kib_ref[...] = jnp.where(low, ki, 0.0).astype(BF16)
    wo_ref[...] = (pltpu.roll(kw, IDX_DIM, 1) * IDX_HEADS ** -0.5) * IDX_DIM ** -0.5


def _rope_tables(pos, head_dim):
    r = head_dim // 4
    half = r // 2
    inv = ROPE_THETA ** (-jnp.arange(half, dtype=F32) * 2.0 / r)
    ang = pos.astype(F32)[:, None] * inv[None, :]
    cos, sin = jnp.cos(ang), jnp.sin(ang)
    n = pos.shape[0]
    zh = jnp.zeros((n, half), F32)
    rest = head_dim - r
    c = jnp.concatenate([cos, cos, jnp.ones((n, rest), F32)], axis=-1)
    s_lo = jnp.concatenate([-sin, zh, jnp.zeros((n, rest), F32)], axis=-1)
    s_hi = jnp.concatenate([zh, sin, jnp.zeros((n, rest), F32)], axis=-1)
    rep = LANES // head_dim
    return tuple(jnp.tile(t, (1, rep)) for t in (c, s_lo, s_hi))


QK_TM = 512


def qk_post(proj, table_pos, table_block, q_gain, k_gain, ik_gain):
    tm = QK_TM
    n_tok = proj.shape[0]
    t128 = _rope_tables(table_pos, HEAD_DIM)
    t64 = _rope_tables(table_pos, IDX_DIM)
    ik_gain128 = jnp.concatenate([ik_gain, jnp.zeros((LANES - IDX_DIM,), F32)])[None, :]

    def col(width, start):
        return pl.BlockSpec((tm, width), lambda i: (i, start // width))

    def row(width):
        return pl.BlockSpec((tm, width), lambda i: (i, 0))

    table = pl.BlockSpec((tm, LANES), lambda i: (table_block(i), 0))
    gain = pl.BlockSpec((1, LANES), lambda i: (0, 0))
    return pl.pallas_call(
        _qk_post_kernel,
        grid=(n_tok // tm,),
        in_specs=[col(ATTN_WIDTH, COL_Q), col(KV_WIDTH, COL_K), col(KV_WIDTH, COL_V), col(IDX_HEADS * IDX_DIM, COL_QI),
                  col(LANES, COL_KIWI)] + [table] * 6 + [gain] * 3,
        out_specs=[row(ATTN_WIDTH), row(KV_WIDTH), row(KV_WIDTH), row(KV_WIDTH), row(N_KV_HEADS * V_AUG), row(IDX_HEADS * LANES),
                   row(IDX_DIM), row(LANES), row(LANES)],
        out_shape=[jax.ShapeDtypeStruct((n_tok, ATTN_WIDTH), BF16),
                   jax.ShapeDtypeStruct((n_tok, KV_WIDTH), F32), jax.ShapeDtypeStruct((n_tok, KV_WIDTH), BF16),
                   jax.ShapeDtypeStruct((n_tok, KV_WIDTH), F32), jax.ShapeDtypeStruct((n_tok, N_KV_HEADS * V_AUG), BF16),
                   jax.ShapeDtypeStruct((n_tok, IDX_HEADS * LANES), BF16),
                   jax.ShapeDtypeStruct((n_tok, IDX_DIM), F32), jax.ShapeDtypeStruct((n_tok, LANES), BF16),
                   jax.ShapeDtypeStruct((n_tok, LANES), F32)],
        compiler_params=_params(("arbitrary",)),
        name="qk_post",
    )(proj, proj, proj, proj, proj, *t128, *t64, q_gain[None, :], k_gain[None, :], ik_gain128)


def _gelu_tanh(x):
    return 0.5 * x * (1.0 + jnp.tanh(np.float32(np.sqrt(2.0 / np.pi)) * (x + 0.044715 * (x * x * x))))


SSM_LT = SSM_SB // LANES
SSM_SEG = 64


def _ssm_kernel(u_ref, wb_ref, wc_ref, pw_ref, d_ref, h0_ref, g_ref, sre_ref, sim_ref,
                er_ref, ei_ref, car_ref, up_ref, yp_ref):
    c = pl.program_id(2)

    @pl.when(c == 0)
    def _():
        car_ref[...] = h0_ref[...]

    for j in range(SSM_SEG):
        up_ref[j * SUBLANES:(j + 1) * SUBLANES, :] = u_ref[pl.ds(j, SUBLANES, stride=SSM_SEG), :]
    e = _dot(up_ref[...].astype(BF16), wb_ref[...])
    tiles = [slice(lt * LANES, (lt + 1) * LANES) for lt in range(SSM_LT)]
    for lt, sl in enumerate(tiles):
        er_ref[lt] = e[:, sl]
        ei_ref[lt] = e[:, SSM_SB + lt * LANES:SSM_SB + (lt + 1) * LANES]

    def cmul_add(ar, ai, br, bi, cr, ci):
        return ar * br - ai * bi + cr, ar * bi + ai * br + ci

    lb = [(pw_ref[0, 0:1, sl], pw_ref[1, 0:1, sl]) for sl in tiles]
    zero = jnp.zeros((SUBLANES, LANES), F32)
    st = [(zero, zero)] * SSM_LT
    for j in range(SSM_SEG):
        rows = slice(j * SUBLANES, (j + 1) * SUBLANES)
        for lt in range(SSM_LT):
            st[lt] = cmul_add(*lb[lt], *st[lt], er_ref[lt, rows, :], ei_ref[lt, rows, :])
            er_ref[lt, rows, :] = st[lt][0]
            ei_ref[lt, rows, :] = st[lt][1]

    enter = []
    for lt, sl in enumerate(tiles):
        seg_r, seg_i = pw_ref[0, SSM_SEG - 1:SSM_SEG, sl], pw_ref[1, SSM_SEG - 1:SSM_SEG, sl]
        cr, ci = car_ref[0:1, sl], car_ref[1:2, sl]
        rows_r, rows_i = [], []
        for r in range(SUBLANES):
            rows_r.append(cr)
            rows_i.append(ci)
            cr, ci = cmul_add(seg_r, seg_i, cr, ci, st[lt][0][r:r + 1], st[lt][1][r:r + 1])
        car_ref[0:1, sl] = cr
        car_ref[1:2, sl] = ci
        enter.append((jnp.concatenate(rows_r, axis=0), jnp.concatenate(rows_i, axis=0)))

    for j in range(SSM_SEG):
        rows = slice(j * SUBLANES, (j + 1) * SUBLANES)
        for lt, sl in enumerate(tiles):
            xr, xi = cmul_add(pw_ref[0, j:j + 1, sl], pw_ref[1, j:j + 1, sl], *enter[lt],
                              er_ref[lt, rows, :], ei_ref[lt, rows, :])
            er_ref[lt, rows, :] = xr
            ei_ref[lt, rows, :] = xi

    y = None
    for lt, sl in enumerate(tiles):
        t = _dot(er_ref[lt].astype(BF16), wc_ref[0, sl, :]) - _dot(ei_ref[lt].astype(BF16), wc_ref[1, sl, :])
        y = t if y is None else y + t
    yp_ref[...] = y
    out_rows = 2 * SUBLANES
    for t0 in range(0, SUBLANES * SSM_SEG, out_rows):
        r, j0 = divmod(t0, SSM_SEG)
        rows = slice(t0, t0 + out_rows)
        yt = yp_ref[pl.ds(j0 * SUBLANES + r, out_rows, stride=SUBLANES), :] + d_ref[...] * u_ref[rows, :]
        g_ref[rows, :] = _gelu_tanh(yt).astype(BF16)

    @pl.when(c == pl.num_programs(2) - 1)
    def _():
        sre_ref[...] = car_ref[0:1, :]
        sim_ref[...] = car_ref[1:2, :]


def _ssm_weights(a_re, a_im, log_dt, b_re, b_im, c_re, c_im):
    lam_re, lam_im = a_re, a_im
    dt = jnp.exp(log_dt)[:, None]
    mag = jnp.exp(lam_re * dt)
    lb_re, lb_im = mag * jnp.cos(lam_im * dt), mag * jnp.sin(lam_im * dt)
    den = lam_re * lam_re + lam_im * lam_im
    num_re = lb_re - 1.0
    z_re = (num_re * lam_re + lb_im * lam_im) / den
    z_im = (lb_im * lam_re - num_re * lam_im) / den
    zb_re = z_re[:, :, None] * b_re - z_im[:, :, None] * b_im
    zb_im = z_re[:, :, None] * b_im + z_im[:, :, None] * b_re
    eye = jnp.eye(8, dtype=F32)

    def blockdiag_in(w):
        return jnp.einsum('jgph,gk->jghkp', w.reshape(SSM_LB, 8, SSM_STATE, SSM_GROUP), eye).reshape(SSM_LB, LANES, SSM_SB)

    def blockdiag_out(w):
        return jnp.einsum('jghp,gk->jkpgh', w.reshape(SSM_LB, 8, SSM_GROUP, SSM_STATE), eye).reshape(SSM_LB, SSM_SB, LANES)

    wb = jnp.concatenate([blockdiag_in(zb_re), blockdiag_in(zb_im)], axis=-1).astype(BF16)
    wc = jnp.stack([blockdiag_out(c_re), blockdiag_out(c_im)], axis=1).astype(BF16)

    pr, pi_ = lb_re.reshape(SSM_LB, 1, SSM_SB), lb_im.reshape(SSM_LB, 1, SSM_SB)
    while pr.shape[1] < SSM_SEG:
        tr, ti = pr[:, -1:], pi_[:, -1:]
        pr, pi_ = (jnp.concatenate([pr, pr * tr - pi_ * ti], axis=1), jnp.concatenate([pi_, pr * ti + pi_ * tr], axis=1))
    pw = jnp.stack([pr, pi_], axis=1)
    return wb, wc, pw


def ssm(proj, ssm_w, d_skip, h0, *, n_batch, seq, row0):
    wb, wc, pw = ssm_w
    tc = SUBLANES * SSM_SEG
    n_chunks = seq // tc
    blk0 = row0 // tc
    n_tok = n_batch * seq
    state_shape = jax.ShapeDtypeStruct((n_batch, SSM_LB, 1, SSM_SB), F32)
    state_spec = pl.BlockSpec((None, None, 1, SSM_SB), lambda b, j, c: (b, j, 0, 0))
    g, s_re, s_im = pl.pallas_call(
        _ssm_kernel,
        grid=(n_batch, SSM_LB, n_chunks),
        in_specs=[pl.BlockSpec((tc, LANES), lambda b, j, c: (blk0 + b * n_chunks + c, j)),
                  pl.BlockSpec((None, LANES, 2 * SSM_SB), lambda b, j, c: (j, 0, 0)),
                  pl.BlockSpec((None, 2, SSM_SB, LANES), lambda b, j, c: (j, 0, 0, 0)),
                  pl.BlockSpec((None, 2, SSM_SEG, SSM_SB), lambda b, j, c: (j, 0, 0, 0)),
                  pl.BlockSpec((1, LANES), lambda b, j, c: (0, j)),
                  pl.BlockSpec((None, None, 2, SSM_SB), lambda b, j, c: (b, j, 0, 0))],
        out_specs=[pl.BlockSpec((tc, LANES), lambda b, j, c: (b * n_chunks + c, j)), state_spec, state_spec],
        out_shape=[jax.ShapeDtypeStruct((n_tok, SSM_WIDTH), BF16), state_shape, state_shape],
        scratch_shapes=[pltpu.VMEM((SSM_LT, tc, LANES), F32), pltpu.VMEM((SSM_LT, tc, LANES), F32),
                        pltpu.VMEM((2, SSM_SB), F32), pltpu.VMEM((tc, LANES), F32), pltpu.VMEM((tc, LANES), F32)],
        compiler_params=_params(("arbitrary", "arbitrary", "arbitrary")),
        name="ssm",
    )(proj, wb, wc, pw, d_skip[None, :], h0)
    return g, s_re.reshape(n_batch, SSM_GROUPS, SSM_STATE), s_im.reshape(n_batch, SSM_GROUPS, SSM_STATE)


def _ssm_step_kernel(u_ref, wb_ref, wc_ref, pw_ref, d_ref, h0_ref, g_ref, sre_ref, sim_ref, er_ref, ei_ref, *, seq):
    n_seq = h0_ref.shape[1]
    u = u_ref[...]
    e = _dot(u.astype(BF16), wb_ref[...])
    n_lt = SSM_SB // LANES
    y = d_ref[...] * u
    for lt in range(n_lt):
        sl = slice(lt * LANES, (lt + 1) * LANES)
        er_ref[...] = e[:, lt * LANES:(lt + 1) * LANES]
        ei_ref[...] = e[:, SSM_SB + lt * LANES:SSM_SB + (lt + 1) * LANES]
        lr, li = pw_ref[0, 0:1, sl], pw_ref[1, 0:1, sl]
        sr, si = h0_ref[0, :, sl], h0_ref[1, :, sl]
        for t in range(seq):
            rows = pl.ds(t, n_seq, stride=seq)
            sr, si = lr * sr - li * si + er_ref[rows, :], lr * si + li * sr + ei_ref[rows, :]
            er_ref[rows, :] = sr
            ei_ref[rows, :] = si
        y = y + (_dot(er_ref[...].astype(BF16), wc_ref[0, sl, :]) - _dot(ei_ref[...].astype(BF16), wc_ref[1, sl, :]))
        sre_ref[:, sl] = sr
        sim_ref[:, sl] = si
    g_ref[...] = _gelu_tanh(y).astype(BF16)


def ssm_step(proj, ssm_w, d_skip, h0, *, n_batch, seq, row0):
    wb, wc, pw = ssm_w
    n_tok = n_batch * seq
    assert row0 % n_tok == 0
    state_shape = jax.ShapeDtypeStruct((SSM_LB, n_batch, SSM_SB), F32)
    state_spec = pl.BlockSpec((None, n_batch, SSM_SB), lambda j: (j, 0, 0))
    g, s_re, s_im = pl.pallas_call(
        functools.partial(_ssm_step_kernel, seq=seq),
        grid=(SSM_LB,),
        in_specs=[pl.BlockSpec((n_tok, LANES), lambda j: (row0 // n_tok, j)),
                  pl.BlockSpec((None, LANES, 2 * SSM_SB), lambda j: (j, 0, 0)),
                  pl.BlockSpec((None, 2, SSM_SB, LANES), lambda j: (j, 0, 0, 0)),
                  pl.BlockSpec((None, 2, SSM_SEG, SSM_SB), lambda j: (j, 0, 0, 0)),
                  pl.BlockSpec((1, LANES), lambda j: (0, j)),
                  pl.BlockSpec((None, 2, n_batch, SSM_SB), lambda j: (j, 0, 0, 0))],
        out_specs=[pl.BlockSpec((n_tok, LANES), lambda j: (0, j)), state_spec, state_spec],
        out_shape=[jax.ShapeDtypeStruct((n_tok, SSM_WIDTH), BF16), state_shape, state_shape],
        scratch_shapes=[pltpu.VMEM((n_tok, LANES), F32), pltpu.VMEM((n_tok, LANES), F32)],
        compiler_params=_params(("arbitrary",)),
        name="ssm_step",
    )(proj, wb, wc, pw, d_skip[None, :], h0)

    def per_seq(s):
        return s.transpose(1, 0, 2).reshape(n_batch, SSM_GROUPS, SSM_STATE)

    return g, per_seq(s_re), per_seq(s_im)


def _row_sum(x):
    return jnp.sum(x, axis=1, keepdims=True)


def _row_count(mask):
    return _row_sum(jnp.where(mask, 1, 0))


I16 = jnp.int16
I16_MIN = -2 ** 15


def _count16(ref, cand, compare):
    accs = [None] * 4
    for t in range(ref.shape[1] // LANES):
        x = jnp.where(compare(ref[:, t * LANES:(t + 1) * LANES], cand), I16(1), I16(0))
        accs[t % 4] = x if accs[t % 4] is None else accs[t % 4] + x
    accs = [a for a in accs if a is not None]
    total = accs[0]
    for a in accs[1:]:
        total = total + a
    return _row_sum(total.astype(I32))


def _bisect16(ref, target):
    def step(i, base):
        cand = base + lax.shift_left(np.int32(1), np.int32(15) - i)
        cnt = _count16(ref, cand.astype(I16), lambda a, b: a >= b)
        return jnp.where(cnt >= target, cand, base)
    return lax.fori_loop(0, 16, step, jnp.full((ref.shape[0], 1), I16_MIN, I32))


def _stack_heads(ref, heads):
    return jnp.concatenate([ref[:, h * LANES:(h + 1) * LANES] for h in heads], axis=0)


def _dsa_body(q_ref, qi_ref, wi_ref, k_ref, v_ref, ki_ref, o_ref, key_ref, bias_ref, hi_ref, lo_ref,
              *, q_pos_first, s_valid, n_sel, packed_bisect, stack):
    bq, n_keys = key_ref.shape
    col = lax.broadcasted_iota(I32, (bq, n_keys), 1)
    qpos = q_pos_first + lax.broadcasted_iota(I32, (bq, 1), 0)
    allowed = col < jnp.minimum((qpos // CHUNK + 1) * CHUNK, s_valid)

    ki = ki_ref[...]
    score = None
    for h0 in range(0, IDX_HEADS, stack):
        d = _dot_nt(_stack_heads(qi_ref, range(h0, h0 + stack)), ki)
        for j in range(stack):
            t = jnp.maximum(d[j * bq:(j + 1) * bq], 0.0) * wi_ref[:, h0 + j:h0 + j + 1]
            score = t if score is None else score + t
    score = jnp.where(score == 0.0, 0.0, score)
    bits = pltpu.bitcast(score, I32)
    key = jnp.where(bits < 0, bits ^ np.int32(0x7FFFFFFF), bits)
    key = jnp.where(allowed, key, KEY_NEG_INF)
    key_ref[...] = key

    if packed_bisect:
        hi_ref[...] = (key >> 16).astype(I16)
        lo_ref[...] = ((key & 0xFFFF) + I16_MIN).astype(I16)
        thr_hi = _bisect16(hi_ref, n_sel)
        thr_hi16 = thr_hi.astype(I16)
        need_lo = n_sel - _count16(hi_ref, thr_hi16, lambda a, b: a > b)
        lo_ref[...] = jnp.where(hi_ref[...] == thr_hi16, lo_ref[...], I16(I16_MIN))
        thr_lo = _bisect16(lo_ref, need_lo)
        thr = lax.shift_left(thr_hi, np.int32(16)) + (thr_lo - I16_MIN)
    else:
        def bisect(i, base):
            cand = base + lax.shift_left(np.int32(1), np.int32(31) - i)
            cnt = _row_count(key_ref[...] >= cand)
            return jnp.where(cnt >= n_sel, cand, base)
        thr = lax.fori_loop(0, 32, bisect, jnp.full((bq, 1), INT_MIN, I32))
    thr = jnp.maximum(thr, KEY_NEG_INF)

    key = key_ref[...]
    need = n_sel - _row_count(key > thr)
    n_eq = _row_count(key == thr)
    n_bits = int(n_keys - 1).bit_length()

    def tie_cut():
        def step(i, j0):
            cand = j0 + lax.shift_left(np.int32(1), np.int32(n_bits - 1) - i)
            cnt = _row_sum(jnp.where(key_ref[...] == thr, jnp.where(col < cand, 1, 0), 0))
            return jnp.where(cnt < need, cand, j0)
        return lax.fori_loop(0, n_bits, step, jnp.zeros((bq, 1), I32))

    split = jnp.max(jnp.where(n_eq > need, 1, 0)) > 0
    j_last = lax.cond(split, tie_cut, lambda: jnp.full((bq, 1), n_keys, I32))
    tie_bias = jnp.where(thr == KEY_NEG_INF, -jnp.inf, 0.0)
    bias_ref[...] = jnp.where(key > thr, 0.0,
                              jnp.where(key == thr, jnp.where(col <= j_last, tie_bias, -jnp.inf), -jnp.inf))

    c = np.float32(HEAD_DIM ** -0.5 * np.log2(np.e))
    for h0 in range(0, N_HEADS, stack):
        kv = h0 // KV_GROUP
        heads = range(h0, h0 + stack)
        s_all = _dot_nt(_stack_heads(q_ref, heads), k_ref[:, kv * HEAD_DIM:(kv + 1) * HEAD_DIM])
        ps = []
        for g in range(stack):
            s = s_all[g * bq:(g + 1) * bq] + bias_ref[...]
            m = jnp.max(s, axis=1, keepdims=True)
            ps.append(jnp.exp2((s - m) * c).astype(BF16))
        pv = _dot(jnp.concatenate(ps, axis=0), v_ref[:, kv * V_AUG:(kv + 1) * V_AUG])
        for g, h in enumerate(heads):
            o = pv[g * bq:(g + 1) * bq]
            o_ref[:, h * HEAD_DIM:(h + 1) * HEAD_DIM] = (o[:, :HEAD_DIM] / o[:, HEAD_DIM:HEAD_DIM + 1]).astype(BF16)


def _dsa_scratch(bq, n_keys):
    return [pltpu.VMEM((bq, n_keys), I32), pltpu.VMEM((bq, n_keys), F32),
            pltpu.VMEM((bq, n_keys), I16), pltpu.VMEM((bq, n_keys), I16)]


def _dsa_kernel(q_ref, qi_ref, wi_ref, k_ref, v_ref, ki_ref, o_ref, *scratch, q_pos0, **static):
    bq = scratch[0].shape[0]
    _dsa_body(q_ref, qi_ref, wi_ref, k_ref, v_ref, ki_ref, o_ref, *scratch,
              q_pos_first=q_pos0 + pl.program_id(1) * bq, **static)


def dsa(q, qi, wi, k, v, ki, *, bq, q_blk0, n_qblk, n_keys, n_sel, packed_bisect, stack):
    n_batch, seq = q.shape[:2]

    def qspec(width):
        return pl.BlockSpec((None, bq, width), lambda b, i: (b, q_blk0 + i, 0))

    def kspec(width):
        return pl.BlockSpec((None, n_keys, width), lambda b, i: (b, 0, 0))

    return pl.pallas_call(
        functools.partial(_dsa_kernel, q_pos0=q_blk0 * bq, s_valid=seq, n_sel=n_sel, packed_bisect=packed_bisect,
                          stack=stack),
        grid=(n_batch, n_qblk),
        in_specs=[qspec(ATTN_WIDTH), qspec(IDX_HEADS * LANES), qspec(LANES), kspec(KV_WIDTH), kspec(N_KV_HEADS * V_AUG),
                  kspec(LANES)],
        out_specs=pl.BlockSpec((None, bq, ATTN_WIDTH), lambda b, i: (b, i, 0)),
        out_shape=jax.ShapeDtypeStruct((n_batch, n_qblk * bq, ATTN_WIDTH), BF16),
        scratch_shapes=_dsa_scratch(bq, n_keys),
        compiler_params=_params(("arbitrary", "arbitrary")),
        name="dsa",
    )(q, qi, wi, k, v, ki)


def _dsa_step_kernel(q_ref, qi_ref, wi_ref, ck_ref, cv_ref, cki_ref, nk_ref, nv_ref, nki_ref, o_ref,
                     k_buf, v_buf, ki_buf, *scratch, past, n_sel):
    ts = nk_ref.shape[0]
    n_keys = k_buf.shape[0]
    k_buf[0:past, :] = ck_ref[...].astype(BF16)
    _store_v_aug(v_buf, 0, cv_ref[...])
    for buf, new in ((k_buf, nk_ref), (v_buf, nv_ref)):
        buf[past:past + ts, :] = new[...]
        buf[past + ts:n_keys, :] = jnp.zeros((n_keys - past - ts, buf.shape[1]), BF16)
    ki_buf[0:past, 0:IDX_DIM] = cki_ref[...].astype(BF16)
    ki_buf[0:past, IDX_DIM:LANES] = jnp.zeros((past, LANES - IDX_DIM), BF16)
    ki_buf[past:past + ts, :] = nki_ref[...]
    ki_buf[past + ts:n_keys, :] = jnp.zeros((n_keys - past - ts, LANES), BF16)
    _dsa_body(q_ref, qi_ref, wi_ref, k_buf, v_buf, ki_buf, o_ref, *scratch,
              q_pos_first=past, s_valid=past + ts, n_sel=n_sel, packed_bisect=True, stack=KV_GROUP)


def dsa_step(q, qi, wi, cache_k, cache_v, cache_ki, k_new, v_new, ki_new, *, n_sel):
    n_batch, ts = q.shape[:2]
    past = cache_k.shape[1]
    n_keys = -(-(past + ts) // LANES) * LANES

    def spec(rows, width):
        return pl.BlockSpec((None, rows, width), lambda b: (b, 0, 0))

    return pl.pallas_call(
        functools.partial(_dsa_step_kernel, past=past, n_sel=n_sel),
        grid=(n_batch,),
        in_specs=[spec(ts, ATTN_WIDTH), spec(ts, IDX_HEADS * LANES), spec(ts, LANES),
                  spec(past, KV_WIDTH), spec(past, KV_WIDTH), spec(past, IDX_DIM),
                  spec(ts, KV_WIDTH), spec(ts, N_KV_HEADS * V_AUG), spec(ts, LANES)],
        out_specs=spec(ts, ATTN_WIDTH),
        out_shape=jax.ShapeDtypeStruct((n_batch, ts, ATTN_WIDTH), BF16),
        scratch_shapes=[pltpu.VMEM((n_keys, KV_WIDTH), BF16), pltpu.VMEM((n_keys, N_KV_HEADS * V_AUG), BF16),
                        pltpu.VMEM((n_keys, LANES), BF16),
                        *_dsa_scratch(ts, n_keys)],
        compiler_params=_params(("arbitrary",)),
        name="dsa_step",
    )(q, qi, wi, cache_k, cache_v, cache_ki, k_new, v_new, ki_new)


def _merge_kernel(g_ref, a_ref, ga_ref, gb_ref, wv_ref, wg_ref, wb_ref, o_ref):
    g = g_ref[...]
    branch_a = _dot(g, wv_ref[...]) * jax.nn.sigmoid(_dot(g, wg_ref[...]))
    branch_b = _dot(a_ref[...], wb_ref[...])
    merged = jax.nn.sigmoid(ga_ref[...]) * branch_a + jax.nn.sigmoid(gb_ref[...]) * branch_b
    o_ref[...] = merged.astype(BF16)


def merge(g, attn, proj, w_val, w_gate, w_branch, *, tm=1024, tn=512):
    n_tok = g.shape[0]
    nj = D_MODEL // tn

    def wspec():
        return pl.BlockSpec((SSM_WIDTH, tn), lambda i, j: (0, j))

    return pl.pallas_call(
        _merge_kernel,
        grid=(n_tok // tm, nj),
        in_specs=[pl.BlockSpec((tm, SSM_WIDTH), lambda i, j: (i, 0)),
                  pl.BlockSpec((tm, ATTN_WIDTH), lambda i, j: (i, 0)),
                  pl.BlockSpec((tm, tn), lambda i, j: (i, COL_GA // tn + j)),
                  pl.BlockSpec((tm, tn), lambda i, j: (i, COL_GB // tn + j)),
                  wspec(), wspec(), wspec()],
        out_specs=pl.BlockSpec((tm, tn), lambda i, j: (i, j)),
        out_shape=jax.ShapeDtypeStruct((n_tok, D_MODEL), BF16),
        compiler_params=_params(("arbitrary", "arbitrary")),
        name="merge",
    )(g, attn, proj, proj, w_val, w_gate, w_branch)


ROUTER_COLS = N_EXPERT_GROUPS + N_EXPERTS
MOE_TM = 256


def _first_lane_of_max(x, lane_f):
    m = jnp.max(x, axis=1, keepdims=True)
    return m, jnp.min(jnp.where(x == m, lane_f, float(LANES)), axis=1, keepdims=True)


def _out_proj_kernel(x_ref, m_ref, wo_ref, gn_ref, wrh_ref, wrl_ref, br_ref, cin_ref,
                     h_ref, hn_ref, ri_ref, rw_ref, cnt_ref, carry_ref):
    @pl.when(pl.program_id(0) == 0)
    def _():
        carry_ref[...] = cin_ref[...]

    h = x_ref[...] + _dot(m_ref[...], wo_ref[...])
    h_ref[...] = h
    ms = jnp.mean(h * h, axis=-1, keepdims=True)
    hn = h * lax.rsqrt(ms + EPS) * gn_ref[...]
    hn_ref[...] = hn
    hh, hl = _split_bf16(hn)
    wrh = wrh_ref[...]
    lg = _dot(hh, wrh) + _dot(hl, wrh) + _dot(hh, wrl_ref[...]) + br_ref[...]

    tm = lg.shape[0]
    lane = lax.broadcasted_iota(I32, lg.shape, 1)
    lane_f = lane.astype(F32)
    ninf = -jnp.inf
    gl = jnp.where(lane < N_EXPERT_GROUPS, lg, ninf)
    gmax, gsel = _first_lane_of_max(gl, lane_f)
    g_w = 1.0 / jnp.sum(jnp.exp(gl - gmax), axis=1, keepdims=True)
    lo = N_EXPERT_GROUPS + EXPERTS_PER_GROUP * gsel
    el = jnp.where(lane_f >= lo, jnp.where(lane_f < lo + EXPERTS_PER_GROUP, lg, ninf), ninf)
    v1, i1 = _first_lane_of_max(el, lane_f)
    el2 = jnp.where(lane_f == i1, ninf, el)
    v2, i2 = _first_lane_of_max(el2, lane_f)
    t = jnp.exp(v2 - v1)
    s1 = 1.0 / (1.0 + t)
    w1 = s1 * g_w
    w2 = (t * s1) * g_w

    m1 = jnp.where(lane_f == i1, 1.0, 0.0)
    m2 = jnp.where(lane_f == i2, 1.0, 0.0)
    both = m1 + m2
    tri = jnp.where(lax.broadcasted_iota(I32, (tm, tm), 0) > lax.broadcasted_iota(I32, (tm, tm), 1), 1.0, 0.0)
    before = _dot(tri.astype(BF16), both.astype(BF16)) + carry_ref[...]
    r1 = jnp.sum(before * m1, axis=1, keepdims=True)
    r2 = jnp.sum(before * m2, axis=1, keepdims=True)
    carry_ref[...] = carry_ref[...] + jnp.sum(both, axis=0, keepdims=True)
    cnt_ref[...] = carry_ref[...]
    e1 = i1 - float(N_EXPERT_GROUPS)
    e2 = i2 - float(N_EXPERT_GROUPS)
    fields = jnp.where(lane == 0, e1, jnp.where(lane == 1, e2, jnp.where(lane == 2, r1, jnp.where(lane == 3, r2, 0.0))))
    ri_ref[...] = fields.T[0:SUBLANES, :].astype(I32)
    rw_ref[...] = jnp.where(lane == 0, w1, jnp.where(lane == 1, w2, 0.0))


def _router_weights(w_router_group, b_router_group, w_router_expert, b_router_expert):
    wr = jnp.concatenate([w_router_group, w_router_expert, jnp.zeros((D_MODEL, LANES - ROUTER_COLS), F32)], axis=1)
    wr_hi = wr.astype(BF16)
    wr_lo = (wr - wr_hi.astype(F32)).astype(BF16)
    br = jnp.concatenate([b_router_group, b_router_expert, jnp.zeros((LANES - ROUTER_COLS,), F32)])[None, :]
    return wr_hi, wr_lo, br


def out_proj(x, merged, w_out, ffn_gain, router_w, counts_in, *, tm=256):
    n_tok = x.shape[0]
    wr_hi, wr_lo, br = router_w

    def row(width):
        return pl.BlockSpec((tm, width), lambda i: (i, 0))

    def const(shape):
        return pl.BlockSpec(shape, lambda i: (0, 0), pipeline_mode=pl.Buffered(1))

    return pl.pallas_call(
        _out_proj_kernel,
        grid=(n_tok // tm,),
        in_specs=[row(D_MODEL), row(D_MODEL), const((D_MODEL, D_MODEL)), const((1, D_MODEL)),
                  const((D_MODEL, LANES)), const((D_MODEL, LANES)), const((1, LANES)), const((1, LANES))],
        out_specs=[row(D_MODEL), row(D_MODEL), pl.BlockSpec((SUBLANES, tm), lambda i: (0, i)), row(LANES),
                   pl.BlockSpec((1, LANES), lambda i: (0, 0))],
        out_shape=[jax.ShapeDtypeStruct((n_tok, D_MODEL), F32), jax.ShapeDtypeStruct((n_tok, D_MODEL), F32),
                   jax.ShapeDtypeStruct((SUBLANES, n_tok), I32), jax.ShapeDtypeStruct((n_tok, LANES), F32),
                   jax.ShapeDtypeStruct((1, LANES), F32)],
        scratch_shapes=[pltpu.VMEM((1, LANES), F32)],
        compiler_params=_params(("arbitrary",)),
        name="out_proj",
    )(x, merged, w_out, ffn_gain[None, :], wr_hi, wr_lo, br, counts_in)


def _block_layout(counts):
    padded = (counts + MOE_TM - 1) // MOE_TM * MOE_TM
    pad_end = jnp.cumsum(padded).astype(I32)
    pad_start = pad_end - padded
    n_used = pad_end[-1] // MOE_TM
    return pad_start, pad_end, n_used


def _moe_rows(n_tok):
    return -(-(n_tok * TOP_K + N_EXPERTS * (MOE_TM - 1)) // MOE_TM) * MOE_TM


DISPATCH_TM = 512


def _wait_rows(src_hbm, dst, sem, n_rows):
    pltpu.make_async_copy(src_hbm.at[pl.ds(0, n_rows)], dst, sem).wait()


def _dispatch_kernel(d0_ref, d1_ref, pe_ref, cnt_ref, nu_ref, hna_ref, hnb_ref, xs_hbm, zbuf, sem, semz,
                     *, n_blocks, a_tiles):
    i = pl.program_id(0)

    def zero_block(row0):
        return pltpu.make_async_copy(zbuf, xs_hbm.at[pl.ds(pl.multiple_of(row0, MOE_TM), MOE_TM)], semz)

    @pl.when(i == 0)
    def _():
        zbuf[...] = jnp.zeros_like(zbuf)
        for start in (True, False):
            for e in range(N_EXPERTS):
                @pl.when(cnt_ref[e] > 0)
                def _():
                    cp = zero_block(pe_ref[e] - MOE_TM)
                    cp.start() if start else cp.wait()

            def tail(b, c):
                cp = zero_block(b * MOE_TM)
                cp.start() if start else cp.wait()
                return c
            lax.fori_loop(nu_ref[0], n_blocks, tail, 0)

    base = i * DISPATCH_TM

    def scatter(hn_ref):
        def body(r, c):
            src = hn_ref.at[pl.ds(r, 1)]
            pltpu.make_async_copy(src, xs_hbm.at[pl.ds(d0_ref[base + r], 1)], sem).start()
            pltpu.make_async_copy(src, xs_hbm.at[pl.ds(d1_ref[base + r], 1)], sem).start()
            return c
        lax.fori_loop(0, DISPATCH_TM, body, 0, unroll=8)
        for _ in range(TOP_K):
            pltpu.make_async_copy(hn_ref, xs_hbm.at[pl.ds(0, DISPATCH_TM)], sem).wait()

    @pl.when(i < a_tiles)
    def _():
        scatter(hna_ref)

    @pl.when(i >= a_tiles)
    def _():
        scatter(hnb_ref)


def dispatch(hn_a, hn_b, dest0, dest1, pad_end, counts, n_used):
    a_tiles, b_tiles = hn_a.shape[0] // DISPATCH_TM, hn_b.shape[0] // DISPATCH_TM
    rows = _moe_rows(hn_a.shape[0] + hn_b.shape[0])
    grid_spec = pltpu.PrefetchScalarGridSpec(
        num_scalar_prefetch=5,
        grid=(a_tiles + b_tiles,),
        in_specs=[pl.BlockSpec((DISPATCH_TM, D_MODEL), lambda i, *_: (jnp.minimum(i, a_tiles - 1), 0)),
                  pl.BlockSpec((DISPATCH_TM, D_MODEL), lambda i, *_: (jnp.maximum(i - a_tiles, 0), 0))],
        out_specs=pl.BlockSpec(memory_space=pl.ANY),
        scratch_shapes=[pltpu.VMEM((MOE_TM, D_MODEL), F32), pltpu.SemaphoreType.DMA(()), pltpu.SemaphoreType.DMA(())],
    )
    return pl.pallas_call(
        functools.partial(_dispatch_kernel, n_blocks=rows // MOE_TM, a_tiles=a_tiles),
        grid_spec=grid_spec,
        out_shape=jax.ShapeDtypeStruct((rows, D_MODEL), F32),
        compiler_params=_params(("arbitrary",)),
        name="dispatch",
    )(dest0, dest1, pad_end, counts, n_used, hn_a, hn_b)


def _moe_kernel(blk_e_ref, nu_ref, xs_ref, wg_ref, wu_ref, wd_ref, ys_ref):
    i = pl.program_id(0)

    @pl.when(i < nu_ref[0])
    def _():
        x = xs_ref[...].astype(BF16)
        hg = _dot(x, wg_ref[...])
        hu = _dot(x, wu_ref[...])
        hmid = (jax.nn.silu(hg) * hu).astype(BF16)
        ys_ref[...] = _dot(hmid, wd_ref[...])

    @pl.when(i >= nu_ref[0])
    def _():
        ys_ref[...] = jnp.zeros_like(ys_ref)


def moe(xs, blk_e, n_used, w_gate, w_up, w_down):
    rows = xs.shape[0]
    grid_spec = pltpu.PrefetchScalarGridSpec(
        num_scalar_prefetch=2,
        grid=(rows // MOE_TM,),
        in_specs=[pl.BlockSpec((MOE_TM, D_MODEL), lambda i, be, nu: (jnp.minimum(i, nu[0] - 1), 0)),
                  pl.BlockSpec((None, D_MODEL, EXPERT_FF), lambda i, be, nu: (be[i], 0, 0)),
                  pl.BlockSpec((None, D_MODEL, EXPERT_FF), lambda i, be, nu: (be[i], 0, 0)),
                  pl.BlockSpec((None, EXPERT_FF, D_MODEL), lambda i, be, nu: (be[i], 0, 0))],
        out_specs=pl.BlockSpec((MOE_TM, D_MODEL), lambda i, be, nu: (i, 0)),
    )
    return pl.pallas_call(
        _moe_kernel,
        grid_spec=grid_spec,
        out_shape=jax.ShapeDtypeStruct((rows, D_MODEL), F32),
        compiler_params=_params(("arbitrary",)),
        name="moe",
    )(blk_e, n_used, xs, w_gate, w_up, w_down)


def _gather_rows(idx_ref, idx0, src_hbm, dst, sem, n_rows):
    def body(r, carry):
        t = idx_ref[idx0 + r]
        pltpu.make_async_copy(src_hbm.at[pl.ds(t, 1)], dst.at[pl.ds(r, 1)], sem).start()
        return carry
    lax.fori_loop(0, n_rows, body, 0, unroll=8)


def _combine_kernel(r0_ref, r1_ref, ys_hbm, h_ref, w_ref, o_ref, buf, sem, *, tm, tok0):
    i = pl.program_id(0)

    def issue(block, slot):
        _gather_rows(r0_ref, tok0 + block * tm, ys_hbm, buf.at[slot, 0], sem.at[slot], tm)
        _gather_rows(r1_ref, tok0 + block * tm, ys_hbm, buf.at[slot, 1], sem.at[slot], tm)

    @pl.when(i == 0)
    def _():
        issue(0, 0)

    @pl.when(i + 1 < pl.num_programs(0))
    def _():
        issue(i + 1, (i + 1) % 2)

    slot = i % 2
    _wait_rows(ys_hbm, buf.at[slot, 0], sem.at[slot], tm)
    _wait_rows(ys_hbm, buf.at[slot, 1], sem.at[slot], tm)
    w = w_ref[...]
    o_ref[...] = h_ref[...] + (buf[slot, 0] * w[:, 0:1] + buf[slot, 1] * w[:, 1:2])


def combine(ys, h, route_w, rows0, rows1, *, tok0, tm=256):
    n_tok = h.shape[0]
    grid_spec = pltpu.PrefetchScalarGridSpec(
        num_scalar_prefetch=2,
        grid=(n_tok // tm,),
        in_specs=[pl.BlockSpec(memory_space=pl.ANY),
                  pl.BlockSpec((tm, D_MODEL), lambda i, a, b: (i, 0)),
                  pl.BlockSpec((tm, LANES), lambda i, a, b: (i, 0))],
        out_specs=pl.BlockSpec((tm, D_MODEL), lambda i, a, b: (i, 0)),
        scratch_shapes=[pltpu.VMEM((2, 2, tm, D_MODEL), F32), pltpu.SemaphoreType.DMA((2,))],
    )
    return pl.pallas_call(
        functools.partial(_combine_kernel, tm=tm, tok0=tok0),
        grid_spec=grid_spec,
        out_shape=jax.ShapeDtypeStruct((n_tok, D_MODEL), F32),
        compiler_params=_params(("arbitrary",)),
        name="combine",
    )(rows0, rows1, ys, h, route_w)


def _regroup_w_in(w_in):
    sizes = (SSM_WIDTH, ATTN_WIDTH, KV_WIDTH, KV_WIDTH, IDX_HEADS * IDX_DIM, IDX_DIM, IDX_HEADS, D_MODEL, D_MODEL)
    u, q, k, v, qi, ki, wi, ga, gb = jnp.split(w_in, np.cumsum(sizes)[:-1].tolist(), axis=1)
    pad = jnp.zeros((D_MODEL, PROJ_COLS - COL_KIWI - IDX_DIM - IDX_HEADS), F32)
    return jnp.concatenate([u, q, ga, gb, k, v, qi, ki, wi, pad], axis=1).astype(BF16)


def _layer(x_p, x_s, cache_k, cache_v, cache_ki, h0_re, h0_im, p):
    bp, tp, _ = x_p.shape
    bs, ts, _ = x_s.shape
    past = cache_k.shape[1]
    n_p, n_s = bp * tp, bs * ts
    n_tok = n_p + n_s

    w_in = _regroup_w_in(p['w_in'])
    ssm_w = _ssm_weights(p['ssm_A_re'], p['ssm_A_im'], p['ssm_log_dt'], p['ssm_B_re'], p['ssm_B_im'],
                         p['ssm_C_re'], p['ssm_C_im'])
    glu_w = (p['w_glu_val'].astype(BF16), p['w_glu_gate'].astype(BF16), p['w_attn_branch'].astype(BF16))
    w_out = p['w_out'].astype(BF16)
    router_w = _router_weights(p['w_router_group'], p['b_router_group'], p['w_router_expert'], p['b_router_expert'])
    seq_tiles = tp // QK_TM

    def front(x, table_pos, table_block):
        proj = in_proj(x, p['norm_mix_g'][None, :], w_in)
        return proj, qk_post(proj, table_pos, table_block, p['q_norm_g'], p['k_norm_g'], p['idx_k_norm_g'])

    def seqs(a, b, t):
        return a.reshape(b, t, a.shape[-1])

    xp = x_p.reshape(n_p, D_MODEL)
    proj_p, (q_b, kf_p, k_b, vf_p, v_b, qi_b, kif_p, ki_b, wi) = front(
        xp, jnp.arange(tp, dtype=I32), lambda i: i % seq_tiles)
    g_p, sre_p, sim_p = ssm(proj_p, ssm_w, p['ssm_D'], jnp.zeros((bp, SSM_LB, 2, SSM_SB), F32),
                            n_batch=bp, seq=tp, row0=0)
    bq = 128
    n_buckets = min(8, tp // bq)
    per = tp // bq // n_buckets
    qp, qip, wip = seqs(q_b, bp, tp), seqs(qi_b, bp, tp), seqs(wi, bp, tp)
    kp, vp, kip = seqs(k_b, bp, tp), seqs(v_b, bp, tp), seqs(ki_b, bp, tp)
    attn_p = jnp.concatenate(
        [dsa(qp, qip, wip, kp, vp, kip, bq=bq, q_blk0=n * per, n_qblk=per, n_keys=(n + 1) * per * bq,
             n_sel=min(IDX_TOPK, tp // 4), packed_bisect=False, stack=1)
         for n in range(n_buckets)], axis=1).reshape(n_p, ATTN_WIDTH)
    merged_p = merge(g_p, attn_p, proj_p, *glu_w)
    h_p, hn_p, ri_p, rw_p, cnt_p = out_proj(xp, merged_p, w_out, p['norm_ffn_g'], router_w, jnp.zeros((1, LANES), F32))

    xs_ = x_s.reshape(n_s, D_MODEL)
    proj_s, (q_b, kf_s, k_b, vf_s, v_b, qi_b, kif_s, ki_b, wi) = front(
        xs_, jnp.tile(past + jnp.arange(ts, dtype=I32), QK_TM // ts), lambda i: 0)
    h0 = jnp.stack([h0_re.reshape(bs, SSM_LB, SSM_SB), h0_im.reshape(bs, SSM_LB, SSM_SB)]).transpose(2, 0, 1, 3)
    g_s, sre_s, sim_s = ssm_step(proj_s, ssm_w, p['ssm_D'], h0, n_batch=bs, seq=ts, row0=0)
    attn_s = dsa_step(seqs(q_b, bs, ts), seqs(qi_b, bs, ts), seqs(wi, bs, ts),
                      cache_k.reshape(bs, past, KV_WIDTH), cache_v.reshape(bs, past, KV_WIDTH), cache_ki,
                      seqs(k_b, bs, ts), seqs(v_b, bs, ts), seqs(ki_b, bs, ts),
                      n_sel=min(IDX_TOPK, (past + ts) // 4)).reshape(n_s, ATTN_WIDTH)
    merged_s = merge(g_s, attn_s, proj_s, *glu_w)
    h_s, hn_s, ri_s, rw_s, cnt = out_proj(xs_, merged_s, w_out, p['norm_ffn_g'], router_w, cnt_p)

    counts = cnt[0, N_EXPERT_GROUPS:ROUTER_COLS].astype(I32)
    pad_start, pad_end, n_used = _block_layout(counts)
    route_i = jnp.concatenate([ri_p, ri_s], axis=1)
    dest0 = pad_start[route_i[0]] + route_i[2]
    dest1 = pad_start[route_i[1]] + route_i[3]
    n_blocks = _moe_rows(n_tok) // MOE_TM
    blk = jnp.minimum(jnp.arange(n_blocks, dtype=I32), n_used - 1)
    blk_e = jnp.minimum(jnp.sum((pad_end[None, :] <= (blk * MOE_TM)[:, None]).astype(I32), axis=1), N_EXPERTS - 1)
    n_used = n_used.reshape(1)

    xs = dispatch(hn_p, hn_s, dest0, dest1, pad_end, counts, n_used)
    ys = moe(xs, blk_e, n_used, p['w_exp_gate'].astype(BF16), p['w_exp_up'].astype(BF16), p['w_exp_down'].astype(BF16))
    y_p = combine(ys, h_p, rw_p, dest0, dest1, tok0=0).reshape(bp, tp, D_MODEL)
    y_s = combine(ys, h_s, rw_s, dest0, dest1, tok0=n_p).reshape(bs, ts, D_MODEL)

    def heads(a, b, t):
        return a.reshape(b, t, N_KV_HEADS, HEAD_DIM)

    new_p = (heads(kf_p, bp, tp), heads(vf_p, bp, tp), kif_p.reshape(bp, tp, IDX_DIM), sre_p, sim_p)
    new_s = (heads(kf_s, bs, ts), heads(vf_s, bs, ts), kif_s.reshape(bs, ts, IDX_DIM), sre_s, sim_s)
    return y_p, y_s, new_p, new_s


def kernel(x_prompt, x_sample, cache_k, cache_v, cache_idx_k, state_ssm_re, state_ssm_im, norm_mix_g, w_in, q_norm_g, k_norm_g, idx_k_norm_g, ssm_A_re, ssm_A_im, ssm_log_dt, ssm_B_re, ssm_B_im, ssm_C_re, ssm_C_im, ssm_D, w_glu_val, w_glu_gate, w_attn_branch, w_out, norm_ffn_g, w_router_group, b_router_group, w_router_expert, b_router_expert, w_exp_gate, w_exp_up, w_exp_down):
    depth = w_in.shape[0]
    assert depth == 1, "prompt and sample tokens are batched through one layer"
    names = ('norm_mix_g', 'w_in', 'q_norm_g', 'k_norm_g', 'idx_k_norm_g', 'ssm_A_re', 'ssm_A_im', 'ssm_log_dt',
             'ssm_B_re', 'ssm_B_im', 'ssm_C_re', 'ssm_C_im', 'ssm_D', 'w_glu_val', 'w_glu_gate', 'w_attn_branch',
             'w_out', 'norm_ffn_g', 'w_router_group', 'b_router_group', 'w_router_expert', 'b_router_expert',
             'w_exp_gate', 'w_exp_up', 'w_exp_down')
    vals = (norm_mix_g, w_in, q_norm_g, k_norm_g, idx_k_norm_g, ssm_A_re, ssm_A_im, ssm_log_dt, ssm_B_re, ssm_B_im,
            ssm_C_re, ssm_C_im, ssm_D, w_glu_val, w_glu_gate, w_attn_branch, w_out, norm_ffn_g, w_router_group,
            b_router_group, w_router_expert, b_router_expert, w_exp_gate, w_exp_up, w_exp_down)
    p = {n: v[0] for n, v in zip(names, vals)}
    y_p, y_s, new_p, new_s = _layer(x_prompt, x_sample, cache_k[0], cache_v[0], cache_idx_k[0],
                                    state_ssm_re[0], state_ssm_im[0], p)
    st_p = tuple(a[None] for a in new_p)
    st_s = tuple(a[None] for a in new_s)
    return (y_p, y_s) + st_p + st_s
```

```python
import functools

import numpy as np
import jax
import jax.numpy as jnp
from jax import lax
from jax.experimental import pallas as pl
from jax.experimental.pallas import tpu as pltpu

F32 = jnp.float32
BF16 = jnp.bfloat16
I32 = jnp.int32

D_MODEL = 2048
CHUNK = 64
SSM_WIDTH = 1024
SSM_GROUP = 16
SSM_GROUPS = 64
SSM_STATE = 64
ATTN_WIDTH = 1024
HEAD_DIM = 128
N_HEADS = 8
N_KV_HEADS = 2
KV_GROUP = 4
IDX_HEADS = 8
IDX_DIM = 64
IDX_TOPK = 256
ROPE_THETA = 500000.0
N_EXPERT_GROUPS = 4
EXPERTS_PER_GROUP = 8
N_EXPERTS = 32
TOP_K = 2
EXPERT_FF = 1024
EPS = 1e-6

LANES = 128
SUBLANES = 8
VMEM_LIMIT = 56 * 1024 * 1024

COL_U, COL_Q, COL_GA, COL_GB, COL_K, COL_V, COL_QI, COL_KIWI = 0, 1024, 2048, 4096, 6144, 6400, 6656, 7168
PROJ_COLS = 7296
PROJ_TN = 2432
KV_WIDTH = N_KV_HEADS * HEAD_DIM

SSM_LB = SSM_WIDTH // LANES
SSM_SB = 8 * SSM_STATE

INT_MIN = np.int32(-2 ** 31)
KEY_NEG_INF = np.int32(np.array([0xFF800000], np.uint32).view(np.int32)[0] ^ 0x7FFFFFFF)


def _params(sem, vmem=VMEM_LIMIT):
    return pltpu.CompilerParams(dimension_semantics=sem, vmem_limit_bytes=vmem)


def _dot(a, b):
    return jnp.dot(a, b, preferred_element_type=F32)


def _dot_nt(a, b):
    return lax.dot_general(a, b, (((1,), (1,)), ((), ())), preferred_element_type=F32)


def _split_bf16(x):
    hi = x.astype(BF16)
    lo = (x - hi.astype(F32)).astype(BF16)
    return hi, lo


def _in_proj_kernel(x_ref, g_ref, w_ref, o_ref, xn_ref):
    @pl.when(pl.program_id(1) == 0)
    def _():
        x = x_ref[...]
        ms = jnp.mean(x * x, axis=-1, keepdims=True)
        xn_ref[...] = (x * lax.rsqrt(ms + EPS) * g_ref[...]).astype(BF16)

    o_ref[...] = _dot(xn_ref[...], w_ref[...])


def in_proj(x, gain, w_bf16, *, tm=512):
    n_tok = x.shape[0]
    return pl.pallas_call(
        _in_proj_kernel,
        grid=(n_tok // tm, PROJ_COLS // PROJ_TN),
        in_specs=[pl.BlockSpec((tm, D_MODEL), lambda i, j: (i, 0)),
                  pl.BlockSpec((1, D_MODEL), lambda i, j: (0, 0)),
                  pl.BlockSpec((D_MODEL, PROJ_TN), lambda i, j: (0, j))],
        out_specs=pl.BlockSpec((tm, PROJ_TN), lambda i, j: (i, j)),
        out_shape=jax.ShapeDtypeStruct((n_tok, PROJ_COLS), F32),
        scratch_shapes=[pltpu.VMEM((tm, D_MODEL), BF16)],
        compiler_params=_params(("arbitrary", "arbitrary")),
        name="in_proj",
    )(x, gain, w_bf16)


def _rope(x, c, s_lo, s_hi, half):
    n = x.shape[-1]
    return x * c + pltpu.roll(x, n - half, 1) * s_lo + pltpu.roll(x, half, 1) * s_hi


def _head_norm(x, g):
    ms = jnp.mean(x * x, axis=-1, keepdims=True)
    return x * lax.rsqrt(ms + EPS) * g


V_AUG = 2 * HEAD_DIM


def _store_v_aug(dst_ref, row0, v_heads):
    n = v_heads[0].shape[0]
    one_col = jnp.where(lax.broadcasted_iota(I32, (n, HEAD_DIM), 1) == 0, 1.0, 0.0).astype(BF16)
    for h, v in enumerate(v_heads):
        dst_ref[row0:row0 + n, h * V_AUG:h * V_AUG + HEAD_DIM] = v.astype(BF16)
        dst_ref[row0:row0 + n, h * V_AUG + HEAD_DIM:(h + 1) * V_AUG] = one_col


def _qk_post_kernel(q_ref, k_ref, v_ref, qi_ref, kw_ref, c128_ref, sl128_ref, sh128_ref,
                    c64_ref, sl64_ref, sh64_ref, qg_ref, kg_ref, ig_ref,
                    qo_ref, kf_ref, kb_ref, vf_ref, vb_ref, qio_ref, kif_ref, kib_ref, wo_ref):
    c128, sl128, sh128 = c128_ref[...], sl128_ref[...], sh128_ref[...]
    c64, sl64, sh64 = c64_ref[...], sl64_ref[...], sh64_ref[...]
    half128 = HEAD_DIM // 8
    half64 = IDX_DIM // 8
    for h in range(N_HEADS):
        sl = slice(h * LANES, (h + 1) * LANES)
        qo_ref[:, sl] = _rope(_head_norm(q_ref[:, sl], qg_ref[...]), c128, sl128, sh128, half128).astype(BF16)
    for h in range(N_KV_HEADS):
        sl = slice(h * LANES, (h + 1) * LANES)
        kk = _rope(_head_norm(k_ref[:, sl], kg_ref[...]), c128, sl128, sh128, half128)
        kf_ref[:, sl] = kk
        kb_ref[:, sl] = kk.astype(BF16)
    v = v_ref[...]
    vf_ref[...] = v
    _store_v_aug(vb_ref, 0, [v[:, h * HEAD_DIM:(h + 1) * HEAD_DIM] for h in range(N_KV_HEADS)])
    lane = lax.broadcasted_iota(I32, c64.shape, 1)
    low = lane < IDX_DIM
    for p in range(IDX_HEADS // 2):
        x = _rope(qi_ref[:, p * LANES:(p + 1) * LANES], c64, sl64, sh64, half64)
        qio_ref[:, (2 * p) * LANES:(2 * p + 1) * LANES] = jnp.where(low, x, 0.0).astype(BF16)
        qio_ref[:, (2 * p + 1) * LANES:(2 * p + 2) * LANES] = jnp.where(low, pltpu.roll(x, IDX_DIM, 1), 0.0).astype(BF16)
    kw = kw_ref[...]
    ms = jnp.sum(jnp.where(low, kw * kw, 0.0), axis=-1, keepdims=True) * (1.0 / IDX_DIM)
    ki = _rope(kw * lax.rsqrt(ms + EPS) * ig_ref[...], c64, sl64, sh64, half64)
    kif_ref[...] = ki[:, :IDX_DIM]
    kib_ref[...] = jnp.where(low, ki, 0.0).astype(BF16)
    wo_ref[...] = (pltpu.roll(kw, IDX_DIM, 1) * IDX_HEADS ** -0.5) * IDX_DIM ** -0.5


def _rope_tables(pos, head_dim):
    r = head_dim // 4
    half = r // 2
    inv = ROPE_THETA ** (-jnp.arange(half, dtype=F32) * 2.0 / r)
    ang = pos.astype(F32)[:, None] * inv[None, :]
    cos, sin = jnp.cos(ang), jnp.sin(ang)
    n = pos.shape[0]
    zh = jnp.zeros((n, half), F32)
    rest = head_dim - r
    c = jnp.concatenate([cos, cos, jnp.ones((n, rest), F32)], axis=-1)
    s_lo = jnp.concatenate([-sin, zh, jnp.zeros((n, rest), F32)], axis=-1)
    s_hi = jnp.concatenate([zh, sin, jnp.zeros((n, rest), F32)], axis=-1)
    rep = LANES // head_dim
    return tuple(jnp.tile(t, (1, rep)) for t in (c, s_lo, s_hi))


QK_TM = 512


def qk_post(proj, table_pos, table_block, q_gain, k_gain, ik_gain):
    tm = QK_TM
    n_tok = proj.shape[0]
    t128 = _rope_tables(table_pos, HEAD_DIM)
    t64 = _rope_tables(table_pos, IDX_DIM)
    ik_gain128 = jnp.concatenate([ik_gain, jnp.zeros((LANES - IDX_DIM,), F32)])[None, :]

    def col(width, start):
        return pl.BlockSpec((tm, width), lambda i: (i, start // width))

    def row(width):
        return pl.BlockSpec((tm, width), lambda i: (i, 0))

    table = pl.BlockSpec((tm, LANES), lambda i: (table_block(i), 0))
    gain = pl.BlockSpec((1, LANES), lambda i: (0, 0))
    return pl.pallas_call(
        _qk_post_kernel,
        grid=(n_tok // tm,),
        in_specs=[col(ATTN_WIDTH, COL_Q), col(KV_WIDTH, COL_K), col(KV_WIDTH, COL_V), col(IDX_HEADS * IDX_DIM, COL_QI),
                  col(LANES, COL_KIWI)] + [table] * 6 + [gain] * 3,
        out_specs=[row(ATTN_WIDTH), row(KV_WIDTH), row(KV_WIDTH), row(KV_WIDTH), row(N_KV_HEADS * V_AUG), row(IDX_HEADS * LANES),
                   row(IDX_DIM), row(LANES), row(LANES)],
        out_shape=[jax.ShapeDtypeStruct((n_tok, ATTN_WIDTH), BF16),
                   jax.ShapeDtypeStruct((n_tok, KV_WIDTH), F32), jax.ShapeDtypeStruct((n_tok, KV_WIDTH), BF16),
                   jax.ShapeDtypeStruct((n_tok, KV_WIDTH), F32), jax.ShapeDtypeStruct((n_tok, N_KV_HEADS * V_AUG), BF16),
                   jax.ShapeDtypeStruct((n_tok, IDX_HEADS * LANES), BF16),
                   jax.ShapeDtypeStruct((n_tok, IDX_DIM), F32), jax.ShapeDtypeStruct((n_tok, LANES), BF16),
                   jax.ShapeDtypeStruct((n_tok, LANES), F32)],
        compiler_params=_params(("arbitrary",)),
        name="qk_post",
    )(proj, proj, proj, proj, proj, *t128, *t64, q_gain[None, :], k_gain[None, :], ik_gain128)


def _gelu_tanh(x):
    return 0.5 * x * (1.0 + jnp.tanh(np.float32(np.sqrt(2.0 / np.pi)) * (x + 0.044715 * (x * x * x))))


SSM_LT = SSM_SB // LANES
SSM_SEG = 64


def _ssm_kernel(u_ref, wb_ref, wc_ref, pw_ref, d_ref, h0_ref, g_ref, sre_ref, sim_ref,
                er_ref, ei_ref, car_ref, up_ref, yp_ref):
    c = pl.program_id(2)

    @pl.when(c == 0)
    def _():
        car_ref[...] = h0_ref[...]

    for j in range(SSM_SEG):
        up_ref[j * SUBLANES:(j + 1) * SUBLANES, :] = u_ref[pl.ds(j, SUBLANES, stride=SSM_SEG), :]
    e = _dot(up_ref[...].astype(BF16), wb_ref[...])
    tiles = [slice(lt * LANES, (lt + 1) * LANES) for lt in range(SSM_LT)]
    for lt, sl in enumerate(tiles):
        er_ref[lt] = e[:, sl]
        ei_ref[lt] = e[:, SSM_SB + lt * LANES:SSM_SB + (lt + 1) * LANES]

    def cmul_add(ar, ai, br, bi, cr, ci):
        return ar * br - ai * bi + cr, ar * bi + ai * br + ci

    lb = [(pw_ref[0, 0:1, sl], pw_ref[1, 0:1, sl]) for sl in tiles]
    zero = jnp.zeros((SUBLANES, LANES), F32)
    st = [(zero, zero)] * SSM_LT
    for j in range(SSM_SEG):
        rows = slice(j * SUBLANES, (j + 1) * SUBLANES)
        for lt in range(SSM_LT):
            st[lt] = cmul_add(*lb[lt], *st[lt], er_ref[lt, rows, :], ei_ref[lt, rows, :])
            er_ref[lt, rows, :] = st[lt][0]
            ei_ref[lt, rows, :] = st[lt][1]

    enter = []
    for lt, sl in enumerate(tiles):
        seg_r, seg_i = pw_ref[0, SSM_SEG - 1:SSM_SEG, sl], pw_ref[1, SSM_SEG - 1:SSM_SEG, sl]
        cr, ci = car_ref[0:1, sl], car_ref[1:2, sl]
        rows_r, rows_i = [], []
        for r in range(SUBLANES):
            rows_r.append(cr)
            rows_i.append(ci)
            cr, ci = cmul_add(seg_r, seg_i, cr, ci, st[lt][0][r:r + 1], st[lt][1][r:r + 1])
        car_ref[0:1, sl] = cr
        car_ref[1:2, sl] = ci
        enter.append((jnp.concatenate(rows_r, axis=0), jnp.concatenate(rows_i, axis=0)))

    for j in range(SSM_SEG):
        rows = slice(j * SUBLANES, (j + 1) * SUBLANES)
        for lt, sl in enumerate(tiles):
            xr, xi = cmul_add(pw_ref[0, j:j + 1, sl], pw_ref[1, j:j + 1, sl], *enter[lt],
                              er_ref[lt, rows, :], ei_ref[lt, rows, :])
            er_ref[lt, rows, :] = xr
            ei_ref[lt, rows, :] = xi

    y = None
    for lt, sl in enumerate(tiles):
        t = _dot(er_ref[lt].astype(BF16), wc_ref[0, sl, :]) - _dot(ei_ref[lt].astype(BF16), wc_ref[1, sl, :])
        y = t if y is None else y + t
    yp_ref[...] = y
    out_rows = 2 * SUBLANES
    for t0 in range(0, SUBLANES * SSM_SEG, out_rows):
        r, j0 = divmod(t0, SSM_SEG)
        rows = slice(t0, t0 + out_rows)
        yt = yp_ref[pl.ds(j0 * SUBLANES + r, out_rows, stride=SUBLANES), :] + d_ref[...] * u_ref[rows, :]
        g_ref[rows, :] = _gelu_tanh(yt).astype(BF16)

    @pl.when(c == pl.num_programs(2) - 1)
    def _():
        sre_ref[...] = car_ref[0:1, :]
        sim_ref[...] = car_ref[1:2, :]


def _ssm_weights(a_re, a_im, log_dt, b_re, b_im, c_re, c_im):
    lam_re, lam_im = a_re, a_im
    dt = jnp.exp(log_dt)[:, None]
    mag = jnp.exp(lam_re * dt)
    lb_re, lb_im = mag * jnp.cos(lam_im * dt), mag * jnp.sin(lam_im * dt)
    den = lam_re * lam_re + lam_im * lam_im
    num_re = lb_re - 1.0
    z_re = (num_re * lam_re + lb_im * lam_im) / den
    z_im = (lb_im * lam_re - num_re * lam_im) / den
    zb_re = z_re[:, :, None] * b_re - z_im[:, :, None] * b_im
    zb_im = z_re[:, :, None] * b_im + z_im[:, :, None] * b_re
    eye = jnp.eye(8, dtype=F32)

    def blockdiag_in(w):
        return jnp.einsum('jgph,gk->jghkp', w.reshape(SSM_LB, 8, SSM_STATE, SSM_GROUP), eye).reshape(SSM_LB, LANES, SSM_SB)

    def blockdiag_out(w):
        return jnp.einsum('jghp,gk->jkpgh', w.reshape(SSM_LB, 8, SSM_GROUP, SSM_STATE), eye).reshape(SSM_LB, SSM_SB, LANES)

    wb = jnp.concatenate([blockdiag_in(zb_re), blockdiag_in(zb_im)], axis=-1).astype(BF16)
    wc = jnp.stack([blockdiag_out(c_re), blockdiag_out(c_im)], axis=1).astype(BF16)

    pr, pi_ = lb_re.reshape(SSM_LB, 1, SSM_SB), lb_im.reshape(SSM_LB, 1, SSM_SB)
    while pr.shape[1] < SSM_SEG:
        tr, ti = pr[:, -1:], pi_[:, -1:]
        pr, pi_ = (jnp.concatenate([pr, pr * tr - pi_ * ti], axis=1), jnp.concatenate([pi_, pr * ti + pi_ * tr], axis=1))
    pw = jnp.stack([pr, pi_], axis=1)
    return wb, wc, pw


def ssm(proj, ssm_w, d_skip, h0, *, n_batch, seq, row0):
    wb, wc, pw = ssm_w
    tc = SUBLANES * SSM_SEG
    n_chunks = seq // tc
    blk0 = row0 // tc
    n_tok = n_batch * seq
    state_shape = jax.ShapeDtypeStruct((n_batch, SSM_LB, 1, SSM_SB), F32)
    state_spec = pl.BlockSpec((None, None, 1, SSM_SB), lambda b, j, c: (b, j, 0, 0))
    g, s_re, s_im = pl.pallas_call(
        _ssm_kernel,
        grid=(n_batch, SSM_LB, n_chunks),
        in_specs=[pl.BlockSpec((tc, LANES), lambda b, j, c: (blk0 + b * n_chunks + c, j)),
                  pl.BlockSpec((None, LANES, 2 * SSM_SB), lambda b, j, c: (j, 0, 0)),
                  pl.BlockSpec((None, 2, SSM_SB, LANES), lambda b, j, c: (j, 0, 0, 0)),
                  pl.BlockSpec((None, 2, SSM_SEG, SSM_SB), lambda b, j, c: (j, 0, 0, 0)),
                  pl.BlockSpec((1, LANES), lambda b, j, c: (0, j)),
                  pl.BlockSpec((None, None, 2, SSM_SB), lambda b, j, c: (b, j, 0, 0))],
        out_specs=[pl.BlockSpec((tc, LANES), lambda b, j, c: (b * n_chunks + c, j)), state_spec, state_spec],
        out_shape=[jax.ShapeDtypeStruct((n_tok, SSM_WIDTH), BF16), state_shape, state_shape],
        scratch_shapes=[pltpu.VMEM((SSM_LT, tc, LANES), F32), pltpu.VMEM((SSM_LT, tc, LANES), F32),
                        pltpu.VMEM((2, SSM_SB), F32), pltpu.VMEM((tc, LANES), F32), pltpu.VMEM((tc, LANES), F32)],
        compiler_params=_params(("arbitrary", "arbitrary", "arbitrary")),
        name="ssm",
    )(proj, wb, wc, pw, d_skip[None, :], h0)
    return g, s_re.reshape(n_batch, SSM_GROUPS, SSM_STATE), s_im.reshape(n_batch, SSM_GROUPS, SSM_STATE)


def _ssm_step_kernel(u_ref, wb_ref, wc_ref, pw_ref, d_ref, h0_ref, g_ref, sre_ref, sim_ref, er_ref, ei_ref, *, seq):
    n_seq = h0_ref.shape[1]
    u = u_ref[...]
    e = _dot(u.astype(BF16), wb_ref[...])
    n_lt = SSM_SB // LANES
    y = d_ref[...] * u
    for lt in range(n_lt):
        sl = slice(lt * LANES, (lt + 1) * LANES)
        er_ref[...] = e[:, lt * LANES:(lt + 1) * LANES]
        ei_ref[...] = e[:, SSM_SB + lt * LANES:SSM_SB + (lt + 1) * LANES]
        lr, li = pw_ref[0, 0:1, sl], pw_ref[1, 0:1, sl]
        sr, si = h0_ref[0, :, sl], h0_ref[1, :, sl]
        for t in range(seq):
            rows = pl.ds(t, n_seq, stride=seq)
            sr, si = lr * sr - li * si + er_ref[rows, :], lr * si + li * sr + ei_ref[rows, :]
            er_ref[rows, :] = sr
            ei_ref[rows, :] = si
        y = y + (_dot(er_ref[...].astype(BF16), wc_ref[0, sl, :]) - _dot(ei_ref[...].astype(BF16), wc_ref[1, sl, :]))
        sre_ref[:, sl] = sr
        sim_ref[:, sl] = si
    g_ref[...] = _gelu_tanh(y).astype(BF16)


def ssm_step(proj, ssm_w, d_skip, h0, *, n_batch, seq, row0):
    wb, wc, pw = ssm_w
    n_tok = n_batch * seq
    assert row0 % n_tok == 0
    state_shape = jax.ShapeDtypeStruct((SSM_LB, n_batch, SSM_SB), F32)
    state_spec = pl.BlockSpec((None, n_batch, SSM_SB), lambda j: (j, 0, 0))
    g, s_re, s_im = pl.pallas_call(
        functools.partial(_ssm_step_kernel, seq=seq),
        grid=(SSM_LB,),
        in_specs=[pl.BlockSpec((n_tok, LANES), lambda j: (row0 // n_tok, j)),
                  pl.BlockSpec((None, LANES, 2 * SSM_SB), lambda j: (j, 0, 0)),
                  pl.BlockSpec((None, 2, SSM_SB, LANES), lambda j: (j, 0, 0, 0)),
                  pl.BlockSpec((None, 2, SSM_SEG, SSM_SB), lambda j: (j, 0, 0, 0)),
                  pl.BlockSpec((1, LANES), lambda j: (0, j)),
                  pl.BlockSpec((None, 2, n_batch, SSM_SB), lambda j: (j, 0, 0, 0))],
        out_specs=[pl.BlockSpec((n_tok, LANES), lambda j: (0, j)), state_spec, state_spec],
        out_shape=[jax.ShapeDtypeStruct((n_tok, SSM_WIDTH), BF16), state_shape, state_shape],
        scratch_shapes=[pltpu.VMEM((n_tok, LANES), F32), pltpu.VMEM((n_tok, LANES), F32)],
        compiler_params=_params(("arbitrary",)),
        name="ssm_step",
    )(proj, wb, wc, pw, d_skip[None, :], h0)

    def per_seq(s):
        return s.transpose(1, 0, 2).reshape(n_batch, SSM_GROUPS, SSM_STATE)

    return g, per_seq(s_re), per_seq(s_im)


def _row_sum(x):
    return jnp.sum(x, axis=1, keepdims=True)


def _row_count(mask):
    return _row_sum(jnp.where(mask, 1, 0))


I16 = jnp.int16
I16_MIN = -2 ** 15


def _count16(ref, cand, compare):
    accs = [None] * 4
    for t in range(ref.shape[1] // LANES):
        x = jnp.where(compare(ref[:, t * LANES:(t + 1) * LANES], cand), I16(1), I16(0))
        accs[t % 4] = x if accs[t % 4] is None else accs[t % 4] + x
    accs = [a for a in accs if a is not None]
    total = accs[0]
    for a in accs[1:]:
        total = total + a
    return _row_sum(total.astype(I32))


def _bisect16(ref, target):
    def step(i, base):
        cand = base + lax.shift_left(np.int32(1), np.int32(15) - i)
        cnt = _count16(ref, cand.astype(I16), lambda a, b: a >= b)
        return jnp.where(cnt >= target, cand, base)
    return lax.fori_loop(0, 16, step, jnp.full((ref.shape[0], 1), I16_MIN, I32))


def _stack_heads(ref, heads):
    return jnp.concatenate([ref[:, h * LANES:(h + 1) * LANES] for h in heads], axis=0)


def _dsa_body(q_ref, qi_ref, wi_ref, k_ref, v_ref, ki_ref, o_ref, key_ref, bias_ref, hi_ref, lo_ref,
              *, q_pos_first, s_valid, n_sel, packed_bisect, stack):
    bq, n_keys = key_ref.shape
    col = lax.broadcasted_iota(I32, (bq, n_keys), 1)
    qpos = q_pos_first + lax.broadcasted_iota(I32, (bq, 1), 0)
    allowed = col < jnp.minimum((qpos // CHUNK + 1) * CHUNK, s_valid)

    ki = ki_ref[...]
    score = None
    for h0 in range(0, IDX_HEADS, stack):
        d = _dot_nt(_stack_heads(qi_ref, range(h0, h0 + stack)), ki)
        for j in range(stack):
            t = jnp.maximum(d[j * bq:(j + 1) * bq], 0.0) * wi_ref[:, h0 + j:h0 + j + 1]
            score = t if score is None else score + t
    score = jnp.where(score == 0.0, 0.0, score)
    bits = pltpu.bitcast(score, I32)
    key = jnp.where(bits < 0, bits ^ np.int32(0x7FFFFFFF), bits)
    key = jnp.where(allowed, key, KEY_NEG_INF)
    key_ref[...] = key

    if packed_bisect:
        hi_ref[...] = (key >> 16).astype(I16)
        lo_ref[...] = ((key & 0xFFFF) + I16_MIN).astype(I16)
        thr_hi = _bisect16(hi_ref, n_sel)
        thr_hi16 = thr_hi.astype(I16)
        need_lo = n_sel - _count16(hi_ref, thr_hi16, lambda a, b: a > b)
        lo_ref[...] = jnp.where(hi_ref[...] == thr_hi16, lo_ref[...], I16(I16_MIN))
        thr_lo = _bisect16(lo_ref, need_lo)
        thr = lax.shift_left(thr_hi, np.int32(16)) + (thr_lo - I16_MIN)
    else:
        def bisect(i, base):
            cand = base + lax.shift_left(np.int32(1), np.int32(31) - i)
            cnt = _row_count(key_ref[...] >= cand)
            return jnp.where(cnt >= n_sel, cand, base)
        thr = lax.fori_loop(0, 32, bisect, jnp.full((bq, 1), INT_MIN, I32))
    thr = jnp.maximum(thr, KEY_NEG_INF)

    key = key_ref[...]
    need = n_sel - _row_count(key > thr)
    n_eq = _row_count(key == thr)
    n_bits = int(n_keys - 1).bit_length()

    def tie_cut():
        def step(i, j0):
            cand = j0 + lax.shift_left(np.int32(1), np.int32(n_bits - 1) - i)
            cnt = _row_sum(jnp.where(key_ref[...] == thr, jnp.where(col < cand, 1, 0), 0))
            return jnp.where(cnt < need, cand, j0)
        return lax.fori_loop(0, n_bits, step, jnp.zeros((bq, 1), I32))

    split = jnp.max(jnp.where(n_eq > need, 1, 0)) > 0
    j_last = lax.cond(split, tie_cut, lambda: jnp.full((bq, 1), n_keys, I32))
    tie_bias = jnp.where(thr == KEY_NEG_INF, -jnp.inf, 0.0)
    bias_ref[...] = jnp.where(key > thr, 0.0,
                              jnp.where(key == thr, jnp.where(col <= j_last, tie_bias, -jnp.inf), -jnp.inf))

    c = np.float32(HEAD_DIM ** -0.5 * np.log2(np.e))
    for h0 in range(0, N_HEADS, stack):
        kv = h0 // KV_GROUP
        heads = range(h0, h0 + stack)
        s_all = _dot_nt(_stack_heads(q_ref, heads), k_ref[:, kv * HEAD_DIM:(kv + 1) * HEAD_DIM])
        ps = []
        for g in range(stack):
            s = s_all[g * bq:(g + 1) * bq] + bias_ref[...]
            m = jnp.max(s, axis=1, keepdims=True)
            ps.append(jnp.exp2((s - m) * c).astype(BF16))
        pv = _dot(jnp.concatenate(ps, axis=0), v_ref[:, kv * V_AUG:(kv + 1) * V_AUG])
        for g, h in enumerate(heads):
            o = pv[g * bq:(g + 1) * bq]
            o_ref[:, h * HEAD_DIM:(h + 1) * HEAD_DIM] = (o[:, :HEAD_DIM] / o[:, HEAD_DIM:HEAD_DIM + 1]).astype(BF16)


def _dsa_scratch(bq, n_keys):
    return [pltpu.VMEM((bq, n_keys), I32), pltpu.VMEM((bq, n_keys), F32),
            pltpu.VMEM((bq, n_keys), I16), pltpu.VMEM((bq, n_keys), I16)]


def _dsa_kernel(q_ref, qi_ref, wi_ref, k_ref, v_ref, ki_ref, o_ref, *scratch, q_pos0, **static):
    bq = scratch[0].shape[0]
    _dsa_body(q_ref, qi_ref, wi_ref, k_ref, v_ref, ki_ref, o_ref, *scratch,
              q_pos_first=q_pos0 + pl.program_id(1) * bq, **static)


def dsa(q, qi, wi, k, v, ki, *, bq, q_blk0, n_qblk, n_keys, n_sel, packed_bisect, stack):
    n_batch, seq = q.shape[:2]

    def qspec(width):
        return pl.BlockSpec((None, bq, width), lambda b, i: (b, q_blk0 + i, 0))

    def kspec(width):
        return pl.BlockSpec((None, n_keys, width), lambda b, i: (b, 0, 0))

    return pl.pallas_call(
        functools.partial(_dsa_kernel, q_pos0=q_blk0 * bq, s_valid=seq, n_sel=n_sel, packed_bisect=packed_bisect,
                          stack=stack),
        grid=(n_batch, n_qblk),
        in_specs=[qspec(ATTN_WIDTH), qspec(IDX_HEADS * LANES), qspec(LANES), kspec(KV_WIDTH), kspec(N_KV_HEADS * V_AUG),
                  kspec(LANES)],
        out_specs=pl.BlockSpec((None, bq, ATTN_WIDTH), lambda b, i: (b, i, 0)),
        out_shape=jax.ShapeDtypeStruct((n_batch, n_qblk * bq, ATTN_WIDTH), BF16),
        scratch_shapes=_dsa_scratch(bq, n_keys),
        compiler_params=_params(("arbitrary", "arbitrary")),
        name="dsa",
    )(q, qi, wi, k, v, ki)


def _dsa_step_kernel(q_ref, qi_ref, wi_ref, ck_hbm, cv_hbm, cki_ref, nk_ref, nv_ref, nki_ref, o_ref,
                     k_buf, v_buf, ki_buf, cache_buf, sem, *scratch, past, n_sel):
    b = pl.program_id(0)

    def cache_copies(seq, slot):
        return [pltpu.make_async_copy(src.at[seq, :, h, :], cache_buf.at[slot, a, h], sem.at[slot])
                for a, src in enumerate((ck_hbm, cv_hbm)) for h in range(N_KV_HEADS)]

    @pl.when(b == 0)
    def _():
        for cp in cache_copies(0, 0):
            cp.start()

    @pl.when(b + 1 < pl.num_programs(0))
    def _():
        for cp in cache_copies(b + 1, (b + 1) % 2):
            cp.start()

    slot = b % 2
    for cp in cache_copies(b, slot):
        cp.wait()

    ts = nk_ref.shape[0]
    n_keys = k_buf.shape[0]
    for h in range(N_KV_HEADS):
        k_buf[0:past, h * HEAD_DIM:(h + 1) * HEAD_DIM] = cache_buf[slot, 0, h].astype(BF16)
    _store_v_aug(v_buf, 0, [cache_buf[slot, 1, h] for h in range(N_KV_HEADS)])
    for buf, new in ((k_buf, nk_ref), (v_buf, nv_ref)):
        buf[past:past + ts, :] = new[...]
        buf[past + ts:n_keys, :] = jnp.zeros((n_keys - past - ts, buf.shape[1]), BF16)
    ki_buf[0:past, 0:IDX_DIM] = cki_ref[...].astype(BF16)
    ki_buf[0:past, IDX_DIM:LANES] = jnp.zeros((past, LANES - IDX_DIM), BF16)
    ki_buf[past:past + ts, :] = nki_ref[...]
    ki_buf[past + ts:n_keys, :] = jnp.zeros((n_keys - past - ts, LANES), BF16)
    _dsa_body(q_ref, qi_ref, wi_ref, k_buf, v_buf, ki_buf, o_ref, *scratch,
              q_pos_first=past, s_valid=past + ts, n_sel=n_sel, packed_bisect=True, stack=KV_GROUP)


def dsa_step(q, qi, wi, cache_k, cache_v, cache_ki, k_new, v_new, ki_new, *, n_sel):
    n_batch, ts = q.shape[:2]
    past = cache_k.shape[1]
    n_keys = -(-(past + ts) // LANES) * LANES

    def spec(rows, width):
        return pl.BlockSpec((None, rows, width), lambda b: (b, 0, 0))

    return pl.pallas_call(
        functools.partial(_dsa_step_kernel, past=past, n_sel=n_sel),
        grid=(n_batch,),
        in_specs=[spec(ts, ATTN_WIDTH), spec(ts, IDX_HEADS * LANES), spec(ts, LANES),
                  pl.BlockSpec(memory_space=pl.ANY), pl.BlockSpec(memory_space=pl.ANY), spec(past, IDX_DIM),
                  spec(ts, KV_WIDTH), spec(ts, N_KV_HEADS * V_AUG), spec(ts, LANES)],
        out_specs=spec(ts, ATTN_WIDTH),
        out_shape=jax.ShapeDtypeStruct((n_batch, ts, ATTN_WIDTH), BF16),
        scratch_shapes=[pltpu.VMEM((n_keys, KV_WIDTH), BF16), pltpu.VMEM((n_keys, N_KV_HEADS * V_AUG), BF16),
                        pltpu.VMEM((n_keys, LANES), BF16),
                        pltpu.VMEM((2, 2, N_KV_HEADS, past, HEAD_DIM), F32), pltpu.SemaphoreType.DMA((2,)),
                        *_dsa_scratch(ts, n_keys)],
        compiler_params=_params(("arbitrary",)),
        name="dsa_step",
    )(q, qi, wi, cache_k, cache_v, cache_ki, k_new, v_new, ki_new)


def _merge_kernel(g_ref, a_ref, ga_ref, gb_ref, wv_ref, wg_ref, wb_ref, o_ref):
    g = g_ref[...]
    branch_a = _dot(g, wv_ref[...]) * jax.nn.sigmoid(_dot(g, wg_ref[...]))
    branch_b = _dot(a_ref[...], wb_ref[...])
    merged = jax.nn.sigmoid(ga_ref[...]) * branch_a + jax.nn.sigmoid(gb_ref[...]) * branch_b
    o_ref[...] = merged.astype(BF16)


def merge(g, attn, proj, w_val, w_gate, w_branch, *, tm=1024, tn=512):
    n_tok = g.shape[0]
    nj = D_MODEL // tn

    def wspec():
        return pl.BlockSpec((SSM_WIDTH, tn), lambda i, j: (0, j))

    return pl.pallas_call(
        _merge_kernel,
        grid=(n_tok // tm, nj),
        in_specs=[pl.BlockSpec((tm, SSM_WIDTH), lambda i, j: (i, 0)),
                  pl.BlockSpec((tm, ATTN_WIDTH), lambda i, j: (i, 0)),
                  pl.BlockSpec((tm, tn), lambda i, j: (i, COL_GA // tn + j)),
                  pl.BlockSpec((tm, tn), lambda i, j: (i, COL_GB // tn + j)),
                  wspec(), wspec(), wspec()],
        out_specs=pl.BlockSpec((tm, tn), lambda i, j: (i, j)),
        out_shape=jax.ShapeDtypeStruct((n_tok, D_MODEL), BF16),
        compiler_params=_params(("arbitrary", "arbitrary")),
        name="merge",
    )(g, attn, proj, proj, w_val, w_gate, w_branch)


ROUTER_COLS = N_EXPERT_GROUPS + N_EXPERTS
MOE_TM = 256


def _first_lane_of_max(x, lane_f):
    m = jnp.max(x, axis=1, keepdims=True)
    return m, jnp.min(jnp.where(x == m, lane_f, float(LANES)), axis=1, keepdims=True)


def _out_proj_kernel(x_ref, m_ref, wo_ref, gn_ref, wrh_ref, wrl_ref, br_ref, cin_ref,
                     h_ref, hn_ref, ri_ref, rw_ref, cnt_ref, carry_ref):
    @pl.when(pl.program_id(0) == 0)
    def _():
        carry_ref[...] = cin_ref[...]

    h = x_ref[...] + _dot(m_ref[...], wo_ref[...])
    h_ref[...] = h
    ms = jnp.mean(h * h, axis=-1, keepdims=True)
    hn = h * lax.rsqrt(ms + EPS) * gn_ref[...]
    hn_ref[...] = hn
    hh, hl = _split_bf16(hn)
    wrh = wrh_ref[...]
    lg = _dot(hh, wrh) + _dot(hl, wrh) + _dot(hh, wrl_ref[...]) + br_ref[...]

    tm = lg.shape[0]
    lane = lax.broadcasted_iota(I32, lg.shape, 1)
    lane_f = lane.astype(F32)
    ninf = -jnp.inf
    gl = jnp.where(lane < N_EXPERT_GROUPS, lg, ninf)
    gmax, gsel = _first_lane_of_max(gl, lane_f)
    g_w = 1.0 / jnp.sum(jnp.exp(gl - gmax), axis=1, keepdims=True)
    lo = N_EXPERT_GROUPS + EXPERTS_PER_GROUP * gsel
    el = jnp.where(lane_f >= lo, jnp.where(lane_f < lo + EXPERTS_PER_GROUP, lg, ninf), ninf)
    v1, i1 = _first_lane_of_max(el, lane_f)
    el2 = jnp.where(lane_f == i1, ninf, el)
    v2, i2 = _first_lane_of_max(el2, lane_f)
    t = jnp.exp(v2 - v1)
    s1 = 1.0 / (1.0 + t)
    w1 = s1 * g_w
    w2 = (t * s1) * g_w

    m1 = jnp.where(lane_f == i1, 1.0, 0.0)
    m2 = jnp.where(lane_f == i2, 1.0, 0.0)
    both = m1 + m2
    tri = jnp.where(lax.broadcasted_iota(I32, (tm, tm), 0) > lax.broadcasted_iota(I32, (tm, tm), 1), 1.0, 0.0)
    before = _dot(tri.astype(BF16), both.astype(BF16)) + carry_ref[...]
    r1 = jnp.sum(before * m1, axis=1, keepdims=True)
    r2 = jnp.sum(before * m2, axis=1, keepdims=True)
    carry_ref[...] = carry_ref[...] + jnp.sum(both, axis=0, keepdims=True)
    cnt_ref[...] = carry_ref[...]
    e1 = i1 - float(N_EXPERT_GROUPS)
    e2 = i2 - float(N_EXPERT_GROUPS)
    fields = jnp.where(lane == 0, e1, jnp.where(lane == 1, e2, jnp.where(lane == 2, r1, jnp.where(lane == 3, r2, 0.0))))
    ri_ref[...] = fields.T[0:SUBLANES, :].astype(I32)
    rw_ref[...] = jnp.where(lane == 0, w1, jnp.where(lane == 1, w2, 0.0))


def _router_weights(w_router_group, b_router_group, w_router_expert, b_router_expert):
    wr = jnp.concatenate([w_router_group, w_router_expert, jnp.zeros((D_MODEL, LANES - ROUTER_COLS), F32)], axis=1)
    wr_hi = wr.astype(BF16)
    wr_lo = (wr - wr_hi.astype(F32)).astype(BF16)
    br = jnp.concatenate([b_router_group, b_router_expert, jnp.zeros((LANES - ROUTER_COLS,), F32)])[None, :]
    return wr_hi, wr_lo, br


def out_proj(x, merged, w_out, ffn_gain, router_w, counts_in, *, tm=256):
    n_tok = x.shape[0]
    wr_hi, wr_lo, br = router_w

    def row(width):
        return pl.BlockSpec((tm, width), lambda i: (i, 0))

    def const(shape):
        return pl.BlockSpec(shape, lambda i: (0, 0), pipeline_mode=pl.Buffered(1))

    return pl.pallas_call(
        _out_proj_kernel,
        grid=(n_tok // tm,),
        in_specs=[row(D_MODEL), row(D_MODEL), const((D_MODEL, D_MODEL)), const((1, D_MODEL)),
                  const((D_MODEL, LANES)), const((D_MODEL, LANES)), const((1, LANES)), const((1, LANES))],
        out_specs=[row(D_MODEL), row(D_MODEL), pl.BlockSpec((SUBLANES, tm), lambda i: (0, i)), row(LANES),
                   pl.BlockSpec((1, LANES), lambda i: (0, 0))],
        out_shape=[jax.ShapeDtypeStruct((n_tok, D_MODEL), F32), jax.ShapeDtypeStruct((n_tok, D_MODEL), F32),
                   jax.ShapeDtypeStruct((SUBLANES, n_tok), I32), jax.ShapeDtypeStruct((n_tok, LANES), F32),
                   jax.ShapeDtypeStruct((1, LANES), F32)],
        scratch_shapes=[pltpu.VMEM((1, LANES), F32)],
        compiler_params=_params(("arbitrary",)),
        name="out_proj",
    )(x, merged, w_out, ffn_gain[None, :], wr_hi, wr_lo, br, counts_in)


def _block_layout(counts):
    padded = (counts + MOE_TM - 1) // MOE_TM * MOE_TM
    pad_end = jnp.cumsum(padded).astype(I32)
    pad_start = pad_end - padded
    n_used = pad_end[-1] // MOE_TM
    return pad_start, pad_end, n_used


def _moe_rows(n_tok):
    return -(-(n_tok * TOP_K + N_EXPERTS * (MOE_TM - 1)) // MOE_TM) * MOE_TM


DISPATCH_TM = 512


def _wait_rows(src_hbm, dst, sem, n_rows):
    pltpu.make_async_copy(src_hbm.at[pl.ds(0, n_rows)], dst, sem).wait()


def _dispatch_kernel(d0_ref, d1_ref, pe_ref, cnt_ref, nu_ref, hna_ref, hnb_ref, xs_hbm, zbuf, sem, semz,
                     *, n_blocks, a_tiles):
    i = pl.program_id(0)

    def zero_block(row0):
        return pltpu.make_async_copy(zbuf, xs_hbm.at[pl.ds(pl.multiple_of(row0, MOE_TM), MOE_TM)], semz)

    @pl.when(i == 0)
    def _():
        zbuf[...] = jnp.zeros_like(zbuf)
        for start in (True, False):
            for e in range(N_EXPERTS):
                @pl.when(cnt_ref[e] > 0)
                def _():
                    cp = zero_block(pe_ref[e] - MOE_TM)
                    cp.start() if start else cp.wait()

            def tail(b, c):
                cp = zero_block(b * MOE_TM)
                cp.start() if start else cp.wait()
                return c
            lax.fori_loop(nu_ref[0], n_blocks, tail, 0)

    base = i * DISPATCH_TM

    def scatter(hn_ref):
        def body(r, c):
            src = hn_ref.at[pl.ds(r, 1)]
            pltpu.make_async_copy(src, xs_hbm.at[pl.ds(d0_ref[base + r], 1)], sem).start()
            pltpu.make_async_copy(src, xs_hbm.at[pl.ds(d1_ref[base + r], 1)], sem).start()
            return c
        lax.fori_loop(0, DISPATCH_TM, body, 0, unroll=8)
        for _ in range(TOP_K):
            pltpu.make_async_copy(hn_ref, xs_hbm.at[pl.ds(0, DISPATCH_TM)], sem).wait()

    @pl.when(i < a_tiles)
    def _():
        scatter(hna_ref)

    @pl.when(i >= a_tiles)
    def _():
        scatter(hnb_ref)


def dispatch(hn_a, hn_b, dest0, dest1, pad_end, counts, n_used):
    a_tiles, b_tiles = hn_a.shape[0] // DISPATCH_TM, hn_b.shape[0] // DISPATCH_TM
    rows = _moe_rows(hn_a.shape[0] + hn_b.shape[0])
    grid_spec = pltpu.PrefetchScalarGridSpec(
        num_scalar_prefetch=5,
        grid=(a_tiles + b_tiles,),
        in_specs=[pl.BlockSpec((DISPATCH_TM, D_MODEL), lambda i, *_: (jnp.minimum(i, a_tiles - 1), 0)),
                  pl.BlockSpec((DISPATCH_TM, D_MODEL), lambda i, *_: (jnp.maximum(i - a_tiles, 0), 0))],
        out_specs=pl.BlockSpec(memory_space=pl.ANY),
        scratch_shapes=[pltpu.VMEM((MOE_TM, D_MODEL), F32), pltpu.SemaphoreType.DMA(()), pltpu.SemaphoreType.DMA(())],
    )
    return pl.pallas_call(
        functools.partial(_dispatch_kernel, n_blocks=rows // MOE_TM, a_tiles=a_tiles),
        grid_spec=grid_spec,
        out_shape=jax.ShapeDtypeStruct((rows, D_MODEL), F32),
        compiler_params=_params(("arbitrary",)),
        name="dispatch",
    )(dest0, dest1, pad_end, counts, n_used, hn_a, hn_b)


def _moe_kernel(blk_e_ref, nu_ref, xs_ref, wg_ref, wu_ref, wd_ref, ys_ref):
    i = pl.program_id(0)

    @pl.when(i < nu_ref[0])
    def _():
        x = xs_ref[...].astype(BF16)
        hg = _dot(x, wg_ref[...])
        hu = _dot(x, wu_ref[...])
        hmid = (jax.nn.silu(hg) * hu).astype(BF16)
        ys_ref[...] = _dot(hmid, wd_ref[...])

    @pl.when(i >= nu_ref[0])
    def _():
        ys_ref[...] = jnp.zeros_like(ys_ref)


def moe(xs, blk_e, n_used, w_gate, w_up, w_down):
    rows = xs.shape[0]
    grid_spec = pltpu.PrefetchScalarGridSpec(
        num_scalar_prefetch=2,
        grid=(rows // MOE_TM,),
        in_specs=[pl.BlockSpec((MOE_TM, D_MODEL), lambda i, be, nu: (jnp.minimum(i, nu[0] - 1), 0)),
                  pl.BlockSpec((None, D_MODEL, EXPERT_FF), lambda i, be, nu: (be[i], 0, 0)),
                  pl.BlockSpec((None, D_MODEL, EXPERT_FF), lambda i, be, nu: (be[i], 0, 0)),
                  pl.BlockSpec((None, EXPERT_FF, D_MODEL), lambda i, be, nu: (be[i], 0, 0))],
        out_specs=pl.BlockSpec((MOE_TM, D_MODEL), lambda i, be, nu: (i, 0)),
    )
    return pl.pallas_call(
        _moe_kernel,
        grid_spec=grid_spec,
        out_shape=jax.ShapeDtypeStruct((rows, D_MODEL), F32),
        compiler_params=_params(("arbitrary",)),
        name="moe",
    )(blk_e, n_used, xs, w_gate, w_up, w_down)


def _gather_rows(idx_ref, idx0, src_hbm, dst, sem, n_rows):
    def body(r, carry):
        t = idx_ref[idx0 + r]
        pltpu.make_async_copy(src_hbm.at[pl.ds(t, 1)], dst.at[pl.ds(r, 1)], sem).start()
        return carry
    lax.fori_loop(0, n_rows, body, 0, unroll=8)


def _combine_kernel(r0_ref, r1_ref, ys_hbm, h_ref, w_ref, o_ref, buf, sem, *, tm, tok0):
    i = pl.program_id(0)

    def issue(block, slot):
        _gather_rows(r0_ref, tok0 + block * tm, ys_hbm, buf.at[slot, 0], sem.at[slot], tm)
        _gather_rows(r1_ref, tok0 + block * tm, ys_hbm, buf.at[slot, 1], sem.at[slot], tm)

    @pl.when(i == 0)
    def _():
        issue(0, 0)

    @pl.when(i + 1 < pl.num_programs(0))
    def _():
        issue(i + 1, (i + 1) % 2)

    slot = i % 2
    _wait_rows(ys_hbm, buf.at[slot, 0], sem.at[slot], tm)
    _wait_rows(ys_hbm, buf.at[slot, 1], sem.at[slot], tm)
    w = w_ref[...]
    o_ref[...] = h_ref[...] + (buf[slot, 0] * w[:, 0:1] + buf[slot, 1] * w[:, 1:2])


def combine(ys, h, route_w, rows0, rows1, *, tok0, tm=256):
    n_tok = h.shape[0]
    grid_spec = pltpu.PrefetchScalarGridSpec(
        num_scalar_prefetch=2,
        grid=(n_tok // tm,),
        in_specs=[pl.BlockSpec(memory_space=pl.ANY),
                  pl.BlockSpec((tm, D_MODEL), lambda i, a, b: (i, 0)),
                  pl.BlockSpec((tm, LANES), lambda i, a, b: (i, 0))],
        out_specs=pl.BlockSpec((tm, D_MODEL), lambda i, a, b: (i, 0)),
        scratch_shapes=[pltpu.VMEM((2, 2, tm, D_MODEL), F32), pltpu.SemaphoreType.DMA((2,))],
    )
    return pl.pallas_call(
        functools.partial(_combine_kernel, tm=tm, tok0=tok0),
        grid_spec=grid_spec,
        out_shape=jax.ShapeDtypeStruct((n_tok, D_MODEL), F32),
        compiler_params=_params(("arbitrary",)),
        name="combine",
    )(rows0, rows1, ys, h, route_w)


def _regroup_w_in(w_in):
    sizes = (SSM_WIDTH, ATTN_WIDTH, KV_WIDTH, KV_WIDTH, IDX_HEADS * IDX_DIM, IDX_DIM, IDX_HEADS, D_MODEL, D_MODEL)
    u, q, k, v, qi, ki, wi, ga, gb = jnp.split(w_in, np.cumsum(sizes)[:-1].tolist(), axis=1)
    pad = jnp.zeros((D_MODEL, PROJ_COLS - COL_KIWI - IDX_DIM - IDX_HEADS), F32)
    return jnp.concatenate([u, q, ga, gb, k, v, qi, ki, wi, pad], axis=1).astype(BF16)


def _layer(x_p, x_s, cache_k, cache_v, cache_ki, h0_re, h0_im, p):
    bp, tp, _ = x_p.shape
    bs, ts, _ = x_s.shape
    past = cache_k.shape[1]
    n_p, n_s = bp * tp, bs * ts
    n_tok = n_p + n_s

    w_in = _regroup_w_in(p['w_in'])
    ssm_w = _ssm_weights(p['ssm_A_re'], p['ssm_A_im'], p['ssm_log_dt'], p['ssm_B_re'], p['ssm_B_im'],
                         p['ssm_C_re'], p['ssm_C_im'])
    glu_w = (p['w_glu_val'].astype(BF16), p['w_glu_gate'].astype(BF16), p['w_attn_branch'].astype(BF16))
    w_out = p['w_out'].astype(BF16)
    router_w = _router_weights(p['w_router_group'], p['b_router_group'], p['w_router_expert'], p['b_router_expert'])
    seq_tiles = tp // QK_TM

    def front(x, table_pos, table_block):
        proj = in_proj(x, p['norm_mix_g'][None, :], w_in)
        return proj, qk_post(proj, table_pos, table_block, p['q_norm_g'], p['k_norm_g'], p['idx_k_norm_g'])

    def seqs(a, b, t):
        return a.reshape(b, t, a.shape[-1])

    xp = x_p.reshape(n_p, D_MODEL)
    proj_p, (q_b, kf_p, k_b, vf_p, v_b, qi_b, kif_p, ki_b, wi) = front(
        xp, jnp.arange(tp, dtype=I32), lambda i: i % seq_tiles)
    g_p, sre_p, sim_p = ssm(proj_p, ssm_w, p['ssm_D'], jnp.zeros((bp, SSM_LB, 2, SSM_SB), F32),
                            n_batch=bp, seq=tp, row0=0)
    bq = 128
    n_buckets = min(16, tp // bq)
    per = tp // bq // n_buckets
    qp, qip, wip = seqs(q_b, bp, tp), seqs(qi_b, bp, tp), seqs(wi, bp, tp)
    kp, vp, kip = seqs(k_b, bp, tp), seqs(v_b, bp, tp), seqs(ki_b, bp, tp)
    attn_p = jnp.concatenate(
        [dsa(qp, qip, wip, kp, vp, kip, bq=bq, q_blk0=n * per, n_qblk=per, n_keys=(n + 1) * per * bq,
             n_sel=min(IDX_TOPK, tp // 4), packed_bisect=False, stack=1)
         for n in range(n_buckets)], axis=1).reshape(n_p, ATTN_WIDTH)
    merged_p = merge(g_p, attn_p, proj_p, *glu_w)
    h_p, hn_p, ri_p, rw_p, cnt_p = out_proj(xp, merged_p, w_out, p['norm_ffn_g'], router_w, jnp.zeros((1, LANES), F32))

    xs_ = x_s.reshape(n_s, D_MODEL)
    proj_s, (q_b, kf_s, k_b, vf_s, v_b, qi_b, kif_s, ki_b, wi) = front(
        xs_, jnp.tile(past + jnp.arange(ts, dtype=I32), QK_TM // ts), lambda i: 0)
    h0 = jnp.stack([h0_re.reshape(bs, SSM_LB, SSM_SB), h0_im.reshape(bs, SSM_LB, SSM_SB)]).transpose(2, 0, 1, 3)
    g_s, sre_s, sim_s = ssm_step(proj_s, ssm_w, p['ssm_D'], h0, n_batch=bs, seq=ts, row0=0)
    attn_s = dsa_step(seqs(q_b, bs, ts), seqs(qi_b, bs, ts), seqs(wi, bs, ts),
                      cache_k, cache_v, cache_ki,
                      seqs(k_b, bs, ts), seqs(v_b, bs, ts), seqs(ki_b, bs, ts),
                      n_sel=min(IDX_TOPK, (past + ts) // 4)).reshape(n_s, ATTN_WIDTH)
    merged_s = merge(g_s, attn_s, proj_s, *glu_w)
    h_s, hn_s, ri_s, rw_s, cnt = out_proj(xs_, merged_s, w_out, p['norm_ffn_g'], router_w, cnt_p)

    counts = cnt[0, N_EXPERT_GROUPS:ROUTER_COLS].astype(I32)
    pad_start, pad_end, n_used = _block_layout(counts)
    route_i = jnp.concatenate([ri_p, ri_s], axis=1)
    dest0 = pad_start[route_i[0]] + route_i[2]
    dest1 = pad_start[route_i[1]] + route_i[3]
    n_blocks = _moe_rows(n_tok) // MOE_TM
    blk = jnp.minimum(jnp.arange(n_blocks, dtype=I32), n_used - 1)
    blk_e = jnp.minimum(jnp.sum((pad_end[None, :] <= (blk * MOE_TM)[:, None]).astype(I32), axis=1), N_EXPERTS - 1)
    n_used = n_used.reshape(1)

    xs = dispatch(hn_p, hn_s, dest0, dest1, pad_end, counts, n_used)
    ys = moe(xs, blk_e, n_used, p['w_exp_gate'].astype(BF16), p['w_exp_up'].astype(BF16), p['w_exp_down'].astype(BF16))
    y_p = combine(ys, h_p, rw_p, dest0, dest1, tok0=0).reshape(bp, tp, D_MODEL)
    y_s = combine(ys, h_s, rw_s, dest0, dest1, tok0=n_p).reshape(bs, ts, D_MODEL)

    def heads(a, b, t):
        return a.reshape(b, t, N_KV_HEADS, HEAD_DIM)

    new_p = (heads(kf_p, bp, tp), heads(vf_p, bp, tp), kif_p.reshape(bp, tp, IDX_DIM), sre_p, sim_p)
    new_s = (heads(kf_s, bs, ts), heads(vf_s, bs, ts), kif_s.reshape(bs, ts, IDX_DIM), sre_s, sim_s)
    return y_p, y_s, new_p, new_s


def kernel(x_prompt, x_sample, cache_k, cache_v, cache_idx_k, state_ssm_re, state_ssm_im, norm_mix_g, w_in, q_norm_g, k_norm_g, idx_k_norm_g, ssm_A_re, ssm_A_im, ssm_log_dt, ssm_B_re, ssm_B_im, ssm_C_re, ssm_C_im, ssm_D, w_glu_val, w_glu_gate, w_attn_branch, w_out, norm_ffn_g, w_router_group, b_router_group, w_router_expert, b_router_expert, w_exp_gate, w_exp_up, w_exp_down):
    depth = w_in.shape[0]
    assert depth == 1, "prompt and sample tokens are batched through one layer"
    names = ('norm_mix_g', 'w_in', 'q_norm_g', 'k_norm_g', 'idx_k_norm_g', 'ssm_A_re', 'ssm_A_im', 'ssm_log_dt',
             'ssm_B_re', 'ssm_B_im', 'ssm_C_re', 'ssm_C_im', 'ssm_D', 'w_glu_val', 'w_glu_gate', 'w_attn_branch',
             'w_out', 'norm_ffn_g', 'w_router_group', 'b_router_group', 'w_router_expert', 'b_router_expert',
             'w_exp_gate', 'w_exp_up', 'w_exp_down')
    vals = (norm_mix_g, w_in, q_norm_g, k_norm_g, idx_k_norm_g, ssm_A_re, ssm_A_im, ssm_log_dt, ssm_B_re, ssm_B_im,
            ssm_C_re, ssm_C_im, ssm_D, w_glu_val, w_glu_gate, w_attn_branch, w_out, norm_ffn_g, w_router_group,
            b_router_group, w_router_expert, b_router_expert, w_exp_gate, w_exp_up, w_exp_down)
    p = {n: v[0] for n, v in zip(names, vals)}
    y_p, y_s, new_p, new_s = _layer(x_prompt, x_sample, cache_k[0], cache_v[0], cache_idx_k[0],
                                    state_ssm_re[0], state_ssm_im[0], p)
    st_p = tuple(a[None] for a in new_p)
    st_s = tuple(a[None] for a in new_s)
    return (y_p, y_s) + st_p + st_s
```

```python
import functools

import numpy as np
import jax
import jax.numpy as jnp
from jax import lax
from jax.experimental import pallas as pl
from jax.experimental.pallas import tpu as pltpu

F32 = jnp.float32
BF16 = jnp.bfloat16
I32 = jnp.int32

D_MODEL = 2048
CHUNK = 64
SSM_WIDTH = 1024
SSM_GROUP = 16
SSM_GROUPS = 64
SSM_STATE = 64
ATTN_WIDTH = 1024
HEAD_DIM = 128
N_HEADS = 8
N_KV_HEADS = 2
KV_GROUP = 4
IDX_HEADS = 8
IDX_DIM = 64
IDX_TOPK = 256
ROPE_THETA = 500000.0
N_EXPERT_GROUPS = 4
EXPERTS_PER_GROUP = 8
N_EXPERTS = 32
TOP_K = 2
EXPERT_FF = 1024
EPS = 1e-6

LANES = 128
SUBLANES = 8
VMEM_LIMIT = 56 * 1024 * 1024

COL_U, COL_Q, COL_GA, COL_GB, COL_K, COL_V, COL_QI, COL_KIWI = 0, 1024, 2048, 4096, 6144, 6400, 6656, 7168
PROJ_COLS = 7296
PROJ_TN = 2432
KV_WIDTH = N_KV_HEADS * HEAD_DIM

SSM_LB = SSM_WIDTH // LANES
SSM_SB = 8 * SSM_STATE

INT_MIN = np.int32(-2 ** 31)
KEY_NEG_INF = np.int32(np.array([0xFF800000], np.uint32).view(np.int32)[0] ^ 0x7FFFFFFF)


def _params(sem, vmem=VMEM_LIMIT):
    return pltpu.CompilerParams(dimension_semantics=sem, vmem_limit_bytes=vmem)


def _dot(a, b):
    return jnp.dot(a, b, preferred_element_type=F32)


def _dot_nt(a, b):
    return lax.dot_general(a, b, (((1,), (1,)), ((), ())), preferred_element_type=F32)


def _split_bf16(x):
    hi = x.astype(BF16)
    lo = (x - hi.astype(F32)).astype(BF16)
    return hi, lo


def _in_proj_kernel(x_ref, g_ref, w_ref, o_ref, xn_ref):
    @pl.when(pl.program_id(1) == 0)
    def _():
        x = x_ref[...]
        ms = jnp.mean(x * x, axis=-1, keepdims=True)
        xn_ref[...] = (x * lax.rsqrt(ms + EPS) * g_ref[...]).astype(BF16)

    o_ref[...] = _dot(xn_ref[...], w_ref[...])


def in_proj(x, gain, w_bf16, *, tm=512):
    n_tok = x.shape[0]
    return pl.pallas_call(
        _in_proj_kernel,
        grid=(n_tok // tm, PROJ_COLS // PROJ_TN),
        in_specs=[pl.BlockSpec((tm, D_MODEL), lambda i, j: (i, 0)),
                  pl.BlockSpec((1, D_MODEL), lambda i, j: (0, 0)),
                  pl.BlockSpec((D_MODEL, PROJ_TN), lambda i, j: (0, j))],
        out_specs=pl.BlockSpec((tm, PROJ_TN), lambda i, j: (i, j)),
        out_shape=jax.ShapeDtypeStruct((n_tok, PROJ_COLS), F32),
        scratch_shapes=[pltpu.VMEM((tm, D_MODEL), BF16)],
        compiler_params=_params(("arbitrary", "arbitrary")),
        name="in_proj",
    )(x, gain, w_bf16)


def _rope(x, c, s_lo, s_hi, half):
    n = x.shape[-1]
    return x * c + pltpu.roll(x, n - half, 1) * s_lo + pltpu.roll(x, half, 1) * s_hi


def _head_norm(x, g):
    ms = jnp.mean(x * x, axis=-1, keepdims=True)
    return x * lax.rsqrt(ms + EPS) * g


V_AUG = 2 * HEAD_DIM


def _store_v_aug(dst_ref, row0, v_heads):
    n = v_heads[0].shape[0]
    one_col = jnp.where(lax.broadcasted_iota(I32, (n, HEAD_DIM), 1) == 0, 1.0, 0.0).astype(BF16)
    for h, v in enumerate(v_heads):
        dst_ref[row0:row0 + n, h * V_AUG:h * V_AUG + HEAD_DIM] = v.astype(BF16)
        dst_ref[row0:row0 + n, h * V_AUG + HEAD_DIM:(h + 1) * V_AUG] = one_col


def _qk_post_kernel(q_ref, k_ref, v_ref, qi_ref, kw_ref, c128_ref, sl128_ref, sh128_ref,
                    c64_ref, sl64_ref, sh64_ref, qg_ref, kg_ref, ig_ref,
                    qo_ref, kf_ref, kb_ref, vf_ref, vb_ref, qio_ref, kif_ref, kib_ref, wo_ref):
    c128, sl128, sh128 = c128_ref[...], sl128_ref[...], sh128_ref[...]
    c64, sl64, sh64 = c64_ref[...], sl64_ref[...], sh64_ref[...]
    half128 = HEAD_DIM // 8
    half64 = IDX_DIM // 8
    for h in range(N_HEADS):
        sl = slice(h * LANES, (h + 1) * LANES)
        qo_ref[:, sl] = _rope(_head_norm(q_ref[:, sl], qg_ref[...]), c128, sl128, sh128, half128).astype(BF16)
    for h in range(N_KV_HEADS):
        sl = slice(h * LANES, (h + 1) * LANES)
        kk = _rope(_head_norm(k_ref[:, sl], kg_ref[...]), c128, sl128, sh128, half128)
        kf_ref[:, sl] = kk
        kb_ref[:, sl] = kk.astype(BF16)
    v = v_ref[...]
    vf_ref[...] = v
    _store_v_aug(vb_ref, 0, [v[:, h * HEAD_DIM:(h + 1) * HEAD_DIM] for h in range(N_KV_HEADS)])
    lane = lax.broadcasted_iota(I32, c64.shape, 1)
    low = lane < IDX_DIM
    for p in range(IDX_HEADS // 2):
        x = _rope(qi_ref[:, p * LANES:(p + 1) * LANES], c64, sl64, sh64, half64)
        qio_ref[:, (2 * p) * LANES:(2 * p + 1) * LANES] = jnp.where(low, x, 0.0).astype(BF16)
        qio_ref[:, (2 * p + 1) * LANES:(2 * p + 2) * LANES] = jnp.where(low, pltpu.roll(x, IDX_DIM, 1), 0.0).astype(BF16)
    kw = kw_ref[...]
    ms = jnp.sum(jnp.where(low, kw * kw, 0.0), axis=-1, keepdims=True) * (1.0 / IDX_DIM)
    ki = _rope(kw * lax.rsqrt(ms + EPS) * ig_ref[...], c64, sl64, sh64, half64)
    kif_ref[...] = ki[:, :IDX_DIM]
    kib_ref[...] = jnp.where(low, ki, 0.0).astype(BF16)
    wo_ref[...] = (pltpu.roll(kw, IDX_DIM, 1) * IDX_HEADS ** -0.5) * IDX_DIM ** -0.5


def _rope_tables(pos, head_dim):
    r = head_dim // 4
    half = r // 2
    inv = ROPE_THETA ** (-jnp.arange(half, dtype=F32) * 2.0 / r)
    ang = pos.astype(F32)[:, None] * inv[None, :]
    cos, sin = jnp.cos(ang), jnp.sin(ang)
    n = pos.shape[0]
    zh = jnp.zeros((n, half), F32)
    rest = head_dim - r
    c = jnp.concatenate([cos, cos, jnp.ones((n, rest), F32)], axis=-1)
    s_lo = jnp.concatenate([-sin, zh, jnp.zeros((n, rest), F32)], axis=-1)
    s_hi = jnp.concatenate([zh, sin, jnp.zeros((n, rest), F32)], axis=-1)
    rep = LANES // head_dim
    return tuple(jnp.tile(t, (1, rep)) for t in (c, s_lo, s_hi))


QK_TM = 512


def qk_post(proj, table_pos, table_block, q_gain, k_gain, ik_gain):
    tm = QK_TM
    n_tok = proj.shape[0]
    t128 = _rope_tables(table_pos, HEAD_DIM)
    t64 = _rope_tables(table_pos, IDX_DIM)
    ik_gain128 = jnp.concatenate([ik_gain, jnp.zeros((LANES - IDX_DIM,), F32)])[None, :]

    def col(width, start):
        return pl.BlockSpec((tm, width), lambda i: (i, start // width))

    def row(width):
        return pl.BlockSpec((tm, width), lambda i: (i, 0))

    table = pl.BlockSpec((tm, LANES), lambda i: (table_block(i), 0))
    gain = pl.BlockSpec((1, LANES), lambda i: (0, 0))
    return pl.pallas_call(
        _qk_post_kernel,
        grid=(n_tok // tm,),
        in_specs=[col(ATTN_WIDTH, COL_Q), col(KV_WIDTH, COL_K), col(KV_WIDTH, COL_V), col(IDX_HEADS * IDX_DIM, COL_QI),
                  col(LANES, COL_KIWI)] + [table] * 6 + [gain] * 3,
        out_specs=[row(ATTN_WIDTH), row(KV_WIDTH), row(KV_WIDTH), row(KV_WIDTH), row(N_KV_HEADS * V_AUG), row(IDX_HEADS * LANES),
                   row(IDX_DIM), row(LANES), row(LANES)],
        out_shape=[jax.ShapeDtypeStruct((n_tok, ATTN_WIDTH), BF16),
                   jax.ShapeDtypeStruct((n_tok, KV_WIDTH), F32), jax.ShapeDtypeStruct((n_tok, KV_WIDTH), BF16),
                   jax.ShapeDtypeStruct((n_tok, KV_WIDTH), F32), jax.ShapeDtypeStruct((n_tok, N_KV_HEADS * V_AUG), BF16),
                   jax.ShapeDtypeStruct((n_tok, IDX_HEADS * LANES), BF16),
                   jax.ShapeDtypeStruct((n_tok, IDX_DIM), F32), jax.ShapeDtypeStruct((n_tok, LANES), BF16),
                   jax.ShapeDtypeStruct((n_tok, LANES), F32)],
        compiler_params=_params(("arbitrary",)),
        name="qk_post",
    )(proj, proj, proj, proj, proj, *t128, *t64, q_gain[None, :], k_gain[None, :], ik_gain128)


def _gelu_tanh(x):
    return 0.5 * x * (1.0 + jnp.tanh(np.float32(np.sqrt(2.0 / np.pi)) * (x + 0.044715 * (x * x * x))))


SSM_LT = SSM_SB // LANES
SSM_SEG = 64


def _ssm_kernel(u_ref, wb_ref, wc_ref, pw_ref, d_ref, h0_ref, g_ref, sre_ref, sim_ref,
                er_ref, ei_ref, car_ref, up_ref, yp_ref):
    c = pl.program_id(2)

    @pl.when(c == 0)
    def _():
        car_ref[...] = h0_ref[...]

    for j in range(SSM_SEG):
        up_ref[j * SUBLANES:(j + 1) * SUBLANES, :] = u_ref[pl.ds(j, SUBLANES, stride=SSM_SEG), :]
    e = _dot(up_ref[...].astype(BF16), wb_ref[...])
    tiles = [slice(lt * LANES, (lt + 1) * LANES) for lt in range(SSM_LT)]
    for lt, sl in enumerate(tiles):
        er_ref[lt] = e[:, sl]
        ei_ref[lt] = e[:, SSM_SB + lt * LANES:SSM_SB + (lt + 1) * LANES]

    def cmul_add(ar, ai, br, bi, cr, ci):
        return ar * br - ai * bi + cr, ar * bi + ai * br + ci

    lb = [(pw_ref[0, 0:1, sl], pw_ref[1, 0:1, sl]) for sl in tiles]
    zero = jnp.zeros((SUBLANES, LANES), F32)
    st = [(zero, zero)] * SSM_LT
    for j in range(SSM_SEG):
        rows = slice(j * SUBLANES, (j + 1) * SUBLANES)
        for lt in range(SSM_LT):
            st[lt] = cmul_add(*lb[lt], *st[lt], er_ref[lt, rows, :], ei_ref[lt, rows, :])
            er_ref[lt, rows, :] = st[lt][0]
            ei_ref[lt, rows, :] = st[lt][1]

    enter = []
    for lt, sl in enumerate(tiles):
        seg_r, seg_i = pw_ref[0, SSM_SEG - 1:SSM_SEG, sl], pw_ref[1, SSM_SEG - 1:SSM_SEG, sl]
        cr, ci = car_ref[0:1, sl], car_ref[1:2, sl]
        rows_r, rows_i = [], []
        for r in range(SUBLANES):
            rows_r.append(cr)
            rows_i.append(ci)
            cr, ci = cmul_add(seg_r, seg_i, cr, ci, st[lt][0][r:r + 1], st[lt][1][r:r + 1])
        car_ref[0:1, sl] = cr
        car_ref[1:2, sl] = ci
        enter.append((jnp.concatenate(rows_r, axis=0), jnp.concatenate(rows_i, axis=0)))

    for j in range(SSM_SEG):
        rows = slice(j * SUBLANES, (j + 1) * SUBLANES)
        for lt, sl in enumerate(tiles):
            xr, xi = cmul_add(pw_ref[0, j:j + 1, sl], pw_ref[1, j:j + 1, sl], *enter[lt],
                              er_ref[lt, rows, :], ei_ref[lt, rows, :])
            er_ref[lt, rows, :] = xr
            ei_ref[lt, rows, :] = xi

    y = None
    for lt, sl in enumerate(tiles):
        t = _dot(er_ref[lt].astype(BF16), wc_ref[0, sl, :]) - _dot(ei_ref[lt].astype(BF16), wc_ref[1, sl, :])
        y = t if y is None else y + t
    yp_ref[...] = y
    out_rows = 2 * SUBLANES
    for t0 in range(0, SUBLANES * SSM_SEG, out_rows):
        r, j0 = divmod(t0, SSM_SEG)
        rows = slice(t0, t0 + out_rows)
        yt = yp_ref[pl.ds(j0 * SUBLANES + r, out_rows, stride=SUBLANES), :] + d_ref[...] * u_ref[rows, :]
        g_ref[rows, :] = _gelu_tanh(yt).astype(BF16)

    @pl.when(c == pl.num_programs(2) - 1)
    def _():
        sre_ref[...] = car_ref[0:1, :]
        sim_ref[...] = car_ref[1:2, :]


def _ssm_weights(a_re, a_im, log_dt, b_re, b_im, c_re, c_im):
    lam_re, lam_im = a_re, a_im
    dt = jnp.exp(log_dt)[:, None]
    mag = jnp.exp(lam_re * dt)
    lb_re, lb_im = mag * jnp.cos(lam_im * dt), mag * jnp.sin(lam_im * dt)
    den = lam_re * lam_re + lam_im * lam_im
    num_re = lb_re - 1.0
    z_re = (num_re * lam_re + lb_im * lam_im) / den
    z_im = (lb_im * lam_re - num_re * lam_im) / den
    zb_re = z_re[:, :, None] * b_re - z_im[:, :, None] * b_im
    zb_im = z_re[:, :, None] * b_im + z_im[:, :, None] * b_re
    eye = jnp.eye(8, dtype=F32)

    def blockdiag_in(w):
        return jnp.einsum('jgph,gk->jghkp', w.reshape(SSM_LB, 8, SSM_STATE, SSM_GROUP), eye).reshape(SSM_LB, LANES, SSM_SB)

    def blockdiag_out(w):
        return jnp.einsum('jghp,gk->jkpgh', w.reshape(SSM_LB, 8, SSM_GROUP, SSM_STATE), eye).reshape(SSM_LB, SSM_SB, LANES)

    wb = jnp.concatenate([blockdiag_in(zb_re), blockdiag_in(zb_im)], axis=-1).astype(BF16)
    wc = jnp.stack([blockdiag_out(c_re), blockdiag_out(c_im)], axis=1).astype(BF16)

    pr, pi_ = lb_re.reshape(SSM_LB, 1, SSM_SB), lb_im.reshape(SSM_LB, 1, SSM_SB)
    while pr.shape[1] < SSM_SEG:
        tr, ti = pr[:, -1:], pi_[:, -1:]
        pr, pi_ = (jnp.concatenate([pr, pr * tr - pi_ * ti], axis=1), jnp.concatenate([pi_, pr * ti + pi_ * tr], axis=1))
    pw = jnp.stack([pr, pi_], axis=1)
    return wb, wc, pw


def ssm(proj, ssm_w, d_skip, h0, *, n_batch, seq, row0):
    wb, wc, pw = ssm_w
    tc = SUBLANES * SSM_SEG
    n_chunks = seq // tc
    blk0 = row0 // tc
    n_tok = n_batch * seq
    state_shape = jax.ShapeDtypeStruct((n_batch, SSM_LB, 1, SSM_SB), F32)
    state_spec = pl.BlockSpec((None, None, 1, SSM_SB), lambda b, j, c: (b, j, 0, 0))
    g, s_re, s_im = pl.pallas_call(
        _ssm_kernel,
        grid=(n_batch, SSM_LB, n_chunks),
        in_specs=[pl.BlockSpec((tc, LANES), lambda b, j, c: (blk0 + b * n_chunks + c, j)),
                  pl.BlockSpec((None, LANES, 2 * SSM_SB), lambda b, j, c: (j, 0, 0)),
                  pl.BlockSpec((None, 2, SSM_SB, LANES), lambda b, j, c: (j, 0, 0, 0)),
                  pl.BlockSpec((None, 2, SSM_SEG, SSM_SB), lambda b, j, c: (j, 0, 0, 0)),
                  pl.BlockSpec((1, LANES), lambda b, j, c: (0, j)),
                  pl.BlockSpec((None, None, 2, SSM_SB), lambda b, j, c: (b, j, 0, 0))],
        out_specs=[pl.BlockSpec((tc, LANES), lambda b, j, c: (b * n_chunks + c, j)), state_spec, state_spec],
        out_shape=[jax.ShapeDtypeStruct((n_tok, SSM_WIDTH), BF16), state_shape, state_shape],
        scratch_shapes=[pltpu.VMEM((SSM_LT, tc, LANES), F32), pltpu.VMEM((SSM_LT, tc, LANES), F32),
                        pltpu.VMEM((2, SSM_SB), F32), pltpu.VMEM((tc, LANES), F32), pltpu.VMEM((tc, LANES), F32)],
        compiler_params=_params(("arbitrary", "arbitrary", "arbitrary")),
        name="ssm",
    )(proj, wb, wc, pw, d_skip[None, :], h0)
    return g, s_re.reshape(n_batch, SSM_GROUPS, SSM_STATE), s_im.reshape(n_batch, SSM_GROUPS, SSM_STATE)


def _ssm_step_kernel(u_ref, wb_ref, wc_ref, pw_ref, d_ref, h0_ref, g_ref, sre_ref, sim_ref, er_ref, ei_ref, *, seq):
    n_seq = h0_ref.shape[1]
    u = u_ref[...]
    e = _dot(u.astype(BF16), wb_ref[...])
    n_lt = SSM_SB // LANES
    y = d_ref[...] * u
    for lt in range(n_lt):
        sl = slice(lt * LANES, (lt + 1) * LANES)
        er_ref[...] = e[:, lt * LANES:(lt + 1) * LANES]
        ei_ref[...] = e[:, SSM_SB + lt * LANES:SSM_SB + (lt + 1) * LANES]
        lr, li = pw_ref[0, 0:1, sl], pw_ref[1, 0:1, sl]
        sr, si = h0_ref[0, :, sl], h0_ref[1, :, sl]
        for t in range(seq):
            rows = pl.ds(t, n_seq, stride=seq)
            sr, si = lr * sr - li * si + er_ref[rows, :], lr * si + li * sr + ei_ref[rows, :]
            er_ref[rows, :] = sr
            ei_ref[rows, :] = si
        y = y + (_dot(er_ref[...].astype(BF16), wc_ref[0, sl, :]) - _dot(ei_ref[...].astype(BF16), wc_ref[1, sl, :]))
        sre_ref[:, sl] = sr
        sim_ref[:, sl] = si
    g_ref[...] = _gelu_tanh(y).astype(BF16)


def ssm_step(proj, ssm_w, d_skip, h0, *, n_batch, seq, row0):
    wb, wc, pw = ssm_w
    n_tok = n_batch * seq
    assert row0 % n_tok == 0
    state_shape = jax.ShapeDtypeStruct((SSM_LB, n_batch, SSM_SB), F32)
    state_spec = pl.BlockSpec((None, n_batch, SSM_SB), lambda j: (j, 0, 0))
    g, s_re, s_im = pl.pallas_call(
        functools.partial(_ssm_step_kernel, seq=seq),
        grid=(SSM_LB,),
        in_specs=[pl.BlockSpec((n_tok, LANES), lambda j: (row0 // n_tok, j)),
                  pl.BlockSpec((None, LANES, 2 * SSM_SB), lambda j: (j, 0, 0)),
                  pl.BlockSpec((None, 2, SSM_SB, LANES), lambda j: (j, 0, 0, 0)),
                  pl.BlockSpec((None, 2, SSM_SEG, SSM_SB), lambda j: (j, 0, 0, 0)),
                  pl.BlockSpec((1, LANES), lambda j: (0, j)),
                  pl.BlockSpec((None, 2, n_batch, SSM_SB), lambda j: (j, 0, 0, 0))],
        out_specs=[pl.BlockSpec((n_tok, LANES), lambda j: (0, j)), state_spec, state_spec],
        out_shape=[jax.ShapeDtypeStruct((n_tok, SSM_WIDTH), BF16), state_shape, state_shape],
        scratch_shapes=[pltpu.VMEM((n_tok, LANES), F32), pltpu.VMEM((n_tok, LANES), F32)],
        compiler_params=_params(("arbitrary",)),
        name="ssm_step",
    )(proj, wb, wc, pw, d_skip[None, :], h0)

    def per_seq(s):
        return s.transpose(1, 0, 2).reshape(n_batch, SSM_GROUPS, SSM_STATE)

    return g, per_seq(s_re), per_seq(s_im)


def _row_sum(x):
    return jnp.sum(x, axis=1, keepdims=True)


def _row_count(mask):
    return _row_sum(jnp.where(mask, 1, 0))


I16 = jnp.int16
I16_MIN = -2 ** 15


def _count16(ref, cand, compare):
    accs = [None] * 4
    for t in range(ref.shape[1] // LANES):
        x = jnp.where(compare(ref[:, t * LANES:(t + 1) * LANES], cand), I16(1), I16(0))
        accs[t % 4] = x if accs[t % 4] is None else accs[t % 4] + x
    accs = [a for a in accs if a is not None]
    total = accs[0]
    for a in accs[1:]:
        total = total + a
    return _row_sum(total.astype(I32))


def _bisect16(ref, target):
    def step(i, base):
        cand = base + lax.shift_left(np.int32(1), np.int32(15) - i)
        cnt = _count16(ref, cand.astype(I16), lambda a, b: a >= b)
        return jnp.where(cnt >= target, cand, base)
    return lax.fori_loop(0, 16, step, jnp.full((ref.shape[0], 1), I16_MIN, I32))


def _stack_heads(ref, heads):
    return jnp.concatenate([ref[:, h * LANES:(h + 1) * LANES] for h in heads], axis=0)


def _dsa_body(q_ref, qi_ref, wi_ref, k_ref, v_ref, ki_ref, o_ref, key_ref, bias_ref, hi_ref, lo_ref,
              *, q_pos_first, s_valid, n_sel, packed_bisect, stack):
    bq, n_keys = key_ref.shape
    col = lax.broadcasted_iota(I32, (bq, n_keys), 1)
    qpos = q_pos_first + lax.broadcasted_iota(I32, (bq, 1), 0)
    allowed = col < jnp.minimum((qpos // CHUNK + 1) * CHUNK, s_valid)

    ki = ki_ref[...]
    score = None
    for h0 in range(0, IDX_HEADS, stack):
        d = _dot_nt(_stack_heads(qi_ref, range(h0, h0 + stack)), ki)
        for j in range(stack):
            t = jnp.maximum(d[j * bq:(j + 1) * bq], 0.0) * wi_ref[:, h0 + j:h0 + j + 1]
            score = t if score is None else score + t
    score = jnp.where(score == 0.0, 0.0, score)
    bits = pltpu.bitcast(score, I32)
    key = jnp.where(bits < 0, bits ^ np.int32(0x7FFFFFFF), bits)
    key = jnp.where(allowed, key, KEY_NEG_INF)
    key_ref[...] = key

    if packed_bisect:
        hi_ref[...] = (key >> 16).astype(I16)
        lo_ref[...] = ((key & 0xFFFF) + I16_MIN).astype(I16)
        thr_hi = _bisect16(hi_ref, n_sel)
        thr_hi16 = thr_hi.astype(I16)
        need_lo = n_sel - _count16(hi_ref, thr_hi16, lambda a, b: a > b)
        lo_ref[...] = jnp.where(hi_ref[...] == thr_hi16, lo_ref[...], I16(I16_MIN))
        thr_lo = _bisect16(lo_ref, need_lo)
        thr = lax.shift_left(thr_hi, np.int32(16)) + (thr_lo - I16_MIN)
    else:
        def bisect(i, base):
            cand = base + lax.shift_left(np.int32(1), np.int32(31) - i)
            cnt = _row_count(key_ref[...] >= cand)
            return jnp.where(cnt >= n_sel, cand, base)
        thr = lax.fori_loop(0, 32, bisect, jnp.full((bq, 1), INT_MIN, I32))
    thr = jnp.maximum(thr, KEY_NEG_INF)

    key = key_ref[...]
    need = n_sel - _row_count(key > thr)
    n_eq = _row_count(key == thr)
    n_bits = int(n_keys - 1).bit_length()

    def tie_cut():
        def step(i, j0):
            cand = j0 + lax.shift_left(np.int32(1), np.int32(n_bits - 1) - i)
            cnt = _row_sum(jnp.where(key_ref[...] == thr, jnp.where(col < cand, 1, 0), 0))
            return jnp.where(cnt < need, cand, j0)
        return lax.fori_loop(0, n_bits, step, jnp.zeros((bq, 1), I32))

    split = jnp.max(jnp.where(n_eq > need, 1, 0)) > 0
    j_last = lax.cond(split, tie_cut, lambda: jnp.full((bq, 1), n_keys, I32))
    tie_bias = jnp.where(thr == KEY_NEG_INF, -jnp.inf, 0.0)
    bias_ref[...] = jnp.where(key > thr, 0.0,
                              jnp.where(key == thr, jnp.where(col <= j_last, tie_bias, -jnp.inf), -jnp.inf))

    c = np.float32(HEAD_DIM ** -0.5 * np.log2(np.e))
    for h0 in range(0, N_HEADS, stack):
        kv = h0 // KV_GROUP
        heads = range(h0, h0 + stack)
        s_all = _dot_nt(_stack_heads(q_ref, heads), k_ref[:, kv * HEAD_DIM:(kv + 1) * HEAD_DIM])
        ps = []
        for g in range(stack):
            s = s_all[g * bq:(g + 1) * bq] + bias_ref[...]
            m = jnp.max(s, axis=1, keepdims=True)
            ps.append(jnp.exp2((s - m) * c).astype(BF16))
        pv = _dot(jnp.concatenate(ps, axis=0), v_ref[:, kv * V_AUG:(kv + 1) * V_AUG])
        for g, h in enumerate(heads):
            o = pv[g * bq:(g + 1) * bq]
            o_ref[:, h * HEAD_DIM:(h + 1) * HEAD_DIM] = (o[:, :HEAD_DIM] / o[:, HEAD_DIM:HEAD_DIM + 1]).astype(BF16)


def _dsa_scratch(bq, n_keys):
    return [pltpu.VMEM((bq, n_keys), I32), pltpu.VMEM((bq, n_keys), F32),
            pltpu.VMEM((bq, n_keys), I16), pltpu.VMEM((bq, n_keys), I16)]


def _dsa_kernel(q_ref, qi_ref, wi_ref, k_ref, v_ref, ki_ref, o_ref, *scratch, q_pos0, **static):
    bq = scratch[0].shape[0]
    _dsa_body(q_ref, qi_ref, wi_ref, k_ref, v_ref, ki_ref, o_ref, *scratch,
              q_pos_first=q_pos0 + pl.program_id(1) * bq, **static)


def dsa(q, qi, wi, k, v, ki, *, bq, q_blk0, n_qblk, n_keys, n_sel, packed_bisect, stack):
    n_batch, seq = q.shape[:2]

    def qspec(width):
        return pl.BlockSpec((None, bq, width), lambda b, i: (b, q_blk0 + i, 0))

    def kspec(width):
        return pl.BlockSpec((None, n_keys, width), lambda b, i: (b, 0, 0))

    return pl.pallas_call(
        functools.partial(_dsa_kernel, q_pos0=q_blk0 * bq, s_valid=seq, n_sel=n_sel, packed_bisect=packed_bisect,
                          stack=stack),
        grid=(n_batch, n_qblk),
        in_specs=[qspec(ATTN_WIDTH), qspec(IDX_HEADS * LANES), qspec(LANES), kspec(KV_WIDTH), kspec(N_KV_HEADS * V_AUG),
                  kspec(LANES)],
        out_specs=pl.BlockSpec((None, bq, ATTN_WIDTH), lambda b, i: (b, i, 0)),
        out_shape=jax.ShapeDtypeStruct((n_batch, n_qblk * bq, ATTN_WIDTH), BF16),
        scratch_shapes=_dsa_scratch(bq, n_keys),
        compiler_params=_params(("arbitrary", "arbitrary")),
        name="dsa",
    )(q, qi, wi, k, v, ki)


def _dsa_step_kernel(q_ref, qi_ref, wi_ref, ck_hbm, cv_hbm, cki_ref, nk_ref, nv_ref, nki_ref, o_ref,
                     k_buf, v_buf, ki_buf, cache_buf, sem, *scratch, past, n_sel):
    b = pl.program_id(0)

    def cache_copies(seq, slot):
        return [pltpu.make_async_copy(src.at[seq, :, h, :], cache_buf.at[slot, a, h], sem.at[slot])
                for a, src in enumerate((ck_hbm, cv_hbm)) for h in range(N_KV_HEADS)]

    @pl.when(b == 0)
    def _():
        for cp in cache_copies(0, 0):
            cp.start()

    @pl.when(b + 1 < pl.num_programs(0))
    def _():
        for cp in cache_copies(b + 1, (b + 1) % 2):
            cp.start()

    slot = b % 2
    for cp in cache_copies(b, slot):
        cp.wait()

    ts = nk_ref.shape[0]
    n_keys = k_buf.shape[0]
    for h in range(N_KV_HEADS):
        k_buf[0:past, h * HEAD_DIM:(h + 1) * HEAD_DIM] = cache_buf[slot, 0, h].astype(BF16)
    _store_v_aug(v_buf, 0, [cache_buf[slot, 1, h] for h in range(N_KV_HEADS)])
    for buf, new in ((k_buf, nk_ref), (v_buf, nv_ref)):
        buf[past:past + ts, :] = new[...]
        buf[past + ts:n_keys, :] = jnp.zeros((n_keys - past - ts, buf.shape[1]), BF16)
    ki_buf[0:past, 0:IDX_DIM] = cki_ref[...].astype(BF16)
    ki_buf[0:past, IDX_DIM:LANES] = jnp.zeros((past, LANES - IDX_DIM), BF16)
    ki_buf[past:past + ts, :] = nki_ref[...]
    ki_buf[past + ts:n_keys, :] = jnp.zeros((n_keys - past - ts, LANES), BF16)
    _dsa_body(q_ref, qi_ref, wi_ref, k_buf, v_buf, ki_buf, o_ref, *scratch,
              q_pos_first=past, s_valid=past + ts, n_sel=n_sel, packed_bisect=True, stack=KV_GROUP)


def dsa_step(q, qi, wi, cache_k, cache_v, cache_ki, k_new, v_new, ki_new, *, n_sel):
    n_batch, ts = q.shape[:2]
    past = cache_k.shape[1]
    n_keys = -(-(past + ts) // LANES) * LANES

    def spec(rows, width):
        return pl.BlockSpec((None, rows, width), lambda b: (b, 0, 0))

    return pl.pallas_call(
        functools.partial(_dsa_step_kernel, past=past, n_sel=n_sel),
        grid=(n_batch,),
        in_specs=[spec(ts, ATTN_WIDTH), spec(ts, IDX_HEADS * LANES), spec(ts, LANES),
                  pl.BlockSpec(memory_space=pl.ANY), pl.BlockSpec(memory_space=pl.ANY), spec(past, IDX_DIM),
                  spec(ts, KV_WIDTH), spec(ts, N_KV_HEADS * V_AUG), spec(ts, LANES)],
        out_specs=spec(ts, ATTN_WIDTH),
        out_shape=jax.ShapeDtypeStruct((n_batch, ts, ATTN_WIDTH), BF16),
        scratch_shapes=[pltpu.VMEM((n_keys, KV_WIDTH), BF16), pltpu.VMEM((n_keys, N_KV_HEADS * V_AUG), BF16),
                        pltpu.VMEM((n_keys, LANES), BF16),
                        pltpu.VMEM((2, 2, N_KV_HEADS, past, HEAD_DIM), F32), pltpu.SemaphoreType.DMA((2,)),
                        *_dsa_scratch(ts, n_keys)],
        compiler_params=_params(("arbitrary",)),
        name="dsa_step",
    )(q, qi, wi, cache_k, cache_v, cache_ki, k_new, v_new, ki_new)


def _merge_kernel(g_ref, a_ref, ga_ref, gb_ref, wv_ref, wg_ref, wb_ref, o_ref):
    g = g_ref[...]
    branch_a = _dot(g, wv_ref[...]) * jax.nn.sigmoid(_dot(g, wg_ref[...]))
    branch_b = _dot(a_ref[...], wb_ref[...])
    merged = jax.nn.sigmoid(ga_ref[...]) * branch_a + jax.nn.sigmoid(gb_ref[...]) * branch_b
    o_ref[...] = merged.astype(BF16)


def merge(g, attn, proj, w_val, w_gate, w_branch, *, tm=1024, tn=512):
    n_tok = g.shape[0]
    nj = D_MODEL // tn

    def wspec():
        return pl.BlockSpec((SSM_WIDTH, tn), lambda i, j: (0, j))

    return pl.pallas_call(
        _merge_kernel,
        grid=(n_tok // tm, nj),
        in_specs=[pl.BlockSpec((tm, SSM_WIDTH), lambda i, j: (i, 0)),
                  pl.BlockSpec((tm, ATTN_WIDTH), lambda i, j: (i, 0)),
                  pl.BlockSpec((tm, tn), lambda i, j: (i, COL_GA // tn + j)),
                  pl.BlockSpec((tm, tn), lambda i, j: (i, COL_GB // tn + j)),
                  wspec(), wspec(), wspec()],
        out_specs=pl.BlockSpec((tm, tn), lambda i, j: (i, j)),
        out_shape=jax.ShapeDtypeStruct((n_tok, D_MODEL), BF16),
        compiler_params=_params(("arbitrary", "arbitrary")),
        name="merge",
    )(g, attn, proj, proj, w_val, w_gate, w_branch)


ROUTER_COLS = N_EXPERT_GROUPS + N_EXPERTS
MOE_TM = 256


def _first_lane_of_max(x, lane_f):
    m = jnp.max(x, axis=1, keepdims=True)
    return m, jnp.min(jnp.where(x == m, lane_f, float(LANES)), axis=1, keepdims=True)


def _out_proj_kernel(x_ref, m_ref, wo_ref, gn_ref, wrh_ref, wrl_ref, br_ref, cin_ref,
                     h_ref, hn_ref, ri_ref, rw_ref, cnt_ref, carry_ref):
    @pl.when(pl.program_id(0) == 0)
    def _():
        carry_ref[...] = cin_ref[...]

    h = x_ref[...] + _dot(m_ref[...], wo_ref[...])
    h_ref[...] = h
    ms = jnp.mean(h * h, axis=-1, keepdims=True)
    hn = h * lax.rsqrt(ms + EPS) * gn_ref[...]
    hn_ref[...] = hn
    hh, hl = _split_bf16(hn)
    wrh = wrh_ref[...]
    lg = _dot(hh, wrh) + _dot(hl, wrh) + _dot(hh, wrl_ref[...]) + br_ref[...]

    tm = lg.shape[0]
    lane = lax.broadcasted_iota(I32, lg.shape, 1)
    lane_f = lane.astype(F32)
    ninf = -jnp.inf
    gl = jnp.where(lane < N_EXPERT_GROUPS, lg, ninf)
    gmax, gsel = _first_lane_of_max(gl, lane_f)
    g_w = 1.0 / jnp.sum(jnp.exp(gl - gmax), axis=1, keepdims=True)
    lo = N_EXPERT_GROUPS + EXPERTS_PER_GROUP * gsel
    el = jnp.where(lane_f >= lo, jnp.where(lane_f < lo + EXPERTS_PER_GROUP, lg, ninf), ninf)
    v1, i1 = _first_lane_of_max(el, lane_f)
    el2 = jnp.where(lane_f == i1, ninf, el)
    v2, i2 = _first_lane_of_max(el2, lane_f)
    t = jnp.exp(v2 - v1)
    s1 = 1.0 / (1.0 + t)
    w1 = s1 * g_w
    w2 = (t * s1) * g_w

    m1 = jnp.where(lane_f == i1, 1.0, 0.0)
    m2 = jnp.where(lane_f == i2, 1.0, 0.0)
    both = m1 + m2
    tri = jnp.where(lax.broadcasted_iota(I32, (tm, tm), 0) > lax.broadcasted_iota(I32, (tm, tm), 1), 1.0, 0.0)
    before = _dot(tri.astype(BF16), both.astype(BF16)) + carry_ref[...]
    r1 = jnp.sum(before * m1, axis=1, keepdims=True)
    r2 = jnp.sum(before * m2, axis=1, keepdims=True)
    carry_ref[...] = carry_ref[...] + jnp.sum(both, axis=0, keepdims=True)
    cnt_ref[...] = carry_ref[...]
    e1 = i1 - float(N_EXPERT_GROUPS)
    e2 = i2 - float(N_EXPERT_GROUPS)
    fields = jnp.where(lane == 0, e1, jnp.where(lane == 1, e2, jnp.where(lane == 2, r1, jnp.where(lane == 3, r2, 0.0))))
    ri_ref[...] = fields.T[0:SUBLANES, :].astype(I32)
    rw_ref[...] = jnp.where(lane == 0, w1, jnp.where(lane == 1, w2, 0.0))


def _router_weights(w_router_group, b_router_group, w_router_expert, b_router_expert):
    wr = jnp.concatenate([w_router_group, w_router_expert, jnp.zeros((D_MODEL, LANES - ROUTER_COLS), F32)], axis=1)
    wr_hi = wr.astype(BF16)
    wr_lo = (wr - wr_hi.astype(F32)).astype(BF16)
    br = jnp.concatenate([b_router_group, b_router_expert, jnp.zeros((LANES - ROUTER_COLS,), F32)])[None, :]
    return wr_hi, wr_lo, br


def out_proj(x, merged, w_out, ffn_gain, router_w, counts_in, *, tm=256):
    n_tok = x.shape[0]
    wr_hi, wr_lo, br = router_w

    def row(width):
        return pl.BlockSpec((tm, width), lambda i: (i, 0))

    def const(shape):
        return pl.BlockSpec(shape, lambda i: (0, 0), pipeline_mode=pl.Buffered(1))

    return pl.pallas_call(
        _out_proj_kernel,
        grid=(n_tok // tm,),
        in_specs=[row(D_MODEL), row(D_MODEL), const((D_MODEL, D_MODEL)), const((1, D_MODEL)),
                  const((D_MODEL, LANES)), const((D_MODEL, LANES)), const((1, LANES)), const((1, LANES))],
        out_specs=[row(D_MODEL), row(D_MODEL), pl.BlockSpec((SUBLANES, tm), lambda i: (0, i)), row(LANES),
                   pl.BlockSpec((1, LANES), lambda i: (0, 0))],
        out_shape=[jax.ShapeDtypeStruct((n_tok, D_MODEL), F32), jax.ShapeDtypeStruct((n_tok, D_MODEL), F32),
                   jax.ShapeDtypeStruct((SUBLANES, n_tok), I32), jax.ShapeDtypeStruct((n_tok, LANES), F32),
                   jax.ShapeDtypeStruct((1, LANES), F32)],
        scratch_shapes=[pltpu.VMEM((1, LANES), F32)],
        compiler_params=_params(("arbitrary",)),
        name="out_proj",
    )(x, merged, w_out, ffn_gain[None, :], wr_hi, wr_lo, br, counts_in)


def _block_layout(counts):
    padded = (counts + MOE_TM - 1) // MOE_TM * MOE_TM
    pad_end = jnp.cumsum(padded).astype(I32)
    pad_start = pad_end - padded
    n_used = pad_end[-1] // MOE_TM
    return pad_start, pad_end, n_used


def _moe_rows(n_tok):
    return -(-(n_tok * TOP_K + N_EXPERTS * (MOE_TM - 1)) // MOE_TM) * MOE_TM


DISPATCH_TM = 512


def _wait_rows(src_hbm, dst, sem, n_rows):
    pltpu.make_async_copy(src_hbm.at[pl.ds(0, n_rows)], dst, sem).wait()


def _dispatch_kernel(d0_ref, d1_ref, pe_ref, cnt_ref, nu_ref, hna_ref, hnb_ref, xs_hbm, zbuf, sem, semz,
                     *, n_blocks, a_tiles):
    i = pl.program_id(0)

    def zero_block(row0):
        return pltpu.make_async_copy(zbuf, xs_hbm.at[pl.ds(pl.multiple_of(row0, MOE_TM), MOE_TM)], semz)

    @pl.when(i == 0)
    def _():
        zbuf[...] = jnp.zeros_like(zbuf)
        for start in (True, False):
            for e in range(N_EXPERTS):
                @pl.when(cnt_ref[e] > 0)
                def _():
                    cp = zero_block(pe_ref[e] - MOE_TM)
                    cp.start() if start else cp.wait()

            def tail(b, c):
                cp = zero_block(b * MOE_TM)
                cp.start() if start else cp.wait()
                return c
            lax.fori_loop(nu_ref[0], n_blocks, tail, 0)

    base = i * DISPATCH_TM

    def scatter(hn_ref):
        def body(r, c):
            src = hn_ref.at[pl.ds(r, 1)]
            pltpu.make_async_copy(src, xs_hbm.at[pl.ds(d0_ref[base + r], 1)], sem).start()
            pltpu.make_async_copy(src, xs_hbm.at[pl.ds(d1_ref[base + r], 1)], sem).start()
            return c
        lax.fori_loop(0, DISPATCH_TM, body, 0, unroll=8)
        for _ in range(TOP_K):
            pltpu.make_async_copy(hn_ref, xs_hbm.at[pl.ds(0, DISPATCH_TM)], sem).wait()

    @pl.when(i < a_tiles)
    def _():
        scatter(hna_ref)

    @pl.when(i >= a_tiles)
    def _():
        scatter(hnb_ref)


def dispatch(hn_a, hn_b, dest0, dest1, pad_end, counts, n_used):
    a_tiles, b_tiles = hn_a.shape[0] // DISPATCH_TM, hn_b.shape[0] // DISPATCH_TM
    rows = _moe_rows(hn_a.shape[0] + hn_b.shape[0])
    grid_spec = pltpu.PrefetchScalarGridSpec(
        num_scalar_prefetch=5,
        grid=(a_tiles + b_tiles,),
        in_specs=[pl.BlockSpec((DISPATCH_TM, D_MODEL), lambda i, *_: (jnp.minimum(i, a_tiles - 1), 0)),
                  pl.BlockSpec((DISPATCH_TM, D_MODEL), lambda i, *_: (jnp.maximum(i - a_tiles, 0), 0))],
        out_specs=pl.BlockSpec(memory_space=pl.ANY),
        scratch_shapes=[pltpu.VMEM((MOE_TM, D_MODEL), F32), pltpu.SemaphoreType.DMA(()), pltpu.SemaphoreType.DMA(())],
    )
    return pl.pallas_call(
        functools.partial(_dispatch_kernel, n_blocks=rows // MOE_TM, a_tiles=a_tiles),
        grid_spec=grid_spec,
        out_shape=jax.ShapeDtypeStruct((rows, D_MODEL), F32),
        compiler_params=_params(("arbitrary",)),
        name="dispatch",
    )(dest0, dest1, pad_end, counts, n_used, hn_a, hn_b)


MOE_UNITS = 8
MOE_UG = D_MODEL // MOE_UNITS
MOE_UD = EXPERT_FF // MOE_UNITS


def _moe_kernel(blk_e_ref, nu_ref, nxt_ref, upb_ref, xs_ref, wg_hbm, wu_hbm, wd_hbm, ys_ref,
                wg_bf, wu_bf, wd_bf, stg_g, stg_u, stg_d, sem, st_ref):
    i = pl.program_id(0)
    cur_slot, pos, cur_e = 0, 1, 2

    def unit_copies(e, unit, s):
        g_rows = pl.ds(pl.multiple_of(unit * MOE_UG, MOE_UG), MOE_UG)
        d_rows = pl.ds(pl.multiple_of(unit * MOE_UD, MOE_UD), MOE_UD)
        return (pltpu.make_async_copy(wg_hbm.at[e, g_rows, :], stg_g.at[s], sem.at[s]),
                pltpu.make_async_copy(wu_hbm.at[e, g_rows, :], stg_u.at[s], sem.at[s]),
                pltpu.make_async_copy(wd_hbm.at[e, d_rows, :], stg_d.at[s], sem.at[s]))

    def start_unit(e, unit):
        for cp in unit_copies(e, unit, unit % 2):
            cp.start()

    def begin_load(e):
        st_ref[pos] = 0
        start_unit(e, 0)
        start_unit(e, 1)

    def advance(e, slot, n):
        def body(_, c):
            unit = st_ref[pos]

            @pl.when(unit < MOE_UNITS)
            def _():
                s = unit % 2
                for cp in unit_copies(e, unit, s):
                    cp.wait()
                g_rows = pl.ds(pl.multiple_of(unit * MOE_UG, MOE_UG), MOE_UG)
                d_rows = pl.ds(pl.multiple_of(unit * MOE_UD, MOE_UD), MOE_UD)
                wg_bf[slot, g_rows, :] = stg_g[s].astype(BF16)
                wu_bf[slot, g_rows, :] = stg_u[s].astype(BF16)
                wd_bf[slot, d_rows, :] = stg_d[s].astype(BF16)

                @pl.when(unit + 2 < MOE_UNITS)
                def _():
                    start_unit(e, unit + 2)
                st_ref[pos] = unit + 1
            return c
        lax.fori_loop(0, n, body, 0)

    def load_next(nxt):
        @pl.when(nxt >= 0)
        def _():
            begin_load(nxt)

        @pl.when(nxt < 0)
        def _():
            st_ref[pos] = MOE_UNITS

    @pl.when(i < nu_ref[0])
    def _():
        e = blk_e_ref[i]
        nxt = nxt_ref[i]

        @pl.when(i == 0)
        def _():
            st_ref[cur_slot] = 0
            st_ref[cur_e] = e
            begin_load(e)
            advance(e, 0, MOE_UNITS)
            load_next(nxt)

        @pl.when(jnp.logical_and(i > 0, e != st_ref[cur_e]))
        def _():
            slot = 1 - st_ref[cur_slot]
            advance(e, slot, MOE_UNITS)
            st_ref[cur_slot] = slot
            st_ref[cur_e] = e
            load_next(nxt)

        slot = st_ref[cur_slot]
        x = xs_ref[...].astype(BF16)
        hg = _dot(x, wg_bf[slot])
        hu = _dot(x, wu_bf[slot])
        hmid = (jax.nn.silu(hg) * hu).astype(BF16)
        ys_ref[...] = _dot(hmid, wd_bf[slot])

        @pl.when(nxt >= 0)
        def _():
            advance(nxt, 1 - slot, upb_ref[i])

    @pl.when(i >= nu_ref[0])
    def _():
        ys_ref[...] = jnp.zeros_like(ys_ref)


def moe(xs, blk_e, n_used, nxt_e, units_per_block, w_gate, w_up, w_down):
    rows = xs.shape[0]
    grid_spec = pltpu.PrefetchScalarGridSpec(
        num_scalar_prefetch=4,
        grid=(rows // MOE_TM,),
        in_specs=[pl.BlockSpec((MOE_TM, D_MODEL), lambda i, be, nu, nx, ub: (jnp.minimum(i, nu[0] - 1), 0)),
                  pl.BlockSpec(memory_space=pl.ANY), pl.BlockSpec(memory_space=pl.ANY), pl.BlockSpec(memory_space=pl.ANY)],
        out_specs=pl.BlockSpec((MOE_TM, D_MODEL), lambda i, be, nu, nx, ub: (i, 0)),
        scratch_shapes=[pltpu.VMEM((2, D_MODEL, EXPERT_FF), BF16), pltpu.VMEM((2, D_MODEL, EXPERT_FF), BF16),
                        pltpu.VMEM((2, EXPERT_FF, D_MODEL), BF16),
                        pltpu.VMEM((2, MOE_UG, EXPERT_FF), F32), pltpu.VMEM((2, MOE_UG, EXPERT_FF), F32),
                        pltpu.VMEM((2, MOE_UD, D_MODEL), F32),
                        pltpu.SemaphoreType.DMA((2,)), pltpu.SMEM((3,), I32)],
    )
    return pl.pallas_call(
        _moe_kernel,
        grid_spec=grid_spec,
        out_shape=jax.ShapeDtypeStruct((rows, D_MODEL), F32),
        compiler_params=_params(("arbitrary",)),
        name="moe",
    )(blk_e, n_used, nxt_e, units_per_block, xs, w_gate, w_up, w_down)


def _gather_rows(idx_ref, idx0, src_hbm, dst, sem, n_rows):
    def body(r, carry):
        t = idx_ref[idx0 + r]
        pltpu.make_async_copy(src_hbm.at[pl.ds(t, 1)], dst.at[pl.ds(r, 1)], sem).start()
        return carry
    lax.fori_loop(0, n_rows, body, 0, unroll=8)


def _combine_kernel(r0_ref, r1_ref, ys_hbm, h_ref, w_ref, o_ref, buf, sem, *, tm, tok0):
    i = pl.program_id(0)

    def issue(block, slot):
        _gather_rows(r0_ref, tok0 + block * tm, ys_hbm, buf.at[slot, 0], sem.at[slot], tm)
        _gather_rows(r1_ref, tok0 + block * tm, ys_hbm, buf.at[slot, 1], sem.at[slot], tm)

    @pl.when(i == 0)
    def _():
        issue(0, 0)

    @pl.when(i + 1 < pl.num_programs(0))
    def _():
        issue(i + 1, (i + 1) % 2)

    slot = i % 2
    _wait_rows(ys_hbm, buf.at[slot, 0], sem.at[slot], tm)
    _wait_rows(ys_hbm, buf.at[slot, 1], sem.at[slot], tm)
    w = w_ref[...]
    o_ref[...] = h_ref[...] + (buf[slot, 0] * w[:, 0:1] + buf[slot, 1] * w[:, 1:2])


def combine(ys, h, route_w, rows0, rows1, *, tok0, tm=256):
    n_tok = h.shape[0]
    grid_spec = pltpu.PrefetchScalarGridSpec(
        num_scalar_prefetch=2,
        grid=(n_tok // tm,),
        in_specs=[pl.BlockSpec(memory_space=pl.ANY),
                  pl.BlockSpec((tm, D_MODEL), lambda i, a, b: (i, 0)),
                  pl.BlockSpec((tm, LANES), lambda i, a, b: (i, 0))],
        out_specs=pl.BlockSpec((tm, D_MODEL), lambda i, a, b: (i, 0)),
        scratch_shapes=[pltpu.VMEM((2, 2, tm, D_MODEL), F32), pltpu.SemaphoreType.DMA((2,))],
    )
    return pl.pallas_call(
        functools.partial(_combine_kernel, tm=tm, tok0=tok0),
        grid_spec=grid_spec,
        out_shape=jax.ShapeDtypeStruct((n_tok, D_MODEL), F32),
        compiler_params=_params(("arbitrary",)),
        name="combine",
    )(rows0, rows1, ys, h, route_w)


def _regroup_w_in(w_in):
    sizes = (SSM_WIDTH, ATTN_WIDTH, KV_WIDTH, KV_WIDTH, IDX_HEADS * IDX_DIM, IDX_DIM, IDX_HEADS, D_MODEL, D_MODEL)
    u, q, k, v, qi, ki, wi, ga, gb = jnp.split(w_in, np.cumsum(sizes)[:-1].tolist(), axis=1)
    pad = jnp.zeros((D_MODEL, PROJ_COLS - COL_KIWI - IDX_DIM - IDX_HEADS), F32)
    return jnp.concatenate([u, q, ga, gb, k, v, qi, ki, wi, pad], axis=1).astype(BF16)


def _layer(x_p, x_s, cache_k, cache_v, cache_ki, h0_re, h0_im, p):
    bp, tp, _ = x_p.shape
    bs, ts, _ = x_s.shape
    past = cache_k.shape[1]
    n_p, n_s = bp * tp, bs * ts
    n_tok = n_p + n_s

    w_in = _regroup_w_in(p['w_in'])
    ssm_w = _ssm_weights(p['ssm_A_re'], p['ssm_A_im'], p['ssm_log_dt'], p['ssm_B_re'], p['ssm_B_im'],
                         p['ssm_C_re'], p['ssm_C_im'])
    glu_w = (p['w_glu_val'].astype(BF16), p['w_glu_gate'].astype(BF16), p['w_attn_branch'].astype(BF16))
    w_out = p['w_out'].astype(BF16)
    router_w = _router_weights(p['w_router_group'], p['b_router_group'], p['w_router_expert'], p['b_router_expert'])
    seq_tiles = tp // QK_TM

    def front(x, table_pos, table_block):
        proj = in_proj(x, p['norm_mix_g'][None, :], w_in)
        return proj, qk_post(proj, table_pos, table_block, p['q_norm_g'], p['k_norm_g'], p['idx_k_norm_g'])

    def seqs(a, b, t):
        return a.reshape(b, t, a.shape[-1])

    xp = x_p.reshape(n_p, D_MODEL)
    proj_p, (q_b, kf_p, k_b, vf_p, v_b, qi_b, kif_p, ki_b, wi) = front(
        xp, jnp.arange(tp, dtype=I32), lambda i: i % seq_tiles)
    g_p, sre_p, sim_p = ssm(proj_p, ssm_w, p['ssm_D'], jnp.zeros((bp, SSM_LB, 2, SSM_SB), F32),
                            n_batch=bp, seq=tp, row0=0)
    bq = 128
    n_buckets = min(16, tp // bq)
    per = tp // bq // n_buckets
    qp, qip, wip = seqs(q_b, bp, tp), seqs(qi_b, bp, tp), seqs(wi, bp, tp)
    kp, vp, kip = seqs(k_b, bp, tp), seqs(v_b, bp, tp), seqs(ki_b, bp, tp)
    attn_p = jnp.concatenate(
        [dsa(qp, qip, wip, kp, vp, kip, bq=bq, q_blk0=n * per, n_qblk=per, n_keys=(n + 1) * per * bq,
             n_sel=min(IDX_TOPK, tp // 4), packed_bisect=False, stack=1)
         for n in range(n_buckets)], axis=1).reshape(n_p, ATTN_WIDTH)
    merged_p = merge(g_p, attn_p, proj_p, *glu_w)
    h_p, hn_p, ri_p, rw_p, cnt_p = out_proj(xp, merged_p, w_out, p['norm_ffn_g'], router_w, jnp.zeros((1, LANES), F32))

    xs_ = x_s.reshape(n_s, D_MODEL)
    proj_s, (q_b, kf_s, k_b, vf_s, v_b, qi_b, kif_s, ki_b, wi) = front(
        xs_, jnp.tile(past + jnp.arange(ts, dtype=I32), QK_TM // ts), lambda i: 0)
    h0 = jnp.stack([h0_re.reshape(bs, SSM_LB, SSM_SB), h0_im.reshape(bs, SSM_LB, SSM_SB)]).transpose(2, 0, 1, 3)
    g_s, sre_s, sim_s = ssm_step(proj_s, ssm_w, p['ssm_D'], h0, n_batch=bs, seq=ts, row0=0)
    attn_s = dsa_step(seqs(q_b, bs, ts), seqs(qi_b, bs, ts), seqs(wi, bs, ts),
                      cache_k, cache_v, cache_ki,
                      seqs(k_b, bs, ts), seqs(v_b, bs, ts), seqs(ki_b, bs, ts),
                      n_sel=min(IDX_TOPK, (past + ts) // 4)).reshape(n_s, ATTN_WIDTH)
    merged_s = merge(g_s, attn_s, proj_s, *glu_w)
    h_s, hn_s, ri_s, rw_s, cnt = out_proj(xs_, merged_s, w_out, p['norm_ffn_g'], router_w, cnt_p)

    counts = cnt[0, N_EXPERT_GROUPS:ROUTER_COLS].astype(I32)
    pad_start, pad_end, n_used = _block_layout(counts)
    route_i = jnp.concatenate([ri_p, ri_s], axis=1)
    dest0 = pad_start[route_i[0]] + route_i[2]
    dest1 = pad_start[route_i[1]] + route_i[3]
    n_blocks = _moe_rows(n_tok) // MOE_TM
    blk = jnp.minimum(jnp.arange(n_blocks, dtype=I32), n_used - 1)
    blk_e = jnp.minimum(jnp.sum((pad_end[None, :] <= (blk * MOE_TM)[:, None]).astype(I32), axis=1), N_EXPERTS - 1)
    after = pad_end[blk_e] // MOE_TM
    nxt_e = jnp.where(after < n_used, blk_e[jnp.minimum(after, n_blocks - 1)], -1).astype(I32)
    blocks_of_e = jnp.maximum((pad_end - pad_start)[blk_e] // MOE_TM, 1)
    units_per_block = ((MOE_UNITS + blocks_of_e - 1) // blocks_of_e).astype(I32)
    n_used = n_used.reshape(1)

    xs = dispatch(hn_p, hn_s, dest0, dest1, pad_end, counts, n_used)
    ys = moe(xs, blk_e, n_used, nxt_e, units_per_block, p['w_exp_gate'], p['w_exp_up'], p['w_exp_down'])
    y_p = combine(ys, h_p, rw_p, dest0, dest1, tok0=0).reshape(bp, tp, D_MODEL)
    y_s = combine(ys, h_s, rw_s, dest0, dest1, tok0=n_p).reshape(bs, ts, D_MODEL)

    def heads(a, b, t):
        return a.reshape(b, t, N_KV_HEADS, HEAD_DIM)

    new_p = (heads(kf_p, bp, tp), heads(vf_p, bp, tp), kif_p.reshape(bp, tp, IDX_DIM), sre_p, sim_p)
    new_s = (heads(kf_s, bs, ts), heads(vf_s, bs, ts), kif_s.reshape(bs, ts, IDX_DIM), sre_s, sim_s)
    return y_p, y_s, new_p, new_s


def kernel(x_prompt, x_sample, cache_k, cache_v, cache_idx_k, state_ssm_re, state_ssm_im, norm_mix_g, w_in, q_norm_g, k_norm_g, idx_k_norm_g, ssm_A_re, ssm_A_im, ssm_log_dt, ssm_B_re, ssm_B_im, ssm_C_re, ssm_C_im, ssm_D, w_glu_val, w_glu_gate, w_attn_branch, w_out, norm_ffn_g, w_router_group, b_router_group, w_router_expert, b_router_expert, w_exp_gate, w_exp_up, w_exp_down):
    depth = w_in.shape[0]
    assert depth == 1, "prompt and sample tokens are batched through one layer"
    names = ('norm_mix_g', 'w_in', 'q_norm_g', 'k_norm_g', 'idx_k_norm_g', 'ssm_A_re', 'ssm_A_im', 'ssm_log_dt',
             'ssm_B_re', 'ssm_B_im', 'ssm_C_re', 'ssm_C_im', 'ssm_D', 'w_glu_val', 'w_glu_gate', 'w_attn_branch',
             'w_out', 'norm_ffn_g', 'w_router_group', 'b_router_group', 'w_router_expert', 'b_router_expert',
             'w_exp_gate', 'w_exp_up', 'w_exp_down')
    vals = (norm_mix_g, w_in, q_norm_g, k_norm_g, idx_k_norm_g, ssm_A_re, ssm_A_im, ssm_log_dt, ssm_B_re, ssm_B_im,
            ssm_C_re, ssm_C_im, ssm_D, w_glu_val, w_glu_gate, w_attn_branch, w_out, norm_ffn_g, w_router_group,
            b_router_group, w_router_expert, b_router_expert, w_exp_gate, w_exp_up, w_exp_down)
    p = {n: v[0] for n, v in zip(names, vals)}
    y_p, y_s, new_p, new_s = _layer(x_prompt, x_sample, cache_k[0], cache_v[0], cache_idx_k[0],
                                    state_ssm_re[0], state_ssm_im[0], p)
    st_p = tuple(a[None] for a in new_p)
    st_s = tuple(a[None] for a in new_s)
    return (y_p, y_s) + st_p + st_s
```

```python
import functools

import numpy as np
import jax
import jax.numpy as jnp
from jax import lax
from jax.experimental import pallas as pl
from jax.experimental.pallas import tpu as pltpu

F32 = jnp.float32
BF16 = jnp.bfloat16
I32 = jnp.int32

D_MODEL = 2048
CHUNK = 64
SSM_WIDTH = 1024
SSM_GROUP = 16
SSM_GROUPS = 64
SSM_STATE = 64
ATTN_WIDTH = 1024
HEAD_DIM = 128
N_HEADS = 8
N_KV_HEADS = 2
KV_GROUP = 4
IDX_HEADS = 8
IDX_DIM = 64
IDX_TOPK = 256
ROPE_THETA = 500000.0
N_EXPERT_GROUPS = 4
EXPERTS_PER_GROUP = 8
N_EXPERTS = 32
TOP_K = 2
EXPERT_FF = 1024
EPS = 1e-6

LANES = 128
SUBLANES = 8
VMEM_LIMIT = 56 * 1024 * 1024

COL_U, COL_Q, COL_GA, COL_GB, COL_K, COL_V, COL_QI, COL_KIWI = 0, 1024, 2048, 4096, 6144, 6400, 6656, 7168
PROJ_COLS = 7296
PROJ_TN = 2432
KV_WIDTH = N_KV_HEADS * HEAD_DIM

SSM_LB = SSM_WIDTH // LANES
SSM_SB = 8 * SSM_STATE

INT_MIN = np.int32(-2 ** 31)
KEY_NEG_INF = np.int32(np.array([0xFF800000], np.uint32).view(np.int32)[0] ^ 0x7FFFFFFF)


def _params(sem, vmem=VMEM_LIMIT):
    return pltpu.CompilerParams(dimension_semantics=sem, vmem_limit_bytes=vmem)


def _dot(a, b):
    return jnp.dot(a, b, preferred_element_type=F32)


def _dot_nt(a, b):
    return lax.dot_general(a, b, (((1,), (1,)), ((), ())), preferred_element_type=F32)


def _split_bf16(x):
    hi = x.astype(BF16)
    lo = (x - hi.astype(F32)).astype(BF16)
    return hi, lo


def _in_proj_kernel(x_ref, g_ref, w_ref, o_ref, xn_ref):
    @pl.when(pl.program_id(1) == 0)
    def _():
        x = x_ref[...]
        ms = jnp.mean(x * x, axis=-1, keepdims=True)
        xn_ref[...] = (x * lax.rsqrt(ms + EPS) * g_ref[...]).astype(BF16)

    o_ref[...] = _dot(xn_ref[...], w_ref[...])


def in_proj(x, gain, w_bf16, *, tm=512):
    n_tok = x.shape[0]
    return pl.pallas_call(
        _in_proj_kernel,
        grid=(n_tok // tm, PROJ_COLS // PROJ_TN),
        in_specs=[pl.BlockSpec((tm, D_MODEL), lambda i, j: (i, 0)),
                  pl.BlockSpec((1, D_MODEL), lambda i, j: (0, 0)),
                  pl.BlockSpec((D_MODEL, PROJ_TN), lambda i, j: (0, j))],
        out_specs=pl.BlockSpec((tm, PROJ_TN), lambda i, j: (i, j)),
        out_shape=jax.ShapeDtypeStruct((n_tok, PROJ_COLS), F32),
        scratch_shapes=[pltpu.VMEM((tm, D_MODEL), BF16)],
        compiler_params=_params(("arbitrary", "arbitrary")),
        name="in_proj",
    )(x, gain, w_bf16)


def _rope(x, c, s_lo, s_hi, half):
    n = x.shape[-1]
    return x * c + pltpu.roll(x, n - half, 1) * s_lo + pltpu.roll(x, half, 1) * s_hi


def _head_norm(x, g):
    ms = jnp.mean(x * x, axis=-1, keepdims=True)
    return x * lax.rsqrt(ms + EPS) * g


V_AUG = 2 * HEAD_DIM


def _store_v_aug(dst_ref, row0, v_heads):
    n = v_heads[0].shape[0]
    one_col = jnp.where(lax.broadcasted_iota(I32, (n, HEAD_DIM), 1) == 0, 1.0, 0.0).astype(BF16)
    for h, v in enumerate(v_heads):
        dst_ref[row0:row0 + n, h * V_AUG:h * V_AUG + HEAD_DIM] = v.astype(BF16)
        dst_ref[row0:row0 + n, h * V_AUG + HEAD_DIM:(h + 1) * V_AUG] = one_col


def _qk_post_kernel(q_ref, k_ref, v_ref, qi_ref, kw_ref, c128_ref, sl128_ref, sh128_ref,
                    c64_ref, sl64_ref, sh64_ref, qg_ref, kg_ref, ig_ref,
                    qo_ref, kf_ref, kb_ref, vf_ref, vb_ref, qio_ref, kif_ref, kib_ref, wo_ref):
    c128, sl128, sh128 = c128_ref[...], sl128_ref[...], sh128_ref[...]
    c64, sl64, sh64 = c64_ref[...], sl64_ref[...], sh64_ref[...]
    half128 = HEAD_DIM // 8
    half64 = IDX_DIM // 8
    for h in range(N_HEADS):
        sl = slice(h * LANES, (h + 1) * LANES)
        qo_ref[:, sl] = _rope(_head_norm(q_ref[:, sl], qg_ref[...]), c128, sl128, sh128, half128).astype(BF16)
    for h in range(N_KV_HEADS):
        sl = slice(h * LANES, (h + 1) * LANES)
        kk = _rope(_head_norm(k_ref[:, sl], kg_ref[...]), c128, sl128, sh128, half128)
        kf_ref[:, sl] = kk
        kb_ref[:, sl] = kk.astype(BF16)
    v = v_ref[...]
    vf_ref[...] = v
    _store_v_aug(vb_ref, 0, [v[:, h * HEAD_DIM:(h + 1) * HEAD_DIM] for h in range(N_KV_HEADS)])
    lane = lax.broadcasted_iota(I32, c64.shape, 1)
    low = lane < IDX_DIM
    for p in range(IDX_HEADS // 2):
        x = _rope(qi_ref[:, p * LANES:(p + 1) * LANES], c64, sl64, sh64, half64)
        qio_ref[:, (2 * p) * LANES:(2 * p + 1) * LANES] = jnp.where(low, x, 0.0).astype(BF16)
        qio_ref[:, (2 * p + 1) * LANES:(2 * p + 2) * LANES] = jnp.where(low, pltpu.roll(x, IDX_DIM, 1), 0.0).astype(BF16)
    kw = kw_ref[...]
    ms = jnp.sum(jnp.where(low, kw * kw, 0.0), axis=-1, keepdims=True) * (1.0 / IDX_DIM)
    ki = _rope(kw * lax.rsqrt(ms + EPS) * ig_ref[...], c64, sl64, sh64, half64)
    kif_ref[...] = ki[:, :IDX_DIM]
    kib_ref[...] = jnp.where(low, ki, 0.0).astype(BF16)
    wo_ref[...] = (pltpu.roll(kw, IDX_DIM, 1) * IDX_HEADS ** -0.5) * IDX_DIM ** -0.5


def _rope_tables(pos, head_dim):
    r = head_dim // 4
    half = r // 2
    inv = ROPE_THETA ** (-jnp.arange(half, dtype=F32) * 2.0 / r)
    ang = pos.astype(F32)[:, None] * inv[None, :]
    cos, sin = jnp.cos(ang), jnp.sin(ang)
    n = pos.shape[0]
    zh = jnp.zeros((n, half), F32)
    rest = head_dim - r
    c = jnp.concatenate([cos, cos, jnp.ones((n, rest), F32)], axis=-1)
    s_lo = jnp.concatenate([-sin, zh, jnp.zeros((n, rest), F32)], axis=-1)
    s_hi = jnp.concatenate([zh, sin, jnp.zeros((n, rest), F32)], axis=-1)
    rep = LANES // head_dim
    return tuple(jnp.tile(t, (1, rep)) for t in (c, s_lo, s_hi))


QK_TM = 512


def qk_post(proj, table_pos, table_block, q_gain, k_gain, ik_gain):
    tm = QK_TM
    n_tok = proj.shape[0]
    t128 = _rope_tables(table_pos, HEAD_DIM)
    t64 = _rope_tables(table_pos, IDX_DIM)
    ik_gain128 = jnp.concatenate([ik_gain, jnp.zeros((LANES - IDX_DIM,), F32)])[None, :]

    def col(width, start):
        return pl.BlockSpec((tm, width), lambda i: (i, start // width))

    def row(width):
        return pl.BlockSpec((tm, width), lambda i: (i, 0))

    table = pl.BlockSpec((tm, LANES), lambda i: (table_block(i), 0))
    gain = pl.BlockSpec((1, LANES), lambda i: (0, 0))
    return pl.pallas_call(
        _qk_post_kernel,
        grid=(n_tok // tm,),
        in_specs=[col(ATTN_WIDTH, COL_Q), col(KV_WIDTH, COL_K), col(KV_WIDTH, COL_V), col(IDX_HEADS * IDX_DIM, COL_QI),
                  col(LANES, COL_KIWI)] + [table] * 6 + [gain] * 3,
        out_specs=[row(ATTN_WIDTH), row(KV_WIDTH), row(KV_WIDTH), row(KV_WIDTH), row(N_KV_HEADS * V_AUG), row(IDX_HEADS * LANES),
                   row(IDX_DIM), row(LANES), row(LANES)],
        out_shape=[jax.ShapeDtypeStruct((n_tok, ATTN_WIDTH), BF16),
                   jax.ShapeDtypeStruct((n_tok, KV_WIDTH), F32), jax.ShapeDtypeStruct((n_tok, KV_WIDTH), BF16),
                   jax.ShapeDtypeStruct((n_tok, KV_WIDTH), F32), jax.ShapeDtypeStruct((n_tok, N_KV_HEADS * V_AUG), BF16),
                   jax.ShapeDtypeStruct((n_tok, IDX_HEADS * LANES), BF16),
                   jax.ShapeDtypeStruct((n_tok, IDX_DIM), F32), jax.ShapeDtypeStruct((n_tok, LANES), BF16),
                   jax.ShapeDtypeStruct((n_tok, LANES), F32)],
        compiler_params=_params(("arbitrary",)),
        name="qk_post",
    )(proj, proj, proj, proj, proj, *t128, *t64, q_gain[None, :], k_gain[None, :], ik_gain128)


def _gelu_tanh(x):
    return 0.5 * x * (1.0 + jnp.tanh(np.float32(np.sqrt(2.0 / np.pi)) * (x + 0.044715 * (x * x * x))))


SSM_LT = SSM_SB // LANES
SSM_SEG = 64


def _ssm_kernel(u_ref, wb_ref, wc_ref, pw_ref, d_ref, h0_ref, g_ref, sre_ref, sim_ref,
                er_ref, ei_ref, car_ref, up_ref, yp_ref):
    c = pl.program_id(2)

    @pl.when(c == 0)
    def _():
        car_ref[...] = h0_ref[...]

    for j in range(SSM_SEG):
        up_ref[j * SUBLANES:(j + 1) * SUBLANES, :] = u_ref[pl.ds(j, SUBLANES, stride=SSM_SEG), :]
    e = _dot(up_ref[...].astype(BF16), wb_ref[...])
    tiles = [slice(lt * LANES, (lt + 1) * LANES) for lt in range(SSM_LT)]
    for lt, sl in enumerate(tiles):
        er_ref[lt] = e[:, sl]
        ei_ref[lt] = e[:, SSM_SB + lt * LANES:SSM_SB + (lt + 1) * LANES]

    def cmul_add(ar, ai, br, bi, cr, ci):
        return ar * br - ai * bi + cr, ar * bi + ai * br + ci

    lb = [(pw_ref[0, 0:1, sl], pw_ref[1, 0:1, sl]) for sl in tiles]
    zero = jnp.zeros((SUBLANES, LANES), F32)
    st = [(zero, zero)] * SSM_LT
    for j in range(SSM_SEG):
        rows = slice(j * SUBLANES, (j + 1) * SUBLANES)
        for lt in range(SSM_LT):
            st[lt] = cmul_add(*lb[lt], *st[lt], er_ref[lt, rows, :], ei_ref[lt, rows, :])
            er_ref[lt, rows, :] = st[lt][0]
            ei_ref[lt, rows, :] = st[lt][1]

    enter = []
    for lt, sl in enumerate(tiles):
        seg_r, seg_i = pw_ref[0, SSM_SEG - 1:SSM_SEG, sl], pw_ref[1, SSM_SEG - 1:SSM_SEG, sl]
        cr, ci = car_ref[0:1, sl], car_ref[1:2, sl]
        rows_r, rows_i = [], []
        for r in range(SUBLANES):
            rows_r.append(cr)
            rows_i.append(ci)
            cr, ci = cmul_add(seg_r, seg_i, cr, ci, st[lt][0][r:r + 1], st[lt][1][r:r + 1])
        car_ref[0:1, sl] = cr
        car_ref[1:2, sl] = ci
        enter.append((jnp.concatenate(rows_r, axis=0), jnp.concatenate(rows_i, axis=0)))

    for j in range(SSM_SEG):
        rows = slice(j * SUBLANES, (j + 1) * SUBLANES)
        for lt, sl in enumerate(tiles):
            xr, xi = cmul_add(pw_ref[0, j:j + 1, sl], pw_ref[1, j:j + 1, sl], *enter[lt],
                              er_ref[lt, rows, :], ei_ref[lt, rows, :])
            er_ref[lt, rows, :] = xr
            ei_ref[lt, rows, :] = xi

    y = None
    for lt, sl in enumerate(tiles):
        t = _dot(er_ref[lt].astype(BF16), wc_ref[0, sl, :]) - _dot(ei_ref[lt].astype(BF16), wc_ref[1, sl, :])
        y = t if y is None else y + t
    yp_ref[...] = y
    out_rows = 2 * SUBLANES
    for t0 in range(0, SUBLANES * SSM_SEG, out_rows):
        r, j0 = divmod(t0, SSM_SEG)
        rows = slice(t0, t0 + out_rows)
        yt = yp_ref[pl.ds(j0 * SUBLANES + r, out_rows, stride=SUBLANES), :] + d_ref[...] * u_ref[rows, :]
        g_ref[rows, :] = _gelu_tanh(yt).astype(BF16)

    @pl.when(c == pl.num_programs(2) - 1)
    def _():
        sre_ref[...] = car_ref[0:1, :]
        sim_ref[...] = car_ref[1:2, :]


def _ssm_weights(a_re, a_im, log_dt, b_re, b_im, c_re, c_im):
    lam_re, lam_im = a_re, a_im
    dt = jnp.exp(log_dt)[:, None]
    mag = jnp.exp(lam_re * dt)
    lb_re, lb_im = mag * jnp.cos(lam_im * dt), mag * jnp.sin(lam_im * dt)
    den = lam_re * lam_re + lam_im * lam_im
    num_re = lb_re - 1.0
    z_re = (num_re * lam_re + lb_im * lam_im) / den
    z_im = (lb_im * lam_re - num_re * lam_im) / den
    zb_re = z_re[:, :, None] * b_re - z_im[:, :, None] * b_im
    zb_im = z_re[:, :, None] * b_im + z_im[:, :, None] * b_re
    eye = jnp.eye(8, dtype=F32)

    def blockdiag_in(w):
        return jnp.einsum('jgph,gk->jghkp', w.reshape(SSM_LB, 8, SSM_STATE, SSM_GROUP), eye).reshape(SSM_LB, LANES, SSM_SB)

    def blockdiag_out(w):
        return jnp.einsum('jghp,gk->jkpgh', w.reshape(SSM_LB, 8, SSM_GROUP, SSM_STATE), eye).reshape(SSM_LB, SSM_SB, LANES)

    wb = jnp.concatenate([blockdiag_in(zb_re), blockdiag_in(zb_im)], axis=-1).astype(BF16)
    wc = jnp.stack([blockdiag_out(c_re), blockdiag_out(c_im)], axis=1).astype(BF16)

    pr, pi_ = lb_re.reshape(SSM_LB, 1, SSM_SB), lb_im.reshape(SSM_LB, 1, SSM_SB)
    while pr.shape[1] < SSM_SEG:
        tr, ti = pr[:, -1:], pi_[:, -1:]
        pr, pi_ = (jnp.concatenate([pr, pr * tr - pi_ * ti], axis=1), jnp.concatenate([pi_, pr * ti + pi_ * tr], axis=1))
    pw = jnp.stack([pr, pi_], axis=1)
    return wb, wc, pw


def ssm(proj, ssm_w, d_skip, h0, *, n_batch, seq, row0):
    wb, wc, pw = ssm_w
    tc = SUBLANES * SSM_SEG
    n_chunks = seq // tc
    blk0 = row0 // tc
    n_tok = n_batch * seq
    state_shape = jax.ShapeDtypeStruct((n_batch, SSM_LB, 1, SSM_SB), F32)
    state_spec = pl.BlockSpec((None, None, 1, SSM_SB), lambda b, j, c: (b, j, 0, 0))
    g, s_re, s_im = pl.pallas_call(
        _ssm_kernel,
        grid=(n_batch, SSM_LB, n_chunks),
        in_specs=[pl.BlockSpec((tc, LANES), lambda b, j, c: (blk0 + b * n_chunks + c, j)),
                  pl.BlockSpec((None, LANES, 2 * SSM_SB), lambda b, j, c: (j, 0, 0)),
                  pl.BlockSpec((None, 2, SSM_SB, LANES), lambda b, j, c: (j, 0, 0, 0)),
                  pl.BlockSpec((None, 2, SSM_SEG, SSM_SB), lambda b, j, c: (j, 0, 0, 0)),
                  pl.BlockSpec((1, LANES), lambda b, j, c: (0, j)),
                  pl.BlockSpec((None, None, 2, SSM_SB), lambda b, j, c: (b, j, 0, 0))],
        out_specs=[pl.BlockSpec((tc, LANES), lambda b, j, c: (b * n_chunks + c, j)), state_spec, state_spec],
        out_shape=[jax.ShapeDtypeStruct((n_tok, SSM_WIDTH), BF16), state_shape, state_shape],
        scratch_shapes=[pltpu.VMEM((SSM_LT, tc, LANES), F32), pltpu.VMEM((SSM_LT, tc, LANES), F32),
                        pltpu.VMEM((2, SSM_SB), F32), pltpu.VMEM((tc, LANES), F32), pltpu.VMEM((tc, LANES), F32)],
        compiler_params=_params(("arbitrary", "arbitrary", "arbitrary")),
        name="ssm",
    )(proj, wb, wc, pw, d_skip[None, :], h0)
    return g, s_re.reshape(n_batch, SSM_GROUPS, SSM_STATE), s_im.reshape(n_batch, SSM_GROUPS, SSM_STATE)


def _ssm_step_kernel(u_ref, wb_ref, wc_ref, pw_ref, d_ref, h0_ref, g_ref, sre_ref, sim_ref, er_ref, ei_ref, *, seq):
    n_seq = h0_ref.shape[1]
    u = u_ref[...]
    e = _dot(u.astype(BF16), wb_ref[...])
    n_lt = SSM_SB // LANES
    y = d_ref[...] * u
    for lt in range(n_lt):
        sl = slice(lt * LANES, (lt + 1) * LANES)
        er_ref[...] = e[:, lt * LANES:(lt + 1) * LANES]
        ei_ref[...] = e[:, SSM_SB + lt * LANES:SSM_SB + (lt + 1) * LANES]
        lr, li = pw_ref[0, 0:1, sl], pw_ref[1, 0:1, sl]
        sr, si = h0_ref[0, :, sl], h0_ref[1, :, sl]
        for t in range(seq):
            rows = pl.ds(t, n_seq, stride=seq)
            sr, si = lr * sr - li * si + er_ref[rows, :], lr * si + li * sr + ei_ref[rows, :]
            er_ref[rows, :] = sr
            ei_ref[rows, :] = si
        y = y + (_dot(er_ref[...].astype(BF16), wc_ref[0, sl, :]) - _dot(ei_ref[...].astype(BF16), wc_ref[1, sl, :]))
        sre_ref[:, sl] = sr
        sim_ref[:, sl] = si
    g_ref[...] = _gelu_tanh(y).astype(BF16)


def ssm_step(proj, ssm_w, d_skip, h0, *, n_batch, seq, row0):
    wb, wc, pw = ssm_w
    n_tok = n_batch * seq
    assert row0 % n_tok == 0
    state_shape = jax.ShapeDtypeStruct((SSM_LB, n_batch, SSM_SB), F32)
    state_spec = pl.BlockSpec((None, n_batch, SSM_SB), lambda j: (j, 0, 0))
    g, s_re, s_im = pl.pallas_call(
        functools.partial(_ssm_step_kernel, seq=seq),
        grid=(SSM_LB,),
        in_specs=[pl.BlockSpec((n_tok, LANES), lambda j: (row0 // n_tok, j)),
                  pl.BlockSpec((None, LANES, 2 * SSM_SB), lambda j: (j, 0, 0)),
                  pl.BlockSpec((None, 2, SSM_SB, LANES), lambda j: (j, 0, 0, 0)),
                  pl.BlockSpec((None, 2, SSM_SEG, SSM_SB), lambda j: (j, 0, 0, 0)),
                  pl.BlockSpec((1, LANES), lambda j: (0, j)),
                  pl.BlockSpec((None, 2, n_batch, SSM_SB), lambda j: (j, 0, 0, 0))],
        out_specs=[pl.BlockSpec((n_tok, LANES), lambda j: (0, j)), state_spec, state_spec],
        out_shape=[jax.ShapeDtypeStruct((n_tok, SSM_WIDTH), BF16), state_shape, state_shape],
        scratch_shapes=[pltpu.VMEM((n_tok, LANES), F32), pltpu.VMEM((n_tok, LANES), F32)],
        compiler_params=_params(("arbitrary",)),
        name="ssm_step",
    )(proj, wb, wc, pw, d_skip[None, :], h0)

    def per_seq(s):
        return s.transpose(1, 0, 2).reshape(n_batch, SSM_GROUPS, SSM_STATE)

    return g, per_seq(s_re), per_seq(s_im)


def _row_sum(x):
    return jnp.sum(x, axis=1, keepdims=True)


def _row_count(mask):
    return _row_sum(jnp.where(mask, 1, 0))


I16 = jnp.int16
I16_MIN = -2 ** 15


def _count16(ref, cand, compare):
    accs = [None] * 4
    for t in range(ref.shape[1] // LANES):
        x = jnp.where(compare(ref[:, t * LANES:(t + 1) * LANES], cand), I16(1), I16(0))
        accs[t % 4] = x if accs[t % 4] is None else accs[t % 4] + x
    accs = [a for a in accs if a is not None]
    total = accs[0]
    for a in accs[1:]:
        total = total + a
    return _row_sum(total.astype(I32))


def _bisect16(ref, target):
    def step(i, base):
        cand = base + lax.shift_left(np.int32(1), np.int32(15) - i)
        cnt = _count16(ref, cand.astype(I16), lambda a, b: a >= b)
        return jnp.where(cnt >= target, cand, base)
    return lax.fori_loop(0, 16, step, jnp.full((ref.shape[0], 1), I16_MIN, I32))


def _bisect32(key_ref, n_sel):
    bq, n_keys = key_ref.shape
    hr = bq // 2

    def lane_counts(h, cand):
        accs = [None] * 4
        for t in range(n_keys // LANES):
            x = jnp.where(key_ref[h * hr:(h + 1) * hr, t * LANES:(t + 1) * LANES] >= cand, 1, 0)
            accs[t % 4] = x if accs[t % 4] is None else accs[t % 4] + x
        accs = [a for a in accs if a is not None]
        total = accs[0]
        for a in accs[1:]:
            total = total + a
        return total

    def decide(part, cand, base):
        return jnp.where(_row_sum(part) >= n_sel, cand, base)

    def bit(i):
        return lax.shift_left(np.int32(1), np.int32(31) - i)

    def body(i, state):
        base_a, base_b, part_b = state
        cand_a = base_a + bit(i)
        part_a = lane_counts(0, cand_a)
        base_b = decide(part_b, base_b + bit(i - 1), base_b)
        part_b = lane_counts(1, base_b + bit(i))
        return decide(part_a, cand_a, base_a), base_b, part_b

    base0 = jnp.full((hr, 1), INT_MIN, I32)
    first = base0 + bit(0)
    state = (decide(lane_counts(0, first), first, base0), base0, lane_counts(1, first))
    base_a, base_b, part_b = lax.fori_loop(1, 32, body, state)
    base_b = decide(part_b, base_b + bit(31), base_b)
    return jnp.concatenate([base_a, base_b], axis=0)


def _stack_heads(ref, heads):
    return jnp.concatenate([ref[:, h * LANES:(h + 1) * LANES] for h in heads], axis=0)


def _dsa_body(q_ref, qi_ref, wi_ref, k_ref, v_ref, ki_ref, o_ref, key_ref, bias_ref, hi_ref, lo_ref,
              *, q_pos_first, s_valid, n_sel, packed_bisect, stack):
    bq, n_keys = key_ref.shape
    col = lax.broadcasted_iota(I32, (bq, n_keys), 1)
    qpos = q_pos_first + lax.broadcasted_iota(I32, (bq, 1), 0)
    allowed = col < jnp.minimum((qpos // CHUNK + 1) * CHUNK, s_valid)

    ki = ki_ref[...]
    score = None
    for h0 in range(0, IDX_HEADS, stack):
        d = _dot_nt(_stack_heads(qi_ref, range(h0, h0 + stack)), ki)
        for j in range(stack):
            t = jnp.maximum(d[j * bq:(j + 1) * bq], 0.0) * wi_ref[:, h0 + j:h0 + j + 1]
            score = t if score is None else score + t
    score = jnp.where(score == 0.0, 0.0, score)
    bits = pltpu.bitcast(score, I32)
    key = jnp.where(bits < 0, bits ^ np.int32(0x7FFFFFFF), bits)
    key = jnp.where(allowed, key, KEY_NEG_INF)
    key_ref[...] = key

    if packed_bisect:
        hi_ref[...] = (key >> 16).astype(I16)
        lo_ref[...] = ((key & 0xFFFF) + I16_MIN).astype(I16)
        thr_hi = _bisect16(hi_ref, n_sel)
        thr_hi16 = thr_hi.astype(I16)
        need_lo = n_sel - _count16(hi_ref, thr_hi16, lambda a, b: a > b)
        lo_ref[...] = jnp.where(hi_ref[...] == thr_hi16, lo_ref[...], I16(I16_MIN))
        thr_lo = _bisect16(lo_ref, need_lo)
        thr = lax.shift_left(thr_hi, np.int32(16)) + (thr_lo - I16_MIN)
    else:
        thr = _bisect32(key_ref, n_sel)
    thr = jnp.maximum(thr, KEY_NEG_INF)

    key = key_ref[...]
    need = n_sel - _row_count(key > thr)
    n_eq = _row_count(key == thr)
    n_bits = int(n_keys - 1).bit_length()

    def tie_cut():
        def step(i, j0):
            cand = j0 + lax.shift_left(np.int32(1), np.int32(n_bits - 1) - i)
            cnt = _row_sum(jnp.where(key_ref[...] == thr, jnp.where(col < cand, 1, 0), 0))
            return jnp.where(cnt < need, cand, j0)
        return lax.fori_loop(0, n_bits, step, jnp.zeros((bq, 1), I32))

    split = jnp.max(jnp.where(n_eq > need, 1, 0)) > 0
    j_last = lax.cond(split, tie_cut, lambda: jnp.full((bq, 1), n_keys, I32))
    tie_bias = jnp.where(thr == KEY_NEG_INF, -jnp.inf, 0.0)
    bias_ref[...] = jnp.where(key > thr, 0.0,
                              jnp.where(key == thr, jnp.where(col <= j_last, tie_bias, -jnp.inf), -jnp.inf))

    c = np.float32(HEAD_DIM ** -0.5 * np.log2(np.e))
    for h0 in range(0, N_HEADS, stack):
        kv = h0 // KV_GROUP
        heads = range(h0, h0 + stack)
        s_all = _dot_nt(_stack_heads(q_ref, heads), k_ref[:, kv * HEAD_DIM:(kv + 1) * HEAD_DIM])
        ps = []
        for g in range(stack):
            s = s_all[g * bq:(g + 1) * bq] + bias_ref[...]
            m = jnp.max(s, axis=1, keepdims=True)
            ps.append(jnp.exp2((s - m) * c).astype(BF16))
        pv = _dot(jnp.concatenate(ps, axis=0), v_ref[:, kv * V_AUG:(kv + 1) * V_AUG])
        for g, h in enumerate(heads):
            o = pv[g * bq:(g + 1) * bq]
            o_ref[:, h * HEAD_DIM:(h + 1) * HEAD_DIM] = (o[:, :HEAD_DIM] / o[:, HEAD_DIM:HEAD_DIM + 1]).astype(BF16)


def _dsa_scratch(bq, n_keys):
    return [pltpu.VMEM((bq, n_keys), I32), pltpu.VMEM((bq, n_keys), F32),
            pltpu.VMEM((bq, n_keys), I16), pltpu.VMEM((bq, n_keys), I16)]


def _dsa_kernel(q_ref, qi_ref, wi_ref, k_ref, v_ref, ki_ref, o_ref, *scratch, q_pos0, **static):
    bq = scratch[0].shape[0]
    _dsa_body(q_ref, qi_ref, wi_ref, k_ref, v_ref, ki_ref, o_ref, *scratch,
              q_pos_first=q_pos0 + pl.program_id(1) * bq, **static)


def dsa(q, qi, wi, k, v, ki, *, bq, q_blk0, n_qblk, n_keys, n_sel, packed_bisect, stack):
    n_batch, seq = q.shape[:2]

    def qspec(width):
        return pl.BlockSpec((None, bq, width), lambda b, i: (b, q_blk0 + i, 0))

    def kspec(width):
        return pl.BlockSpec((None, n_keys, width), lambda b, i: (b, 0, 0))

    return pl.pallas_call(
        functools.partial(_dsa_kernel, q_pos0=q_blk0 * bq, s_valid=seq, n_sel=n_sel, packed_bisect=packed_bisect,
                          stack=stack),
        grid=(n_batch, n_qblk),
        in_specs=[qspec(ATTN_WIDTH), qspec(IDX_HEADS * LANES), qspec(LANES), kspec(KV_WIDTH), kspec(N_KV_HEADS * V_AUG),
                  kspec(LANES)],
        out_specs=pl.BlockSpec((None, bq, ATTN_WIDTH), lambda b, i: (b, i, 0)),
        out_shape=jax.ShapeDtypeStruct((n_batch, n_qblk * bq, ATTN_WIDTH), BF16),
        scratch_shapes=_dsa_scratch(bq, n_keys),
        compiler_params=_params(("arbitrary", "arbitrary")),
        name="dsa",
    )(q, qi, wi, k, v, ki)


def _dsa_step_kernel(q_ref, qi_ref, wi_ref, ck_hbm, cv_hbm, cki_ref, nk_ref, nv_ref, nki_ref, o_ref,
                     k_buf, v_buf, ki_buf, cache_buf, sem, *scratch, past, n_sel):
    b = pl.program_id(0)

    def cache_copies(seq, slot):
        return [pltpu.make_async_copy(src.at[seq, :, h, :], cache_buf.at[slot, a, h], sem.at[slot])
                for a, src in enumerate((ck_hbm, cv_hbm)) for h in range(N_KV_HEADS)]

    @pl.when(b == 0)
    def _():
        for cp in cache_copies(0, 0):
            cp.start()

    @pl.when(b + 1 < pl.num_programs(0))
    def _():
        for cp in cache_copies(b + 1, (b + 1) % 2):
            cp.start()

    slot = b % 2
    for cp in cache_copies(b, slot):
        cp.wait()

    ts = nk_ref.shape[0]
    n_keys = k_buf.shape[0]
    for h in range(N_KV_HEADS):
        k_buf[0:past, h * HEAD_DIM:(h + 1) * HEAD_DIM] = cache_buf[slot, 0, h].astype(BF16)
    _store_v_aug(v_buf, 0, [cache_buf[slot, 1, h] for h in range(N_KV_HEADS)])
    for buf, new in ((k_buf, nk_ref), (v_buf, nv_ref)):
        buf[past:past + ts, :] = new[...]
        buf[past + ts:n_keys, :] = jnp.zeros((n_keys - past - ts, buf.shape[1]), BF16)
    ki_buf[0:past, 0:IDX_DIM] = cki_ref[...].astype(BF16)
    ki_buf[0:past, IDX_DIM:LANES] = jnp.zeros((past, LANES - IDX_DIM), BF16)
    ki_buf[past:past + ts, :] = nki_ref[...]
    ki_buf[past + ts:n_keys, :] = jnp.zeros((n_keys - past - ts, LANES), BF16)
    _dsa_body(q_ref, qi_ref, wi_ref, k_buf, v_buf, ki_buf, o_ref, *scratch,
              q_pos_first=past, s_valid=past + ts, n_sel=n_sel, packed_bisect=True, stack=KV_GROUP)


def dsa_step(q, qi, wi, cache_k, cache_v, cache_ki, k_new, v_new, ki_new, *, n_sel):
    n_batch, ts = q.shape[:2]
    past = cache_k.shape[1]
    n_keys = -(-(past + ts) // LANES) * LANES

    def spec(rows, width):
        return pl.BlockSpec((None, rows, width), lambda b: (b, 0, 0))

    return pl.pallas_call(
        functools.partial(_dsa_step_kernel, past=past, n_sel=n_sel),
        grid=(n_batch,),
        in_specs=[spec(ts, ATTN_WIDTH), spec(ts, IDX_HEADS * LANES), spec(ts, LANES),
                  pl.BlockSpec(memory_space=pl.ANY), pl.BlockSpec(memory_space=pl.ANY), spec(past, IDX_DIM),
                  spec(ts, KV_WIDTH), spec(ts, N_KV_HEADS * V_AUG), spec(ts, LANES)],
        out_specs=spec(ts, ATTN_WIDTH),
        out_shape=jax.ShapeDtypeStruct((n_batch, ts, ATTN_WIDTH), BF16),
        scratch_shapes=[pltpu.VMEM((n_keys, KV_WIDTH), BF16), pltpu.VMEM((n_keys, N_KV_HEADS * V_AUG), BF16),
                        pltpu.VMEM((n_keys, LANES), BF16),
                        pltpu.VMEM((2, 2, N_KV_HEADS, past, HEAD_DIM), F32), pltpu.SemaphoreType.DMA((2,)),
                        *_dsa_scratch(ts, n_keys)],
        compiler_params=_params(("arbitrary",)),
        name="dsa_step",
    )(q, qi, wi, cache_k, cache_v, cache_ki, k_new, v_new, ki_new)


def _merge_kernel(g_ref, a_ref, ga_ref, gb_ref, wv_ref, wg_ref, wb_ref, o_ref):
    g = g_ref[...]
    branch_a = _dot(g, wv_ref[...]) * jax.nn.sigmoid(_dot(g, wg_ref[...]))
    branch_b = _dot(a_ref[...], wb_ref[...])
    merged = jax.nn.sigmoid(ga_ref[...]) * branch_a + jax.nn.sigmoid(gb_ref[...]) * branch_b
    o_ref[...] = merged.astype(BF16)


def merge(g, attn, proj, w_val, w_gate, w_branch, *, tm=1024, tn=512):
    n_tok = g.shape[0]
    nj = D_MODEL // tn

    def wspec():
        return pl.BlockSpec((SSM_WIDTH, tn), lambda i, j: (0, j))

    return pl.pallas_call(
        _merge_kernel,
        grid=(n_tok // tm, nj),
        in_specs=[pl.BlockSpec((tm, SSM_WIDTH), lambda i, j: (i, 0)),
                  pl.BlockSpec((tm, ATTN_WIDTH), lambda i, j: (i, 0)),
                  pl.BlockSpec((tm, tn), lambda i, j: (i, COL_GA // tn + j)),
                  pl.BlockSpec((tm, tn), lambda i, j: (i, COL_GB // tn + j)),
                  wspec(), wspec(), wspec()],
        out_specs=pl.BlockSpec((tm, tn), lambda i, j: (i, j)),
        out_shape=jax.ShapeDtypeStruct((n_tok, D_MODEL), BF16),
        compiler_params=_params(("arbitrary", "arbitrary")),
        name="merge",
    )(g, attn, proj, proj, w_val, w_gate, w_branch)


ROUTER_COLS = N_EXPERT_GROUPS + N_EXPERTS
MOE_TM = 256


def _first_lane_of_max(x, lane_f):
    m = jnp.max(x, axis=1, keepdims=True)
    return m, jnp.min(jnp.where(x == m, lane_f, float(LANES)), axis=1, keepdims=True)


def _out_proj_kernel(x_ref, m_ref, wo_ref, gn_ref, wrh_ref, wrl_ref, br_ref, cin_ref,
                     h_ref, hn_ref, ri_ref, rw_ref, cnt_ref, carry_ref):
    @pl.when(pl.program_id(0) == 0)
    def _():
        carry_ref[...] = cin_ref[...]

    h = x_ref[...] + _dot(m_ref[...], wo_ref[...])
    h_ref[...] = h
    ms = jnp.mean(h * h, axis=-1, keepdims=True)
    hn = h * lax.rsqrt(ms + EPS) * gn_ref[...]
    hn_ref[...] = hn
    hh, hl = _split_bf16(hn)
    wrh = wrh_ref[...]
    lg = _dot(hh, wrh) + _dot(hl, wrh) + _dot(hh, wrl_ref[...]) + br_ref[...]

    tm = lg.shape[0]
    lane = lax.broadcasted_iota(I32, lg.shape, 1)
    lane_f = lane.astype(F32)
    ninf = -jnp.inf
    gl = jnp.where(lane < N_EXPERT_GROUPS, lg, ninf)
    gmax, gsel = _first_lane_of_max(gl, lane_f)
    g_w = 1.0 / jnp.sum(jnp.exp(gl - gmax), axis=1, keepdims=True)
    lo = N_EXPERT_GROUPS + EXPERTS_PER_GROUP * gsel
    el = jnp.where(lane_f >= lo, jnp.where(lane_f < lo + EXPERTS_PER_GROUP, lg, ninf), ninf)
    v1, i1 = _first_lane_of_max(el, lane_f)
    el2 = jnp.where(lane_f == i1, ninf, el)
    v2, i2 = _first_lane_of_max(el2, lane_f)
    t = jnp.exp(v2 - v1)
    s1 = 1.0 / (1.0 + t)
    w1 = s1 * g_w
    w2 = (t * s1) * g_w

    m1 = jnp.where(lane_f == i1, 1.0, 0.0)
    m2 = jnp.where(lane_f == i2, 1.0, 0.0)
    both = m1 + m2
    tri = jnp.where(lax.broadcasted_iota(I32, (tm, tm), 0) > lax.broadcasted_iota(I32, (tm, tm), 1), 1.0, 0.0)
    before = _dot(tri.astype(BF16), both.astype(BF16)) + carry_ref[...]
    r1 = jnp.sum(before * m1, axis=1, keepdims=True)
    r2 = jnp.sum(before * m2, axis=1, keepdims=True)
    carry_ref[...] = carry_ref[...] + jnp.sum(both, axis=0, keepdims=True)
    cnt_ref[...] = carry_ref[...]
    e1 = i1 - float(N_EXPERT_GROUPS)
    e2 = i2 - float(N_EXPERT_GROUPS)
    fields = jnp.where(lane == 0, e1, jnp.where(lane == 1, e2, jnp.where(lane == 2, r1, jnp.where(lane == 3, r2, 0.0))))
    ri_ref[...] = fields.T[0:SUBLANES, :].astype(I32)
    rw_ref[...] = jnp.where(lane == 0, w1, jnp.where(lane == 1, w2, 0.0))


def _router_weights(w_router_group, b_router_group, w_router_expert, b_router_expert):
    wr = jnp.concatenate([w_router_group, w_router_expert, jnp.zeros((D_MODEL, LANES - ROUTER_COLS), F32)], axis=1)
    wr_hi = wr.astype(BF16)
    wr_lo = (wr - wr_hi.astype(F32)).astype(BF16)
    br = jnp.concatenate([b_router_group, b_router_expert, jnp.zeros((LANES - ROUTER_COLS,), F32)])[None, :]
    return wr_hi, wr_lo, br


def out_proj(x, merged, w_out, ffn_gain, router_w, counts_in, *, tm=256):
    n_tok = x.shape[0]
    wr_hi, wr_lo, br = router_w

    def row(width):
        return pl.BlockSpec((tm, width), lambda i: (i, 0))

    def const(shape):
        return pl.BlockSpec(shape, lambda i: (0, 0), pipeline_mode=pl.Buffered(1))

    return pl.pallas_call(
        _out_proj_kernel,
        grid=(n_tok // tm,),
        in_specs=[row(D_MODEL), row(D_MODEL), const((D_MODEL, D_MODEL)), const((1, D_MODEL)),
                  const((D_MODEL, LANES)), const((D_MODEL, LANES)), const((1, LANES)), const((1, LANES))],
        out_specs=[row(D_MODEL), row(D_MODEL), pl.BlockSpec((SUBLANES, tm), lambda i: (0, i)), row(LANES),
                   pl.BlockSpec((1, LANES), lambda i: (0, 0))],
        out_shape=[jax.ShapeDtypeStruct((n_tok, D_MODEL), F32), jax.ShapeDtypeStruct((n_tok, D_MODEL), F32),
                   jax.ShapeDtypeStruct((SUBLANES, n_tok), I32), jax.ShapeDtypeStruct((n_tok, LANES), F32),
                   jax.ShapeDtypeStruct((1, LANES), F32)],
        scratch_shapes=[pltpu.VMEM((1, LANES), F32)],
        compiler_params=_params(("arbitrary",)),
        name="out_proj",
    )(x, merged, w_out, ffn_gain[None, :], wr_hi, wr_lo, br, counts_in)


def _block_layout(counts):
    padded = (counts + MOE_TM - 1) // MOE_TM * MOE_TM
    pad_end = jnp.cumsum(padded).astype(I32)
    pad_start = pad_end - padded
    n_used = pad_end[-1] // MOE_TM
    return pad_start, pad_end, n_used


def _moe_rows(n_tok):
    return -(-(n_tok * TOP_K + N_EXPERTS * (MOE_TM - 1)) // MOE_TM) * MOE_TM


DISPATCH_TM = 512


def _wait_rows(src_hbm, dst, sem, n_rows):
    pltpu.make_async_copy(src_hbm.at[pl.ds(0, n_rows)], dst, sem).wait()


def _dispatch_kernel(d0_ref, d1_ref, pe_ref, cnt_ref, nu_ref, hna_ref, hnb_ref, xs_hbm, zbuf, sem, semz,
                     *, n_blocks, a_tiles):
    i = pl.program_id(0)

    def zero_block(row0):
        return pltpu.make_async_copy(zbuf, xs_hbm.at[pl.ds(pl.multiple_of(row0, MOE_TM), MOE_TM)], semz)

    @pl.when(i == 0)
    def _():
        zbuf[...] = jnp.zeros_like(zbuf)
        for start in (True, False):
            for e in range(N_EXPERTS):
                @pl.when(cnt_ref[e] > 0)
                def _():
                    cp = zero_block(pe_ref[e] - MOE_TM)
                    cp.start() if start else cp.wait()

            def tail(b, c):
                cp = zero_block(b * MOE_TM)
                cp.start() if start else cp.wait()
                return c
            lax.fori_loop(nu_ref[0], n_blocks, tail, 0)

    base = i * DISPATCH_TM

    def scatter(hn_ref):
        def body(r, c):
            src = hn_ref.at[pl.ds(r, 1)]
            pltpu.make_async_copy(src, xs_hbm.at[pl.ds(d0_ref[base + r], 1)], sem).start()
            pltpu.make_async_copy(src, xs_hbm.at[pl.ds(d1_ref[base + r], 1)], sem).start()
            return c
        lax.fori_loop(0, DISPATCH_TM, body, 0, unroll=8)
        for _ in range(TOP_K):
            pltpu.make_async_copy(hn_ref, xs_hbm.at[pl.ds(0, DISPATCH_TM)], sem).wait()

    @pl.when(i < a_tiles)
    def _():
        scatter(hna_ref)

    @pl.when(i >= a_tiles)
    def _():
        scatter(hnb_ref)


def dispatch(hn_a, hn_b, dest0, dest1, pad_end, counts, n_used):
    a_tiles, b_tiles = hn_a.shape[0] // DISPATCH_TM, hn_b.shape[0] // DISPATCH_TM
    rows = _moe_rows(hn_a.shape[0] + hn_b.shape[0])
    grid_spec = pltpu.PrefetchScalarGridSpec(
        num_scalar_prefetch=5,
        grid=(a_tiles + b_tiles,),
        in_specs=[pl.BlockSpec((DISPATCH_TM, D_MODEL), lambda i, *_: (jnp.minimum(i, a_tiles - 1), 0)),
                  pl.BlockSpec((DISPATCH_TM, D_MODEL), lambda i, *_: (jnp.maximum(i - a_tiles, 0), 0))],
        out_specs=pl.BlockSpec(memory_space=pl.ANY),
        scratch_shapes=[pltpu.VMEM((MOE_TM, D_MODEL), F32), pltpu.SemaphoreType.DMA(()), pltpu.SemaphoreType.DMA(())],
    )
    return pl.pallas_call(
        functools.partial(_dispatch_kernel, n_blocks=rows // MOE_TM, a_tiles=a_tiles),
        grid_spec=grid_spec,
        out_shape=jax.ShapeDtypeStruct((rows, D_MODEL), F32),
        compiler_params=_params(("arbitrary",)),
        name="dispatch",
    )(dest0, dest1, pad_end, counts, n_used, hn_a, hn_b)


MOE_UNITS = 8
MOE_UG = D_MODEL // MOE_UNITS
MOE_UD = EXPERT_FF // MOE_UNITS


def _moe_kernel(blk_e_ref, nu_ref, nxt_ref, upb_ref, xs_ref, wg_hbm, wu_hbm, wd_hbm, ys_ref,
                wg_bf, wu_bf, wd_bf, stg_g, stg_u, stg_d, sem, st_ref):
    i = pl.program_id(0)
    cur_slot, pos, cur_e = 0, 1, 2

    def unit_copies(e, unit, s):
        g_rows = pl.ds(pl.multiple_of(unit * MOE_UG, MOE_UG), MOE_UG)
        d_rows = pl.ds(pl.multiple_of(unit * MOE_UD, MOE_UD), MOE_UD)
        return (pltpu.make_async_copy(wg_hbm.at[e, g_rows, :], stg_g.at[s], sem.at[s]),
                pltpu.make_async_copy(wu_hbm.at[e, g_rows, :], stg_u.at[s], sem.at[s]),
                pltpu.make_async_copy(wd_hbm.at[e, d_rows, :], stg_d.at[s], sem.at[s]))

    def start_unit(e, unit):
        for cp in unit_copies(e, unit, unit % 2):
            cp.start()

    def begin_load(e):
        st_ref[pos] = 0
        start_unit(e, 0)
        start_unit(e, 1)

    def advance(e, slot, n):
        def body(_, c):
            unit = st_ref[pos]

            @pl.when(unit < MOE_UNITS)
            def _():
                s = unit % 2
                for cp in unit_copies(e, unit, s):
                    cp.wait()
                g_rows = pl.ds(pl.multiple_of(unit * MOE_UG, MOE_UG), MOE_UG)
                d_rows = pl.ds(pl.multiple_of(unit * MOE_UD, MOE_UD), MOE_UD)
                wg_bf[slot, g_rows, :] = stg_g[s].astype(BF16)
                wu_bf[slot, g_rows, :] = stg_u[s].astype(BF16)
                wd_bf[slot, d_rows, :] = stg_d[s].astype(BF16)

                @pl.when(unit + 2 < MOE_UNITS)
                def _():
                    start_unit(e, unit + 2)
                st_ref[pos] = unit + 1
            return c
        lax.fori_loop(0, n, body, 0)

    def load_next(nxt):
        @pl.when(nxt >= 0)
        def _():
            begin_load(nxt)

        @pl.when(nxt < 0)
        def _():
            st_ref[pos] = MOE_UNITS

    @pl.when(i < nu_ref[0])
    def _():
        e = blk_e_ref[i]
        nxt = nxt_ref[i]

        @pl.when(i == 0)
        def _():
            st_ref[cur_slot] = 0
            st_ref[cur_e] = e
            begin_load(e)
            advance(e, 0, MOE_UNITS)
            load_next(nxt)

        @pl.when(jnp.logical_and(i > 0, e != st_ref[cur_e]))
        def _():
            slot = 1 - st_ref[cur_slot]
            advance(e, slot, MOE_UNITS)
            st_ref[cur_slot] = slot
            st_ref[cur_e] = e
            load_next(nxt)

        slot = st_ref[cur_slot]
        x = xs_ref[...].astype(BF16)
        hg = _dot(x, wg_bf[slot])
        hu = _dot(x, wu_bf[slot])
        hmid = (jax.nn.silu(hg) * hu).astype(BF16)
        ys_ref[...] = _dot(hmid, wd_bf[slot])

        @pl.when(nxt >= 0)
        def _():
            advance(nxt, 1 - slot, upb_ref[i])

    @pl.when(i >= nu_ref[0])
    def _():
        ys_ref[...] = jnp.zeros_like(ys_ref)


def moe(xs, blk_e, n_used, nxt_e, units_per_block, w_gate, w_up, w_down):
    rows = xs.shape[0]
    grid_spec = pltpu.PrefetchScalarGridSpec(
        num_scalar_prefetch=4,
        grid=(rows // MOE_TM,),
        in_specs=[pl.BlockSpec((MOE_TM, D_MODEL), lambda i, be, nu, nx, ub: (jnp.minimum(i, nu[0] - 1), 0)),
                  pl.BlockSpec(memory_space=pl.ANY), pl.BlockSpec(memory_space=pl.ANY), pl.BlockSpec(memory_space=pl.ANY)],
        out_specs=pl.BlockSpec((MOE_TM, D_MODEL), lambda i, be, nu, nx, ub: (i, 0)),
        scratch_shapes=[pltpu.VMEM((2, D_MODEL, EXPERT_FF), BF16), pltpu.VMEM((2, D_MODEL, EXPERT_FF), BF16),
                        pltpu.VMEM((2, EXPERT_FF, D_MODEL), BF16),
                        pltpu.VMEM((2, MOE_UG, EXPERT_FF), F32), pltpu.VMEM((2, MOE_UG, EXPERT_FF), F32),
                        pltpu.VMEM((2, MOE_UD, D_MODEL), F32),
                        pltpu.SemaphoreType.DMA((2,)), pltpu.SMEM((3,), I32)],
    )
    return pl.pallas_call(
        _moe_kernel,
        grid_spec=grid_spec,
        out_shape=jax.ShapeDtypeStruct((rows, D_MODEL), F32),
        compiler_params=_params(("arbitrary",)),
        name="moe",
    )(blk_e, n_used, nxt_e, units_per_block, xs, w_gate, w_up, w_down)


def _gather_rows(idx_ref, idx0, src_hbm, dst, sem, n_rows):
    def body(r, carry):
        t = idx_ref[idx0 + r]
        pltpu.make_async_copy(src_hbm.at[pl.ds(t, 1)], dst.at[pl.ds(r, 1)], sem).start()
        return carry
    lax.fori_loop(0, n_rows, body, 0, unroll=8)


def _combine_kernel(r0_ref, r1_ref, ys_hbm, h_ref, w_ref, o_ref, buf, sem, *, tm, tok0):
    i = pl.program_id(0)

    def issue(block, slot):
        _gather_rows(r0_ref, tok0 + block * tm, ys_hbm, buf.at[slot, 0], sem.at[slot], tm)
        _gather_rows(r1_ref, tok0 + block * tm, ys_hbm, buf.at[slot, 1], sem.at[slot], tm)

    @pl.when(i == 0)
    def _():
        issue(0, 0)

    @pl.when(i + 1 < pl.num_programs(0))
    def _():
        issue(i + 1, (i + 1) % 2)

    slot = i % 2
    _wait_rows(ys_hbm, buf.at[slot, 0], sem.at[slot], tm)
    _wait_rows(ys_hbm, buf.at[slot, 1], sem.at[slot], tm)
    w = w_ref[...]
    o_ref[...] = h_ref[...] + (buf[slot, 0] * w[:, 0:1] + buf[slot, 1] * w[:, 1:2])


def combine(ys, h, route_w, rows0, rows1, *, tok0, tm=256):
    n_tok = h.shape[0]
    grid_spec = pltpu.PrefetchScalarGridSpec(
        num_scalar_prefetch=2,
        grid=(n_tok // tm,),
        in_specs=[pl.BlockSpec(memory_space=pl.ANY),
                  pl.BlockSpec((tm, D_MODEL), lambda i, a, b: (i, 0)),
                  pl.BlockSpec((tm, LANES), lambda i, a, b: (i, 0))],
        out_specs=pl.BlockSpec((tm, D_MODEL), lambda i, a, b: (i, 0)),
        scratch_shapes=[pltpu.VMEM((2, 2, tm, D_MODEL), F32), pltpu.SemaphoreType.DMA((2,))],
    )
    return pl.pallas_call(
        functools.partial(_combine_kernel, tm=tm, tok0=tok0),
        grid_spec=grid_spec,
        out_shape=jax.ShapeDtypeStruct((n_tok, D_MODEL), F32),
        compiler_params=_params(("arbitrary",)),
        name="combine",
    )(rows0, rows1, ys, h, route_w)


def _regroup_w_in(w_in):
    sizes = (SSM_WIDTH, ATTN_WIDTH, KV_WIDTH, KV_WIDTH, IDX_HEADS * IDX_DIM, IDX_DIM, IDX_HEADS, D_MODEL, D_MODEL)
    u, q, k, v, qi, ki, wi, ga, gb = jnp.split(w_in, np.cumsum(sizes)[:-1].tolist(), axis=1)
    pad = jnp.zeros((D_MODEL, PROJ_COLS - COL_KIWI - IDX_DIM - IDX_HEADS), F32)
    return jnp.concatenate([u, q, ga, gb, k, v, qi, ki, wi, pad], axis=1).astype(BF16)


def _layer(x_p, x_s, cache_k, cache_v, cache_ki, h0_re, h0_im, p):
    bp, tp, _ = x_p.shape
    bs, ts, _ = x_s.shape
    past = cache_k.shape[1]
    n_p, n_s = bp * tp, bs * ts
    n_tok = n_p + n_s

    w_in = _regroup_w_in(p['w_in'])
    ssm_w = _ssm_weights(p['ssm_A_re'], p['ssm_A_im'], p['ssm_log_dt'], p['ssm_B_re'], p['ssm_B_im'],
                         p['ssm_C_re'], p['ssm_C_im'])
    glu_w = (p['w_glu_val'].astype(BF16), p['w_glu_gate'].astype(BF16), p['w_attn_branch'].astype(BF16))
    w_out = p['w_out'].astype(BF16)
    router_w = _router_weights(p['w_router_group'], p['b_router_group'], p['w_router_expert'], p['b_router_expert'])
    seq_tiles = tp // QK_TM

    def front(x, table_pos, table_block):
        proj = in_proj(x, p['norm_mix_g'][None, :], w_in)
        return proj, qk_post(proj, table_pos, table_block, p['q_norm_g'], p['k_norm_g'], p['idx_k_norm_g'])

    def seqs(a, b, t):
        return a.reshape(b, t, a.shape[-1])

    xp = x_p.reshape(n_p, D_MODEL)
    proj_p, (q_b, kf_p, k_b, vf_p, v_b, qi_b, kif_p, ki_b, wi) = front(
        xp, jnp.arange(tp, dtype=I32), lambda i: i % seq_tiles)
    g_p, sre_p, sim_p = ssm(proj_p, ssm_w, p['ssm_D'], jnp.zeros((bp, SSM_LB, 2, SSM_SB), F32),
                            n_batch=bp, seq=tp, row0=0)
    bq = 128
    n_buckets = min(16, tp // bq)
    per = tp // bq // n_buckets
    qp, qip, wip = seqs(q_b, bp, tp), seqs(qi_b, bp, tp), seqs(wi, bp, tp)
    kp, vp, kip = seqs(k_b, bp, tp), seqs(v_b, bp, tp), seqs(ki_b, bp, tp)
    attn_p = jnp.concatenate(
        [dsa(qp, qip, wip, kp, vp, kip, bq=bq, q_blk0=n * per, n_qblk=per, n_keys=(n + 1) * per * bq,
             n_sel=min(IDX_TOPK, tp // 4), packed_bisect=False, stack=1)
         for n in range(n_buckets)], axis=1).reshape(n_p, ATTN_WIDTH)
    merged_p = merge(g_p, attn_p, proj_p, *glu_w)
    h_p, hn_p, ri_p, rw_p, cnt_p = out_proj(xp, merged_p, w_out, p['norm_ffn_g'], router_w, jnp.zeros((1, LANES), F32))

    xs_ = x_s.reshape(n_s, D_MODEL)
    proj_s, (q_b, kf_s, k_b, vf_s, v_b, qi_b, kif_s, ki_b, wi) = front(
        xs_, jnp.tile(past + jnp.arange(ts, dtype=I32), QK_TM // ts), lambda i: 0)
    h0 = jnp.stack([h0_re.reshape(bs, SSM_LB, SSM_SB), h0_im.reshape(bs, SSM_LB, SSM_SB)]).transpose(2, 0, 1, 3)
    g_s, sre_s, sim_s = ssm_step(proj_s, ssm_w, p['ssm_D'], h0, n_batch=bs, seq=ts, row0=0)
    attn_s = dsa_step(seqs(q_b, bs, ts), seqs(qi_b, bs, ts), seqs(wi, bs, ts),
                      cache_k, cache_v, cache_ki,
                      seqs(k_b, bs, ts), seqs(v_b, bs, ts), seqs(ki_b, bs, ts),
                      n_sel=min(IDX_TOPK, (past + ts) // 4)).reshape(n_s, ATTN_WIDTH)
    merged_s = merge(g_s, attn_s, proj_s, *glu_w)
    h_s, hn_s, ri_s, rw_s, cnt = out_proj(xs_, merged_s, w_out, p['norm_ffn_g'], router_w, cnt_p)

    counts = cnt[0, N_EXPERT_GROUPS:ROUTER_COLS].astype(I32)
    pad_start, pad_end, n_used = _block_layout(counts)
    route_i = jnp.concatenate([ri_p, ri_s], axis=1)
    dest0 = pad_start[route_i[0]] + route_i[2]
    dest1 = pad_start[route_i[1]] + route_i[3]
    n_blocks = _moe_rows(n_tok) // MOE_TM
    blk = jnp.minimum(jnp.arange(n_blocks, dtype=I32), n_used - 1)
    blk_e = jnp.minimum(jnp.sum((pad_end[None, :] <= (blk * MOE_TM)[:, None]).astype(I32), axis=1), N_EXPERTS - 1)
    after = pad_end[blk_e] // MOE_TM
    nxt_e = jnp.where(after < n_used, blk_e[jnp.minimum(after, n_blocks - 1)], -1).astype(I32)
    blocks_of_e = jnp.maximum((pad_end - pad_start)[blk_e] // MOE_TM, 1)
    units_per_block = ((MOE_UNITS + blocks_of_e - 1) // blocks_of_e).astype(I32)
    n_used = n_used.reshape(1)

    xs = dispatch(hn_p, hn_s, dest0, dest1, pad_end, counts, n_used)
    ys = moe(xs, blk_e, n_used, nxt_e, units_per_block, p['w_exp_gate'], p['w_exp_up'], p['w_exp_down'])
    y_p = combine(ys, h_p, rw_p, dest0, dest1, tok0=0).reshape(bp, tp, D_MODEL)
    y_s = combine(ys, h_s, rw_s, dest0, dest1, tok0=n_p).reshape(bs, ts, D_MODEL)

    def heads(a, b, t):
        return a.reshape(b, t, N_KV_HEADS, HEAD_DIM)

    new_p = (heads(kf_p, bp, tp), heads(vf_p, bp, tp), kif_p.reshape(bp, tp, IDX_DIM), sre_p, sim_p)
    new_s = (heads(kf_s, bs, ts), heads(vf_s, bs, ts), kif_s.reshape(bs, ts, IDX_DIM), sre_s, sim_s)
    return y_p, y_s, new_p, new_s


def kernel(x_prompt, x_sample, cache_k, cache_v, cache_idx_k, state_ssm_re, state_ssm_im, norm_mix_g, w_in, q_norm_g, k_norm_g, idx_k_norm_g, ssm_A_re, ssm_A_im, ssm_log_dt, ssm_B_re, ssm_B_im, ssm_C_re, ssm_C_im, ssm_D, w_glu_val, w_glu_gate, w_attn_branch, w_out, norm_ffn_g, w_router_group, b_router_group, w_router_expert, b_router_expert, w_exp_gate, w_exp_up, w_exp_down):
    depth = w_in.shape[0]
    assert depth == 1, "prompt and sample tokens are batched through one layer"
    names = ('norm_mix_g', 'w_in', 'q_norm_g', 'k_norm_g', 'idx_k_norm_g', 'ssm_A_re', 'ssm_A_im', 'ssm_log_dt',
             'ssm_B_re', 'ssm_B_im', 'ssm_C_re', 'ssm_C_im', 'ssm_D', 'w_glu_val', 'w_glu_gate', 'w_attn_branch',
             'w_out', 'norm_ffn_g', 'w_router_group', 'b_router_group', 'w_router_expert', 'b_router_expert',
             'w_exp_gate', 'w_exp_up', 'w_exp_down')
    vals = (norm_mix_g, w_in, q_norm_g, k_norm_g, idx_k_norm_g, ssm_A_re, ssm_A_im, ssm_log_dt, ssm_B_re, ssm_B_im,
            ssm_C_re, ssm_C_im, ssm_D, w_glu_val, w_glu_gate, w_attn_branch, w_out, norm_ffn_g, w_router_group,
            b_router_group, w_router_expert, b_router_expert, w_exp_gate, w_exp_up, w_exp_down)
    p = {n: v[0] for n, v in zip(names, vals)}
    y_p, y_s, new_p, new_s = _layer(x_prompt, x_sample, cache_k[0], cache_v[0], cache_idx_k[0],
                                    state_ssm_re[0], state_ssm_im[0], p)
    st_p = tuple(a[None] for a in new_p)
    st_s = tuple(a[None] for a in new_s)
    return (y_p, y_s) + st_p + st_s
```

```python
import functools

import numpy as np
import jax
import jax.numpy as jnp
from jax import lax
from jax.experimental import pallas as pl
from jax.experimental.pallas import tpu as pltpu

F32 = jnp.float32
BF16 = jnp.bfloat16
I32 = jnp.int32

D_MODEL = 2048
CHUNK = 64
SSM_WIDTH = 1024
SSM_GROUP = 16
SSM_GROUPS = 64
SSM_STATE = 64
ATTN_WIDTH = 1024
HEAD_DIM = 128
N_HEADS = 8
N_KV_HEADS = 2
KV_GROUP = 4
IDX_HEADS = 8
IDX_DIM = 64
IDX_TOPK = 256
ROPE_THETA = 500000.0
N_EXPERT_GROUPS = 4
EXPERTS_PER_GROUP = 8
N_EXPERTS = 32
TOP_K = 2
EXPERT_FF = 1024
EPS = 1e-6

LANES = 128
SUBLANES = 8
VMEM_LIMIT = 56 * 1024 * 1024

COL_U, COL_Q, COL_GA, COL_GB, COL_K, COL_V, COL_QI, COL_KIWI = 0, 1024, 2048, 4096, 6144, 6400, 6656, 7168
PROJ_COLS = 7296
PROJ_TN = 2432
KV_WIDTH = N_KV_HEADS * HEAD_DIM

SSM_LB = SSM_WIDTH // LANES
SSM_SB = 8 * SSM_STATE

INT_MIN = np.int32(-2 ** 31)
KEY_NEG_INF = np.int32(np.array([0xFF800000], np.uint32).view(np.int32)[0] ^ 0x7FFFFFFF)


def _params(sem, vmem=VMEM_LIMIT):
    return pltpu.CompilerParams(dimension_semantics=sem, vmem_limit_bytes=vmem)


def _dot(a, b):
    return jnp.dot(a, b, preferred_element_type=F32)


def _dot_nt(a, b):
    return lax.dot_general(a, b, (((1,), (1,)), ((), ())), preferred_element_type=F32)


def _split_bf16(x):
    hi = x.astype(BF16)
    lo = (x - hi.astype(F32)).astype(BF16)
    return hi, lo


def _in_proj_kernel(x_ref, g_ref, w_ref, o_ref, xn_ref):
    @pl.when(pl.program_id(1) == 0)
    def _():
        x = x_ref[...]
        ms = jnp.mean(x * x, axis=-1, keepdims=True)
        xn_ref[...] = (x * lax.rsqrt(ms + EPS) * g_ref[...]).astype(BF16)

    o_ref[...] = _dot(xn_ref[...], w_ref[...])


def in_proj(x, gain, w_bf16, *, tm=512):
    n_tok = x.shape[0]
    return pl.pallas_call(
        _in_proj_kernel,
        grid=(n_tok // tm, PROJ_COLS // PROJ_TN),
        in_specs=[pl.BlockSpec((tm, D_MODEL), lambda i, j: (i, 0)),
                  pl.BlockSpec((1, D_MODEL), lambda i, j: (0, 0)),
                  pl.BlockSpec((D_MODEL, PROJ_TN), lambda i, j: (0, j))],
        out_specs=pl.BlockSpec((tm, PROJ_TN), lambda i, j: (i, j)),
        out_shape=jax.ShapeDtypeStruct((n_tok, PROJ_COLS), F32),
        scratch_shapes=[pltpu.VMEM((tm, D_MODEL), BF16)],
        compiler_params=_params(("arbitrary", "arbitrary")),
        name="in_proj",
    )(x, gain, w_bf16)


def _rope(x, c, s_lo, s_hi, half):
    n = x.shape[-1]
    return x * c + pltpu.roll(x, n - half, 1) * s_lo + pltpu.roll(x, half, 1) * s_hi


def _head_norm(x, g):
    ms = jnp.mean(x * x, axis=-1, keepdims=True)
    return x * lax.rsqrt(ms + EPS) * g


V_AUG = 2 * HEAD_DIM


def _store_v_aug(dst_ref, row0, v_heads):
    n = v_heads[0].shape[0]
    one_col = jnp.where(lax.broadcasted_iota(I32, (n, HEAD_DIM), 1) == 0, 1.0, 0.0).astype(BF16)
    for h, v in enumerate(v_heads):
        dst_ref[row0:row0 + n, h * V_AUG:h * V_AUG + HEAD_DIM] = v.astype(BF16)
        dst_ref[row0:row0 + n, h * V_AUG + HEAD_DIM:(h + 1) * V_AUG] = one_col


def _qk_post_kernel(q_ref, k_ref, v_ref, qi_ref, kw_ref, c128_ref, sl128_ref, sh128_ref,
                    c64_ref, sl64_ref, sh64_ref, qg_ref, kg_ref, ig_ref,
                    qo_ref, kf_ref, kb_ref, vf_ref, vb_ref, qio_ref, kif_ref, kib_ref, wo_ref):
    c128, sl128, sh128 = c128_ref[...], sl128_ref[...], sh128_ref[...]
    c64, sl64, sh64 = c64_ref[...], sl64_ref[...], sh64_ref[...]
    half128 = HEAD_DIM // 8
    half64 = IDX_DIM // 8
    for h in range(N_HEADS):
        sl = slice(h * LANES, (h + 1) * LANES)
        qo_ref[:, sl] = _rope(_head_norm(q_ref[:, sl], qg_ref[...]), c128, sl128, sh128, half128).astype(BF16)
    for h in range(N_KV_HEADS):
        sl = slice(h * LANES, (h + 1) * LANES)
        kk = _rope(_head_norm(k_ref[:, sl], kg_ref[...]), c128, sl128, sh128, half128)
        kf_ref[:, sl] = kk
        kb_ref[:, sl] = kk.astype(BF16)
    v = v_ref[...]
    vf_ref[...] = v
    _store_v_aug(vb_ref, 0, [v[:, h * HEAD_DIM:(h + 1) * HEAD_DIM] for h in range(N_KV_HEADS)])
    lane = lax.broadcasted_iota(I32, c64.shape, 1)
    low = lane < IDX_DIM
    for p in range(IDX_HEADS // 2):
        x = _rope(qi_ref[:, p * LANES:(p + 1) * LANES], c64, sl64, sh64, half64)
        qio_ref[:, (2 * p) * LANES:(2 * p + 1) * LANES] = jnp.where(low, x, 0.0).astype(BF16)
        qio_ref[:, (2 * p + 1) * LANES:(2 * p + 2) * LANES] = jnp.where(low, pltpu.roll(x, IDX_DIM, 1), 0.0).astype(BF16)
    kw = kw_ref[...]
    ms = jnp.sum(jnp.where(low, kw * kw, 0.0), axis=-1, keepdims=True) * (1.0 / IDX_DIM)
    ki = _rope(kw * lax.rsqrt(ms + EPS) * ig_ref[...], c64, sl64, sh64, half64)
    kif_ref[...] = ki[:, :IDX_DIM]
    kib_ref[...] = jnp.where(low, ki, 0.0).astype(BF16)
    wo_ref[...] = (pltpu.roll(kw, IDX_DIM, 1) * IDX_HEADS ** -0.5) * IDX_DIM ** -0.5


def _rope_tables(pos, head_dim):
    r = head_dim // 4
    half = r // 2
    inv = ROPE_THETA ** (-jnp.arange(half, dtype=F32) * 2.0 / r)
    ang = pos.astype(F32)[:, None] * inv[None, :]
    cos, sin = jnp.cos(ang), jnp.sin(ang)
    n = pos.shape[0]
    zh = jnp.zeros((n, half), F32)
    rest = head_dim - r
    c = jnp.concatenate([cos, cos, jnp.ones((n, rest), F32)], axis=-1)
    s_lo = jnp.concatenate([-sin, zh, jnp.zeros((n, rest), F32)], axis=-1)
    s_hi = jnp.concatenate([zh, sin, jnp.zeros((n, rest), F32)], axis=-1)
    rep = LANES // head_dim
    return tuple(jnp.tile(t, (1, rep)) for t in (c, s_lo, s_hi))


QK_TM = 512


def qk_post(proj, table_pos, table_block, q_gain, k_gain, ik_gain):
    tm = QK_TM
    n_tok = proj.shape[0]
    t128 = _rope_tables(table_pos, HEAD_DIM)
    t64 = _rope_tables(table_pos, IDX_DIM)
    ik_gain128 = jnp.concatenate([ik_gain, jnp.zeros((LANES - IDX_DIM,), F32)])[None, :]

    def col(width, start):
        return pl.BlockSpec((tm, width), lambda i: (i, start // width))

    def row(width):
        return pl.BlockSpec((tm, width), lambda i: (i, 0))

    table = pl.BlockSpec((tm, LANES), lambda i: (table_block(i), 0))
    gain = pl.BlockSpec((1, LANES), lambda i: (0, 0))
    return pl.pallas_call(
        _qk_post_kernel,
        grid=(n_tok // tm,),
        in_specs=[col(ATTN_WIDTH, COL_Q), col(KV_WIDTH, COL_K), col(KV_WIDTH, COL_V), col(IDX_HEADS * IDX_DIM, COL_QI),
                  col(LANES, COL_KIWI)] + [table] * 6 + [gain] * 3,
        out_specs=[row(ATTN_WIDTH), row(KV_WIDTH), row(KV_WIDTH), row(KV_WIDTH), row(N_KV_HEADS * V_AUG), row(IDX_HEADS * LANES),
                   row(IDX_DIM), row(LANES), row(LANES)],
        out_shape=[jax.ShapeDtypeStruct((n_tok, ATTN_WIDTH), BF16),
                   jax.ShapeDtypeStruct((n_tok, KV_WIDTH), F32), jax.ShapeDtypeStruct((n_tok, KV_WIDTH), BF16),
                   jax.ShapeDtypeStruct((n_tok, KV_WIDTH), F32), jax.ShapeDtypeStruct((n_tok, N_KV_HEADS * V_AUG), BF16),
                   jax.ShapeDtypeStruct((n_tok, IDX_HEADS * LANES), BF16),
                   jax.ShapeDtypeStruct((n_tok, IDX_DIM), F32), jax.ShapeDtypeStruct((n_tok, LANES), BF16),
                   jax.ShapeDtypeStruct((n_tok, LANES), F32)],
        compiler_params=_params(("arbitrary",)),
        name="qk_post",
    )(proj, proj, proj, proj, proj, *t128, *t64, q_gain[None, :], k_gain[None, :], ik_gain128)


def _gelu_tanh(x):
    return 0.5 * x * (1.0 + jnp.tanh(np.float32(np.sqrt(2.0 / np.pi)) * (x + 0.044715 * (x * x * x))))


SSM_LT = SSM_SB // LANES
SSM_SEG = 64


def _ssm_kernel(u_ref, wb_ref, wc_ref, pw_ref, d_ref, h0_ref, g_ref, sre_ref, sim_ref,
                er_ref, ei_ref, car_ref, up_ref, yp_ref):
    c = pl.program_id(2)

    @pl.when(c == 0)
    def _():
        car_ref[...] = h0_ref[...]

    for j in range(SSM_SEG):
        up_ref[j * SUBLANES:(j + 1) * SUBLANES, :] = u_ref[pl.ds(j, SUBLANES, stride=SSM_SEG), :]
    e = _dot(up_ref[...].astype(BF16), wb_ref[...])
    tiles = [slice(lt * LANES, (lt + 1) * LANES) for lt in range(SSM_LT)]
    for lt, sl in enumerate(tiles):
        er_ref[lt] = e[:, sl]
        ei_ref[lt] = e[:, SSM_SB + lt * LANES:SSM_SB + (lt + 1) * LANES]

    def cmul_add(ar, ai, br, bi, cr, ci):
        return ar * br - ai * bi + cr, ar * bi + ai * br + ci

    lb = [(pw_ref[0, 0:1, sl], pw_ref[1, 0:1, sl]) for sl in tiles]
    zero = jnp.zeros((SUBLANES, LANES), F32)
    st = [(zero, zero)] * SSM_LT
    for j in range(SSM_SEG):
        rows = slice(j * SUBLANES, (j + 1) * SUBLANES)
        for lt in range(SSM_LT):
            st[lt] = cmul_add(*lb[lt], *st[lt], er_ref[lt, rows, :], ei_ref[lt, rows, :])
            er_ref[lt, rows, :] = st[lt][0]
            ei_ref[lt, rows, :] = st[lt][1]

    enter = []
    for lt, sl in enumerate(tiles):
        seg_r, seg_i = pw_ref[0, SSM_SEG - 1:SSM_SEG, sl], pw_ref[1, SSM_SEG - 1:SSM_SEG, sl]
        cr, ci = car_ref[0:1, sl], car_ref[1:2, sl]
        rows_r, rows_i = [], []
        for r in range(SUBLANES):
            rows_r.append(cr)
            rows_i.append(ci)
            cr, ci = cmul_add(seg_r, seg_i, cr, ci, st[lt][0][r:r + 1], st[lt][1][r:r + 1])
        car_ref[0:1, sl] = cr
        car_ref[1:2, sl] = ci
        enter.append((jnp.concatenate(rows_r, axis=0), jnp.concatenate(rows_i, axis=0)))

    for j in range(SSM_SEG):
        rows = slice(j * SUBLANES, (j + 1) * SUBLANES)
        for lt, sl in enumerate(tiles):
            xr, xi = cmul_add(pw_ref[0, j:j + 1, sl], pw_ref[1, j:j + 1, sl], *enter[lt],
                              er_ref[lt, rows, :], ei_ref[lt, rows, :])
            er_ref[lt, rows, :] = xr
            ei_ref[lt, rows, :] = xi

    y = None
    for lt, sl in enumerate(tiles):
        t = _dot(er_ref[lt].astype(BF16), wc_ref[0, sl, :]) - _dot(ei_ref[lt].astype(BF16), wc_ref[1, sl, :])
        y = t if y is None else y + t
    yp_ref[...] = y
    out_rows = 2 * SUBLANES
    for t0 in range(0, SUBLANES * SSM_SEG, out_rows):
        r, j0 = divmod(t0, SSM_SEG)
        rows = slice(t0, t0 + out_rows)
        yt = yp_ref[pl.ds(j0 * SUBLANES + r, out_rows, stride=SUBLANES), :] + d_ref[...] * u_ref[rows, :]
        g_ref[rows, :] = _gelu_tanh(yt).astype(BF16)

    @pl.when(c == pl.num_programs(2) - 1)
    def _():
        sre_ref[...] = car_ref[0:1, :]
        sim_ref[...] = car_ref[1:2, :]


def _ssm_weights(a_re, a_im, log_dt, b_re, b_im, c_re, c_im):
    lam_re, lam_im = a_re, a_im
    dt = jnp.exp(log_dt)[:, None]
    mag = jnp.exp(lam_re * dt)
    lb_re, lb_im = mag * jnp.cos(lam_im * dt), mag * jnp.sin(lam_im * dt)
    den = lam_re * lam_re + lam_im * lam_im
    num_re = lb_re - 1.0
    z_re = (num_re * lam_re + lb_im * lam_im) / den
    z_im = (lb_im * lam_re - num_re * lam_im) / den
    zb_re = z_re[:, :, None] * b_re - z_im[:, :, None] * b_im
    zb_im = z_re[:, :, None] * b_im + z_im[:, :, None] * b_re
    eye = jnp.eye(8, dtype=F32)

    def blockdiag_in(w):
        return jnp.einsum('jgph,gk->jghkp', w.reshape(SSM_LB, 8, SSM_STATE, SSM_GROUP), eye).reshape(SSM_LB, LANES, SSM_SB)

    def blockdiag_out(w):
        return jnp.einsum('jghp,gk->jkpgh', w.reshape(SSM_LB, 8, SSM_GROUP, SSM_STATE), eye).reshape(SSM_LB, SSM_SB, LANES)

    wb = jnp.concatenate([blockdiag_in(zb_re), blockdiag_in(zb_im)], axis=-1).astype(BF16)
    wc = jnp.stack([blockdiag_out(c_re), blockdiag_out(c_im)], axis=1).astype(BF16)

    pr, pi_ = lb_re.reshape(SSM_LB, 1, SSM_SB), lb_im.reshape(SSM_LB, 1, SSM_SB)
    while pr.shape[1] < SSM_SEG:
        tr, ti = pr[:, -1:], pi_[:, -1:]
        pr, pi_ = (jnp.concatenate([pr, pr * tr - pi_ * ti], axis=1), jnp.concatenate([pi_, pr * ti + pi_ * tr], axis=1))
    pw = jnp.stack([pr, pi_], axis=1)
    return wb, wc, pw


def ssm(proj, ssm_w, d_skip, h0, *, n_batch, seq, row0):
    wb, wc, pw = ssm_w
    tc = SUBLANES * SSM_SEG
    n_chunks = seq // tc
    blk0 = row0 // tc
    n_tok = n_batch * seq
    state_shape = jax.ShapeDtypeStruct((n_batch, SSM_LB, 1, SSM_SB), F32)
    state_spec = pl.BlockSpec((None, None, 1, SSM_SB), lambda b, j, c: (b, j, 0, 0))
    g, s_re, s_im = pl.pallas_call(
        _ssm_kernel,
        grid=(n_batch, SSM_LB, n_chunks),
        in_specs=[pl.BlockSpec((tc, LANES), lambda b, j, c: (blk0 + b * n_chunks + c, j)),
                  pl.BlockSpec((None, LANES, 2 * SSM_SB), lambda b, j, c: (j, 0, 0)),
                  pl.BlockSpec((None, 2, SSM_SB, LANES), lambda b, j, c: (j, 0, 0, 0)),
                  pl.BlockSpec((None, 2, SSM_SEG, SSM_SB), lambda b, j, c: (j, 0, 0, 0)),
                  pl.BlockSpec((1, LANES), lambda b, j, c: (0, j)),
                  pl.BlockSpec((None, None, 2, SSM_SB), lambda b, j, c: (b, j, 0, 0))],
        out_specs=[pl.BlockSpec((tc, LANES), lambda b, j, c: (b * n_chunks + c, j)), state_spec, state_spec],
        out_shape=[jax.ShapeDtypeStruct((n_tok, SSM_WIDTH), BF16), state_shape, state_shape],
        scratch_shapes=[pltpu.VMEM((SSM_LT, tc, LANES), F32), pltpu.VMEM((SSM_LT, tc, LANES), F32),
                        pltpu.VMEM((2, SSM_SB), F32), pltpu.VMEM((tc, LANES), F32), pltpu.VMEM((tc, LANES), F32)],
        compiler_params=_params(("arbitrary", "arbitrary", "arbitrary")),
        name="ssm",
    )(proj, wb, wc, pw, d_skip[None, :], h0)
    return g, s_re.reshape(n_batch, SSM_GROUPS, SSM_STATE), s_im.reshape(n_batch, SSM_GROUPS, SSM_STATE)


def _ssm_step_kernel(u_ref, wb_ref, wc_ref, pw_ref, d_ref, h0_ref, g_ref, sre_ref, sim_ref, er_ref, ei_ref, *, seq):
    n_seq = h0_ref.shape[1]
    u = u_ref[...]
    e = _dot(u.astype(BF16), wb_ref[...])
    n_lt = SSM_SB // LANES
    y = d_ref[...] * u
    for lt in range(n_lt):
        sl = slice(lt * LANES, (lt + 1) * LANES)
        er_ref[...] = e[:, lt * LANES:(lt + 1) * LANES]
        ei_ref[...] = e[:, SSM_SB + lt * LANES:SSM_SB + (lt + 1) * LANES]
        lr, li = pw_ref[0, 0:1, sl], pw_ref[1, 0:1, sl]
        sr, si = h0_ref[0, :, sl], h0_ref[1, :, sl]
        for t in range(seq):
            rows = pl.ds(t, n_seq, stride=seq)
            sr, si = lr * sr - li * si + er_ref[rows, :], lr * si + li * sr + ei_ref[rows, :]
            er_ref[rows, :] = sr
            ei_ref[rows, :] = si
        y = y + (_dot(er_ref[...].astype(BF16), wc_ref[0, sl, :]) - _dot(ei_ref[...].astype(BF16), wc_ref[1, sl, :]))
        sre_ref[:, sl] = sr
        sim_ref[:, sl] = si
    g_ref[...] = _gelu_tanh(y).astype(BF16)


def ssm_step(proj, ssm_w, d_skip, h0, *, n_batch, seq, row0):
    wb, wc, pw = ssm_w
    n_tok = n_batch * seq
    assert row0 % n_tok == 0
    state_shape = jax.ShapeDtypeStruct((SSM_LB, n_batch, SSM_SB), F32)
    state_spec = pl.BlockSpec((None, n_batch, SSM_SB), lambda j: (j, 0, 0))
    g, s_re, s_im = pl.pallas_call(
        functools.partial(_ssm_step_kernel, seq=seq),
        grid=(SSM_LB,),
        in_specs=[pl.BlockSpec((n_tok, LANES), lambda j: (row0 // n_tok, j)),
                  pl.BlockSpec((None, LANES, 2 * SSM_SB), lambda j: (j, 0, 0)),
                  pl.BlockSpec((None, 2, SSM_SB, LANES), lambda j: (j, 0, 0, 0)),
                  pl.BlockSpec((None, 2, SSM_SEG, SSM_SB), lambda j: (j, 0, 0, 0)),
                  pl.BlockSpec((1, LANES), lambda j: (0, j)),
                  pl.BlockSpec((None, 2, n_batch, SSM_SB), lambda j: (j, 0, 0, 0))],
        out_specs=[pl.BlockSpec((n_tok, LANES), lambda j: (0, j)), state_spec, state_spec],
        out_shape=[jax.ShapeDtypeStruct((n_tok, SSM_WIDTH), BF16), state_shape, state_shape],
        scratch_shapes=[pltpu.VMEM((n_tok, LANES), F32), pltpu.VMEM((n_tok, LANES), F32)],
        compiler_params=_params(("arbitrary",)),
        name="ssm_step",
    )(proj, wb, wc, pw, d_skip[None, :], h0)

    def per_seq(s):
        return s.transpose(1, 0, 2).reshape(n_batch, SSM_GROUPS, SSM_STATE)

    return g, per_seq(s_re), per_seq(s_im)


def _row_sum(x):
    return jnp.sum(x, axis=1, keepdims=True)


def _row_count(mask):
    return _row_sum(jnp.where(mask, 1, 0))


I16 = jnp.int16
I16_MIN = -2 ** 15


def _count16(ref, cand, compare):
    accs = [None] * 4
    for t in range(ref.shape[1] // LANES):
        x = jnp.where(compare(ref[:, t * LANES:(t + 1) * LANES], cand), I16(1), I16(0))
        accs[t % 4] = x if accs[t % 4] is None else accs[t % 4] + x
    accs = [a for a in accs if a is not None]
    total = accs[0]
    for a in accs[1:]:
        total = total + a
    return _row_sum(total.astype(I32))


def _bisect16(ref, target):
    def step(i, base):
        cand = base + lax.shift_left(np.int32(1), np.int32(15) - i)
        cnt = _count16(ref, cand.astype(I16), lambda a, b: a >= b)
        return jnp.where(cnt >= target, cand, base)
    return lax.fori_loop(0, 16, step, jnp.full((ref.shape[0], 1), I16_MIN, I32))


def _bisect32(key_ref, n_sel):
    bq, n_keys = key_ref.shape
    hr = bq // 2

    def lane_counts(h, cand):
        accs = [None] * 4
        for t in range(n_keys // LANES):
            x = jnp.where(key_ref[h * hr:(h + 1) * hr, t * LANES:(t + 1) * LANES] >= cand, 1, 0)
            accs[t % 4] = x if accs[t % 4] is None else accs[t % 4] + x
        accs = [a for a in accs if a is not None]
        total = accs[0]
        for a in accs[1:]:
            total = total + a
        return total

    def decide(part, cand, base):
        return jnp.where(_row_sum(part) >= n_sel, cand, base)

    def bit(i):
        return lax.shift_left(np.int32(1), np.int32(31) - i)

    def body(i, state):
        base_a, base_b, part_b = state
        cand_a = base_a + bit(i)
        part_a = lane_counts(0, cand_a)
        base_b = decide(part_b, base_b + bit(i - 1), base_b)
        part_b = lane_counts(1, base_b + bit(i))
        return decide(part_a, cand_a, base_a), base_b, part_b

    base0 = jnp.full((hr, 1), INT_MIN, I32)
    first = base0 + bit(0)
    state = (decide(lane_counts(0, first), first, base0), base0, lane_counts(1, first))
    base_a, base_b, part_b = lax.fori_loop(1, 32, body, state)
    base_b = decide(part_b, base_b + bit(31), base_b)
    return jnp.concatenate([base_a, base_b], axis=0)


def _stack_heads(ref, heads):
    return jnp.concatenate([ref[:, h * LANES:(h + 1) * LANES] for h in heads], axis=0)


def _dsa_body(q_ref, qi_ref, wi_ref, k_ref, v_ref, ki_ref, o_ref, key_ref, bias_ref, hi_ref, lo_ref,
              *, q_pos_first, s_valid, n_sel, packed_bisect, stack):
    bq, n_keys = key_ref.shape
    col = lax.broadcasted_iota(I32, (bq, n_keys), 1)
    qpos = q_pos_first + lax.broadcasted_iota(I32, (bq, 1), 0)
    allowed = col < jnp.minimum((qpos // CHUNK + 1) * CHUNK, s_valid)

    ki = ki_ref[...]
    score = None
    for h0 in range(0, IDX_HEADS, stack):
        d = _dot_nt(_stack_heads(qi_ref, range(h0, h0 + stack)), ki)
        for j in range(stack):
            t = jnp.maximum(d[j * bq:(j + 1) * bq], 0.0) * wi_ref[:, h0 + j:h0 + j + 1]
            score = t if score is None else score + t
    score = jnp.where(score == 0.0, 0.0, score)
    bits = pltpu.bitcast(score, I32)
    key = jnp.where(bits < 0, bits ^ np.int32(0x7FFFFFFF), bits)
    key = jnp.where(allowed, key, KEY_NEG_INF)
    key_ref[...] = key

    if packed_bisect:
        hi_ref[...] = (key >> 16).astype(I16)
        lo_ref[...] = ((key & 0xFFFF) + I16_MIN).astype(I16)
        thr_hi = _bisect16(hi_ref, n_sel)
        thr_hi16 = thr_hi.astype(I16)
        need_lo = n_sel - _count16(hi_ref, thr_hi16, lambda a, b: a > b)
        lo_ref[...] = jnp.where(hi_ref[...] == thr_hi16, lo_ref[...], I16(I16_MIN))
        thr_lo = _bisect16(lo_ref, need_lo)
        thr = lax.shift_left(thr_hi, np.int32(16)) + (thr_lo - I16_MIN)
    else:
        thr = _bisect32(key_ref, n_sel)
    thr = jnp.maximum(thr, KEY_NEG_INF)

    key = key_ref[...]
    need = n_sel - _row_count(key > thr)
    n_eq = _row_count(key == thr)
    n_bits = int(n_keys - 1).bit_length()

    def tie_cut():
        def step(i, j0):
            cand = j0 + lax.shift_left(np.int32(1), np.int32(n_bits - 1) - i)
            cnt = _row_sum(jnp.where(key_ref[...] == thr, jnp.where(col < cand, 1, 0), 0))
            return jnp.where(cnt < need, cand, j0)
        return lax.fori_loop(0, n_bits, step, jnp.zeros((bq, 1), I32))

    split = jnp.max(jnp.where(n_eq > need, 1, 0)) > 0
    j_last = lax.cond(split, tie_cut, lambda: jnp.full((bq, 1), n_keys, I32))
    tie_bias = jnp.where(thr == KEY_NEG_INF, -jnp.inf, 0.0)
    bias_ref[...] = jnp.where(key > thr, 0.0,
                              jnp.where(key == thr, jnp.where(col <= j_last, tie_bias, -jnp.inf), -jnp.inf))

    c = np.float32(HEAD_DIM ** -0.5 * np.log2(np.e))
    for h0 in range(0, N_HEADS, stack):
        kv = h0 // KV_GROUP
        heads = range(h0, h0 + stack)
        s_all = _dot_nt(_stack_heads(q_ref, heads), k_ref[:, kv * HEAD_DIM:(kv + 1) * HEAD_DIM])
        ps = []
        for g in range(stack):
            s = s_all[g * bq:(g + 1) * bq] + bias_ref[...]
            m = jnp.max(s, axis=1, keepdims=True)
            ps.append(jnp.exp2((s - m) * c).astype(BF16))
        pv = _dot(jnp.concatenate(ps, axis=0), v_ref[:, kv * V_AUG:(kv + 1) * V_AUG])
        for g, h in enumerate(heads):
            o = pv[g * bq:(g + 1) * bq]
            o_ref[:, h * HEAD_DIM:(h + 1) * HEAD_DIM] = (o[:, :HEAD_DIM] / o[:, HEAD_DIM:HEAD_DIM + 1]).astype(BF16)


def _dsa_scratch(bq, n_keys):
    return [pltpu.VMEM((bq, n_keys), I32), pltpu.VMEM((bq, n_keys), F32),
            pltpu.VMEM((bq, n_keys), I16), pltpu.VMEM((bq, n_keys), I16)]


def _dsa_kernel(q_ref, qi_ref, wi_ref, k_ref, v_ref, ki_ref, o_ref, *scratch, q_pos0, **static):
    bq = scratch[0].shape[0]
    _dsa_body(q_ref, qi_ref, wi_ref, k_ref, v_ref, ki_ref, o_ref, *scratch,
              q_pos_first=q_pos0 + pl.program_id(1) * bq, **static)


def dsa(q, qi, wi, k, v, ki, *, bq, q_blk0, n_qblk, n_keys, n_sel, packed_bisect, stack):
    n_batch, seq = q.shape[:2]

    def qspec(width):
        return pl.BlockSpec((None, bq, width), lambda b, i: (b, q_blk0 + i, 0))

    def kspec(width):
        return pl.BlockSpec((None, n_keys, width), lambda b, i: (b, 0, 0))

    return pl.pallas_call(
        functools.partial(_dsa_kernel, q_pos0=q_blk0 * bq, s_valid=seq, n_sel=n_sel, packed_bisect=packed_bisect,
                          stack=stack),
        grid=(n_batch, n_qblk),
        in_specs=[qspec(ATTN_WIDTH), qspec(IDX_HEADS * LANES), qspec(LANES), kspec(KV_WIDTH), kspec(N_KV_HEADS * V_AUG),
                  kspec(LANES)],
        out_specs=pl.BlockSpec((None, bq, ATTN_WIDTH), lambda b, i: (b, i, 0)),
        out_shape=jax.ShapeDtypeStruct((n_batch, n_qblk * bq, ATTN_WIDTH), BF16),
        scratch_shapes=_dsa_scratch(bq, n_keys),
        compiler_params=_params(("arbitrary", "arbitrary")),
        name="dsa",
    )(q, qi, wi, k, v, ki)


def _dsa_step_kernel(q_ref, qi_ref, wi_ref, ck_hbm, cv_hbm, cki_ref, nk_ref, nv_ref, nki_ref, o_ref,
                     k_buf, v_buf, ki_buf, cache_buf, sem, *scratch, past, n_sel):
    b = pl.program_id(0)

    def cache_copies(seq, slot):
        return [pltpu.make_async_copy(src.at[seq, :, h, :], cache_buf.at[slot, a, h], sem.at[slot])
                for a, src in enumerate((ck_hbm, cv_hbm)) for h in range(N_KV_HEADS)]

    @pl.when(b == 0)
    def _():
        for cp in cache_copies(0, 0):
            cp.start()

    @pl.when(b + 1 < pl.num_programs(0))
    def _():
        for cp in cache_copies(b + 1, (b + 1) % 2):
            cp.start()

    slot = b % 2
    for cp in cache_copies(b, slot):
        cp.wait()

    ts = nk_ref.shape[0]
    n_keys = k_buf.shape[0]
    for h in range(N_KV_HEADS):
        k_buf[0:past, h * HEAD_DIM:(h + 1) * HEAD_DIM] = cache_buf[slot, 0, h].astype(BF16)
    _store_v_aug(v_buf, 0, [cache_buf[slot, 1, h] for h in range(N_KV_HEADS)])
    for buf, new in ((k_buf, nk_ref), (v_buf, nv_ref)):
        buf[past:past + ts, :] = new[...]
        buf[past + ts:n_keys, :] = jnp.zeros((n_keys - past - ts, buf.shape[1]), BF16)
    ki_buf[0:past, 0:IDX_DIM] = cki_ref[...].astype(BF16)
    ki_buf[0:past, IDX_DIM:LANES] = jnp.zeros((past, LANES - IDX_DIM), BF16)
    ki_buf[past:past + ts, :] = nki_ref[...]
    ki_buf[past + ts:n_keys, :] = jnp.zeros((n_keys - past - ts, LANES), BF16)
    _dsa_body(q_ref, qi_ref, wi_ref, k_buf, v_buf, ki_buf, o_ref, *scratch,
              q_pos_first=past, s_valid=past + ts, n_sel=n_sel, packed_bisect=True, stack=KV_GROUP)


def dsa_step(q, qi, wi, cache_k, cache_v, cache_ki, k_new, v_new, ki_new, *, n_sel):
    n_batch, ts = q.shape[:2]
    past = cache_k.shape[1]
    n_keys = -(-(past + ts) // LANES) * LANES

    def spec(rows, width):
        return pl.BlockSpec((None, rows, width), lambda b: (b, 0, 0))

    return pl.pallas_call(
        functools.partial(_dsa_step_kernel, past=past, n_sel=n_sel),
        grid=(n_batch,),
        in_specs=[spec(ts, ATTN_WIDTH), spec(ts, IDX_HEADS * LANES), spec(ts, LANES),
                  pl.BlockSpec(memory_space=pl.ANY), pl.BlockSpec(memory_space=pl.ANY), spec(past, IDX_DIM),
                  spec(ts, KV_WIDTH), spec(ts, N_KV_HEADS * V_AUG), spec(ts, LANES)],
        out_specs=spec(ts, ATTN_WIDTH),
        out_shape=jax.ShapeDtypeStruct((n_batch, ts, ATTN_WIDTH), BF16),
        scratch_shapes=[pltpu.VMEM((n_keys, KV_WIDTH), BF16), pltpu.VMEM((n_keys, N_KV_HEADS * V_AUG), BF16),
                        pltpu.VMEM((n_keys, LANES), BF16),
                        pltpu.VMEM((2, 2, N_KV_HEADS, past, HEAD_DIM), F32), pltpu.SemaphoreType.DMA((2,)),
                        *_dsa_scratch(ts, n_keys)],
        compiler_params=_params(("arbitrary",)),
        name="dsa_step",
    )(q, qi, wi, cache_k, cache_v, cache_ki, k_new, v_new, ki_new)


def _merge_kernel(g_ref, a_ref, ga_ref, gb_ref, wv_ref, wg_ref, wb_ref, o_ref):
    g = g_ref[...]
    branch_a = _dot(g, wv_ref[...]) * jax.nn.sigmoid(_dot(g, wg_ref[...]))
    branch_b = _dot(a_ref[...], wb_ref[...])
    merged = jax.nn.sigmoid(ga_ref[...]) * branch_a + jax.nn.sigmoid(gb_ref[...]) * branch_b
    o_ref[...] = merged.astype(BF16)


def merge(g, attn, proj, w_val, w_gate, w_branch, *, tm=1024, tn=512):
    n_tok = g.shape[0]
    nj = D_MODEL // tn

    def wspec():
        return pl.BlockSpec((SSM_WIDTH, tn), lambda i, j: (0, j))

    return pl.pallas_call(
        _merge_kernel,
        grid=(n_tok // tm, nj),
        in_specs=[pl.BlockSpec((tm, SSM_WIDTH), lambda i, j: (i, 0)),
                  pl.BlockSpec((tm, ATTN_WIDTH), lambda i, j: (i, 0)),
                  pl.BlockSpec((tm, tn), lambda i, j: (i, COL_GA // tn + j)),
                  pl.BlockSpec((tm, tn), lambda i, j: (i, COL_GB // tn + j)),
                  wspec(), wspec(), wspec()],
        out_specs=pl.BlockSpec((tm, tn), lambda i, j: (i, j)),
        out_shape=jax.ShapeDtypeStruct((n_tok, D_MODEL), BF16),
        compiler_params=_params(("arbitrary", "arbitrary")),
        name="merge",
    )(g, attn, proj, proj, w_val, w_gate, w_branch)


ROUTER_COLS = N_EXPERT_GROUPS + N_EXPERTS
MOE_TM = 256


def _first_lane_of_max(x, lane_f):
    m = jnp.max(x, axis=1, keepdims=True)
    return m, jnp.min(jnp.where(x == m, lane_f, float(LANES)), axis=1, keepdims=True)


def _out_proj_kernel(x_ref, m_ref, wo_ref, gn_ref, wrh_ref, wrl_ref, br_ref, cin_ref,
                     h_ref, hn_ref, ri_ref, rw_ref, cnt_ref, carry_ref):
    @pl.when(pl.program_id(0) == 0)
    def _():
        carry_ref[...] = cin_ref[...]

    h = x_ref[...] + _dot(m_ref[...], wo_ref[...])
    h_ref[...] = h
    ms = jnp.mean(h * h, axis=-1, keepdims=True)
    hn = h * lax.rsqrt(ms + EPS) * gn_ref[...]
    hn_ref[...] = hn
    hh, hl = _split_bf16(hn)
    wrh = wrh_ref[...]
    lg = _dot(hh, wrh) + _dot(hl, wrh) + _dot(hh, wrl_ref[...]) + br_ref[...]

    tm = lg.shape[0]
    lane = lax.broadcasted_iota(I32, lg.shape, 1)
    lane_f = lane.astype(F32)
    ninf = -jnp.inf
    gl = jnp.where(lane < N_EXPERT_GROUPS, lg, ninf)
    gmax, gsel = _first_lane_of_max(gl, lane_f)
    g_w = 1.0 / jnp.sum(jnp.exp(gl - gmax), axis=1, keepdims=True)
    lo = N_EXPERT_GROUPS + EXPERTS_PER_GROUP * gsel
    el = jnp.where(lane_f >= lo, jnp.where(lane_f < lo + EXPERTS_PER_GROUP, lg, ninf), ninf)
    v1, i1 = _first_lane_of_max(el, lane_f)
    el2 = jnp.where(lane_f == i1, ninf, el)
    v2, i2 = _first_lane_of_max(el2, lane_f)
    t = jnp.exp(v2 - v1)
    s1 = 1.0 / (1.0 + t)
    w1 = s1 * g_w
    w2 = (t * s1) * g_w

    m1 = jnp.where(lane_f == i1, 1.0, 0.0)
    m2 = jnp.where(lane_f == i2, 1.0, 0.0)
    both = m1 + m2
    tri = jnp.where(lax.broadcasted_iota(I32, (tm, tm), 0) > lax.broadcasted_iota(I32, (tm, tm), 1), 1.0, 0.0)
    before = _dot(tri.astype(BF16), both.astype(BF16)) + carry_ref[...]
    r1 = jnp.sum(before * m1, axis=1, keepdims=True)
    r2 = jnp.sum(before * m2, axis=1, keepdims=True)
    carry_ref[...] = carry_ref[...] + jnp.sum(both, axis=0, keepdims=True)
    cnt_ref[...] = carry_ref[...]
    e1 = i1 - float(N_EXPERT_GROUPS)
    e2 = i2 - float(N_EXPERT_GROUPS)
    fields = jnp.where(lane == 0, e1, jnp.where(lane == 1, e2, jnp.where(lane == 2, r1, jnp.where(lane == 3, r2, 0.0))))
    ri_ref[...] = fields.T[0:SUBLANES, :].astype(I32)
    rw_ref[...] = jnp.where(lane == 0, w1, jnp.where(lane == 1, w2, 0.0))


def _router_weights(w_router_group, b_router_group, w_router_expert, b_router_expert):
    wr = jnp.concatenate([w_router_group, w_router_expert, jnp.zeros((D_MODEL, LANES - ROUTER_COLS), F32)], axis=1)
    wr_hi = wr.astype(BF16)
    wr_lo = (wr - wr_hi.astype(F32)).astype(BF16)
    br = jnp.concatenate([b_router_group, b_router_expert, jnp.zeros((LANES - ROUTER_COLS,), F32)])[None, :]
    return wr_hi, wr_lo, br


def out_proj(x, merged, w_out, ffn_gain, router_w, counts_in, *, tm=256):
    n_tok = x.shape[0]
    wr_hi, wr_lo, br = router_w

    def row(width):
        return pl.BlockSpec((tm, width), lambda i: (i, 0))

    def const(shape):
        return pl.BlockSpec(shape, lambda i: (0, 0), pipeline_mode=pl.Buffered(1))

    return pl.pallas_call(
        _out_proj_kernel,
        grid=(n_tok // tm,),
        in_specs=[row(D_MODEL), row(D_MODEL), const((D_MODEL, D_MODEL)), const((1, D_MODEL)),
                  const((D_MODEL, LANES)), const((D_MODEL, LANES)), const((1, LANES)), const((1, LANES))],
        out_specs=[row(D_MODEL), row(D_MODEL), pl.BlockSpec((SUBLANES, tm), lambda i: (0, i)), row(LANES),
                   pl.BlockSpec((1, LANES), lambda i: (0, 0))],
        out_shape=[jax.ShapeDtypeStruct((n_tok, D_MODEL), F32), jax.ShapeDtypeStruct((n_tok, D_MODEL), F32),
                   jax.ShapeDtypeStruct((SUBLANES, n_tok), I32), jax.ShapeDtypeStruct((n_tok, LANES), F32),
                   jax.ShapeDtypeStruct((1, LANES), F32)],
        scratch_shapes=[pltpu.VMEM((1, LANES), F32)],
        compiler_params=_params(("arbitrary",)),
        name="out_proj",
    )(x, merged, w_out, ffn_gain[None, :], wr_hi, wr_lo, br, counts_in)


def _block_layout(counts):
    padded = (counts + MOE_TM - 1) // MOE_TM * MOE_TM
    pad_end = jnp.cumsum(padded).astype(I32)
    pad_start = pad_end - padded
    n_used = pad_end[-1] // MOE_TM
    return pad_start, pad_end, n_used


def _moe_rows(n_tok):
    return -(-(n_tok * TOP_K + N_EXPERTS * (MOE_TM - 1)) // MOE_TM) * MOE_TM


DISPATCH_TM = 512


def _wait_rows(src_hbm, dst, sem, n_rows):
    pltpu.make_async_copy(src_hbm.at[pl.ds(0, n_rows)], dst, sem).wait()


def _dispatch_kernel(d0_ref, d1_ref, pe_ref, cnt_ref, nu_ref, hna_ref, hnb_ref, xs_hbm, zbuf, sem, semz,
                     *, n_blocks, a_tiles):
    i = pl.program_id(0)

    def zero_block(row0):
        return pltpu.make_async_copy(zbuf, xs_hbm.at[pl.ds(pl.multiple_of(row0, MOE_TM), MOE_TM)], semz)

    @pl.when(i == 0)
    def _():
        zbuf[...] = jnp.zeros_like(zbuf)
        for start in (True, False):
            for e in range(N_EXPERTS):
                @pl.when(cnt_ref[e] > 0)
                def _():
                    cp = zero_block(pe_ref[e] - MOE_TM)
                    cp.start() if start else cp.wait()

            def tail(b, c):
                cp = zero_block(b * MOE_TM)
                cp.start() if start else cp.wait()
                return c
            lax.fori_loop(nu_ref[0], n_blocks, tail, 0)

    base = i * DISPATCH_TM

    def scatter(hn_ref):
        def body(r, c):
            src = hn_ref.at[pl.ds(r, 1)]
            pltpu.make_async_copy(src, xs_hbm.at[pl.ds(d0_ref[base + r], 1)], sem).start()
            pltpu.make_async_copy(src, xs_hbm.at[pl.ds(d1_ref[base + r], 1)], sem).start()
            return c
        lax.fori_loop(0, DISPATCH_TM, body, 0, unroll=8)
        for _ in range(TOP_K):
            pltpu.make_async_copy(hn_ref, xs_hbm.at[pl.ds(0, DISPATCH_TM)], sem).wait()

    @pl.when(i < a_tiles)
    def _():
        scatter(hna_ref)

    @pl.when(i >= a_tiles)
    def _():
        scatter(hnb_ref)


def dispatch(hn_a, hn_b, dest0, dest1, pad_end, counts, n_used):
    a_tiles, b_tiles = hn_a.shape[0] // DISPATCH_TM, hn_b.shape[0] // DISPATCH_TM
    rows = _moe_rows(hn_a.shape[0] + hn_b.shape[0])
    grid_spec = pltpu.PrefetchScalarGridSpec(
        num_scalar_prefetch=5,
        grid=(a_tiles + b_tiles,),
        in_specs=[pl.BlockSpec((DISPATCH_TM, D_MODEL), lambda i, *_: (jnp.minimum(i, a_tiles - 1), 0)),
                  pl.BlockSpec((DISPATCH_TM, D_MODEL), lambda i, *_: (jnp.maximum(i - a_tiles, 0), 0))],
        out_specs=pl.BlockSpec(memory_space=pl.ANY),
        scratch_shapes=[pltpu.VMEM((MOE_TM, D_MODEL), F32), pltpu.SemaphoreType.DMA(()), pltpu.SemaphoreType.DMA(())],
    )
    return pl.pallas_call(
        functools.partial(_dispatch_kernel, n_blocks=rows // MOE_TM, a_tiles=a_tiles),
        grid_spec=grid_spec,
        out_shape=jax.ShapeDtypeStruct((rows, D_MODEL), F32),
        compiler_params=_params(("arbitrary",)),
        name="dispatch",
    )(dest0, dest1, pad_end, counts, n_used, hn_a, hn_b)


MOE_UNITS = 8
MOE_UG = D_MODEL // MOE_UNITS
MOE_UD = EXPERT_FF // MOE_UNITS


def _moe_kernel(blk_e_ref, nu_ref, nxt_ref, upb_ref, xs_ref, wg_hbm, wu_hbm, wd_hbm, ys_ref,
                wg_bf, wu_bf, wd_bf, stg_g, stg_u, stg_d, sem, st_ref):
    i = pl.program_id(0)
    cur_slot, pos, cur_e = 0, 1, 2

    def unit_copies(e, unit, s):
        g_rows = pl.ds(pl.multiple_of(unit * MOE_UG, MOE_UG), MOE_UG)
        d_rows = pl.ds(pl.multiple_of(unit * MOE_UD, MOE_UD), MOE_UD)
        return (pltpu.make_async_copy(wg_hbm.at[e, g_rows, :], stg_g.at[s], sem.at[s]),
                pltpu.make_async_copy(wu_hbm.at[e, g_rows, :], stg_u.at[s], sem.at[s]),
                pltpu.make_async_copy(wd_hbm.at[e, d_rows, :], stg_d.at[s], sem.at[s]))

    def start_unit(e, unit):
        for cp in unit_copies(e, unit, unit % 2):
            cp.start()

    def begin_load(e):
        st_ref[pos] = 0
        start_unit(e, 0)
        start_unit(e, 1)

    def advance(e, slot, n):
        def body(_, c):
            unit = st_ref[pos]

            @pl.when(unit < MOE_UNITS)
            def _():
                s = unit % 2
                for cp in unit_copies(e, unit, s):
                    cp.wait()
                g_rows = pl.ds(pl.multiple_of(unit * MOE_UG, MOE_UG), MOE_UG)
                d_rows = pl.ds(pl.multiple_of(unit * MOE_UD, MOE_UD), MOE_UD)
                wg_bf[slot, g_rows, :] = stg_g[s].astype(BF16)
                wu_bf[slot, g_rows, :] = stg_u[s].astype(BF16)
                wd_bf[slot, d_rows, :] = stg_d[s].astype(BF16)

                @pl.when(unit + 2 < MOE_UNITS)
                def _():
                    start_unit(e, unit + 2)
                st_ref[pos] = unit + 1
            return c
        lax.fori_loop(0, n, body, 0)

    def load_next(nxt):
        @pl.when(nxt >= 0)
        def _():
            begin_load(nxt)

        @pl.when(nxt < 0)
        def _():
            st_ref[pos] = MOE_UNITS

    @pl.when(i < nu_ref[0])
    def _():
        e = blk_e_ref[i]
        nxt = nxt_ref[i]

        @pl.when(i == 0)
        def _():
            st_ref[cur_slot] = 0
            st_ref[cur_e] = e
            begin_load(e)
            advance(e, 0, MOE_UNITS)
            load_next(nxt)

        @pl.when(jnp.logical_and(i > 0, e != st_ref[cur_e]))
        def _():
            slot = 1 - st_ref[cur_slot]
            advance(e, slot, MOE_UNITS)
            st_ref[cur_slot] = slot
            st_ref[cur_e] = e
            load_next(nxt)

        slot = st_ref[cur_slot]
        x = xs_ref[...].astype(BF16)
        hg = _dot(x, wg_bf[slot])
        hu = _dot(x, wu_bf[slot])
        hmid = (jax.nn.silu(hg) * hu).astype(BF16)
        ys_ref[...] = _dot(hmid, wd_bf[slot])

        @pl.when(nxt >= 0)
        def _():
            advance(nxt, 1 - slot, upb_ref[i])

    @pl.when(i >= nu_ref[0])
    def _():
        ys_ref[...] = jnp.zeros_like(ys_ref)


def moe(xs, blk_e, n_used, nxt_e, units_per_block, w_gate, w_up, w_down):
    rows = xs.shape[0]
    grid_spec = pltpu.PrefetchScalarGridSpec(
        num_scalar_prefetch=4,
        grid=(rows // MOE_TM,),
        in_specs=[pl.BlockSpec((MOE_TM, D_MODEL), lambda i, be, nu, nx, ub: (jnp.minimum(i, nu[0] - 1), 0)),
                  pl.BlockSpec(memory_space=pl.ANY), pl.BlockSpec(memory_space=pl.ANY), pl.BlockSpec(memory_space=pl.ANY)],
        out_specs=pl.BlockSpec((MOE_TM, D_MODEL), lambda i, be, nu, nx, ub: (i, 0)),
        scratch_shapes=[pltpu.VMEM((2, D_MODEL, EXPERT_FF), BF16), pltpu.VMEM((2, D_MODEL, EXPERT_FF), BF16),
                        pltpu.VMEM((2, EXPERT_FF, D_MODEL), BF16),
                        pltpu.VMEM((2, MOE_UG, EXPERT_FF), F32), pltpu.VMEM((2, MOE_UG, EXPERT_FF), F32),
                        pltpu.VMEM((2, MOE_UD, D_MODEL), F32),
                        pltpu.SemaphoreType.DMA((2,)), pltpu.SMEM((3,), I32)],
    )
    return pl.pallas_call(
        _moe_kernel,
        grid_spec=grid_spec,
        out_shape=jax.ShapeDtypeStruct((rows, D_MODEL), F32),
        compiler_params=_params(("arbitrary",)),
        name="moe",
    )(blk_e, n_used, nxt_e, units_per_block, xs, w_gate, w_up, w_down)


def _combine_kernel(r0_ref, r1_ref, ys_hbm, h_ref, w_ref, o_ref, buf, sem, *, tm, tok0):
    i = pl.program_id(0)

    def issue(block, slot):
        base = tok0 + block * tm

        def body(r, carry):
            for k, idx_ref in enumerate((r0_ref, r1_ref)):
                pltpu.make_async_copy(ys_hbm.at[pl.ds(idx_ref[base + r], 1)], buf.at[slot, k, pl.ds(r, 1)],
                                      sem.at[slot]).start()
            return carry
        lax.fori_loop(0, tm, body, 0, unroll=8)

    @pl.when(i == 0)
    def _():
        issue(0, 0)

    @pl.when(i + 1 < pl.num_programs(0))
    def _():
        issue(i + 1, (i + 1) % 2)

    slot = i % 2
    _wait_rows(ys_hbm, buf.at[slot, 0], sem.at[slot], tm)
    _wait_rows(ys_hbm, buf.at[slot, 1], sem.at[slot], tm)
    w = w_ref[...]
    o_ref[...] = h_ref[...] + (buf[slot, 0] * w[:, 0:1] + buf[slot, 1] * w[:, 1:2])


def combine(ys, h, route_w, rows0, rows1, *, tok0, tm=256):
    n_tok = h.shape[0]
    grid_spec = pltpu.PrefetchScalarGridSpec(
        num_scalar_prefetch=2,
        grid=(n_tok // tm,),
        in_specs=[pl.BlockSpec(memory_space=pl.ANY),
                  pl.BlockSpec((tm, D_MODEL), lambda i, a, b: (i, 0)),
                  pl.BlockSpec((tm, LANES), lambda i, a, b: (i, 0))],
        out_specs=pl.BlockSpec((tm, D_MODEL), lambda i, a, b: (i, 0)),
        scratch_shapes=[pltpu.VMEM((2, 2, tm, D_MODEL), F32), pltpu.SemaphoreType.DMA((2,))],
    )
    return pl.pallas_call(
        functools.partial(_combine_kernel, tm=tm, tok0=tok0),
        grid_spec=grid_spec,
        out_shape=jax.ShapeDtypeStruct((n_tok, D_MODEL), F32),
        compiler_params=_params(("arbitrary",)),
        name="combine",
    )(rows0, rows1, ys, h, route_w)


def _regroup_w_in(w_in):
    sizes = (SSM_WIDTH, ATTN_WIDTH, KV_WIDTH, KV_WIDTH, IDX_HEADS * IDX_DIM, IDX_DIM, IDX_HEADS, D_MODEL, D_MODEL)
    u, q, k, v, qi, ki, wi, ga, gb = jnp.split(w_in, np.cumsum(sizes)[:-1].tolist(), axis=1)
    pad = jnp.zeros((D_MODEL, PROJ_COLS - COL_KIWI - IDX_DIM - IDX_HEADS), F32)
    return jnp.concatenate([u, q, ga, gb, k, v, qi, ki, wi, pad], axis=1).astype(BF16)


def _layer(x_p, x_s, cache_k, cache_v, cache_ki, h0_re, h0_im, p):
    bp, tp, _ = x_p.shape
    bs, ts, _ = x_s.shape
    past = cache_k.shape[1]
    n_p, n_s = bp * tp, bs * ts
    n_tok = n_p + n_s

    w_in = _regroup_w_in(p['w_in'])
    ssm_w = _ssm_weights(p['ssm_A_re'], p['ssm_A_im'], p['ssm_log_dt'], p['ssm_B_re'], p['ssm_B_im'],
                         p['ssm_C_re'], p['ssm_C_im'])
    glu_w = (p['w_glu_val'].astype(BF16), p['w_glu_gate'].astype(BF16), p['w_attn_branch'].astype(BF16))
    w_out = p['w_out'].astype(BF16)
    router_w = _router_weights(p['w_router_group'], p['b_router_group'], p['w_router_expert'], p['b_router_expert'])
    seq_tiles = tp // QK_TM

    def front(x, table_pos, table_block):
        proj = in_proj(x, p['norm_mix_g'][None, :], w_in)
        return proj, qk_post(proj, table_pos, table_block, p['q_norm_g'], p['k_norm_g'], p['idx_k_norm_g'])

    def seqs(a, b, t):
        return a.reshape(b, t, a.shape[-1])

    xp = x_p.reshape(n_p, D_MODEL)
    proj_p, (q_b, kf_p, k_b, vf_p, v_b, qi_b, kif_p, ki_b, wi) = front(
        xp, jnp.arange(tp, dtype=I32), lambda i: i % seq_tiles)
    g_p, sre_p, sim_p = ssm(proj_p, ssm_w, p['ssm_D'], jnp.zeros((bp, SSM_LB, 2, SSM_SB), F32),
                            n_batch=bp, seq=tp, row0=0)
    bq = 128
    n_buckets = min(16, tp // bq)
    per = tp // bq // n_buckets
    qp, qip, wip = seqs(q_b, bp, tp), seqs(qi_b, bp, tp), seqs(wi, bp, tp)
    kp, vp, kip = seqs(k_b, bp, tp), seqs(v_b, bp, tp), seqs(ki_b, bp, tp)
    attn_p = jnp.concatenate(
        [dsa(qp, qip, wip, kp, vp, kip, bq=bq, q_blk0=n * per, n_qblk=per, n_keys=(n + 1) * per * bq,
             n_sel=min(IDX_TOPK, tp // 4), packed_bisect=False, stack=(1 if n % 2 == 0 else 2))
         for n in range(n_buckets)], axis=1).reshape(n_p, ATTN_WIDTH)
    merged_p = merge(g_p, attn_p, proj_p, *glu_w)
    h_p, hn_p, ri_p, rw_p, cnt_p = out_proj(xp, merged_p, w_out, p['norm_ffn_g'], router_w, jnp.zeros((1, LANES), F32))

    xs_ = x_s.reshape(n_s, D_MODEL)
    proj_s, (q_b, kf_s, k_b, vf_s, v_b, qi_b, kif_s, ki_b, wi) = front(
        xs_, jnp.tile(past + jnp.arange(ts, dtype=I32), QK_TM // ts), lambda i: 0)
    h0 = jnp.stack([h0_re.reshape(bs, SSM_LB, SSM_SB), h0_im.reshape(bs, SSM_LB, SSM_SB)]).transpose(2, 0, 1, 3)
    g_s, sre_s, sim_s = ssm_step(proj_s, ssm_w, p['ssm_D'], h0, n_batch=bs, seq=ts, row0=0)
    attn_s = dsa_step(seqs(q_b, bs, ts), seqs(qi_b, bs, ts), seqs(wi, bs, ts),
                      cache_k, cache_v, cache_ki,
                      seqs(k_b, bs, ts), seqs(v_b, bs, ts), seqs(ki_b, bs, ts),
                      n_sel=min(IDX_TOPK, (past + ts) // 4)).reshape(n_s, ATTN_WIDTH)
    merged_s = merge(g_s, attn_s, proj_s, *glu_w)
    h_s, hn_s, ri_s, rw_s, cnt = out_proj(xs_, merged_s, w_out, p['norm_ffn_g'], router_w, cnt_p)

    counts = cnt[0, N_EXPERT_GROUPS:ROUTER_COLS].astype(I32)
    pad_start, pad_end, n_used = _block_layout(counts)
    route_i = jnp.concatenate([ri_p, ri_s], axis=1)
    dest0 = pad_start[route_i[0]] + route_i[2]
    dest1 = pad_start[route_i[1]] + route_i[3]
    n_blocks = _moe_rows(n_tok) // MOE_TM
    blk = jnp.minimum(jnp.arange(n_blocks, dtype=I32), n_used - 1)
    blk_e = jnp.minimum(jnp.sum((pad_end[None, :] <= (blk * MOE_TM)[:, None]).astype(I32), axis=1), N_EXPERTS - 1)
    after = pad_end[blk_e] // MOE_TM
    nxt_e = jnp.where(after < n_used, blk_e[jnp.minimum(after, n_blocks - 1)], -1).astype(I32)
    blocks_of_e = jnp.maximum((pad_end - pad_start)[blk_e] // MOE_TM, 1)
    units_per_block = ((MOE_UNITS + blocks_of_e - 1) // blocks_of_e).astype(I32)
    n_used = n_used.reshape(1)

    xs = dispatch(hn_p, hn_s, dest0, dest1, pad_end, counts, n_used)
    ys = moe(xs, blk_e, n_used, nxt_e, units_per_block, p['w_exp_gate'], p['w_exp_up'], p['w_exp_down'])
    y_p = combine(ys, h_p, rw_p, dest0, dest1, tok0=0).reshape(bp, tp, D_MODEL)
    y_s = combine(ys, h_s, rw_s, dest0, dest1, tok0=n_p).reshape(bs, ts, D_MODEL)

    def heads(a, b, t):
        return a.reshape(b, t, N_KV_HEADS, HEAD_DIM)

    new_p = (heads(kf_p, bp, tp), heads(vf_p, bp, tp), kif_p.reshape(bp, tp, IDX_DIM), sre_p, sim_p)
    new_s = (heads(kf_s, bs, ts), heads(vf_s, bs, ts), kif_s.reshape(bs, ts, IDX_DIM), sre_s, sim_s)
    return y_p, y_s, new_p, new_s


def kernel(x_prompt, x_sample, cache_k, cache_v, cache_idx_k, state_ssm_re, state_ssm_im, norm_mix_g, w_in, q_norm_g, k_norm_g, idx_k_norm_g, ssm_A_re, ssm_A_im, ssm_log_dt, ssm_B_re, ssm_B_im, ssm_C_re, ssm_C_im, ssm_D, w_glu_val, w_glu_gate, w_attn_branch, w_out, norm_ffn_g, w_router_group, b_router_group, w_router_expert, b_router_expert, w_exp_gate, w_exp_up, w_exp_down):
    depth = w_in.shape[0]
    assert depth == 1, "prompt and sample tokens are batched through one layer"
    names = ('norm_mix_g', 'w_in', 'q_norm_g', 'k_norm_g', 'idx_k_norm_g', 'ssm_A_re', 'ssm_A_im', 'ssm_log_dt',
             'ssm_B_re', 'ssm_B_im', 'ssm_C_re', 'ssm_C_im', 'ssm_D', 'w_glu_val', 'w_glu_gate', 'w_attn_branch',
             'w_out', 'norm_ffn_g', 'w_router_group', 'b_router_group', 'w_router_expert', 'b_router_expert',
             'w_exp_gate', 'w_exp_up', 'w_exp_down')
    vals = (norm_mix_g, w_in, q_norm_g, k_norm_g, idx_k_norm_g, ssm_A_re, ssm_A_im, ssm_log_dt, ssm_B_re, ssm_B_im,
            ssm_C_re, ssm_C_im, ssm_D, w_glu_val, w_glu_gate, w_attn_branch, w_out, norm_ffn_g, w_router_group,
            b_router_group, w_router_expert, b_router_expert, w_exp_gate, w_exp_up, w_exp_down)
    p = {n: v[0] for n, v in zip(names, vals)}
    y_p, y_s, new_p, new_s = _layer(x_prompt, x_sample, cache_k[0], cache_v[0], cache_idx_k[0],
                                    state_ssm_re[0], state_ssm_im[0], p)
    st_p = tuple(a[None] for a in new_p)
    st_s = tuple(a[None] for a in new_s)
    return (y_p, y_s) + st_p + st_s
```

```python
import functools

import numpy as np
import jax
import jax.numpy as jnp
from jax import lax
from jax.experimental import pallas as pl
from jax.experimental.pallas import tpu as pltpu

F32 = jnp.float32
BF16 = jnp.bfloat16
I32 = jnp.int32

D_MODEL = 2048
CHUNK = 64
SSM_WIDTH = 1024
SSM_GROUP = 16
SSM_GROUPS = 64
SSM_STATE = 64
ATTN_WIDTH = 1024
HEAD_DIM = 128
N_HEADS = 8
N_KV_HEADS = 2
KV_GROUP = 4
IDX_HEADS = 8
IDX_DIM = 64
IDX_TOPK = 256
ROPE_THETA = 500000.0
N_EXPERT_GROUPS = 4
EXPERTS_PER_GROUP = 8
N_EXPERTS = 32
TOP_K = 2
EXPERT_FF = 1024
EPS = 1e-6

LANES = 128
SUBLANES = 8
VMEM_LIMIT = 56 * 1024 * 1024

COL_U, COL_Q, COL_GA, COL_GB, COL_K, COL_V, COL_QI, COL_KIWI = 0, 1024, 2048, 4096, 6144, 6400, 6656, 7168
PROJ_COLS = 7296
PROJ_TN = 2432
KV_WIDTH = N_KV_HEADS * HEAD_DIM

SSM_LB = SSM_WIDTH // LANES
SSM_SB = 8 * SSM_STATE

INT_MIN = np.int32(-2 ** 31)
KEY_NEG_INF = np.int32(np.array([0xFF800000], np.uint32).view(np.int32)[0] ^ 0x7FFFFFFF)


def _params(sem, vmem=VMEM_LIMIT):
    return pltpu.CompilerParams(dimension_semantics=sem, vmem_limit_bytes=vmem)


def _dot(a, b):
    return jnp.dot(a, b, preferred_element_type=F32)


def _dot_nt(a, b):
    return lax.dot_general(a, b, (((1,), (1,)), ((), ())), preferred_element_type=F32)


def _split_bf16(x):
    hi = x.astype(BF16)
    lo = (x - hi.astype(F32)).astype(BF16)
    return hi, lo


def _in_proj_kernel(x_ref, g_ref, w_ref, o_ref, xn_ref):
    @pl.when(pl.program_id(1) == 0)
    def _():
        x = x_ref[...]
        ms = jnp.mean(x * x, axis=-1, keepdims=True)
        xn_ref[...] = (x * lax.rsqrt(ms + EPS) * g_ref[...]).astype(BF16)

    o_ref[...] = _dot(xn_ref[...], w_ref[...])


def in_proj(x, gain, w_bf16, *, tm=512):
    n_tok = x.shape[0]
    return pl.pallas_call(
        _in_proj_kernel,
        grid=(n_tok // tm, PROJ_COLS // PROJ_TN),
        in_specs=[pl.BlockSpec((tm, D_MODEL), lambda i, j: (i, 0)),
                  pl.BlockSpec((1, D_MODEL), lambda i, j: (0, 0)),
                  pl.BlockSpec((D_MODEL, PROJ_TN), lambda i, j: (0, j))],
        out_specs=pl.BlockSpec((tm, PROJ_TN), lambda i, j: (i, j)),
        out_shape=jax.ShapeDtypeStruct((n_tok, PROJ_COLS), F32),
        scratch_shapes=[pltpu.VMEM((tm, D_MODEL), BF16)],
        compiler_params=_params(("arbitrary", "arbitrary")),
        name="in_proj",
    )(x, gain, w_bf16)


def _rope(x, c, s_lo, s_hi, half):
    n = x.shape[-1]
    return x * c + pltpu.roll(x, n - half, 1) * s_lo + pltpu.roll(x, half, 1) * s_hi


def _head_norm(x, g):
    ms = jnp.mean(x * x, axis=-1, keepdims=True)
    return x * lax.rsqrt(ms + EPS) * g


V_AUG = 2 * HEAD_DIM


def _store_v_aug(dst_ref, row0, v_heads):
    n = v_heads[0].shape[0]
    one_col = jnp.where(lax.broadcasted_iota(I32, (n, HEAD_DIM), 1) == 0, 1.0, 0.0).astype(BF16)
    for h, v in enumerate(v_heads):
        dst_ref[row0:row0 + n, h * V_AUG:h * V_AUG + HEAD_DIM] = v.astype(BF16)
        dst_ref[row0:row0 + n, h * V_AUG + HEAD_DIM:(h + 1) * V_AUG] = one_col


def _qk_post_kernel(q_ref, k_ref, v_ref, qi_ref, kw_ref, c128_ref, sl128_ref, sh128_ref,
                    c64_ref, sl64_ref, sh64_ref, qg_ref, kg_ref, ig_ref,
                    qo_ref, kf_ref, kb_ref, vf_ref, vb_ref, qio_ref, kif_ref, kib_ref, wo_ref):
    c128, sl128, sh128 = c128_ref[...], sl128_ref[...], sh128_ref[...]
    c64, sl64, sh64 = c64_ref[...], sl64_ref[...], sh64_ref[...]
    half128 = HEAD_DIM // 8
    half64 = IDX_DIM // 8
    for h in range(N_HEADS):
        sl = slice(h * LANES, (h + 1) * LANES)
        qo_ref[:, sl] = _rope(_head_norm(q_ref[:, sl], qg_ref[...]), c128, sl128, sh128, half128).astype(BF16)
    for h in range(N_KV_HEADS):
        sl = slice(h * LANES, (h + 1) * LANES)
        kk = _rope(_head_norm(k_ref[:, sl], kg_ref[...]), c128, sl128, sh128, half128)
        kf_ref[:, sl] = kk
        kb_ref[:, sl] = kk.astype(BF16)
    v = v_ref[...]
    vf_ref[...] = v
    _store_v_aug(vb_ref, 0, [v[:, h * HEAD_DIM:(h + 1) * HEAD_DIM] for h in range(N_KV_HEADS)])
    lane = lax.broadcasted_iota(I32, c64.shape, 1)
    low = lane < IDX_DIM
    for p in range(IDX_HEADS // 2):
        x = _rope(qi_ref[:, p * LANES:(p + 1) * LANES], c64, sl64, sh64, half64)
        qio_ref[:, (2 * p) * LANES:(2 * p + 1) * LANES] = jnp.where(low, x, 0.0).astype(BF16)
        qio_ref[:, (2 * p + 1) * LANES:(2 * p + 2) * LANES] = jnp.where(low, pltpu.roll(x, IDX_DIM, 1), 0.0).astype(BF16)
    kw = kw_ref[...]
    ms = jnp.sum(jnp.where(low, kw * kw, 0.0), axis=-1, keepdims=True) * (1.0 / IDX_DIM)
    ki = _rope(kw * lax.rsqrt(ms + EPS) * ig_ref[...], c64, sl64, sh64, half64)
    kif_ref[...] = ki[:, :IDX_DIM]
    kib_ref[...] = jnp.where(low, ki, 0.0).astype(BF16)
    wo_ref[...] = (pltpu.roll(kw, IDX_DIM, 1) * IDX_HEADS ** -0.5) * IDX_DIM ** -0.5


def _rope_tables(pos, head_dim):
    r = head_dim // 4
    half = r // 2
    inv = ROPE_THETA ** (-jnp.arange(half, dtype=F32) * 2.0 / r)
    ang = pos.astype(F32)[:, None] * inv[None, :]
    cos, sin = jnp.cos(ang), jnp.sin(ang)
    n = pos.shape[0]
    zh = jnp.zeros((n, half), F32)
    rest = head_dim - r
    c = jnp.concatenate([cos, cos, jnp.ones((n, rest), F32)], axis=-1)
    s_lo = jnp.concatenate([-sin, zh, jnp.zeros((n, rest), F32)], axis=-1)
    s_hi = jnp.concatenate([zh, sin, jnp.zeros((n, rest), F32)], axis=-1)
    rep = LANES // head_dim
    return tuple(jnp.tile(t, (1, rep)) for t in (c, s_lo, s_hi))


QK_TM = 512


def qk_post(proj, table_pos, table_block, q_gain, k_gain, ik_gain):
    tm = QK_TM
    n_tok = proj.shape[0]
    t128 = _rope_tables(table_pos, HEAD_DIM)
    t64 = _rope_tables(table_pos, IDX_DIM)
    ik_gain128 = jnp.concatenate([ik_gain, jnp.zeros((LANES - IDX_DIM,), F32)])[None, :]

    def col(width, start):
        return pl.BlockSpec((tm, width), lambda i: (i, start // width))

    def row(width):
        return pl.BlockSpec((tm, width), lambda i: (i, 0))

    table = pl.BlockSpec((tm, LANES), lambda i: (table_block(i), 0))
    gain = pl.BlockSpec((1, LANES), lambda i: (0, 0))
    return pl.pallas_call(
        _qk_post_kernel,
        grid=(n_tok // tm,),
        in_specs=[col(ATTN_WIDTH, COL_Q), col(KV_WIDTH, COL_K), col(KV_WIDTH, COL_V), col(IDX_HEADS * IDX_DIM, COL_QI),
                  col(LANES, COL_KIWI)] + [table] * 6 + [gain] * 3,
        out_specs=[row(ATTN_WIDTH), row(KV_WIDTH), row(KV_WIDTH), row(KV_WIDTH), row(N_KV_HEADS * V_AUG), row(IDX_HEADS * LANES),
                   row(IDX_DIM), row(LANES), row(LANES)],
        out_shape=[jax.ShapeDtypeStruct((n_tok, ATTN_WIDTH), BF16),
                   jax.ShapeDtypeStruct((n_tok, KV_WIDTH), F32), jax.ShapeDtypeStruct((n_tok, KV_WIDTH), BF16),
                   jax.ShapeDtypeStruct((n_tok, KV_WIDTH), F32), jax.ShapeDtypeStruct((n_tok, N_KV_HEADS * V_AUG), BF16),
                   jax.ShapeDtypeStruct((n_tok, IDX_HEADS * LANES), BF16),
                   jax.ShapeDtypeStruct((n_tok, IDX_DIM), F32), jax.ShapeDtypeStruct((n_tok, LANES), BF16),
                   jax.ShapeDtypeStruct((n_tok, LANES), F32)],
        compiler_params=_params(("arbitrary",)),
        name="qk_post",
    )(proj, proj, proj, proj, proj, *t128, *t64, q_gain[None, :], k_gain[None, :], ik_gain128)


def _gelu_tanh(x):
    return 0.5 * x * (1.0 + jnp.tanh(np.float32(np.sqrt(2.0 / np.pi)) * (x + 0.044715 * (x * x * x))))


SSM_LT = SSM_SB // LANES
SSM_SEG = 64


def _ssm_kernel(u_ref, wb_ref, wc_ref, pw_ref, d_ref, h0_ref, g_ref, sre_ref, sim_ref,
                er_ref, ei_ref, car_ref, up_ref, yp_ref):
    c = pl.program_id(2)

    @pl.when(c == 0)
    def _():
        car_ref[...] = h0_ref[...]

    for j in range(SSM_SEG):
        up_ref[j * SUBLANES:(j + 1) * SUBLANES, :] = u_ref[pl.ds(j, SUBLANES, stride=SSM_SEG), :]
    e = _dot(up_ref[...].astype(BF16), wb_ref[...])
    tiles = [slice(lt * LANES, (lt + 1) * LANES) for lt in range(SSM_LT)]
    for lt, sl in enumerate(tiles):
        er_ref[lt] = e[:, sl]
        ei_ref[lt] = e[:, SSM_SB + lt * LANES:SSM_SB + (lt + 1) * LANES]

    def cmul_add(ar, ai, br, bi, cr, ci):
        return ar * br - ai * bi + cr, ar * bi + ai * br + ci

    lb = [(pw_ref[0, 0:1, sl], pw_ref[1, 0:1, sl]) for sl in tiles]
    zero = jnp.zeros((SUBLANES, LANES), F32)
    st = [(zero, zero)] * SSM_LT
    for j in range(SSM_SEG):
        rows = slice(j * SUBLANES, (j + 1) * SUBLANES)
        for lt in range(SSM_LT):
            st[lt] = cmul_add(*lb[lt], *st[lt], er_ref[lt, rows, :], ei_ref[lt, rows, :])
            er_ref[lt, rows, :] = st[lt][0]
            ei_ref[lt, rows, :] = st[lt][1]

    enter = []
    for lt, sl in enumerate(tiles):
        seg_r, seg_i = pw_ref[0, SSM_SEG - 1:SSM_SEG, sl], pw_ref[1, SSM_SEG - 1:SSM_SEG, sl]
        cr, ci = car_ref[0:1, sl], car_ref[1:2, sl]
        rows_r, rows_i = [], []
        for r in range(SUBLANES):
            rows_r.append(cr)
            rows_i.append(ci)
            cr, ci = cmul_add(seg_r, seg_i, cr, ci, st[lt][0][r:r + 1], st[lt][1][r:r + 1])
        car_ref[0:1, sl] = cr
        car_ref[1:2, sl] = ci
        enter.append((jnp.concatenate(rows_r, axis=0), jnp.concatenate(rows_i, axis=0)))

    for j in range(SSM_SEG):
        rows = slice(j * SUBLANES, (j + 1) * SUBLANES)
        for lt, sl in enumerate(tiles):
            xr, xi = cmul_add(pw_ref[0, j:j + 1, sl], pw_ref[1, j:j + 1, sl], *enter[lt],
                              er_ref[lt, rows, :], ei_ref[lt, rows, :])
            er_ref[lt, rows, :] = xr
            ei_ref[lt, rows, :] = xi

    y = None
    for lt, sl in enumerate(tiles):
        t = _dot(er_ref[lt].astype(BF16), wc_ref[0, sl, :]) - _dot(ei_ref[lt].astype(BF16), wc_ref[1, sl, :])
        y = t if y is None else y + t
    yp_ref[...] = y
    out_rows = 2 * SUBLANES
    for t0 in range(0, SUBLANES * SSM_SEG, out_rows):
        r, j0 = divmod(t0, SSM_SEG)
        rows = slice(t0, t0 + out_rows)
        yt = yp_ref[pl.ds(j0 * SUBLANES + r, out_rows, stride=SUBLANES), :] + d_ref[...] * u_ref[rows, :]
        g_ref[rows, :] = _gelu_tanh(yt).astype(BF16)

    @pl.when(c == pl.num_programs(2) - 1)
    def _():
        sre_ref[...] = car_ref[0:1, :]
        sim_ref[...] = car_ref[1:2, :]


def _ssm_weights(a_re, a_im, log_dt, b_re, b_im, c_re, c_im):
    lam_re, lam_im = a_re, a_im
    dt = jnp.exp(log_dt)[:, None]
    mag = jnp.exp(lam_re * dt)
    lb_re, lb_im = mag * jnp.cos(lam_im * dt), mag * jnp.sin(lam_im * dt)
    den = lam_re * lam_re + lam_im * lam_im
    num_re = lb_re - 1.0
    z_re = (num_re * lam_re + lb_im * lam_im) / den
    z_im = (lb_im * lam_re - num_re * lam_im) / den
    zb_re = z_re[:, :, None] * b_re - z_im[:, :, None] * b_im
    zb_im = z_re[:, :, None] * b_im + z_im[:, :, None] * b_re
    eye = jnp.eye(8, dtype=F32)

    def blockdiag_in(w):
        return jnp.einsum('jgph,gk->jghkp', w.reshape(SSM_LB, 8, SSM_STATE, SSM_GROUP), eye).reshape(SSM_LB, LANES, SSM_SB)

    def blockdiag_out(w):
        return jnp.einsum('jghp,gk->jkpgh', w.reshape(SSM_LB, 8, SSM_GROUP, SSM_STATE), eye).reshape(SSM_LB, SSM_SB, LANES)

    wb = jnp.concatenate([blockdiag_in(zb_re), blockdiag_in(zb_im)], axis=-1).astype(BF16)
    wc = jnp.stack([blockdiag_out(c_re), blockdiag_out(c_im)], axis=1).astype(BF16)

    pr, pi_ = lb_re.reshape(SSM_LB, 1, SSM_SB), lb_im.reshape(SSM_LB, 1, SSM_SB)
    while pr.shape[1] < SSM_SEG:
        tr, ti = pr[:, -1:], pi_[:, -1:]
        pr, pi_ = (jnp.concatenate([pr, pr * tr - pi_ * ti], axis=1), jnp.concatenate([pi_, pr * ti + pi_ * tr], axis=1))
    pw = jnp.stack([pr, pi_], axis=1)
    return wb, wc, pw


def ssm(proj, ssm_w, d_skip, h0, *, n_batch, seq, row0):
    wb, wc, pw = ssm_w
    tc = SUBLANES * SSM_SEG
    n_chunks = seq // tc
    blk0 = row0 // tc
    n_tok = n_batch * seq
    state_shape = jax.ShapeDtypeStruct((n_batch, SSM_LB, 1, SSM_SB), F32)
    state_spec = pl.BlockSpec((None, None, 1, SSM_SB), lambda b, j, c: (b, j, 0, 0))
    g, s_re, s_im = pl.pallas_call(
        _ssm_kernel,
        grid=(n_batch, SSM_LB, n_chunks),
        in_specs=[pl.BlockSpec((tc, LANES), lambda b, j, c: (blk0 + b * n_chunks + c, j)),
                  pl.BlockSpec((None, LANES, 2 * SSM_SB), lambda b, j, c: (j, 0, 0)),
                  pl.BlockSpec((None, 2, SSM_SB, LANES), lambda b, j, c: (j, 0, 0, 0)),
                  pl.BlockSpec((None, 2, SSM_SEG, SSM_SB), lambda b, j, c: (j, 0, 0, 0)),
                  pl.BlockSpec((1, LANES), lambda b, j, c: (0, j)),
                  pl.BlockSpec((None, None, 2, SSM_SB), lambda b, j, c: (b, j, 0, 0))],
        out_specs=[pl.BlockSpec((tc, LANES), lambda b, j, c: (b * n_chunks + c, j)), state_spec, state_spec],
        out_shape=[jax.ShapeDtypeStruct((n_tok, SSM_WIDTH), BF16), state_shape, state_shape],
        scratch_shapes=[pltpu.VMEM((SSM_LT, tc, LANES), F32), pltpu.VMEM((SSM_LT, tc, LANES), F32),
                        pltpu.VMEM((2, SSM_SB), F32), pltpu.VMEM((tc, LANES), F32), pltpu.VMEM((tc, LANES), F32)],
        compiler_params=_params(("arbitrary", "arbitrary", "arbitrary")),
        name="ssm",
    )(proj, wb, wc, pw, d_skip[None, :], h0)
    return g, s_re.reshape(n_batch, SSM_GROUPS, SSM_STATE), s_im.reshape(n_batch, SSM_GROUPS, SSM_STATE)


def _ssm_step_kernel(u_ref, wb_ref, wc_ref, pw_ref, d_ref, h0_ref, g_ref, sre_ref, sim_ref, er_ref, ei_ref, *, seq):
    n_seq = h0_ref.shape[1]
    u = u_ref[...]
    e = _dot(u.astype(BF16), wb_ref[...])
    n_lt = SSM_SB // LANES
    y = d_ref[...] * u
    for lt in range(n_lt):
        sl = slice(lt * LANES, (lt + 1) * LANES)
        er_ref[...] = e[:, lt * LANES:(lt + 1) * LANES]
        ei_ref[...] = e[:, SSM_SB + lt * LANES:SSM_SB + (lt + 1) * LANES]
        lr, li = pw_ref[0, 0:1, sl], pw_ref[1, 0:1, sl]
        sr, si = h0_ref[0, :, sl], h0_ref[1, :, sl]
        for t in range(seq):
            rows = pl.ds(t, n_seq, stride=seq)
            sr, si = lr * sr - li * si + er_ref[rows, :], lr * si + li * sr + ei_ref[rows, :]
            er_ref[rows, :] = sr
            ei_ref[rows, :] = si
        y = y + (_dot(er_ref[...].astype(BF16), wc_ref[0, sl, :]) - _dot(ei_ref[...].astype(BF16), wc_ref[1, sl, :]))
        sre_ref[:, sl] = sr
        sim_ref[:, sl] = si
    g_ref[...] = _gelu_tanh(y).astype(BF16)


def ssm_step(proj, ssm_w, d_skip, h0, *, n_batch, seq, row0):
    wb, wc, pw = ssm_w
    n_tok = n_batch * seq
    assert row0 % n_tok == 0
    state_shape = jax.ShapeDtypeStruct((SSM_LB, n_batch, SSM_SB), F32)
    state_spec = pl.BlockSpec((None, n_batch, SSM_SB), lambda j: (j, 0, 0))
    g, s_re, s_im = pl.pallas_call(
        functools.partial(_ssm_step_kernel, seq=seq),
        grid=(SSM_LB,),
        in_specs=[pl.BlockSpec((n_tok, LANES), lambda j: (row0 // n_tok, j)),
                  pl.BlockSpec((None, LANES, 2 * SSM_SB), lambda j: (j, 0, 0)),
                  pl.BlockSpec((None, 2, SSM_SB, LANES), lambda j: (j, 0, 0, 0)),
                  pl.BlockSpec((None, 2, SSM_SEG, SSM_SB), lambda j: (j, 0, 0, 0)),
                  pl.BlockSpec((1, LANES), lambda j: (0, j)),
                  pl.BlockSpec((None, 2, n_batch, SSM_SB), lambda j: (j, 0, 0, 0))],
        out_specs=[pl.BlockSpec((n_tok, LANES), lambda j: (0, j)), state_spec, state_spec],
        out_shape=[jax.ShapeDtypeStruct((n_tok, SSM_WIDTH), BF16), state_shape, state_shape],
        scratch_shapes=[pltpu.VMEM((n_tok, LANES), F32), pltpu.VMEM((n_tok, LANES), F32)],
        compiler_params=_params(("arbitrary",)),
        name="ssm_step",
    )(proj, wb, wc, pw, d_skip[None, :], h0)

    def per_seq(s):
        return s.transpose(1, 0, 2).reshape(n_batch, SSM_GROUPS, SSM_STATE)

    return g, per_seq(s_re), per_seq(s_im)


def _row_sum(x):
    return jnp.sum(x, axis=1, keepdims=True)


def _row_count(mask):
    return _row_sum(jnp.where(mask, 1, 0))


I16 = jnp.int16
I16_MIN = -2 ** 15


def _count16(ref, cand, compare):
    accs = [None] * 4
    for t in range(ref.shape[1] // LANES):
        x = jnp.where(compare(ref[:, t * LANES:(t + 1) * LANES], cand), I16(1), I16(0))
        accs[t % 4] = x if accs[t % 4] is None else accs[t % 4] + x
    accs = [a for a in accs if a is not None]
    total = accs[0]
    for a in accs[1:]:
        total = total + a
    return _row_sum(total.astype(I32))


def _bisect16(ref, target):
    def step(i, base):
        cand = base + lax.shift_left(np.int32(1), np.int32(15) - i)
        cnt = _count16(ref, cand.astype(I16), lambda a, b: a >= b)
        return jnp.where(cnt >= target, cand, base)
    return lax.fori_loop(0, 16, step, jnp.full((ref.shape[0], 1), I16_MIN, I32))


def _bisect32(key_ref, n_sel):
    bq, n_keys = key_ref.shape
    hr = bq // 2

    def lane_counts(h, cand):
        accs = [None] * 4
        for t in range(n_keys // LANES):
            x = jnp.where(key_ref[h * hr:(h + 1) * hr, t * LANES:(t + 1) * LANES] >= cand, 1, 0)
            accs[t % 4] = x if accs[t % 4] is None else accs[t % 4] + x
        accs = [a for a in accs if a is not None]
        total = accs[0]
        for a in accs[1:]:
            total = total + a
        return total

    def decide(part, cand, base):
        return jnp.where(_row_sum(part) >= n_sel, cand, base)

    def bit(i):
        return lax.shift_left(np.int32(1), np.int32(31) - i)

    def body(i, state):
        base_a, base_b, part_b = state
        cand_a = base_a + bit(i)
        part_a = lane_counts(0, cand_a)
        base_b = decide(part_b, base_b + bit(i - 1), base_b)
        part_b = lane_counts(1, base_b + bit(i))
        return decide(part_a, cand_a, base_a), base_b, part_b

    base0 = jnp.full((hr, 1), INT_MIN, I32)
    first = base0 + bit(0)
    state = (decide(lane_counts(0, first), first, base0), base0, lane_counts(1, first))
    base_a, base_b, part_b = lax.fori_loop(1, 32, body, state)
    base_b = decide(part_b, base_b + bit(31), base_b)
    return jnp.concatenate([base_a, base_b], axis=0)


def _stack_heads(ref, heads):
    return jnp.concatenate([ref[:, h * LANES:(h + 1) * LANES] for h in heads], axis=0)


def _dsa_body(q_ref, qi_ref, wi_ref, k_ref, v_ref, ki_ref, o_ref, key_ref, bias_ref, hi_ref, lo_ref, p_ref,
              *, q_pos_first, s_valid, n_sel, packed_bisect, stack):
    bq, n_keys = key_ref.shape
    col = lax.broadcasted_iota(I32, (bq, n_keys), 1)
    qpos = q_pos_first + lax.broadcasted_iota(I32, (bq, 1), 0)
    allowed = col < jnp.minimum((qpos // CHUNK + 1) * CHUNK, s_valid)

    ki = ki_ref[...]
    score = None
    for h0 in range(0, IDX_HEADS, stack):
        d = _dot_nt(_stack_heads(qi_ref, range(h0, h0 + stack)), ki)
        for j in range(stack):
            t = jnp.maximum(d[j * bq:(j + 1) * bq], 0.0) * wi_ref[:, h0 + j:h0 + j + 1]
            score = t if score is None else score + t
    score = jnp.where(score == 0.0, 0.0, score)
    bits = pltpu.bitcast(score, I32)
    key = jnp.where(bits < 0, bits ^ np.int32(0x7FFFFFFF), bits)
    key = jnp.where(allowed, key, KEY_NEG_INF)
    key_ref[...] = key

    if packed_bisect:
        hi_ref[...] = (key >> 16).astype(I16)
        lo_ref[...] = ((key & 0xFFFF) + I16_MIN).astype(I16)
        thr_hi = _bisect16(hi_ref, n_sel)
        thr_hi16 = thr_hi.astype(I16)
        need_lo = n_sel - _count16(hi_ref, thr_hi16, lambda a, b: a > b)
        lo_ref[...] = jnp.where(hi_ref[...] == thr_hi16, lo_ref[...], I16(I16_MIN))
        thr_lo = _bisect16(lo_ref, need_lo)
        thr = lax.shift_left(thr_hi, np.int32(16)) + (thr_lo - I16_MIN)
    else:
        thr = _bisect32(key_ref, n_sel)
    thr = jnp.maximum(thr, KEY_NEG_INF)

    key = key_ref[...]
    need = n_sel - _row_count(key > thr)
    n_eq = _row_count(key == thr)
    n_bits = int(n_keys - 1).bit_length()

    def tie_cut():
        def step(i, j0):
            cand = j0 + lax.shift_left(np.int32(1), np.int32(n_bits - 1) - i)
            cnt = _row_sum(jnp.where(key_ref[...] == thr, jnp.where(col < cand, 1, 0), 0))
            return jnp.where(cnt < need, cand, j0)
        return lax.fori_loop(0, n_bits, step, jnp.zeros((bq, 1), I32))

    split = jnp.max(jnp.where(n_eq > need, 1, 0)) > 0
    j_last = lax.cond(split, tie_cut, lambda: jnp.full((bq, 1), n_keys, I32))
    tie_bias = jnp.where(thr == KEY_NEG_INF, -jnp.inf, 0.0)
    bias_ref[...] = jnp.where(key > thr, 0.0,
                              jnp.where(key == thr, jnp.where(col <= j_last, tie_bias, -jnp.inf), -jnp.inf))

    c = np.float32(HEAD_DIM ** -0.5 * np.log2(np.e))
    for h0 in range(0, N_HEADS, stack):
        kv = h0 // KV_GROUP
        heads = range(h0, h0 + stack)
        s_all = _dot_nt(_stack_heads(q_ref, heads), k_ref[:, kv * HEAD_DIM:(kv + 1) * HEAD_DIM])
        for g in range(stack):
            s = s_all[g * bq:(g + 1) * bq] + bias_ref[...]
            m = jnp.max(s, axis=1, keepdims=True)
            p_ref[g * bq:(g + 1) * bq, :] = jnp.exp2((s - m) * c).astype(BF16)
        pv = _dot(p_ref[0:stack * bq, :], v_ref[:, kv * V_AUG:(kv + 1) * V_AUG])
        for g, h in enumerate(heads):
            o = pv[g * bq:(g + 1) * bq]
            o_ref[:, h * HEAD_DIM:(h + 1) * HEAD_DIM] = (o[:, :HEAD_DIM] / o[:, HEAD_DIM:HEAD_DIM + 1]).astype(BF16)


def _dsa_scratch(bq, n_keys):
    return [pltpu.VMEM((bq, n_keys), I32), pltpu.VMEM((bq, n_keys), F32),
            pltpu.VMEM((bq, n_keys), I16), pltpu.VMEM((bq, n_keys), I16), pltpu.VMEM((KV_GROUP * bq, n_keys), BF16)]


def _dsa_kernel(q_ref, qi_ref, wi_ref, k_ref, v_ref, ki_ref, o_ref, *scratch, q_pos0, **static):
    bq = scratch[0].shape[0]
    _dsa_body(q_ref, qi_ref, wi_ref, k_ref, v_ref, ki_ref, o_ref, *scratch,
              q_pos_first=q_pos0 + pl.program_id(1) * bq, **static)


def dsa(q, qi, wi, k, v, ki, *, bq, q_blk0, n_qblk, n_keys, n_sel, packed_bisect, stack):
    n_batch, seq = q.shape[:2]

    def qspec(width):
        return pl.BlockSpec((None, bq, width), lambda b, i: (b, q_blk0 + i, 0))

    def kspec(width):
        return pl.BlockSpec((None, n_keys, width), lambda b, i: (b, 0, 0))

    return pl.pallas_call(
        functools.partial(_dsa_kernel, q_pos0=q_blk0 * bq, s_valid=seq, n_sel=n_sel, packed_bisect=packed_bisect,
                          stack=stack),
        grid=(n_batch, n_qblk),
        in_specs=[qspec(ATTN_WIDTH), qspec(IDX_HEADS * LANES), qspec(LANES), kspec(KV_WIDTH), kspec(N_KV_HEADS * V_AUG),
                  kspec(LANES)],
        out_specs=pl.BlockSpec((None, bq, ATTN_WIDTH), lambda b, i: (b, i, 0)),
        out_shape=jax.ShapeDtypeStruct((n_batch, n_qblk * bq, ATTN_WIDTH), BF16),
        scratch_shapes=_dsa_scratch(bq, n_keys),
        compiler_params=_params(("arbitrary", "arbitrary")),
        name="dsa",
    )(q, qi, wi, k, v, ki)


def _dsa_step_kernel(q_ref, qi_ref, wi_ref, ck_hbm, cv_hbm, cki_ref, nk_ref, nv_ref, nki_ref, o_ref,
                     k_buf, v_buf, ki_buf, cache_buf, sem, *scratch, past, n_sel):
    b = pl.program_id(0)

    def cache_copies(seq, slot):
        return [pltpu.make_async_copy(src.at[seq, :, h, :], cache_buf.at[slot, a, h], sem.at[slot])
                for a, src in enumerate((ck_hbm, cv_hbm)) for h in range(N_KV_HEADS)]

    @pl.when(b == 0)
    def _():
        for cp in cache_copies(0, 0):
            cp.start()

    @pl.when(b + 1 < pl.num_programs(0))
    def _():
        for cp in cache_copies(b + 1, (b + 1) % 2):
            cp.start()

    slot = b % 2
    for cp in cache_copies(b, slot):
        cp.wait()

    ts = nk_ref.shape[0]
    n_keys = k_buf.shape[0]
    for h in range(N_KV_HEADS):
        k_buf[0:past, h * HEAD_DIM:(h + 1) * HEAD_DIM] = cache_buf[slot, 0, h].astype(BF16)
    _store_v_aug(v_buf, 0, [cache_buf[slot, 1, h] for h in range(N_KV_HEADS)])
    for buf, new in ((k_buf, nk_ref), (v_buf, nv_ref)):
        buf[past:past + ts, :] = new[...]
        buf[past + ts:n_keys, :] = jnp.zeros((n_keys - past - ts, buf.shape[1]), BF16)
    ki_buf[0:past, 0:IDX_DIM] = cki_ref[...].astype(BF16)
    ki_buf[0:past, IDX_DIM:LANES] = jnp.zeros((past, LANES - IDX_DIM), BF16)
    ki_buf[past:past + ts, :] = nki_ref[...]
    ki_buf[past + ts:n_keys, :] = jnp.zeros((n_keys - past - ts, LANES), BF16)
    _dsa_body(q_ref, qi_ref, wi_ref, k_buf, v_buf, ki_buf, o_ref, *scratch,
              q_pos_first=past, s_valid=past + ts, n_sel=n_sel, packed_bisect=True, stack=KV_GROUP)


def dsa_step(q, qi, wi, cache_k, cache_v, cache_ki, k_new, v_new, ki_new, *, n_sel):
    n_batch, ts = q.shape[:2]
    past = cache_k.shape[1]
    n_keys = -(-(past + ts) // LANES) * LANES

    def spec(rows, width):
        return pl.BlockSpec((None, rows, width), lambda b: (b, 0, 0))

    return pl.pallas_call(
        functools.partial(_dsa_step_kernel, past=past, n_sel=n_sel),
        grid=(n_batch,),
        in_specs=[spec(ts, ATTN_WIDTH), spec(ts, IDX_HEADS * LANES), spec(ts, LANES),
                  pl.BlockSpec(memory_space=pl.ANY), pl.BlockSpec(memory_space=pl.ANY), spec(past, IDX_DIM),
                  spec(ts, KV_WIDTH), spec(ts, N_KV_HEADS * V_AUG), spec(ts, LANES)],
        out_specs=spec(ts, ATTN_WIDTH),
        out_shape=jax.ShapeDtypeStruct((n_batch, ts, ATTN_WIDTH), BF16),
        scratch_shapes=[pltpu.VMEM((n_keys, KV_WIDTH), BF16), pltpu.VMEM((n_keys, N_KV_HEADS * V_AUG), BF16),
                        pltpu.VMEM((n_keys, LANES), BF16),
                        pltpu.VMEM((2, 2, N_KV_HEADS, past, HEAD_DIM), F32), pltpu.SemaphoreType.DMA((2,)),
                        *_dsa_scratch(ts, n_keys)],
        compiler_params=_params(("arbitrary",)),
        name="dsa_step",
    )(q, qi, wi, cache_k, cache_v, cache_ki, k_new, v_new, ki_new)


def _merge_kernel(g_ref, a_ref, ga_ref, gb_ref, wv_ref, wg_ref, wb_ref, o_ref):
    g = g_ref[...]
    branch_a = _dot(g, wv_ref[...]) * jax.nn.sigmoid(_dot(g, wg_ref[...]))
    branch_b = _dot(a_ref[...], wb_ref[...])
    merged = jax.nn.sigmoid(ga_ref[...]) * branch_a + jax.nn.sigmoid(gb_ref[...]) * branch_b
    o_ref[...] = merged.astype(BF16)


def merge(g, attn, proj, w_val, w_gate, w_branch, *, tm=1024, tn=512):
    n_tok = g.shape[0]
    nj = D_MODEL // tn

    def wspec():
        return pl.BlockSpec((SSM_WIDTH, tn), lambda i, j: (0, j))

    return pl.pallas_call(
        _merge_kernel,
        grid=(n_tok // tm, nj),
        in_specs=[pl.BlockSpec((tm, SSM_WIDTH), lambda i, j: (i, 0)),
                  pl.BlockSpec((tm, ATTN_WIDTH), lambda i, j: (i, 0)),
                  pl.BlockSpec((tm, tn), lambda i, j: (i, COL_GA // tn + j)),
                  pl.BlockSpec((tm, tn), lambda i, j: (i, COL_GB // tn + j)),
                  wspec(), wspec(), wspec()],
        out_specs=pl.BlockSpec((tm, tn), lambda i, j: (i, j)),
        out_shape=jax.ShapeDtypeStruct((n_tok, D_MODEL), BF16),
        compiler_params=_params(("arbitrary", "arbitrary")),
        name="merge",
    )(g, attn, proj, proj, w_val, w_gate, w_branch)


ROUTER_COLS = N_EXPERT_GROUPS + N_EXPERTS
MOE_TM = 256


def _first_lane_of_max(x, lane_f):
    m = jnp.max(x, axis=1, keepdims=True)
    return m, jnp.min(jnp.where(x == m, lane_f, float(LANES)), axis=1, keepdims=True)


def _out_proj_kernel(x_ref, m_ref, wo_ref, gn_ref, wrh_ref, wrl_ref, br_ref, cin_ref,
                     h_ref, hn_ref, ri_ref, rw_ref, cnt_ref, carry_ref):
    @pl.when(pl.program_id(0) == 0)
    def _():
        carry_ref[...] = cin_ref[...]

    h = x_ref[...] + _dot(m_ref[...], wo_ref[...])
    h_ref[...] = h
    ms = jnp.mean(h * h, axis=-1, keepdims=True)
    hn = h * lax.rsqrt(ms + EPS) * gn_ref[...]
    hn_ref[...] = hn
    hh, hl = _split_bf16(hn)
    wrh = wrh_ref[...]
    lg = _dot(hh, wrh) + _dot(hl, wrh) + _dot(hh, wrl_ref[...]) + br_ref[...]

    tm = lg.shape[0]
    lane = lax.broadcasted_iota(I32, lg.shape, 1)
    lane_f = lane.astype(F32)
    ninf = -jnp.inf
    gl = jnp.where(lane < N_EXPERT_GROUPS, lg, ninf)
    gmax, gsel = _first_lane_of_max(gl, lane_f)
    g_w = 1.0 / jnp.sum(jnp.exp(gl - gmax), axis=1, keepdims=True)
    lo = N_EXPERT_GROUPS + EXPERTS_PER_GROUP * gsel
    el = jnp.where(lane_f >= lo, jnp.where(lane_f < lo + EXPERTS_PER_GROUP, lg, ninf), ninf)
    v1, i1 = _first_lane_of_max(el, lane_f)
    el2 = jnp.where(lane_f == i1, ninf, el)
    v2, i2 = _first_lane_of_max(el2, lane_f)
    t = jnp.exp(v2 - v1)
    s1 = 1.0 / (1.0 + t)
    w1 = s1 * g_w
    w2 = (t * s1) * g_w

    m1 = jnp.where(lane_f == i1, 1.0, 0.0)
    m2 = jnp.where(lane_f == i2, 1.0, 0.0)
    both = m1 + m2
    tri = jnp.where(lax.broadcasted_iota(I32, (tm, tm), 0) > lax.broadcasted_iota(I32, (tm, tm), 1), 1.0, 0.0)
    before = _dot(tri.astype(BF16), both.astype(BF16)) + carry_ref[...]
    r1 = jnp.sum(before * m1, axis=1, keepdims=True)
    r2 = jnp.sum(before * m2, axis=1, keepdims=True)
    carry_ref[...] = carry_ref[...] + jnp.sum(both, axis=0, keepdims=True)
    cnt_ref[...] = carry_ref[...]
    e1 = i1 - float(N_EXPERT_GROUPS)
    e2 = i2 - float(N_EXPERT_GROUPS)
    fields = jnp.where(lane == 0, e1, jnp.where(lane == 1, e2, jnp.where(lane == 2, r1, jnp.where(lane == 3, r2, 0.0))))
    ri_ref[...] = fields.T[0:SUBLANES, :].astype(I32)
    rw_ref[...] = jnp.where(lane == 0, w1, jnp.where(lane == 1, w2, 0.0))


def _router_weights(w_router_group, b_router_group, w_router_expert, b_router_expert):
    wr = jnp.concatenate([w_router_group, w_router_expert, jnp.zeros((D_MODEL, LANES - ROUTER_COLS), F32)], axis=1)
    wr_hi = wr.astype(BF16)
    wr_lo = (wr - wr_hi.astype(F32)).astype(BF16)
    br = jnp.concatenate([b_router_group, b_router_expert, jnp.zeros((LANES - ROUTER_COLS,), F32)])[None, :]
    return wr_hi, wr_lo, br


def out_proj(x, merged, w_out, ffn_gain, router_w, counts_in, *, tm=256):
    n_tok = x.shape[0]
    wr_hi, wr_lo, br = router_w

    def row(width):
        return pl.BlockSpec((tm, width), lambda i: (i, 0))

    def const(shape):
        return pl.BlockSpec(shape, lambda i: (0, 0), pipeline_mode=pl.Buffered(1))

    return pl.pallas_call(
        _out_proj_kernel,
        grid=(n_tok // tm,),
        in_specs=[row(D_MODEL), row(D_MODEL), const((D_MODEL, D_MODEL)), const((1, D_MODEL)),
                  const((D_MODEL, LANES)), const((D_MODEL, LANES)), const((1, LANES)), const((1, LANES))],
        out_specs=[row(D_MODEL), row(D_MODEL), pl.BlockSpec((SUBLANES, tm), lambda i: (0, i)), row(LANES),
                   pl.BlockSpec((1, LANES), lambda i: (0, 0))],
        out_shape=[jax.ShapeDtypeStruct((n_tok, D_MODEL), F32), jax.ShapeDtypeStruct((n_tok, D_MODEL), F32),
                   jax.ShapeDtypeStruct((SUBLANES, n_tok), I32), jax.ShapeDtypeStruct((n_tok, LANES), F32),
                   jax.ShapeDtypeStruct((1, LANES), F32)],
        scratch_shapes=[pltpu.VMEM((1, LANES), F32)],
        compiler_params=_params(("arbitrary",)),
        name="out_proj",
    )(x, merged, w_out, ffn_gain[None, :], wr_hi, wr_lo, br, counts_in)


def _block_layout(counts):
    padded = (counts + MOE_TM - 1) // MOE_TM * MOE_TM
    pad_end = jnp.cumsum(padded).astype(I32)
    pad_start = pad_end - padded
    n_used = pad_end[-1] // MOE_TM
    return pad_start, pad_end, n_used


def _moe_rows(n_tok):
    return -(-(n_tok * TOP_K + N_EXPERTS * (MOE_TM - 1)) // MOE_TM) * MOE_TM


DISPATCH_TM = 512


def _wait_rows(src_hbm, dst, sem, n_rows):
    pltpu.make_async_copy(src_hbm.at[pl.ds(0, n_rows)], dst, sem).wait()


def _dispatch_kernel(d0_ref, d1_ref, pe_ref, cnt_ref, nu_ref, hna_ref, hnb_ref, xs_hbm, zbuf, sem, semz,
                     *, n_blocks, a_tiles):
    i = pl.program_id(0)

    def zero_block(row0):
        return pltpu.make_async_copy(zbuf, xs_hbm.at[pl.ds(pl.multiple_of(row0, MOE_TM), MOE_TM)], semz)

    @pl.when(i == 0)
    def _():
        zbuf[...] = jnp.zeros_like(zbuf)
        for start in (True, False):
            for e in range(N_EXPERTS):
                @pl.when(cnt_ref[e] > 0)
                def _():
                    cp = zero_block(pe_ref[e] - MOE_TM)
                    cp.start() if start else cp.wait()

            def tail(b, c):
                cp = zero_block(b * MOE_TM)
                cp.start() if start else cp.wait()
                return c
            lax.fori_loop(nu_ref[0], n_blocks, tail, 0)

    base = i * DISPATCH_TM

    def scatter(hn_ref):
        def body(r, c):
            src = hn_ref.at[pl.ds(r, 1)]
            pltpu.make_async_copy(src, xs_hbm.at[pl.ds(d0_ref[base + r], 1)], sem).start()
            pltpu.make_async_copy(src, xs_hbm.at[pl.ds(d1_ref[base + r], 1)], sem).start()
            return c
        lax.fori_loop(0, DISPATCH_TM, body, 0, unroll=8)
        for _ in range(TOP_K):
            pltpu.make_async_copy(hn_ref, xs_hbm.at[pl.ds(0, DISPATCH_TM)], sem).wait()

    @pl.when(i < a_tiles)
    def _():
        scatter(hna_ref)

    @pl.when(i >= a_tiles)
    def _():
        scatter(hnb_ref)


def dispatch(hn_a, hn_b, dest0, dest1, pad_end, counts, n_used):
    a_tiles, b_tiles = hn_a.shape[0] // DISPATCH_TM, hn_b.shape[0] // DISPATCH_TM
    rows = _moe_rows(hn_a.shape[0] + hn_b.shape[0])
    grid_spec = pltpu.PrefetchScalarGridSpec(
        num_scalar_prefetch=5,
        grid=(a_tiles + b_tiles,),
        in_specs=[pl.BlockSpec((DISPATCH_TM, D_MODEL), lambda i, *_: (jnp.minimum(i, a_tiles - 1), 0)),
                  pl.BlockSpec((DISPATCH_TM, D_MODEL), lambda i, *_: (jnp.maximum(i - a_tiles, 0), 0))],
        out_specs=pl.BlockSpec(memory_space=pl.ANY),
        scratch_shapes=[pltpu.VMEM((MOE_TM, D_MODEL), F32), pltpu.SemaphoreType.DMA(()), pltpu.SemaphoreType.DMA(())],
    )
    return pl.pallas_call(
        functools.partial(_dispatch_kernel, n_blocks=rows // MOE_TM, a_tiles=a_tiles),
        grid_spec=grid_spec,
        out_shape=jax.ShapeDtypeStruct((rows, D_MODEL), F32),
        compiler_params=_params(("arbitrary",)),
        name="dispatch",
    )(dest0, dest1, pad_end, counts, n_used, hn_a, hn_b)


MOE_UNITS = 8
MOE_UG = D_MODEL // MOE_UNITS
MOE_UD = EXPERT_FF // MOE_UNITS


def _moe_kernel(blk_e_ref, nu_ref, nxt_ref, upb_ref, xs_ref, wg_hbm, wu_hbm, wd_hbm, ys_ref,
                wg_bf, wu_bf, wd_bf, stg_g, stg_u, stg_d, sem, st_ref):
    i = pl.program_id(0)
    cur_slot, pos, cur_e = 0, 1, 2

    def unit_copies(e, unit, s):
        g_rows = pl.ds(pl.multiple_of(unit * MOE_UG, MOE_UG), MOE_UG)
        d_rows = pl.ds(pl.multiple_of(unit * MOE_UD, MOE_UD), MOE_UD)
        return (pltpu.make_async_copy(wg_hbm.at[e, g_rows, :], stg_g.at[s], sem.at[s]),
                pltpu.make_async_copy(wu_hbm.at[e, g_rows, :], stg_u.at[s], sem.at[s]),
                pltpu.make_async_copy(wd_hbm.at[e, d_rows, :], stg_d.at[s], sem.at[s]))

    def start_unit(e, unit):
        for cp in unit_copies(e, unit, unit % 2):
            cp.start()

    def begin_load(e):
        st_ref[pos] = 0
        start_unit(e, 0)
        start_unit(e, 1)

    def advance(e, slot, n):
        def body(_, c):
            unit = st_ref[pos]

            @pl.when(unit < MOE_UNITS)
            def _():
                s = unit % 2
                for cp in unit_copies(e, unit, s):
                    cp.wait()
                g_rows = pl.ds(pl.multiple_of(unit * MOE_UG, MOE_UG), MOE_UG)
                d_rows = pl.ds(pl.multiple_of(unit * MOE_UD, MOE_UD), MOE_UD)
                wg_bf[slot, g_rows, :] = stg_g[s].astype(BF16)
                wu_bf[slot, g_rows, :] = stg_u[s].astype(BF16)
                wd_bf[slot, d_rows, :] = stg_d[s].astype(BF16)

                @pl.when(unit + 2 < MOE_UNITS)
                def _():
                    start_unit(e, unit + 2)
                st_ref[pos] = unit + 1
            return c
        lax.fori_loop(0, n, body, 0)

    def load_next(nxt):
        @pl.when(nxt >= 0)
        def _():
            begin_load(nxt)

        @pl.when(nxt < 0)
        def _():
            st_ref[pos] = MOE_UNITS

    @pl.when(i < nu_ref[0])
    def _():
        e = blk_e_ref[i]
        nxt = nxt_ref[i]

        @pl.when(i == 0)
        def _():
            st_ref[cur_slot] = 0
            st_ref[cur_e] = e
            begin_load(e)
            advance(e, 0, MOE_UNITS)
            load_next(nxt)

        @pl.when(jnp.logical_and(i > 0, e != st_ref[cur_e]))
        def _():
            slot = 1 - st_ref[cur_slot]
            advance(e, slot, MOE_UNITS)
            st_ref[cur_slot] = slot
            st_ref[cur_e] = e
            load_next(nxt)

        slot = st_ref[cur_slot]
        x = xs_ref[...].astype(BF16)
        hg = _dot(x, wg_bf[slot])
        hu = _dot(x, wu_bf[slot])
        hmid = (jax.nn.silu(hg) * hu).astype(BF16)
        ys_ref[...] = _dot(hmid, wd_bf[slot])

        @pl.when(nxt >= 0)
        def _():
            advance(nxt, 1 - slot, upb_ref[i])

    @pl.when(i >= nu_ref[0])
    def _():
        ys_ref[...] = jnp.zeros_like(ys_ref)


def moe(xs, blk_e, n_used, nxt_e, units_per_block, w_gate, w_up, w_down):
    rows = xs.shape[0]
    grid_spec = pltpu.PrefetchScalarGridSpec(
        num_scalar_prefetch=4,
        grid=(rows // MOE_TM,),
        in_specs=[pl.BlockSpec((MOE_TM, D_MODEL), lambda i, be, nu, nx, ub: (jnp.minimum(i, nu[0] - 1), 0)),
                  pl.BlockSpec(memory_space=pl.ANY), pl.BlockSpec(memory_space=pl.ANY), pl.BlockSpec(memory_space=pl.ANY)],
        out_specs=pl.BlockSpec((MOE_TM, D_MODEL), lambda i, be, nu, nx, ub: (i, 0)),
        scratch_shapes=[pltpu.VMEM((2, D_MODEL, EXPERT_FF), BF16), pltpu.VMEM((2, D_MODEL, EXPERT_FF), BF16),
                        pltpu.VMEM((2, EXPERT_FF, D_MODEL), BF16),
                        pltpu.VMEM((2, MOE_UG, EXPERT_FF), F32), pltpu.VMEM((2, MOE_UG, EXPERT_FF), F32),
                        pltpu.VMEM((2, MOE_UD, D_MODEL), F32),
                        pltpu.SemaphoreType.DMA((2,)), pltpu.SMEM((3,), I32)],
    )
    return pl.pallas_call(
        _moe_kernel,
        grid_spec=grid_spec,
        out_shape=jax.ShapeDtypeStruct((rows, D_MODEL), F32),
        compiler_params=_params(("arbitrary",)),
        name="moe",
    )(blk_e, n_used, nxt_e, units_per_block, xs, w_gate, w_up, w_down)


def _combine_kernel(r0_ref, r1_ref, ys_hbm, h_ref, w_ref, o_ref, buf, sem, *, tm, tok0):
    i = pl.program_id(0)

    def issue(block, slot):
        base = tok0 + block * tm

        def body(r, carry):
            for k, idx_ref in enumerate((r0_ref, r1_ref)):
                pltpu.make_async_copy(ys_hbm.at[pl.ds(idx_ref[base + r], 1)], buf.at[slot, k, pl.ds(r, 1)],
                                      sem.at[slot]).start()
            return carry
        lax.fori_loop(0, tm, body, 0, unroll=8)

    @pl.when(i == 0)
    def _():
        issue(0, 0)

    @pl.when(i + 1 < pl.num_programs(0))
    def _():
        issue(i + 1, (i + 1) % 2)

    slot = i % 2
    _wait_rows(ys_hbm, buf.at[slot, 0], sem.at[slot], tm)
    _wait_rows(ys_hbm, buf.at[slot, 1], sem.at[slot], tm)
    w = w_ref[...]
    o_ref[...] = h_ref[...] + (buf[slot, 0] * w[:, 0:1] + buf[slot, 1] * w[:, 1:2])


def combine(ys, h, route_w, rows0, rows1, *, tok0, tm=256):
    n_tok = h.shape[0]
    grid_spec = pltpu.PrefetchScalarGridSpec(
        num_scalar_prefetch=2,
        grid=(n_tok // tm,),
        in_specs=[pl.BlockSpec(memory_space=pl.ANY),
                  pl.BlockSpec((tm, D_MODEL), lambda i, a, b: (i, 0)),
                  pl.BlockSpec((tm, LANES), lambda i, a, b: (i, 0))],
        out_specs=pl.BlockSpec((tm, D_MODEL), lambda i, a, b: (i, 0)),
        scratch_shapes=[pltpu.VMEM((2, 2, tm, D_MODEL), F32), pltpu.SemaphoreType.DMA((2,))],
    )
    return pl.pallas_call(
        functools.partial(_combine_kernel, tm=tm, tok0=tok0),
        grid_spec=grid_spec,
        out_shape=jax.ShapeDtypeStruct((n_tok, D_MODEL), F32),
        compiler_params=_params(("arbitrary",)),
        name="combine",
    )(rows0, rows1, ys, h, route_w)


def _regroup_w_in(w_in):
    sizes = (SSM_WIDTH, ATTN_WIDTH, KV_WIDTH, KV_WIDTH, IDX_HEADS * IDX_DIM, IDX_DIM, IDX_HEADS, D_MODEL, D_MODEL)
    u, q, k, v, qi, ki, wi, ga, gb = jnp.split(w_in, np.cumsum(sizes)[:-1].tolist(), axis=1)
    pad = jnp.zeros((D_MODEL, PROJ_COLS - COL_KIWI - IDX_DIM - IDX_HEADS), F32)
    return jnp.concatenate([u, q, ga, gb, k, v, qi, ki, wi, pad], axis=1).astype(BF16)


def _layer(x_p, x_s, cache_k, cache_v, cache_ki, h0_re, h0_im, p):
    bp, tp, _ = x_p.shape
    bs, ts, _ = x_s.shape
    past = cache_k.shape[1]
    n_p, n_s = bp * tp, bs * ts
    n_tok = n_p + n_s

    w_in = _regroup_w_in(p['w_in'])
    ssm_w = _ssm_weights(p['ssm_A_re'], p['ssm_A_im'], p['ssm_log_dt'], p['ssm_B_re'], p['ssm_B_im'],
                         p['ssm_C_re'], p['ssm_C_im'])
    glu_w = (p['w_glu_val'].astype(BF16), p['w_glu_gate'].astype(BF16), p['w_attn_branch'].astype(BF16))
    w_out = p['w_out'].astype(BF16)
    router_w = _router_weights(p['w_router_group'], p['b_router_group'], p['w_router_expert'], p['b_router_expert'])
    seq_tiles = tp // QK_TM

    def front(x, table_pos, table_block):
        proj = in_proj(x, p['norm_mix_g'][None, :], w_in)
        return proj, qk_post(proj, table_pos, table_block, p['q_norm_g'], p['k_norm_g'], p['idx_k_norm_g'])

    def seqs(a, b, t):
        return a.reshape(b, t, a.shape[-1])

    xp = x_p.reshape(n_p, D_MODEL)
    proj_p, (q_b, kf_p, k_b, vf_p, v_b, qi_b, kif_p, ki_b, wi) = front(
        xp, jnp.arange(tp, dtype=I32), lambda i: i % seq_tiles)
    g_p, sre_p, sim_p = ssm(proj_p, ssm_w, p['ssm_D'], jnp.zeros((bp, SSM_LB, 2, SSM_SB), F32),
                            n_batch=bp, seq=tp, row0=0)
    bq = 128
    n_buckets = min(16, tp // bq)
    per = tp // bq // n_buckets
    qp, qip, wip = seqs(q_b, bp, tp), seqs(qi_b, bp, tp), seqs(wi, bp, tp)
    kp, vp, kip = seqs(k_b, bp, tp), seqs(v_b, bp, tp), seqs(ki_b, bp, tp)
    attn_p = jnp.concatenate(
        [dsa(qp, qip, wip, kp, vp, kip, bq=bq, q_blk0=n * per, n_qblk=per, n_keys=(n + 1) * per * bq,
             n_sel=min(IDX_TOPK, tp // 4), packed_bisect=False, stack=(1 if n % 2 == 0 else KV_GROUP))
         for n in range(n_buckets)], axis=1).reshape(n_p, ATTN_WIDTH)
    merged_p = merge(g_p, attn_p, proj_p, *glu_w)
    h_p, hn_p, ri_p, rw_p, cnt_p = out_proj(xp, merged_p, w_out, p['norm_ffn_g'], router_w, jnp.zeros((1, LANES), F32))

    xs_ = x_s.reshape(n_s, D_MODEL)
    proj_s, (q_b, kf_s, k_b, vf_s, v_b, qi_b, kif_s, ki_b, wi) = front(
        xs_, jnp.tile(past + jnp.arange(ts, dtype=I32), QK_TM // ts), lambda i: 0)
    h0 = jnp.stack([h0_re.reshape(bs, SSM_LB, SSM_SB), h0_im.reshape(bs, SSM_LB, SSM_SB)]).transpose(2, 0, 1, 3)
    g_s, sre_s, sim_s = ssm_step(proj_s, ssm_w, p['ssm_D'], h0, n_batch=bs, seq=ts, row0=0)
    attn_s = dsa_step(seqs(q_b, bs, ts), seqs(qi_b, bs, ts), seqs(wi, bs, ts),
                      cache_k, cache_v, cache_ki,
                      seqs(k_b, bs, ts), seqs(v_b, bs, ts), seqs(ki_b, bs, ts),
                      n_sel=min(IDX_TOPK, (past + ts) // 4)).reshape(n_s, ATTN_WIDTH)
    merged_s = merge(g_s, attn_s, proj_s, *glu_w)
    h_s, hn_s, ri_s, rw_s, cnt = out_proj(xs_, merged_s, w_out, p['norm_ffn_g'], router_w, cnt_p)

    counts = cnt[0, N_EXPERT_GROUPS:ROUTER_COLS].astype(I32)
    pad_start, pad_end, n_used = _block_layout(counts)
    route_i = jnp.concatenate([ri_p, ri_s], axis=1)
    dest0 = pad_start[route_i[0]] + route_i[2]
    dest1 = pad_start[route_i[1]] + route_i[3]
    n_blocks = _moe_rows(n_tok) // MOE_TM
    blk = jnp.minimum(jnp.arange(n_blocks, dtype=I32), n_used - 1)
    blk_e = jnp.minimum(jnp.sum((pad_end[None, :] <= (blk * MOE_TM)[:, None]).astype(I32), axis=1), N_EXPERTS - 1)
    after = pad_end[blk_e] // MOE_TM
    nxt_e = jnp.where(after < n_used, blk_e[jnp.minimum(after, n_blocks - 1)], -1).astype(I32)
    blocks_of_e = jnp.maximum((pad_end - pad_start)[blk_e] // MOE_TM, 1)
    units_per_block = ((MOE_UNITS + blocks_of_e - 1) // blocks_of_e).astype(I32)
    n_used = n_used.reshape(1)

    xs = dispatch(hn_p, hn_s, dest0, dest1, pad_end, counts, n_used)
    ys = moe(xs, blk_e, n_used, nxt_e, units_per_block, p['w_exp_gate'], p['w_exp_up'], p['w_exp_down'])
    y_p = combine(ys, h_p, rw_p, dest0, dest1, tok0=0).reshape(bp, tp, D_MODEL)
    y_s = combine(ys, h_s, rw_s, dest0, dest1, tok0=n_p).reshape(bs, ts, D_MODEL)

    def heads(a, b, t):
        return a.reshape(b, t, N_KV_HEADS, HEAD_DIM)

    new_p = (heads(kf_p, bp, tp), heads(vf_p, bp, tp), kif_p.reshape(bp, tp, IDX_DIM), sre_p, sim_p)
    new_s = (heads(kf_s, bs, ts), heads(vf_s, bs, ts), kif_s.reshape(bs, ts, IDX_DIM), sre_s, sim_s)
    return y_p, y_s, new_p, new_s


def kernel(x_prompt, x_sample, cache_k, cache_v, cache_idx_k, state_ssm_re, state_ssm_im, norm_mix_g, w_in, q_norm_g, k_norm_g, idx_k_norm_g, ssm_A_re, ssm_A_im, ssm_log_dt, ssm_B_re, ssm_B_im, ssm_C_re, ssm_C_im, ssm_D, w_glu_val, w_glu_gate, w_attn_branch, w_out, norm_ffn_g, w_router_group, b_router_group, w_router_expert, b_router_expert, w_exp_gate, w_exp_up, w_exp_down):
    depth = w_in.shape[0]
    assert depth == 1, "prompt and sample tokens are batched through one layer"
    names = ('norm_mix_g', 'w_in', 'q_norm_g', 'k_norm_g', 'idx_k_norm_g', 'ssm_A_re', 'ssm_A_im', 'ssm_log_dt',
             'ssm_B_re', 'ssm_B_im', 'ssm_C_re', 'ssm_C_im', 'ssm_D', 'w_glu_val', 'w_glu_gate', 'w_attn_branch',
             'w_out', 'norm_ffn_g', 'w_router_group', 'b_router_group', 'w_router_expert', 'b_router_expert',
             'w_exp_gate', 'w_exp_up', 'w_exp_down')
    vals = (norm_mix_g, w_in, q_norm_g, k_norm_g, idx_k_norm_g, ssm_A_re, ssm_A_im, ssm_log_dt, ssm_B_re, ssm_B_im,
            ssm_C_re, ssm_C_im, ssm_D, w_glu_val, w_glu_gate, w_attn_branch, w_out, norm_ffn_g, w_router_group,
            b_router_group, w_router_expert, b_router_expert, w_exp_gate, w_exp_up, w_exp_down)
    p = {n: v[0] for n, v in zip(names, vals)}
    y_p, y_s, new_p, new_s = _layer(x_prompt, x_sample, cache_k[0], cache_v[0], cache_idx_k[0],
                                    state_ssm_re[0], state_ssm_im[0], p)
    st_p = tuple(a[None] for a in new_p)
    st_s = tuple(a[None] for a in new_s)
    return (y_p, y_s) + st_p + st_s
```

```python
import functools

import numpy as np
import jax
import jax.numpy as jnp
from jax import lax
from jax.experimental import pallas as pl
from jax.experimental.pallas import tpu as pltpu

F32 = jnp.float32
BF16 = jnp.bfloat16
I32 = jnp.int32

D_MODEL = 2048
CHUNK = 64
SSM_WIDTH = 1024
SSM_GROUP = 16
SSM_GROUPS = 64
SSM_STATE = 64
ATTN_WIDTH = 1024
HEAD_DIM = 128
N_HEADS = 8
N_KV_HEADS = 2
KV_GROUP = 4
IDX_HEADS = 8
IDX_DIM = 64
IDX_TOPK = 256
ROPE_THETA = 500000.0
N_EXPERT_GROUPS = 4
EXPERTS_PER_GROUP = 8
N_EXPERTS = 32
TOP_K = 2
EXPERT_FF = 1024
EPS = 1e-6

LANES = 128
SUBLANES = 8
VMEM_LIMIT = 56 * 1024 * 1024

COL_U, COL_Q, COL_GA, COL_GB, COL_K, COL_V, COL_QI, COL_KIWI = 0, 1024, 2048, 4096, 6144, 6400, 6656, 7168
PROJ_COLS = 7296
PROJ_TN = 2432
KV_WIDTH = N_KV_HEADS * HEAD_DIM

SSM_LB = SSM_WIDTH // LANES
SSM_SB = 8 * SSM_STATE

INT_MIN = np.int32(-2 ** 31)
KEY_NEG_INF = np.int32(np.array([0xFF800000], np.uint32).view(np.int32)[0] ^ 0x7FFFFFFF)


def _params(sem, vmem=VMEM_LIMIT):
    return pltpu.CompilerParams(dimension_semantics=sem, vmem_limit_bytes=vmem)


def _dot(a, b):
    return jnp.dot(a, b, preferred_element_type=F32)


def _dot_nt(a, b):
    return lax.dot_general(a, b, (((1,), (1,)), ((), ())), preferred_element_type=F32)


def _split_bf16(x):
    hi = x.astype(BF16)
    lo = (x - hi.astype(F32)).astype(BF16)
    return hi, lo


def _in_proj_kernel(x_ref, g_ref, w_ref, o_ref, xn_ref):
    @pl.when(pl.program_id(1) == 0)
    def _():
        x = x_ref[...]
        ms = jnp.mean(x * x, axis=-1, keepdims=True)
        xn_ref[...] = (x * lax.rsqrt(ms + EPS) * g_ref[...]).astype(BF16)

    o_ref[...] = _dot(xn_ref[...], w_ref[...])


def in_proj(x, gain, w_bf16, *, tm=512):
    n_tok = x.shape[0]
    return pl.pallas_call(
        _in_proj_kernel,
        grid=(n_tok // tm, PROJ_COLS // PROJ_TN),
        in_specs=[pl.BlockSpec((tm, D_MODEL), lambda i, j: (i, 0)),
                  pl.BlockSpec((1, D_MODEL), lambda i, j: (0, 0)),
                  pl.BlockSpec((D_MODEL, PROJ_TN), lambda i, j: (0, j))],
        out_specs=pl.BlockSpec((tm, PROJ_TN), lambda i, j: (i, j)),
        out_shape=jax.ShapeDtypeStruct((n_tok, PROJ_COLS), F32),
        scratch_shapes=[pltpu.VMEM((tm, D_MODEL), BF16)],
        compiler_params=_params(("arbitrary", "arbitrary")),
        name="in_proj",
    )(x, gain, w_bf16)


def _rope(x, c, s_lo, s_hi, half):
    n = x.shape[-1]
    return x * c + pltpu.roll(x, n - half, 1) * s_lo + pltpu.roll(x, half, 1) * s_hi


def _head_norm(x, g):
    ms = jnp.mean(x * x, axis=-1, keepdims=True)
    return x * lax.rsqrt(ms + EPS) * g


V_AUG = 2 * HEAD_DIM


def _store_v_aug(dst_ref, row0, v_heads):
    n = v_heads[0].shape[0]
    one_col = jnp.where(lax.broadcasted_iota(I32, (n, HEAD_DIM), 1) == 0, 1.0, 0.0).astype(BF16)
    for h, v in enumerate(v_heads):
        dst_ref[row0:row0 + n, h * V_AUG:h * V_AUG + HEAD_DIM] = v.astype(BF16)
        dst_ref[row0:row0 + n, h * V_AUG + HEAD_DIM:(h + 1) * V_AUG] = one_col


def _qk_post_kernel(q_ref, k_ref, v_ref, qi_ref, kw_ref, c128_ref, sl128_ref, sh128_ref,
                    c64_ref, sl64_ref, sh64_ref, qg_ref, kg_ref, ig_ref,
                    qo_ref, kf_ref, kb_ref, vf_ref, vb_ref, qio_ref, kif_ref, kib_ref, wo_ref):
    c128, sl128, sh128 = c128_ref[...], sl128_ref[...], sh128_ref[...]
    c64, sl64, sh64 = c64_ref[...], sl64_ref[...], sh64_ref[...]
    half128 = HEAD_DIM // 8
    half64 = IDX_DIM // 8
    for h in range(N_HEADS):
        sl = slice(h * LANES, (h + 1) * LANES)
        qo_ref[:, sl] = _rope(_head_norm(q_ref[:, sl], qg_ref[...]), c128, sl128, sh128, half128).astype(BF16)
    for h in range(N_KV_HEADS):
        sl = slice(h * LANES, (h + 1) * LANES)
        kk = _rope(_head_norm(k_ref[:, sl], kg_ref[...]), c128, sl128, sh128, half128)
        kf_ref[:, sl] = kk
        kb_ref[:, sl] = kk.astype(BF16)
    v = v_ref[...]
    vf_ref[...] = v
    _store_v_aug(vb_ref, 0, [v[:, h * HEAD_DIM:(h + 1) * HEAD_DIM] for h in range(N_KV_HEADS)])
    lane = lax.broadcasted_iota(I32, c64.shape, 1)
    low = lane < IDX_DIM
    for p in range(IDX_HEADS // 2):
        x = _rope(qi_ref[:, p * LANES:(p + 1) * LANES], c64, sl64, sh64, half64)
        qio_ref[:, (2 * p) * LANES:(2 * p + 1) * LANES] = jnp.where(low, x, 0.0).astype(BF16)
        qio_ref[:, (2 * p + 1) * LANES:(2 * p + 2) * LANES] = jnp.where(low, pltpu.roll(x, IDX_DIM, 1), 0.0).astype(BF16)
    kw = kw_ref[...]
    ms = jnp.sum(jnp.where(low, kw * kw, 0.0), axis=-1, keepdims=True) * (1.0 / IDX_DIM)
    ki = _rope(kw * lax.rsqrt(ms + EPS) * ig_ref[...], c64, sl64, sh64, half64)
    kif_ref[...] = ki[:, :IDX_DIM]
    kib_ref[...] = jnp.where(low, ki, 0.0).astype(BF16)
    wo_ref[...] = (pltpu.roll(kw, IDX_DIM, 1) * IDX_HEADS ** -0.5) * IDX_DIM ** -0.5


def _rope_tables(pos, head_dim):
    r = head_dim // 4
    half = r // 2
    inv = ROPE_THETA ** (-jnp.arange(half, dtype=F32) * 2.0 / r)
    ang = pos.astype(F32)[:, None] * inv[None, :]
    cos, sin = jnp.cos(ang), jnp.sin(ang)
    n = pos.shape[0]
    zh = jnp.zeros((n, half), F32)
    rest = head_dim - r
    c = jnp.concatenate([cos, cos, jnp.ones((n, rest), F32)], axis=-1)
    s_lo = jnp.concatenate([-sin, zh, jnp.zeros((n, rest), F32)], axis=-1)
    s_hi = jnp.concatenate([zh, sin, jnp.zeros((n, rest), F32)], axis=-1)
    rep = LANES // head_dim
    return tuple(jnp.tile(t, (1, rep)) for t in (c, s_lo, s_hi))


QK_TM = 512


def qk_post(proj, table_pos, table_block, q_gain, k_gain, ik_gain):
    tm = QK_TM
    n_tok = proj.shape[0]
    t128 = _rope_tables(table_pos, HEAD_DIM)
    t64 = _rope_tables(table_pos, IDX_DIM)
    ik_gain128 = jnp.concatenate([ik_gain, jnp.zeros((LANES - IDX_DIM,), F32)])[None, :]

    def col(width, start):
        return pl.BlockSpec((tm, width), lambda i: (i, start // width))

    def row(width):
        return pl.BlockSpec((tm, width), lambda i: (i, 0))

    table = pl.BlockSpec((tm, LANES), lambda i: (table_block(i), 0))
    gain = pl.BlockSpec((1, LANES), lambda i: (0, 0))
    return pl.pallas_call(
        _qk_post_kernel,
        grid=(n_tok // tm,),
        in_specs=[col(ATTN_WIDTH, COL_Q), col(KV_WIDTH, COL_K), col(KV_WIDTH, COL_V), col(IDX_HEADS * IDX_DIM, COL_QI),
                  col(LANES, COL_KIWI)] + [table] * 6 + [gain] * 3,
        out_specs=[row(ATTN_WIDTH), row(KV_WIDTH), row(KV_WIDTH), row(KV_WIDTH), row(N_KV_HEADS * V_AUG), row(IDX_HEADS * LANES),
                   row(IDX_DIM), row(LANES), row(LANES)],
        out_shape=[jax.ShapeDtypeStruct((n_tok, ATTN_WIDTH), BF16),
                   jax.ShapeDtypeStruct((n_tok, KV_WIDTH), F32), jax.ShapeDtypeStruct((n_tok, KV_WIDTH), BF16),
                   jax.ShapeDtypeStruct((n_tok, KV_WIDTH), F32), jax.ShapeDtypeStruct((n_tok, N_KV_HEADS * V_AUG), BF16),
                   jax.ShapeDtypeStruct((n_tok, IDX_HEADS * LANES), BF16),
                   jax.ShapeDtypeStruct((n_tok, IDX_DIM), F32), jax.ShapeDtypeStruct((n_tok, LANES), BF16),
                   jax.ShapeDtypeStruct((n_tok, LANES), F32)],
        compiler_params=_params(("arbitrary",)),
        name="qk_post",
    )(proj, proj, proj, proj, proj, *t128, *t64, q_gain[None, :], k_gain[None, :], ik_gain128)


def _gelu_tanh(x):
    return 0.5 * x * (1.0 + jnp.tanh(np.float32(np.sqrt(2.0 / np.pi)) * (x + 0.044715 * (x * x * x))))


SSM_LT = SSM_SB // LANES
SSM_SEG = 64


def _ssm_kernel(u_ref, wb_ref, wc_ref, pw_ref, d_ref, h0_ref, g_ref, sre_ref, sim_ref,
                er_ref, ei_ref, car_ref, up_ref, yp_ref):
    c = pl.program_id(2)

    @pl.when(c == 0)
    def _():
        car_ref[...] = h0_ref[...]

    for j in range(SSM_SEG):
        up_ref[j * SUBLANES:(j + 1) * SUBLANES, :] = u_ref[pl.ds(j, SUBLANES, stride=SSM_SEG), :]
    e = _dot(up_ref[...].astype(BF16), wb_ref[...])
    tiles = [slice(lt * LANES, (lt + 1) * LANES) for lt in range(SSM_LT)]
    for lt, sl in enumerate(tiles):
        er_ref[lt] = e[:, sl]
        ei_ref[lt] = e[:, SSM_SB + lt * LANES:SSM_SB + (lt + 1) * LANES]

    def cmul_add(ar, ai, br, bi, cr, ci):
        return ar * br - ai * bi + cr, ar * bi + ai * br + ci

    lb = [(pw_ref[0, 0:1, sl], pw_ref[1, 0:1, sl]) for sl in tiles]
    zero = jnp.zeros((SUBLANES, LANES), F32)
    st = [(zero, zero)] * SSM_LT
    for j in range(SSM_SEG):
        rows = slice(j * SUBLANES, (j + 1) * SUBLANES)
        for lt in range(SSM_LT):
            st[lt] = cmul_add(*lb[lt], *st[lt], er_ref[lt, rows, :], ei_ref[lt, rows, :])
            er_ref[lt, rows, :] = st[lt][0]
            ei_ref[lt, rows, :] = st[lt][1]

    enter = []
    for lt, sl in enumerate(tiles):
        seg_r, seg_i = pw_ref[0, SSM_SEG - 1:SSM_SEG, sl], pw_ref[1, SSM_SEG - 1:SSM_SEG, sl]
        cr, ci = car_ref[0:1, sl], car_ref[1:2, sl]
        rows_r, rows_i = [], []
        for r in range(SUBLANES):
            rows_r.append(cr)
            rows_i.append(ci)
            cr, ci = cmul_add(seg_r, seg_i, cr, ci, st[lt][0][r:r + 1], st[lt][1][r:r + 1])
        car_ref[0:1, sl] = cr
        car_ref[1:2, sl] = ci
        enter.append((jnp.concatenate(rows_r, axis=0), jnp.concatenate(rows_i, axis=0)))

    for j in range(SSM_SEG):
        rows = slice(j * SUBLANES, (j + 1) * SUBLANES)
        for lt, sl in enumerate(tiles):
            xr, xi = cmul_add(pw_ref[0, j:j + 1, sl], pw_ref[1, j:j + 1, sl], *enter[lt],
                              er_ref[lt, rows, :], ei_ref[lt, rows, :])
            er_ref[lt, rows, :] = xr
            ei_ref[lt, rows, :] = xi

    y = None
    for lt, sl in enumerate(tiles):
        t = _dot(er_ref[lt].astype(BF16), wc_ref[0, sl, :]) - _dot(ei_ref[lt].astype(BF16), wc_ref[1, sl, :])
        y = t if y is None else y + t
    yp_ref[...] = y
    out_rows = 2 * SUBLANES
    for t0 in range(0, SUBLANES * SSM_SEG, out_rows):
        r, j0 = divmod(t0, SSM_SEG)
        rows = slice(t0, t0 + out_rows)
        yt = yp_ref[pl.ds(j0 * SUBLANES + r, out_rows, stride=SUBLANES), :] + d_ref[...] * u_ref[rows, :]
        g_ref[rows, :] = _gelu_tanh(yt).astype(BF16)

    @pl.when(c == pl.num_programs(2) - 1)
    def _():
        sre_ref[...] = car_ref[0:1, :]
        sim_ref[...] = car_ref[1:2, :]


def _ssm_weights(a_re, a_im, log_dt, b_re, b_im, c_re, c_im):
    lam_re, lam_im = a_re, a_im
    dt = jnp.exp(log_dt)[:, None]
    mag = jnp.exp(lam_re * dt)
    lb_re, lb_im = mag * jnp.cos(lam_im * dt), mag * jnp.sin(lam_im * dt)
    den = lam_re * lam_re + lam_im * lam_im
    num_re = lb_re - 1.0
    z_re = (num_re * lam_re + lb_im * lam_im) / den
    z_im = (lb_im * lam_re - num_re * lam_im) / den
    zb_re = z_re[:, :, None] * b_re - z_im[:, :, None] * b_im
    zb_im = z_re[:, :, None] * b_im + z_im[:, :, None] * b_re
    eye = jnp.eye(8, dtype=F32)

    def blockdiag_in(w):
        return jnp.einsum('jgph,gk->jghkp', w.reshape(SSM_LB, 8, SSM_STATE, SSM_GROUP), eye).reshape(SSM_LB, LANES, SSM_SB)

    def blockdiag_out(w):
        return jnp.einsum('jghp,gk->jkpgh', w.reshape(SSM_LB, 8, SSM_GROUP, SSM_STATE), eye).reshape(SSM_LB, SSM_SB, LANES)

    wb = jnp.concatenate([blockdiag_in(zb_re), blockdiag_in(zb_im)], axis=-1).astype(BF16)
    wc = jnp.stack([blockdiag_out(c_re), blockdiag_out(c_im)], axis=1).astype(BF16)

    pr, pi_ = lb_re.reshape(SSM_LB, 1, SSM_SB), lb_im.reshape(SSM_LB, 1, SSM_SB)
    while pr.shape[1] < SSM_SEG:
        tr, ti = pr[:, -1:], pi_[:, -1:]
        pr, pi_ = (jnp.concatenate([pr, pr * tr - pi_ * ti], axis=1), jnp.concatenate([pi_, pr * ti + pi_ * tr], axis=1))
    pw = jnp.stack([pr, pi_], axis=1)
    return wb, wc, pw


def ssm(proj, ssm_w, d_skip, h0, *, n_batch, seq, row0):
    wb, wc, pw = ssm_w
    tc = SUBLANES * SSM_SEG
    n_chunks = seq // tc
    blk0 = row0 // tc
    n_tok = n_batch * seq
    state_shape = jax.ShapeDtypeStruct((n_batch, SSM_LB, 1, SSM_SB), F32)
    state_spec = pl.BlockSpec((None, None, 1, SSM_SB), lambda b, j, c: (b, j, 0, 0))
    g, s_re, s_im = pl.pallas_call(
        _ssm_kernel,
        grid=(n_batch, SSM_LB, n_chunks),
        in_specs=[pl.BlockSpec((tc, LANES), lambda b, j, c: (blk0 + b * n_chunks + c, j)),
                  pl.BlockSpec((None, LANES, 2 * SSM_SB), lambda b, j, c: (j, 0, 0)),
                  pl.BlockSpec((None, 2, SSM_SB, LANES), lambda b, j, c: (j, 0, 0, 0)),
                  pl.BlockSpec((None, 2, SSM_SEG, SSM_SB), lambda b, j, c: (j, 0, 0, 0)),
                  pl.BlockSpec((1, LANES), lambda b, j, c: (0, j)),
                  pl.BlockSpec((None, None, 2, SSM_SB), lambda b, j, c: (b, j, 0, 0))],
        out_specs=[pl.BlockSpec((tc, LANES), lambda b, j, c: (b * n_chunks + c, j)), state_spec, state_spec],
        out_shape=[jax.ShapeDtypeStruct((n_tok, SSM_WIDTH), BF16), state_shape, state_shape],
        scratch_shapes=[pltpu.VMEM((SSM_LT, tc, LANES), F32), pltpu.VMEM((SSM_LT, tc, LANES), F32),
                        pltpu.VMEM((2, SSM_SB), F32), pltpu.VMEM((tc, LANES), F32), pltpu.VMEM((tc, LANES), F32)],
        compiler_params=_params(("arbitrary", "arbitrary", "arbitrary")),
        name="ssm",
    )(proj, wb, wc, pw, d_skip[None, :], h0)
    return g, s_re.reshape(n_batch, SSM_GROUPS, SSM_STATE), s_im.reshape(n_batch, SSM_GROUPS, SSM_STATE)


def _ssm_step_kernel(u_ref, wb_ref, wc_ref, pw_ref, d_ref, h0_ref, g_ref, sre_ref, sim_ref, er_ref, ei_ref, *, seq):
    n_seq = h0_ref.shape[1]
    u = u_ref[...]
    e = _dot(u.astype(BF16), wb_ref[...])
    n_lt = SSM_SB // LANES
    y = d_ref[...] * u
    for lt in range(n_lt):
        sl = slice(lt * LANES, (lt + 1) * LANES)
        er_ref[...] = e[:, lt * LANES:(lt + 1) * LANES]
        ei_ref[...] = e[:, SSM_SB + lt * LANES:SSM_SB + (lt + 1) * LANES]
        lr, li = pw_ref[0, 0:1, sl], pw_ref[1, 0:1, sl]
        sr, si = h0_ref[0, :, sl], h0_ref[1, :, sl]
        for t in range(seq):
            rows = pl.ds(t, n_seq, stride=seq)
            sr, si = lr * sr - li * si + er_ref[rows, :], lr * si + li * sr + ei_ref[rows, :]
            er_ref[rows, :] = sr
            ei_ref[rows, :] = si
        y = y + (_dot(er_ref[...].astype(BF16), wc_ref[0, sl, :]) - _dot(ei_ref[...].astype(BF16), wc_ref[1, sl, :]))
        sre_ref[:, sl] = sr
        sim_ref[:, sl] = si
    g_ref[...] = _gelu_tanh(y).astype(BF16)


def ssm_step(proj, ssm_w, d_skip, h0, *, n_batch, seq, row0):
    wb, wc, pw = ssm_w
    n_tok = n_batch * seq
    assert row0 % n_tok == 0
    state_shape = jax.ShapeDtypeStruct((SSM_LB, n_batch, SSM_SB), F32)
    state_spec = pl.BlockSpec((None, n_batch, SSM_SB), lambda j: (j, 0, 0))
    g, s_re, s_im = pl.pallas_call(
        functools.partial(_ssm_step_kernel, seq=seq),
        grid=(SSM_LB,),
        in_specs=[pl.BlockSpec((n_tok, LANES), lambda j: (row0 // n_tok, j)),
                  pl.BlockSpec((None, LANES, 2 * SSM_SB), lambda j: (j, 0, 0)),
                  pl.BlockSpec((None, 2, SSM_SB, LANES), lambda j: (j, 0, 0, 0)),
                  pl.BlockSpec((None, 2, SSM_SEG, SSM_SB), lambda j: (j, 0, 0, 0)),
                  pl.BlockSpec((1, LANES), lambda j: (0, j)),
                  pl.BlockSpec((None, 2, n_batch, SSM_SB), lambda j: (j, 0, 0, 0))],
        out_specs=[pl.BlockSpec((n_tok, LANES), lambda j: (0, j)), state_spec, state_spec],
        out_shape=[jax.ShapeDtypeStruct((n_tok, SSM_WIDTH), BF16), state_shape, state_shape],
        scratch_shapes=[pltpu.VMEM((n_tok, LANES), F32), pltpu.VMEM((n_tok, LANES), F32)],
        compiler_params=_params(("arbitrary",)),
        name="ssm_step",
    )(proj, wb, wc, pw, d_skip[None, :], h0)

    def per_seq(s):
        return s.transpose(1, 0, 2).reshape(n_batch, SSM_GROUPS, SSM_STATE)

    return g, per_seq(s_re), per_seq(s_im)


def _row_sum(x):
    return jnp.sum(x, axis=1, keepdims=True)


def _row_count(mask):
    return _row_sum(jnp.where(mask, 1, 0))


I16 = jnp.int16
I16_MIN = -2 ** 15


def _count16(ref, cand, compare):
    accs = [None] * 4
    for t in range(ref.shape[1] // LANES):
        x = jnp.where(compare(ref[:, t * LANES:(t + 1) * LANES], cand), I16(1), I16(0))
        accs[t % 4] = x if accs[t % 4] is None else accs[t % 4] + x
    accs = [a for a in accs if a is not None]
    total = accs[0]
    for a in accs[1:]:
        total = total + a
    return _row_sum(total.astype(I32))


def _bisect16(ref, target):
    def step(i, base):
        cand = base + lax.shift_left(np.int32(1), np.int32(15) - i)
        cnt = _count16(ref, cand.astype(I16), lambda a, b: a >= b)
        return jnp.where(cnt >= target, cand, base)
    return lax.fori_loop(0, 16, step, jnp.full((ref.shape[0], 1), I16_MIN, I32))


def _bisect32(key_ref, n_sel):
    bq, n_keys = key_ref.shape
    hr = bq // 2

    def lane_counts(h, cand):
        accs = [None] * 4
        for t in range(n_keys // LANES):
            x = jnp.where(key_ref[h * hr:(h + 1) * hr, t * LANES:(t + 1) * LANES] >= cand, 1, 0)
            accs[t % 4] = x if accs[t % 4] is None else accs[t % 4] + x
        accs = [a for a in accs if a is not None]
        total = accs[0]
        for a in accs[1:]:
            total = total + a
        return total

    def decide(part, cand, base):
        return jnp.where(_row_sum(part) >= n_sel, cand, base)

    def bit(i):
        return lax.shift_left(np.int32(1), np.int32(31) - i)

    def body(i, state):
        base_a, base_b, part_b = state
        cand_a = base_a + bit(i)
        part_a = lane_counts(0, cand_a)
        base_b = decide(part_b, base_b + bit(i - 1), base_b)
        part_b = lane_counts(1, base_b + bit(i))
        return decide(part_a, cand_a, base_a), base_b, part_b

    base0 = jnp.full((hr, 1), INT_MIN, I32)
    first = base0 + bit(0)
    state = (decide(lane_counts(0, first), first, base0), base0, lane_counts(1, first))
    base_a, base_b, part_b = lax.fori_loop(1, 32, body, state)
    base_b = decide(part_b, base_b + bit(31), base_b)
    return jnp.concatenate([base_a, base_b], axis=0)


def _stack_heads(ref, heads):
    return jnp.concatenate([ref[:, h * LANES:(h + 1) * LANES] for h in heads], axis=0)


def _dsa_body(q_ref, qi_ref, wi_ref, k_ref, v_ref, ki_ref, o_ref, key_ref, bias_ref, hi_ref, lo_ref, p_ref,
              *, q_pos_first, s_valid, n_sel, packed_bisect, stack):
    bq, n_keys = key_ref.shape
    col = lax.broadcasted_iota(I32, (bq, n_keys), 1)
    qpos = q_pos_first + lax.broadcasted_iota(I32, (bq, 1), 0)
    allowed = col < jnp.minimum((qpos // CHUNK + 1) * CHUNK, s_valid)

    ki = ki_ref[...]
    score = None
    for h0 in range(0, IDX_HEADS, stack):
        d = _dot_nt(_stack_heads(qi_ref, range(h0, h0 + stack)), ki)
        for j in range(stack):
            t = jnp.maximum(d[j * bq:(j + 1) * bq], 0.0) * wi_ref[:, h0 + j:h0 + j + 1]
            score = t if score is None else score + t
    score = jnp.where(score == 0.0, 0.0, score)
    bits = pltpu.bitcast(score, I32)
    key = jnp.where(bits < 0, bits ^ np.int32(0x7FFFFFFF), bits)
    key = jnp.where(allowed, key, KEY_NEG_INF)
    key_ref[...] = key

    if packed_bisect:
        hi_ref[...] = (key >> 16).astype(I16)
        lo_ref[...] = ((key & 0xFFFF) + I16_MIN).astype(I16)
        thr_hi = _bisect16(hi_ref, n_sel)
        thr_hi16 = thr_hi.astype(I16)
        need_lo = n_sel - _count16(hi_ref, thr_hi16, lambda a, b: a > b)
        lo_ref[...] = jnp.where(hi_ref[...] == thr_hi16, lo_ref[...], I16(I16_MIN))
        thr_lo = _bisect16(lo_ref, need_lo)
        thr = lax.shift_left(thr_hi, np.int32(16)) + (thr_lo - I16_MIN)
    else:
        thr = _bisect32(key_ref, n_sel)
    thr = jnp.maximum(thr, KEY_NEG_INF)

    key = key_ref[...]
    need = n_sel - _row_count(key > thr)
    n_eq = _row_count(key == thr)
    n_bits = int(n_keys - 1).bit_length()

    def tie_cut():
        def step(i, j0):
            cand = j0 + lax.shift_left(np.int32(1), np.int32(n_bits - 1) - i)
            cnt = _row_sum(jnp.where(key_ref[...] == thr, jnp.where(col < cand, 1, 0), 0))
            return jnp.where(cnt < need, cand, j0)
        return lax.fori_loop(0, n_bits, step, jnp.zeros((bq, 1), I32))

    split = jnp.max(jnp.where(n_eq > need, 1, 0)) > 0
    j_last = lax.cond(split, tie_cut, lambda: jnp.full((bq, 1), n_keys, I32))
    tie_bias = jnp.where(thr == KEY_NEG_INF, -jnp.inf, 0.0)
    bias_ref[...] = jnp.where(key > thr, 0.0,
                              jnp.where(key == thr, jnp.where(col <= j_last, tie_bias, -jnp.inf), -jnp.inf))

    c = np.float32(HEAD_DIM ** -0.5 * np.log2(np.e))
    for h0 in range(0, N_HEADS, stack):
        kv = h0 // KV_GROUP
        heads = range(h0, h0 + stack)
        s_all = _dot_nt(_stack_heads(q_ref, heads), k_ref[:, kv * HEAD_DIM:(kv + 1) * HEAD_DIM])
        for g in range(stack):
            s = s_all[g * bq:(g + 1) * bq] + bias_ref[...]
            m = jnp.max(s, axis=1, keepdims=True)
            p_ref[g * bq:(g + 1) * bq, :] = jnp.exp2((s - m) * c).astype(BF16)
        pv = _dot(p_ref[0:stack * bq, :], v_ref[:, kv * V_AUG:(kv + 1) * V_AUG])
        for g, h in enumerate(heads):
            o = pv[g * bq:(g + 1) * bq]
            o_ref[:, h * HEAD_DIM:(h + 1) * HEAD_DIM] = (o[:, :HEAD_DIM] / o[:, HEAD_DIM:HEAD_DIM + 1]).astype(BF16)


def _dsa_scratch(bq, n_keys):
    return [pltpu.VMEM((bq, n_keys), I32), pltpu.VMEM((bq, n_keys), F32),
            pltpu.VMEM((bq, n_keys), I16), pltpu.VMEM((bq, n_keys), I16), pltpu.VMEM((KV_GROUP * bq, n_keys), BF16)]


def _dsa_kernel(q_ref, qi_ref, wi_ref, k_ref, v_ref, ki_ref, o_ref, *scratch, q_pos0, **static):
    bq = scratch[0].shape[0]
    _dsa_body(q_ref, qi_ref, wi_ref, k_ref, v_ref, ki_ref, o_ref, *scratch,
              q_pos_first=q_pos0 + pl.program_id(1) * bq, **static)


def dsa(q, qi, wi, k, v, ki, *, bq, q_blk0, n_qblk, n_keys, n_sel, packed_bisect, stack):
    n_batch, seq = q.shape[:2]

    def qspec(width):
        return pl.BlockSpec((None, bq, width), lambda b, i: (b, q_blk0 + i, 0))

    def kspec(width):
        return pl.BlockSpec((None, n_keys, width), lambda b, i: (b, 0, 0))

    return pl.pallas_call(
        functools.partial(_dsa_kernel, q_pos0=q_blk0 * bq, s_valid=seq, n_sel=n_sel, packed_bisect=packed_bisect,
                          stack=stack),
        grid=(n_batch, n_qblk),
        in_specs=[qspec(ATTN_WIDTH), qspec(IDX_HEADS * LANES), qspec(LANES), kspec(KV_WIDTH), kspec(N_KV_HEADS * V_AUG),
                  kspec(LANES)],
        out_specs=pl.BlockSpec((None, bq, ATTN_WIDTH), lambda b, i: (b, i, 0)),
        out_shape=jax.ShapeDtypeStruct((n_batch, n_qblk * bq, ATTN_WIDTH), BF16),
        scratch_shapes=_dsa_scratch(bq, n_keys),
        compiler_params=_params(("arbitrary", "arbitrary")),
        name="dsa",
    )(q, qi, wi, k, v, ki)


def _dsa_step_kernel(q_ref, qi_ref, wi_ref, ck_hbm, cv_hbm, cki_ref, nk_ref, nv_ref, nki_ref, o_ref,
                     k_buf, v_buf, ki_buf, cache_buf, sem, *scratch, past, n_sel):
    b = pl.program_id(0)

    def cache_copies(seq, slot):
        return [pltpu.make_async_copy(src.at[seq, :, h, :], cache_buf.at[slot, a, h], sem.at[slot])
                for a, src in enumerate((ck_hbm, cv_hbm)) for h in range(N_KV_HEADS)]

    @pl.when(b == 0)
    def _():
        for cp in cache_copies(0, 0):
            cp.start()

    @pl.when(b + 1 < pl.num_programs(0))
    def _():
        for cp in cache_copies(b + 1, (b + 1) % 2):
            cp.start()

    slot = b % 2
    for cp in cache_copies(b, slot):
        cp.wait()

    ts = nk_ref.shape[0]
    n_keys = k_buf.shape[0]
    for h in range(N_KV_HEADS):
        k_buf[0:past, h * HEAD_DIM:(h + 1) * HEAD_DIM] = cache_buf[slot, 0, h].astype(BF16)
    _store_v_aug(v_buf, 0, [cache_buf[slot, 1, h] for h in range(N_KV_HEADS)])
    for buf, new in ((k_buf, nk_ref), (v_buf, nv_ref)):
        buf[past:past + ts, :] = new[...]
        buf[past + ts:n_keys, :] = jnp.zeros((n_keys - past - ts, buf.shape[1]), BF16)
    ki_buf[0:past, 0:IDX_DIM] = cki_ref[...].astype(BF16)
    ki_buf[0:past, IDX_DIM:LANES] = jnp.zeros((past, LANES - IDX_DIM), BF16)
    ki_buf[past:past + ts, :] = nki_ref[...]
    ki_buf[past + ts:n_keys, :] = jnp.zeros((n_keys - past - ts, LANES), BF16)
    _dsa_body(q_ref, qi_ref, wi_ref, k_buf, v_buf, ki_buf, o_ref, *scratch,
              q_pos_first=past, s_valid=past + ts, n_sel=n_sel, packed_bisect=True, stack=KV_GROUP)


def dsa_step(q, qi, wi, cache_k, cache_v, cache_ki, k_new, v_new, ki_new, *, n_sel):
    n_batch, ts = q.shape[:2]
    past = cache_k.shape[1]
    n_keys = -(-(past + ts) // LANES) * LANES

    def spec(rows, width):
        return pl.BlockSpec((None, rows, width), lambda b: (b, 0, 0))

    return pl.pallas_call(
        functools.partial(_dsa_step_kernel, past=past, n_sel=n_sel),
        grid=(n_batch,),
        in_specs=[spec(ts, ATTN_WIDTH), spec(ts, IDX_HEADS * LANES), spec(ts, LANES),
                  pl.BlockSpec(memory_space=pl.ANY), pl.BlockSpec(memory_space=pl.ANY), spec(past, IDX_DIM),
                  spec(ts, KV_WIDTH), spec(ts, N_KV_HEADS * V_AUG), spec(ts, LANES)],
        out_specs=spec(ts, ATTN_WIDTH),
        out_shape=jax.ShapeDtypeStruct((n_batch, ts, ATTN_WIDTH), BF16),
        scratch_shapes=[pltpu.VMEM((n_keys, KV_WIDTH), BF16), pltpu.VMEM((n_keys, N_KV_HEADS * V_AUG), BF16),
                        pltpu.VMEM((n_keys, LANES), BF16),
                        pltpu.VMEM((2, 2, N_KV_HEADS, past, HEAD_DIM), F32), pltpu.SemaphoreType.DMA((2,)),
                        *_dsa_scratch(ts, n_keys)],
        compiler_params=_params(("arbitrary",)),
        name="dsa_step",
    )(q, qi, wi, cache_k, cache_v, cache_ki, k_new, v_new, ki_new)


def _merge_kernel(g_ref, a_ref, ga_ref, gb_ref, wv_ref, wg_ref, wb_ref, o_ref):
    g = g_ref[...]
    branch_a = _dot(g, wv_ref[...]) * jax.nn.sigmoid(_dot(g, wg_ref[...]))
    branch_b = _dot(a_ref[...], wb_ref[...])
    merged = jax.nn.sigmoid(ga_ref[...]) * branch_a + jax.nn.sigmoid(gb_ref[...]) * branch_b
    o_ref[...] = merged.astype(BF16)


def merge(g, attn, proj, w_val, w_gate, w_branch, *, tm=1024, tn=512):
    n_tok = g.shape[0]
    nj = D_MODEL // tn

    def wspec():
        return pl.BlockSpec((SSM_WIDTH, tn), lambda i, j: (0, j))

    return pl.pallas_call(
        _merge_kernel,
        grid=(n_tok // tm, nj),
        in_specs=[pl.BlockSpec((tm, SSM_WIDTH), lambda i, j: (i, 0)),
                  pl.BlockSpec((tm, ATTN_WIDTH), lambda i, j: (i, 0)),
                  pl.BlockSpec((tm, tn), lambda i, j: (i, COL_GA // tn + j)),
                  pl.BlockSpec((tm, tn), lambda i, j: (i, COL_GB // tn + j)),
                  wspec(), wspec(), wspec()],
        out_specs=pl.BlockSpec((tm, tn), lambda i, j: (i, j)),
        out_shape=jax.ShapeDtypeStruct((n_tok, D_MODEL), BF16),
        compiler_params=_params(("arbitrary", "arbitrary")),
        name="merge",
    )(g, attn, proj, proj, w_val, w_gate, w_branch)


ROUTER_COLS = N_EXPERT_GROUPS + N_EXPERTS
MOE_TM = 256


def _first_lane_of_max(x, lane_f):
    m = jnp.max(x, axis=1, keepdims=True)
    return m, jnp.min(jnp.where(x == m, lane_f, float(LANES)), axis=1, keepdims=True)


def _out_proj_kernel(x_ref, m_ref, wo_ref, gn_ref, wr_ref, br_ref, cin_ref,
                     h_ref, hn_ref, ri_ref, rw_ref, cnt_ref, carry_ref):
    @pl.when(pl.program_id(0) == 0)
    def _():
        carry_ref[...] = cin_ref[...]

    h = x_ref[...] + _dot(m_ref[...], wo_ref[...])
    h_ref[...] = h
    ms = jnp.mean(h * h, axis=-1, keepdims=True)
    hn = h * lax.rsqrt(ms + EPS) * gn_ref[...]
    hn_ref[...] = hn
    hh, hl = _split_bf16(hn)
    both = _dot(hh, wr_ref[...])
    lg = both[:, :LANES] + _dot(hl, wr_ref[:, 0:LANES]) + both[:, LANES:] + br_ref[...]

    tm = lg.shape[0]
    lane = lax.broadcasted_iota(I32, lg.shape, 1)
    lane_f = lane.astype(F32)
    ninf = -jnp.inf
    gl = jnp.where(lane < N_EXPERT_GROUPS, lg, ninf)
    gmax, gsel = _first_lane_of_max(gl, lane_f)
    g_w = 1.0 / jnp.sum(jnp.exp(gl - gmax), axis=1, keepdims=True)
    lo = N_EXPERT_GROUPS + EXPERTS_PER_GROUP * gsel
    el = jnp.where(lane_f >= lo, jnp.where(lane_f < lo + EXPERTS_PER_GROUP, lg, ninf), ninf)
    v1, i1 = _first_lane_of_max(el, lane_f)
    el2 = jnp.where(lane_f == i1, ninf, el)
    v2, i2 = _first_lane_of_max(el2, lane_f)
    t = jnp.exp(v2 - v1)
    s1 = 1.0 / (1.0 + t)
    w1 = s1 * g_w
    w2 = (t * s1) * g_w

    m1 = jnp.where(lane_f == i1, 1.0, 0.0)
    m2 = jnp.where(lane_f == i2, 1.0, 0.0)
    both = m1 + m2
    tri = jnp.where(lax.broadcasted_iota(I32, (tm, tm), 0) > lax.broadcasted_iota(I32, (tm, tm), 1), 1.0, 0.0)
    before = _dot(tri.astype(BF16), both.astype(BF16)) + carry_ref[...]
    r1 = jnp.sum(before * m1, axis=1, keepdims=True)
    r2 = jnp.sum(before * m2, axis=1, keepdims=True)
    carry_ref[...] = carry_ref[...] + jnp.sum(both, axis=0, keepdims=True)
    cnt_ref[...] = carry_ref[...]
    e1 = i1 - float(N_EXPERT_GROUPS)
    e2 = i2 - float(N_EXPERT_GROUPS)
    fields = jnp.where(lane == 0, e1, jnp.where(lane == 1, e2, jnp.where(lane == 2, r1, jnp.where(lane == 3, r2, 0.0))))
    ri_ref[...] = fields.T[0:SUBLANES, :].astype(I32)
    rw_ref[...] = jnp.where(lane == 0, w1, jnp.where(lane == 1, w2, 0.0))


def _router_weights(w_router_group, b_router_group, w_router_expert, b_router_expert):
    wr = jnp.concatenate([w_router_group, w_router_expert, jnp.zeros((D_MODEL, LANES - ROUTER_COLS), F32)], axis=1)
    wr_hi = wr.astype(BF16)
    wr_lo = (wr - wr_hi.astype(F32)).astype(BF16)
    br = jnp.concatenate([b_router_group, b_router_expert, jnp.zeros((LANES - ROUTER_COLS,), F32)])[None, :]
    return jnp.concatenate([wr_hi, wr_lo], axis=1), br


def out_proj(x, merged, w_out, ffn_gain, router_w, counts_in, *, tm=256):
    n_tok = x.shape[0]
    wr, br = router_w

    def row(width):
        return pl.BlockSpec((tm, width), lambda i: (i, 0))

    def const(shape):
        return pl.BlockSpec(shape, lambda i: (0, 0), pipeline_mode=pl.Buffered(1))

    return pl.pallas_call(
        _out_proj_kernel,
        grid=(n_tok // tm,),
        in_specs=[row(D_MODEL), row(D_MODEL), const((D_MODEL, D_MODEL)), const((1, D_MODEL)),
                  const((D_MODEL, 2 * LANES)), const((1, LANES)), const((1, LANES))],
        out_specs=[row(D_MODEL), row(D_MODEL), pl.BlockSpec((SUBLANES, tm), lambda i: (0, i)), row(LANES),
                   pl.BlockSpec((1, LANES), lambda i: (0, 0))],
        out_shape=[jax.ShapeDtypeStruct((n_tok, D_MODEL), F32), jax.ShapeDtypeStruct((n_tok, D_MODEL), F32),
                   jax.ShapeDtypeStruct((SUBLANES, n_tok), I32), jax.ShapeDtypeStruct((n_tok, LANES), F32),
                   jax.ShapeDtypeStruct((1, LANES), F32)],
        scratch_shapes=[pltpu.VMEM((1, LANES), F32)],
        compiler_params=_params(("arbitrary",)),
        name="out_proj",
    )(x, merged, w_out, ffn_gain[None, :], wr, br, counts_in)


def _block_layout(counts):
    padded = (counts + MOE_TM - 1) // MOE_TM * MOE_TM
    pad_end = jnp.cumsum(padded).astype(I32)
    pad_start = pad_end - padded
    n_used = pad_end[-1] // MOE_TM
    return pad_start, pad_end, n_used


def _dest_kernel(ps_ref, ri_ref, o_ref):
    ri = ri_ref[...]
    start = jnp.zeros_like(ri)
    for k in range(N_EXPERTS):
        start = jnp.where(ri == k, ps_ref[k], start)
    o_ref[...] = start + pltpu.roll(ri, SUBLANES - TOP_K, 0)


def dest_rows(route_i, pad_start):
    grid_spec = pltpu.PrefetchScalarGridSpec(
        num_scalar_prefetch=1, grid=(1,),
        in_specs=[pl.BlockSpec(route_i.shape, lambda i, ps: (0, 0))],
        out_specs=pl.BlockSpec(route_i.shape, lambda i, ps: (0, 0)))
    return pl.pallas_call(_dest_kernel, grid_spec=grid_spec, out_shape=jax.ShapeDtypeStruct(route_i.shape, I32),
                          compiler_params=_params(("arbitrary",)), name="dest_rows")(pad_start, route_i)


def _moe_rows(n_tok):
    return -(-(n_tok * TOP_K + N_EXPERTS * (MOE_TM - 1)) // MOE_TM) * MOE_TM


DISPATCH_TM = 512


def _wait_rows(src_hbm, dst, sem, n_rows):
    pltpu.make_async_copy(src_hbm.at[pl.ds(0, n_rows)], dst, sem).wait()


def _dispatch_kernel(d0_ref, d1_ref, pe_ref, cnt_ref, nu_ref, hna_ref, hnb_ref, xs_hbm, zbuf, sem, semz,
                     *, n_blocks, a_tiles):
    i = pl.program_id(0)

    def zero_block(row0):
        return pltpu.make_async_copy(zbuf, xs_hbm.at[pl.ds(pl.multiple_of(row0, MOE_TM), MOE_TM)], semz)

    @pl.when(i == 0)
    def _():
        zbuf[...] = jnp.zeros_like(zbuf)
        for start in (True, False):
            for e in range(N_EXPERTS):
                @pl.when(cnt_ref[e] > 0)
                def _():
                    cp = zero_block(pe_ref[e] - MOE_TM)
                    cp.start() if start else cp.wait()

            def tail(b, c):
                cp = zero_block(b * MOE_TM)
                cp.start() if start else cp.wait()
                return c
            lax.fori_loop(nu_ref[0], n_blocks, tail, 0)

    base = i * DISPATCH_TM

    def scatter(hn_ref):
        def body(r, c):
            src = hn_ref.at[pl.ds(r, 1)]
            pltpu.make_async_copy(src, xs_hbm.at[pl.ds(d0_ref[base + r], 1)], sem).start()
            pltpu.make_async_copy(src, xs_hbm.at[pl.ds(d1_ref[base + r], 1)], sem).start()
            return c
        lax.fori_loop(0, DISPATCH_TM, body, 0, unroll=8)
        for _ in range(TOP_K):
            pltpu.make_async_copy(hn_ref, xs_hbm.at[pl.ds(0, DISPATCH_TM)], sem).wait()

    @pl.when(i < a_tiles)
    def _():
        scatter(hna_ref)

    @pl.when(i >= a_tiles)
    def _():
        scatter(hnb_ref)


def dispatch(hn_a, hn_b, dest0, dest1, pad_end, counts, n_used):
    a_tiles, b_tiles = hn_a.shape[0] // DISPATCH_TM, hn_b.shape[0] // DISPATCH_TM
    rows = _moe_rows(hn_a.shape[0] + hn_b.shape[0])
    grid_spec = pltpu.PrefetchScalarGridSpec(
        num_scalar_prefetch=5,
        grid=(a_tiles + b_tiles,),
        in_specs=[pl.BlockSpec((DISPATCH_TM, D_MODEL), lambda i, *_: (jnp.minimum(i, a_tiles - 1), 0)),
                  pl.BlockSpec((DISPATCH_TM, D_MODEL), lambda i, *_: (jnp.maximum(i - a_tiles, 0), 0))],
        out_specs=pl.BlockSpec(memory_space=pl.ANY),
        scratch_shapes=[pltpu.VMEM((MOE_TM, D_MODEL), F32), pltpu.SemaphoreType.DMA(()), pltpu.SemaphoreType.DMA(())],
    )
    return pl.pallas_call(
        functools.partial(_dispatch_kernel, n_blocks=rows // MOE_TM, a_tiles=a_tiles),
        grid_spec=grid_spec,
        out_shape=jax.ShapeDtypeStruct((rows, D_MODEL), F32),
        compiler_params=_params(("arbitrary",)),
        name="dispatch",
    )(dest0, dest1, pad_end, counts, n_used, hn_a, hn_b)


MOE_UNITS = 8
MOE_UG = D_MODEL // MOE_UNITS
MOE_UD = EXPERT_FF // MOE_UNITS


def _moe_kernel(blk_e_ref, nu_ref, nxt_ref, upb_ref, xs_ref, wg_hbm, wu_hbm, wd_hbm, ys_ref,
                wg_bf, wu_bf, wd_bf, stg_g, stg_u, stg_d, sem, st_ref):
    i = pl.program_id(0)
    cur_slot, pos, cur_e = 0, 1, 2

    def unit_copies(e, unit, s):
        g_rows = pl.ds(pl.multiple_of(unit * MOE_UG, MOE_UG), MOE_UG)
        d_rows = pl.ds(pl.multiple_of(unit * MOE_UD, MOE_UD), MOE_UD)
        return (pltpu.make_async_copy(wg_hbm.at[e, g_rows, :], stg_g.at[s], sem.at[s]),
                pltpu.make_async_copy(wu_hbm.at[e, g_rows, :], stg_u.at[s], sem.at[s]),
                pltpu.make_async_copy(wd_hbm.at[e, d_rows, :], stg_d.at[s], sem.at[s]))

    def start_unit(e, unit):
        for cp in unit_copies(e, unit, unit % 2):
            cp.start()

    def begin_load(e):
        st_ref[pos] = 0
        start_unit(e, 0)
        start_unit(e, 1)

    def advance(e, slot, n):
        def body(_, c):
            unit = st_ref[pos]

            @pl.when(unit < MOE_UNITS)
            def _():
                s = unit % 2
                for cp in unit_copies(e, unit, s):
                    cp.wait()
                g_rows = pl.ds(pl.multiple_of(unit * MOE_UG, MOE_UG), MOE_UG)
                d_rows = pl.ds(pl.multiple_of(unit * MOE_UD, MOE_UD), MOE_UD)
                wg_bf[slot, g_rows, :] = stg_g[s].astype(BF16)
                wu_bf[slot, g_rows, :] = stg_u[s].astype(BF16)
                wd_bf[slot, d_rows, :] = stg_d[s].astype(BF16)

                @pl.when(unit + 2 < MOE_UNITS)
                def _():
                    start_unit(e, unit + 2)
                st_ref[pos] = unit + 1
            return c
        lax.fori_loop(0, n, body, 0)

    def load_next(nxt):
        @pl.when(nxt >= 0)
        def _():
            begin_load(nxt)

        @pl.when(nxt < 0)
        def _():
            st_ref[pos] = MOE_UNITS

    @pl.when(i < nu_ref[0])
    def _():
        e = blk_e_ref[i]
        nxt = nxt_ref[i]

        @pl.when(i == 0)
        def _():
            st_ref[cur_slot] = 0
            st_ref[cur_e] = e
            begin_load(e)
            advance(e, 0, MOE_UNITS)
            load_next(nxt)

        @pl.when(jnp.logical_and(i > 0, e != st_ref[cur_e]))
        def _():
            slot = 1 - st_ref[cur_slot]
            advance(e, slot, MOE_UNITS)
            st_ref[cur_slot] = slot
            st_ref[cur_e] = e
            load_next(nxt)

        slot = st_ref[cur_slot]
        x = xs_ref[...].astype(BF16)
        hg = _dot(x, wg_bf[slot])
        hu = _dot(x, wu_bf[slot])
        hmid = (jax.nn.silu(hg) * hu).astype(BF16)
        ys_ref[...] = _dot(hmid, wd_bf[slot])

        @pl.when(nxt >= 0)
        def _():
            advance(nxt, 1 - slot, upb_ref[i])

    @pl.when(i >= nu_ref[0])
    def _():
        ys_ref[...] = jnp.zeros_like(ys_ref)


def moe(xs, blk_e, n_used, nxt_e, units_per_block, w_gate, w_up, w_down):
    rows = xs.shape[0]
    grid_spec = pltpu.PrefetchScalarGridSpec(
        num_scalar_prefetch=4,
        grid=(rows // MOE_TM,),
        in_specs=[pl.BlockSpec((MOE_TM, D_MODEL), lambda i, be, nu, nx, ub: (jnp.minimum(i, nu[0] - 1), 0)),
                  pl.BlockSpec(memory_space=pl.ANY), pl.BlockSpec(memory_space=pl.ANY), pl.BlockSpec(memory_space=pl.ANY)],
        out_specs=pl.BlockSpec((MOE_TM, D_MODEL), lambda i, be, nu, nx, ub: (i, 0)),
        scratch_shapes=[pltpu.VMEM((2, D_MODEL, EXPERT_FF), BF16), pltpu.VMEM((2, D_MODEL, EXPERT_FF), BF16),
                        pltpu.VMEM((2, EXPERT_FF, D_MODEL), BF16),
                        pltpu.VMEM((2, MOE_UG, EXPERT_FF), F32), pltpu.VMEM((2, MOE_UG, EXPERT_FF), F32),
                        pltpu.VMEM((2, MOE_UD, D_MODEL), F32),
                        pltpu.SemaphoreType.DMA((2,)), pltpu.SMEM((3,), I32)],
    )
    return pl.pallas_call(
        _moe_kernel,
        grid_spec=grid_spec,
        out_shape=jax.ShapeDtypeStruct((rows, D_MODEL), F32),
        compiler_params=_params(("arbitrary",)),
        name="moe",
    )(blk_e, n_used, nxt_e, units_per_block, xs, w_gate, w_up, w_down)


def _combine_kernel(r0_ref, r1_ref, ys_hbm, h_ref, w_ref, o_ref, buf, sem, *, tm, tok0):
    i = pl.program_id(0)

    def issue(block, slot):
        base = tok0 + block * tm

        def body(r, carry):
            for k, idx_ref in enumerate((r0_ref, r1_ref)):
                pltpu.make_async_copy(ys_hbm.at[pl.ds(idx_ref[base + r], 1)], buf.at[slot, k, pl.ds(r, 1)],
                                      sem.at[slot]).start()
            return carry
        lax.fori_loop(0, tm, body, 0, unroll=8)

    @pl.when(i == 0)
    def _():
        issue(0, 0)

    @pl.when(i + 1 < pl.num_programs(0))
    def _():
        issue(i + 1, (i + 1) % 2)

    slot = i % 2
    _wait_rows(ys_hbm, buf.at[slot, 0], sem.at[slot], tm)
    _wait_rows(ys_hbm, buf.at[slot, 1], sem.at[slot], tm)
    w = w_ref[...]
    o_ref[...] = h_ref[...] + (buf[slot, 0] * w[:, 0:1] + buf[slot, 1] * w[:, 1:2])


def combine(ys, h, route_w, rows0, rows1, *, tok0, tm=256):
    n_tok = h.shape[0]
    grid_spec = pltpu.PrefetchScalarGridSpec(
        num_scalar_prefetch=2,
        grid=(n_tok // tm,),
        in_specs=[pl.BlockSpec(memory_space=pl.ANY),
                  pl.BlockSpec((tm, D_MODEL), lambda i, a, b: (i, 0)),
                  pl.BlockSpec((tm, LANES), lambda i, a, b: (i, 0))],
        out_specs=pl.BlockSpec((tm, D_MODEL), lambda i, a, b: (i, 0)),
        scratch_shapes=[pltpu.VMEM((2, 2, tm, D_MODEL), F32), pltpu.SemaphoreType.DMA((2,))],
    )
    return pl.pallas_call(
        functools.partial(_combine_kernel, tm=tm, tok0=tok0),
        grid_spec=grid_spec,
        out_shape=jax.ShapeDtypeStruct((n_tok, D_MODEL), F32),
        compiler_params=_params(("arbitrary",)),
        name="combine",
    )(rows0, rows1, ys, h, route_w)


def _regroup_w_in(w_in):
    sizes = (SSM_WIDTH, ATTN_WIDTH, KV_WIDTH, KV_WIDTH, IDX_HEADS * IDX_DIM, IDX_DIM, IDX_HEADS, D_MODEL, D_MODEL)
    u, q, k, v, qi, ki, wi, ga, gb = jnp.split(w_in, np.cumsum(sizes)[:-1].tolist(), axis=1)
    pad = jnp.zeros((D_MODEL, PROJ_COLS - COL_KIWI - IDX_DIM - IDX_HEADS), F32)
    return jnp.concatenate([u, q, ga, gb, k, v, qi, ki, wi, pad], axis=1).astype(BF16)


def _layer(x_p, x_s, cache_k, cache_v, cache_ki, h0_re, h0_im, p):
    bp, tp, _ = x_p.shape
    bs, ts, _ = x_s.shape
    past = cache_k.shape[1]
    n_p, n_s = bp * tp, bs * ts
    n_tok = n_p + n_s

    w_in = _regroup_w_in(p['w_in'])
    ssm_w = _ssm_weights(p['ssm_A_re'], p['ssm_A_im'], p['ssm_log_dt'], p['ssm_B_re'], p['ssm_B_im'],
                         p['ssm_C_re'], p['ssm_C_im'])
    glu_w = (p['w_glu_val'].astype(BF16), p['w_glu_gate'].astype(BF16), p['w_attn_branch'].astype(BF16))
    w_out = p['w_out'].astype(BF16)
    router_w = _router_weights(p['w_router_group'], p['b_router_group'], p['w_router_expert'], p['b_router_expert'])
    seq_tiles = tp // QK_TM

    def front(x, table_pos, table_block):
        proj = in_proj(x, p['norm_mix_g'][None, :], w_in)
        return proj, qk_post(proj, table_pos, table_block, p['q_norm_g'], p['k_norm_g'], p['idx_k_norm_g'])

    def seqs(a, b, t):
        return a.reshape(b, t, a.shape[-1])

    xp = x_p.reshape(n_p, D_MODEL)
    proj_p, (q_b, kf_p, k_b, vf_p, v_b, qi_b, kif_p, ki_b, wi) = front(
        xp, jnp.arange(tp, dtype=I32), lambda i: i % seq_tiles)
    g_p, sre_p, sim_p = ssm(proj_p, ssm_w, p['ssm_D'], jnp.zeros((bp, SSM_LB, 2, SSM_SB), F32),
                            n_batch=bp, seq=tp, row0=0)
    bq = 128
    n_buckets = min(16, tp // bq)
    per = tp // bq // n_buckets
    qp, qip, wip = seqs(q_b, bp, tp), seqs(qi_b, bp, tp), seqs(wi, bp, tp)
    kp, vp, kip = seqs(k_b, bp, tp), seqs(v_b, bp, tp), seqs(ki_b, bp, tp)
    attn_p = jnp.concatenate(
        [dsa(qp, qip, wip, kp, vp, kip, bq=bq, q_blk0=n * per, n_qblk=per, n_keys=(n + 1) * per * bq,
             n_sel=min(IDX_TOPK, tp // 4), packed_bisect=False, stack=1)
         for n in range(n_buckets)], axis=1).reshape(n_p, ATTN_WIDTH)
    merged_p = merge(g_p, attn_p, proj_p, *glu_w)
    h_p, hn_p, ri_p, rw_p, cnt_p = out_proj(xp, merged_p, w_out, p['norm_ffn_g'], router_w, jnp.zeros((1, LANES), F32))

    xs_ = x_s.reshape(n_s, D_MODEL)
    proj_s, (q_b, kf_s, k_b, vf_s, v_b, qi_b, kif_s, ki_b, wi) = front(
        xs_, jnp.tile(past + jnp.arange(ts, dtype=I32), QK_TM // ts), lambda i: 0)
    h0 = jnp.stack([h0_re.reshape(bs, SSM_LB, SSM_SB), h0_im.reshape(bs, SSM_LB, SSM_SB)]).transpose(2, 0, 1, 3)
    g_s, sre_s, sim_s = ssm_step(proj_s, ssm_w, p['ssm_D'], h0, n_batch=bs, seq=ts, row0=0)
    attn_s = dsa_step(seqs(q_b, bs, ts), seqs(qi_b, bs, ts), seqs(wi, bs, ts),
                      cache_k, cache_v, cache_ki,
                      seqs(k_b, bs, ts), seqs(v_b, bs, ts), seqs(ki_b, bs, ts),
                      n_sel=min(IDX_TOPK, (past + ts) // 4)).reshape(n_s, ATTN_WIDTH)
    merged_s = merge(g_s, attn_s, proj_s, *glu_w)
    h_s, hn_s, ri_s, rw_s, cnt = out_proj(xs_, merged_s, w_out, p['norm_ffn_g'], router_w, cnt_p)

    counts = cnt[0, N_EXPERT_GROUPS:ROUTER_COLS].astype(I32)
    pad_start, pad_end, n_used = _block_layout(counts)
    dest = dest_rows(jnp.concatenate([ri_p, ri_s], axis=1), pad_start)
    dest0, dest1 = dest[0], dest[1]
    n_blocks = _moe_rows(n_tok) // MOE_TM
    blk = jnp.minimum(jnp.arange(n_blocks, dtype=I32), n_used - 1)
    blk_e = jnp.minimum(jnp.sum((pad_end[None, :] <= (blk * MOE_TM)[:, None]).astype(I32), axis=1), N_EXPERTS - 1)
    after = pad_end[blk_e] // MOE_TM
    nxt_e = jnp.where(after < n_used, blk_e[jnp.minimum(after, n_blocks - 1)], -1).astype(I32)
    blocks_of_e = jnp.maximum((pad_end - pad_start)[blk_e] // MOE_TM, 1)
    units_per_block = ((MOE_UNITS + blocks_of_e - 1) // blocks_of_e).astype(I32)
    n_used = n_used.reshape(1)

    xs = dispatch(hn_p, hn_s, dest0, dest1, pad_end, counts, n_used)
    ys = moe(xs, blk_e, n_used, nxt_e, units_per_block, p['w_exp_gate'], p['w_exp_up'], p['w_exp_down'])
    y_p = combine(ys, h_p, rw_p, dest0, dest1, tok0=0).reshape(bp, tp, D_MODEL)
    y_s = combine(ys, h_s, rw_s, dest0, dest1, tok0=n_p).reshape(bs, ts, D_MODEL)

    def heads(a, b, t):
        return a.reshape(b, t, N_KV_HEADS, HEAD_DIM)

    new_p = (heads(kf_p, bp, tp), heads(vf_p, bp, tp), kif_p.reshape(bp, tp, IDX_DIM), sre_p, sim_p)
    new_s = (heads(kf_s, bs, ts), heads(vf_s, bs, ts), kif_s.reshape(bs, ts, IDX_DIM), sre_s, sim_s)
    return y_p, y_s, new_p, new_s


def kernel(x_prompt, x_sample, cache_k, cache_v, cache_idx_k, state_ssm_re, state_ssm_im, norm_mix_g, w_in, q_norm_g, k_norm_g, idx_k_norm_g, ssm_A_re, ssm_A_im, ssm_log_dt, ssm_B_re, ssm_B_im, ssm_C_re, ssm_C_im, ssm_D, w_glu_val, w_glu_gate, w_attn_branch, w_out, norm_ffn_g, w_router_group, b_router_group, w_router_expert, b_router_expert, w_exp_gate, w_exp_up, w_exp_down):
    depth = w_in.shape[0]
    assert depth == 1, "prompt and sample tokens are batched through one layer"
    names = ('norm_mix_g', 'w_in', 'q_norm_g', 'k_norm_g', 'idx_k_norm_g', 'ssm_A_re', 'ssm_A_im', 'ssm_log_dt',
             'ssm_B_re', 'ssm_B_im', 'ssm_C_re', 'ssm_C_im', 'ssm_D', 'w_glu_val', 'w_glu_gate', 'w_attn_branch',
             'w_out', 'norm_ffn_g', 'w_router_group', 'b_router_group', 'w_router_expert', 'b_router_expert',
             'w_exp_gate', 'w_exp_up', 'w_exp_down')
    vals = (norm_mix_g, w_in, q_norm_g, k_norm_g, idx_k_norm_g, ssm_A_re, ssm_A_im, ssm_log_dt, ssm_B_re, ssm_B_im,
            ssm_C_re, ssm_C_im, ssm_D, w_glu_val, w_glu_gate, w_attn_branch, w_out, norm_ffn_g, w_router_group,
            b_router_group, w_router_expert, b_router_expert, w_exp_gate, w_exp_up, w_exp_down)
    p = {n: v[0] for n, v in zip(names, vals)}
    y_p, y_s, new_p, new_s = _layer(x_prompt, x_sample, cache_k[0], cache_v[0], cache_idx_k[0],
                                    state_ssm_re[0], state_ssm_im[0], p)
    st_p = tuple(a[None] for a in new_p)
    st_s = tuple(a[None] for a in new_s)
    return (y_p, y_s) + st_p + st_s
```

```python
import functools

import numpy as np
import jax
import jax.numpy as jnp
from jax import lax
from jax.experimental import pallas as pl
from jax.experimental.pallas import tpu as pltpu

F32 = jnp.float32
BF16 = jnp.bfloat16
I32 = jnp.int32

D_MODEL = 2048
CHUNK = 64
SSM_WIDTH = 1024
SSM_GROUP = 16
SSM_GROUPS = 64
SSM_STATE = 64
ATTN_WIDTH = 1024
HEAD_DIM = 128
N_HEADS = 8
N_KV_HEADS = 2
KV_GROUP = 4
IDX_HEADS = 8
IDX_DIM = 64
IDX_TOPK = 256
ROPE_THETA = 500000.0
N_EXPERT_GROUPS = 4
EXPERTS_PER_GROUP = 8
N_EXPERTS = 32
TOP_K = 2
EXPERT_FF = 1024
EPS = 1e-6

LANES = 128
SUBLANES = 8
VMEM_LIMIT = 56 * 1024 * 1024

COL_U, COL_Q, COL_GA, COL_GB, COL_K, COL_V, COL_QI, COL_KIWI = 0, 1024, 2048, 4096, 6144, 6400, 6656, 7168
PROJ_COLS = 7296
PROJ_TN = 2432
KV_WIDTH = N_KV_HEADS * HEAD_DIM

SSM_LB = SSM_WIDTH // LANES
SSM_SB = 8 * SSM_STATE

INT_MIN = np.int32(-2 ** 31)
KEY_NEG_INF = np.int32(np.array([0xFF800000], np.uint32).view(np.int32)[0] ^ 0x7FFFFFFF)


def _params(sem, vmem=VMEM_LIMIT):
    return pltpu.CompilerParams(dimension_semantics=sem, vmem_limit_bytes=vmem)


def _dot(a, b):
    return jnp.dot(a, b, preferred_element_type=F32)


def _dot_nt(a, b):
    return lax.dot_general(a, b, (((1,), (1,)), ((), ())), preferred_element_type=F32)


def _split_bf16(x):
    hi = x.astype(BF16)
    lo = (x - hi.astype(F32)).astype(BF16)
    return hi, lo


def _in_proj_kernel(x_ref, g_ref, w_ref, o_ref, xn_ref):
    @pl.when(pl.program_id(1) == 0)
    def _():
        x = x_ref[...]
        ms = jnp.mean(x * x, axis=-1, keepdims=True)
        xn_ref[...] = (x * lax.rsqrt(ms + EPS) * g_ref[...]).astype(BF16)

    o_ref[...] = _dot(xn_ref[...], w_ref[...])


def in_proj(x, gain, w_bf16, *, tm=512):
    n_tok = x.shape[0]
    return pl.pallas_call(
        _in_proj_kernel,
        grid=(n_tok // tm, PROJ_COLS // PROJ_TN),
        in_specs=[pl.BlockSpec((tm, D_MODEL), lambda i, j: (i, 0)),
                  pl.BlockSpec((1, D_MODEL), lambda i, j: (0, 0)),
                  pl.BlockSpec((D_MODEL, PROJ_TN), lambda i, j: (0, j))],
        out_specs=pl.BlockSpec((tm, PROJ_TN), lambda i, j: (i, j)),
        out_shape=jax.ShapeDtypeStruct((n_tok, PROJ_COLS), F32),
        scratch_shapes=[pltpu.VMEM((tm, D_MODEL), BF16)],
        compiler_params=_params(("arbitrary", "arbitrary")),
        name="in_proj",
    )(x, gain, w_bf16)


def _rope(x, c, s_lo, s_hi, half):
    n = x.shape[-1]
    return x * c + pltpu.roll(x, n - half, 1) * s_lo + pltpu.roll(x, half, 1) * s_hi


def _head_norm(x, g):
    ms = jnp.mean(x * x, axis=-1, keepdims=True)
    return x * lax.rsqrt(ms + EPS) * g


V_AUG = 2 * HEAD_DIM


def _store_v_aug(dst_ref, row0, v_heads):
    n = v_heads[0].shape[0]
    one_col = jnp.where(lax.broadcasted_iota(I32, (n, HEAD_DIM), 1) == 0, 1.0, 0.0).astype(BF16)
    for h, v in enumerate(v_heads):
        dst_ref[row0:row0 + n, h * V_AUG:h * V_AUG + HEAD_DIM] = v.astype(BF16)
        dst_ref[row0:row0 + n, h * V_AUG + HEAD_DIM:(h + 1) * V_AUG] = one_col


def _qk_post_kernel(q_ref, k_ref, v_ref, qi_ref, kw_ref, c128_ref, sl128_ref, sh128_ref,
                    c64_ref, sl64_ref, sh64_ref, qg_ref, kg_ref, ig_ref,
                    qo_ref, kf_ref, kb_ref, vf_ref, vb_ref, qio_ref, kif_ref, kib_ref, wo_ref):
    c128, sl128, sh128 = c128_ref[...], sl128_ref[...], sh128_ref[...]
    c64, sl64, sh64 = c64_ref[...], sl64_ref[...], sh64_ref[...]
    half128 = HEAD_DIM // 8
    half64 = IDX_DIM // 8
    for h in range(N_HEADS):
        sl = slice(h * LANES, (h + 1) * LANES)
        qo_ref[:, sl] = _rope(_head_norm(q_ref[:, sl], qg_ref[...]), c128, sl128, sh128, half128).astype(BF16)
    for h in range(N_KV_HEADS):
        sl = slice(h * LANES, (h + 1) * LANES)
        kk = _rope(_head_norm(k_ref[:, sl], kg_ref[...]), c128, sl128, sh128, half128)
        kf_ref[:, sl] = kk
        kb_ref[:, sl] = kk.astype(BF16)
    v = v_ref[...]
    vf_ref[...] = v
    _store_v_aug(vb_ref, 0, [v[:, h * HEAD_DIM:(h + 1) * HEAD_DIM] for h in range(N_KV_HEADS)])
    lane = lax.broadcasted_iota(I32, c64.shape, 1)
    low = lane < IDX_DIM
    for p in range(IDX_HEADS // 2):
        x = _rope(qi_ref[:, p * LANES:(p + 1) * LANES], c64, sl64, sh64, half64)
        qio_ref[:, (2 * p) * LANES:(2 * p + 1) * LANES] = jnp.where(low, x, 0.0).astype(BF16)
        qio_ref[:, (2 * p + 1) * LANES:(2 * p + 2) * LANES] = jnp.where(low, pltpu.roll(x, IDX_DIM, 1), 0.0).astype(BF16)
    kw = kw_ref[...]
    ms = jnp.sum(jnp.where(low, kw * kw, 0.0), axis=-1, keepdims=True) * (1.0 / IDX_DIM)
    ki = _rope(kw * lax.rsqrt(ms + EPS) * ig_ref[...], c64, sl64, sh64, half64)
    kif_ref[...] = ki[:, :IDX_DIM]
    kib_ref[...] = jnp.where(low, ki, 0.0).astype(BF16)
    wo_ref[...] = (pltpu.roll(kw, IDX_DIM, 1) * IDX_HEADS ** -0.5) * IDX_DIM ** -0.5


def _rope_tables(pos, head_dim):
    r = head_dim // 4
    half = r // 2
    inv = ROPE_THETA ** (-jnp.arange(half, dtype=F32) * 2.0 / r)
    ang = pos.astype(F32)[:, None] * inv[None, :]
    cos, sin = jnp.cos(ang), jnp.sin(ang)
    n = pos.shape[0]
    zh = jnp.zeros((n, half), F32)
    rest = head_dim - r
    c = jnp.concatenate([cos, cos, jnp.ones((n, rest), F32)], axis=-1)
    s_lo = jnp.concatenate([-sin, zh, jnp.zeros((n, rest), F32)], axis=-1)
    s_hi = jnp.concatenate([zh, sin, jnp.zeros((n, rest), F32)], axis=-1)
    rep = LANES // head_dim
    return tuple(jnp.tile(t, (1, rep)) for t in (c, s_lo, s_hi))


QK_TM = 512


def qk_post(proj, table_pos, table_block, q_gain, k_gain, ik_gain):
    tm = QK_TM
    n_tok = proj.shape[0]
    t128 = _rope_tables(table_pos, HEAD_DIM)
    t64 = _rope_tables(table_pos, IDX_DIM)
    ik_gain128 = jnp.concatenate([ik_gain, jnp.zeros((LANES - IDX_DIM,), F32)])[None, :]

    def col(width, start):
        return pl.BlockSpec((tm, width), lambda i: (i, start // width))

    def row(width):
        return pl.BlockSpec((tm, width), lambda i: (i, 0))

    table = pl.BlockSpec((tm, LANES), lambda i: (table_block(i), 0))
    gain = pl.BlockSpec((1, LANES), lambda i: (0, 0))
    return pl.pallas_call(
        _qk_post_kernel,
        grid=(n_tok // tm,),
        in_specs=[col(ATTN_WIDTH, COL_Q), col(KV_WIDTH, COL_K), col(KV_WIDTH, COL_V), col(IDX_HEADS * IDX_DIM, COL_QI),
                  col(LANES, COL_KIWI)] + [table] * 6 + [gain] * 3,
        out_specs=[row(ATTN_WIDTH), row(KV_WIDTH), row(KV_WIDTH), row(KV_WIDTH), row(N_KV_HEADS * V_AUG), row(IDX_HEADS * LANES),
                   row(IDX_DIM), row(LANES), row(LANES)],
        out_shape=[jax.ShapeDtypeStruct((n_tok, ATTN_WIDTH), BF16),
                   jax.ShapeDtypeStruct((n_tok, KV_WIDTH), F32), jax.ShapeDtypeStruct((n_tok, KV_WIDTH), BF16),
                   jax.ShapeDtypeStruct((n_tok, KV_WIDTH), F32), jax.ShapeDtypeStruct((n_tok, N_KV_HEADS * V_AUG), BF16),
                   jax.ShapeDtypeStruct((n_tok, IDX_HEADS * LANES), BF16),
                   jax.ShapeDtypeStruct((n_tok, IDX_DIM), F32), jax.ShapeDtypeStruct((n_tok, LANES), BF16),
                   jax.ShapeDtypeStruct((n_tok, LANES), F32)],
        compiler_params=_params(("arbitrary",)),
        name="qk_post",
    )(proj, proj, proj, proj, proj, *t128, *t64, q_gain[None, :], k_gain[None, :], ik_gain128)


def _gelu_tanh(x):
    return 0.5 * x * (1.0 + jnp.tanh(np.float32(np.sqrt(2.0 / np.pi)) * (x + 0.044715 * (x * x * x))))


SSM_LT = SSM_SB // LANES
SSM_SEG = 64


def _ssm_kernel(u_ref, wb_ref, wc_ref, pw_ref, d_ref, h0_ref, g_ref, sre_ref, sim_ref,
                er_ref, ei_ref, car_ref, up_ref, yp_ref):
    c = pl.program_id(2)

    @pl.when(c == 0)
    def _():
        car_ref[...] = h0_ref[...]

    for j in range(SSM_SEG):
        up_ref[j * SUBLANES:(j + 1) * SUBLANES, :] = u_ref[pl.ds(j, SUBLANES, stride=SSM_SEG), :]
    e = _dot(up_ref[...].astype(BF16), wb_ref[...])
    tiles = [slice(lt * LANES, (lt + 1) * LANES) for lt in range(SSM_LT)]
    for lt, sl in enumerate(tiles):
        er_ref[lt] = e[:, sl]
        ei_ref[lt] = e[:, SSM_SB + lt * LANES:SSM_SB + (lt + 1) * LANES]

    def cmul_add(ar, ai, br, bi, cr, ci):
        return ar * br - ai * bi + cr, ar * bi + ai * br + ci

    lb = [(pw_ref[0, 0:1, sl], pw_ref[1, 0:1, sl]) for sl in tiles]
    zero = jnp.zeros((SUBLANES, LANES), F32)
    st = [(zero, zero)] * SSM_LT
    for j in range(SSM_SEG):
        rows = slice(j * SUBLANES, (j + 1) * SUBLANES)
        for lt in range(SSM_LT):
            st[lt] = cmul_add(*lb[lt], *st[lt], er_ref[lt, rows, :], ei_ref[lt, rows, :])
            er_ref[lt, rows, :] = st[lt][0]
            ei_ref[lt, rows, :] = st[lt][1]

    enter = []
    for lt, sl in enumerate(tiles):
        seg_r, seg_i = pw_ref[0, SSM_SEG - 1:SSM_SEG, sl], pw_ref[1, SSM_SEG - 1:SSM_SEG, sl]
        cr, ci = car_ref[0:1, sl], car_ref[1:2, sl]
        rows_r, rows_i = [], []
        for r in range(SUBLANES):
            rows_r.append(cr)
            rows_i.append(ci)
            cr, ci = cmul_add(seg_r, seg_i, cr, ci, st[lt][0][r:r + 1], st[lt][1][r:r + 1])
        car_ref[0:1, sl] = cr
        car_ref[1:2, sl] = ci
        enter.append((jnp.concatenate(rows_r, axis=0), jnp.concatenate(rows_i, axis=0)))

    for j in range(SSM_SEG):
        rows = slice(j * SUBLANES, (j + 1) * SUBLANES)
        for lt, sl in enumerate(tiles):
            xr, xi = cmul_add(pw_ref[0, j:j + 1, sl], pw_ref[1, j:j + 1, sl], *enter[lt],
                              er_ref[lt, rows, :], ei_ref[lt, rows, :])
            er_ref[lt, rows, :] = xr
            ei_ref[lt, rows, :] = xi

    y = None
    for lt, sl in enumerate(tiles):
        t = _dot(er_ref[lt].astype(BF16), wc_ref[0, sl, :]) - _dot(ei_ref[lt].astype(BF16), wc_ref[1, sl, :])
        y = t if y is None else y + t
    yp_ref[...] = y
    out_rows = 2 * SUBLANES
    for t0 in range(0, SUBLANES * SSM_SEG, out_rows):
        r, j0 = divmod(t0, SSM_SEG)
        rows = slice(t0, t0 + out_rows)
        yt = yp_ref[pl.ds(j0 * SUBLANES + r, out_rows, stride=SUBLANES), :] + d_ref[...] * u_ref[rows, :]
        g_ref[rows, :] = _gelu_tanh(yt).astype(BF16)

    @pl.when(c == pl.num_programs(2) - 1)
    def _():
        sre_ref[...] = car_ref[0:1, :]
        sim_ref[...] = car_ref[1:2, :]


def _ssm_weights(a_re, a_im, log_dt, b_re, b_im, c_re, c_im):
    lam_re, lam_im = a_re, a_im
    dt = jnp.exp(log_dt)[:, None]
    mag = jnp.exp(lam_re * dt)
    lb_re, lb_im = mag * jnp.cos(lam_im * dt), mag * jnp.sin(lam_im * dt)
    den = lam_re * lam_re + lam_im * lam_im
    num_re = lb_re - 1.0
    z_re = (num_re * lam_re + lb_im * lam_im) / den
    z_im = (lb_im * lam_re - num_re * lam_im) / den
    zb_re = z_re[:, :, None] * b_re - z_im[:, :, None] * b_im
    zb_im = z_re[:, :, None] * b_im + z_im[:, :, None] * b_re
    eye = jnp.eye(8, dtype=F32)

    def blockdiag_in(w):
        return jnp.einsum('jgph,gk->jghkp', w.reshape(SSM_LB, 8, SSM_STATE, SSM_GROUP), eye).reshape(SSM_LB, LANES, SSM_SB)

    def blockdiag_out(w):
        return jnp.einsum('jghp,gk->jkpgh', w.reshape(SSM_LB, 8, SSM_GROUP, SSM_STATE), eye).reshape(SSM_LB, SSM_SB, LANES)

    wb = jnp.concatenate([blockdiag_in(zb_re), blockdiag_in(zb_im)], axis=-1).astype(BF16)
    wc = jnp.stack([blockdiag_out(c_re), blockdiag_out(c_im)], axis=1).astype(BF16)

    pr, pi_ = lb_re.reshape(SSM_LB, 1, SSM_SB), lb_im.reshape(SSM_LB, 1, SSM_SB)
    while pr.shape[1] < SSM_SEG:
        tr, ti = pr[:, -1:], pi_[:, -1:]
        pr, pi_ = (jnp.concatenate([pr, pr * tr - pi_ * ti], axis=1), jnp.concatenate([pi_, pr * ti + pi_ * tr], axis=1))
    pw = jnp.stack([pr, pi_], axis=1)
    return wb, wc, pw


def ssm(proj, ssm_w, d_skip, h0, *, n_batch, seq, row0):
    wb, wc, pw = ssm_w
    tc = SUBLANES * SSM_SEG
    n_chunks = seq // tc
    blk0 = row0 // tc
    n_tok = n_batch * seq
    state_shape = jax.ShapeDtypeStruct((n_batch, SSM_LB, 1, SSM_SB), F32)
    state_spec = pl.BlockSpec((None, None, 1, SSM_SB), lambda b, j, c: (b, j, 0, 0))
    g, s_re, s_im = pl.pallas_call(
        _ssm_kernel,
        grid=(n_batch, SSM_LB, n_chunks),
        in_specs=[pl.BlockSpec((tc, LANES), lambda b, j, c: (blk0 + b * n_chunks + c, j)),
                  pl.BlockSpec((None, LANES, 2 * SSM_SB), lambda b, j, c: (j, 0, 0)),
                  pl.BlockSpec((None, 2, SSM_SB, LANES), lambda b, j, c: (j, 0, 0, 0)),
                  pl.BlockSpec((None, 2, SSM_SEG, SSM_SB), lambda b, j, c: (j, 0, 0, 0)),
                  pl.BlockSpec((1, LANES), lambda b, j, c: (0, j)),
                  pl.BlockSpec((None, None, 2, SSM_SB), lambda b, j, c: (b, j, 0, 0))],
        out_specs=[pl.BlockSpec((tc, LANES), lambda b, j, c: (b * n_chunks + c, j)), state_spec, state_spec],
        out_shape=[jax.ShapeDtypeStruct((n_tok, SSM_WIDTH), BF16), state_shape, state_shape],
        scratch_shapes=[pltpu.VMEM((SSM_LT, tc, LANES), F32), pltpu.VMEM((SSM_LT, tc, LANES), F32),
                        pltpu.VMEM((2, SSM_SB), F32), pltpu.VMEM((tc, LANES), F32), pltpu.VMEM((tc, LANES), F32)],
        compiler_params=_params(("arbitrary", "arbitrary", "arbitrary")),
        name="ssm",
    )(proj, wb, wc, pw, d_skip[None, :], h0)
    return g, s_re.reshape(n_batch, SSM_GROUPS, SSM_STATE), s_im.reshape(n_batch, SSM_GROUPS, SSM_STATE)


def _ssm_step_kernel(u_ref, wb_ref, wc_ref, pw_ref, d_ref, h0_ref, g_ref, sre_ref, sim_ref, er_ref, ei_ref, *, seq):
    n_seq = h0_ref.shape[1]
    u = u_ref[...]
    e = _dot(u.astype(BF16), wb_ref[...])
    n_lt = SSM_SB // LANES
    y = d_ref[...] * u
    for lt in range(n_lt):
        sl = slice(lt * LANES, (lt + 1) * LANES)
        er_ref[...] = e[:, lt * LANES:(lt + 1) * LANES]
        ei_ref[...] = e[:, SSM_SB + lt * LANES:SSM_SB + (lt + 1) * LANES]
        lr, li = pw_ref[0, 0:1, sl], pw_ref[1, 0:1, sl]
        sr, si = h0_ref[0, :, sl], h0_ref[1, :, sl]
        for t in range(seq):
            rows = pl.ds(t, n_seq, stride=seq)
            sr, si = lr * sr - li * si + er_ref[rows, :], lr * si + li * sr + ei_ref[rows, :]
            er_ref[rows, :] = sr
            ei_ref[rows, :] = si
        y = y + (_dot(er_ref[...].astype(BF16), wc_ref[0, sl, :]) - _dot(ei_ref[...].astype(BF16), wc_ref[1, sl, :]))
        sre_ref[:, sl] = sr
        sim_ref[:, sl] = si
    g_ref[...] = _gelu_tanh(y).astype(BF16)


def ssm_step(proj, ssm_w, d_skip, h0, *, n_batch, seq, row0):
    wb, wc, pw = ssm_w
    n_tok = n_batch * seq
    assert row0 % n_tok == 0
    state_shape = jax.ShapeDtypeStruct((SSM_LB, n_batch, SSM_SB), F32)
    state_spec = pl.BlockSpec((None, n_batch, SSM_SB), lambda j: (j, 0, 0))
    g, s_re, s_im = pl.pallas_call(
        functools.partial(_ssm_step_kernel, seq=seq),
        grid=(SSM_LB,),
        in_specs=[pl.BlockSpec((n_tok, LANES), lambda j: (row0 // n_tok, j)),
                  pl.BlockSpec((None, LANES, 2 * SSM_SB), lambda j: (j, 0, 0)),
                  pl.BlockSpec((None, 2, SSM_SB, LANES), lambda j: (j, 0, 0, 0)),
                  pl.BlockSpec((None, 2, SSM_SEG, SSM_SB), lambda j: (j, 0, 0, 0)),
                  pl.BlockSpec((1, LANES), lambda j: (0, j)),
                  pl.BlockSpec((None, 2, n_batch, SSM_SB), lambda j: (j, 0, 0, 0))],
        out_specs=[pl.BlockSpec((n_tok, LANES), lambda j: (0, j)), state_spec, state_spec],
        out_shape=[jax.ShapeDtypeStruct((n_tok, SSM_WIDTH), BF16), state_shape, state_shape],
        scratch_shapes=[pltpu.VMEM((n_tok, LANES), F32), pltpu.VMEM((n_tok, LANES), F32)],
        compiler_params=_params(("arbitrary",)),
        name="ssm_step",
    )(proj, wb, wc, pw, d_skip[None, :], h0)

    def per_seq(s):
        return s.transpose(1, 0, 2).reshape(n_batch, SSM_GROUPS, SSM_STATE)

    return g, per_seq(s_re), per_seq(s_im)


def _row_sum(x):
    return jnp.sum(x, axis=1, keepdims=True)


def _row_count(mask):
    return _row_sum(jnp.where(mask, 1, 0))


I16 = jnp.int16
I16_MIN = -2 ** 15


def _count16(ref, cand, compare):
    accs = [None] * 4
    for t in range(ref.shape[1] // LANES):
        x = jnp.where(compare(ref[:, t * LANES:(t + 1) * LANES], cand), I16(1), I16(0))
        accs[t % 4] = x if accs[t % 4] is None else accs[t % 4] + x
    accs = [a for a in accs if a is not None]
    total = accs[0]
    for a in accs[1:]:
        total = total + a
    return _row_sum(total.astype(I32))


def _bisect16(ref, target):
    def step(i, base):
        cand = base + lax.shift_left(np.int32(1), np.int32(15) - i)
        cnt = _count16(ref, cand.astype(I16), lambda a, b: a >= b)
        return jnp.where(cnt >= target, cand, base)
    return lax.fori_loop(0, 16, step, jnp.full((ref.shape[0], 1), I16_MIN, I32))


def _bisect32(key_ref, n_sel):
    bq, n_keys = key_ref.shape
    hr = bq // 2

    def lane_counts(h, cand):
        accs = [None] * 4
        for t in range(n_keys // LANES):
            x = jnp.where(key_ref[h * hr:(h + 1) * hr, t * LANES:(t + 1) * LANES] >= cand, 1, 0)
            accs[t % 4] = x if accs[t % 4] is None else accs[t % 4] + x
        accs = [a for a in accs if a is not None]
        total = accs[0]
        for a in accs[1:]:
            total = total + a
        return total

    def decide(part, cand, base):
        return jnp.where(_row_sum(part) >= n_sel, cand, base)

    def bit(i):
        return lax.shift_left(np.int32(1), np.int32(31) - i)

    def body(i, state):
        base_a, base_b, part_b = state
        cand_a = base_a + bit(i)
        part_a = lane_counts(0, cand_a)
        base_b = decide(part_b, base_b + bit(i - 1), base_b)
        part_b = lane_counts(1, base_b + bit(i))
        return decide(part_a, cand_a, base_a), base_b, part_b

    base0 = jnp.full((hr, 1), INT_MIN, I32)
    first = base0 + bit(0)
    state = (decide(lane_counts(0, first), first, base0), base0, lane_counts(1, first))
    base_a, base_b, part_b = lax.fori_loop(1, 32, body, state)
    base_b = decide(part_b, base_b + bit(31), base_b)
    return jnp.concatenate([base_a, base_b], axis=0)


def _stack_heads(ref, heads):
    return jnp.concatenate([ref[:, h * LANES:(h + 1) * LANES] for h in heads], axis=0)


def _dsa_body(q_ref, qi_ref, wi_ref, k_ref, v_ref, ki_ref, o_ref, key_ref, bias_ref, hi_ref, lo_ref, p_ref,
              *, q_pos_first, s_valid, n_sel, packed_bisect, stack):
    bq, n_keys = key_ref.shape
    col = lax.broadcasted_iota(I32, (bq, n_keys), 1)
    qpos = q_pos_first + lax.broadcasted_iota(I32, (bq, 1), 0)
    allowed = col < jnp.minimum((qpos // CHUNK + 1) * CHUNK, s_valid)

    ki = ki_ref[...]
    score = None
    for h0 in range(0, IDX_HEADS, stack):
        d = _dot_nt(_stack_heads(qi_ref, range(h0, h0 + stack)), ki)
        for j in range(stack):
            t = jnp.maximum(d[j * bq:(j + 1) * bq], 0.0) * wi_ref[:, h0 + j:h0 + j + 1]
            score = t if score is None else score + t
    score = jnp.where(score == 0.0, 0.0, score)
    bits = pltpu.bitcast(score, I32)
    key = jnp.where(bits < 0, bits ^ np.int32(0x7FFFFFFF), bits)
    key = jnp.where(allowed, key, KEY_NEG_INF)
    key_ref[...] = key

    if packed_bisect:
        hi_ref[...] = (key >> 16).astype(I16)
        lo_ref[...] = ((key & 0xFFFF) + I16_MIN).astype(I16)
        thr_hi = _bisect16(hi_ref, n_sel)
        thr_hi16 = thr_hi.astype(I16)
        need_lo = n_sel - _count16(hi_ref, thr_hi16, lambda a, b: a > b)
        lo_ref[...] = jnp.where(hi_ref[...] == thr_hi16, lo_ref[...], I16(I16_MIN))
        thr_lo = _bisect16(lo_ref, need_lo)
        thr = lax.shift_left(thr_hi, np.int32(16)) + (thr_lo - I16_MIN)
    else:
        thr = _bisect32(key_ref, n_sel)
    thr = jnp.maximum(thr, KEY_NEG_INF)

    key = key_ref[...]
    need = n_sel - _row_count(key > thr)
    n_eq = _row_count(key == thr)
    n_bits = int(n_keys - 1).bit_length()

    def tie_cut():
        def step(i, j0):
            cand = j0 + lax.shift_left(np.int32(1), np.int32(n_bits - 1) - i)
            cnt = _row_sum(jnp.where(key_ref[...] == thr, jnp.where(col < cand, 1, 0), 0))
            return jnp.where(cnt < need, cand, j0)
        return lax.fori_loop(0, n_bits, step, jnp.zeros((bq, 1), I32))

    split = jnp.max(jnp.where(n_eq > need, 1, 0)) > 0
    j_last = lax.cond(split, tie_cut, lambda: jnp.full((bq, 1), n_keys, I32))
    tie_bias = jnp.where(thr == KEY_NEG_INF, -jnp.inf, 0.0)
    bias_ref[...] = jnp.where(key > thr, 0.0,
                              jnp.where(key == thr, jnp.where(col <= j_last, tie_bias, -jnp.inf), -jnp.inf))

    c = np.float32(HEAD_DIM ** -0.5 * np.log2(np.e))
    for h0 in range(0, N_HEADS, stack):
        kv = h0 // KV_GROUP
        heads = range(h0, h0 + stack)
        s_all = _dot_nt(_stack_heads(q_ref, heads), k_ref[:, kv * HEAD_DIM:(kv + 1) * HEAD_DIM])
        for g in range(stack):
            s = s_all[g * bq:(g + 1) * bq] + bias_ref[...]
            m = jnp.max(s, axis=1, keepdims=True)
            p_ref[g * bq:(g + 1) * bq, :] = jnp.exp2((s - m) * c).astype(BF16)
        pv = _dot(p_ref[0:stack * bq, :], v_ref[:, kv * V_AUG:(kv + 1) * V_AUG])
        for g, h in enumerate(heads):
            o = pv[g * bq:(g + 1) * bq]
            o_ref[:, h * HEAD_DIM:(h + 1) * HEAD_DIM] = (o[:, :HEAD_DIM] / o[:, HEAD_DIM:HEAD_DIM + 1]).astype(BF16)


def _dsa_scratch(bq, n_keys):
    return [pltpu.VMEM((bq, n_keys), I32), pltpu.VMEM((bq, n_keys), F32),
            pltpu.VMEM((bq, n_keys), I16), pltpu.VMEM((bq, n_keys), I16), pltpu.VMEM((KV_GROUP * bq, n_keys), BF16)]


def _dsa_kernel(q_ref, qi_ref, wi_ref, k_ref, v_ref, ki_ref, o_ref, *scratch, q_pos0, **static):
    bq = scratch[0].shape[0]
    _dsa_body(q_ref, qi_ref, wi_ref, k_ref, v_ref, ki_ref, o_ref, *scratch,
              q_pos_first=q_pos0 + pl.program_id(1) * bq, **static)


def dsa(q, qi, wi, k, v, ki, *, bq, q_blk0, n_qblk, n_keys, n_sel, packed_bisect, stack):
    n_batch, seq = q.shape[:2]

    def qspec(width):
        return pl.BlockSpec((None, bq, width), lambda b, i: (b, q_blk0 + i, 0))

    def kspec(width):
        return pl.BlockSpec((None, n_keys, width), lambda b, i: (b, 0, 0))

    return pl.pallas_call(
        functools.partial(_dsa_kernel, q_pos0=q_blk0 * bq, s_valid=seq, n_sel=n_sel, packed_bisect=packed_bisect,
                          stack=stack),
        grid=(n_batch, n_qblk),
        in_specs=[qspec(ATTN_WIDTH), qspec(IDX_HEADS * LANES), qspec(LANES), kspec(KV_WIDTH), kspec(N_KV_HEADS * V_AUG),
                  kspec(LANES)],
        out_specs=pl.BlockSpec((None, bq, ATTN_WIDTH), lambda b, i: (b, i, 0)),
        out_shape=jax.ShapeDtypeStruct((n_batch, n_qblk * bq, ATTN_WIDTH), BF16),
        scratch_shapes=_dsa_scratch(bq, n_keys),
        compiler_params=_params(("arbitrary", "arbitrary")),
        name="dsa",
    )(q, qi, wi, k, v, ki)


def _dsa_step_kernel(q_ref, qi_ref, wi_ref, ck_hbm, cv_hbm, cki_ref, nk_ref, nv_ref, nki_ref, o_ref,
                     k_buf, v_buf, ki_buf, cache_buf, sem, *scratch, past, n_sel):
    b = pl.program_id(0)

    def cache_copies(seq, slot):
        return [pltpu.make_async_copy(src.at[seq, :, h, :], cache_buf.at[slot, a, h], sem.at[slot])
                for a, src in enumerate((ck_hbm, cv_hbm)) for h in range(N_KV_HEADS)]

    @pl.when(b == 0)
    def _():
        for cp in cache_copies(0, 0):
            cp.start()

    @pl.when(b + 1 < pl.num_programs(0))
    def _():
        for cp in cache_copies(b + 1, (b + 1) % 2):
            cp.start()

    slot = b % 2
    for cp in cache_copies(b, slot):
        cp.wait()

    ts = nk_ref.shape[0]
    n_keys = k_buf.shape[0]
    for h in range(N_KV_HEADS):
        k_buf[0:past, h * HEAD_DIM:(h + 1) * HEAD_DIM] = cache_buf[slot, 0, h].astype(BF16)
    _store_v_aug(v_buf, 0, [cache_buf[slot, 1, h] for h in range(N_KV_HEADS)])
    for buf, new in ((k_buf, nk_ref), (v_buf, nv_ref)):
        buf[past:past + ts, :] = new[...]
        buf[past + ts:n_keys, :] = jnp.zeros((n_keys - past - ts, buf.shape[1]), BF16)
    ki_buf[0:past, 0:IDX_DIM] = cki_ref[...].astype(BF16)
    ki_buf[0:past, IDX_DIM:LANES] = jnp.zeros((past, LANES - IDX_DIM), BF16)
    ki_buf[past:past + ts, :] = nki_ref[...]
    ki_buf[past + ts:n_keys, :] = jnp.zeros((n_keys - past - ts, LANES), BF16)
    _dsa_body(q_ref, qi_ref, wi_ref, k_buf, v_buf, ki_buf, o_ref, *scratch,
              q_pos_first=past, s_valid=past + ts, n_sel=n_sel, packed_bisect=True, stack=KV_GROUP)


def dsa_step(q, qi, wi, cache_k, cache_v, cache_ki, k_new, v_new, ki_new, *, n_sel):
    n_batch, ts = q.shape[:2]
    past = cache_k.shape[1]
    n_keys = -(-(past + ts) // LANES) * LANES

    def spec(rows, width):
        return pl.BlockSpec((None, rows, width), lambda b: (b, 0, 0))

    return pl.pallas_call(
        functools.partial(_dsa_step_kernel, past=past, n_sel=n_sel),
        grid=(n_batch,),
        in_specs=[spec(ts, ATTN_WIDTH), spec(ts, IDX_HEADS * LANES), spec(ts, LANES),
                  pl.BlockSpec(memory_space=pl.ANY), pl.BlockSpec(memory_space=pl.ANY), spec(past, IDX_DIM),
                  spec(ts, KV_WIDTH), spec(ts, N_KV_HEADS * V_AUG), spec(ts, LANES)],
        out_specs=spec(ts, ATTN_WIDTH),
        out_shape=jax.ShapeDtypeStruct((n_batch, ts, ATTN_WIDTH), BF16),
        scratch_shapes=[pltpu.VMEM((n_keys, KV_WIDTH), BF16), pltpu.VMEM((n_keys, N_KV_HEADS * V_AUG), BF16),
                        pltpu.VMEM((n_keys, LANES), BF16),
                        pltpu.VMEM((2, 2, N_KV_HEADS, past, HEAD_DIM), F32), pltpu.SemaphoreType.DMA((2,)),
                        *_dsa_scratch(ts, n_keys)],
        compiler_params=_params(("arbitrary",)),
        name="dsa_step",
    )(q, qi, wi, cache_k, cache_v, cache_ki, k_new, v_new, ki_new)


def _merge_kernel(g_ref, a_ref, ga_ref, gb_ref, wv_ref, wg_ref, wb_ref, o_ref):
    g = g_ref[...]
    branch_a = _dot(g, wv_ref[...]) * jax.nn.sigmoid(_dot(g, wg_ref[...]))
    branch_b = _dot(a_ref[...], wb_ref[...])
    merged = jax.nn.sigmoid(ga_ref[...]) * branch_a + jax.nn.sigmoid(gb_ref[...]) * branch_b
    o_ref[...] = merged.astype(BF16)


def merge(g, attn, proj, w_val, w_gate, w_branch, *, tm=1024, tn=512):
    n_tok = g.shape[0]
    nj = D_MODEL // tn

    def wspec():
        return pl.BlockSpec((SSM_WIDTH, tn), lambda i, j: (0, j))

    return pl.pallas_call(
        _merge_kernel,
        grid=(n_tok // tm, nj),
        in_specs=[pl.BlockSpec((tm, SSM_WIDTH), lambda i, j: (i, 0)),
                  pl.BlockSpec((tm, ATTN_WIDTH), lambda i, j: (i, 0)),
                  pl.BlockSpec((tm, tn), lambda i, j: (i, COL_GA // tn + j)),
                  pl.BlockSpec((tm, tn), lambda i, j: (i, COL_GB // tn + j)),
                  wspec(), wspec(), wspec()],
        out_specs=pl.BlockSpec((tm, tn), lambda i, j: (i, j)),
        out_shape=jax.ShapeDtypeStruct((n_tok, D_MODEL), BF16),
        compiler_params=_params(("arbitrary", "arbitrary")),
        name="merge",
    )(g, attn, proj, proj, w_val, w_gate, w_branch)


ROUTER_COLS = N_EXPERT_GROUPS + N_EXPERTS
MOE_TM = 256


def _first_lane_of_max(x, lane_f):
    m = jnp.max(x, axis=1, keepdims=True)
    return m, jnp.min(jnp.where(x == m, lane_f, float(LANES)), axis=1, keepdims=True)


def _out_proj_kernel(x_ref, m_ref, wo_ref, gn_ref, wr_ref, br_ref, cin_ref,
                     h_ref, hn_ref, ri_ref, rw_ref, cnt_ref, carry_ref):
    @pl.when(pl.program_id(0) == 0)
    def _():
        carry_ref[...] = cin_ref[...]

    h = x_ref[...] + _dot(m_ref[...], wo_ref[...])
    h_ref[...] = h
    ms = jnp.mean(h * h, axis=-1, keepdims=True)
    hn = h * lax.rsqrt(ms + EPS) * gn_ref[...]
    hn_ref[...] = hn
    hh, hl = _split_bf16(hn)
    both = _dot(hh, wr_ref[...])
    lg = both[:, :LANES] + _dot(hl, wr_ref[:, 0:LANES]) + both[:, LANES:] + br_ref[...]

    tm = lg.shape[0]
    lane = lax.broadcasted_iota(I32, lg.shape, 1)
    lane_f = lane.astype(F32)
    ninf = -jnp.inf
    gl = jnp.where(lane < N_EXPERT_GROUPS, lg, ninf)
    gmax, gsel = _first_lane_of_max(gl, lane_f)
    g_w = 1.0 / jnp.sum(jnp.exp(gl - gmax), axis=1, keepdims=True)
    lo = N_EXPERT_GROUPS + EXPERTS_PER_GROUP * gsel
    el = jnp.where(lane_f >= lo, jnp.where(lane_f < lo + EXPERTS_PER_GROUP, lg, ninf), ninf)
    v1, i1 = _first_lane_of_max(el, lane_f)
    el2 = jnp.where(lane_f == i1, ninf, el)
    v2, i2 = _first_lane_of_max(el2, lane_f)
    t = jnp.exp(v2 - v1)
    s1 = 1.0 / (1.0 + t)
    w1 = s1 * g_w
    w2 = (t * s1) * g_w

    m1 = jnp.where(lane_f == i1, 1.0, 0.0)
    m2 = jnp.where(lane_f == i2, 1.0, 0.0)
    both = m1 + m2
    tri = jnp.where(lax.broadcasted_iota(I32, (tm, tm), 0) > lax.broadcasted_iota(I32, (tm, tm), 1), 1.0, 0.0)
    before = _dot(tri.astype(BF16), both.astype(BF16)) + carry_ref[...]
    r1 = jnp.sum(before * m1, axis=1, keepdims=True)
    r2 = jnp.sum(before * m2, axis=1, keepdims=True)
    carry_ref[...] = carry_ref[...] + jnp.sum(both, axis=0, keepdims=True)
    cnt_ref[...] = carry_ref[...]
    e1 = i1 - float(N_EXPERT_GROUPS)
    e2 = i2 - float(N_EXPERT_GROUPS)
    fields = jnp.where(lane == 0, e1, jnp.where(lane == 1, e2, jnp.where(lane == 2, r1, jnp.where(lane == 3, r2, 0.0))))
    ri_ref[...] = fields.T[0:SUBLANES, :].astype(I32)
    rw_ref[...] = jnp.where(lane == 0, w1, jnp.where(lane == 1, w2, 0.0))


def _router_weights(w_router_group, b_router_group, w_router_expert, b_router_expert):
    wr = jnp.concatenate([w_router_group, w_router_expert, jnp.zeros((D_MODEL, LANES - ROUTER_COLS), F32)], axis=1)
    wr_hi = wr.astype(BF16)
    wr_lo = (wr - wr_hi.astype(F32)).astype(BF16)
    br = jnp.concatenate([b_router_group, b_router_expert, jnp.zeros((LANES - ROUTER_COLS,), F32)])[None, :]
    return jnp.concatenate([wr_hi, wr_lo], axis=1), br


def out_proj(x, merged, w_out, ffn_gain, router_w, counts_in, *, tm=256):
    n_tok = x.shape[0]
    wr, br = router_w

    def row(width):
        return pl.BlockSpec((tm, width), lambda i: (i, 0))

    def const(shape):
        return pl.BlockSpec(shape, lambda i: (0, 0), pipeline_mode=pl.Buffered(1))

    return pl.pallas_call(
        _out_proj_kernel,
        grid=(n_tok // tm,),
        in_specs=[row(D_MODEL), row(D_MODEL), const((D_MODEL, D_MODEL)), const((1, D_MODEL)),
                  const((D_MODEL, 2 * LANES)), const((1, LANES)), const((1, LANES))],
        out_specs=[row(D_MODEL), row(D_MODEL), pl.BlockSpec((SUBLANES, tm), lambda i: (0, i)), row(LANES),
                   pl.BlockSpec((1, LANES), lambda i: (0, 0))],
        out_shape=[jax.ShapeDtypeStruct((n_tok, D_MODEL), F32), jax.ShapeDtypeStruct((n_tok, D_MODEL), F32),
                   jax.ShapeDtypeStruct((SUBLANES, n_tok), I32), jax.ShapeDtypeStruct((n_tok, LANES), F32),
                   jax.ShapeDtypeStruct((1, LANES), F32)],
        scratch_shapes=[pltpu.VMEM((1, LANES), F32)],
        compiler_params=_params(("arbitrary",)),
        name="out_proj",
    )(x, merged, w_out, ffn_gain[None, :], wr, br, counts_in)


def _block_layout(counts):
    padded = (counts + MOE_TM - 1) // MOE_TM * MOE_TM
    pad_end = jnp.cumsum(padded).astype(I32)
    pad_start = pad_end - padded
    n_used = pad_end[-1] // MOE_TM
    return pad_start, pad_end, n_used


def _dest_kernel(ps_ref, ri_ref, o_ref):
    ri = ri_ref[...]
    start = jnp.zeros_like(ri)
    for k in range(N_EXPERTS):
        start = jnp.where(ri == k, ps_ref[k], start)
    o_ref[...] = start + pltpu.roll(ri, SUBLANES - TOP_K, 0)


def dest_rows(route_i, pad_start):
    grid_spec = pltpu.PrefetchScalarGridSpec(
        num_scalar_prefetch=1, grid=(1,),
        in_specs=[pl.BlockSpec(route_i.shape, lambda i, ps: (0, 0))],
        out_specs=pl.BlockSpec(route_i.shape, lambda i, ps: (0, 0)))
    return pl.pallas_call(_dest_kernel, grid_spec=grid_spec, out_shape=jax.ShapeDtypeStruct(route_i.shape, I32),
                          compiler_params=_params(("arbitrary",)), name="dest_rows")(pad_start, route_i)


def _moe_rows(n_tok):
    return -(-(n_tok * TOP_K + N_EXPERTS * (MOE_TM - 1)) // MOE_TM) * MOE_TM


DISPATCH_TM = 512


def _wait_rows(src_hbm, dst, sem, n_rows):
    pltpu.make_async_copy(src_hbm.at[pl.ds(0, n_rows)], dst, sem).wait()


def _dispatch_kernel(d0_ref, d1_ref, pe_ref, cnt_ref, nu_ref, hna_ref, hnb_ref, xs_hbm, zbuf, sem, semz,
                     *, n_blocks, a_tiles):
    i = pl.program_id(0)

    def zero_block(row0):
        return pltpu.make_async_copy(zbuf, xs_hbm.at[pl.ds(pl.multiple_of(row0, MOE_TM), MOE_TM)], semz)

    @pl.when(i == 0)
    def _():
        zbuf[...] = jnp.zeros_like(zbuf)
        for start in (True, False):
            for e in range(N_EXPERTS):
                @pl.when(cnt_ref[e] > 0)
                def _():
                    cp = zero_block(pe_ref[e] - MOE_TM)
                    cp.start() if start else cp.wait()

            def tail(b, c):
                cp = zero_block(b * MOE_TM)
                cp.start() if start else cp.wait()
                return c
            lax.fori_loop(nu_ref[0], n_blocks, tail, 0)

    base = i * DISPATCH_TM

    def scatter(hn_ref):
        def body(r, c):
            src = hn_ref.at[pl.ds(r, 1)]
            pltpu.make_async_copy(src, xs_hbm.at[pl.ds(d0_ref[base + r], 1)], sem).start()
            pltpu.make_async_copy(src, xs_hbm.at[pl.ds(d1_ref[base + r], 1)], sem).start()
            return c
        lax.fori_loop(0, DISPATCH_TM, body, 0, unroll=8)
        for _ in range(TOP_K):
            pltpu.make_async_copy(hn_ref, xs_hbm.at[pl.ds(0, DISPATCH_TM)], sem).wait()

    @pl.when(i < a_tiles)
    def _():
        scatter(hna_ref)

    @pl.when(i >= a_tiles)
    def _():
        scatter(hnb_ref)


def dispatch(hn_a, hn_b, dest0, dest1, pad_end, counts, n_used):
    a_tiles, b_tiles = hn_a.shape[0] // DISPATCH_TM, hn_b.shape[0] // DISPATCH_TM
    rows = _moe_rows(hn_a.shape[0] + hn_b.shape[0])
    grid_spec = pltpu.PrefetchScalarGridSpec(
        num_scalar_prefetch=5,
        grid=(a_tiles + b_tiles,),
        in_specs=[pl.BlockSpec((DISPATCH_TM, D_MODEL), lambda i, *_: (jnp.minimum(i, a_tiles - 1), 0)),
                  pl.BlockSpec((DISPATCH_TM, D_MODEL), lambda i, *_: (jnp.maximum(i - a_tiles, 0), 0))],
        out_specs=pl.BlockSpec(memory_space=pl.ANY),
        scratch_shapes=[pltpu.VMEM((MOE_TM, D_MODEL), F32), pltpu.SemaphoreType.DMA(()), pltpu.SemaphoreType.DMA(())],
    )
    return pl.pallas_call(
        functools.partial(_dispatch_kernel, n_blocks=rows // MOE_TM, a_tiles=a_tiles),
        grid_spec=grid_spec,
        out_shape=jax.ShapeDtypeStruct((rows, D_MODEL), F32),
        compiler_params=_params(("arbitrary",)),
        name="dispatch",
    )(dest0, dest1, pad_end, counts, n_used, hn_a, hn_b)


MOE_UNITS = 8
MOE_UG = D_MODEL // MOE_UNITS
MOE_UD = EXPERT_FF // MOE_UNITS


def _moe_kernel(blk_e_ref, nu_ref, nxt_ref, upb_ref, xs_ref, wg_hbm, wu_hbm, wd_hbm, ys_ref,
                wg_bf, wu_bf, wd_bf, stg_g, stg_u, stg_d, sem, st_ref):
    i = pl.program_id(0)
    cur_slot, pos, cur_e = 0, 1, 2

    def unit_copies(e, unit, s):
        g_rows = pl.ds(pl.multiple_of(unit * MOE_UG, MOE_UG), MOE_UG)
        d_rows = pl.ds(pl.multiple_of(unit * MOE_UD, MOE_UD), MOE_UD)
        return (pltpu.make_async_copy(wg_hbm.at[e, g_rows, :], stg_g.at[s], sem.at[s]),
                pltpu.make_async_copy(wu_hbm.at[e, g_rows, :], stg_u.at[s], sem.at[s]),
                pltpu.make_async_copy(wd_hbm.at[e, d_rows, :], stg_d.at[s], sem.at[s]))

    def start_unit(e, unit):
        for cp in unit_copies(e, unit, unit % 2):
            cp.start()

    def begin_load(e):
        st_ref[pos] = 0
        start_unit(e, 0)
        start_unit(e, 1)

    def advance(e, slot, n):
        def body(_, c):
            unit = st_ref[pos]

            @pl.when(unit < MOE_UNITS)
            def _():
                s = unit % 2
                for cp in unit_copies(e, unit, s):
                    cp.wait()
                g_rows = pl.ds(pl.multiple_of(unit * MOE_UG, MOE_UG), MOE_UG)
                d_rows = pl.ds(pl.multiple_of(unit * MOE_UD, MOE_UD), MOE_UD)
                wg_bf[slot, g_rows, :] = stg_g[s].astype(BF16)
                wu_bf[slot, g_rows, :] = stg_u[s].astype(BF16)
                wd_bf[slot, d_rows, :] = stg_d[s].astype(BF16)

                @pl.when(unit + 2 < MOE_UNITS)
                def _():
                    start_unit(e, unit + 2)
                st_ref[pos] = unit + 1
            return c
        lax.fori_loop(0, n, body, 0)

    def load_next(nxt):
        @pl.when(nxt >= 0)
        def _():
            begin_load(nxt)

        @pl.when(nxt < 0)
        def _():
            st_ref[pos] = MOE_UNITS

    @pl.when(i < nu_ref[0])
    def _():
        e = blk_e_ref[i]
        nxt = nxt_ref[i]

        @pl.when(i == 0)
        def _():
            st_ref[cur_slot] = 0
            st_ref[cur_e] = e
            begin_load(e)
            advance(e, 0, MOE_UNITS)
            load_next(nxt)

        @pl.when(jnp.logical_and(i > 0, e != st_ref[cur_e]))
        def _():
            slot = 1 - st_ref[cur_slot]
            advance(e, slot, MOE_UNITS)
            st_ref[cur_slot] = slot
            st_ref[cur_e] = e
            load_next(nxt)

        slot = st_ref[cur_slot]
        x = xs_ref[...].astype(BF16)
        hg = _dot(x, wg_bf[slot])
        hu = _dot(x, wu_bf[slot])
        hmid = (jax.nn.silu(hg) * hu).astype(BF16)
        ys_ref[...] = _dot(hmid, wd_bf[slot])

        @pl.when(nxt >= 0)
        def _():
            advance(nxt, 1 - slot, upb_ref[i])

    @pl.when(i >= nu_ref[0])
    def _():
        ys_ref[...] = jnp.zeros_like(ys_ref)


def moe(xs, blk_e, n_used, nxt_e, units_per_block, w_gate, w_up, w_down):
    rows = xs.shape[0]
    grid_spec = pltpu.PrefetchScalarGridSpec(
        num_scalar_prefetch=4,
        grid=(rows // MOE_TM,),
        in_specs=[pl.BlockSpec((MOE_TM, D_MODEL), lambda i, be, nu, nx, ub: (jnp.minimum(i, nu[0] - 1), 0)),
                  pl.BlockSpec(memory_space=pl.ANY), pl.BlockSpec(memory_space=pl.ANY), pl.BlockSpec(memory_space=pl.ANY)],
        out_specs=pl.BlockSpec((MOE_TM, D_MODEL), lambda i, be, nu, nx, ub: (i, 0)),
        scratch_shapes=[pltpu.VMEM((2, D_MODEL, EXPERT_FF), BF16), pltpu.VMEM((2, D_MODEL, EXPERT_FF), BF16),
                        pltpu.VMEM((2, EXPERT_FF, D_MODEL), BF16),
                        pltpu.VMEM((2, MOE_UG, EXPERT_FF), F32), pltpu.VMEM((2, MOE_UG, EXPERT_FF), F32),
                        pltpu.VMEM((2, MOE_UD, D_MODEL), F32),
                        pltpu.SemaphoreType.DMA((2,)), pltpu.SMEM((3,), I32)],
    )
    return pl.pallas_call(
        _moe_kernel,
        grid_spec=grid_spec,
        out_shape=jax.ShapeDtypeStruct((rows, D_MODEL), F32),
        compiler_params=_params(("arbitrary",)),
        name="moe",
    )(blk_e, n_used, nxt_e, units_per_block, xs, w_gate, w_up, w_down)


def _combine_kernel(r0_ref, r1_ref, ys_hbm, h_ref, w_ref, o_ref, buf, sem, *, tm, tok0):
    i = pl.program_id(0)

    def issue(block, slot):
        base = tok0 + block * tm

        def body(r, carry):
            for k, idx_ref in enumerate((r0_ref, r1_ref)):
                pltpu.make_async_copy(ys_hbm.at[pl.ds(idx_ref[base + r], 1)], buf.at[slot, k, pl.ds(r, 1)],
                                      sem.at[slot]).start()
            return carry
        lax.fori_loop(0, tm, body, 0, unroll=8)

    @pl.when(i == 0)
    def _():
        issue(0, 0)

    @pl.when(i + 1 < pl.num_programs(0))
    def _():
        issue(i + 1, (i + 1) % 2)

    slot = i % 2
    _wait_rows(ys_hbm, buf.at[slot, 0], sem.at[slot], tm)
    _wait_rows(ys_hbm, buf.at[slot, 1], sem.at[slot], tm)
    w = w_ref[...]
    o_ref[...] = h_ref[...] + (buf[slot, 0] * w[:, 0:1] + buf[slot, 1] * w[:, 1:2])


def combine(ys, h, route_w, rows0, rows1, *, tok0, tm=256):
    n_tok = h.shape[0]
    grid_spec = pltpu.PrefetchScalarGridSpec(
        num_scalar_prefetch=2,
        grid=(n_tok // tm,),
        in_specs=[pl.BlockSpec(memory_space=pl.ANY),
                  pl.BlockSpec((tm, D_MODEL), lambda i, a, b: (i, 0)),
                  pl.BlockSpec((tm, LANES), lambda i, a, b: (i, 0))],
        out_specs=pl.BlockSpec((tm, D_MODEL), lambda i, a, b: (i, 0)),
        scratch_shapes=[pltpu.VMEM((2, 2, tm, D_MODEL), F32), pltpu.SemaphoreType.DMA((2,))],
    )
    return pl.pallas_call(
        functools.partial(_combine_kernel, tm=tm, tok0=tok0),
        grid_spec=grid_spec,
        out_shape=jax.ShapeDtypeStruct((n_tok, D_MODEL), F32),
        compiler_params=_params(("arbitrary",)),
        name="combine",
    )(rows0, rows1, ys, h, route_w)


IN_SIZES = (SSM_WIDTH, ATTN_WIDTH, KV_WIDTH, KV_WIDTH, IDX_HEADS * IDX_DIM, IDX_DIM, IDX_HEADS, D_MODEL, D_MODEL)
IN_COLS = sum(IN_SIZES)
SRC_U, SRC_Q, SRC_K, SRC_V, SRC_QI, SRC_KI, SRC_WI, SRC_GA, SRC_GB = (int(c) for c in np.cumsum((0,) + IN_SIZES[:-1]))


def _regroup_kernel(w_ref, o_ref):
    runs = ((COL_U, SRC_U, SSM_WIDTH + ATTN_WIDTH), (COL_GA, SRC_GA, D_MODEL), (COL_GB, SRC_GB, D_MODEL),
            (COL_K, SRC_K, 2 * KV_WIDTH + IDX_HEADS * IDX_DIM), (COL_KIWI, SRC_KI, IDX_DIM + IDX_HEADS))
    for dst, src, n in runs:
        o_ref[:, dst:dst + n] = w_ref[:, src:src + n].astype(BF16)
    tail = COL_KIWI + IDX_DIM + IDX_HEADS
    o_ref[:, tail:PROJ_COLS] = jnp.zeros((o_ref.shape[0], PROJ_COLS - tail), BF16)


def _regroup_w_in(w_in, *, tr=256):
    return pl.pallas_call(
        _regroup_kernel,
        grid=(D_MODEL // tr,),
        in_specs=[pl.BlockSpec((tr, IN_COLS), lambda i: (i, 0))],
        out_specs=pl.BlockSpec((tr, PROJ_COLS), lambda i: (i, 0)),
        out_shape=jax.ShapeDtypeStruct((D_MODEL, PROJ_COLS), BF16),
        compiler_params=_params(("arbitrary",)),
        name="regroup_w_in",
    )(w_in)


def _layer(x_p, x_s, cache_k, cache_v, cache_ki, h0_re, h0_im, p):
    bp, tp, _ = x_p.shape
    bs, ts, _ = x_s.shape
    past = cache_k.shape[1]
    n_p, n_s = bp * tp, bs * ts
    n_tok = n_p + n_s

    w_in = _regroup_w_in(p['w_in'])
    ssm_w = _ssm_weights(p['ssm_A_re'], p['ssm_A_im'], p['ssm_log_dt'], p['ssm_B_re'], p['ssm_B_im'],
                         p['ssm_C_re'], p['ssm_C_im'])
    glu_w = (p['w_glu_val'].astype(BF16), p['w_glu_gate'].astype(BF16), p['w_attn_branch'].astype(BF16))
    w_out = p['w_out'].astype(BF16)
    router_w = _router_weights(p['w_router_group'], p['b_router_group'], p['w_router_expert'], p['b_router_expert'])
    seq_tiles = tp // QK_TM

    def front(x, table_pos, table_block):
        proj = in_proj(x, p['norm_mix_g'][None, :], w_in)
        return proj, qk_post(proj, table_pos, table_block, p['q_norm_g'], p['k_norm_g'], p['idx_k_norm_g'])

    def seqs(a, b, t):
        return a.reshape(b, t, a.shape[-1])

    xp = x_p.reshape(n_p, D_MODEL)
    proj_p, (q_b, kf_p, k_b, vf_p, v_b, qi_b, kif_p, ki_b, wi) = front(
        xp, jnp.arange(tp, dtype=I32), lambda i: i % seq_tiles)
    g_p, sre_p, sim_p = ssm(proj_p, ssm_w, p['ssm_D'], jnp.zeros((bp, SSM_LB, 2, SSM_SB), F32),
                            n_batch=bp, seq=tp, row0=0)
    bq = 128
    n_buckets = min(16, tp // bq)
    per = tp // bq // n_buckets
    qp, qip, wip = seqs(q_b, bp, tp), seqs(qi_b, bp, tp), seqs(wi, bp, tp)
    kp, vp, kip = seqs(k_b, bp, tp), seqs(v_b, bp, tp), seqs(ki_b, bp, tp)
    attn_p = jnp.concatenate(
        [dsa(qp, qip, wip, kp, vp, kip, bq=bq, q_blk0=n * per, n_qblk=per, n_keys=(n + 1) * per * bq,
             n_sel=min(IDX_TOPK, tp // 4), packed_bisect=False, stack=1)
         for n in range(n_buckets)], axis=1).reshape(n_p, ATTN_WIDTH)
    merged_p = merge(g_p, attn_p, proj_p, *glu_w)
    h_p, hn_p, ri_p, rw_p, cnt_p = out_proj(xp, merged_p, w_out, p['norm_ffn_g'], router_w, jnp.zeros((1, LANES), F32))

    xs_ = x_s.reshape(n_s, D_MODEL)
    proj_s, (q_b, kf_s, k_b, vf_s, v_b, qi_b, kif_s, ki_b, wi) = front(
        xs_, jnp.tile(past + jnp.arange(ts, dtype=I32), QK_TM // ts), lambda i: 0)
    h0 = jnp.stack([h0_re.reshape(bs, SSM_LB, SSM_SB), h0_im.reshape(bs, SSM_LB, SSM_SB)]).transpose(2, 0, 1, 3)
    g_s, sre_s, sim_s = ssm_step(proj_s, ssm_w, p['ssm_D'], h0, n_batch=bs, seq=ts, row0=0)
    attn_s = dsa_step(seqs(q_b, bs, ts), seqs(qi_b, bs, ts), seqs(wi, bs, ts),
                      cache_k, cache_v, cache_ki,
                      seqs(k_b, bs, ts), seqs(v_b, bs, ts), seqs(ki_b, bs, ts),
                      n_sel=min(IDX_TOPK, (past + ts) // 4)).reshape(n_s, ATTN_WIDTH)
    merged_s = merge(g_s, attn_s, proj_s, *glu_w)
    h_s, hn_s, ri_s, rw_s, cnt = out_proj(xs_, merged_s, w_out, p['norm_ffn_g'], router_w, cnt_p)

    counts = cnt[0, N_EXPERT_GROUPS:ROUTER_COLS].astype(I32)
    pad_start, pad_end, n_used = _block_layout(counts)
    dest = dest_rows(jnp.concatenate([ri_p, ri_s], axis=1), pad_start)
    dest0, dest1 = dest[0], dest[1]
    n_blocks = _moe_rows(n_tok) // MOE_TM
    blk = jnp.minimum(jnp.arange(n_blocks, dtype=I32), n_used - 1)
    blk_e = jnp.minimum(jnp.sum((pad_end[None, :] <= (blk * MOE_TM)[:, None]).astype(I32), axis=1), N_EXPERTS - 1)
    after = pad_end[blk_e] // MOE_TM
    nxt_e = jnp.where(after < n_used, blk_e[jnp.minimum(after, n_blocks - 1)], -1).astype(I32)
    blocks_of_e = jnp.maximum((pad_end - pad_start)[blk_e] // MOE_TM, 1)
    units_per_block = ((MOE_UNITS + blocks_of_e - 1) // blocks_of_e).astype(I32)
    n_used = n_used.reshape(1)

    xs = dispatch(hn_p, hn_s, dest0, dest1, pad_end, counts, n_used)
    ys = moe(xs, blk_e, n_used, nxt_e, units_per_block, p['w_exp_gate'], p['w_exp_up'], p['w_exp_down'])
    y_p = combine(ys, h_p, rw_p, dest0, dest1, tok0=0).reshape(bp, tp, D_MODEL)
    y_s = combine(ys, h_s, rw_s, dest0, dest1, tok0=n_p).reshape(bs, ts, D_MODEL)

    def heads(a, b, t):
        return a.reshape(b, t, N_KV_HEADS, HEAD_DIM)

    new_p = (heads(kf_p, bp, tp), heads(vf_p, bp, tp), kif_p.reshape(bp, tp, IDX_DIM), sre_p, sim_p)
    new_s = (heads(kf_s, bs, ts), heads(vf_s, bs, ts), kif_s.reshape(bs, ts, IDX_DIM), sre_s, sim_s)
    return y_p, y_s, new_p, new_s


def kernel(x_prompt, x_sample, cache_k, cache_v, cache_idx_k, state_ssm_re, state_ssm_im, norm_mix_g, w_in, q_norm_g, k_norm_g, idx_k_norm_g, ssm_A_re, ssm_A_im, ssm_log_dt, ssm_B_re, ssm_B_im, ssm_C_re, ssm_C_im, ssm_D, w_glu_val, w_glu_gate, w_attn_branch, w_out, norm_ffn_g, w_router_group, b_router_group, w_router_expert, b_router_expert, w_exp_gate, w_exp_up, w_exp_down):
    depth = w_in.shape[0]
    assert depth == 1, "prompt and sample tokens are batched through one layer"
    names = ('norm_mix_g', 'w_in', 'q_norm_g', 'k_norm_g', 'idx_k_norm_g', 'ssm_A_re', 'ssm_A_im', 'ssm_log_dt',
             'ssm_B_re', 'ssm_B_im', 'ssm_C_re', 'ssm_C_im', 'ssm_D', 'w_glu_val', 'w_glu_gate', 'w_attn_branch',
             'w_out', 'norm_ffn_g', 'w_router_group', 'b_router_group', 'w_router_expert', 'b_router_expert',
             'w_exp_gate', 'w_exp_up', 'w_exp_down')
    vals = (norm_mix_g, w_in, q_norm_g, k_norm_g, idx_k_norm_g, ssm_A_re, ssm_A_im, ssm_log_dt, ssm_B_re, ssm_B_im,
            ssm_C_re, ssm_C_im, ssm_D, w_glu_val, w_glu_gate, w_attn_branch, w_out, norm_ffn_g, w_router_group,
            b_router_group, w_router_expert, b_router_expert, w_exp_gate, w_exp_up, w_exp_down)
    p = {n: v[0] for n, v in zip(names, vals)}
    y_p, y_s, new_p, new_s = _layer(x_prompt, x_sample, cache_k[0], cache_v[0], cache_idx_k[0],
                                    state_ssm_re[0], state_ssm_im[0], p)
    st_p = tuple(a[None] for a in new_p)
    st_s = tuple(a[None] for a in new_s)
    return (y_p, y_s) + st_p + st_s
```

```python
import functools

import numpy as np
import jax
import jax.numpy as jnp
from jax import lax
from jax.experimental import pallas as pl
from jax.experimental.pallas import tpu as pltpu

F32 = jnp.float32
BF16 = jnp.bfloat16
I32 = jnp.int32

D_MODEL = 2048
CHUNK = 64
SSM_WIDTH = 1024
SSM_GROUP = 16
SSM_GROUPS = 64
SSM_STATE = 64
ATTN_WIDTH = 1024
HEAD_DIM = 128
N_HEADS = 8
N_KV_HEADS = 2
KV_GROUP = 4
IDX_HEADS = 8
IDX_DIM = 64
IDX_TOPK = 256
ROPE_THETA = 500000.0
N_EXPERT_GROUPS = 4
EXPERTS_PER_GROUP = 8
N_EXPERTS = 32
TOP_K = 2
EXPERT_FF = 1024
EPS = 1e-6

LANES = 128
SUBLANES = 8
VMEM_LIMIT = 56 * 1024 * 1024

COL_U, COL_Q, COL_GA, COL_GB, COL_K, COL_V, COL_QI, COL_KIWI = 0, 1024, 2048, 4096, 6144, 6400, 6656, 7168
PROJ_COLS = 7296
PROJ_TN = 2432
KV_WIDTH = N_KV_HEADS * HEAD_DIM

SSM_LB = SSM_WIDTH // LANES
SSM_SB = 8 * SSM_STATE

INT_MIN = np.int32(-2 ** 31)
KEY_NEG_INF = np.int32(np.array([0xFF800000], np.uint32).view(np.int32)[0] ^ 0x7FFFFFFF)


def _params(sem, vmem=VMEM_LIMIT):
    return pltpu.CompilerParams(dimension_semantics=sem, vmem_limit_bytes=vmem)


def _dot(a, b):
    return jnp.dot(a, b, preferred_element_type=F32)


def _dot_nt(a, b):
    return lax.dot_general(a, b, (((1,), (1,)), ((), ())), preferred_element_type=F32)


def _split_bf16(x):
    hi = x.astype(BF16)
    lo = (x - hi.astype(F32)).astype(BF16)
    return hi, lo


def _in_proj_kernel(x_ref, g_ref, w_ref, o_ref, xn_ref):
    @pl.when(pl.program_id(1) == 0)
    def _():
        x = x_ref[...]
        ms = jnp.mean(x * x, axis=-1, keepdims=True)
        xn_ref[...] = (x * lax.rsqrt(ms + EPS) * g_ref[...]).astype(BF16)

    o_ref[...] = _dot(xn_ref[...], w_ref[...])


def in_proj(x, gain, w_bf16, *, tm=512):
    n_tok = x.shape[0]
    return pl.pallas_call(
        _in_proj_kernel,
        grid=(n_tok // tm, PROJ_COLS // PROJ_TN),
        in_specs=[pl.BlockSpec((tm, D_MODEL), lambda i, j: (i, 0)),
                  pl.BlockSpec((1, D_MODEL), lambda i, j: (0, 0)),
                  pl.BlockSpec((D_MODEL, PROJ_TN), lambda i, j: (0, j))],
        out_specs=pl.BlockSpec((tm, PROJ_TN), lambda i, j: (i, j)),
        out_shape=jax.ShapeDtypeStruct((n_tok, PROJ_COLS), F32),
        scratch_shapes=[pltpu.VMEM((tm, D_MODEL), BF16)],
        compiler_params=_params(("arbitrary", "arbitrary")),
        name="in_proj",
    )(x, gain, w_bf16)


def _rope(x, c, s_lo, s_hi, half):
    n = x.shape[-1]
    return x * c + pltpu.roll(x, n - half, 1) * s_lo + pltpu.roll(x, half, 1) * s_hi


def _head_norm(x, g):
    ms = jnp.mean(x * x, axis=-1, keepdims=True)
    return x * lax.rsqrt(ms + EPS) * g


V_AUG = 2 * HEAD_DIM


def _store_v_aug(dst_ref, row0, v_heads):
    n = v_heads[0].shape[0]
    one_col = jnp.where(lax.broadcasted_iota(I32, (n, HEAD_DIM), 1) == 0, 1.0, 0.0).astype(BF16)
    for h, v in enumerate(v_heads):
        dst_ref[row0:row0 + n, h * V_AUG:h * V_AUG + HEAD_DIM] = v.astype(BF16)
        dst_ref[row0:row0 + n, h * V_AUG + HEAD_DIM:(h + 1) * V_AUG] = one_col


def _qk_post_kernel(q_ref, k_ref, v_ref, qi_ref, kw_ref, c128_ref, sl128_ref, sh128_ref,
                    c64_ref, sl64_ref, sh64_ref, qg_ref, kg_ref, ig_ref,
                    qo_ref, kf_ref, kb_ref, vf_ref, vb_ref, qio_ref, kif_ref, kib_ref, wo_ref):
    c128, sl128, sh128 = c128_ref[...], sl128_ref[...], sh128_ref[...]
    c64, sl64, sh64 = c64_ref[...], sl64_ref[...], sh64_ref[...]
    half128 = HEAD_DIM // 8
    half64 = IDX_DIM // 8
    for h in range(N_HEADS):
        sl = slice(h * LANES, (h + 1) * LANES)
        qo_ref[:, sl] = _rope(_head_norm(q_ref[:, sl], qg_ref[...]), c128, sl128, sh128, half128).astype(BF16)
    for h in range(N_KV_HEADS):
        sl = slice(h * LANES, (h + 1) * LANES)
        kk = _rope(_head_norm(k_ref[:, sl], kg_ref[...]), c128, sl128, sh128, half128)
        kf_ref[:, sl] = kk
        kb_ref[:, sl] = kk.astype(BF16)
    v = v_ref[...]
    vf_ref[...] = v
    _store_v_aug(vb_ref, 0, [v[:, h * HEAD_DIM:(h + 1) * HEAD_DIM] for h in range(N_KV_HEADS)])
    lane = lax.broadcasted_iota(I32, c64.shape, 1)
    low = lane < IDX_DIM
    for p in range(IDX_HEADS // 2):
        x = _rope(qi_ref[:, p * LANES:(p + 1) * LANES], c64, sl64, sh64, half64)
        qio_ref[:, (2 * p) * LANES:(2 * p + 1) * LANES] = jnp.where(low, x, 0.0).astype(BF16)
        qio_ref[:, (2 * p + 1) * LANES:(2 * p + 2) * LANES] = jnp.where(low, pltpu.roll(x, IDX_DIM, 1), 0.0).astype(BF16)
    kw = kw_ref[...]
    ms = jnp.sum(jnp.where(low, kw * kw, 0.0), axis=-1, keepdims=True) * (1.0 / IDX_DIM)
    ki = _rope(kw * lax.rsqrt(ms + EPS) * ig_ref[...], c64, sl64, sh64, half64)
    kif_ref[...] = ki[:, :IDX_DIM]
    kib_ref[...] = jnp.where(low, ki, 0.0).astype(BF16)
    wo_ref[...] = (pltpu.roll(kw, IDX_DIM, 1) * IDX_HEADS ** -0.5) * IDX_DIM ** -0.5


def _rope_tables(pos, head_dim):
    r = head_dim // 4
    half = r // 2
    inv = ROPE_THETA ** (-jnp.arange(half, dtype=F32) * 2.0 / r)
    ang = pos.astype(F32)[:, None] * inv[None, :]
    cos, sin = jnp.cos(ang), jnp.sin(ang)
    n = pos.shape[0]
    zh = jnp.zeros((n, half), F32)
    rest = head_dim - r
    c = jnp.concatenate([cos, cos, jnp.ones((n, rest), F32)], axis=-1)
    s_lo = jnp.concatenate([-sin, zh, jnp.zeros((n, rest), F32)], axis=-1)
    s_hi = jnp.concatenate([zh, sin, jnp.zeros((n, rest), F32)], axis=-1)
    rep = LANES // head_dim
    return tuple(jnp.tile(t, (1, rep)) for t in (c, s_lo, s_hi))


QK_TM = 512


def qk_post(proj, table_pos, table_block, q_gain, k_gain, ik_gain):
    tm = QK_TM
    n_tok = proj.shape[0]
    t128 = _rope_tables(table_pos, HEAD_DIM)
    t64 = _rope_tables(table_pos, IDX_DIM)
    ik_gain128 = jnp.concatenate([ik_gain, jnp.zeros((LANES - IDX_DIM,), F32)])[None, :]

    def col(width, start):
        return pl.BlockSpec((tm, width), lambda i: (i, start // width))

    def row(width):
        return pl.BlockSpec((tm, width), lambda i: (i, 0))

    table = pl.BlockSpec((tm, LANES), lambda i: (table_block(i), 0))
    gain = pl.BlockSpec((1, LANES), lambda i: (0, 0))
    return pl.pallas_call(
        _qk_post_kernel,
        grid=(n_tok // tm,),
        in_specs=[col(ATTN_WIDTH, COL_Q), col(KV_WIDTH, COL_K), col(KV_WIDTH, COL_V), col(IDX_HEADS * IDX_DIM, COL_QI),
                  col(LANES, COL_KIWI)] + [table] * 6 + [gain] * 3,
        out_specs=[row(ATTN_WIDTH), row(KV_WIDTH), row(KV_WIDTH), row(KV_WIDTH), row(N_KV_HEADS * V_AUG), row(IDX_HEADS * LANES),
                   row(IDX_DIM), row(LANES), row(LANES)],
        out_shape=[jax.ShapeDtypeStruct((n_tok, ATTN_WIDTH), BF16),
                   jax.ShapeDtypeStruct((n_tok, KV_WIDTH), F32), jax.ShapeDtypeStruct((n_tok, KV_WIDTH), BF16),
                   jax.ShapeDtypeStruct((n_tok, KV_WIDTH), F32), jax.ShapeDtypeStruct((n_tok, N_KV_HEADS * V_AUG), BF16),
                   jax.ShapeDtypeStruct((n_tok, IDX_HEADS * LANES), BF16),
                   jax.ShapeDtypeStruct((n_tok, IDX_DIM), F32), jax.ShapeDtypeStruct((n_tok, LANES), BF16),
                   jax.ShapeDtypeStruct((n_tok, LANES), F32)],
        compiler_params=_params(("arbitrary",)),
        name="qk_post",
    )(proj, proj, proj, proj, proj, *t128, *t64, q_gain[None, :], k_gain[None, :], ik_gain128)


def _gelu_tanh(x):
    return 0.5 * x * (1.0 + jnp.tanh(np.float32(np.sqrt(2.0 / np.pi)) * (x + 0.044715 * (x * x * x))))


SSM_LT = SSM_SB // LANES
SSM_SEG = 128


def _ssm_kernel(u_ref, wb_ref, wc_ref, pw_ref, d_ref, h0_ref, g_ref, sre_ref, sim_ref,
                er_ref, ei_ref, car_ref, up_ref, yp_ref):
    c = pl.program_id(2)

    @pl.when(c == 0)
    def _():
        car_ref[...] = h0_ref[...]

    for j in range(SSM_SEG):
        up_ref[j * SUBLANES:(j + 1) * SUBLANES, :] = u_ref[pl.ds(j, SUBLANES, stride=SSM_SEG), :]
    e = _dot(up_ref[...].astype(BF16), wb_ref[...])
    tiles = [slice(lt * LANES, (lt + 1) * LANES) for lt in range(SSM_LT)]
    for lt, sl in enumerate(tiles):
        er_ref[lt] = e[:, sl]
        ei_ref[lt] = e[:, SSM_SB + lt * LANES:SSM_SB + (lt + 1) * LANES]

    def cmul_add(ar, ai, br, bi, cr, ci):
        return ar * br - ai * bi + cr, ar * bi + ai * br + ci

    lb = [(pw_ref[0, 0:1, sl], pw_ref[1, 0:1, sl]) for sl in tiles]
    zero = jnp.zeros((SUBLANES, LANES), F32)
    st = [(zero, zero)] * SSM_LT
    for j in range(SSM_SEG):
        rows = slice(j * SUBLANES, (j + 1) * SUBLANES)
        for lt in range(SSM_LT):
            st[lt] = cmul_add(*lb[lt], *st[lt], er_ref[lt, rows, :], ei_ref[lt, rows, :])
            er_ref[lt, rows, :] = st[lt][0]
            ei_ref[lt, rows, :] = st[lt][1]

    enter = []
    for lt, sl in enumerate(tiles):
        seg_r, seg_i = pw_ref[0, SSM_SEG - 1:SSM_SEG, sl], pw_ref[1, SSM_SEG - 1:SSM_SEG, sl]
        cr, ci = car_ref[0:1, sl], car_ref[1:2, sl]
        rows_r, rows_i = [], []
        for r in range(SUBLANES):
            rows_r.append(cr)
            rows_i.append(ci)
            cr, ci = cmul_add(seg_r, seg_i, cr, ci, st[lt][0][r:r + 1], st[lt][1][r:r + 1])
        car_ref[0:1, sl] = cr
        car_ref[1:2, sl] = ci
        enter.append((jnp.concatenate(rows_r, axis=0), jnp.concatenate(rows_i, axis=0)))

    for j in range(SSM_SEG):
        rows = slice(j * SUBLANES, (j + 1) * SUBLANES)
        for lt, sl in enumerate(tiles):
            xr, xi = cmul_add(pw_ref[0, j:j + 1, sl], pw_ref[1, j:j + 1, sl], *enter[lt],
                              er_ref[lt, rows, :], ei_ref[lt, rows, :])
            er_ref[lt, rows, :] = xr
            ei_ref[lt, rows, :] = xi

    y = None
    for lt, sl in enumerate(tiles):
        t = _dot(er_ref[lt].astype(BF16), wc_ref[0, sl, :]) - _dot(ei_ref[lt].astype(BF16), wc_ref[1, sl, :])
        y = t if y is None else y + t
    yp_ref[...] = y
    out_rows = 2 * SUBLANES
    for t0 in range(0, SUBLANES * SSM_SEG, out_rows):
        r, j0 = divmod(t0, SSM_SEG)
        rows = slice(t0, t0 + out_rows)
        yt = yp_ref[pl.ds(j0 * SUBLANES + r, out_rows, stride=SUBLANES), :] + d_ref[...] * u_ref[rows, :]
        g_ref[rows, :] = _gelu_tanh(yt).astype(BF16)

    @pl.when(c == pl.num_programs(2) - 1)
    def _():
        sre_ref[...] = car_ref[0:1, :]
        sim_ref[...] = car_ref[1:2, :]


def _ssm_weights(a_re, a_im, log_dt, b_re, b_im, c_re, c_im):
    lam_re, lam_im = a_re, a_im
    dt = jnp.exp(log_dt)[:, None]
    mag = jnp.exp(lam_re * dt)
    lb_re, lb_im = mag * jnp.cos(lam_im * dt), mag * jnp.sin(lam_im * dt)
    den = lam_re * lam_re + lam_im * lam_im
    num_re = lb_re - 1.0
    z_re = (num_re * lam_re + lb_im * lam_im) / den
    z_im = (lb_im * lam_re - num_re * lam_im) / den
    zb_re = z_re[:, :, None] * b_re - z_im[:, :, None] * b_im
    zb_im = z_re[:, :, None] * b_im + z_im[:, :, None] * b_re
    eye = jnp.eye(8, dtype=F32)

    def blockdiag_in(w):
        return jnp.einsum('jgph,gk->jghkp', w.reshape(SSM_LB, 8, SSM_STATE, SSM_GROUP), eye).reshape(SSM_LB, LANES, SSM_SB)

    def blockdiag_out(w):
        return jnp.einsum('jghp,gk->jkpgh', w.reshape(SSM_LB, 8, SSM_GROUP, SSM_STATE), eye).reshape(SSM_LB, SSM_SB, LANES)

    wb = jnp.concatenate([blockdiag_in(zb_re), blockdiag_in(zb_im)], axis=-1).astype(BF16)
    wc = jnp.stack([blockdiag_out(c_re), blockdiag_out(c_im)], axis=1).astype(BF16)

    pr, pi_ = lb_re.reshape(SSM_LB, 1, SSM_SB), lb_im.reshape(SSM_LB, 1, SSM_SB)
    while pr.shape[1] < SSM_SEG:
        tr, ti = pr[:, -1:], pi_[:, -1:]
        pr, pi_ = (jnp.concatenate([pr, pr * tr - pi_ * ti], axis=1), jnp.concatenate([pi_, pr * ti + pi_ * tr], axis=1))
    pw = jnp.stack([pr, pi_], axis=1)
    return wb, wc, pw


def ssm(proj, ssm_w, d_skip, h0, *, n_batch, seq, row0):
    wb, wc, pw = ssm_w
    tc = SUBLANES * SSM_SEG
    n_chunks = seq // tc
    blk0 = row0 // tc
    n_tok = n_batch * seq
    state_shape = jax.ShapeDtypeStruct((n_batch, SSM_LB, 1, SSM_SB), F32)
    state_spec = pl.BlockSpec((None, None, 1, SSM_SB), lambda b, j, c: (b, j, 0, 0))
    g, s_re, s_im = pl.pallas_call(
        _ssm_kernel,
        grid=(n_batch, SSM_LB, n_chunks),
        in_specs=[pl.BlockSpec((tc, LANES), lambda b, j, c: (blk0 + b * n_chunks + c, j)),
                  pl.BlockSpec((None, LANES, 2 * SSM_SB), lambda b, j, c: (j, 0, 0)),
                  pl.BlockSpec((None, 2, SSM_SB, LANES), lambda b, j, c: (j, 0, 0, 0)),
                  pl.BlockSpec((None, 2, SSM_SEG, SSM_SB), lambda b, j, c: (j, 0, 0, 0)),
                  pl.BlockSpec((1, LANES), lambda b, j, c: (0, j)),
                  pl.BlockSpec((None, None, 2, SSM_SB), lambda b, j, c: (b, j, 0, 0))],
        out_specs=[pl.BlockSpec((tc, LANES), lambda b, j, c: (b * n_chunks + c, j)), state_spec, state_spec],
        out_shape=[jax.ShapeDtypeStruct((n_tok, SSM_WIDTH), BF16), state_shape, state_shape],
        scratch_shapes=[pltpu.VMEM((SSM_LT, tc, LANES), F32), pltpu.VMEM((SSM_LT, tc, LANES), F32),
                        pltpu.VMEM((2, SSM_SB), F32), pltpu.VMEM((tc, LANES), F32), pltpu.VMEM((tc, LANES), F32)],
        compiler_params=_params(("arbitrary", "arbitrary", "arbitrary")),
        name="ssm",
    )(proj, wb, wc, pw, d_skip[None, :], h0)
    return g, s_re.reshape(n_batch, SSM_GROUPS, SSM_STATE), s_im.reshape(n_batch, SSM_GROUPS, SSM_STATE)


def _ssm_step_kernel(u_ref, wb_ref, wc_ref, pw_ref, d_ref, h0_ref, g_ref, sre_ref, sim_ref, er_ref, ei_ref, *, seq):
    n_seq = h0_ref.shape[1]
    u = u_ref[...]
    e = _dot(u.astype(BF16), wb_ref[...])
    n_lt = SSM_SB // LANES
    y = d_ref[...] * u
    for lt in range(n_lt):
        sl = slice(lt * LANES, (lt + 1) * LANES)
        er_ref[...] = e[:, lt * LANES:(lt + 1) * LANES]
        ei_ref[...] = e[:, SSM_SB + lt * LANES:SSM_SB + (lt + 1) * LANES]
        lr, li = pw_ref[0, 0:1, sl], pw_ref[1, 0:1, sl]
        sr, si = h0_ref[0, :, sl], h0_ref[1, :, sl]
        for t in range(seq):
            rows = pl.ds(t, n_seq, stride=seq)
            sr, si = lr * sr - li * si + er_ref[rows, :], lr * si + li * sr + ei_ref[rows, :]
            er_ref[rows, :] = sr
            ei_ref[rows, :] = si
        y = y + (_dot(er_ref[...].astype(BF16), wc_ref[0, sl, :]) - _dot(ei_ref[...].astype(BF16), wc_ref[1, sl, :]))
        sre_ref[:, sl] = sr
        sim_ref[:, sl] = si
    g_ref[...] = _gelu_tanh(y).astype(BF16)


def ssm_step(proj, ssm_w, d_skip, h0, *, n_batch, seq, row0):
    wb, wc, pw = ssm_w
    n_tok = n_batch * seq
    assert row0 % n_tok == 0
    state_shape = jax.ShapeDtypeStruct((SSM_LB, n_batch, SSM_SB), F32)
    state_spec = pl.BlockSpec((None, n_batch, SSM_SB), lambda j: (j, 0, 0))
    g, s_re, s_im = pl.pallas_call(
        functools.partial(_ssm_step_kernel, seq=seq),
        grid=(SSM_LB,),
        in_specs=[pl.BlockSpec((n_tok, LANES), lambda j: (row0 // n_tok, j)),
                  pl.BlockSpec((None, LANES, 2 * SSM_SB), lambda j: (j, 0, 0)),
                  pl.BlockSpec((None, 2, SSM_SB, LANES), lambda j: (j, 0, 0, 0)),
                  pl.BlockSpec((None, 2, SSM_SEG, SSM_SB), lambda j: (j, 0, 0, 0)),
                  pl.BlockSpec((1, LANES), lambda j: (0, j)),
                  pl.BlockSpec((None, 2, n_batch, SSM_SB), lambda j: (j, 0, 0, 0))],
        out_specs=[pl.BlockSpec((n_tok, LANES), lambda j: (0, j)), state_spec, state_spec],
        out_shape=[jax.ShapeDtypeStruct((n_tok, SSM_WIDTH), BF16), state_shape, state_shape],
        scratch_shapes=[pltpu.VMEM((n_tok, LANES), F32), pltpu.VMEM((n_tok, LANES), F32)],
        compiler_params=_params(("arbitrary",)),
        name="ssm_step",
    )(proj, wb, wc, pw, d_skip[None, :], h0)

    def per_seq(s):
        return s.transpose(1, 0, 2).reshape(n_batch, SSM_GROUPS, SSM_STATE)

    return g, per_seq(s_re), per_seq(s_im)


def _row_sum(x):
    return jnp.sum(x, axis=1, keepdims=True)


def _row_count(mask):
    return _row_sum(jnp.where(mask, 1, 0))


I16 = jnp.int16
I16_MIN = -2 ** 15


def _count16(ref, cand, compare):
    accs = [None] * 4
    for t in range(ref.shape[1] // LANES):
        x = jnp.where(compare(ref[:, t * LANES:(t + 1) * LANES], cand), I16(1), I16(0))
        accs[t % 4] = x if accs[t % 4] is None else accs[t % 4] + x
    accs = [a for a in accs if a is not None]
    total = accs[0]
    for a in accs[1:]:
        total = total + a
    return _row_sum(total.astype(I32))


def _bisect16(ref, target):
    def step(i, base):
        cand = base + lax.shift_left(np.int32(1), np.int32(15) - i)
        cnt = _count16(ref, cand.astype(I16), lambda a, b: a >= b)
        return jnp.where(cnt >= target, cand, base)
    return lax.fori_loop(0, 16, step, jnp.full((ref.shape[0], 1), I16_MIN, I32))


def _bisect32(key_ref, n_sel):
    bq, n_keys = key_ref.shape
    hr = bq // 2

    def lane_counts(h, cand):
        accs = [None] * 4
        for t in range(n_keys // LANES):
            x = jnp.where(key_ref[h * hr:(h + 1) * hr, t * LANES:(t + 1) * LANES] >= cand, 1, 0)
            accs[t % 4] = x if accs[t % 4] is None else accs[t % 4] + x
        accs = [a for a in accs if a is not None]
        total = accs[0]
        for a in accs[1:]:
            total = total + a
        return total

    def decide(part, cand, base):
        return jnp.where(_row_sum(part) >= n_sel, cand, base)

    def bit(i):
        return lax.shift_left(np.int32(1), np.int32(31) - i)

    def body(i, state):
        base_a, base_b, part_b = state
        cand_a = base_a + bit(i)
        part_a = lane_counts(0, cand_a)
        base_b = decide(part_b, base_b + bit(i - 1), base_b)
        part_b = lane_counts(1, base_b + bit(i))
        return decide(part_a, cand_a, base_a), base_b, part_b

    base0 = jnp.full((hr, 1), INT_MIN, I32)
    first = base0 + bit(0)
    state = (decide(lane_counts(0, first), first, base0), base0, lane_counts(1, first))
    base_a, base_b, part_b = lax.fori_loop(1, 32, body, state)
    base_b = decide(part_b, base_b + bit(31), base_b)
    return jnp.concatenate([base_a, base_b], axis=0)


def _stack_heads(ref, heads):
    return jnp.concatenate([ref[:, h * LANES:(h + 1) * LANES] for h in heads], axis=0)


def _dsa_body(q_ref, qi_ref, wi_ref, k_ref, v_ref, ki_ref, o_ref, key_ref, bias_ref, hi_ref, lo_ref, p_ref,
              *, q_pos_first, s_valid, n_sel, packed_bisect, stack):
    bq, n_keys = key_ref.shape
    col = lax.broadcasted_iota(I32, (bq, n_keys), 1)
    qpos = q_pos_first + lax.broadcasted_iota(I32, (bq, 1), 0)
    allowed = col < jnp.minimum((qpos // CHUNK + 1) * CHUNK, s_valid)

    ki = ki_ref[...]
    score = None
    for h0 in range(0, IDX_HEADS, stack):
        d = _dot_nt(_stack_heads(qi_ref, range(h0, h0 + stack)), ki)
        for j in range(stack):
            t = jnp.maximum(d[j * bq:(j + 1) * bq], 0.0) * wi_ref[:, h0 + j:h0 + j + 1]
            score = t if score is None else score + t
    score = jnp.where(score == 0.0, 0.0, score)
    bits = pltpu.bitcast(score, I32)
    key = jnp.where(bits < 0, bits ^ np.int32(0x7FFFFFFF), bits)
    key = jnp.where(allowed, key, KEY_NEG_INF)
    key_ref[...] = key

    if packed_bisect:
        hi_ref[...] = (key >> 16).astype(I16)
        lo_ref[...] = ((key & 0xFFFF) + I16_MIN).astype(I16)
        thr_hi = _bisect16(hi_ref, n_sel)
        thr_hi16 = thr_hi.astype(I16)
        need_lo = n_sel - _count16(hi_ref, thr_hi16, lambda a, b: a > b)
        lo_ref[...] = jnp.where(hi_ref[...] == thr_hi16, lo_ref[...], I16(I16_MIN))
        thr_lo = _bisect16(lo_ref, need_lo)
        thr = lax.shift_left(thr_hi, np.int32(16)) + (thr_lo - I16_MIN)
    else:
        thr = _bisect32(key_ref, n_sel)
    thr = jnp.maximum(thr, KEY_NEG_INF)

    key = key_ref[...]
    need = n_sel - _row_count(key > thr)
    n_eq = _row_count(key == thr)
    n_bits = int(n_keys - 1).bit_length()

    def tie_cut():
        def step(i, j0):
            cand = j0 + lax.shift_left(np.int32(1), np.int32(n_bits - 1) - i)
            cnt = _row_sum(jnp.where(key_ref[...] == thr, jnp.where(col < cand, 1, 0), 0))
            return jnp.where(cnt < need, cand, j0)
        return lax.fori_loop(0, n_bits, step, jnp.zeros((bq, 1), I32))

    split = jnp.max(jnp.where(n_eq > need, 1, 0)) > 0
    j_last = lax.cond(split, tie_cut, lambda: jnp.full((bq, 1), n_keys, I32))
    tie_bias = jnp.where(thr == KEY_NEG_INF, -jnp.inf, 0.0)
    bias_ref[...] = jnp.where(key > thr, 0.0,
                              jnp.where(key == thr, jnp.where(col <= j_last, tie_bias, -jnp.inf), -jnp.inf))

    c = np.float32(HEAD_DIM ** -0.5 * np.log2(np.e))
    for h0 in range(0, N_HEADS, stack):
        kv = h0 // KV_GROUP
        heads = range(h0, h0 + stack)
        s_all = _dot_nt(_stack_heads(q_ref, heads), k_ref[:, kv * HEAD_DIM:(kv + 1) * HEAD_DIM])
        for g in range(stack):
            s = s_all[g * bq:(g + 1) * bq] + bias_ref[...]
            m = jnp.max(s, axis=1, keepdims=True)
            p_ref[g * bq:(g + 1) * bq, :] = jnp.exp2((s - m) * c).astype(BF16)
        pv = _dot(p_ref[0:stack * bq, :], v_ref[:, kv * V_AUG:(kv + 1) * V_AUG])
        for g, h in enumerate(heads):
            o = pv[g * bq:(g + 1) * bq]
            o_ref[:, h * HEAD_DIM:(h + 1) * HEAD_DIM] = (o[:, :HEAD_DIM] / o[:, HEAD_DIM:HEAD_DIM + 1]).astype(BF16)


def _dsa_scratch(bq, n_keys):
    return [pltpu.VMEM((bq, n_keys), I32), pltpu.VMEM((bq, n_keys), F32),
            pltpu.VMEM((bq, n_keys), I16), pltpu.VMEM((bq, n_keys), I16), pltpu.VMEM((KV_GROUP * bq, n_keys), BF16)]


def _dsa_kernel(q_ref, qi_ref, wi_ref, k_ref, v_ref, ki_ref, o_ref, *scratch, q_pos0, **static):
    bq = scratch[0].shape[0]
    _dsa_body(q_ref, qi_ref, wi_ref, k_ref, v_ref, ki_ref, o_ref, *scratch,
              q_pos_first=q_pos0 + pl.program_id(1) * bq, **static)


def dsa(q, qi, wi, k, v, ki, *, bq, q_blk0, n_qblk, n_keys, n_sel, packed_bisect, stack):
    n_batch, seq = q.shape[:2]

    def qspec(width):
        return pl.BlockSpec((None, bq, width), lambda b, i: (b, q_blk0 + i, 0))

    def kspec(width):
        return pl.BlockSpec((None, n_keys, width), lambda b, i: (b, 0, 0))

    return pl.pallas_call(
        functools.partial(_dsa_kernel, q_pos0=q_blk0 * bq, s_valid=seq, n_sel=n_sel, packed_bisect=packed_bisect,
                          stack=stack),
        grid=(n_batch, n_qblk),
        in_specs=[qspec(ATTN_WIDTH), qspec(IDX_HEADS * LANES), qspec(LANES), kspec(KV_WIDTH), kspec(N_KV_HEADS * V_AUG),
                  kspec(LANES)],
        out_specs=pl.BlockSpec((None, bq, ATTN_WIDTH), lambda b, i: (b, i, 0)),
        out_shape=jax.ShapeDtypeStruct((n_batch, n_qblk * bq, ATTN_WIDTH), BF16),
        scratch_shapes=_dsa_scratch(bq, n_keys),
        compiler_params=_params(("arbitrary", "arbitrary")),
        name="dsa",
    )(q, qi, wi, k, v, ki)


def _dsa_step_kernel(q_ref, qi_ref, wi_ref, ck_hbm, cv_hbm, cki_ref, nk_ref, nv_ref, nki_ref, o_ref,
                     k_buf, v_buf, ki_buf, cache_buf, sem, *scratch, past, n_sel):
    b = pl.program_id(0)

    def cache_copies(seq, slot):
        return [pltpu.make_async_copy(src.at[seq, :, h, :], cache_buf.at[slot, a, h], sem.at[slot])
                for a, src in enumerate((ck_hbm, cv_hbm)) for h in range(N_KV_HEADS)]

    @pl.when(b == 0)
    def _():
        for cp in cache_copies(0, 0):
            cp.start()

    @pl.when(b + 1 < pl.num_programs(0))
    def _():
        for cp in cache_copies(b + 1, (b + 1) % 2):
            cp.start()

    slot = b % 2
    for cp in cache_copies(b, slot):
        cp.wait()

    ts = nk_ref.shape[0]
    n_keys = k_buf.shape[0]
    for h in range(N_KV_HEADS):
        k_buf[0:past, h * HEAD_DIM:(h + 1) * HEAD_DIM] = cache_buf[slot, 0, h].astype(BF16)
    _store_v_aug(v_buf, 0, [cache_buf[slot, 1, h] for h in range(N_KV_HEADS)])
    for buf, new in ((k_buf, nk_ref), (v_buf, nv_ref)):
        buf[past:past + ts, :] = new[...]
        buf[past + ts:n_keys, :] = jnp.zeros((n_keys - past - ts, buf.shape[1]), BF16)
    ki_buf[0:past, 0:IDX_DIM] = cki_ref[...].astype(BF16)
    ki_buf[0:past, IDX_DIM:LANES] = jnp.zeros((past, LANES - IDX_DIM), BF16)
    ki_buf[past:past + ts, :] = nki_ref[...]
    ki_buf[past + ts:n_keys, :] = jnp.zeros((n_keys - past - ts, LANES), BF16)
    _dsa_body(q_ref, qi_ref, wi_ref, k_buf, v_buf, ki_buf, o_ref, *scratch,
              q_pos_first=past, s_valid=past + ts, n_sel=n_sel, packed_bisect=True, stack=KV_GROUP)


def dsa_step(q, qi, wi, cache_k, cache_v, cache_ki, k_new, v_new, ki_new, *, n_sel):
    n_batch, ts = q.shape[:2]
    past = cache_k.shape[1]
    n_keys = -(-(past + ts) // LANES) * LANES

    def spec(rows, width):
        return pl.BlockSpec((None, rows, width), lambda b: (b, 0, 0))

    return pl.pallas_call(
        functools.partial(_dsa_step_kernel, past=past, n_sel=n_sel),
        grid=(n_batch,),
        in_specs=[spec(ts, ATTN_WIDTH), spec(ts, IDX_HEADS * LANES), spec(ts, LANES),
                  pl.BlockSpec(memory_space=pl.ANY), pl.BlockSpec(memory_space=pl.ANY), spec(past, IDX_DIM),
                  spec(ts, KV_WIDTH), spec(ts, N_KV_HEADS * V_AUG), spec(ts, LANES)],
        out_specs=spec(ts, ATTN_WIDTH),
        out_shape=jax.ShapeDtypeStruct((n_batch, ts, ATTN_WIDTH), BF16),
        scratch_shapes=[pltpu.VMEM((n_keys, KV_WIDTH), BF16), pltpu.VMEM((n_keys, N_KV_HEADS * V_AUG), BF16),
                        pltpu.VMEM((n_keys, LANES), BF16),
                        pltpu.VMEM((2, 2, N_KV_HEADS, past, HEAD_DIM), F32), pltpu.SemaphoreType.DMA((2,)),
                        *_dsa_scratch(ts, n_keys)],
        compiler_params=_params(("arbitrary",)),
        name="dsa_step",
    )(q, qi, wi, cache_k, cache_v, cache_ki, k_new, v_new, ki_new)


def _merge_kernel(g_ref, a_ref, ga_ref, gb_ref, wv_ref, wg_ref, wb_ref, o_ref):
    g = g_ref[...]
    branch_a = _dot(g, wv_ref[...]) * jax.nn.sigmoid(_dot(g, wg_ref[...]))
    branch_b = _dot(a_ref[...], wb_ref[...])
    merged = jax.nn.sigmoid(ga_ref[...]) * branch_a + jax.nn.sigmoid(gb_ref[...]) * branch_b
    o_ref[...] = merged.astype(BF16)


def merge(g, attn, proj, w_val, w_gate, w_branch, *, tm=1024, tn=512):
    n_tok = g.shape[0]
    nj = D_MODEL // tn

    def wspec():
        return pl.BlockSpec((SSM_WIDTH, tn), lambda i, j: (0, j))

    return pl.pallas_call(
        _merge_kernel,
        grid=(n_tok // tm, nj),
        in_specs=[pl.BlockSpec((tm, SSM_WIDTH), lambda i, j: (i, 0)),
                  pl.BlockSpec((tm, ATTN_WIDTH), lambda i, j: (i, 0)),
                  pl.BlockSpec((tm, tn), lambda i, j: (i, COL_GA // tn + j)),
                  pl.BlockSpec((tm, tn), lambda i, j: (i, COL_GB // tn + j)),
                  wspec(), wspec(), wspec()],
        out_specs=pl.BlockSpec((tm, tn), lambda i, j: (i, j)),
        out_shape=jax.ShapeDtypeStruct((n_tok, D_MODEL), BF16),
        compiler_params=_params(("arbitrary", "arbitrary")),
        name="merge",
    )(g, attn, proj, proj, w_val, w_gate, w_branch)


ROUTER_COLS = N_EXPERT_GROUPS + N_EXPERTS
MOE_TM = 256


def _first_lane_of_max(x, lane_f):
    m = jnp.max(x, axis=1, keepdims=True)
    return m, jnp.min(jnp.where(x == m, lane_f, float(LANES)), axis=1, keepdims=True)


def _out_proj_kernel(x_ref, m_ref, wo_ref, gn_ref, wr_ref, br_ref, cin_ref,
                     h_ref, hn_ref, ri_ref, rw_ref, cnt_ref, carry_ref):
    @pl.when(pl.program_id(0) == 0)
    def _():
        carry_ref[...] = cin_ref[...]

    h = x_ref[...] + _dot(m_ref[...], wo_ref[...])
    h_ref[...] = h
    ms = jnp.mean(h * h, axis=-1, keepdims=True)
    hn = h * lax.rsqrt(ms + EPS) * gn_ref[...]
    hn_ref[...] = hn
    hh, hl = _split_bf16(hn)
    both = _dot(hh, wr_ref[...])
    lg = both[:, :LANES] + _dot(hl, wr_ref[:, 0:LANES]) + both[:, LANES:] + br_ref[...]

    tm = lg.shape[0]
    lane = lax.broadcasted_iota(I32, lg.shape, 1)
    lane_f = lane.astype(F32)
    ninf = -jnp.inf
    gl = jnp.where(lane < N_EXPERT_GROUPS, lg, ninf)
    gmax, gsel = _first_lane_of_max(gl, lane_f)
    g_w = 1.0 / jnp.sum(jnp.exp(gl - gmax), axis=1, keepdims=True)
    lo = N_EXPERT_GROUPS + EXPERTS_PER_GROUP * gsel
    el = jnp.where(lane_f >= lo, jnp.where(lane_f < lo + EXPERTS_PER_GROUP, lg, ninf), ninf)
    v1, i1 = _first_lane_of_max(el, lane_f)
    el2 = jnp.where(lane_f == i1, ninf, el)
    v2, i2 = _first_lane_of_max(el2, lane_f)
    t = jnp.exp(v2 - v1)
    s1 = 1.0 / (1.0 + t)
    w1 = s1 * g_w
    w2 = (t * s1) * g_w

    m1 = jnp.where(lane_f == i1, 1.0, 0.0)
    m2 = jnp.where(lane_f == i2, 1.0, 0.0)
    both = m1 + m2
    tri = jnp.where(lax.broadcasted_iota(I32, (tm, tm), 0) > lax.broadcasted_iota(I32, (tm, tm), 1), 1.0, 0.0)
    before = _dot(tri.astype(BF16), both.astype(BF16)) + carry_ref[...]
    r1 = jnp.sum(before * m1, axis=1, keepdims=True)
    r2 = jnp.sum(before * m2, axis=1, keepdims=True)
    carry_ref[...] = carry_ref[...] + jnp.sum(both, axis=0, keepdims=True)
    cnt_ref[...] = carry_ref[...]
    e1 = i1 - float(N_EXPERT_GROUPS)
    e2 = i2 - float(N_EXPERT_GROUPS)
    fields = jnp.where(lane == 0, e1, jnp.where(lane == 1, e2, jnp.where(lane == 2, r1, jnp.where(lane == 3, r2, 0.0))))
    ri_ref[...] = fields.T[0:SUBLANES, :].astype(I32)
    rw_ref[...] = jnp.where(lane == 0, w1, jnp.where(lane == 1, w2, 0.0))


def _router_weights(w_router_group, b_router_group, w_router_expert, b_router_expert):
    wr = jnp.concatenate([w_router_group, w_router_expert, jnp.zeros((D_MODEL, LANES - ROUTER_COLS), F32)], axis=1)
    wr_hi = wr.astype(BF16)
    wr_lo = (wr - wr_hi.astype(F32)).astype(BF16)
    br = jnp.concatenate([b_router_group, b_router_expert, jnp.zeros((LANES - ROUTER_COLS,), F32)])[None, :]
    return jnp.concatenate([wr_hi, wr_lo], axis=1), br


def out_proj(x, merged, w_out, ffn_gain, router_w, counts_in, *, tm=256):
    n_tok = x.shape[0]
    wr, br = router_w

    def row(width):
        return pl.BlockSpec((tm, width), lambda i: (i, 0))

    def const(shape):
        return pl.BlockSpec(shape, lambda i: (0, 0), pipeline_mode=pl.Buffered(1))

    return pl.pallas_call(
        _out_proj_kernel,
        grid=(n_tok // tm,),
        in_specs=[row(D_MODEL), row(D_MODEL), const((D_MODEL, D_MODEL)), const((1, D_MODEL)),
                  const((D_MODEL, 2 * LANES)), const((1, LANES)), const((1, LANES))],
        out_specs=[row(D_MODEL), row(D_MODEL), pl.BlockSpec((SUBLANES, tm), lambda i: (0, i)), row(LANES),
                   pl.BlockSpec((1, LANES), lambda i: (0, 0))],
        out_shape=[jax.ShapeDtypeStruct((n_tok, D_MODEL), F32), jax.ShapeDtypeStruct((n_tok, D_MODEL), F32),
                   jax.ShapeDtypeStruct((SUBLANES, n_tok), I32), jax.ShapeDtypeStruct((n_tok, LANES), F32),
                   jax.ShapeDtypeStruct((1, LANES), F32)],
        scratch_shapes=[pltpu.VMEM((1, LANES), F32)],
        compiler_params=_params(("arbitrary",)),
        name="out_proj",
    )(x, merged, w_out, ffn_gain[None, :], wr, br, counts_in)


def _block_layout(counts):
    padded = (counts + MOE_TM - 1) // MOE_TM * MOE_TM
    pad_end = jnp.cumsum(padded).astype(I32)
    pad_start = pad_end - padded
    n_used = pad_end[-1] // MOE_TM
    return pad_start, pad_end, n_used


def _dest_kernel(ps_ref, ri_ref, o_ref):
    ri = ri_ref[...]
    start = jnp.zeros_like(ri)
    for k in range(N_EXPERTS):
        start = jnp.where(ri == k, ps_ref[k], start)
    o_ref[...] = start + pltpu.roll(ri, SUBLANES - TOP_K, 0)


def dest_rows(route_i, pad_start):
    grid_spec = pltpu.PrefetchScalarGridSpec(
        num_scalar_prefetch=1, grid=(1,),
        in_specs=[pl.BlockSpec(route_i.shape, lambda i, ps: (0, 0))],
        out_specs=pl.BlockSpec(route_i.shape, lambda i, ps: (0, 0)))
    return pl.pallas_call(_dest_kernel, grid_spec=grid_spec, out_shape=jax.ShapeDtypeStruct(route_i.shape, I32),
                          compiler_params=_params(("arbitrary",)), name="dest_rows")(pad_start, route_i)


def _moe_rows(n_tok):
    return -(-(n_tok * TOP_K + N_EXPERTS * (MOE_TM - 1)) // MOE_TM) * MOE_TM


DISPATCH_TM = 512


def _wait_rows(src_hbm, dst, sem, n_rows):
    pltpu.make_async_copy(src_hbm.at[pl.ds(0, n_rows)], dst, sem).wait()


def _dispatch_kernel(d0_ref, d1_ref, pe_ref, cnt_ref, nu_ref, hna_ref, hnb_ref, xs_hbm, zbuf, sem, semz,
                     *, n_blocks, a_tiles):
    i = pl.program_id(0)

    def zero_block(row0):
        return pltpu.make_async_copy(zbuf, xs_hbm.at[pl.ds(pl.multiple_of(row0, MOE_TM), MOE_TM)], semz)

    @pl.when(i == 0)
    def _():
        zbuf[...] = jnp.zeros_like(zbuf)
        for start in (True, False):
            for e in range(N_EXPERTS):
                @pl.when(cnt_ref[e] > 0)
                def _():
                    cp = zero_block(pe_ref[e] - MOE_TM)
                    cp.start() if start else cp.wait()

            def tail(b, c):
                cp = zero_block(b * MOE_TM)
                cp.start() if start else cp.wait()
                return c
            lax.fori_loop(nu_ref[0], n_blocks, tail, 0)

    base = i * DISPATCH_TM

    def scatter(hn_ref):
        def body(r, c):
            src = hn_ref.at[pl.ds(r, 1)]
            pltpu.make_async_copy(src, xs_hbm.at[pl.ds(d0_ref[base + r], 1)], sem).start()
            pltpu.make_async_copy(src, xs_hbm.at[pl.ds(d1_ref[base + r], 1)], sem).start()
            return c
        lax.fori_loop(0, DISPATCH_TM, body, 0, unroll=8)
        for _ in range(TOP_K):
            pltpu.make_async_copy(hn_ref, xs_hbm.at[pl.ds(0, DISPATCH_TM)], sem).wait()

    @pl.when(i < a_tiles)
    def _():
        scatter(hna_ref)

    @pl.when(i >= a_tiles)
    def _():
        scatter(hnb_ref)


def dispatch(hn_a, hn_b, dest0, dest1, pad_end, counts, n_used):
    a_tiles, b_tiles = hn_a.shape[0] // DISPATCH_TM, hn_b.shape[0] // DISPATCH_TM
    rows = _moe_rows(hn_a.shape[0] + hn_b.shape[0])
    grid_spec = pltpu.PrefetchScalarGridSpec(
        num_scalar_prefetch=5,
        grid=(a_tiles + b_tiles,),
        in_specs=[pl.BlockSpec((DISPATCH_TM, D_MODEL), lambda i, *_: (jnp.minimum(i, a_tiles - 1), 0)),
                  pl.BlockSpec((DISPATCH_TM, D_MODEL), lambda i, *_: (jnp.maximum(i - a_tiles, 0), 0))],
        out_specs=pl.BlockSpec(memory_space=pl.ANY),
        scratch_shapes=[pltpu.VMEM((MOE_TM, D_MODEL), F32), pltpu.SemaphoreType.DMA(()), pltpu.SemaphoreType.DMA(())],
    )
    return pl.pallas_call(
        functools.partial(_dispatch_kernel, n_blocks=rows // MOE_TM, a_tiles=a_tiles),
        grid_spec=grid_spec,
        out_shape=jax.ShapeDtypeStruct((rows, D_MODEL), F32),
        compiler_params=_params(("arbitrary",)),
        name="dispatch",
    )(dest0, dest1, pad_end, counts, n_used, hn_a, hn_b)


MOE_UNITS = 8
MOE_UG = D_MODEL // MOE_UNITS
MOE_UD = EXPERT_FF // MOE_UNITS


def _moe_kernel(blk_e_ref, nu_ref, nxt_ref, upb_ref, xs_ref, wg_hbm, wu_hbm, wd_hbm, ys_ref,
                wg_bf, wu_bf, wd_bf, stg_g, stg_u, stg_d, sem, st_ref):
    i = pl.program_id(0)
    cur_slot, pos, cur_e = 0, 1, 2

    def unit_copies(e, unit, s):
        g_rows = pl.ds(pl.multiple_of(unit * MOE_UG, MOE_UG), MOE_UG)
        d_rows = pl.ds(pl.multiple_of(unit * MOE_UD, MOE_UD), MOE_UD)
        return (pltpu.make_async_copy(wg_hbm.at[e, g_rows, :], stg_g.at[s], sem.at[s]),
                pltpu.make_async_copy(wu_hbm.at[e, g_rows, :], stg_u.at[s], sem.at[s]),
                pltpu.make_async_copy(wd_hbm.at[e, d_rows, :], stg_d.at[s], sem.at[s]))

    def start_unit(e, unit):
        for cp in unit_copies(e, unit, unit % 2):
            cp.start()

    def begin_load(e):
        st_ref[pos] = 0
        start_unit(e, 0)
        start_unit(e, 1)

    def advance(e, slot, n):
        def body(_, c):
            unit = st_ref[pos]

            @pl.when(unit < MOE_UNITS)
            def _():
                s = unit % 2
                for cp in unit_copies(e, unit, s):
                    cp.wait()
                g_rows = pl.ds(pl.multiple_of(unit * MOE_UG, MOE_UG), MOE_UG)
                d_rows = pl.ds(pl.multiple_of(unit * MOE_UD, MOE_UD), MOE_UD)
                wg_bf[slot, g_rows, :] = stg_g[s].astype(BF16)
                wu_bf[slot, g_rows, :] = stg_u[s].astype(BF16)
                wd_bf[slot, d_rows, :] = stg_d[s].astype(BF16)

                @pl.when(unit + 2 < MOE_UNITS)
                def _():
                    start_unit(e, unit + 2)
                st_ref[pos] = unit + 1
            return c
        lax.fori_loop(0, n, body, 0)

    def load_next(nxt):
        @pl.when(nxt >= 0)
        def _():
            begin_load(nxt)

        @pl.when(nxt < 0)
        def _():
            st_ref[pos] = MOE_UNITS

    @pl.when(i < nu_ref[0])
    def _():
        e = blk_e_ref[i]
        nxt = nxt_ref[i]

        @pl.when(i == 0)
        def _():
            st_ref[cur_slot] = 0
            st_ref[cur_e] = e
            begin_load(e)
            advance(e, 0, MOE_UNITS)
            load_next(nxt)

        @pl.when(jnp.logical_and(i > 0, e != st_ref[cur_e]))
        def _():
            slot = 1 - st_ref[cur_slot]
            advance(e, slot, MOE_UNITS)
            st_ref[cur_slot] = slot
            st_ref[cur_e] = e
            load_next(nxt)

        slot = st_ref[cur_slot]
        x = xs_ref[...].astype(BF16)
        hg = _dot(x, wg_bf[slot])
        hu = _dot(x, wu_bf[slot])
        hmid = (jax.nn.silu(hg) * hu).astype(BF16)
        ys_ref[...] = _dot(hmid, wd_bf[slot])

        @pl.when(nxt >= 0)
        def _():
            advance(nxt, 1 - slot, upb_ref[i])

    @pl.when(i >= nu_ref[0])
    def _():
        ys_ref[...] = jnp.zeros_like(ys_ref)


def moe(xs, blk_e, n_used, nxt_e, units_per_block, w_gate, w_up, w_down):
    rows = xs.shape[0]
    grid_spec = pltpu.PrefetchScalarGridSpec(
        num_scalar_prefetch=4,
        grid=(rows // MOE_TM,),
        in_specs=[pl.BlockSpec((MOE_TM, D_MODEL), lambda i, be, nu, nx, ub: (jnp.minimum(i, nu[0] - 1), 0)),
                  pl.BlockSpec(memory_space=pl.ANY), pl.BlockSpec(memory_space=pl.ANY), pl.BlockSpec(memory_space=pl.ANY)],
        out_specs=pl.BlockSpec((MOE_TM, D_MODEL), lambda i, be, nu, nx, ub: (i, 0)),
        scratch_shapes=[pltpu.VMEM((2, D_MODEL, EXPERT_FF), BF16), pltpu.VMEM((2, D_MODEL, EXPERT_FF), BF16),
                        pltpu.VMEM((2, EXPERT_FF, D_MODEL), BF16),
                        pltpu.VMEM((2, MOE_UG, EXPERT_FF), F32), pltpu.VMEM((2, MOE_UG, EXPERT_FF), F32),
                        pltpu.VMEM((2, MOE_UD, D_MODEL), F32),
                        pltpu.SemaphoreType.DMA((2,)), pltpu.SMEM((3,), I32)],
    )
    return pl.pallas_call(
        _moe_kernel,
        grid_spec=grid_spec,
        out_shape=jax.ShapeDtypeStruct((rows, D_MODEL), F32),
        compiler_params=_params(("arbitrary",)),
        name="moe",
    )(blk_e, n_used, nxt_e, units_per_block, xs, w_gate, w_up, w_down)


def _combine_kernel(r0_ref, r1_ref, ys_hbm, h_ref, w_ref, o_ref, buf, sem, *, tm, tok0):
    i = pl.program_id(0)

    def issue(block, slot):
        base = tok0 + block * tm

        def body(r, carry):
            for k, idx_ref in enumerate((r0_ref, r1_ref)):
                pltpu.make_async_copy(ys_hbm.at[pl.ds(idx_ref[base + r], 1)], buf.at[slot, k, pl.ds(r, 1)],
                                      sem.at[slot]).start()
            return carry
        lax.fori_loop(0, tm, body, 0, unroll=8)

    @pl.when(i == 0)
    def _():
        issue(0, 0)

    @pl.when(i + 1 < pl.num_programs(0))
    def _():
        issue(i + 1, (i + 1) % 2)

    slot = i % 2
    _wait_rows(ys_hbm, buf.at[slot, 0], sem.at[slot], tm)
    _wait_rows(ys_hbm, buf.at[slot, 1], sem.at[slot], tm)
    w = w_ref[...]
    o_ref[...] = h_ref[...] + (buf[slot, 0] * w[:, 0:1] + buf[slot, 1] * w[:, 1:2])


def combine(ys, h, route_w, rows0, rows1, *, tok0, tm=256):
    n_tok = h.shape[0]
    grid_spec = pltpu.PrefetchScalarGridSpec(
        num_scalar_prefetch=2,
        grid=(n_tok // tm,),
        in_specs=[pl.BlockSpec(memory_space=pl.ANY),
                  pl.BlockSpec((tm, D_MODEL), lambda i, a, b: (i, 0)),
                  pl.BlockSpec((tm, LANES), lambda i, a, b: (i, 0))],
        out_specs=pl.BlockSpec((tm, D_MODEL), lambda i, a, b: (i, 0)),
        scratch_shapes=[pltpu.VMEM((2, 2, tm, D_MODEL), F32), pltpu.SemaphoreType.DMA((2,))],
    )
    return pl.pallas_call(
        functools.partial(_combine_kernel, tm=tm, tok0=tok0),
        grid_spec=grid_spec,
        out_shape=jax.ShapeDtypeStruct((n_tok, D_MODEL), F32),
        compiler_params=_params(("arbitrary",)),
        name="combine",
    )(rows0, rows1, ys, h, route_w)


IN_SIZES = (SSM_WIDTH, ATTN_WIDTH, KV_WIDTH, KV_WIDTH, IDX_HEADS * IDX_DIM, IDX_DIM, IDX_HEADS, D_MODEL, D_MODEL)
IN_COLS = sum(IN_SIZES)
SRC_U, SRC_Q, SRC_K, SRC_V, SRC_QI, SRC_KI, SRC_WI, SRC_GA, SRC_GB = (int(c) for c in np.cumsum((0,) + IN_SIZES[:-1]))


def _regroup_kernel(w_ref, o_ref):
    runs = ((COL_U, SRC_U, SSM_WIDTH + ATTN_WIDTH), (COL_GA, SRC_GA, D_MODEL), (COL_GB, SRC_GB, D_MODEL),
            (COL_K, SRC_K, 2 * KV_WIDTH + IDX_HEADS * IDX_DIM), (COL_KIWI, SRC_KI, IDX_DIM + IDX_HEADS))
    for dst, src, n in runs:
        o_ref[:, dst:dst + n] = w_ref[:, src:src + n].astype(BF16)
    tail = COL_KIWI + IDX_DIM + IDX_HEADS
    o_ref[:, tail:PROJ_COLS] = jnp.zeros((o_ref.shape[0], PROJ_COLS - tail), BF16)


def _regroup_w_in(w_in, *, tr=256):
    return pl.pallas_call(
        _regroup_kernel,
        grid=(D_MODEL // tr,),
        in_specs=[pl.BlockSpec((tr, IN_COLS), lambda i: (i, 0))],
        out_specs=pl.BlockSpec((tr, PROJ_COLS), lambda i: (i, 0)),
        out_shape=jax.ShapeDtypeStruct((D_MODEL, PROJ_COLS), BF16),
        compiler_params=_params(("arbitrary",)),
        name="regroup_w_in",
    )(w_in)


def _layer(x_p, x_s, cache_k, cache_v, cache_ki, h0_re, h0_im, p):
    bp, tp, _ = x_p.shape
    bs, ts, _ = x_s.shape
    past = cache_k.shape[1]
    n_p, n_s = bp * tp, bs * ts
    n_tok = n_p + n_s

    w_in = _regroup_w_in(p['w_in'])
    ssm_w = _ssm_weights(p['ssm_A_re'], p['ssm_A_im'], p['ssm_log_dt'], p['ssm_B_re'], p['ssm_B_im'],
                         p['ssm_C_re'], p['ssm_C_im'])
    glu_w = (p['w_glu_val'].astype(BF16), p['w_glu_gate'].astype(BF16), p['w_attn_branch'].astype(BF16))
    w_out = p['w_out'].astype(BF16)
    router_w = _router_weights(p['w_router_group'], p['b_router_group'], p['w_router_expert'], p['b_router_expert'])
    seq_tiles = tp // QK_TM

    def front(x, table_pos, table_block):
        proj = in_proj(x, p['norm_mix_g'][None, :], w_in)
        return proj, qk_post(proj, table_pos, table_block, p['q_norm_g'], p['k_norm_g'], p['idx_k_norm_g'])

    def seqs(a, b, t):
        return a.reshape(b, t, a.shape[-1])

    xp = x_p.reshape(n_p, D_MODEL)
    proj_p, (q_b, kf_p, k_b, vf_p, v_b, qi_b, kif_p, ki_b, wi) = front(
        xp, jnp.arange(tp, dtype=I32), lambda i: i % seq_tiles)
    g_p, sre_p, sim_p = ssm(proj_p, ssm_w, p['ssm_D'], jnp.zeros((bp, SSM_LB, 2, SSM_SB), F32),
                            n_batch=bp, seq=tp, row0=0)
    bq = 128
    n_buckets = min(16, tp // bq)
    per = tp // bq // n_buckets
    qp, qip, wip = seqs(q_b, bp, tp), seqs(qi_b, bp, tp), seqs(wi, bp, tp)
    kp, vp, kip = seqs(k_b, bp, tp), seqs(v_b, bp, tp), seqs(ki_b, bp, tp)
    attn_p = jnp.concatenate(
        [dsa(qp, qip, wip, kp, vp, kip, bq=bq, q_blk0=n * per, n_qblk=per, n_keys=(n + 1) * per * bq,
             n_sel=min(IDX_TOPK, tp // 4), packed_bisect=False, stack=1)
         for n in range(n_buckets)], axis=1).reshape(n_p, ATTN_WIDTH)
    merged_p = merge(g_p, attn_p, proj_p, *glu_w)
    h_p, hn_p, ri_p, rw_p, cnt_p = out_proj(xp, merged_p, w_out, p['norm_ffn_g'], router_w, jnp.zeros((1, LANES), F32))

    xs_ = x_s.reshape(n_s, D_MODEL)
    proj_s, (q_b, kf_s, k_b, vf_s, v_b, qi_b, kif_s, ki_b, wi) = front(
        xs_, jnp.tile(past + jnp.arange(ts, dtype=I32), QK_TM // ts), lambda i: 0)
    h0 = jnp.stack([h0_re.reshape(bs, SSM_LB, SSM_SB), h0_im.reshape(bs, SSM_LB, SSM_SB)]).transpose(2, 0, 1, 3)
    g_s, sre_s, sim_s = ssm_step(proj_s, ssm_w, p['ssm_D'], h0, n_batch=bs, seq=ts, row0=0)
    attn_s = dsa_step(seqs(q_b, bs, ts), seqs(qi_b, bs, ts), seqs(wi, bs, ts),
                      cache_k, cache_v, cache_ki,
                      seqs(k_b, bs, ts), seqs(v_b, bs, ts), seqs(ki_b, bs, ts),
                      n_sel=min(IDX_TOPK, (past + ts) // 4)).reshape(n_s, ATTN_WIDTH)
    merged_s = merge(g_s, attn_s, proj_s, *glu_w)
    h_s, hn_s, ri_s, rw_s, cnt = out_proj(xs_, merged_s, w_out, p['norm_ffn_g'], router_w, cnt_p)

    counts = cnt[0, N_EXPERT_GROUPS:ROUTER_COLS].astype(I32)
    pad_start, pad_end, n_used = _block_layout(counts)
    dest = dest_rows(jnp.concatenate([ri_p, ri_s], axis=1), pad_start)
    dest0, dest1 = dest[0], dest[1]
    n_blocks = _moe_rows(n_tok) // MOE_TM
    blk = jnp.minimum(jnp.arange(n_blocks, dtype=I32), n_used - 1)
    blk_e = jnp.minimum(jnp.sum((pad_end[None, :] <= (blk * MOE_TM)[:, None]).astype(I32), axis=1), N_EXPERTS - 1)
    after = pad_end[blk_e] // MOE_TM
    nxt_e = jnp.where(after < n_used, blk_e[jnp.minimum(after, n_blocks - 1)], -1).astype(I32)
    blocks_of_e = jnp.maximum((pad_end - pad_start)[blk_e] // MOE_TM, 1)
    units_per_block = ((MOE_UNITS + blocks_of_e - 1) // blocks_of_e).astype(I32)
    n_used = n_used.reshape(1)

    xs = dispatch(hn_p, hn_s, dest0, dest1, pad_end, counts, n_used)
    ys = moe(xs, blk_e, n_used, nxt_e, units_per_block, p['w_exp_gate'], p['w_exp_up'], p['w_exp_down'])
    y_p = combine(ys, h_p, rw_p, dest0, dest1, tok0=0).reshape(bp, tp, D_MODEL)
    y_s = combine(ys, h_s, rw_s, dest0, dest1, tok0=n_p).reshape(bs, ts, D_MODEL)

    def heads(a, b, t):
        return a.reshape(b, t, N_KV_HEADS, HEAD_DIM)

    new_p = (heads(kf_p, bp, tp), heads(vf_p, bp, tp), kif_p.reshape(bp, tp, IDX_DIM), sre_p, sim_p)
    new_s = (heads(kf_s, bs, ts), heads(vf_s, bs, ts), kif_s.reshape(bs, ts, IDX_DIM), sre_s, sim_s)
    return y_p, y_s, new_p, new_s


def kernel(x_prompt, x_sample, cache_k, cache_v, cache_idx_k, state_ssm_re, state_ssm_im, norm_mix_g, w_in, q_norm_g, k_norm_g, idx_k_norm_g, ssm_A_re, ssm_A_im, ssm_log_dt, ssm_B_re, ssm_B_im, ssm_C_re, ssm_C_im, ssm_D, w_glu_val, w_glu_gate, w_attn_branch, w_out, norm_ffn_g, w_router_group, b_router_group, w_router_expert, b_router_expert, w_exp_gate, w_exp_up, w_exp_down):
    depth = w_in.shape[0]
    assert depth == 1, "prompt and sample tokens are batched through one layer"
    names = ('norm_mix_g', 'w_in', 'q_norm_g', 'k_norm_g', 'idx_k_norm_g', 'ssm_A_re', 'ssm_A_im', 'ssm_log_dt',
             'ssm_B_re', 'ssm_B_im', 'ssm_C_re', 'ssm_C_im', 'ssm_D', 'w_glu_val', 'w_glu_gate', 'w_attn_branch',
             'w_out', 'norm_ffn_g', 'w_router_group', 'b_router_group', 'w_router_expert', 'b_router_expert',
             'w_exp_gate', 'w_exp_up', 'w_exp_down')
    vals = (norm_mix_g, w_in, q_norm_g, k_norm_g, idx_k_norm_g, ssm_A_re, ssm_A_im, ssm_log_dt, ssm_B_re, ssm_B_im,
            ssm_C_re, ssm_C_im, ssm_D, w_glu_val, w_glu_gate, w_attn_branch, w_out, norm_ffn_g, w_router_group,
            b_router_group, w_router_expert, b_router_expert, w_exp_gate, w_exp_up, w_exp_down)
    p = {n: v[0] for n, v in zip(names, vals)}
    y_p, y_s, new_p, new_s = _layer(x_prompt, x_sample, cache_k[0], cache_v[0], cache_idx_k[0],
                                    state_ssm_re[0], state_ssm_im[0], p)
    st_p = tuple(a[None] for a in new_p)
    st_s = tuple(a[None] for a in new_s)
    return (y_p, y_s) + st_p + st_s
```

```python
import functools

import numpy as np
import jax
import jax.numpy as jnp
from jax import lax
from jax.experimental import pallas as pl
from jax.experimental.pallas import tpu as pltpu

F32 = jnp.float32
BF16 = jnp.bfloat16
I32 = jnp.int32

D_MODEL = 2048
CHUNK = 64
SSM_WIDTH = 1024
SSM_GROUP = 16
SSM_GROUPS = 64
SSM_STATE = 64
ATTN_WIDTH = 1024
HEAD_DIM = 128
N_HEADS = 8
N_KV_HEADS = 2
KV_GROUP = 4
IDX_HEADS = 8
IDX_DIM = 64
IDX_TOPK = 256
ROPE_THETA = 500000.0
N_EXPERT_GROUPS = 4
EXPERTS_PER_GROUP = 8
N_EXPERTS = 32
TOP_K = 2
EXPERT_FF = 1024
EPS = 1e-6

LANES = 128
SUBLANES = 8
VMEM_LIMIT = 56 * 1024 * 1024

COL_U, COL_Q, COL_GA, COL_GB, COL_K, COL_V, COL_QI, COL_KIWI = 0, 1024, 2048, 4096, 6144, 6400, 6656, 7168
PROJ_COLS = 7296
PROJ_TN = 2432
KV_WIDTH = N_KV_HEADS * HEAD_DIM

SSM_LB = SSM_WIDTH // LANES
SSM_SB = 8 * SSM_STATE

INT_MIN = np.int32(-2 ** 31)
KEY_NEG_INF = np.int32(np.array([0xFF800000], np.uint32).view(np.int32)[0] ^ 0x7FFFFFFF)


def _params(sem, vmem=VMEM_LIMIT):
    return pltpu.CompilerParams(dimension_semantics=sem, vmem_limit_bytes=vmem)


def _dot(a, b):
    return jnp.dot(a, b, preferred_element_type=F32)


def _dot_nt(a, b):
    return lax.dot_general(a, b, (((1,), (1,)), ((), ())), preferred_element_type=F32)


def _split_bf16(x):
    hi = x.astype(BF16)
    lo = (x - hi.astype(F32)).astype(BF16)
    return hi, lo


def _in_proj_kernel(x_ref, g_ref, w_ref, o_ref, xn_ref):
    @pl.when(pl.program_id(1) == 0)
    def _():
        x = x_ref[...]
        ms = jnp.mean(x * x, axis=-1, keepdims=True)
        xn_ref[...] = (x * lax.rsqrt(ms + EPS) * g_ref[...]).astype(BF16)

    o_ref[...] = _dot(xn_ref[...], w_ref[...])


def in_proj(x, gain, w_bf16, *, tm=512):
    n_tok = x.shape[0]
    return pl.pallas_call(
        _in_proj_kernel,
        grid=(n_tok // tm, PROJ_COLS // PROJ_TN),
        in_specs=[pl.BlockSpec((tm, D_MODEL), lambda i, j: (i, 0)),
                  pl.BlockSpec((1, D_MODEL), lambda i, j: (0, 0)),
                  pl.BlockSpec((D_MODEL, PROJ_TN), lambda i, j: (0, j))],
        out_specs=pl.BlockSpec((tm, PROJ_TN), lambda i, j: (i, j)),
        out_shape=jax.ShapeDtypeStruct((n_tok, PROJ_COLS), F32),
        scratch_shapes=[pltpu.VMEM((tm, D_MODEL), BF16)],
        compiler_params=_params(("arbitrary", "arbitrary")),
        name="in_proj",
    )(x, gain, w_bf16)


def _rope(x, c, s_lo, s_hi, half):
    n = x.shape[-1]
    return x * c + pltpu.roll(x, n - half, 1) * s_lo + pltpu.roll(x, half, 1) * s_hi


def _head_norm(x, g):
    ms = jnp.mean(x * x, axis=-1, keepdims=True)
    return x * lax.rsqrt(ms + EPS) * g


V_AUG = 2 * HEAD_DIM


def _store_v_aug(dst_ref, row0, v_heads):
    n = v_heads[0].shape[0]
    one_col = jnp.where(lax.broadcasted_iota(I32, (n, HEAD_DIM), 1) == 0, 1.0, 0.0).astype(BF16)
    for h, v in enumerate(v_heads):
        dst_ref[row0:row0 + n, h * V_AUG:h * V_AUG + HEAD_DIM] = v.astype(BF16)
        dst_ref[row0:row0 + n, h * V_AUG + HEAD_DIM:(h + 1) * V_AUG] = one_col


def _qk_post_kernel(q_ref, k_ref, v_ref, qi_ref, kw_ref, c128_ref, sl128_ref, sh128_ref,
                    c64_ref, sl64_ref, sh64_ref, qg_ref, kg_ref, ig_ref,
                    qo_ref, kf_ref, kb_ref, vf_ref, vb_ref, qio_ref, kif_ref, kib_ref, wo_ref):
    c128, sl128, sh128 = c128_ref[...], sl128_ref[...], sh128_ref[...]
    c64, sl64, sh64 = c64_ref[...], sl64_ref[...], sh64_ref[...]
    half128 = HEAD_DIM // 8
    half64 = IDX_DIM // 8
    for h in range(N_HEADS):
        sl = slice(h * LANES, (h + 1) * LANES)
        qo_ref[:, sl] = _rope(_head_norm(q_ref[:, sl], qg_ref[...]), c128, sl128, sh128, half128).astype(BF16)
    for h in range(N_KV_HEADS):
        sl = slice(h * LANES, (h + 1) * LANES)
        kk = _rope(_head_norm(k_ref[:, sl], kg_ref[...]), c128, sl128, sh128, half128)
        kf_ref[:, sl] = kk
        kb_ref[:, sl] = kk.astype(BF16)
    v = v_ref[...]
    vf_ref[...] = v
    _store_v_aug(vb_ref, 0, [v[:, h * HEAD_DIM:(h + 1) * HEAD_DIM] for h in range(N_KV_HEADS)])
    lane = lax.broadcasted_iota(I32, c64.shape, 1)
    low = lane < IDX_DIM
    for p in range(IDX_HEADS // 2):
        x = _rope(qi_ref[:, p * LANES:(p + 1) * LANES], c64, sl64, sh64, half64)
        qio_ref[:, (2 * p) * LANES:(2 * p + 1) * LANES] = jnp.where(low, x, 0.0).astype(BF16)
        qio_ref[:, (2 * p + 1) * LANES:(2 * p + 2) * LANES] = jnp.where(low, pltpu.roll(x, IDX_DIM, 1), 0.0).astype(BF16)
    kw = kw_ref[...]
    ms = jnp.sum(jnp.where(low, kw * kw, 0.0), axis=-1, keepdims=True) * (1.0 / IDX_DIM)
    ki = _rope(kw * lax.rsqrt(ms + EPS) * ig_ref[...], c64, sl64, sh64, half64)
    kif_ref[...] = ki[:, :IDX_DIM]
    kib_ref[...] = jnp.where(low, ki, 0.0).astype(BF16)
    wo_ref[...] = (pltpu.roll(kw, IDX_DIM, 1) * IDX_HEADS ** -0.5) * IDX_DIM ** -0.5


def _rope_tables(pos, head_dim):
    r = head_dim // 4
    half = r // 2
    inv = ROPE_THETA ** (-jnp.arange(half, dtype=F32) * 2.0 / r)
    ang = pos.astype(F32)[:, None] * inv[None, :]
    cos, sin = jnp.cos(ang), jnp.sin(ang)
    n = pos.shape[0]
    zh = jnp.zeros((n, half), F32)
    rest = head_dim - r
    c = jnp.concatenate([cos, cos, jnp.ones((n, rest), F32)], axis=-1)
    s_lo = jnp.concatenate([-sin, zh, jnp.zeros((n, rest), F32)], axis=-1)
    s_hi = jnp.concatenate([zh, sin, jnp.zeros((n, rest), F32)], axis=-1)
    rep = LANES // head_dim
    return tuple(jnp.tile(t, (1, rep)) for t in (c, s_lo, s_hi))


QK_TM = 512


def qk_post(proj, table_pos, table_block, q_gain, k_gain, ik_gain):
    tm = QK_TM
    n_tok = proj.shape[0]
    t128 = _rope_tables(table_pos, HEAD_DIM)
    t64 = _rope_tables(table_pos, IDX_DIM)
    ik_gain128 = jnp.concatenate([ik_gain, jnp.zeros((LANES - IDX_DIM,), F32)])[None, :]

    def col(width, start):
        return pl.BlockSpec((tm, width), lambda i: (i, start // width))

    def row(width):
        return pl.BlockSpec((tm, width), lambda i: (i, 0))

    table = pl.BlockSpec((tm, LANES), lambda i: (table_block(i), 0))
    gain = pl.BlockSpec((1, LANES), lambda i: (0, 0))
    return pl.pallas_call(
        _qk_post_kernel,
        grid=(n_tok // tm,),
        in_specs=[col(ATTN_WIDTH, COL_Q), col(KV_WIDTH, COL_K), col(KV_WIDTH, COL_V), col(IDX_HEADS * IDX_DIM, COL_QI),
                  col(LANES, COL_KIWI)] + [table] * 6 + [gain] * 3,
        out_specs=[row(ATTN_WIDTH), row(KV_WIDTH), row(KV_WIDTH), row(KV_WIDTH), row(N_KV_HEADS * V_AUG), row(IDX_HEADS * LANES),
                   row(IDX_DIM), row(LANES), row(LANES)],
        out_shape=[jax.ShapeDtypeStruct((n_tok, ATTN_WIDTH), BF16),
                   jax.ShapeDtypeStruct((n_tok, KV_WIDTH), F32), jax.ShapeDtypeStruct((n_tok, KV_WIDTH), BF16),
                   jax.ShapeDtypeStruct((n_tok, KV_WIDTH), F32), jax.ShapeDtypeStruct((n_tok, N_KV_HEADS * V_AUG), BF16),
                   jax.ShapeDtypeStruct((n_tok, IDX_HEADS * LANES), BF16),
                   jax.ShapeDtypeStruct((n_tok, IDX_DIM), F32), jax.ShapeDtypeStruct((n_tok, LANES), BF16),
                   jax.ShapeDtypeStruct((n_tok, LANES), F32)],
        compiler_params=_params(("arbitrary",)),
        name="qk_post",
    )(proj, proj, proj, proj, proj, *t128, *t64, q_gain[None, :], k_gain[None, :], ik_gain128)


def _gelu_tanh(x):
    return 0.5 * x * (1.0 + jnp.tanh(np.float32(np.sqrt(2.0 / np.pi)) * (x + 0.044715 * (x * x * x))))


SSM_LT = SSM_SB // LANES
SSM_SEG = 128


def _ssm_kernel(u_ref, wb_ref, wc_ref, pw_ref, d_ref, h0_ref, g_ref, sre_ref, sim_ref,
                er_ref, ei_ref, car_ref, up_ref, yp_ref):
    c = pl.program_id(2)

    @pl.when(c == 0)
    def _():
        car_ref[...] = h0_ref[...]

    for j in range(SSM_SEG):
        up_ref[j * SUBLANES:(j + 1) * SUBLANES, :] = u_ref[pl.ds(j, SUBLANES, stride=SSM_SEG), :]
    e = _dot(up_ref[...].astype(BF16), wb_ref[...])
    tiles = [slice(lt * LANES, (lt + 1) * LANES) for lt in range(SSM_LT)]
    for lt, sl in enumerate(tiles):
        er_ref[lt] = e[:, sl]
        ei_ref[lt] = e[:, SSM_SB + lt * LANES:SSM_SB + (lt + 1) * LANES]

    def cmul_add(ar, ai, br, bi, cr, ci):
        return ar * br - ai * bi + cr, ar * bi + ai * br + ci

    lb = [(pw_ref[0, 0:1, sl], pw_ref[1, 0:1, sl]) for sl in tiles]
    zero = jnp.zeros((SUBLANES, LANES), F32)
    st = [(zero, zero)] * SSM_LT
    for j in range(SSM_SEG):
        rows = slice(j * SUBLANES, (j + 1) * SUBLANES)
        for lt in range(SSM_LT):
            st[lt] = cmul_add(*lb[lt], *st[lt], er_ref[lt, rows, :], ei_ref[lt, rows, :])
            er_ref[lt, rows, :] = st[lt][0]
            ei_ref[lt, rows, :] = st[lt][1]

    enter = []
    for lt, sl in enumerate(tiles):
        seg_r, seg_i = pw_ref[0, SSM_SEG - 1:SSM_SEG, sl], pw_ref[1, SSM_SEG - 1:SSM_SEG, sl]
        cr, ci = car_ref[0:1, sl], car_ref[1:2, sl]
        rows_r, rows_i = [], []
        for r in range(SUBLANES):
            rows_r.append(cr)
            rows_i.append(ci)
            cr, ci = cmul_add(seg_r, seg_i, cr, ci, st[lt][0][r:r + 1], st[lt][1][r:r + 1])
        car_ref[0:1, sl] = cr
        car_ref[1:2, sl] = ci
        enter.append((jnp.concatenate(rows_r, axis=0), jnp.concatenate(rows_i, axis=0)))

    for j in range(SSM_SEG):
        rows = slice(j * SUBLANES, (j + 1) * SUBLANES)
        for lt, sl in enumerate(tiles):
            xr, xi = cmul_add(pw_ref[0, j:j + 1, sl], pw_ref[1, j:j + 1, sl], *enter[lt],
                              er_ref[lt, rows, :], ei_ref[lt, rows, :])
            er_ref[lt, rows, :] = xr
            ei_ref[lt, rows, :] = xi

    y = None
    for lt, sl in enumerate(tiles):
        t = _dot(er_ref[lt].astype(BF16), wc_ref[0, sl, :]) - _dot(ei_ref[lt].astype(BF16), wc_ref[1, sl, :])
        y = t if y is None else y + t
    yp_ref[...] = y
    out_rows = 2 * SUBLANES
    for t0 in range(0, SUBLANES * SSM_SEG, out_rows):
        r, j0 = divmod(t0, SSM_SEG)
        rows = slice(t0, t0 + out_rows)
        yt = yp_ref[pl.ds(j0 * SUBLANES + r, out_rows, stride=SUBLANES), :] + d_ref[...] * u_ref[rows, :]
        g_ref[rows, :] = _gelu_tanh(yt).astype(BF16)

    @pl.when(c == pl.num_programs(2) - 1)
    def _():
        sre_ref[...] = car_ref[0:1, :]
        sim_ref[...] = car_ref[1:2, :]


def _ssm_weights(a_re, a_im, log_dt, b_re, b_im, c_re, c_im):
    lam_re, lam_im = a_re, a_im
    dt = jnp.exp(log_dt)[:, None]
    mag = jnp.exp(lam_re * dt)
    lb_re, lb_im = mag * jnp.cos(lam_im * dt), mag * jnp.sin(lam_im * dt)
    den = lam_re * lam_re + lam_im * lam_im
    num_re = lb_re - 1.0
    z_re = (num_re * lam_re + lb_im * lam_im) / den
    z_im = (lb_im * lam_re - num_re * lam_im) / den
    zb_re = z_re[:, :, None] * b_re - z_im[:, :, None] * b_im
    zb_im = z_re[:, :, None] * b_im + z_im[:, :, None] * b_re
    eye = jnp.eye(8, dtype=F32)

    def blockdiag_in(w):
        return jnp.einsum('jgph,gk->jghkp', w.reshape(SSM_LB, 8, SSM_STATE, SSM_GROUP), eye).reshape(SSM_LB, LANES, SSM_SB)

    def blockdiag_out(w):
        return jnp.einsum('jghp,gk->jkpgh', w.reshape(SSM_LB, 8, SSM_GROUP, SSM_STATE), eye).reshape(SSM_LB, SSM_SB, LANES)

    wb = jnp.concatenate([blockdiag_in(zb_re), blockdiag_in(zb_im)], axis=-1).astype(BF16)
    wc = jnp.stack([blockdiag_out(c_re), blockdiag_out(c_im)], axis=1).astype(BF16)

    pr, pi_ = lb_re.reshape(SSM_LB, 1, SSM_SB), lb_im.reshape(SSM_LB, 1, SSM_SB)
    while pr.shape[1] < SSM_SEG:
        tr, ti = pr[:, -1:], pi_[:, -1:]
        pr, pi_ = (jnp.concatenate([pr, pr * tr - pi_ * ti], axis=1), jnp.concatenate([pi_, pr * ti + pi_ * tr], axis=1))
    pw = jnp.stack([pr, pi_], axis=1)
    return wb, wc, pw


def ssm(proj, ssm_w, d_skip, h0, *, n_batch, seq, row0):
    wb, wc, pw = ssm_w
    tc = SUBLANES * SSM_SEG
    n_chunks = seq // tc
    blk0 = row0 // tc
    n_tok = n_batch * seq
    state_shape = jax.ShapeDtypeStruct((n_batch, SSM_LB, 1, SSM_SB), F32)
    state_spec = pl.BlockSpec((None, None, 1, SSM_SB), lambda b, j, c: (b, j, 0, 0))
    g, s_re, s_im = pl.pallas_call(
        _ssm_kernel,
        grid=(n_batch, SSM_LB, n_chunks),
        in_specs=[pl.BlockSpec((tc, LANES), lambda b, j, c: (blk0 + b * n_chunks + c, j)),
                  pl.BlockSpec((None, LANES, 2 * SSM_SB), lambda b, j, c: (j, 0, 0)),
                  pl.BlockSpec((None, 2, SSM_SB, LANES), lambda b, j, c: (j, 0, 0, 0)),
                  pl.BlockSpec((None, 2, SSM_SEG, SSM_SB), lambda b, j, c: (j, 0, 0, 0)),
                  pl.BlockSpec((1, LANES), lambda b, j, c: (0, j)),
                  pl.BlockSpec((None, None, 2, SSM_SB), lambda b, j, c: (b, j, 0, 0))],
        out_specs=[pl.BlockSpec((tc, LANES), lambda b, j, c: (b * n_chunks + c, j)), state_spec, state_spec],
        out_shape=[jax.ShapeDtypeStruct((n_tok, SSM_WIDTH), BF16), state_shape, state_shape],
        scratch_shapes=[pltpu.VMEM((SSM_LT, tc, LANES), F32), pltpu.VMEM((SSM_LT, tc, LANES), F32),
                        pltpu.VMEM((2, SSM_SB), F32), pltpu.VMEM((tc, LANES), F32), pltpu.VMEM((tc, LANES), F32)],
        compiler_params=_params(("arbitrary", "arbitrary", "arbitrary")),
        name="ssm",
    )(proj, wb, wc, pw, d_skip[None, :], h0)
    return g, s_re.reshape(n_batch, SSM_GROUPS, SSM_STATE), s_im.reshape(n_batch, SSM_GROUPS, SSM_STATE)


def _ssm_step_kernel(u_ref, wb_ref, wc_ref, pw_ref, d_ref, h0_ref, g_ref, sre_ref, sim_ref, er_ref, ei_ref, *, seq):
    n_seq = h0_ref.shape[1]
    u = u_ref[...]
    e = _dot(u.astype(BF16), wb_ref[...])
    n_lt = SSM_SB // LANES
    y = d_ref[...] * u
    for lt in range(n_lt):
        sl = slice(lt * LANES, (lt + 1) * LANES)
        er_ref[...] = e[:, lt * LANES:(lt + 1) * LANES]
        ei_ref[...] = e[:, SSM_SB + lt * LANES:SSM_SB + (lt + 1) * LANES]
        lr, li = pw_ref[0, 0:1, sl], pw_ref[1, 0:1, sl]
        sr, si = h0_ref[0, :, sl], h0_ref[1, :, sl]
        for t in range(seq):
            rows = pl.ds(t, n_seq, stride=seq)
            sr, si = lr * sr - li * si + er_ref[rows, :], lr * si + li * sr + ei_ref[rows, :]
            er_ref[rows, :] = sr
            ei_ref[rows, :] = si
        y = y + (_dot(er_ref[...].astype(BF16), wc_ref[0, sl, :]) - _dot(ei_ref[...].astype(BF16), wc_ref[1, sl, :]))
        sre_ref[:, sl] = sr
        sim_ref[:, sl] = si
    g_ref[...] = _gelu_tanh(y).astype(BF16)


def ssm_step(proj, ssm_w, d_skip, h0, *, n_batch, seq, row0):
    wb, wc, pw = ssm_w
    n_tok = n_batch * seq
    assert row0 % n_tok == 0
    state_shape = jax.ShapeDtypeStruct((SSM_LB, n_batch, SSM_SB), F32)
    state_spec = pl.BlockSpec((None, n_batch, SSM_SB), lambda j: (j, 0, 0))
    g, s_re, s_im = pl.pallas_call(
        functools.partial(_ssm_step_kernel, seq=seq),
        grid=(SSM_LB,),
        in_specs=[pl.BlockSpec((n_tok, LANES), lambda j: (row0 // n_tok, j)),
                  pl.BlockSpec((None, LANES, 2 * SSM_SB), lambda j: (j, 0, 0)),
                  pl.BlockSpec((None, 2, SSM_SB, LANES), lambda j: (j, 0, 0, 0)),
                  pl.BlockSpec((None, 2, SSM_SEG, SSM_SB), lambda j: (j, 0, 0, 0)),
                  pl.BlockSpec((1, LANES), lambda j: (0, j)),
                  pl.BlockSpec((None, 2, n_batch, SSM_SB), lambda j: (j, 0, 0, 0))],
        out_specs=[pl.BlockSpec((n_tok, LANES), lambda j: (0, j)), state_spec, state_spec],
        out_shape=[jax.ShapeDtypeStruct((n_tok, SSM_WIDTH), BF16), state_shape, state_shape],
        scratch_shapes=[pltpu.VMEM((n_tok, LANES), F32), pltpu.VMEM((n_tok, LANES), F32)],
        compiler_params=_params(("arbitrary",)),
        name="ssm_step",
    )(proj, wb, wc, pw, d_skip[None, :], h0)

    def per_seq(s):
        return s.transpose(1, 0, 2).reshape(n_batch, SSM_GROUPS, SSM_STATE)

    return g, per_seq(s_re), per_seq(s_im)


def _row_sum(x):
    return jnp.sum(x, axis=1, keepdims=True)


def _row_count(mask):
    return _row_sum(jnp.where(mask, 1, 0))


I16 = jnp.int16
I16_MIN = -2 ** 15


def _count16(ref, cand, compare):
    accs = [None] * 4
    for t in range(ref.shape[1] // LANES):
        x = jnp.where(compare(ref[:, t * LANES:(t + 1) * LANES], cand), I16(1), I16(0))
        accs[t % 4] = x if accs[t % 4] is None else accs[t % 4] + x
    accs = [a for a in accs if a is not None]
    total = accs[0]
    for a in accs[1:]:
        total = total + a
    return _row_sum(total.astype(I32))


def _bisect16(ref, target):
    def step(i, base):
        cand = base + lax.shift_left(np.int32(1), np.int32(15) - i)
        cnt = _count16(ref, cand.astype(I16), lambda a, b: a >= b)
        return jnp.where(cnt >= target, cand, base)
    return lax.fori_loop(0, 16, step, jnp.full((ref.shape[0], 1), I16_MIN, I32))


def _bisect32(key_ref, n_sel):
    bq, n_keys = key_ref.shape
    hr = bq // 2

    def lane_counts(h, cand):
        accs = [None] * 4
        for t in range(n_keys // LANES):
            x = jnp.where(key_ref[h * hr:(h + 1) * hr, t * LANES:(t + 1) * LANES] >= cand, 1, 0)
            accs[t % 4] = x if accs[t % 4] is None else accs[t % 4] + x
        accs = [a for a in accs if a is not None]
        total = accs[0]
        for a in accs[1:]:
            total = total + a
        return total

    def decide(part, cand, base):
        return jnp.where(_row_sum(part) >= n_sel, cand, base)

    def bit(i):
        return lax.shift_left(np.int32(1), np.int32(31) - i)

    def body(i, state):
        base_a, base_b, part_b = state
        cand_a = base_a + bit(i)
        part_a = lane_counts(0, cand_a)
        base_b = decide(part_b, base_b + bit(i - 1), base_b)
        part_b = lane_counts(1, base_b + bit(i))
        return decide(part_a, cand_a, base_a), base_b, part_b

    base0 = jnp.full((hr, 1), INT_MIN, I32)
    first = base0 + bit(0)
    state = (decide(lane_counts(0, first), first, base0), base0, lane_counts(1, first))
    base_a, base_b, part_b = lax.fori_loop(1, 32, body, state)
    base_b = decide(part_b, base_b + bit(31), base_b)
    return jnp.concatenate([base_a, base_b], axis=0)


def _stack_heads(ref, heads):
    return jnp.concatenate([ref[:, h * LANES:(h + 1) * LANES] for h in heads], axis=0)


def _dsa_body(q_ref, qi_ref, wi_ref, k_ref, v_ref, ki_ref, o_ref, key_ref, bias_ref, hi_ref, lo_ref, p_ref,
              *, q_pos_first, s_valid, n_sel, packed_bisect, stack):
    bq, n_keys = key_ref.shape
    col = lax.broadcasted_iota(I32, (bq, n_keys), 1)
    qpos = q_pos_first + lax.broadcasted_iota(I32, (bq, 1), 0)
    allowed = col < jnp.minimum((qpos // CHUNK + 1) * CHUNK, s_valid)

    ki = ki_ref[...]
    score = None
    for h0 in range(0, IDX_HEADS, stack):
        d = _dot_nt(_stack_heads(qi_ref, range(h0, h0 + stack)), ki)
        for j in range(stack):
            t = jnp.maximum(d[j * bq:(j + 1) * bq], 0.0) * wi_ref[:, h0 + j:h0 + j + 1]
            score = t if score is None else score + t
    score = jnp.where(score == 0.0, 0.0, score)
    bits = pltpu.bitcast(score, I32)
    key = jnp.where(bits < 0, bits ^ np.int32(0x7FFFFFFF), bits)
    key = jnp.where(allowed, key, KEY_NEG_INF)
    key_ref[...] = key

    if packed_bisect:
        hi_ref[...] = (key >> 16).astype(I16)
        lo_ref[...] = ((key & 0xFFFF) + I16_MIN).astype(I16)
        thr_hi = _bisect16(hi_ref, n_sel)
        thr_hi16 = thr_hi.astype(I16)
        need_lo = n_sel - _count16(hi_ref, thr_hi16, lambda a, b: a > b)
        lo_ref[...] = jnp.where(hi_ref[...] == thr_hi16, lo_ref[...], I16(I16_MIN))
        thr_lo = _bisect16(lo_ref, need_lo)
        thr = lax.shift_left(thr_hi, np.int32(16)) + (thr_lo - I16_MIN)
    else:
        thr = _bisect32(key_ref, n_sel)
    thr = jnp.maximum(thr, KEY_NEG_INF)

    key = key_ref[...]
    need = n_sel - _row_count(key > thr)
    n_eq = _row_count(key == thr)
    n_bits = int(n_keys - 1).bit_length()

    def tie_cut():
        def step(i, j0):
            cand = j0 + lax.shift_left(np.int32(1), np.int32(n_bits - 1) - i)
            cnt = _row_sum(jnp.where(key_ref[...] == thr, jnp.where(col < cand, 1, 0), 0))
            return jnp.where(cnt < need, cand, j0)
        return lax.fori_loop(0, n_bits, step, jnp.zeros((bq, 1), I32))

    split = jnp.max(jnp.where(n_eq > need, 1, 0)) > 0
    j_last = lax.cond(split, tie_cut, lambda: jnp.full((bq, 1), n_keys, I32))
    tie_bias = jnp.where(thr == KEY_NEG_INF, -jnp.inf, 0.0)
    bias_ref[...] = jnp.where(key > thr, 0.0,
                              jnp.where(key == thr, jnp.where(col <= j_last, tie_bias, -jnp.inf), -jnp.inf))

    c = np.float32(HEAD_DIM ** -0.5 * np.log2(np.e))
    for h0 in range(0, N_HEADS, stack):
        kv = h0 // KV_GROUP
        heads = range(h0, h0 + stack)
        s_all = _dot_nt(_stack_heads(q_ref, heads), k_ref[:, kv * HEAD_DIM:(kv + 1) * HEAD_DIM])
        for g in range(stack):
            s = s_all[g * bq:(g + 1) * bq] + bias_ref[...]
            m = jnp.max(s, axis=1, keepdims=True)
            p_ref[g * bq:(g + 1) * bq, :] = jnp.exp2((s - m) * c).astype(BF16)
        pv = _dot(p_ref[0:stack * bq, :], v_ref[:, kv * V_AUG:(kv + 1) * V_AUG])
        for g, h in enumerate(heads):
            o = pv[g * bq:(g + 1) * bq]
            o_ref[:, h * HEAD_DIM:(h + 1) * HEAD_DIM] = (o[:, :HEAD_DIM] / o[:, HEAD_DIM:HEAD_DIM + 1]).astype(BF16)


def _dsa_scratch(bq, n_keys):
    return [pltpu.VMEM((bq, n_keys), I32), pltpu.VMEM((bq, n_keys), F32),
            pltpu.VMEM((bq, n_keys), I16), pltpu.VMEM((bq, n_keys), I16), pltpu.VMEM((KV_GROUP * bq, n_keys), BF16)]


def _dsa_kernel(q_ref, qi_ref, wi_ref, k_ref, v_ref, ki_ref, o_ref, *scratch, q_pos0, **static):
    bq = scratch[0].shape[0]
    _dsa_body(q_ref, qi_ref, wi_ref, k_ref, v_ref, ki_ref, o_ref, *scratch,
              q_pos_first=q_pos0 + pl.program_id(1) * bq, **static)


def dsa(q, qi, wi, k, v, ki, *, bq, q_blk0, n_qblk, n_keys, n_sel, packed_bisect, stack):
    n_batch, seq = q.shape[:2]

    def qspec(width):
        return pl.BlockSpec((None, bq, width), lambda b, i: (b, q_blk0 + i, 0))

    def kspec(width):
        return pl.BlockSpec((None, n_keys, width), lambda b, i: (b, 0, 0))

    return pl.pallas_call(
        functools.partial(_dsa_kernel, q_pos0=q_blk0 * bq, s_valid=seq, n_sel=n_sel, packed_bisect=packed_bisect,
                          stack=stack),
        grid=(n_batch, n_qblk),
        in_specs=[qspec(ATTN_WIDTH), qspec(IDX_HEADS * LANES), qspec(LANES), kspec(KV_WIDTH), kspec(N_KV_HEADS * V_AUG),
                  kspec(LANES)],
        out_specs=pl.BlockSpec((None, bq, ATTN_WIDTH), lambda b, i: (b, i, 0)),
        out_shape=jax.ShapeDtypeStruct((n_batch, n_qblk * bq, ATTN_WIDTH), BF16),
        scratch_shapes=_dsa_scratch(bq, n_keys),
        compiler_params=_params(("arbitrary", "arbitrary")),
        name="dsa",
    )(q, qi, wi, k, v, ki)


def _dsa_step_kernel(q_ref, qi_ref, wi_ref, ck_hbm, cv_hbm, cki_ref, nk_ref, nv_ref, nki_ref, o_ref,
                     k_buf, v_buf, ki_buf, cache_buf, sem, *scratch, past, n_sel):
    b = pl.program_id(0)

    def cache_copies(seq, slot):
        return [pltpu.make_async_copy(src.at[seq, :, h, :], cache_buf.at[slot, a, h], sem.at[slot])
                for a, src in enumerate((ck_hbm, cv_hbm)) for h in range(N_KV_HEADS)]

    @pl.when(b == 0)
    def _():
        for cp in cache_copies(0, 0):
            cp.start()

    @pl.when(b + 1 < pl.num_programs(0))
    def _():
        for cp in cache_copies(b + 1, (b + 1) % 2):
            cp.start()

    slot = b % 2
    for cp in cache_copies(b, slot):
        cp.wait()

    ts = nk_ref.shape[0]
    n_keys = k_buf.shape[0]
    for h in range(N_KV_HEADS):
        k_buf[0:past, h * HEAD_DIM:(h + 1) * HEAD_DIM] = cache_buf[slot, 0, h].astype(BF16)
    _store_v_aug(v_buf, 0, [cache_buf[slot, 1, h] for h in range(N_KV_HEADS)])
    for buf, new in ((k_buf, nk_ref), (v_buf, nv_ref)):
        buf[past:past + ts, :] = new[...]
        buf[past + ts:n_keys, :] = jnp.zeros((n_keys - past - ts, buf.shape[1]), BF16)
    ki_buf[0:past, 0:IDX_DIM] = cki_ref[...].astype(BF16)
    ki_buf[0:past, IDX_DIM:LANES] = jnp.zeros((past, LANES - IDX_DIM), BF16)
    ki_buf[past:past + ts, :] = nki_ref[...]
    ki_buf[past + ts:n_keys, :] = jnp.zeros((n_keys - past - ts, LANES), BF16)
    _dsa_body(q_ref, qi_ref, wi_ref, k_buf, v_buf, ki_buf, o_ref, *scratch,
              q_pos_first=past, s_valid=past + ts, n_sel=n_sel, packed_bisect=True, stack=KV_GROUP)


def dsa_step(q, qi, wi, cache_k, cache_v, cache_ki, k_new, v_new, ki_new, *, n_sel):
    n_batch, ts = q.shape[:2]
    past = cache_k.shape[1]
    n_keys = -(-(past + ts) // LANES) * LANES

    def spec(rows, width):
        return pl.BlockSpec((None, rows, width), lambda b: (b, 0, 0))

    return pl.pallas_call(
        functools.partial(_dsa_step_kernel, past=past, n_sel=n_sel),
        grid=(n_batch,),
        in_specs=[spec(ts, ATTN_WIDTH), spec(ts, IDX_HEADS * LANES), spec(ts, LANES),
                  pl.BlockSpec(memory_space=pl.ANY), pl.BlockSpec(memory_space=pl.ANY), spec(past, IDX_DIM),
                  spec(ts, KV_WIDTH), spec(ts, N_KV_HEADS * V_AUG), spec(ts, LANES)],
        out_specs=spec(ts, ATTN_WIDTH),
        out_shape=jax.ShapeDtypeStruct((n_batch, ts, ATTN_WIDTH), BF16),
        scratch_shapes=[pltpu.VMEM((n_keys, KV_WIDTH), BF16), pltpu.VMEM((n_keys, N_KV_HEADS * V_AUG), BF16),
                        pltpu.VMEM((n_keys, LANES), BF16),
                        pltpu.VMEM((2, 2, N_KV_HEADS, past, HEAD_DIM), F32), pltpu.SemaphoreType.DMA((2,)),
                        *_dsa_scratch(ts, n_keys)],
        compiler_params=_params(("arbitrary",)),
        name="dsa_step",
    )(q, qi, wi, cache_k, cache_v, cache_ki, k_new, v_new, ki_new)


def _merge_kernel(g_ref, a_ref, ga_ref, gb_ref, wv_ref, wg_ref, wb_ref, o_ref):
    g = g_ref[...]
    branch_a = _dot(g, wv_ref[...]) * jax.nn.sigmoid(_dot(g, wg_ref[...]))
    branch_b = _dot(a_ref[...], wb_ref[...])
    merged = jax.nn.sigmoid(ga_ref[...]) * branch_a + jax.nn.sigmoid(gb_ref[...]) * branch_b
    o_ref[...] = merged.astype(BF16)


def merge(g, attn, proj, w_val, w_gate, w_branch, *, tm=1024, tn=512):
    n_tok = g.shape[0]
    nj = D_MODEL // tn

    def wspec():
        return pl.BlockSpec((SSM_WIDTH, tn), lambda i, j: (0, j))

    return pl.pallas_call(
        _merge_kernel,
        grid=(n_tok // tm, nj),
        in_specs=[pl.BlockSpec((tm, SSM_WIDTH), lambda i, j: (i, 0)),
                  pl.BlockSpec((tm, ATTN_WIDTH), lambda i, j: (i, 0)),
                  pl.BlockSpec((tm, tn), lambda i, j: (i, COL_GA // tn + j)),
                  pl.BlockSpec((tm, tn), lambda i, j: (i, COL_GB // tn + j)),
                  wspec(), wspec(), wspec()],
        out_specs=pl.BlockSpec((tm, tn), lambda i, j: (i, j)),
        out_shape=jax.ShapeDtypeStruct((n_tok, D_MODEL), BF16),
        compiler_params=_params(("arbitrary", "arbitrary")),
        name="merge",
    )(g, attn, proj, proj, w_val, w_gate, w_branch)


ROUTER_COLS = N_EXPERT_GROUPS + N_EXPERTS
MOE_TM = 256


def _first_lane_of_max(x, lane_f):
    m = jnp.max(x, axis=1, keepdims=True)
    return m, jnp.min(jnp.where(x == m, lane_f, float(LANES)), axis=1, keepdims=True)


def _out_proj_kernel(x_ref, m_ref, wo_ref, gn_ref, wr_ref, br_ref, cin_ref,
                     h_ref, hn_ref, ri_ref, rw_ref, cnt_ref, carry_ref):
    @pl.when(pl.program_id(0) == 0)
    def _():
        carry_ref[...] = cin_ref[...]

    h = x_ref[...] + _dot(m_ref[...], wo_ref[...])
    h_ref[...] = h
    ms = jnp.mean(h * h, axis=-1, keepdims=True)
    hn = h * lax.rsqrt(ms + EPS) * gn_ref[...]
    hn_ref[...] = hn
    hh, hl = _split_bf16(hn)
    both = _dot(hh, wr_ref[...])
    lg = both[:, :LANES] + _dot(hl, wr_ref[:, 0:LANES]) + both[:, LANES:] + br_ref[...]

    tm = lg.shape[0]
    lane = lax.broadcasted_iota(I32, lg.shape, 1)
    lane_f = lane.astype(F32)
    ninf = -jnp.inf
    gl = jnp.where(lane < N_EXPERT_GROUPS, lg, ninf)
    gmax, gsel = _first_lane_of_max(gl, lane_f)
    g_w = 1.0 / jnp.sum(jnp.exp(gl - gmax), axis=1, keepdims=True)
    lo = N_EXPERT_GROUPS + EXPERTS_PER_GROUP * gsel
    el = jnp.where(lane_f >= lo, jnp.where(lane_f < lo + EXPERTS_PER_GROUP, lg, ninf), ninf)
    v1, i1 = _first_lane_of_max(el, lane_f)
    el2 = jnp.where(lane_f == i1, ninf, el)
    v2, i2 = _first_lane_of_max(el2, lane_f)
    t = jnp.exp(v2 - v1)
    s1 = 1.0 / (1.0 + t)
    w1 = s1 * g_w
    w2 = (t * s1) * g_w

    m1 = jnp.where(lane_f == i1, 1.0, 0.0)
    m2 = jnp.where(lane_f == i2, 1.0, 0.0)
    both = m1 + m2
    tri = jnp.where(lax.broadcasted_iota(I32, (tm, tm), 0) > lax.broadcasted_iota(I32, (tm, tm), 1), 1.0, 0.0)
    before = _dot(tri.astype(BF16), both.astype(BF16)) + carry_ref[...]
    r1 = jnp.sum(before * m1, axis=1, keepdims=True)
    r2 = jnp.sum(before * m2, axis=1, keepdims=True)
    carry_ref[...] = carry_ref[...] + jnp.sum(both, axis=0, keepdims=True)
    cnt_ref[...] = carry_ref[...]
    e1 = i1 - float(N_EXPERT_GROUPS)
    e2 = i2 - float(N_EXPERT_GROUPS)
    fields = jnp.where(lane == 0, e1, jnp.where(lane == 1, e2, jnp.where(lane == 2, r1, jnp.where(lane == 3, r2, 0.0))))
    ri_ref[...] = fields.T[0:SUBLANES, :].astype(I32)
    rw_ref[...] = jnp.where(lane == 0, w1, jnp.where(lane == 1, w2, 0.0))


def _router_weights(w_router_group, b_router_group, w_router_expert, b_router_expert):
    wr = jnp.concatenate([w_router_group, w_router_expert, jnp.zeros((D_MODEL, LANES - ROUTER_COLS), F32)], axis=1)
    wr_hi = wr.astype(BF16)
    wr_lo = (wr - wr_hi.astype(F32)).astype(BF16)
    br = jnp.concatenate([b_router_group, b_router_expert, jnp.zeros((LANES - ROUTER_COLS,), F32)])[None, :]
    return jnp.concatenate([wr_hi, wr_lo], axis=1), br


def out_proj(x, merged, w_out, ffn_gain, router_w, counts_in, *, tm=512):
    n_tok = x.shape[0]
    wr, br = router_w

    def row(width):
        return pl.BlockSpec((tm, width), lambda i: (i, 0))

    def const(shape):
        return pl.BlockSpec(shape, lambda i: (0, 0), pipeline_mode=pl.Buffered(1))

    return pl.pallas_call(
        _out_proj_kernel,
        grid=(n_tok // tm,),
        in_specs=[row(D_MODEL), row(D_MODEL), const((D_MODEL, D_MODEL)), const((1, D_MODEL)),
                  const((D_MODEL, 2 * LANES)), const((1, LANES)), const((1, LANES))],
        out_specs=[row(D_MODEL), row(D_MODEL), pl.BlockSpec((SUBLANES, tm), lambda i: (0, i)), row(LANES),
                   pl.BlockSpec((1, LANES), lambda i: (0, 0))],
        out_shape=[jax.ShapeDtypeStruct((n_tok, D_MODEL), F32), jax.ShapeDtypeStruct((n_tok, D_MODEL), F32),
                   jax.ShapeDtypeStruct((SUBLANES, n_tok), I32), jax.ShapeDtypeStruct((n_tok, LANES), F32),
                   jax.ShapeDtypeStruct((1, LANES), F32)],
        scratch_shapes=[pltpu.VMEM((1, LANES), F32)],
        compiler_params=_params(("arbitrary",)),
        name="out_proj",
    )(x, merged, w_out, ffn_gain[None, :], wr, br, counts_in)


def _block_layout(counts):
    padded = (counts + MOE_TM - 1) // MOE_TM * MOE_TM
    pad_end = jnp.cumsum(padded).astype(I32)
    pad_start = pad_end - padded
    n_used = pad_end[-1] // MOE_TM
    return pad_start, pad_end, n_used


def _dest_kernel(ps_ref, ri_ref, o_ref):
    ri = ri_ref[...]
    start = jnp.zeros_like(ri)
    for k in range(N_EXPERTS):
        start = jnp.where(ri == k, ps_ref[k], start)
    o_ref[...] = start + pltpu.roll(ri, SUBLANES - TOP_K, 0)


def dest_rows(route_i, pad_start):
    grid_spec = pltpu.PrefetchScalarGridSpec(
        num_scalar_prefetch=1, grid=(1,),
        in_specs=[pl.BlockSpec(route_i.shape, lambda i, ps: (0, 0))],
        out_specs=pl.BlockSpec(route_i.shape, lambda i, ps: (0, 0)))
    return pl.pallas_call(_dest_kernel, grid_spec=grid_spec, out_shape=jax.ShapeDtypeStruct(route_i.shape, I32),
                          compiler_params=_params(("arbitrary",)), name="dest_rows")(pad_start, route_i)


def _moe_rows(n_tok):
    return -(-(n_tok * TOP_K + N_EXPERTS * (MOE_TM - 1)) // MOE_TM) * MOE_TM


DISPATCH_TM = 1024


def _wait_rows(src_hbm, dst, sem, n_rows):
    pltpu.make_async_copy(src_hbm.at[pl.ds(0, n_rows)], dst, sem).wait()


def _dispatch_kernel(d0_ref, d1_ref, pe_ref, cnt_ref, nu_ref, hna_ref, hnb_ref, xs_hbm, zbuf, sem, semz,
                     *, n_blocks, a_tiles):
    i = pl.program_id(0)

    def zero_block(row0):
        return pltpu.make_async_copy(zbuf, xs_hbm.at[pl.ds(pl.multiple_of(row0, MOE_TM), MOE_TM)], semz)

    @pl.when(i == 0)
    def _():
        zbuf[...] = jnp.zeros_like(zbuf)
        for start in (True, False):
            for e in range(N_EXPERTS):
                @pl.when(cnt_ref[e] > 0)
                def _():
                    cp = zero_block(pe_ref[e] - MOE_TM)
                    cp.start() if start else cp.wait()

            def tail(b, c):
                cp = zero_block(b * MOE_TM)
                cp.start() if start else cp.wait()
                return c
            lax.fori_loop(nu_ref[0], n_blocks, tail, 0)

    base = i * DISPATCH_TM

    def scatter(hn_ref):
        def body(r, c):
            src = hn_ref.at[pl.ds(r, 1)]
            pltpu.make_async_copy(src, xs_hbm.at[pl.ds(d0_ref[base + r], 1)], sem).start()
            pltpu.make_async_copy(src, xs_hbm.at[pl.ds(d1_ref[base + r], 1)], sem).start()
            return c
        lax.fori_loop(0, DISPATCH_TM, body, 0, unroll=8)
        for _ in range(TOP_K):
            pltpu.make_async_copy(hn_ref, xs_hbm.at[pl.ds(0, DISPATCH_TM)], sem).wait()

    @pl.when(i < a_tiles)
    def _():
        scatter(hna_ref)

    @pl.when(i >= a_tiles)
    def _():
        scatter(hnb_ref)


def dispatch(hn_a, hn_b, dest0, dest1, pad_end, counts, n_used):
    a_tiles, b_tiles = hn_a.shape[0] // DISPATCH_TM, hn_b.shape[0] // DISPATCH_TM
    rows = _moe_rows(hn_a.shape[0] + hn_b.shape[0])
    grid_spec = pltpu.PrefetchScalarGridSpec(
        num_scalar_prefetch=5,
        grid=(a_tiles + b_tiles,),
        in_specs=[pl.BlockSpec((DISPATCH_TM, D_MODEL), lambda i, *_: (jnp.minimum(i, a_tiles - 1), 0)),
                  pl.BlockSpec((DISPATCH_TM, D_MODEL), lambda i, *_: (jnp.maximum(i - a_tiles, 0), 0))],
        out_specs=pl.BlockSpec(memory_space=pl.ANY),
        scratch_shapes=[pltpu.VMEM((MOE_TM, D_MODEL), F32), pltpu.SemaphoreType.DMA(()), pltpu.SemaphoreType.DMA(())],
    )
    return pl.pallas_call(
        functools.partial(_dispatch_kernel, n_blocks=rows // MOE_TM, a_tiles=a_tiles),
        grid_spec=grid_spec,
        out_shape=jax.ShapeDtypeStruct((rows, D_MODEL), F32),
        compiler_params=_params(("arbitrary",)),
        name="dispatch",
    )(dest0, dest1, pad_end, counts, n_used, hn_a, hn_b)


MOE_UNITS = 8
MOE_UG = D_MODEL // MOE_UNITS
MOE_UD = EXPERT_FF // MOE_UNITS


def _moe_kernel(blk_e_ref, nu_ref, nxt_ref, upb_ref, xs_ref, wg_hbm, wu_hbm, wd_hbm, ys_ref,
                wg_bf, wu_bf, wd_bf, stg_g, stg_u, stg_d, sem, st_ref):
    i = pl.program_id(0)
    cur_slot, pos, cur_e = 0, 1, 2

    def unit_copies(e, unit, s):
        g_rows = pl.ds(pl.multiple_of(unit * MOE_UG, MOE_UG), MOE_UG)
        d_rows = pl.ds(pl.multiple_of(unit * MOE_UD, MOE_UD), MOE_UD)
        return (pltpu.make_async_copy(wg_hbm.at[e, g_rows, :], stg_g.at[s], sem.at[s]),
                pltpu.make_async_copy(wu_hbm.at[e, g_rows, :], stg_u.at[s], sem.at[s]),
                pltpu.make_async_copy(wd_hbm.at[e, d_rows, :], stg_d.at[s], sem.at[s]))

    def start_unit(e, unit):
        for cp in unit_copies(e, unit, unit % 2):
            cp.start()

    def begin_load(e):
        st_ref[pos] = 0
        start_unit(e, 0)
        start_unit(e, 1)

    def advance(e, slot, n):
        def body(_, c):
            unit = st_ref[pos]

            @pl.when(unit < MOE_UNITS)
            def _():
                s = unit % 2
                for cp in unit_copies(e, unit, s):
                    cp.wait()
                g_rows = pl.ds(pl.multiple_of(unit * MOE_UG, MOE_UG), MOE_UG)
                d_rows = pl.ds(pl.multiple_of(unit * MOE_UD, MOE_UD), MOE_UD)
                wg_bf[slot, g_rows, :] = stg_g[s].astype(BF16)
                wu_bf[slot, g_rows, :] = stg_u[s].astype(BF16)
                wd_bf[slot, d_rows, :] = stg_d[s].astype(BF16)

                @pl.when(unit + 2 < MOE_UNITS)
                def _():
                    start_unit(e, unit + 2)
                st_ref[pos] = unit + 1
            return c
        lax.fori_loop(0, n, body, 0)

    def load_next(nxt):
        @pl.when(nxt >= 0)
        def _():
            begin_load(nxt)

        @pl.when(nxt < 0)
        def _():
            st_ref[pos] = MOE_UNITS

    @pl.when(i < nu_ref[0])
    def _():
        e = blk_e_ref[i]
        nxt = nxt_ref[i]

        @pl.when(i == 0)
        def _():
            st_ref[cur_slot] = 0
            st_ref[cur_e] = e
            begin_load(e)
            advance(e, 0, MOE_UNITS)
            load_next(nxt)

        @pl.when(jnp.logical_and(i > 0, e != st_ref[cur_e]))
        def _():
            slot = 1 - st_ref[cur_slot]
            advance(e, slot, MOE_UNITS)
            st_ref[cur_slot] = slot
            st_ref[cur_e] = e
            load_next(nxt)

        slot = st_ref[cur_slot]
        x = xs_ref[...].astype(BF16)
        hg = _dot(x, wg_bf[slot])
        hu = _dot(x, wu_bf[slot])
        hmid = (jax.nn.silu(hg) * hu).astype(BF16)
        ys_ref[...] = _dot(hmid, wd_bf[slot])

        @pl.when(nxt >= 0)
        def _():
            advance(nxt, 1 - slot, upb_ref[i])

    @pl.when(i >= nu_ref[0])
    def _():
        ys_ref[...] = jnp.zeros_like(ys_ref)


def moe(xs, blk_e, n_used, nxt_e, units_per_block, w_gate, w_up, w_down):
    rows = xs.shape[0]
    grid_spec = pltpu.PrefetchScalarGridSpec(
        num_scalar_prefetch=4,
        grid=(rows // MOE_TM,),
        in_specs=[pl.BlockSpec((MOE_TM, D_MODEL), lambda i, be, nu, nx, ub: (jnp.minimum(i, nu[0] - 1), 0)),
                  pl.BlockSpec(memory_space=pl.ANY), pl.BlockSpec(memory_space=pl.ANY), pl.BlockSpec(memory_space=pl.ANY)],
        out_specs=pl.BlockSpec((MOE_TM, D_MODEL), lambda i, be, nu, nx, ub: (i, 0)),
        scratch_shapes=[pltpu.VMEM((2, D_MODEL, EXPERT_FF), BF16), pltpu.VMEM((2, D_MODEL, EXPERT_FF), BF16),
                        pltpu.VMEM((2, EXPERT_FF, D_MODEL), BF16),
                        pltpu.VMEM((2, MOE_UG, EXPERT_FF), F32), pltpu.VMEM((2, MOE_UG, EXPERT_FF), F32),
                        pltpu.VMEM((2, MOE_UD, D_MODEL), F32),
                        pltpu.SemaphoreType.DMA((2,)), pltpu.SMEM((3,), I32)],
    )
    return pl.pallas_call(
        _moe_kernel,
        grid_spec=grid_spec,
        out_shape=jax.ShapeDtypeStruct((rows, D_MODEL), F32),
        compiler_params=_params(("arbitrary",)),
        name="moe",
    )(blk_e, n_used, nxt_e, units_per_block, xs, w_gate, w_up, w_down)


def _combine_kernel(r0_ref, r1_ref, ys_hbm, h_ref, w_ref, o_ref, buf, sem, *, tm, tok0):
    i = pl.program_id(0)

    def issue(block, slot):
        base = tok0 + block * tm

        def body(r, carry):
            for k, idx_ref in enumerate((r0_ref, r1_ref)):
                pltpu.make_async_copy(ys_hbm.at[pl.ds(idx_ref[base + r], 1)], buf.at[slot, k, pl.ds(r, 1)],
                                      sem.at[slot]).start()
            return carry
        lax.fori_loop(0, tm, body, 0, unroll=8)

    @pl.when(i == 0)
    def _():
        issue(0, 0)

    @pl.when(i + 1 < pl.num_programs(0))
    def _():
        issue(i + 1, (i + 1) % 2)

    slot = i % 2
    _wait_rows(ys_hbm, buf.at[slot, 0], sem.at[slot], tm)
    _wait_rows(ys_hbm, buf.at[slot, 1], sem.at[slot], tm)
    w = w_ref[...]
    o_ref[...] = h_ref[...] + (buf[slot, 0] * w[:, 0:1] + buf[slot, 1] * w[:, 1:2])


def combine(ys, h, route_w, rows0, rows1, *, tok0, tm=256):
    n_tok = h.shape[0]
    grid_spec = pltpu.PrefetchScalarGridSpec(
        num_scalar_prefetch=2,
        grid=(n_tok // tm,),
        in_specs=[pl.BlockSpec(memory_space=pl.ANY),
                  pl.BlockSpec((tm, D_MODEL), lambda i, a, b: (i, 0)),
                  pl.BlockSpec((tm, LANES), lambda i, a, b: (i, 0))],
        out_specs=pl.BlockSpec((tm, D_MODEL), lambda i, a, b: (i, 0)),
        scratch_shapes=[pltpu.VMEM((2, 2, tm, D_MODEL), F32), pltpu.SemaphoreType.DMA((2,))],
    )
    return pl.pallas_call(
        functools.partial(_combine_kernel, tm=tm, tok0=tok0),
        grid_spec=grid_spec,
        out_shape=jax.ShapeDtypeStruct((n_tok, D_MODEL), F32),
        compiler_params=_params(("arbitrary",)),
        name="combine",
    )(rows0, rows1, ys, h, route_w)


IN_SIZES = (SSM_WIDTH, ATTN_WIDTH, KV_WIDTH, KV_WIDTH, IDX_HEADS * IDX_DIM, IDX_DIM, IDX_HEADS, D_MODEL, D_MODEL)
IN_COLS = sum(IN_SIZES)
SRC_U, SRC_Q, SRC_K, SRC_V, SRC_QI, SRC_KI, SRC_WI, SRC_GA, SRC_GB = (int(c) for c in np.cumsum((0,) + IN_SIZES[:-1]))


def _regroup_kernel(w_ref, o_ref):
    runs = ((COL_U, SRC_U, SSM_WIDTH + ATTN_WIDTH), (COL_GA, SRC_GA, D_MODEL), (COL_GB, SRC_GB, D_MODEL),
            (COL_K, SRC_K, 2 * KV_WIDTH + IDX_HEADS * IDX_DIM), (COL_KIWI, SRC_KI, IDX_DIM + IDX_HEADS))
    for dst, src, n in runs:
        o_ref[:, dst:dst + n] = w_ref[:, src:src + n].astype(BF16)
    tail = COL_KIWI + IDX_DIM + IDX_HEADS
    o_ref[:, tail:PROJ_COLS] = jnp.zeros((o_ref.shape[0], PROJ_COLS - tail), BF16)


def _regroup_w_in(w_in, *, tr=256):
    return pl.pallas_call(
        _regroup_kernel,
        grid=(D_MODEL // tr,),
        in_specs=[pl.BlockSpec((tr, IN_COLS), lambda i: (i, 0))],
        out_specs=pl.BlockSpec((tr, PROJ_COLS), lambda i: (i, 0)),
        out_shape=jax.ShapeDtypeStruct((D_MODEL, PROJ_COLS), BF16),
        compiler_params=_params(("arbitrary",)),
        name="regroup_w_in",
    )(w_in)


def _layer(x_p, x_s, cache_k, cache_v, cache_ki, h0_re, h0_im, p):
    bp, tp, _ = x_p.shape
    bs, ts, _ = x_s.shape
    past = cache_k.shape[1]
    n_p, n_s = bp * tp, bs * ts
    n_tok = n_p + n_s

    w_in = _regroup_w_in(p['w_in'])
    ssm_w = _ssm_weights(p['ssm_A_re'], p['ssm_A_im'], p['ssm_log_dt'], p['ssm_B_re'], p['ssm_B_im'],
                         p['ssm_C_re'], p['ssm_C_im'])
    glu_w = (p['w_glu_val'].astype(BF16), p['w_glu_gate'].astype(BF16), p['w_attn_branch'].astype(BF16))
    w_out = p['w_out'].astype(BF16)
    router_w = _router_weights(p['w_router_group'], p['b_router_group'], p['w_router_expert'], p['b_router_expert'])
    seq_tiles = tp // QK_TM

    def front(x, table_pos, table_block):
        proj = in_proj(x, p['norm_mix_g'][None, :], w_in)
        return proj, qk_post(proj, table_pos, table_block, p['q_norm_g'], p['k_norm_g'], p['idx_k_norm_g'])

    def seqs(a, b, t):
        return a.reshape(b, t, a.shape[-1])

    xp = x_p.reshape(n_p, D_MODEL)
    proj_p, (q_b, kf_p, k_b, vf_p, v_b, qi_b, kif_p, ki_b, wi) = front(
        xp, jnp.arange(tp, dtype=I32), lambda i: i % seq_tiles)
    g_p, sre_p, sim_p = ssm(proj_p, ssm_w, p['ssm_D'], jnp.zeros((bp, SSM_LB, 2, SSM_SB), F32),
                            n_batch=bp, seq=tp, row0=0)
    bq = 128
    n_buckets = min(16, tp // bq)
    per = tp // bq // n_buckets
    qp, qip, wip = seqs(q_b, bp, tp), seqs(qi_b, bp, tp), seqs(wi, bp, tp)
    kp, vp, kip = seqs(k_b, bp, tp), seqs(v_b, bp, tp), seqs(ki_b, bp, tp)
    attn_p = jnp.concatenate(
        [dsa(qp, qip, wip, kp, vp, kip, bq=bq, q_blk0=n * per, n_qblk=per, n_keys=(n + 1) * per * bq,
             n_sel=min(IDX_TOPK, tp // 4), packed_bisect=False, stack=1)
         for n in range(n_buckets)], axis=1).reshape(n_p, ATTN_WIDTH)
    merged_p = merge(g_p, attn_p, proj_p, *glu_w)
    h_p, hn_p, ri_p, rw_p, cnt_p = out_proj(xp, merged_p, w_out, p['norm_ffn_g'], router_w, jnp.zeros((1, LANES), F32))

    xs_ = x_s.reshape(n_s, D_MODEL)
    proj_s, (q_b, kf_s, k_b, vf_s, v_b, qi_b, kif_s, ki_b, wi) = front(
        xs_, jnp.tile(past + jnp.arange(ts, dtype=I32), QK_TM // ts), lambda i: 0)
    h0 = jnp.stack([h0_re.reshape(bs, SSM_LB, SSM_SB), h0_im.reshape(bs, SSM_LB, SSM_SB)]).transpose(2, 0, 1, 3)
    g_s, sre_s, sim_s = ssm_step(proj_s, ssm_w, p['ssm_D'], h0, n_batch=bs, seq=ts, row0=0)
    attn_s = dsa_step(seqs(q_b, bs, ts), seqs(qi_b, bs, ts), seqs(wi, bs, ts),
                      cache_k, cache_v, cache_ki,
                      seqs(k_b, bs, ts), seqs(v_b, bs, ts), seqs(ki_b, bs, ts),
                      n_sel=min(IDX_TOPK, (past + ts) // 4)).reshape(n_s, ATTN_WIDTH)
    merged_s = merge(g_s, attn_s, proj_s, *glu_w)
    h_s, hn_s, ri_s, rw_s, cnt = out_proj(xs_, merged_s, w_out, p['norm_ffn_g'], router_w, cnt_p)

    counts = cnt[0, N_EXPERT_GROUPS:ROUTER_COLS].astype(I32)
    pad_start, pad_end, n_used = _block_layout(counts)
    dest = dest_rows(jnp.concatenate([ri_p, ri_s], axis=1), pad_start)
    dest0, dest1 = dest[0], dest[1]
    n_blocks = _moe_rows(n_tok) // MOE_TM
    blk = jnp.minimum(jnp.arange(n_blocks, dtype=I32), n_used - 1)
    blk_e = jnp.minimum(jnp.sum((pad_end[None, :] <= (blk * MOE_TM)[:, None]).astype(I32), axis=1), N_EXPERTS - 1)
    after = pad_end[blk_e] // MOE_TM
    nxt_e = jnp.where(after < n_used, blk_e[jnp.minimum(after, n_blocks - 1)], -1).astype(I32)
    blocks_of_e = jnp.maximum((pad_end - pad_start)[blk_e] // MOE_TM, 1)
    units_per_block = ((MOE_UNITS + blocks_of_e - 1) // blocks_of_e).astype(I32)
    n_used = n_used.reshape(1)

    xs = dispatch(hn_p, hn_s, dest0, dest1, pad_end, counts, n_used)
    ys = moe(xs, blk_e, n_used, nxt_e, units_per_block, p['w_exp_gate'], p['w_exp_up'], p['w_exp_down'])
    y_p = combine(ys, h_p, rw_p, dest0, dest1, tok0=0).reshape(bp, tp, D_MODEL)
    y_s = combine(ys, h_s, rw_s, dest0, dest1, tok0=n_p).reshape(bs, ts, D_MODEL)

    def heads(a, b, t):
        return a.reshape(b, t, N_KV_HEADS, HEAD_DIM)

    new_p = (heads(kf_p, bp, tp), heads(vf_p, bp, tp), kif_p.reshape(bp, tp, IDX_DIM), sre_p, sim_p)
    new_s = (heads(kf_s, bs, ts), heads(vf_s, bs, ts), kif_s.reshape(bs, ts, IDX_DIM), sre_s, sim_s)
    return y_p, y_s, new_p, new_s


def kernel(x_prompt, x_sample, cache_k, cache_v, cache_idx_k, state_ssm_re, state_ssm_im, norm_mix_g, w_in, q_norm_g, k_norm_g, idx_k_norm_g, ssm_A_re, ssm_A_im, ssm_log_dt, ssm_B_re, ssm_B_im, ssm_C_re, ssm_C_im, ssm_D, w_glu_val, w_glu_gate, w_attn_branch, w_out, norm_ffn_g, w_router_group, b_router_group, w_router_expert, b_router_expert, w_exp_gate, w_exp_up, w_exp_down):
    depth = w_in.shape[0]
    assert depth == 1, "prompt and sample tokens are batched through one layer"
    names = ('norm_mix_g', 'w_in', 'q_norm_g', 'k_norm_g', 'idx_k_norm_g', 'ssm_A_re', 'ssm_A_im', 'ssm_log_dt',
             'ssm_B_re', 'ssm_B_im', 'ssm_C_re', 'ssm_C_im', 'ssm_D', 'w_glu_val', 'w_glu_gate', 'w_attn_branch',
             'w_out', 'norm_ffn_g', 'w_router_group', 'b_router_group', 'w_router_expert', 'b_router_expert',
             'w_exp_gate', 'w_exp_up', 'w_exp_down')
    vals = (norm_mix_g, w_in, q_norm_g, k_norm_g, idx_k_norm_g, ssm_A_re, ssm_A_im, ssm_log_dt, ssm_B_re, ssm_B_im,
            ssm_C_re, ssm_C_im, ssm_D, w_glu_val, w_glu_gate, w_attn_branch, w_out, norm_ffn_g, w_router_group,
            b_router_group, w_router_expert, b_router_expert, w_exp_gate, w_exp_up, w_exp_down)
    p = {n: v[0] for n, v in zip(names, vals)}
    y_p, y_s, new_p, new_s = _layer(x_prompt, x_sample, cache_k[0], cache_v[0], cache_idx_k[0],
                                    state_ssm_re[0], state_ssm_im[0], p)
    st_p = tuple(a[None] for a in new_p)
    st_s = tuple(a[None] for a in new_s)
    return (y_p, y_s) + st_p + st_s
```

```python
import functools

import numpy as np
import jax
import jax.numpy as jnp
from jax import lax
from jax.experimental import pallas as pl
from jax.experimental.pallas import tpu as pltpu

F32 = jnp.float32
BF16 = jnp.bfloat16
I32 = jnp.int32

D_MODEL = 2048
CHUNK = 64
SSM_WIDTH = 1024
SSM_GROUP = 16
SSM_GROUPS = 64
SSM_STATE = 64
ATTN_WIDTH = 1024
HEAD_DIM = 128
N_HEADS = 8
N_KV_HEADS = 2
KV_GROUP = 4
IDX_HEADS = 8
IDX_DIM = 64
IDX_TOPK = 256
ROPE_THETA = 500000.0
N_EXPERT_GROUPS = 4
EXPERTS_PER_GROUP = 8
N_EXPERTS = 32
TOP_K = 2
EXPERT_FF = 1024
EPS = 1e-6

LANES = 128
SUBLANES = 8
VMEM_LIMIT = 56 * 1024 * 1024

COL_U, COL_Q, COL_GA, COL_GB, COL_K, COL_V, COL_QI, COL_KIWI = 0, 1024, 2048, 4096, 6144, 6400, 6656, 7168
PROJ_COLS = 7296
PROJ_TN = 2432
KV_WIDTH = N_KV_HEADS * HEAD_DIM

SSM_LB = SSM_WIDTH // LANES
SSM_SB = 8 * SSM_STATE

INT_MIN = np.int32(-2 ** 31)
KEY_NEG_INF = np.int32(np.array([0xFF800000], np.uint32).view(np.int32)[0] ^ 0x7FFFFFFF)


def _params(sem, vmem=VMEM_LIMIT):
    return pltpu.CompilerParams(dimension_semantics=sem, vmem_limit_bytes=vmem)


def _dot(a, b):
    return jnp.dot(a, b, preferred_element_type=F32)


def _dot_nt(a, b):
    return lax.dot_general(a, b, (((1,), (1,)), ((), ())), preferred_element_type=F32)


def _split_bf16(x):
    hi = x.astype(BF16)
    lo = (x - hi.astype(F32)).astype(BF16)
    return hi, lo


def _in_proj_kernel(x_ref, g_ref, w_ref, o_ref, xn_ref):
    @pl.when(pl.program_id(1) == 0)
    def _():
        x = x_ref[...]
        ms = jnp.mean(x * x, axis=-1, keepdims=True)
        xn_ref[...] = (x * lax.rsqrt(ms + EPS) * g_ref[...]).astype(BF16)

    o_ref[...] = _dot(xn_ref[...], w_ref[...])


def in_proj(x, gain, w_bf16, *, tm=512):
    n_tok = x.shape[0]
    return pl.pallas_call(
        _in_proj_kernel,
        grid=(n_tok // tm, PROJ_COLS // PROJ_TN),
        in_specs=[pl.BlockSpec((tm, D_MODEL), lambda i, j: (i, 0)),
                  pl.BlockSpec((1, D_MODEL), lambda i, j: (0, 0)),
                  pl.BlockSpec((D_MODEL, PROJ_TN), lambda i, j: (0, j))],
        out_specs=pl.BlockSpec((tm, PROJ_TN), lambda i, j: (i, j)),
        out_shape=jax.ShapeDtypeStruct((n_tok, PROJ_COLS), F32),
        scratch_shapes=[pltpu.VMEM((tm, D_MODEL), BF16)],
        compiler_params=_params(("arbitrary", "arbitrary")),
        name="in_proj",
    )(x, gain, w_bf16)


def _rope(x, c, s_lo, s_hi, half):
    n = x.shape[-1]
    return x * c + pltpu.roll(x, n - half, 1) * s_lo + pltpu.roll(x, half, 1) * s_hi


def _head_norm(x, g):
    ms = jnp.mean(x * x, axis=-1, keepdims=True)
    return x * lax.rsqrt(ms + EPS) * g


V_AUG = 2 * HEAD_DIM


def _store_v_aug(dst_ref, row0, v_heads):
    n = v_heads[0].shape[0]
    one_col = jnp.where(lax.broadcasted_iota(I32, (n, HEAD_DIM), 1) == 0, 1.0, 0.0).astype(BF16)
    for h, v in enumerate(v_heads):
        dst_ref[row0:row0 + n, h * V_AUG:h * V_AUG + HEAD_DIM] = v.astype(BF16)
        dst_ref[row0:row0 + n, h * V_AUG + HEAD_DIM:(h + 1) * V_AUG] = one_col


def _qk_post_kernel(q_ref, k_ref, v_ref, qi_ref, kw_ref, c128_ref, sl128_ref, sh128_ref,
                    c64_ref, sl64_ref, sh64_ref, qg_ref, kg_ref, ig_ref,
                    qo_ref, kf_ref, kb_ref, vf_ref, vb_ref, qio_ref, kif_ref, kib_ref, wo_ref):
    c128, sl128, sh128 = c128_ref[...], sl128_ref[...], sh128_ref[...]
    c64, sl64, sh64 = c64_ref[...], sl64_ref[...], sh64_ref[...]
    half128 = HEAD_DIM // 8
    half64 = IDX_DIM // 8
    for h in range(N_HEADS):
        sl = slice(h * LANES, (h + 1) * LANES)
        qo_ref[:, sl] = _rope(_head_norm(q_ref[:, sl], qg_ref[...]), c128, sl128, sh128, half128).astype(BF16)
    for h in range(N_KV_HEADS):
        sl = slice(h * LANES, (h + 1) * LANES)
        kk = _rope(_head_norm(k_ref[:, sl], kg_ref[...]), c128, sl128, sh128, half128)
        kf_ref[:, sl] = kk
        kb_ref[:, sl] = kk.astype(BF16)
    v = v_ref[...]
    vf_ref[...] = v
    _store_v_aug(vb_ref, 0, [v[:, h * HEAD_DIM:(h + 1) * HEAD_DIM] for h in range(N_KV_HEADS)])
    lane = lax.broadcasted_iota(I32, c64.shape, 1)
    low = lane < IDX_DIM
    for p in range(IDX_HEADS // 2):
        x = _rope(qi_ref[:, p * LANES:(p + 1) * LANES], c64, sl64, sh64, half64)
        qio_ref[:, (2 * p) * LANES:(2 * p + 1) * LANES] = jnp.where(low, x, 0.0).astype(BF16)
        qio_ref[:, (2 * p + 1) * LANES:(2 * p + 2) * LANES] = jnp.where(low, pltpu.roll(x, IDX_DIM, 1), 0.0).astype(BF16)
    kw = kw_ref[...]
    ms = jnp.sum(jnp.where(low, kw * kw, 0.0), axis=-1, keepdims=True) * (1.0 / IDX_DIM)
    ki = _rope(kw * lax.rsqrt(ms + EPS) * ig_ref[...], c64, sl64, sh64, half64)
    kif_ref[...] = ki[:, :IDX_DIM]
    kib_ref[...] = jnp.where(low, ki, 0.0).astype(BF16)
    wo_ref[...] = (pltpu.roll(kw, IDX_DIM, 1) * IDX_HEADS ** -0.5) * IDX_DIM ** -0.5


def _rope_tables(pos, head_dim):
    r = head_dim // 4
    half = r // 2
    inv = ROPE_THETA ** (-jnp.arange(half, dtype=F32) * 2.0 / r)
    ang = pos.astype(F32)[:, None] * inv[None, :]
    cos, sin = jnp.cos(ang), jnp.sin(ang)
    n = pos.shape[0]
    zh = jnp.zeros((n, half), F32)
    rest = head_dim - r
    c = jnp.concatenate([cos, cos, jnp.ones((n, rest), F32)], axis=-1)
    s_lo = jnp.concatenate([-sin, zh, jnp.zeros((n, rest), F32)], axis=-1)
    s_hi = jnp.concatenate([zh, sin, jnp.zeros((n, rest), F32)], axis=-1)
    rep = LANES // head_dim
    return tuple(jnp.tile(t, (1, rep)) for t in (c, s_lo, s_hi))


QK_TM = 512


def qk_post(proj, table_pos, table_block, q_gain, k_gain, ik_gain):
    tm = QK_TM
    n_tok = proj.shape[0]
    t128 = _rope_tables(table_pos, HEAD_DIM)
    t64 = _rope_tables(table_pos, IDX_DIM)
    ik_gain128 = jnp.concatenate([ik_gain, jnp.zeros((LANES - IDX_DIM,), F32)])[None, :]

    def col(width, start):
        return pl.BlockSpec((tm, width), lambda i: (i, start // width))

    def row(width):
        return pl.BlockSpec((tm, width), lambda i: (i, 0))

    table = pl.BlockSpec((tm, LANES), lambda i: (table_block(i), 0))
    gain = pl.BlockSpec((1, LANES), lambda i: (0, 0))
    return pl.pallas_call(
        _qk_post_kernel,
        grid=(n_tok // tm,),
        in_specs=[col(ATTN_WIDTH, COL_Q), col(KV_WIDTH, COL_K), col(KV_WIDTH, COL_V), col(IDX_HEADS * IDX_DIM, COL_QI),
                  col(LANES, COL_KIWI)] + [table] * 6 + [gain] * 3,
        out_specs=[row(ATTN_WIDTH), row(KV_WIDTH), row(KV_WIDTH), row(KV_WIDTH), row(N_KV_HEADS * V_AUG), row(IDX_HEADS * LANES),
                   row(IDX_DIM), row(LANES), row(LANES)],
        out_shape=[jax.ShapeDtypeStruct((n_tok, ATTN_WIDTH), BF16),
                   jax.ShapeDtypeStruct((n_tok, KV_WIDTH), F32), jax.ShapeDtypeStruct((n_tok, KV_WIDTH), BF16),
                   jax.ShapeDtypeStruct((n_tok, KV_WIDTH), F32), jax.ShapeDtypeStruct((n_tok, N_KV_HEADS * V_AUG), BF16),
                   jax.ShapeDtypeStruct((n_tok, IDX_HEADS * LANES), BF16),
                   jax.ShapeDtypeStruct((n_tok, IDX_DIM), F32), jax.ShapeDtypeStruct((n_tok, LANES), BF16),
                   jax.ShapeDtypeStruct((n_tok, LANES), F32)],
        compiler_params=_params(("arbitrary",)),
        name="qk_post",
    )(proj, proj, proj, proj, proj, *t128, *t64, q_gain[None, :], k_gain[None, :], ik_gain128)


def _gelu_tanh(x):
    return 0.5 * x * (1.0 + jnp.tanh(np.float32(np.sqrt(2.0 / np.pi)) * (x + 0.044715 * (x * x * x))))


SSM_LT = SSM_SB // LANES
SSM_SEG = 256


def _ssm_kernel(u_ref, wb_ref, wc_ref, pw_ref, d_ref, h0_ref, g_ref, sre_ref, sim_ref,
                er_ref, ei_ref, car_ref, up_ref, yp_ref):
    c = pl.program_id(2)

    @pl.when(c == 0)
    def _():
        car_ref[...] = h0_ref[...]

    for j in range(SSM_SEG):
        up_ref[j * SUBLANES:(j + 1) * SUBLANES, :] = u_ref[pl.ds(j, SUBLANES, stride=SSM_SEG), :]
    e = _dot(up_ref[...].astype(BF16), wb_ref[...])
    tiles = [slice(lt * LANES, (lt + 1) * LANES) for lt in range(SSM_LT)]
    for lt, sl in enumerate(tiles):
        er_ref[lt] = e[:, sl]
        ei_ref[lt] = e[:, SSM_SB + lt * LANES:SSM_SB + (lt + 1) * LANES]

    def cmul_add(ar, ai, br, bi, cr, ci):
        return ar * br - ai * bi + cr, ar * bi + ai * br + ci

    lb = [(pw_ref[0, 0:1, sl], pw_ref[1, 0:1, sl]) for sl in tiles]
    zero = jnp.zeros((SUBLANES, LANES), F32)
    st = [(zero, zero)] * SSM_LT
    for j in range(SSM_SEG):
        rows = slice(j * SUBLANES, (j + 1) * SUBLANES)
        for lt in range(SSM_LT):
            st[lt] = cmul_add(*lb[lt], *st[lt], er_ref[lt, rows, :], ei_ref[lt, rows, :])
            er_ref[lt, rows, :] = st[lt][0]
            ei_ref[lt, rows, :] = st[lt][1]

    enter = []
    for lt, sl in enumerate(tiles):
        seg_r, seg_i = pw_ref[0, SSM_SEG - 1:SSM_SEG, sl], pw_ref[1, SSM_SEG - 1:SSM_SEG, sl]
        cr, ci = car_ref[0:1, sl], car_ref[1:2, sl]
        rows_r, rows_i = [], []
        for r in range(SUBLANES):
            rows_r.append(cr)
            rows_i.append(ci)
            cr, ci = cmul_add(seg_r, seg_i, cr, ci, st[lt][0][r:r + 1], st[lt][1][r:r + 1])
        car_ref[0:1, sl] = cr
        car_ref[1:2, sl] = ci
        enter.append((jnp.concatenate(rows_r, axis=0), jnp.concatenate(rows_i, axis=0)))

    for j in range(SSM_SEG):
        rows = slice(j * SUBLANES, (j + 1) * SUBLANES)
        for lt, sl in enumerate(tiles):
            xr, xi = cmul_add(pw_ref[0, j:j + 1, sl], pw_ref[1, j:j + 1, sl], *enter[lt],
                              er_ref[lt, rows, :], ei_ref[lt, rows, :])
            er_ref[lt, rows, :] = xr
            ei_ref[lt, rows, :] = xi

    y = None
    for lt, sl in enumerate(tiles):
        t = _dot(er_ref[lt].astype(BF16), wc_ref[0, sl, :]) - _dot(ei_ref[lt].astype(BF16), wc_ref[1, sl, :])
        y = t if y is None else y + t
    yp_ref[...] = y
    out_rows = 2 * SUBLANES
    for t0 in range(0, SUBLANES * SSM_SEG, out_rows):
        r, j0 = divmod(t0, SSM_SEG)
        rows = slice(t0, t0 + out_rows)
        yt = yp_ref[pl.ds(j0 * SUBLANES + r, out_rows, stride=SUBLANES), :] + d_ref[...] * u_ref[rows, :]
        g_ref[rows, :] = _gelu_tanh(yt).astype(BF16)

    @pl.when(c == pl.num_programs(2) - 1)
    def _():
        sre_ref[...] = car_ref[0:1, :]
        sim_ref[...] = car_ref[1:2, :]


def _ssm_weights(a_re, a_im, log_dt, b_re, b_im, c_re, c_im):
    lam_re, lam_im = a_re, a_im
    dt = jnp.exp(log_dt)[:, None]
    mag = jnp.exp(lam_re * dt)
    lb_re, lb_im = mag * jnp.cos(lam_im * dt), mag * jnp.sin(lam_im * dt)
    den = lam_re * lam_re + lam_im * lam_im
    num_re = lb_re - 1.0
    z_re = (num_re * lam_re + lb_im * lam_im) / den
    z_im = (lb_im * lam_re - num_re * lam_im) / den
    zb_re = z_re[:, :, None] * b_re - z_im[:, :, None] * b_im
    zb_im = z_re[:, :, None] * b_im + z_im[:, :, None] * b_re
    eye = jnp.eye(8, dtype=F32)

    def blockdiag_in(w):
        return jnp.einsum('jgph,gk->jghkp', w.reshape(SSM_LB, 8, SSM_STATE, SSM_GROUP), eye).reshape(SSM_LB, LANES, SSM_SB)

    def blockdiag_out(w):
        return jnp.einsum('jghp,gk->jkpgh', w.reshape(SSM_LB, 8, SSM_GROUP, SSM_STATE), eye).reshape(SSM_LB, SSM_SB, LANES)

    wb = jnp.concatenate([blockdiag_in(zb_re), blockdiag_in(zb_im)], axis=-1).astype(BF16)
    wc = jnp.stack([blockdiag_out(c_re), blockdiag_out(c_im)], axis=1).astype(BF16)

    pr, pi_ = lb_re.reshape(SSM_LB, 1, SSM_SB), lb_im.reshape(SSM_LB, 1, SSM_SB)
    while pr.shape[1] < SSM_SEG:
        tr, ti = pr[:, -1:], pi_[:, -1:]
        pr, pi_ = (jnp.concatenate([pr, pr * tr - pi_ * ti], axis=1), jnp.concatenate([pi_, pr * ti + pi_ * tr], axis=1))
    pw = jnp.stack([pr, pi_], axis=1)
    return wb, wc, pw


def ssm(proj, ssm_w, d_skip, h0, *, n_batch, seq, row0):
    wb, wc, pw = ssm_w
    tc = SUBLANES * SSM_SEG
    n_chunks = seq // tc
    blk0 = row0 // tc
    n_tok = n_batch * seq
    state_shape = jax.ShapeDtypeStruct((n_batch, SSM_LB, 1, SSM_SB), F32)
    state_spec = pl.BlockSpec((None, None, 1, SSM_SB), lambda b, j, c: (b, j, 0, 0))
    g, s_re, s_im = pl.pallas_call(
        _ssm_kernel,
        grid=(n_batch, SSM_LB, n_chunks),
        in_specs=[pl.BlockSpec((tc, LANES), lambda b, j, c: (blk0 + b * n_chunks + c, j)),
                  pl.BlockSpec((None, LANES, 2 * SSM_SB), lambda b, j, c: (j, 0, 0)),
                  pl.BlockSpec((None, 2, SSM_SB, LANES), lambda b, j, c: (j, 0, 0, 0)),
                  pl.BlockSpec((None, 2, SSM_SEG, SSM_SB), lambda b, j, c: (j, 0, 0, 0)),
                  pl.BlockSpec((1, LANES), lambda b, j, c: (0, j)),
                  pl.BlockSpec((None, None, 2, SSM_SB), lambda b, j, c: (b, j, 0, 0))],
        out_specs=[pl.BlockSpec((tc, LANES), lambda b, j, c: (b * n_chunks + c, j)), state_spec, state_spec],
        out_shape=[jax.ShapeDtypeStruct((n_tok, SSM_WIDTH), BF16), state_shape, state_shape],
        scratch_shapes=[pltpu.VMEM((SSM_LT, tc, LANES), F32), pltpu.VMEM((SSM_LT, tc, LANES), F32),
                        pltpu.VMEM((2, SSM_SB), F32), pltpu.VMEM((tc, LANES), F32), pltpu.VMEM((tc, LANES), F32)],
        compiler_params=_params(("arbitrary", "arbitrary", "arbitrary")),
        name="ssm",
    )(proj, wb, wc, pw, d_skip[None, :], h0)
    return g, s_re.reshape(n_batch, SSM_GROUPS, SSM_STATE), s_im.reshape(n_batch, SSM_GROUPS, SSM_STATE)


def _ssm_step_kernel(u_ref, wb_ref, wc_ref, pw_ref, d_ref, h0_ref, g_ref, sre_ref, sim_ref, er_ref, ei_ref, *, seq):
    n_seq = h0_ref.shape[1]
    u = u_ref[...]
    e = _dot(u.astype(BF16), wb_ref[...])
    n_lt = SSM_SB // LANES
    y = d_ref[...] * u
    for lt in range(n_lt):
        sl = slice(lt * LANES, (lt + 1) * LANES)
        er_ref[...] = e[:, lt * LANES:(lt + 1) * LANES]
        ei_ref[...] = e[:, SSM_SB + lt * LANES:SSM_SB + (lt + 1) * LANES]
        lr, li = pw_ref[0, 0:1, sl], pw_ref[1, 0:1, sl]
        sr, si = h0_ref[0, :, sl], h0_ref[1, :, sl]
        for t in range(seq):
            rows = pl.ds(t, n_seq, stride=seq)
            sr, si = lr * sr - li * si + er_ref[rows, :], lr * si + li * sr + ei_ref[rows, :]
            er_ref[rows, :] = sr
            ei_ref[rows, :] = si
        y = y + (_dot(er_ref[...].astype(BF16), wc_ref[0, sl, :]) - _dot(ei_ref[...].astype(BF16), wc_ref[1, sl, :]))
        sre_ref[:, sl] = sr
        sim_ref[:, sl] = si
    g_ref[...] = _gelu_tanh(y).astype(BF16)


def ssm_step(proj, ssm_w, d_skip, h0, *, n_batch, seq, row0):
    wb, wc, pw = ssm_w
    n_tok = n_batch * seq
    assert row0 % n_tok == 0
    state_shape = jax.ShapeDtypeStruct((SSM_LB, n_batch, SSM_SB), F32)
    state_spec = pl.BlockSpec((None, n_batch, SSM_SB), lambda j: (j, 0, 0))
    g, s_re, s_im = pl.pallas_call(
        functools.partial(_ssm_step_kernel, seq=seq),
        grid=(SSM_LB,),
        in_specs=[pl.BlockSpec((n_tok, LANES), lambda j: (row0 // n_tok, j)),
                  pl.BlockSpec((None, LANES, 2 * SSM_SB), lambda j: (j, 0, 0)),
                  pl.BlockSpec((None, 2, SSM_SB, LANES), lambda j: (j, 0, 0, 0)),
                  pl.BlockSpec((None, 2, SSM_SEG, SSM_SB), lambda j: (j, 0, 0, 0)),
                  pl.BlockSpec((1, LANES), lambda j: (0, j)),
                  pl.BlockSpec((None, 2, n_batch, SSM_SB), lambda j: (j, 0, 0, 0))],
        out_specs=[pl.BlockSpec((n_tok, LANES), lambda j: (0, j)), state_spec, state_spec],
        out_shape=[jax.ShapeDtypeStruct((n_tok, SSM_WIDTH), BF16), state_shape, state_shape],
        scratch_shapes=[pltpu.VMEM((n_tok, LANES), F32), pltpu.VMEM((n_tok, LANES), F32)],
        compiler_params=_params(("arbitrary",)),
        name="ssm_step",
    )(proj, wb, wc, pw, d_skip[None, :], h0)

    def per_seq(s):
        return s.transpose(1, 0, 2).reshape(n_batch, SSM_GROUPS, SSM_STATE)

    return g, per_seq(s_re), per_seq(s_im)


def _row_sum(x):
    return jnp.sum(x, axis=1, keepdims=True)


def _row_count(mask):
    return _row_sum(jnp.where(mask, 1, 0))


I16 = jnp.int16
I16_MIN = -2 ** 15


def _count16(ref, cand, compare):
    accs = [None] * 4
    for t in range(ref.shape[1] // LANES):
        x = jnp.where(compare(ref[:, t * LANES:(t + 1) * LANES], cand), I16(1), I16(0))
        accs[t % 4] = x if accs[t % 4] is None else accs[t % 4] + x
    accs = [a for a in accs if a is not None]
    total = accs[0]
    for a in accs[1:]:
        total = total + a
    return _row_sum(total.astype(I32))


def _bisect16(ref, target):
    def step(i, base):
        cand = base + lax.shift_left(np.int32(1), np.int32(15) - i)
        cnt = _count16(ref, cand.astype(I16), lambda a, b: a >= b)
        return jnp.where(cnt >= target, cand, base)
    return lax.fori_loop(0, 16, step, jnp.full((ref.shape[0], 1), I16_MIN, I32))


def _bisect32(key_ref, n_sel):
    bq, n_keys = key_ref.shape
    hr = bq // 2

    def lane_counts(h, cand):
        accs = [None] * 4
        for t in range(n_keys // LANES):
            x = jnp.where(key_ref[h * hr:(h + 1) * hr, t * LANES:(t + 1) * LANES] >= cand, 1, 0)
            accs[t % 4] = x if accs[t % 4] is None else accs[t % 4] + x
        accs = [a for a in accs if a is not None]
        total = accs[0]
        for a in accs[1:]:
            total = total + a
        return total

    def decide(part, cand, base):
        return jnp.where(_row_sum(part) >= n_sel, cand, base)

    def bit(i):
        return lax.shift_left(np.int32(1), np.int32(31) - i)

    def body(i, state):
        base_a, base_b, part_b = state
        cand_a = base_a + bit(i)
        part_a = lane_counts(0, cand_a)
        base_b = decide(part_b, base_b + bit(i - 1), base_b)
        part_b = lane_counts(1, base_b + bit(i))
        return decide(part_a, cand_a, base_a), base_b, part_b

    base0 = jnp.full((hr, 1), INT_MIN, I32)
    first = base0 + bit(0)
    state = (decide(lane_counts(0, first), first, base0), base0, lane_counts(1, first))
    base_a, base_b, part_b = lax.fori_loop(1, 32, body, state)
    base_b = decide(part_b, base_b + bit(31), base_b)
    return jnp.concatenate([base_a, base_b], axis=0)


def _stack_heads(ref, heads):
    return jnp.concatenate([ref[:, h * LANES:(h + 1) * LANES] for h in heads], axis=0)


def _dsa_body(q_ref, qi_ref, wi_ref, k_ref, v_ref, ki_ref, o_ref, key_ref, bias_ref, hi_ref, lo_ref, p_ref,
              *, q_pos_first, s_valid, n_sel, packed_bisect, stack):
    bq, n_keys = key_ref.shape
    col = lax.broadcasted_iota(I32, (bq, n_keys), 1)
    qpos = q_pos_first + lax.broadcasted_iota(I32, (bq, 1), 0)
    allowed = col < jnp.minimum((qpos // CHUNK + 1) * CHUNK, s_valid)

    ki = ki_ref[...]
    score = None
    for h0 in range(0, IDX_HEADS, stack):
        d = _dot_nt(_stack_heads(qi_ref, range(h0, h0 + stack)), ki)
        for j in range(stack):
            t = jnp.maximum(d[j * bq:(j + 1) * bq], 0.0) * wi_ref[:, h0 + j:h0 + j + 1]
            score = t if score is None else score + t
    score = jnp.where(score == 0.0, 0.0, score)
    bits = pltpu.bitcast(score, I32)
    key = jnp.where(bits < 0, bits ^ np.int32(0x7FFFFFFF), bits)
    key = jnp.where(allowed, key, KEY_NEG_INF)
    key_ref[...] = key

    if packed_bisect:
        hi_ref[...] = (key >> 16).astype(I16)
        lo_ref[...] = ((key & 0xFFFF) + I16_MIN).astype(I16)
        thr_hi = _bisect16(hi_ref, n_sel)
        thr_hi16 = thr_hi.astype(I16)
        need_lo = n_sel - _count16(hi_ref, thr_hi16, lambda a, b: a > b)
        lo_ref[...] = jnp.where(hi_ref[...] == thr_hi16, lo_ref[...], I16(I16_MIN))
        thr_lo = _bisect16(lo_ref, need_lo)
        thr = lax.shift_left(thr_hi, np.int32(16)) + (thr_lo - I16_MIN)
    else:
        thr = _bisect32(key_ref, n_sel)
    thr = jnp.maximum(thr, KEY_NEG_INF)

    key = key_ref[...]
    need = n_sel - _row_count(key > thr)
    n_eq = _row_count(key == thr)
    n_bits = int(n_keys - 1).bit_length()

    def tie_cut():
        def step(i, j0):
            cand = j0 + lax.shift_left(np.int32(1), np.int32(n_bits - 1) - i)
            cnt = _row_sum(jnp.where(key_ref[...] == thr, jnp.where(col < cand, 1, 0), 0))
            return jnp.where(cnt < need, cand, j0)
        return lax.fori_loop(0, n_bits, step, jnp.zeros((bq, 1), I32))

    split = jnp.max(jnp.where(n_eq > need, 1, 0)) > 0
    j_last = lax.cond(split, tie_cut, lambda: jnp.full((bq, 1), n_keys, I32))
    tie_bias = jnp.where(thr == KEY_NEG_INF, -jnp.inf, 0.0)
    bias_ref[...] = jnp.where(key > thr, 0.0,
                              jnp.where(key == thr, jnp.where(col <= j_last, tie_bias, -jnp.inf), -jnp.inf))

    c = np.float32(HEAD_DIM ** -0.5 * np.log2(np.e))
    for h0 in range(0, N_HEADS, stack):
        kv = h0 // KV_GROUP
        heads = range(h0, h0 + stack)
        s_all = _dot_nt(_stack_heads(q_ref, heads), k_ref[:, kv * HEAD_DIM:(kv + 1) * HEAD_DIM])
        for g in range(stack):
            s = s_all[g * bq:(g + 1) * bq] + bias_ref[...]
            m = jnp.max(s, axis=1, keepdims=True)
            p_ref[g * bq:(g + 1) * bq, :] = jnp.exp2((s - m) * c).astype(BF16)
        pv = _dot(p_ref[0:stack * bq, :], v_ref[:, kv * V_AUG:(kv + 1) * V_AUG])
        for g, h in enumerate(heads):
            o = pv[g * bq:(g + 1) * bq]
            o_ref[:, h * HEAD_DIM:(h + 1) * HEAD_DIM] = (o[:, :HEAD_DIM] / o[:, HEAD_DIM:HEAD_DIM + 1]).astype(BF16)


def _dsa_scratch(bq, n_keys):
    return [pltpu.VMEM((bq, n_keys), I32), pltpu.VMEM((bq, n_keys), F32),
            pltpu.VMEM((bq, n_keys), I16), pltpu.VMEM((bq, n_keys), I16), pltpu.VMEM((KV_GROUP * bq, n_keys), BF16)]


def _dsa_kernel(q_ref, qi_ref, wi_ref, k_ref, v_ref, ki_ref, o_ref, *scratch, q_pos0, **static):
    bq = scratch[0].shape[0]
    _dsa_body(q_ref, qi_ref, wi_ref, k_ref, v_ref, ki_ref, o_ref, *scratch,
              q_pos_first=q_pos0 + pl.program_id(1) * bq, **static)


def dsa(q, qi, wi, k, v, ki, *, bq, q_blk0, n_qblk, n_keys, n_sel, packed_bisect, stack):
    n_batch, seq = q.shape[:2]

    def qspec(width):
        return pl.BlockSpec((None, bq, width), lambda b, i: (b, q_blk0 + i, 0))

    def kspec(width):
        return pl.BlockSpec((None, n_keys, width), lambda b, i: (b, 0, 0))

    return pl.pallas_call(
        functools.partial(_dsa_kernel, q_pos0=q_blk0 * bq, s_valid=seq, n_sel=n_sel, packed_bisect=packed_bisect,
                          stack=stack),
        grid=(n_batch, n_qblk),
        in_specs=[qspec(ATTN_WIDTH), qspec(IDX_HEADS * LANES), qspec(LANES), kspec(KV_WIDTH), kspec(N_KV_HEADS * V_AUG),
                  kspec(LANES)],
        out_specs=pl.BlockSpec((None, bq, ATTN_WIDTH), lambda b, i: (b, i, 0)),
        out_shape=jax.ShapeDtypeStruct((n_batch, n_qblk * bq, ATTN_WIDTH), BF16),
        scratch_shapes=_dsa_scratch(bq, n_keys),
        compiler_params=_params(("arbitrary", "arbitrary")),
        name="dsa",
    )(q, qi, wi, k, v, ki)


def _dsa_step_kernel(q_ref, qi_ref, wi_ref, ck_hbm, cv_hbm, cki_ref, nk_ref, nv_ref, nki_ref, o_ref,
                     k_buf, v_buf, ki_buf, cache_buf, sem, *scratch, past, n_sel):
    b = pl.program_id(0)

    def cache_copies(seq, slot):
        return [pltpu.make_async_copy(src.at[seq, :, h, :], cache_buf.at[slot, a, h], sem.at[slot])
                for a, src in enumerate((ck_hbm, cv_hbm)) for h in range(N_KV_HEADS)]

    @pl.when(b == 0)
    def _():
        for cp in cache_copies(0, 0):
            cp.start()

    @pl.when(b + 1 < pl.num_programs(0))
    def _():
        for cp in cache_copies(b + 1, (b + 1) % 2):
            cp.start()

    slot = b % 2
    for cp in cache_copies(b, slot):
        cp.wait()

    ts = nk_ref.shape[0]
    n_keys = k_buf.shape[0]
    for h in range(N_KV_HEADS):
        k_buf[0:past, h * HEAD_DIM:(h + 1) * HEAD_DIM] = cache_buf[slot, 0, h].astype(BF16)
    _store_v_aug(v_buf, 0, [cache_buf[slot, 1, h] for h in range(N_KV_HEADS)])
    for buf, new in ((k_buf, nk_ref), (v_buf, nv_ref)):
        buf[past:past + ts, :] = new[...]
        buf[past + ts:n_keys, :] = jnp.zeros((n_keys - past - ts, buf.shape[1]), BF16)
    ki_buf[0:past, 0:IDX_DIM] = cki_ref[...].astype(BF16)
    ki_buf[0:past, IDX_DIM:LANES] = jnp.zeros((past, LANES - IDX_DIM), BF16)
    ki_buf[past:past + ts, :] = nki_ref[...]
    ki_buf[past + ts:n_keys, :] = jnp.zeros((n_keys - past - ts, LANES), BF16)
    _dsa_body(q_ref, qi_ref, wi_ref, k_buf, v_buf, ki_buf, o_ref, *scratch,
              q_pos_first=past, s_valid=past + ts, n_sel=n_sel, packed_bisect=True, stack=KV_GROUP)


def dsa_step(q, qi, wi, cache_k, cache_v, cache_ki, k_new, v_new, ki_new, *, n_sel):
    n_batch, ts = q.shape[:2]
    past = cache_k.shape[1]
    n_keys = -(-(past + ts) // LANES) * LANES

    def spec(rows, width):
        return pl.BlockSpec((None, rows, width), lambda b: (b, 0, 0))

    return pl.pallas_call(
        functools.partial(_dsa_step_kernel, past=past, n_sel=n_sel),
        grid=(n_batch,),
        in_specs=[spec(ts, ATTN_WIDTH), spec(ts, IDX_HEADS * LANES), spec(ts, LANES),
                  pl.BlockSpec(memory_space=pl.ANY), pl.BlockSpec(memory_space=pl.ANY), spec(past, IDX_DIM),
                  spec(ts, KV_WIDTH), spec(ts, N_KV_HEADS * V_AUG), spec(ts, LANES)],
        out_specs=spec(ts, ATTN_WIDTH),
        out_shape=jax.ShapeDtypeStruct((n_batch, ts, ATTN_WIDTH), BF16),
        scratch_shapes=[pltpu.VMEM((n_keys, KV_WIDTH), BF16), pltpu.VMEM((n_keys, N_KV_HEADS * V_AUG), BF16),
                        pltpu.VMEM((n_keys, LANES), BF16),
                        pltpu.VMEM((2, 2, N_KV_HEADS, past, HEAD_DIM), F32), pltpu.SemaphoreType.DMA((2,)),
                        *_dsa_scratch(ts, n_keys)],
        compiler_params=_params(("arbitrary",)),
        name="dsa_step",
    )(q, qi, wi, cache_k, cache_v, cache_ki, k_new, v_new, ki_new)


def _merge_kernel(g_ref, a_ref, ga_ref, gb_ref, wv_ref, wg_ref, wb_ref, o_ref):
    g = g_ref[...]
    branch_a = _dot(g, wv_ref[...]) * jax.nn.sigmoid(_dot(g, wg_ref[...]))
    branch_b = _dot(a_ref[...], wb_ref[...])
    merged = jax.nn.sigmoid(ga_ref[...]) * branch_a + jax.nn.sigmoid(gb_ref[...]) * branch_b
    o_ref[...] = merged.astype(BF16)


def merge(g, attn, proj, w_val, w_gate, w_branch, *, tm=1024, tn=512):
    n_tok = g.shape[0]
    nj = D_MODEL // tn

    def wspec():
        return pl.BlockSpec((SSM_WIDTH, tn), lambda i, j: (0, j))

    return pl.pallas_call(
        _merge_kernel,
        grid=(n_tok // tm, nj),
        in_specs=[pl.BlockSpec((tm, SSM_WIDTH), lambda i, j: (i, 0)),
                  pl.BlockSpec((tm, ATTN_WIDTH), lambda i, j: (i, 0)),
                  pl.BlockSpec((tm, tn), lambda i, j: (i, COL_GA // tn + j)),
                  pl.BlockSpec((tm, tn), lambda i, j: (i, COL_GB // tn + j)),
                  wspec(), wspec(), wspec()],
        out_specs=pl.BlockSpec((tm, tn), lambda i, j: (i, j)),
        out_shape=jax.ShapeDtypeStruct((n_tok, D_MODEL), BF16),
        compiler_params=_params(("arbitrary", "arbitrary")),
        name="merge",
    )(g, attn, proj, proj, w_val, w_gate, w_branch)


ROUTER_COLS = N_EXPERT_GROUPS + N_EXPERTS
MOE_TM = 256


def _first_lane_of_max(x, lane_f):
    m = jnp.max(x, axis=1, keepdims=True)
    return m, jnp.min(jnp.where(x == m, lane_f, float(LANES)), axis=1, keepdims=True)


def _out_proj_kernel(x_ref, m_ref, wo_ref, gn_ref, wr_ref, br_ref, cin_ref,
                     h_ref, hn_ref, ri_ref, rw_ref, cnt_ref, carry_ref):
    @pl.when(pl.program_id(0) == 0)
    def _():
        carry_ref[...] = cin_ref[...]

    h = x_ref[...] + _dot(m_ref[...], wo_ref[...])
    h_ref[...] = h
    ms = jnp.mean(h * h, axis=-1, keepdims=True)
    hn = h * lax.rsqrt(ms + EPS) * gn_ref[...]
    hn_ref[...] = hn
    hh, hl = _split_bf16(hn)
    both = _dot(hh, wr_ref[...])
    lg = both[:, :LANES] + _dot(hl, wr_ref[:, 0:LANES]) + both[:, LANES:] + br_ref[...]

    tm = lg.shape[0]
    lane = lax.broadcasted_iota(I32, lg.shape, 1)
    lane_f = lane.astype(F32)
    ninf = -jnp.inf
    gl = jnp.where(lane < N_EXPERT_GROUPS, lg, ninf)
    gmax, gsel = _first_lane_of_max(gl, lane_f)
    g_w = 1.0 / jnp.sum(jnp.exp(gl - gmax), axis=1, keepdims=True)
    lo = N_EXPERT_GROUPS + EXPERTS_PER_GROUP * gsel
    el = jnp.where(lane_f >= lo, jnp.where(lane_f < lo + EXPERTS_PER_GROUP, lg, ninf), ninf)
    v1, i1 = _first_lane_of_max(el, lane_f)
    el2 = jnp.where(lane_f == i1, ninf, el)
    v2, i2 = _first_lane_of_max(el2, lane_f)
    t = jnp.exp(v2 - v1)
    s1 = 1.0 / (1.0 + t)
    w1 = s1 * g_w
    w2 = (t * s1) * g_w

    m1 = jnp.where(lane_f == i1, 1.0, 0.0)
    m2 = jnp.where(lane_f == i2, 1.0, 0.0)
    both = m1 + m2
    tri = jnp.where(lax.broadcasted_iota(I32, (tm, tm), 0) > lax.broadcasted_iota(I32, (tm, tm), 1), 1.0, 0.0)
    before = _dot(tri.astype(BF16), both.astype(BF16)) + carry_ref[...]
    r1 = jnp.sum(before * m1, axis=1, keepdims=True)
    r2 = jnp.sum(before * m2, axis=1, keepdims=True)
    carry_ref[...] = carry_ref[...] + jnp.sum(both, axis=0, keepdims=True)
    cnt_ref[...] = carry_ref[...]
    e1 = i1 - float(N_EXPERT_GROUPS)
    e2 = i2 - float(N_EXPERT_GROUPS)
    fields = jnp.where(lane == 0, e1, jnp.where(lane == 1, e2, jnp.where(lane == 2, r1, jnp.where(lane == 3, r2, 0.0))))
    ri_ref[...] = fields.T[0:SUBLANES, :].astype(I32)
    rw_ref[...] = jnp.where(lane == 0, w1, jnp.where(lane == 1, w2, 0.0))


def _router_weights(w_router_group, b_router_group, w_router_expert, b_router_expert):
    wr = jnp.concatenate([w_router_group, w_router_expert, jnp.zeros((D_MODEL, LANES - ROUTER_COLS), F32)], axis=1)
    wr_hi = wr.astype(BF16)
    wr_lo = (wr - wr_hi.astype(F32)).astype(BF16)
    br = jnp.concatenate([b_router_group, b_router_expert, jnp.zeros((LANES - ROUTER_COLS,), F32)])[None, :]
    return jnp.concatenate([wr_hi, wr_lo], axis=1), br


def out_proj(x, merged, w_out, ffn_gain, router_w, counts_in, *, tm=512):
    n_tok = x.shape[0]
    wr, br = router_w

    def row(width):
        return pl.BlockSpec((tm, width), lambda i: (i, 0))

    def const(shape):
        return pl.BlockSpec(shape, lambda i: (0, 0), pipeline_mode=pl.Buffered(1))

    return pl.pallas_call(
        _out_proj_kernel,
        grid=(n_tok // tm,),
        in_specs=[row(D_MODEL), row(D_MODEL), const((D_MODEL, D_MODEL)), const((1, D_MODEL)),
                  const((D_MODEL, 2 * LANES)), const((1, LANES)), const((1, LANES))],
        out_specs=[row(D_MODEL), row(D_MODEL), pl.BlockSpec((SUBLANES, tm), lambda i: (0, i)), row(LANES),
                   pl.BlockSpec((1, LANES), lambda i: (0, 0))],
        out_shape=[jax.ShapeDtypeStruct((n_tok, D_MODEL), F32), jax.ShapeDtypeStruct((n_tok, D_MODEL), F32),
                   jax.ShapeDtypeStruct((SUBLANES, n_tok), I32), jax.ShapeDtypeStruct((n_tok, LANES), F32),
                   jax.ShapeDtypeStruct((1, LANES), F32)],
        scratch_shapes=[pltpu.VMEM((1, LANES), F32)],
        compiler_params=_params(("arbitrary",)),
        name="out_proj",
    )(x, merged, w_out, ffn_gain[None, :], wr, br, counts_in)


def _block_layout(counts):
    padded = (counts + MOE_TM - 1) // MOE_TM * MOE_TM
    pad_end = jnp.cumsum(padded).astype(I32)
    pad_start = pad_end - padded
    n_used = pad_end[-1] // MOE_TM
    return pad_start, pad_end, n_used


def _dest_kernel(ps_ref, ri_ref, o_ref):
    ri = ri_ref[...]
    start = jnp.zeros_like(ri)
    for k in range(N_EXPERTS):
        start = jnp.where(ri == k, ps_ref[k], start)
    o_ref[...] = start + pltpu.roll(ri, SUBLANES - TOP_K, 0)


def dest_rows(route_i, pad_start):
    grid_spec = pltpu.PrefetchScalarGridSpec(
        num_scalar_prefetch=1, grid=(1,),
        in_specs=[pl.BlockSpec(route_i.shape, lambda i, ps: (0, 0))],
        out_specs=pl.BlockSpec(route_i.shape, lambda i, ps: (0, 0)))
    return pl.pallas_call(_dest_kernel, grid_spec=grid_spec, out_shape=jax.ShapeDtypeStruct(route_i.shape, I32),
                          compiler_params=_params(("arbitrary",)), name="dest_rows")(pad_start, route_i)


def _moe_rows(n_tok):
    return -(-(n_tok * TOP_K + N_EXPERTS * (MOE_TM - 1)) // MOE_TM) * MOE_TM


DISPATCH_TM = 1024


def _wait_rows(src_hbm, dst, sem, n_rows):
    pltpu.make_async_copy(src_hbm.at[pl.ds(0, n_rows)], dst, sem).wait()


def _dispatch_kernel(d0_ref, d1_ref, pe_ref, cnt_ref, nu_ref, hna_ref, hnb_ref, xs_hbm, zbuf, sem, semz,
                     *, n_blocks, a_tiles):
    i = pl.program_id(0)

    def zero_block(row0):
        return pltpu.make_async_copy(zbuf, xs_hbm.at[pl.ds(pl.multiple_of(row0, MOE_TM), MOE_TM)], semz)

    @pl.when(i == 0)
    def _():
        zbuf[...] = jnp.zeros_like(zbuf)
        for start in (True, False):
            for e in range(N_EXPERTS):
                @pl.when(cnt_ref[e] > 0)
                def _():
                    cp = zero_block(pe_ref[e] - MOE_TM)
                    cp.start() if start else cp.wait()

            def tail(b, c):
                cp = zero_block(b * MOE_TM)
                cp.start() if start else cp.wait()
                return c
            lax.fori_loop(nu_ref[0], n_blocks, tail, 0)

    base = i * DISPATCH_TM

    def scatter(hn_ref):
        def body(r, c):
            src = hn_ref.at[pl.ds(r, 1)]
            pltpu.make_async_copy(src, xs_hbm.at[pl.ds(d0_ref[base + r], 1)], sem).start()
            pltpu.make_async_copy(src, xs_hbm.at[pl.ds(d1_ref[base + r], 1)], sem).start()
            return c
        lax.fori_loop(0, DISPATCH_TM, body, 0, unroll=8)
        for _ in range(TOP_K):
            pltpu.make_async_copy(hn_ref, xs_hbm.at[pl.ds(0, DISPATCH_TM)], sem).wait()

    @pl.when(i < a_tiles)
    def _():
        scatter(hna_ref)

    @pl.when(i >= a_tiles)
    def _():
        scatter(hnb_ref)


def dispatch(hn_a, hn_b, dest0, dest1, pad_end, counts, n_used):
    a_tiles, b_tiles = hn_a.shape[0] // DISPATCH_TM, hn_b.shape[0] // DISPATCH_TM
    rows = _moe_rows(hn_a.shape[0] + hn_b.shape[0])
    grid_spec = pltpu.PrefetchScalarGridSpec(
        num_scalar_prefetch=5,
        grid=(a_tiles + b_tiles,),
        in_specs=[pl.BlockSpec((DISPATCH_TM, D_MODEL), lambda i, *_: (jnp.minimum(i, a_tiles - 1), 0)),
                  pl.BlockSpec((DISPATCH_TM, D_MODEL), lambda i, *_: (jnp.maximum(i - a_tiles, 0), 0))],
        out_specs=pl.BlockSpec(memory_space=pl.ANY),
        scratch_shapes=[pltpu.VMEM((MOE_TM, D_MODEL), F32), pltpu.SemaphoreType.DMA(()), pltpu.SemaphoreType.DMA(())],
    )
    return pl.pallas_call(
        functools.partial(_dispatch_kernel, n_blocks=rows // MOE_TM, a_tiles=a_tiles),
        grid_spec=grid_spec,
        out_shape=jax.ShapeDtypeStruct((rows, D_MODEL), F32),
        compiler_params=_params(("arbitrary",)),
        name="dispatch",
    )(dest0, dest1, pad_end, counts, n_used, hn_a, hn_b)


MOE_UNITS = 8
MOE_UG = D_MODEL // MOE_UNITS
MOE_UD = EXPERT_FF // MOE_UNITS


def _moe_kernel(blk_e_ref, nu_ref, nxt_ref, upb_ref, xs_ref, wg_hbm, wu_hbm, wd_hbm, ys_ref,
                wg_bf, wu_bf, wd_bf, stg_g, stg_u, stg_d, sem, st_ref):
    i = pl.program_id(0)
    cur_slot, pos, cur_e = 0, 1, 2

    def unit_copies(e, unit, s):
        g_rows = pl.ds(pl.multiple_of(unit * MOE_UG, MOE_UG), MOE_UG)
        d_rows = pl.ds(pl.multiple_of(unit * MOE_UD, MOE_UD), MOE_UD)
        return (pltpu.make_async_copy(wg_hbm.at[e, g_rows, :], stg_g.at[s], sem.at[s]),
                pltpu.make_async_copy(wu_hbm.at[e, g_rows, :], stg_u.at[s], sem.at[s]),
                pltpu.make_async_copy(wd_hbm.at[e, d_rows, :], stg_d.at[s], sem.at[s]))

    def start_unit(e, unit):
        for cp in unit_copies(e, unit, unit % 2):
            cp.start()

    def begin_load(e):
        st_ref[pos] = 0
        start_unit(e, 0)
        start_unit(e, 1)

    def advance(e, slot, n):
        def body(_, c):
            unit = st_ref[pos]

            @pl.when(unit < MOE_UNITS)
            def _():
                s = unit % 2
                for cp in unit_copies(e, unit, s):
                    cp.wait()
                g_rows = pl.ds(pl.multiple_of(unit * MOE_UG, MOE_UG), MOE_UG)
                d_rows = pl.ds(pl.multiple_of(unit * MOE_UD, MOE_UD), MOE_UD)
                wg_bf[slot, g_rows, :] = stg_g[s].astype(BF16)
                wu_bf[slot, g_rows, :] = stg_u[s].astype(BF16)
                wd_bf[slot, d_rows, :] = stg_d[s].astype(BF16)

                @pl.when(unit + 2 < MOE_UNITS)
                def _():
                    start_unit(e, unit + 2)
                st_ref[pos] = unit + 1
            return c
        lax.fori_loop(0, n, body, 0)

    def load_next(nxt):
        @pl.when(nxt >= 0)
        def _():
            begin_load(nxt)

        @pl.when(nxt < 0)
        def _():
            st_ref[pos] = MOE_UNITS

    @pl.when(i < nu_ref[0])
    def _():
        e = blk_e_ref[i]
        nxt = nxt_ref[i]

        @pl.when(i == 0)
        def _():
            st_ref[cur_slot] = 0
            st_ref[cur_e] = e
            begin_load(e)
            advance(e, 0, MOE_UNITS)
            load_next(nxt)

        @pl.when(jnp.logical_and(i > 0, e != st_ref[cur_e]))
        def _():
            slot = 1 - st_ref[cur_slot]
            advance(e, slot, MOE_UNITS)
            st_ref[cur_slot] = slot
            st_ref[cur_e] = e
            load_next(nxt)

        slot = st_ref[cur_slot]
        x = xs_ref[...].astype(BF16)
        hg = _dot(x, wg_bf[slot])
        hu = _dot(x, wu_bf[slot])
        hmid = (jax.nn.silu(hg) * hu).astype(BF16)
        ys_ref[...] = _dot(hmid, wd_bf[slot])

        @pl.when(nxt >= 0)
        def _():
            advance(nxt, 1 - slot, upb_ref[i])

    @pl.when(i >= nu_ref[0])
    def _():
        ys_ref[...] = jnp.zeros_like(ys_ref)


def moe(xs, blk_e, n_used, nxt_e, units_per_block, w_gate, w_up, w_down):
    rows = xs.shape[0]
    grid_spec = pltpu.PrefetchScalarGridSpec(
        num_scalar_prefetch=4,
        grid=(rows // MOE_TM,),
        in_specs=[pl.BlockSpec((MOE_TM, D_MODEL), lambda i, be, nu, nx, ub: (jnp.minimum(i, nu[0] - 1), 0)),
                  pl.BlockSpec(memory_space=pl.ANY), pl.BlockSpec(memory_space=pl.ANY), pl.BlockSpec(memory_space=pl.ANY)],
        out_specs=pl.BlockSpec((MOE_TM, D_MODEL), lambda i, be, nu, nx, ub: (i, 0)),
        scratch_shapes=[pltpu.VMEM((2, D_MODEL, EXPERT_FF), BF16), pltpu.VMEM((2, D_MODEL, EXPERT_FF), BF16),
                        pltpu.VMEM((2, EXPERT_FF, D_MODEL), BF16),
                        pltpu.VMEM((2, MOE_UG, EXPERT_FF), F32), pltpu.VMEM((2, MOE_UG, EXPERT_FF), F32),
                        pltpu.VMEM((2, MOE_UD, D_MODEL), F32),
                        pltpu.SemaphoreType.DMA((2,)), pltpu.SMEM((3,), I32)],
    )
    return pl.pallas_call(
        _moe_kernel,
        grid_spec=grid_spec,
        out_shape=jax.ShapeDtypeStruct((rows, D_MODEL), F32),
        compiler_params=_params(("arbitrary",)),
        name="moe",
    )(blk_e, n_used, nxt_e, units_per_block, xs, w_gate, w_up, w_down)


def _combine_kernel(r0_ref, r1_ref, ys_hbm, h_ref, w_ref, o_ref, buf, sem, *, tm, tok0):
    i = pl.program_id(0)

    def issue(block, slot):
        base = tok0 + block * tm

        def body(r, carry):
            for k, idx_ref in enumerate((r0_ref, r1_ref)):
                pltpu.make_async_copy(ys_hbm.at[pl.ds(idx_ref[base + r], 1)], buf.at[slot, k, pl.ds(r, 1)],
                                      sem.at[slot]).start()
            return carry
        lax.fori_loop(0, tm, body, 0, unroll=8)

    @pl.when(i == 0)
    def _():
        issue(0, 0)

    @pl.when(i + 1 < pl.num_programs(0))
    def _():
        issue(i + 1, (i + 1) % 2)

    slot = i % 2
    _wait_rows(ys_hbm, buf.at[slot, 0], sem.at[slot], tm)
    _wait_rows(ys_hbm, buf.at[slot, 1], sem.at[slot], tm)
    w = w_ref[...]
    o_ref[...] = h_ref[...] + (buf[slot, 0] * w[:, 0:1] + buf[slot, 1] * w[:, 1:2])


def combine(ys, h, route_w, rows0, rows1, *, tok0, tm=512):
    n_tok = h.shape[0]
    grid_spec = pltpu.PrefetchScalarGridSpec(
        num_scalar_prefetch=2,
        grid=(n_tok // tm,),
        in_specs=[pl.BlockSpec(memory_space=pl.ANY),
                  pl.BlockSpec((tm, D_MODEL), lambda i, a, b: (i, 0)),
                  pl.BlockSpec((tm, LANES), lambda i, a, b: (i, 0))],
        out_specs=pl.BlockSpec((tm, D_MODEL), lambda i, a, b: (i, 0)),
        scratch_shapes=[pltpu.VMEM((2, 2, tm, D_MODEL), F32), pltpu.SemaphoreType.DMA((2,))],
    )
    return pl.pallas_call(
        functools.partial(_combine_kernel, tm=tm, tok0=tok0),
        grid_spec=grid_spec,
        out_shape=jax.ShapeDtypeStruct((n_tok, D_MODEL), F32),
        compiler_params=_params(("arbitrary",)),
        name="combine",
    )(rows0, rows1, ys, h, route_w)


IN_SIZES = (SSM_WIDTH, ATTN_WIDTH, KV_WIDTH, KV_WIDTH, IDX_HEADS * IDX_DIM, IDX_DIM, IDX_HEADS, D_MODEL, D_MODEL)
IN_COLS = sum(IN_SIZES)
SRC_U, SRC_Q, SRC_K, SRC_V, SRC_QI, SRC_KI, SRC_WI, SRC_GA, SRC_GB = (int(c) for c in np.cumsum((0,) + IN_SIZES[:-1]))


def _regroup_kernel(w_ref, o_ref):
    runs = ((COL_U, SRC_U, SSM_WIDTH + ATTN_WIDTH), (COL_GA, SRC_GA, D_MODEL), (COL_GB, SRC_GB, D_MODEL),
            (COL_K, SRC_K, 2 * KV_WIDTH + IDX_HEADS * IDX_DIM), (COL_KIWI, SRC_KI, IDX_DIM + IDX_HEADS))
    for dst, src, n in runs:
        o_ref[:, dst:dst + n] = w_ref[:, src:src + n].astype(BF16)
    tail = COL_KIWI + IDX_DIM + IDX_HEADS
    o_ref[:, tail:PROJ_COLS] = jnp.zeros((o_ref.shape[0], PROJ_COLS - tail), BF16)


def _regroup_w_in(w_in, *, tr=256):
    return pl.pallas_call(
        _regroup_kernel,
        grid=(D_MODEL // tr,),
        in_specs=[pl.BlockSpec((tr, IN_COLS), lambda i: (i, 0))],
        out_specs=pl.BlockSpec((tr, PROJ_COLS), lambda i: (i, 0)),
        out_shape=jax.ShapeDtypeStruct((D_MODEL, PROJ_COLS), BF16),
        compiler_params=_params(("arbitrary",)),
        name="regroup_w_in",
    )(w_in)


def _layer(x_p, x_s, cache_k, cache_v, cache_ki, h0_re, h0_im, p):
    bp, tp, _ = x_p.shape
    bs, ts, _ = x_s.shape
    past = cache_k.shape[1]
    n_p, n_s = bp * tp, bs * ts
    n_tok = n_p + n_s

    w_in = _regroup_w_in(p['w_in'])
    ssm_w = _ssm_weights(p['ssm_A_re'], p['ssm_A_im'], p['ssm_log_dt'], p['ssm_B_re'], p['ssm_B_im'],
                         p['ssm_C_re'], p['ssm_C_im'])
    glu_w = (p['w_glu_val'].astype(BF16), p['w_glu_gate'].astype(BF16), p['w_attn_branch'].astype(BF16))
    w_out = p['w_out'].astype(BF16)
    router_w = _router_weights(p['w_router_group'], p['b_router_group'], p['w_router_expert'], p['b_router_expert'])
    seq_tiles = tp // QK_TM

    def front(x, table_pos, table_block):
        proj = in_proj(x, p['norm_mix_g'][None, :], w_in)
        return proj, qk_post(proj, table_pos, table_block, p['q_norm_g'], p['k_norm_g'], p['idx_k_norm_g'])

    def seqs(a, b, t):
        return a.reshape(b, t, a.shape[-1])

    xp = x_p.reshape(n_p, D_MODEL)
    proj_p, (q_b, kf_p, k_b, vf_p, v_b, qi_b, kif_p, ki_b, wi) = front(
        xp, jnp.arange(tp, dtype=I32), lambda i: i % seq_tiles)
    g_p, sre_p, sim_p = ssm(proj_p, ssm_w, p['ssm_D'], jnp.zeros((bp, SSM_LB, 2, SSM_SB), F32),
                            n_batch=bp, seq=tp, row0=0)
    bq = 128
    n_buckets = min(16, tp // bq)
    per = tp // bq // n_buckets
    qp, qip, wip = seqs(q_b, bp, tp), seqs(qi_b, bp, tp), seqs(wi, bp, tp)
    kp, vp, kip = seqs(k_b, bp, tp), seqs(v_b, bp, tp), seqs(ki_b, bp, tp)
    attn_p = jnp.concatenate(
        [dsa(qp, qip, wip, kp, vp, kip, bq=bq, q_blk0=n * per, n_qblk=per, n_keys=(n + 1) * per * bq,
             n_sel=min(IDX_TOPK, tp // 4), packed_bisect=False, stack=1)
         for n in range(n_buckets)], axis=1).reshape(n_p, ATTN_WIDTH)
    merged_p = merge(g_p, attn_p, proj_p, *glu_w)
    h_p, hn_p, ri_p, rw_p, cnt_p = out_proj(xp, merged_p, w_out, p['norm_ffn_g'], router_w, jnp.zeros((1, LANES), F32))

    xs_ = x_s.reshape(n_s, D_MODEL)
    proj_s, (q_b, kf_s, k_b, vf_s, v_b, qi_b, kif_s, ki_b, wi) = front(
        xs_, jnp.tile(past + jnp.arange(ts, dtype=I32), QK_TM // ts), lambda i: 0)
    h0 = jnp.stack([h0_re.reshape(bs, SSM_LB, SSM_SB), h0_im.reshape(bs, SSM_LB, SSM_SB)]).transpose(2, 0, 1, 3)
    g_s, sre_s, sim_s = ssm_step(proj_s, ssm_w, p['ssm_D'], h0, n_batch=bs, seq=ts, row0=0)
    attn_s = dsa_step(seqs(q_b, bs, ts), seqs(qi_b, bs, ts), seqs(wi, bs, ts),
                      cache_k, cache_v, cache_ki,
                      seqs(k_b, bs, ts), seqs(v_b, bs, ts), seqs(ki_b, bs, ts),
                      n_sel=min(IDX_TOPK, (past + ts) // 4)).reshape(n_s, ATTN_WIDTH)
    merged_s = merge(g_s, attn_s, proj_s, *glu_w)
    h_s, hn_s, ri_s, rw_s, cnt = out_proj(xs_, merged_s, w_out, p['norm_ffn_g'], router_w, cnt_p)

    counts = cnt[0, N_EXPERT_GROUPS:ROUTER_COLS].astype(I32)
    pad_start, pad_end, n_used = _block_layout(counts)
    dest = dest_rows(jnp.concatenate([ri_p, ri_s], axis=1), pad_start)
    dest0, dest1 = dest[0], dest[1]
    n_blocks = _moe_rows(n_tok) // MOE_TM
    blk = jnp.minimum(jnp.arange(n_blocks, dtype=I32), n_used - 1)
    blk_e = jnp.minimum(jnp.sum((pad_end[None, :] <= (blk * MOE_TM)[:, None]).astype(I32), axis=1), N_EXPERTS - 1)
    after = pad_end[blk_e] // MOE_TM
    nxt_e = jnp.where(after < n_used, blk_e[jnp.minimum(after, n_blocks - 1)], -1).astype(I32)
    blocks_of_e = jnp.maximum((pad_end - pad_start)[blk_e] // MOE_TM, 1)
    units_per_block = ((MOE_UNITS + blocks_of_e - 1) // blocks_of_e).astype(I32)
    n_used = n_used.reshape(1)

    xs = dispatch(hn_p, hn_s, dest0, dest1, pad_end, counts, n_used)
    ys = moe(xs, blk_e, n_used, nxt_e, units_per_block, p['w_exp_gate'], p['w_exp_up'], p['w_exp_down'])
    y_p = combine(ys, h_p, rw_p, dest0, dest1, tok0=0).reshape(bp, tp, D_MODEL)
    y_s = combine(ys, h_s, rw_s, dest0, dest1, tok0=n_p).reshape(bs, ts, D_MODEL)

    def heads(a, b, t):
        return a.reshape(b, t, N_KV_HEADS, HEAD_DIM)

    new_p = (heads(kf_p, bp, tp), heads(vf_p, bp, tp), kif_p.reshape(bp, tp, IDX_DIM), sre_p, sim_p)
    new_s = (heads(kf_s, bs, ts), heads(vf_s, bs, ts), kif_s.reshape(bs, ts, IDX_DIM), sre_s, sim_s)
    return y_p, y_s, new_p, new_s


def kernel(x_prompt, x_sample, cache_k, cache_v, cache_idx_k, state_ssm_re, state_ssm_im, norm_mix_g, w_in, q_norm_g, k_norm_g, idx_k_norm_g, ssm_A_re, ssm_A_im, ssm_log_dt, ssm_B_re, ssm_B_im, ssm_C_re, ssm_C_im, ssm_D, w_glu_val, w_glu_gate, w_attn_branch, w_out, norm_ffn_g, w_router_group, b_router_group, w_router_expert, b_router_expert, w_exp_gate, w_exp_up, w_exp_down):
    depth = w_in.shape[0]
    assert depth == 1, "prompt and sample tokens are batched through one layer"
    names = ('norm_mix_g', 'w_in', 'q_norm_g', 'k_norm_g', 'idx_k_norm_g', 'ssm_A_re', 'ssm_A_im', 'ssm_log_dt',
             'ssm_B_re', 'ssm_B_im', 'ssm_C_re', 'ssm_C_im', 'ssm_D', 'w_glu_val', 'w_glu_gate', 'w_attn_branch',
             'w_out', 'norm_ffn_g', 'w_router_group', 'b_router_group', 'w_router_expert', 'b_router_expert',
             'w_exp_gate', 'w_exp_up', 'w_exp_down')
    vals = (norm_mix_g, w_in, q_norm_g, k_norm_g, idx_k_norm_g, ssm_A_re, ssm_A_im, ssm_log_dt, ssm_B_re, ssm_B_im,
            ssm_C_re, ssm_C_im, ssm_D, w_glu_val, w_glu_gate, w_attn_branch, w_out, norm_ffn_g, w_router_group,
            b_router_group, w_router_expert, b_router_expert, w_exp_gate, w_exp_up, w_exp_down)
    p = {n: v[0] for n, v in zip(names, vals)}
    y_p, y_s, new_p, new_s = _layer(x_prompt, x_sample, cache_k[0], cache_v[0], cache_idx_k[0],
                                    state_ssm_re[0], state_ssm_im[0], p)
    st_p = tuple(a[None] for a in new_p)
    st_s = tuple(a[None] for a in new_s)
    return (y_p, y_s) + st_p + st_s
```

```python
import functools

import numpy as np
import jax
import jax.numpy as jnp
from jax import lax
from jax.experimental import pallas as pl
from jax.experimental.pallas import tpu as pltpu

F32 = jnp.float32
BF16 = jnp.bfloat16
I32 = jnp.int32

D_MODEL = 2048
CHUNK = 64
SSM_WIDTH = 1024
SSM_GROUP = 16
SSM_GROUPS = 64
SSM_STATE = 64
ATTN_WIDTH = 1024
HEAD_DIM = 128
N_HEADS = 8
N_KV_HEADS = 2
KV_GROUP = 4
IDX_HEADS = 8
IDX_DIM = 64
IDX_TOPK = 256
ROPE_THETA = 500000.0
N_EXPERT_GROUPS = 4
EXPERTS_PER_GROUP = 8
N_EXPERTS = 32
TOP_K = 2
EXPERT_FF = 1024
EPS = 1e-6

LANES = 128
SUBLANES = 8
VMEM_LIMIT = 56 * 1024 * 1024

COL_U, COL_Q, COL_GA, COL_GB, COL_K, COL_V, COL_QI, COL_KIWI = 0, 1024, 2048, 4096, 6144, 6400, 6656, 7168
PROJ_COLS = 7296
PROJ_TN = 2432
KV_WIDTH = N_KV_HEADS * HEAD_DIM

SSM_LB = SSM_WIDTH // LANES
SSM_SB = 8 * SSM_STATE

INT_MIN = np.int32(-2 ** 31)
KEY_NEG_INF = np.int32(np.array([0xFF800000], np.uint32).view(np.int32)[0] ^ 0x7FFFFFFF)


def _params(sem, vmem=VMEM_LIMIT):
    return pltpu.CompilerParams(dimension_semantics=sem, vmem_limit_bytes=vmem)


def _dot(a, b):
    return jnp.dot(a, b, preferred_element_type=F32)


def _dot_nt(a, b):
    return lax.dot_general(a, b, (((1,), (1,)), ((), ())), preferred_element_type=F32)


def _split_bf16(x):
    hi = x.astype(BF16)
    lo = (x - hi.astype(F32)).astype(BF16)
    return hi, lo


def _in_proj_kernel(x_ref, g_ref, w_ref, o_ref, xn_ref):
    @pl.when(pl.program_id(1) == 0)
    def _():
        x = x_ref[...]
        ms = jnp.mean(x * x, axis=-1, keepdims=True)
        xn_ref[...] = (x * lax.rsqrt(ms + EPS) * g_ref[...]).astype(BF16)

    o_ref[...] = _dot(xn_ref[...], w_ref[...])


def in_proj(x, gain, w_bf16, *, tm=512):
    n_tok = x.shape[0]
    return pl.pallas_call(
        _in_proj_kernel,
        grid=(n_tok // tm, PROJ_COLS // PROJ_TN),
        in_specs=[pl.BlockSpec((tm, D_MODEL), lambda i, j: (i, 0)),
                  pl.BlockSpec((1, D_MODEL), lambda i, j: (0, 0)),
                  pl.BlockSpec((D_MODEL, PROJ_TN), lambda i, j: (0, j))],
        out_specs=pl.BlockSpec((tm, PROJ_TN), lambda i, j: (i, j)),
        out_shape=jax.ShapeDtypeStruct((n_tok, PROJ_COLS), F32),
        scratch_shapes=[pltpu.VMEM((tm, D_MODEL), BF16)],
        compiler_params=_params(("arbitrary", "arbitrary")),
        name="in_proj",
    )(x, gain, w_bf16)


def _rope(x, c, s_lo, s_hi, half):
    n = x.shape[-1]
    return x * c + pltpu.roll(x, n - half, 1) * s_lo + pltpu.roll(x, half, 1) * s_hi


def _head_norm(x, g):
    ms = jnp.mean(x * x, axis=-1, keepdims=True)
    return x * lax.rsqrt(ms + EPS) * g


V_AUG = 2 * HEAD_DIM


def _store_v_aug(dst_ref, row0, v_heads):
    n = v_heads[0].shape[0]
    one_col = jnp.where(lax.broadcasted_iota(I32, (n, HEAD_DIM), 1) == 0, 1.0, 0.0).astype(BF16)
    for h, v in enumerate(v_heads):
        dst_ref[row0:row0 + n, h * V_AUG:h * V_AUG + HEAD_DIM] = v.astype(BF16)
        dst_ref[row0:row0 + n, h * V_AUG + HEAD_DIM:(h + 1) * V_AUG] = one_col


def _qk_post_kernel(q_ref, k_ref, v_ref, qi_ref, kw_ref, c128_ref, sl128_ref, sh128_ref,
                    c64_ref, sl64_ref, sh64_ref, qg_ref, kg_ref, ig_ref,
                    qo_ref, kf_ref, kb_ref, vf_ref, vb_ref, qio_ref, kif_ref, kib_ref, wo_ref):
    c128, sl128, sh128 = c128_ref[...], sl128_ref[...], sh128_ref[...]
    c64, sl64, sh64 = c64_ref[...], sl64_ref[...], sh64_ref[...]
    half128 = HEAD_DIM // 8
    half64 = IDX_DIM // 8
    for h in range(N_HEADS):
        sl = slice(h * LANES, (h + 1) * LANES)
        qo_ref[:, sl] = _rope(_head_norm(q_ref[:, sl], qg_ref[...]), c128, sl128, sh128, half128).astype(BF16)
    for h in range(N_KV_HEADS):
        sl = slice(h * LANES, (h + 1) * LANES)
        kk = _rope(_head_norm(k_ref[:, sl], kg_ref[...]), c128, sl128, sh128, half128)
        kf_ref[:, sl] = kk
        kb_ref[:, sl] = kk.astype(BF16)
    v = v_ref[...]
    vf_ref[...] = v
    _store_v_aug(vb_ref, 0, [v[:, h * HEAD_DIM:(h + 1) * HEAD_DIM] for h in range(N_KV_HEADS)])
    lane = lax.broadcasted_iota(I32, c64.shape, 1)
    low = lane < IDX_DIM
    for p in range(IDX_HEADS // 2):
        x = _rope(qi_ref[:, p * LANES:(p + 1) * LANES], c64, sl64, sh64, half64)
        qio_ref[:, (2 * p) * LANES:(2 * p + 1) * LANES] = jnp.where(low, x, 0.0).astype(BF16)
        qio_ref[:, (2 * p + 1) * LANES:(2 * p + 2) * LANES] = jnp.where(low, pltpu.roll(x, IDX_DIM, 1), 0.0).astype(BF16)
    kw = kw_ref[...]
    ms = jnp.sum(jnp.where(low, kw * kw, 0.0), axis=-1, keepdims=True) * (1.0 / IDX_DIM)
    ki = _rope(kw * lax.rsqrt(ms + EPS) * ig_ref[...], c64, sl64, sh64, half64)
    kif_ref[...] = ki[:, :IDX_DIM]
    kib_ref[...] = jnp.where(low, ki, 0.0).astype(BF16)
    wo_ref[...] = (pltpu.roll(kw, IDX_DIM, 1) * IDX_HEADS ** -0.5) * IDX_DIM ** -0.5


def _rope_tables(pos, head_dim):
    r = head_dim // 4
    half = r // 2
    inv = ROPE_THETA ** (-jnp.arange(half, dtype=F32) * 2.0 / r)
    ang = pos.astype(F32)[:, None] * inv[None, :]
    cos, sin = jnp.cos(ang), jnp.sin(ang)
    n = pos.shape[0]
    zh = jnp.zeros((n, half), F32)
    rest = head_dim - r
    c = jnp.concatenate([cos, cos, jnp.ones((n, rest), F32)], axis=-1)
    s_lo = jnp.concatenate([-sin, zh, jnp.zeros((n, rest), F32)], axis=-1)
    s_hi = jnp.concatenate([zh, sin, jnp.zeros((n, rest), F32)], axis=-1)
    rep = LANES // head_dim
    return tuple(jnp.tile(t, (1, rep)) for t in (c, s_lo, s_hi))


QK_TM = 512


def qk_post(proj, table_pos, table_block, q_gain, k_gain, ik_gain):
    tm = QK_TM
    n_tok = proj.shape[0]
    t128 = _rope_tables(table_pos, HEAD_DIM)
    t64 = _rope_tables(table_pos, IDX_DIM)
    ik_gain128 = jnp.concatenate([ik_gain, jnp.zeros((LANES - IDX_DIM,), F32)])[None, :]

    def col(width, start):
        return pl.BlockSpec((tm, width), lambda i: (i, start // width))

    def row(width):
        return pl.BlockSpec((tm, width), lambda i: (i, 0))

    table = pl.BlockSpec((tm, LANES), lambda i: (table_block(i), 0))
    gain = pl.BlockSpec((1, LANES), lambda i: (0, 0))
    return pl.pallas_call(
        _qk_post_kernel,
        grid=(n_tok // tm,),
        in_specs=[col(ATTN_WIDTH, COL_Q), col(KV_WIDTH, COL_K), col(KV_WIDTH, COL_V), col(IDX_HEADS * IDX_DIM, COL_QI),
                  col(LANES, COL_KIWI)] + [table] * 6 + [gain] * 3,
        out_specs=[row(ATTN_WIDTH), row(KV_WIDTH), row(KV_WIDTH), row(KV_WIDTH), row(N_KV_HEADS * V_AUG), row(IDX_HEADS * LANES),
                   row(IDX_DIM), row(LANES), row(LANES)],
        out_shape=[jax.ShapeDtypeStruct((n_tok, ATTN_WIDTH), BF16),
                   jax.ShapeDtypeStruct((n_tok, KV_WIDTH), F32), jax.ShapeDtypeStruct((n_tok, KV_WIDTH), BF16),
                   jax.ShapeDtypeStruct((n_tok, KV_WIDTH), F32), jax.ShapeDtypeStruct((n_tok, N_KV_HEADS * V_AUG), BF16),
                   jax.ShapeDtypeStruct((n_tok, IDX_HEADS * LANES), BF16),
                   jax.ShapeDtypeStruct((n_tok, IDX_DIM), F32), jax.ShapeDtypeStruct((n_tok, LANES), BF16),
                   jax.ShapeDtypeStruct((n_tok, LANES), F32)],
        compiler_params=_params(("arbitrary",)),
        name="qk_post",
    )(proj, proj, proj, proj, proj, *t128, *t64, q_gain[None, :], k_gain[None, :], ik_gain128)


def _gelu_tanh(x):
    return 0.5 * x * (1.0 + jnp.tanh(np.float32(np.sqrt(2.0 / np.pi)) * (x + 0.044715 * (x * x * x))))


SSM_LT = SSM_SB // LANES
SSM_SEG = 128


def _ssm_kernel(u_ref, wb_ref, wc_ref, pw_ref, d_ref, h0_ref, g_ref, sre_ref, sim_ref,
                er_ref, ei_ref, car_ref, up_ref, yp_ref):
    c = pl.program_id(2)

    @pl.when(c == 0)
    def _():
        car_ref[...] = h0_ref[...]

    for j in range(SSM_SEG):
        up_ref[j * SUBLANES:(j + 1) * SUBLANES, :] = u_ref[pl.ds(j, SUBLANES, stride=SSM_SEG), :]
    e = _dot(up_ref[...].astype(BF16), wb_ref[...])
    tiles = [slice(lt * LANES, (lt + 1) * LANES) for lt in range(SSM_LT)]
    for lt, sl in enumerate(tiles):
        er_ref[lt] = e[:, sl]
        ei_ref[lt] = e[:, SSM_SB + lt * LANES:SSM_SB + (lt + 1) * LANES]

    def cmul_add(ar, ai, br, bi, cr, ci):
        return ar * br - ai * bi + cr, ar * bi + ai * br + ci

    lb = [(pw_ref[0, 0:1, sl], pw_ref[1, 0:1, sl]) for sl in tiles]
    zero = jnp.zeros((SUBLANES, LANES), F32)
    st = [(zero, zero)] * SSM_LT
    for j in range(SSM_SEG):
        rows = slice(j * SUBLANES, (j + 1) * SUBLANES)
        for lt in range(SSM_LT):
            st[lt] = cmul_add(*lb[lt], *st[lt], er_ref[lt, rows, :], ei_ref[lt, rows, :])
            er_ref[lt, rows, :] = st[lt][0]
            ei_ref[lt, rows, :] = st[lt][1]

    enter = []
    for lt, sl in enumerate(tiles):
        seg_r, seg_i = pw_ref[0, SSM_SEG - 1:SSM_SEG, sl], pw_ref[1, SSM_SEG - 1:SSM_SEG, sl]
        cr, ci = car_ref[0:1, sl], car_ref[1:2, sl]
        rows_r, rows_i = [], []
        for r in range(SUBLANES):
            rows_r.append(cr)
            rows_i.append(ci)
            cr, ci = cmul_add(seg_r, seg_i, cr, ci, st[lt][0][r:r + 1], st[lt][1][r:r + 1])
        car_ref[0:1, sl] = cr
        car_ref[1:2, sl] = ci
        enter.append((jnp.concatenate(rows_r, axis=0), jnp.concatenate(rows_i, axis=0)))

    for j in range(SSM_SEG):
        rows = slice(j * SUBLANES, (j + 1) * SUBLANES)
        for lt, sl in enumerate(tiles):
            xr, xi = cmul_add(pw_ref[0, j:j + 1, sl], pw_ref[1, j:j + 1, sl], *enter[lt],
                              er_ref[lt, rows, :], ei_ref[lt, rows, :])
            er_ref[lt, rows, :] = xr
            ei_ref[lt, rows, :] = xi

    y = None
    for lt, sl in enumerate(tiles):
        t = _dot(er_ref[lt].astype(BF16), wc_ref[0, sl, :]) - _dot(ei_ref[lt].astype(BF16), wc_ref[1, sl, :])
        y = t if y is None else y + t
    yp_ref[...] = y
    out_rows = 2 * SUBLANES
    for t0 in range(0, SUBLANES * SSM_SEG, out_rows):
        r, j0 = divmod(t0, SSM_SEG)
        rows = slice(t0, t0 + out_rows)
        yt = yp_ref[pl.ds(j0 * SUBLANES + r, out_rows, stride=SUBLANES), :] + d_ref[...] * u_ref[rows, :]
        g_ref[rows, :] = _gelu_tanh(yt).astype(BF16)

    @pl.when(c == pl.num_programs(2) - 1)
    def _():
        sre_ref[...] = car_ref[0:1, :]
        sim_ref[...] = car_ref[1:2, :]


def _ssm_weights(a_re, a_im, log_dt, b_re, b_im, c_re, c_im):
    lam_re, lam_im = a_re, a_im
    dt = jnp.exp(log_dt)[:, None]
    mag = jnp.exp(lam_re * dt)
    lb_re, lb_im = mag * jnp.cos(lam_im * dt), mag * jnp.sin(lam_im * dt)
    den = lam_re * lam_re + lam_im * lam_im
    num_re = lb_re - 1.0
    z_re = (num_re * lam_re + lb_im * lam_im) / den
    z_im = (lb_im * lam_re - num_re * lam_im) / den
    zb_re = z_re[:, :, None] * b_re - z_im[:, :, None] * b_im
    zb_im = z_re[:, :, None] * b_im + z_im[:, :, None] * b_re
    eye = jnp.eye(8, dtype=F32)

    def blockdiag_in(w):
        return jnp.einsum('jgph,gk->jghkp', w.reshape(SSM_LB, 8, SSM_STATE, SSM_GROUP), eye).reshape(SSM_LB, LANES, SSM_SB)

    def blockdiag_out(w):
        return jnp.einsum('jghp,gk->jkpgh', w.reshape(SSM_LB, 8, SSM_GROUP, SSM_STATE), eye).reshape(SSM_LB, SSM_SB, LANES)

    wb = jnp.concatenate([blockdiag_in(zb_re), blockdiag_in(zb_im)], axis=-1).astype(BF16)
    wc = jnp.stack([blockdiag_out(c_re), blockdiag_out(c_im)], axis=1).astype(BF16)

    pr, pi_ = lb_re.reshape(SSM_LB, 1, SSM_SB), lb_im.reshape(SSM_LB, 1, SSM_SB)
    while pr.shape[1] < SSM_SEG:
        tr, ti = pr[:, -1:], pi_[:, -1:]
        pr, pi_ = (jnp.concatenate([pr, pr * tr - pi_ * ti], axis=1), jnp.concatenate([pi_, pr * ti + pi_ * tr], axis=1))
    pw = jnp.stack([pr, pi_], axis=1)
    return wb, wc, pw


def ssm(proj, ssm_w, d_skip, h0, *, n_batch, seq, row0):
    wb, wc, pw = ssm_w
    tc = SUBLANES * SSM_SEG
    n_chunks = seq // tc
    blk0 = row0 // tc
    n_tok = n_batch * seq
    state_shape = jax.ShapeDtypeStruct((n_batch, SSM_LB, 1, SSM_SB), F32)
    state_spec = pl.BlockSpec((None, None, 1, SSM_SB), lambda b, j, c: (b, j, 0, 0))
    g, s_re, s_im = pl.pallas_call(
        _ssm_kernel,
        grid=(n_batch, SSM_LB, n_chunks),
        in_specs=[pl.BlockSpec((tc, LANES), lambda b, j, c: (blk0 + b * n_chunks + c, j)),
                  pl.BlockSpec((None, LANES, 2 * SSM_SB), lambda b, j, c: (j, 0, 0)),
                  pl.BlockSpec((None, 2, SSM_SB, LANES), lambda b, j, c: (j, 0, 0, 0)),
                  pl.BlockSpec((None, 2, SSM_SEG, SSM_SB), lambda b, j, c: (j, 0, 0, 0)),
                  pl.BlockSpec((1, LANES), lambda b, j, c: (0, j)),
                  pl.BlockSpec((None, None, 2, SSM_SB), lambda b, j, c: (b, j, 0, 0))],
        out_specs=[pl.BlockSpec((tc, LANES), lambda b, j, c: (b * n_chunks + c, j)), state_spec, state_spec],
        out_shape=[jax.ShapeDtypeStruct((n_tok, SSM_WIDTH), BF16), state_shape, state_shape],
        scratch_shapes=[pltpu.VMEM((SSM_LT, tc, LANES), F32), pltpu.VMEM((SSM_LT, tc, LANES), F32),
                        pltpu.VMEM((2, SSM_SB), F32), pltpu.VMEM((tc, LANES), F32), pltpu.VMEM((tc, LANES), F32)],
        compiler_params=_params(("arbitrary", "arbitrary", "arbitrary")),
        name="ssm",
    )(proj, wb, wc, pw, d_skip[None, :], h0)
    return g, s_re.reshape(n_batch, SSM_GROUPS, SSM_STATE), s_im.reshape(n_batch, SSM_GROUPS, SSM_STATE)


def _ssm_step_kernel(u_ref, wb_ref, wc_ref, pw_ref, d_ref, h0_ref, g_ref, sre_ref, sim_ref, er_ref, ei_ref, *, seq):
    n_seq = h0_ref.shape[1]
    u = u_ref[...]
    e = _dot(u.astype(BF16), wb_ref[...])
    n_lt = SSM_SB // LANES
    y = d_ref[...] * u
    for lt in range(n_lt):
        sl = slice(lt * LANES, (lt + 1) * LANES)
        er_ref[...] = e[:, lt * LANES:(lt + 1) * LANES]
        ei_ref[...] = e[:, SSM_SB + lt * LANES:SSM_SB + (lt + 1) * LANES]
        lr, li = pw_ref[0, 0:1, sl], pw_ref[1, 0:1, sl]
        sr, si = h0_ref[0, :, sl], h0_ref[1, :, sl]
        for t in range(seq):
            rows = pl.ds(t, n_seq, stride=seq)
            sr, si = lr * sr - li * si + er_ref[rows, :], lr * si + li * sr + ei_ref[rows, :]
            er_ref[rows, :] = sr
            ei_ref[rows, :] = si
        y = y + (_dot(er_ref[...].astype(BF16), wc_ref[0, sl, :]) - _dot(ei_ref[...].astype(BF16), wc_ref[1, sl, :]))
        sre_ref[:, sl] = sr
        sim_ref[:, sl] = si
    g_ref[...] = _gelu_tanh(y).astype(BF16)


def ssm_step(proj, ssm_w, d_skip, h0, *, n_batch, seq, row0):
    wb, wc, pw = ssm_w
    n_tok = n_batch * seq
    assert row0 % n_tok == 0
    state_shape = jax.ShapeDtypeStruct((SSM_LB, n_batch, SSM_SB), F32)
    state_spec = pl.BlockSpec((None, n_batch, SSM_SB), lambda j: (j, 0, 0))
    g, s_re, s_im = pl.pallas_call(
        functools.partial(_ssm_step_kernel, seq=seq),
        grid=(SSM_LB,),
        in_specs=[pl.BlockSpec((n_tok, LANES), lambda j: (row0 // n_tok, j)),
                  pl.BlockSpec((None, LANES, 2 * SSM_SB), lambda j: (j, 0, 0)),
                  pl.BlockSpec((None, 2, SSM_SB, LANES), lambda j: (j, 0, 0, 0)),
                  pl.BlockSpec((None, 2, SSM_SEG, SSM_SB), lambda j: (j, 0, 0, 0)),
                  pl.BlockSpec((1, LANES), lambda j: (0, j)),
                  pl.BlockSpec((None, 2, n_batch, SSM_SB), lambda j: (j, 0, 0, 0))],
        out_specs=[pl.BlockSpec((n_tok, LANES), lambda j: (0, j)), state_spec, state_spec],
        out_shape=[jax.ShapeDtypeStruct((n_tok, SSM_WIDTH), BF16), state_shape, state_shape],
        scratch_shapes=[pltpu.VMEM((n_tok, LANES), F32), pltpu.VMEM((n_tok, LANES), F32)],
        compiler_params=_params(("arbitrary",)),
        name="ssm_step",
    )(proj, wb, wc, pw, d_skip[None, :], h0)

    def per_seq(s):
        return s.transpose(1, 0, 2).reshape(n_batch, SSM_GROUPS, SSM_STATE)

    return g, per_seq(s_re), per_seq(s_im)


def _row_sum(x):
    return jnp.sum(x, axis=1, keepdims=True)


def _row_count(mask):
    return _row_sum(jnp.where(mask, 1, 0))


I16 = jnp.int16
I16_MIN = -2 ** 15


def _count16(ref, cand, compare):
    accs = [None] * 4
    for t in range(ref.shape[1] // LANES):
        x = jnp.where(compare(ref[:, t * LANES:(t + 1) * LANES], cand), I16(1), I16(0))
        accs[t % 4] = x if accs[t % 4] is None else accs[t % 4] + x
    accs = [a for a in accs if a is not None]
    total = accs[0]
    for a in accs[1:]:
        total = total + a
    return _row_sum(total.astype(I32))


def _bisect16(ref, target):
    def step(i, base):
        cand = base + lax.shift_left(np.int32(1), np.int32(15) - i)
        cnt = _count16(ref, cand.astype(I16), lambda a, b: a >= b)
        return jnp.where(cnt >= target, cand, base)
    return lax.fori_loop(0, 16, step, jnp.full((ref.shape[0], 1), I16_MIN, I32))


def _bisect32(key_ref, n_sel):
    bq, n_keys = key_ref.shape
    hr = bq // 2

    def lane_counts(h, cand):
        accs = [None] * 4
        for t in range(n_keys // LANES):
            x = jnp.where(key_ref[h * hr:(h + 1) * hr, t * LANES:(t + 1) * LANES] >= cand, 1, 0)
            accs[t % 4] = x if accs[t % 4] is None else accs[t % 4] + x
        accs = [a for a in accs if a is not None]
        total = accs[0]
        for a in accs[1:]:
            total = total + a
        return total

    def decide(part, cand, base):
        return jnp.where(_row_sum(part) >= n_sel, cand, base)

    def bit(i):
        return lax.shift_left(np.int32(1), np.int32(31) - i)

    def body(i, state):
        base_a, base_b, part_b = state
        cand_a = base_a + bit(i)
        part_a = lane_counts(0, cand_a)
        base_b = decide(part_b, base_b + bit(i - 1), base_b)
        part_b = lane_counts(1, base_b + bit(i))
        return decide(part_a, cand_a, base_a), base_b, part_b

    base0 = jnp.full((hr, 1), INT_MIN, I32)
    first = base0 + bit(0)
    state = (decide(lane_counts(0, first), first, base0), base0, lane_counts(1, first))
    base_a, base_b, part_b = lax.fori_loop(1, 32, body, state)
    base_b = decide(part_b, base_b + bit(31), base_b)
    return jnp.concatenate([base_a, base_b], axis=0)


def _stack_heads(ref, heads):
    return jnp.concatenate([ref[:, h * LANES:(h + 1) * LANES] for h in heads], axis=0)


def _dsa_body(q_ref, qi_ref, wi_ref, k_ref, v_ref, ki_ref, o_ref, key_ref, bias_ref, hi_ref, lo_ref, p_ref,
              *, q_pos_first, s_valid, n_sel, packed_bisect, stack):
    bq, n_keys = key_ref.shape
    col = lax.broadcasted_iota(I32, (bq, n_keys), 1)
    qpos = q_pos_first + lax.broadcasted_iota(I32, (bq, 1), 0)
    allowed = col < jnp.minimum((qpos // CHUNK + 1) * CHUNK, s_valid)

    ki = ki_ref[...]
    score = None
    for h0 in range(0, IDX_HEADS, stack):
        d = _dot_nt(_stack_heads(qi_ref, range(h0, h0 + stack)), ki)
        for j in range(stack):
            t = jnp.maximum(d[j * bq:(j + 1) * bq], 0.0) * wi_ref[:, h0 + j:h0 + j + 1]
            score = t if score is None else score + t
    score = jnp.where(score == 0.0, 0.0, score)
    bits = pltpu.bitcast(score, I32)
    key = jnp.where(bits < 0, bits ^ np.int32(0x7FFFFFFF), bits)
    key = jnp.where(allowed, key, KEY_NEG_INF)
    key_ref[...] = key

    if packed_bisect:
        hi_ref[...] = (key >> 16).astype(I16)
        lo_ref[...] = ((key & 0xFFFF) + I16_MIN).astype(I16)
        thr_hi = _bisect16(hi_ref, n_sel)
        thr_hi16 = thr_hi.astype(I16)
        need_lo = n_sel - _count16(hi_ref, thr_hi16, lambda a, b: a > b)
        lo_ref[...] = jnp.where(hi_ref[...] == thr_hi16, lo_ref[...], I16(I16_MIN))
        thr_lo = _bisect16(lo_ref, need_lo)
        thr = lax.shift_left(thr_hi, np.int32(16)) + (thr_lo - I16_MIN)
    else:
        thr = _bisect32(key_ref, n_sel)
    thr = jnp.maximum(thr, KEY_NEG_INF)

    key = key_ref[...]
    need = n_sel - _row_count(key > thr)
    n_eq = _row_count(key == thr)
    n_bits = int(n_keys - 1).bit_length()

    def tie_cut():
        def step(i, j0):
            cand = j0 + lax.shift_left(np.int32(1), np.int32(n_bits - 1) - i)
            cnt = _row_sum(jnp.where(key_ref[...] == thr, jnp.where(col < cand, 1, 0), 0))
            return jnp.where(cnt < need, cand, j0)
        return lax.fori_loop(0, n_bits, step, jnp.zeros((bq, 1), I32))

    split = jnp.max(jnp.where(n_eq > need, 1, 0)) > 0
    j_last = lax.cond(split, tie_cut, lambda: jnp.full((bq, 1), n_keys, I32))
    tie_bias = jnp.where(thr == KEY_NEG_INF, -jnp.inf, 0.0)
    bias_ref[...] = jnp.where(key > thr, 0.0,
                              jnp.where(key == thr, jnp.where(col <= j_last, tie_bias, -jnp.inf), -jnp.inf))

    c = np.float32(HEAD_DIM ** -0.5 * np.log2(np.e))
    for h0 in range(0, N_HEADS, stack):
        kv = h0 // KV_GROUP
        heads = range(h0, h0 + stack)
        s_all = _dot_nt(_stack_heads(q_ref, heads), k_ref[:, kv * HEAD_DIM:(kv + 1) * HEAD_DIM])
        for g in range(stack):
            s = s_all[g * bq:(g + 1) * bq] + bias_ref[...]
            m = jnp.max(s, axis=1, keepdims=True)
            p_ref[g * bq:(g + 1) * bq, :] = jnp.exp2((s - m) * c).astype(BF16)
        pv = _dot(p_ref[0:stack * bq, :], v_ref[:, kv * V_AUG:(kv + 1) * V_AUG])
        for g, h in enumerate(heads):
            o = pv[g * bq:(g + 1) * bq]
            o_ref[:, h * HEAD_DIM:(h + 1) * HEAD_DIM] = (o[:, :HEAD_DIM] / o[:, HEAD_DIM:HEAD_DIM + 1]).astype(BF16)


def _dsa_scratch(bq, n_keys):
    return [pltpu.VMEM((bq, n_keys), I32), pltpu.VMEM((bq, n_keys), F32),
            pltpu.VMEM((bq, n_keys), I16), pltpu.VMEM((bq, n_keys), I16), pltpu.VMEM((KV_GROUP * bq, n_keys), BF16)]


def _dsa_kernel(q_ref, qi_ref, wi_ref, k_ref, v_ref, ki_ref, o_ref, *scratch, q_pos0, **static):
    bq = scratch[0].shape[0]
    _dsa_body(q_ref, qi_ref, wi_ref, k_ref, v_ref, ki_ref, o_ref, *scratch,
              q_pos_first=q_pos0 + pl.program_id(1) * bq, **static)


def dsa(q, qi, wi, k, v, ki, *, bq, q_blk0, n_qblk, n_keys, n_sel, packed_bisect, stack):
    n_batch, seq = q.shape[:2]

    def qspec(width):
        return pl.BlockSpec((None, bq, width), lambda b, i: (b, q_blk0 + i, 0))

    def kspec(width):
        return pl.BlockSpec((None, n_keys, width), lambda b, i: (b, 0, 0))

    return pl.pallas_call(
        functools.partial(_dsa_kernel, q_pos0=q_blk0 * bq, s_valid=seq, n_sel=n_sel, packed_bisect=packed_bisect,
                          stack=stack),
        grid=(n_batch, n_qblk),
        in_specs=[qspec(ATTN_WIDTH), qspec(IDX_HEADS * LANES), qspec(LANES), kspec(KV_WIDTH), kspec(N_KV_HEADS * V_AUG),
                  kspec(LANES)],
        out_specs=pl.BlockSpec((None, bq, ATTN_WIDTH), lambda b, i: (b, i, 0)),
        out_shape=jax.ShapeDtypeStruct((n_batch, n_qblk * bq, ATTN_WIDTH), BF16),
        scratch_shapes=_dsa_scratch(bq, n_keys),
        compiler_params=_params(("arbitrary", "arbitrary")),
        name="dsa",
    )(q, qi, wi, k, v, ki)


def _dsa_step_kernel(q_ref, qi_ref, wi_ref, ck_hbm, cv_hbm, cki_ref, nk_ref, nv_ref, nki_ref, o_ref,
                     k_buf, v_buf, ki_buf, cache_buf, sem, *scratch, past, n_sel):
    b = pl.program_id(0)

    def cache_copies(seq, slot):
        return [pltpu.make_async_copy(src.at[seq, :, h, :], cache_buf.at[slot, a, h], sem.at[slot])
                for a, src in enumerate((ck_hbm, cv_hbm)) for h in range(N_KV_HEADS)]

    @pl.when(b == 0)
    def _():
        for cp in cache_copies(0, 0):
            cp.start()

    @pl.when(b + 1 < pl.num_programs(0))
    def _():
        for cp in cache_copies(b + 1, (b + 1) % 2):
            cp.start()

    slot = b % 2
    for cp in cache_copies(b, slot):
        cp.wait()

    ts = nk_ref.shape[0]
    n_keys = k_buf.shape[0]
    for h in range(N_KV_HEADS):
        k_buf[0:past, h * HEAD_DIM:(h + 1) * HEAD_DIM] = cache_buf[slot, 0, h].astype(BF16)
    _store_v_aug(v_buf, 0, [cache_buf[slot, 1, h] for h in range(N_KV_HEADS)])
    for buf, new in ((k_buf, nk_ref), (v_buf, nv_ref)):
        buf[past:past + ts, :] = new[...]
        buf[past + ts:n_keys, :] = jnp.zeros((n_keys - past - ts, buf.shape[1]), BF16)
    ki_buf[0:past, 0:IDX_DIM] = cki_ref[...].astype(BF16)
    ki_buf[0:past, IDX_DIM:LANES] = jnp.zeros((past, LANES - IDX_DIM), BF16)
    ki_buf[past:past + ts, :] = nki_ref[...]
    ki_buf[past + ts:n_keys, :] = jnp.zeros((n_keys - past - ts, LANES), BF16)
    _dsa_body(q_ref, qi_ref, wi_ref, k_buf, v_buf, ki_buf, o_ref, *scratch,
              q_pos_first=past, s_valid=past + ts, n_sel=n_sel, packed_bisect=True, stack=KV_GROUP)


def dsa_step(q, qi, wi, cache_k, cache_v, cache_ki, k_new, v_new, ki_new, *, n_sel):
    n_batch, ts = q.shape[:2]
    past = cache_k.shape[1]
    n_keys = -(-(past + ts) // LANES) * LANES

    def spec(rows, width):
        return pl.BlockSpec((None, rows, width), lambda b: (b, 0, 0))

    return pl.pallas_call(
        functools.partial(_dsa_step_kernel, past=past, n_sel=n_sel),
        grid=(n_batch,),
        in_specs=[spec(ts, ATTN_WIDTH), spec(ts, IDX_HEADS * LANES), spec(ts, LANES),
                  pl.BlockSpec(memory_space=pl.ANY), pl.BlockSpec(memory_space=pl.ANY), spec(past, IDX_DIM),
                  spec(ts, KV_WIDTH), spec(ts, N_KV_HEADS * V_AUG), spec(ts, LANES)],
        out_specs=spec(ts, ATTN_WIDTH),
        out_shape=jax.ShapeDtypeStruct((n_batch, ts, ATTN_WIDTH), BF16),
        scratch_shapes=[pltpu.VMEM((n_keys, KV_WIDTH), BF16), pltpu.VMEM((n_keys, N_KV_HEADS * V_AUG), BF16),
                        pltpu.VMEM((n_keys, LANES), BF16),
                        pltpu.VMEM((2, 2, N_KV_HEADS, past, HEAD_DIM), F32), pltpu.SemaphoreType.DMA((2,)),
                        *_dsa_scratch(ts, n_keys)],
        compiler_params=_params(("arbitrary",)),
        name="dsa_step",
    )(q, qi, wi, cache_k, cache_v, cache_ki, k_new, v_new, ki_new)


def _merge_kernel(g_ref, a_ref, ga_ref, gb_ref, wv_ref, wg_ref, wb_ref, o_ref):
    g = g_ref[...]
    branch_a = _dot(g, wv_ref[...]) * jax.nn.sigmoid(_dot(g, wg_ref[...]))
    branch_b = _dot(a_ref[...], wb_ref[...])
    merged = jax.nn.sigmoid(ga_ref[...]) * branch_a + jax.nn.sigmoid(gb_ref[...]) * branch_b
    o_ref[...] = merged.astype(BF16)


def merge(g, attn, proj, w_val, w_gate, w_branch, *, tm=1024, tn=512):
    n_tok = g.shape[0]
    nj = D_MODEL // tn

    def wspec():
        return pl.BlockSpec((SSM_WIDTH, tn), lambda i, j: (0, j))

    return pl.pallas_call(
        _merge_kernel,
        grid=(n_tok // tm, nj),
        in_specs=[pl.BlockSpec((tm, SSM_WIDTH), lambda i, j: (i, 0)),
                  pl.BlockSpec((tm, ATTN_WIDTH), lambda i, j: (i, 0)),
                  pl.BlockSpec((tm, tn), lambda i, j: (i, COL_GA // tn + j)),
                  pl.BlockSpec((tm, tn), lambda i, j: (i, COL_GB // tn + j)),
                  wspec(), wspec(), wspec()],
        out_specs=pl.BlockSpec((tm, tn), lambda i, j: (i, j)),
        out_shape=jax.ShapeDtypeStruct((n_tok, D_MODEL), BF16),
        compiler_params=_params(("arbitrary", "arbitrary")),
        name="merge",
    )(g, attn, proj, proj, w_val, w_gate, w_branch)


ROUTER_COLS = N_EXPERT_GROUPS + N_EXPERTS
MOE_TM = 256


def _first_lane_of_max(x, lane_f):
    m = jnp.max(x, axis=1, keepdims=True)
    return m, jnp.min(jnp.where(x == m, lane_f, float(LANES)), axis=1, keepdims=True)


def _out_proj_kernel(x_ref, m_ref, wo_ref, gn_ref, wr_ref, br_ref, cin_ref,
                     h_ref, hn_ref, ri_ref, rw_ref, cnt_ref, carry_ref):
    @pl.when(pl.program_id(0) == 0)
    def _():
        carry_ref[...] = cin_ref[...]

    h = x_ref[...] + _dot(m_ref[...], wo_ref[...])
    h_ref[...] = h
    ms = jnp.mean(h * h, axis=-1, keepdims=True)
    hn = h * lax.rsqrt(ms + EPS) * gn_ref[...]
    hn_ref[...] = hn
    hh, hl = _split_bf16(hn)
    both = _dot(hh, wr_ref[...])
    lg = both[:, :LANES] + _dot(hl, wr_ref[:, 0:LANES]) + both[:, LANES:] + br_ref[...]

    tm = lg.shape[0]
    lane = lax.broadcasted_iota(I32, lg.shape, 1)
    lane_f = lane.astype(F32)
    ninf = -jnp.inf
    gl = jnp.where(lane < N_EXPERT_GROUPS, lg, ninf)
    gmax, gsel = _first_lane_of_max(gl, lane_f)
    g_w = 1.0 / jnp.sum(jnp.exp(gl - gmax), axis=1, keepdims=True)
    lo = N_EXPERT_GROUPS + EXPERTS_PER_GROUP * gsel
    el = jnp.where(lane_f >= lo, jnp.where(lane_f < lo + EXPERTS_PER_GROUP, lg, ninf), ninf)
    v1, i1 = _first_lane_of_max(el, lane_f)
    el2 = jnp.where(lane_f == i1, ninf, el)
    v2, i2 = _first_lane_of_max(el2, lane_f)
    t = jnp.exp(v2 - v1)
    s1 = 1.0 / (1.0 + t)
    w1 = s1 * g_w
    w2 = (t * s1) * g_w

    m1 = jnp.where(lane_f == i1, 1.0, 0.0)
    m2 = jnp.where(lane_f == i2, 1.0, 0.0)
    both = m1 + m2
    tri = jnp.where(lax.broadcasted_iota(I32, (tm, tm), 0) > lax.broadcasted_iota(I32, (tm, tm), 1), 1.0, 0.0)
    before = _dot(tri.astype(BF16), both.astype(BF16)) + carry_ref[...]
    r1 = jnp.sum(before * m1, axis=1, keepdims=True)
    r2 = jnp.sum(before * m2, axis=1, keepdims=True)
    carry_ref[...] = carry_ref[...] + jnp.sum(both, axis=0, keepdims=True)
    cnt_ref[...] = carry_ref[...]
    e1 = i1 - float(N_EXPERT_GROUPS)
    e2 = i2 - float(N_EXPERT_GROUPS)
    fields = jnp.where(lane == 0, e1, jnp.where(lane == 1, e2, jnp.where(lane == 2, r1, jnp.where(lane == 3, r2, 0.0))))
    ri_ref[...] = fields.T[0:SUBLANES, :].astype(I32)
    rw_ref[...] = jnp.where(lane == 0, w1, jnp.where(lane == 1, w2, 0.0))


def _router_weights(w_router_group, b_router_group, w_router_expert, b_router_expert):
    wr = jnp.concatenate([w_router_group, w_router_expert, jnp.zeros((D_MODEL, LANES - ROUTER_COLS), F32)], axis=1)
    wr_hi = wr.astype(BF16)
    wr_lo = (wr - wr_hi.astype(F32)).astype(BF16)
    br = jnp.concatenate([b_router_group, b_router_expert, jnp.zeros((LANES - ROUTER_COLS,), F32)])[None, :]
    return jnp.concatenate([wr_hi, wr_lo], axis=1), br


def out_proj(x, merged, w_out, ffn_gain, router_w, counts_in, *, tm=512):
    n_tok = x.shape[0]
    wr, br = router_w

    def row(width):
        return pl.BlockSpec((tm, width), lambda i: (i, 0))

    def const(shape):
        return pl.BlockSpec(shape, lambda i: (0, 0), pipeline_mode=pl.Buffered(1))

    return pl.pallas_call(
        _out_proj_kernel,
        grid=(n_tok // tm,),
        in_specs=[row(D_MODEL), row(D_MODEL), const((D_MODEL, D_MODEL)), const((1, D_MODEL)),
                  const((D_MODEL, 2 * LANES)), const((1, LANES)), const((1, LANES))],
        out_specs=[row(D_MODEL), row(D_MODEL), pl.BlockSpec((SUBLANES, tm), lambda i: (0, i)), row(LANES),
                   pl.BlockSpec((1, LANES), lambda i: (0, 0))],
        out_shape=[jax.ShapeDtypeStruct((n_tok, D_MODEL), F32), jax.ShapeDtypeStruct((n_tok, D_MODEL), F32),
                   jax.ShapeDtypeStruct((SUBLANES, n_tok), I32), jax.ShapeDtypeStruct((n_tok, LANES), F32),
                   jax.ShapeDtypeStruct((1, LANES), F32)],
        scratch_shapes=[pltpu.VMEM((1, LANES), F32)],
        compiler_params=_params(("arbitrary",)),
        name="out_proj",
    )(x, merged, w_out, ffn_gain[None, :], wr, br, counts_in)


def _block_layout(counts):
    padded = (counts + MOE_TM - 1) // MOE_TM * MOE_TM
    pad_end = jnp.cumsum(padded).astype(I32)
    pad_start = pad_end - padded
    n_used = pad_end[-1] // MOE_TM
    return pad_start, pad_end, n_used


def _dest_kernel(ps_ref, ri_ref, o_ref):
    ri = ri_ref[...]
    start = jnp.zeros_like(ri)
    for k in range(N_EXPERTS):
        start = jnp.where(ri == k, ps_ref[k], start)
    o_ref[...] = start + pltpu.roll(ri, SUBLANES - TOP_K, 0)


def dest_rows(route_i, pad_start):
    grid_spec = pltpu.PrefetchScalarGridSpec(
        num_scalar_prefetch=1, grid=(1,),
        in_specs=[pl.BlockSpec(route_i.shape, lambda i, ps: (0, 0))],
        out_specs=pl.BlockSpec(route_i.shape, lambda i, ps: (0, 0)))
    return pl.pallas_call(_dest_kernel, grid_spec=grid_spec, out_shape=jax.ShapeDtypeStruct(route_i.shape, I32),
                          compiler_params=_params(("arbitrary",)), name="dest_rows")(pad_start, route_i)


def _moe_rows(n_tok):
    return -(-(n_tok * TOP_K + N_EXPERTS * (MOE_TM - 1)) // MOE_TM) * MOE_TM


DISPATCH_TM = 1024


def _wait_rows(src_hbm, dst, sem, n_rows):
    pltpu.make_async_copy(src_hbm.at[pl.ds(0, n_rows)], dst, sem).wait()


def _dispatch_kernel(d0_ref, d1_ref, pe_ref, cnt_ref, nu_ref, hna_ref, hnb_ref, xs_hbm, zbuf, sem, semz,
                     *, n_blocks, a_tiles):
    i = pl.program_id(0)

    def zero_block(row0):
        return pltpu.make_async_copy(zbuf, xs_hbm.at[pl.ds(pl.multiple_of(row0, MOE_TM), MOE_TM)], semz)

    @pl.when(i == 0)
    def _():
        zbuf[...] = jnp.zeros_like(zbuf)
        for start in (True, False):
            for e in range(N_EXPERTS):
                @pl.when(cnt_ref[e] > 0)
                def _():
                    cp = zero_block(pe_ref[e] - MOE_TM)
                    cp.start() if start else cp.wait()

            def tail(b, c):
                cp = zero_block(b * MOE_TM)
                cp.start() if start else cp.wait()
                return c
            lax.fori_loop(nu_ref[0], n_blocks, tail, 0)

    base = i * DISPATCH_TM

    def scatter(hn_ref):
        def body(r, c):
            src = hn_ref.at[pl.ds(r, 1)]
            pltpu.make_async_copy(src, xs_hbm.at[pl.ds(d0_ref[base + r], 1)], sem).start(priority=0)
            pltpu.make_async_copy(src, xs_hbm.at[pl.ds(d1_ref[base + r], 1)], sem).start(priority=1)
            return c
        lax.fori_loop(0, DISPATCH_TM, body, 0, unroll=8)
        for _ in range(TOP_K):
            pltpu.make_async_copy(hn_ref, xs_hbm.at[pl.ds(0, DISPATCH_TM)], sem).wait()

    @pl.when(i < a_tiles)
    def _():
        scatter(hna_ref)

    @pl.when(i >= a_tiles)
    def _():
        scatter(hnb_ref)


def dispatch(hn_a, hn_b, dest0, dest1, pad_end, counts, n_used):
    a_tiles, b_tiles = hn_a.shape[0] // DISPATCH_TM, hn_b.shape[0] // DISPATCH_TM
    rows = _moe_rows(hn_a.shape[0] + hn_b.shape[0])
    grid_spec = pltpu.PrefetchScalarGridSpec(
        num_scalar_prefetch=5,
        grid=(a_tiles + b_tiles,),
        in_specs=[pl.BlockSpec((DISPATCH_TM, D_MODEL), lambda i, *_: (jnp.minimum(i, a_tiles - 1), 0)),
                  pl.BlockSpec((DISPATCH_TM, D_MODEL), lambda i, *_: (jnp.maximum(i - a_tiles, 0), 0))],
        out_specs=pl.BlockSpec(memory_space=pl.ANY),
        scratch_shapes=[pltpu.VMEM((MOE_TM, D_MODEL), F32), pltpu.SemaphoreType.DMA(()), pltpu.SemaphoreType.DMA(())],
    )
    return pl.pallas_call(
        functools.partial(_dispatch_kernel, n_blocks=rows // MOE_TM, a_tiles=a_tiles),
        grid_spec=grid_spec,
        out_shape=jax.ShapeDtypeStruct((rows, D_MODEL), F32),
        compiler_params=_params(("arbitrary",)),
        name="dispatch",
    )(dest0, dest1, pad_end, counts, n_used, hn_a, hn_b)


MOE_UNITS = 8
MOE_UG = D_MODEL // MOE_UNITS
MOE_UD = EXPERT_FF // MOE_UNITS


def _moe_kernel(blk_e_ref, nu_ref, nxt_ref, upb_ref, xs_ref, wg_hbm, wu_hbm, wd_hbm, ys_ref,
                wg_bf, wu_bf, wd_bf, stg_g, stg_u, stg_d, sem, st_ref):
    i = pl.program_id(0)
    cur_slot, pos, cur_e = 0, 1, 2

    def unit_copies(e, unit, s):
        g_rows = pl.ds(pl.multiple_of(unit * MOE_UG, MOE_UG), MOE_UG)
        d_rows = pl.ds(pl.multiple_of(unit * MOE_UD, MOE_UD), MOE_UD)
        return (pltpu.make_async_copy(wg_hbm.at[e, g_rows, :], stg_g.at[s], sem.at[s]),
                pltpu.make_async_copy(wu_hbm.at[e, g_rows, :], stg_u.at[s], sem.at[s]),
                pltpu.make_async_copy(wd_hbm.at[e, d_rows, :], stg_d.at[s], sem.at[s]))

    def start_unit(e, unit):
        for cp in unit_copies(e, unit, unit % 2):
            cp.start()

    def begin_load(e):
        st_ref[pos] = 0
        start_unit(e, 0)
        start_unit(e, 1)

    def advance(e, slot, n):
        def body(_, c):
            unit = st_ref[pos]

            @pl.when(unit < MOE_UNITS)
            def _():
                s = unit % 2
                for cp in unit_copies(e, unit, s):
                    cp.wait()
                g_rows = pl.ds(pl.multiple_of(unit * MOE_UG, MOE_UG), MOE_UG)
                d_rows = pl.ds(pl.multiple_of(unit * MOE_UD, MOE_UD), MOE_UD)
                wg_bf[slot, g_rows, :] = stg_g[s].astype(BF16)
                wu_bf[slot, g_rows, :] = stg_u[s].astype(BF16)
                wd_bf[slot, d_rows, :] = stg_d[s].astype(BF16)

                @pl.when(unit + 2 < MOE_UNITS)
                def _():
                    start_unit(e, unit + 2)
                st_ref[pos] = unit + 1
            return c
        lax.fori_loop(0, n, body, 0)

    def load_next(nxt):
        @pl.when(nxt >= 0)
        def _():
            begin_load(nxt)

        @pl.when(nxt < 0)
        def _():
            st_ref[pos] = MOE_UNITS

    @pl.when(i < nu_ref[0])
    def _():
        e = blk_e_ref[i]
        nxt = nxt_ref[i]

        @pl.when(i == 0)
        def _():
            st_ref[cur_slot] = 0
            st_ref[cur_e] = e
            begin_load(e)
            advance(e, 0, MOE_UNITS)
            load_next(nxt)

        @pl.when(jnp.logical_and(i > 0, e != st_ref[cur_e]))
        def _():
            slot = 1 - st_ref[cur_slot]
            advance(e, slot, MOE_UNITS)
            st_ref[cur_slot] = slot
            st_ref[cur_e] = e
            load_next(nxt)

        slot = st_ref[cur_slot]
        x = xs_ref[...].astype(BF16)
        hg = _dot(x, wg_bf[slot])
        hu = _dot(x, wu_bf[slot])
        hmid = (jax.nn.silu(hg) * hu).astype(BF16)
        ys_ref[...] = _dot(hmid, wd_bf[slot])

        @pl.when(nxt >= 0)
        def _():
            advance(nxt, 1 - slot, upb_ref[i])

    @pl.when(i >= nu_ref[0])
    def _():
        ys_ref[...] = jnp.zeros_like(ys_ref)


def moe(xs, blk_e, n_used, nxt_e, units_per_block, w_gate, w_up, w_down):
    rows = xs.shape[0]
    grid_spec = pltpu.PrefetchScalarGridSpec(
        num_scalar_prefetch=4,
        grid=(rows // MOE_TM,),
        in_specs=[pl.BlockSpec((MOE_TM, D_MODEL), lambda i, be, nu, nx, ub: (jnp.minimum(i, nu[0] - 1), 0)),
                  pl.BlockSpec(memory_space=pl.ANY), pl.BlockSpec(memory_space=pl.ANY), pl.BlockSpec(memory_space=pl.ANY)],
        out_specs=pl.BlockSpec((MOE_TM, D_MODEL), lambda i, be, nu, nx, ub: (i, 0)),
        scratch_shapes=[pltpu.VMEM((2, D_MODEL, EXPERT_FF), BF16), pltpu.VMEM((2, D_MODEL, EXPERT_FF), BF16),
                        pltpu.VMEM((2, EXPERT_FF, D_MODEL), BF16),
                        pltpu.VMEM((2, MOE_UG, EXPERT_FF), F32), pltpu.VMEM((2, MOE_UG, EXPERT_FF), F32),
                        pltpu.VMEM((2, MOE_UD, D_MODEL), F32),
                        pltpu.SemaphoreType.DMA((2,)), pltpu.SMEM((3,), I32)],
    )
    return pl.pallas_call(
        _moe_kernel,
        grid_spec=grid_spec,
        out_shape=jax.ShapeDtypeStruct((rows, D_MODEL), F32),
        compiler_params=_params(("arbitrary",)),
        name="moe",
    )(blk_e, n_used, nxt_e, units_per_block, xs, w_gate, w_up, w_down)


def _combine_kernel(r0_ref, r1_ref, ys_hbm, h_ref, w_ref, o_ref, buf, sem, *, tm, tok0):
    i = pl.program_id(0)

    def issue(block, slot):
        base = tok0 + block * tm

        def body(r, carry):
            for k, idx_ref in enumerate((r0_ref, r1_ref)):
                pltpu.make_async_copy(ys_hbm.at[pl.ds(idx_ref[base + r], 1)], buf.at[slot, k, pl.ds(r, 1)],
                                      sem.at[slot]).start(priority=k)
            return carry
        lax.fori_loop(0, tm, body, 0, unroll=8)

    @pl.when(i == 0)
    def _():
        issue(0, 0)

    @pl.when(i + 1 < pl.num_programs(0))
    def _():
        issue(i + 1, (i + 1) % 2)

    slot = i % 2
    _wait_rows(ys_hbm, buf.at[slot, 0], sem.at[slot], tm)
    _wait_rows(ys_hbm, buf.at[slot, 1], sem.at[slot], tm)
    w = w_ref[...]
    o_ref[...] = h_ref[...] + (buf[slot, 0] * w[:, 0:1] + buf[slot, 1] * w[:, 1:2])


def combine(ys, h, route_w, rows0, rows1, *, tok0, tm=256):
    n_tok = h.shape[0]
    grid_spec = pltpu.PrefetchScalarGridSpec(
        num_scalar_prefetch=2,
        grid=(n_tok // tm,),
        in_specs=[pl.BlockSpec(memory_space=pl.ANY),
                  pl.BlockSpec((tm, D_MODEL), lambda i, a, b: (i, 0)),
                  pl.BlockSpec((tm, LANES), lambda i, a, b: (i, 0))],
        out_specs=pl.BlockSpec((tm, D_MODEL), lambda i, a, b: (i, 0)),
        scratch_shapes=[pltpu.VMEM((2, 2, tm, D_MODEL), F32), pltpu.SemaphoreType.DMA((2,))],
    )
    return pl.pallas_call(
        functools.partial(_combine_kernel, tm=tm, tok0=tok0),
        grid_spec=grid_spec,
        out_shape=jax.ShapeDtypeStruct((n_tok, D_MODEL), F32),
        compiler_params=_params(("arbitrary",)),
        name="combine",
    )(rows0, rows1, ys, h, route_w)


IN_SIZES = (SSM_WIDTH, ATTN_WIDTH, KV_WIDTH, KV_WIDTH, IDX_HEADS * IDX_DIM, IDX_DIM, IDX_HEADS, D_MODEL, D_MODEL)
IN_COLS = sum(IN_SIZES)
SRC_U, SRC_Q, SRC_K, SRC_V, SRC_QI, SRC_KI, SRC_WI, SRC_GA, SRC_GB = (int(c) for c in np.cumsum((0,) + IN_SIZES[:-1]))


def _regroup_kernel(w_ref, o_ref):
    runs = ((COL_U, SRC_U, SSM_WIDTH + ATTN_WIDTH), (COL_GA, SRC_GA, D_MODEL), (COL_GB, SRC_GB, D_MODEL),
            (COL_K, SRC_K, 2 * KV_WIDTH + IDX_HEADS * IDX_DIM), (COL_KIWI, SRC_KI, IDX_DIM + IDX_HEADS))
    for dst, src, n in runs:
        o_ref[:, dst:dst + n] = w_ref[:, src:src + n].astype(BF16)
    tail = COL_KIWI + IDX_DIM + IDX_HEADS
    o_ref[:, tail:PROJ_COLS] = jnp.zeros((o_ref.shape[0], PROJ_COLS - tail), BF16)


def _regroup_w_in(w_in, *, tr=256):
    return pl.pallas_call(
        _regroup_kernel,
        grid=(D_MODEL // tr,),
        in_specs=[pl.BlockSpec((tr, IN_COLS), lambda i: (i, 0))],
        out_specs=pl.BlockSpec((tr, PROJ_COLS), lambda i: (i, 0)),
        out_shape=jax.ShapeDtypeStruct((D_MODEL, PROJ_COLS), BF16),
        compiler_params=_params(("arbitrary",)),
        name="regroup_w_in",
    )(w_in)


def _layer(x_p, x_s, cache_k, cache_v, cache_ki, h0_re, h0_im, p):
    bp, tp, _ = x_p.shape
    bs, ts, _ = x_s.shape
    past = cache_k.shape[1]
    n_p, n_s = bp * tp, bs * ts
    n_tok = n_p + n_s

    w_in = _regroup_w_in(p['w_in'])
    ssm_w = _ssm_weights(p['ssm_A_re'], p['ssm_A_im'], p['ssm_log_dt'], p['ssm_B_re'], p['ssm_B_im'],
                         p['ssm_C_re'], p['ssm_C_im'])
    glu_w = (p['w_glu_val'].astype(BF16), p['w_glu_gate'].astype(BF16), p['w_attn_branch'].astype(BF16))
    w_out = p['w_out'].astype(BF16)
    router_w = _router_weights(p['w_router_group'], p['b_router_group'], p['w_router_expert'], p['b_router_expert'])
    seq_tiles = tp // QK_TM

    def front(x, table_pos, table_block):
        proj = in_proj(x, p['norm_mix_g'][None, :], w_in)
        return proj, qk_post(proj, table_pos, table_block, p['q_norm_g'], p['k_norm_g'], p['idx_k_norm_g'])

    def seqs(a, b, t):
        return a.reshape(b, t, a.shape[-1])

    xp = x_p.reshape(n_p, D_MODEL)
    proj_p, (q_b, kf_p, k_b, vf_p, v_b, qi_b, kif_p, ki_b, wi) = front(
        xp, jnp.arange(tp, dtype=I32), lambda i: i % seq_tiles)
    g_p, sre_p, sim_p = ssm(proj_p, ssm_w, p['ssm_D'], jnp.zeros((bp, SSM_LB, 2, SSM_SB), F32),
                            n_batch=bp, seq=tp, row0=0)
    bq = 128
    n_buckets = min(16, tp // bq)
    per = tp // bq // n_buckets
    qp, qip, wip = seqs(q_b, bp, tp), seqs(qi_b, bp, tp), seqs(wi, bp, tp)
    kp, vp, kip = seqs(k_b, bp, tp), seqs(v_b, bp, tp), seqs(ki_b, bp, tp)
    attn_p = jnp.concatenate(
        [dsa(qp, qip, wip, kp, vp, kip, bq=bq, q_blk0=n * per, n_qblk=per, n_keys=(n + 1) * per * bq,
             n_sel=min(IDX_TOPK, tp // 4), packed_bisect=False, stack=1)
         for n in range(n_buckets)], axis=1).reshape(n_p, ATTN_WIDTH)
    merged_p = merge(g_p, attn_p, proj_p, *glu_w)
    h_p, hn_p, ri_p, rw_p, cnt_p = out_proj(xp, merged_p, w_out, p['norm_ffn_g'], router_w, jnp.zeros((1, LANES), F32))

    xs_ = x_s.reshape(n_s, D_MODEL)
    proj_s, (q_b, kf_s, k_b, vf_s, v_b, qi_b, kif_s, ki_b, wi) = front(
        xs_, jnp.tile(past + jnp.arange(ts, dtype=I32), QK_TM // ts), lambda i: 0)
    h0 = jnp.stack([h0_re.reshape(bs, SSM_LB, SSM_SB), h0_im.reshape(bs, SSM_LB, SSM_SB)]).transpose(2, 0, 1, 3)
    g_s, sre_s, sim_s = ssm_step(proj_s, ssm_w, p['ssm_D'], h0, n_batch=bs, seq=ts, row0=0)
    attn_s = dsa_step(seqs(q_b, bs, ts), seqs(qi_b, bs, ts), seqs(wi, bs, ts),
                      cache_k, cache_v, cache_ki,
                      seqs(k_b, bs, ts), seqs(v_b, bs, ts), seqs(ki_b, bs, ts),
                      n_sel=min(IDX_TOPK, (past + ts) // 4)).reshape(n_s, ATTN_WIDTH)
    merged_s = merge(g_s, attn_s, proj_s, *glu_w)
    h_s, hn_s, ri_s, rw_s, cnt = out_proj(xs_, merged_s, w_out, p['norm_ffn_g'], router_w, cnt_p)

    counts = cnt[0, N_EXPERT_GROUPS:ROUTER_COLS].astype(I32)
    pad_start, pad_end, n_used = _block_layout(counts)
    dest = dest_rows(jnp.concatenate([ri_p, ri_s], axis=1), pad_start)
    dest0, dest1 = dest[0], dest[1]
    n_blocks = _moe_rows(n_tok) // MOE_TM
    blk = jnp.minimum(jnp.arange(n_blocks, dtype=I32), n_used - 1)
    blk_e = jnp.minimum(jnp.sum((pad_end[None, :] <= (blk * MOE_TM)[:, None]).astype(I32), axis=1), N_EXPERTS - 1)
    after = pad_end[blk_e] // MOE_TM
    nxt_e = jnp.where(after < n_used, blk_e[jnp.minimum(after, n_blocks - 1)], -1).astype(I32)
    blocks_of_e = jnp.maximum((pad_end - pad_start)[blk_e] // MOE_TM, 1)
    units_per_block = ((MOE_UNITS + blocks_of_e - 1) // blocks_of_e).astype(I32)
    n_used = n_used.reshape(1)

    xs = dispatch(hn_p, hn_s, dest0, dest1, pad_end, counts, n_used)
    ys = moe(xs, blk_e, n_used, nxt_e, units_per_block, p['w_exp_gate'], p['w_exp_up'], p['w_exp_down'])
    y_p = combine(ys, h_p, rw_p, dest0, dest1, tok0=0).reshape(bp, tp, D_MODEL)
    y_s = combine(ys, h_s, rw_s, dest0, dest1, tok0=n_p).reshape(bs, ts, D_MODEL)

    def heads(a, b, t):
        return a.reshape(b, t, N_KV_HEADS, HEAD_DIM)

    new_p = (heads(kf_p, bp, tp), heads(vf_p, bp, tp), kif_p.reshape(bp, tp, IDX_DIM), sre_p, sim_p)
    new_s = (heads(kf_s, bs, ts), heads(vf_s, bs, ts), kif_s.reshape(bs, ts, IDX_DIM), sre_s, sim_s)
    return y_p, y_s, new_p, new_s


def kernel(x_prompt, x_sample, cache_k, cache_v, cache_idx_k, state_ssm_re, state_ssm_im, norm_mix_g, w_in, q_norm_g, k_norm_g, idx_k_norm_g, ssm_A_re, ssm_A_im, ssm_log_dt, ssm_B_re, ssm_B_im, ssm_C_re, ssm_C_im, ssm_D, w_glu_val, w_glu_gate, w_attn_branch, w_out, norm_ffn_g, w_router_group, b_router_group, w_router_expert, b_router_expert, w_exp_gate, w_exp_up, w_exp_down):
    depth = w_in.shape[0]
    assert depth == 1, "prompt and sample tokens are batched through one layer"
    names = ('norm_mix_g', 'w_in', 'q_norm_g', 'k_norm_g', 'idx_k_norm_g', 'ssm_A_re', 'ssm_A_im', 'ssm_log_dt',
             'ssm_B_re', 'ssm_B_im', 'ssm_C_re', 'ssm_C_im', 'ssm_D', 'w_glu_val', 'w_glu_gate', 'w_attn_branch',
             'w_out', 'norm_ffn_g', 'w_router_group', 'b_router_group', 'w_router_expert', 'b_router_expert',
             'w_exp_gate', 'w_exp_up', 'w_exp_down')
    vals = (norm_mix_g, w_in, q_norm_g, k_norm_g, idx_k_norm_g, ssm_A_re, ssm_A_im, ssm_log_dt, ssm_B_re, ssm_B_im,
            ssm_C_re, ssm_C_im, ssm_D, w_glu_val, w_glu_gate, w_attn_branch, w_out, norm_ffn_g, w_router_group,
            b_router_group, w_router_expert, b_router_expert, w_exp_gate, w_exp_up, w_exp_down)
    p = {n: v[0] for n, v in zip(names, vals)}
    y_p, y_s, new_p, new_s = _layer(x_prompt, x_sample, cache_k[0], cache_v[0], cache_idx_k[0],
                                    state_ssm_re[0], state_ssm_im[0], p)
    st_p = tuple(a[None] for a in new_p)
    st_s = tuple(a[None] for a in new_s)
    return (y_p, y_s) + st_p + st_s
```
